```python
import numpy as np
import jax
import jax.numpy as jnp
from jax import lax

D_MODEL = 2048
BATCH = 2
SEQ = 4096
DEPTH = 1

HEAD_DIM = 128
N_MIX_HEADS = D_MODEL // HEAD_DIM
FOX_HEADS = N_MIX_HEADS // 2
NSA_HEADS = N_MIX_HEADS - FOX_HEADS
NSA_KV_HEADS = 2
NSA_GROUP = NSA_HEADS // NSA_KV_HEADS
N_BRANCH = 3
D_FF = 4 * D_MODEL
ROPE_THETA = 10000.0
Q_BLOCK = 128
CMP_LEN = 32
CMP_STRIDE = 16
CMP_HIDDEN = 2 * HEAD_DIM
SEL_BLOCK = 64
SEL_TOPK = 16
WINDOW = 512
NORM_EPS = 1e-6
ATTN_SCALE = HEAD_DIM ** -0.5
FOX_W = FOX_HEADS * HEAD_DIM
NSA_W = NSA_HEADS * HEAD_DIM
KV_W = NSA_KV_HEADS * HEAD_DIM
IN_SIZES = (FOX_W, FOX_W, FOX_W, FOX_HEADS, NSA_W, KV_W, KV_W, KV_W, KV_W, KV_W, KV_W, N_BRANCH * NSA_HEADS)
D_IN = 3 * FOX_W + FOX_HEADS + NSA_W + 6 * KV_W + N_BRANCH * NSA_HEADS

kernel_name = 'hybrid_fox_nsa_block'


def _rms_norm(x, gain):
    xf = x.astype(jnp.float32)
    y = xf * lax.rsqrt(jnp.mean(xf * xf, axis=-1, keepdims=True) + NORM_EPS)
    return (y * gain.astype(jnp.float32)).astype(x.dtype)


def _modulate(h, shift, scale):
    return h * (1.0 + scale[:, None, :]) + shift[:, None, :]


def _heads(a, n):
    return a.reshape(a.shape[0], a.shape[1], n, HEAD_DIM)


def _rope(x, pos):
    half = HEAD_DIM // 2
    inv_freq = ROPE_THETA ** (-jnp.arange(half, dtype=jnp.float32) / half)
    ang = pos.astype(jnp.float32)[:, None] * inv_freq[None, :]
    cos = jnp.cos(ang)[None, :, None, :]
    sin = jnp.sin(ang)[None, :, None, :]
    xf = x.astype(jnp.float32)
    x1, x2 = xf[..., :half], xf[..., half:]
    return jnp.concatenate([x1 * cos - x2 * sin, x2 * cos + x1 * sin], axis=-1).astype(x.dtype)


def _masked_softmax(s, mask):
    s = jnp.where(mask, s, -jnp.inf)
    m = jnp.max(s, axis=-1, keepdims=True)
    m = jnp.where(jnp.isfinite(m), m, 0.0)
    e = jnp.exp(s - m)
    return e / jnp.maximum(jnp.sum(e, axis=-1, keepdims=True), 1e-30)


def _fox_attention(q, k, v, log_f):
    B, T, H, D = q.shape
    nb = T // Q_BLOCK
    cum = jnp.cumsum(log_f, axis=1)
    q_blocks = q.reshape(B, nb, Q_BLOCK, H, D).transpose(1, 0, 3, 2, 4)
    c_blocks = cum.reshape(B, nb, Q_BLOCK, H).transpose(1, 0, 3, 2)
    k_t = k.transpose(0, 2, 1, 3)
    v_t = v.transpose(0, 2, 1, 3)
    c_k = cum.transpose(0, 2, 1)
    k_pos = jnp.arange(T)

    def block(args):
        q_i, c_i, i = args
        q_pos = i * Q_BLOCK + jnp.arange(Q_BLOCK)
        s = jnp.einsum('bhqd,bhkd->bhqk', q_i, k_t).astype(jnp.float32) * ATTN_SCALE
        s = s + (c_i[..., :, None] - c_k[:, :, None, :])
        p = _masked_softmax(s, k_pos[None, :] <= q_pos[:, None])
        return jnp.einsum('bhqk,bhkd->bhqd', p.astype(v_t.dtype), v_t)

    o = lax.map(block, (q_blocks, c_blocks, jnp.arange(nb)))
    return o.transpose(1, 0, 3, 2, 4).reshape(B, T, H * D)


def _compress(x, pe, w1, w2):
    B, T, G, D = x.shape
    n_cmp = (T - CMP_LEN) // CMP_STRIDE + 1
    idx = np.arange(n_cmp)[:, None] * CMP_STRIDE + np.arange(CMP_LEN)[None, :]
    xb = x[:, idx] + pe[None, None, :, None, :]
    xb = xb.transpose(0, 1, 3, 2, 4).reshape(B, n_cmp, G, CMP_LEN * D)
    return jax.nn.silu(xb @ w1) @ w2


def _nsa_attention(q, k_cmp, v_cmp, k_slc, v_slc, k_win, v_win, gates):
    B, T, H, D = q.shape
    G, hg = NSA_KV_HEADS, NSA_GROUP
    nb = T // Q_BLOCK
    t_pos = jnp.arange(T)
    n_cmp = k_cmp.shape[1]

    q_g = q.reshape(B, T, G, hg, D)
    s = jnp.einsum('btghd,bngd->bghtn', q_g, k_cmp).astype(jnp.float32) * ATTN_SCALE
    cmp_end = jnp.arange(n_cmp) * CMP_STRIDE + CMP_LEN - 1
    p_cmp = _masked_softmax(s, cmp_end[None, :] <= t_pos[:, None])
    o_cmp = jnp.einsum('bghtn,bngd->btghd', p_cmp.astype(v_cmp.dtype), v_cmp)

    n_sel = T // SEL_BLOCK
    c0 = np.arange(n_cmp) * CMP_STRIDE
    s0 = np.arange(n_sel) * SEL_BLOCK
    overlap = np.clip(np.minimum(c0[:, None] + CMP_LEN, s0[None, :] + SEL_BLOCK)
                      - np.maximum(c0[:, None], s0[None, :]), 0, None).astype(np.float32) / CMP_LEN
    imp = jnp.einsum('bghtn,nj->bgtj', p_cmp, jnp.asarray(overlap))
    q_blk = t_pos // SEL_BLOCK
    blk = jnp.arange(n_sel)
    causal = blk[None, :] <= q_blk[:, None]
    forced = (blk[None, :] == 0) | (blk[None, :] == q_blk[:, None]) | (blk[None, :] == q_blk[:, None] - 1)
    imp = jnp.where(forced, jnp.inf, jnp.where(causal, imp, -jnp.inf))
    n_top = min(SEL_TOPK, n_sel)
    _, sel_idx = lax.top_k(imp, n_top)

    q_r = _rope(q, t_pos).reshape(B, T, G, hg, D)
    k_s = k_slc.reshape(B, n_sel, SEL_BLOCK, G, D).transpose(0, 3, 1, 2, 4)
    v_s = v_slc.reshape(B, n_sel, SEL_BLOCK, G, D).transpose(0, 3, 1, 2, 4)
    q_blocks = q_r.reshape(B, nb, Q_BLOCK, G, hg, D).transpose(1, 0, 2, 3, 4, 5)
    i_blocks = sel_idx.reshape(B, G, nb, Q_BLOCK, n_top).transpose(2, 0, 1, 3, 4)
    b_ix = jnp.arange(B)[:, None, None, None]
    g_ix = jnp.arange(G)[None, :, None, None]
    m_keys = n_top * SEL_BLOCK

    def sel_block(args):
        q_i, idx_i, i = args
        q_pos = i * Q_BLOCK + jnp.arange(Q_BLOCK)
        k_g = k_s[b_ix, g_ix, idx_i].reshape(B, G, Q_BLOCK, m_keys, D)
        v_g = v_s[b_ix, g_ix, idx_i].reshape(B, G, Q_BLOCK, m_keys, D)
        k_pos = (idx_i[..., None] * SEL_BLOCK + jnp.arange(SEL_BLOCK)).reshape(B, G, Q_BLOCK, m_keys)
        mask = (k_pos <= q_pos[None, None, :, None])[:, :, None]
        s_i = jnp.einsum('bqghd,bgqmd->bghqm', q_i, k_g).astype(jnp.float32) * ATTN_SCALE
        p = _masked_softmax(s_i, mask)
        return jnp.einsum('bghqm,bgqmd->bqghd', p.astype(v_g.dtype), v_g)

    o_slc = lax.map(sel_block, (q_blocks, i_blocks, jnp.arange(nb)))
    o_slc = o_slc.transpose(1, 0, 2, 3, 4, 5).reshape(B, T, G, hg, D)

    k_wp = jnp.pad(k_win, ((0, 0), (WINDOW, 0), (0, 0), (0, 0)))
    v_wp = jnp.pad(v_win, ((0, 0), (WINDOW, 0), (0, 0), (0, 0)))

    def win_block(args):
        q_i, i = args
        start = i * Q_BLOCK
        k_b = lax.dynamic_slice_in_dim(k_wp, start, Q_BLOCK + WINDOW, axis=1)
        v_b = lax.dynamic_slice_in_dim(v_wp, start, Q_BLOCK + WINDOW, axis=1)
        q_pos = start + jnp.arange(Q_BLOCK)
        k_pos = start - WINDOW + jnp.arange(Q_BLOCK + WINDOW)
        diff = q_pos[:, None] - k_pos[None, :]
        mask = (k_pos[None, :] >= 0) & (diff >= 0) & (diff < WINDOW)
        s_i = jnp.einsum('bqghd,bkgd->bghqk', q_i, k_b).astype(jnp.float32) * ATTN_SCALE
        p = _masked_softmax(s_i, mask)
        return jnp.einsum('bghqk,bkgd->bqghd', p.astype(v_b.dtype), v_b)

    o_win = lax.map(win_block, (q_blocks, jnp.arange(nb)))
    o_win = o_win.transpose(1, 0, 2, 3, 4, 5).reshape(B, T, G, hg, D)

    g = gates.reshape(B, T, G, hg, N_BRANCH)
    o = g[..., 0:1] * o_cmp + g[..., 1:2] * o_slc + g[..., 2:3] * o_win
    return o.reshape(B, T, H * D)


def setup_inputs(seed: int = 0) -> dict:
    key = jax.random.key(seed)
    ks = jax.random.split(key, 24)
    L = DEPTH
    f32 = jnp.float32

    def nrm(k, shape, scale):
        return jax.random.normal(k, shape, f32) * scale

    return {
        'x': nrm(ks[0], (BATCH, SEQ, D_MODEL), 1.0),
        'c': nrm(ks[1], (BATCH, D_MODEL), 1.0),
        'w_ada': nrm(ks[2], (L, D_MODEL, 6 * D_MODEL), D_MODEL ** -0.5),
        'b_ada': nrm(ks[3], (L, 6 * D_MODEL), 0.02),
        'norm1_g': 1.0 + nrm(ks[4], (L, D_MODEL), 0.05),
        'w_in': nrm(ks[5], (L, D_MODEL, D_IN), D_MODEL ** -0.5),
        'b_forget': jax.random.uniform(ks[6], (L, FOX_HEADS), f32, 1.0, 4.0),
        'fox_q_norm': 1.0 + nrm(ks[7], (L, HEAD_DIM), 0.05),
        'fox_k_norm': 1.0 + nrm(ks[8], (L, HEAD_DIM), 0.05),
        'nsa_q_norm': 1.0 + nrm(ks[9], (L, HEAD_DIM), 0.05),
        'cmp_k_norm': 1.0 + nrm(ks[10], (L, HEAD_DIM), 0.05),
        'slc_k_norm': 1.0 + nrm(ks[11], (L, HEAD_DIM), 0.05),
        'win_k_norm': 1.0 + nrm(ks[12], (L, HEAD_DIM), 0.05),
        'cmp_pe_k': nrm(ks[13], (L, CMP_LEN, HEAD_DIM), 0.2),
        'cmp_w1_k': nrm(ks[14], (L, CMP_LEN * HEAD_DIM, CMP_HIDDEN), (CMP_LEN * HEAD_DIM) ** -0.5),
        'cmp_w2_k': nrm(ks[15], (L, CMP_HIDDEN, HEAD_DIM), CMP_HIDDEN ** -0.5),
        'cmp_pe_v': nrm(ks[16], (L, CMP_LEN, HEAD_DIM), 0.2),
        'cmp_w1_v': nrm(ks[17], (L, CMP_LEN * HEAD_DIM, CMP_HIDDEN), (CMP_LEN * HEAD_DIM) ** -0.5),
        'cmp_w2_v': nrm(ks[18], (L, CMP_HIDDEN, HEAD_DIM), CMP_HIDDEN ** -0.5),
        'w_out': nrm(ks[19], (L, D_MODEL, D_MODEL), D_MODEL ** -0.5),
        'norm2_g': 1.0 + nrm(ks[20], (L, D_MODEL), 0.05),
        'w_up': nrm(ks[21], (L, D_MODEL, D_FF), D_MODEL ** -0.5),
        'w_down': nrm(ks[22], (L, D_FF, D_MODEL), D_FF ** -0.5),
    }


def reference(x, c, w_ada, b_ada, norm1_g, w_in, b_forget, fox_q_norm, fox_k_norm, nsa_q_norm,
              cmp_k_norm, slc_k_norm, win_k_norm, cmp_pe_k, cmp_w1_k, cmp_w2_k, cmp_pe_v, cmp_w1_v,
              cmp_w2_v, w_out, norm2_g, w_up, w_down):
    B, T, _ = x.shape
    t_pos = jnp.arange(T)
    split_at = np.cumsum(IN_SIZES)[:-1].tolist()
    c_act = jax.nn.silu(c)
    for l in range(DEPTH):
        mod = c_act @ w_ada[l] + b_ada[l]
        shift1, scale1, gate1, shift2, scale2, gate2 = jnp.split(mod, 6, axis=-1)

        h = _modulate(_rms_norm(x, norm1_g[l]), shift1, scale1)
        proj = h @ w_in[l]
        fq, fk, fv, fz, nq, kc, vc, ksl, vsl, kw, vw, gz = jnp.split(proj, split_at, axis=-1)

        fq = _rms_norm(_heads(fq, FOX_HEADS), fox_q_norm[l])
        fk = _rms_norm(_heads(fk, FOX_HEADS), fox_k_norm[l])
        log_f = jax.nn.log_sigmoid(fz.astype(jnp.float32) + b_forget[l].astype(jnp.float32))
        o_fox = _fox_attention(fq, fk, _heads(fv, FOX_HEADS), log_f)

        nq = _rms_norm(_heads(nq, NSA_HEADS), nsa_q_norm[l])
        k_cmp = _rms_norm(_compress(_heads(kc, NSA_KV_HEADS), cmp_pe_k[l], cmp_w1_k[l], cmp_w2_k[l]), cmp_k_norm[l])
        v_cmp = _compress(_heads(vc, NSA_KV_HEADS), cmp_pe_v[l], cmp_w1_v[l], cmp_w2_v[l])
        k_slc = _rope(_rms_norm(_heads(ksl, NSA_KV_HEADS), slc_k_norm[l]), t_pos)
        k_win = _rope(_rms_norm(_heads(kw, NSA_KV_HEADS), win_k_norm[l]), t_pos)
        gates = jax.nn.sigmoid(gz).reshape(B, T, NSA_HEADS, N_BRANCH)
        o_nsa = _nsa_attention(nq, k_cmp, v_cmp, k_slc, _heads(vsl, NSA_KV_HEADS),
                               k_win, _heads(vw, NSA_KV_HEADS), gates)

        mix = jnp.concatenate([o_fox, o_nsa], axis=-1) @ w_out[l]
        x = x + gate1[:, None, :] * mix

        h = _modulate(_rms_norm(x, norm2_g[l]), shift2, scale2)
        x = x + gate2[:, None, :] * (jnp.square(jax.nn.relu(h @ w_up[l])) @ w_down[l])
    return x
```

```python
import functools

import numpy as np
import jax
import jax.numpy as jnp
from jax import lax
from jax.experimental import pallas as pl
from jax.experimental.pallas import tpu as pltpu

D_MODEL = 2048
HEAD_DIM = 128
FOX_HEADS = 8
NSA_HEADS = 8
NSA_KV_HEADS = 2
NSA_GROUP = NSA_HEADS // NSA_KV_HEADS
N_BRANCH = 3
D_FF = 4 * D_MODEL
ROPE_THETA = 10000.0
CMP_LEN = 32
CMP_STRIDE = 16
CMP_HIDDEN = 2 * HEAD_DIM
SEL_BLOCK = 64
SEL_SHIFT = 6
SEL_TOPK = 16
WINDOW = 512
NORM_EPS = 1e-6
ATTN_SCALE = HEAD_DIM ** -0.5
FOX_W = FOX_HEADS * HEAD_DIM
NSA_W = NSA_HEADS * HEAD_DIM
KV_W = NSA_KV_HEADS * HEAD_DIM

LANES = 128
SEL_LANES = LANES
MASK_BIG = 1e30

COL_FQ = 0
COL_FK = COL_FQ + FOX_W
COL_FV = COL_FK + FOX_W
COL_NQ = COL_FV + FOX_W
COL_KC = COL_NQ + NSA_W
COL_VC = COL_KC + KV_W
COL_KS = COL_VC + KV_W
COL_VS = COL_KS + KV_W
COL_KW = COL_VS + KV_W
COL_VW = COL_KW + KV_W
COL_SMALL = COL_VW + KV_W
D_IN_PAD = COL_SMALL + LANES

VMEM_LIMIT = 56 * 1024 * 1024

f32 = jnp.float32
bf16 = jnp.bfloat16


def _params(*sem):
    return pltpu.CompilerParams(dimension_semantics=sem, vmem_limit_bytes=VMEM_LIMIT)


def _dot_nt(a, b):
    return lax.dot_general(a, b, (((1,), (1,)), ((), ())), preferred_element_type=f32)


def _dot(a, b):
    return jnp.dot(a, b, preferred_element_type=f32)


def _split3(x):
    hi = x.astype(bf16)
    r1 = x - hi.astype(f32)
    mid = r1.astype(bf16)
    lo = (r1 - mid.astype(f32)).astype(bf16)
    return hi, mid, lo


def _dot_f32_by_exact(x, w_bf16):
    hi, mid, lo = _split3(x)
    return _dot(hi, w_bf16) + (_dot(mid, w_bf16) + _dot(lo, w_bf16))


def _rms(x, gain):
    ms = jnp.mean(x * x, axis=-1, keepdims=True)
    return x * lax.rsqrt(ms + NORM_EPS) * gain


def _ada_kernel(ct_ref, w_ref, b_ref, o_ref, *, n_batch, k_chunk):
    ct = ct_ref[...]
    act = ct * jax.nn.sigmoid(ct)
    rows = []
    for b in range(n_batch):
        col = act[:, b:b + 1]
        acc = b_ref[...]
        for k0 in range(0, D_MODEL, k_chunk):
            acc = acc + jnp.sum(w_ref[k0:k0 + k_chunk, :] * col[k0:k0 + k_chunk], axis=0, keepdims=True)
        rows.append(acc)
    o_ref[...] = jnp.concatenate(rows, axis=0)


def _ada(c, w_ada, b_ada):
    n_batch = c.shape[0]
    n_out = w_ada.shape[1]
    tn = 1024
    return pl.pallas_call(
        functools.partial(_ada_kernel, n_batch=n_batch, k_chunk=256),
        out_shape=jax.ShapeDtypeStruct((n_batch, n_out), f32),
        grid=(n_out // tn,),
        in_specs=[
            pl.BlockSpec((D_MODEL, n_batch), lambda j: (0, 0)),
            pl.BlockSpec((D_MODEL, tn), lambda j: (0, j)),
            pl.BlockSpec((1, tn), lambda j: (0, j)),
        ],
        out_specs=pl.BlockSpec((n_batch, tn), lambda j: (0, j)),
        compiler_params=_params("arbitrary"),
        name="ada",
    )(c.T, w_ada, b_ada.reshape(1, n_out))


def _proj_kernel(x_ref, mod_ref, g_ref, w_ref, o_ref, h_ref):
    @pl.when(pl.program_id(1) == 0)
    def _():
        md = mod_ref[0]
        y = _rms(x_ref[...], g_ref[...])
        h_ref[...] = (y * (1.0 + md[1:2]) + md[0:1]).astype(bf16)

    o_ref[...] = _dot(h_ref[...], w_ref[...])


def _proj(x2d, mod3, norm_g, w_bf16, seq_len):
    m_rows = x2d.shape[0]
    n_cols = w_bf16.shape[1]
    tm, tn = 512, 1152
    assert seq_len % tm == 0 and n_cols % tn == 0
    per_b = seq_len // tm
    return pl.pallas_call(
        _proj_kernel,
        out_shape=jax.ShapeDtypeStruct((m_rows, n_cols), f32),
        grid=(m_rows // tm, n_cols // tn),
        in_specs=[
            pl.BlockSpec((tm, D_MODEL), lambda i, j: (i, 0)),
            pl.BlockSpec((1, 6, D_MODEL), lambda i, j: (i // per_b, 0, 0)),
            pl.BlockSpec((1, D_MODEL), lambda i, j: (0, 0)),
            pl.BlockSpec((D_MODEL, tn), lambda i, j: (0, j)),
        ],
        out_specs=pl.BlockSpec((tm, tn), lambda i, j: (i, j)),
        scratch_shapes=[pltpu.VMEM((tm, D_MODEL), bf16)],
        compiler_params=_params("arbitrary", "arbitrary"),
        name="proj",
    )(x2d, mod3, norm_g, w_bf16)


def _prep_kernel(p_ref, gq_ref, gk_ref, gn_ref, gs_ref, gw_ref, bf_ref, inv_ref,
                 fq_ref, fk_ref, fv_ref, cum_ref, nqu_ref, nqr_ref, kc_ref, vc_ref,
                 ks_ref, vs_ref, kw_ref, vw_ref, gate_ref, carry_ref, *, tm):
    i = pl.program_id(1)

    @pl.when(i == 0)
    def _():
        carry_ref[...] = jnp.zeros_like(carry_ref)

    def head(col, h):
        return p_ref[:, col + h * HEAD_DIM: col + (h + 1) * HEAD_DIM]

    row = lax.broadcasted_iota(jnp.int32, (tm, LANES), 0)
    lane = lax.broadcasted_iota(jnp.int32, (tm, LANES), 1)
    pos = i * tm + row
    ang = pos.astype(f32) * inv_ref[...]
    cos = jnp.cos(ang)
    sin = jnp.sin(ang)
    sin_signed = jnp.where(lane < HEAD_DIM // 2, -sin, sin)

    def rope(x):
        return x * cos + pltpu.roll(x, HEAD_DIM // 2, 1) * sin_signed

    for h in range(FOX_HEADS):
        fq_ref[0, h] = (_rms(head(COL_FQ, h), gq_ref[...]) * ATTN_SCALE).astype(bf16)
        fk_ref[0, h] = _rms(head(COL_FK, h), gk_ref[...]).astype(bf16)
        fv_ref[0, h] = head(COL_FV, h).astype(bf16)

    small = p_ref[:, COL_SMALL:COL_SMALL + LANES]
    z = small.T[0:FOX_HEADS, :] + bf_ref[...]
    logf = jnp.minimum(z, 0.0) - jnp.log1p(jnp.exp(-jnp.abs(z)))
    s_idx = lax.broadcasted_iota(jnp.int32, (tm, tm), 0)
    t_idx = lax.broadcasted_iota(jnp.int32, (tm, tm), 1)
    tri = jnp.where(s_idx <= t_idx, 1.0, 0.0).astype(bf16)
    cum = _dot_f32_by_exact(logf, tri) + carry_ref[:, 0:1]
    cum_ref[0] = cum
    carry_ref[...] = jnp.broadcast_to(cum[:, tm - 1:tm], carry_ref.shape)

    gate_ref[0] = jax.nn.sigmoid(small)

    for h in range(NSA_HEADS):
        qn = _rms(head(COL_NQ, h), gn_ref[...]) * ATTN_SCALE
        nqu_ref[0, h] = qn.astype(bf16)
        nqr_ref[0, h] = rope(qn).astype(bf16)
    onehot = jnp.where(lane == lax.shift_right_logical(pos, SEL_SHIFT), 1.0, 0.0).astype(bf16)
    for g in range(NSA_KV_HEADS):
        kc_ref[0, g] = head(COL_KC, g)
        vc_ref[0, g] = head(COL_VC, g)
        ks = rope(_rms(head(COL_KS, g), gs_ref[...])).astype(bf16)
        ks_ref[0, g] = jnp.concatenate([ks, onehot], axis=1)
        vs_ref[0, g] = head(COL_VS, g).astype(bf16)
        kw_ref[0, g] = rope(_rms(head(COL_KW, g), gw_ref[...])).astype(bf16)
        vw_ref[0, g] = head(COL_VW, g).astype(bf16)


def _prep(proj, n_batch, seq_len, gq, gk, gn, gs, gw, b_forget, inv_freq):
    tm = 256
    per_b = seq_len // tm
    hshape = lambda n, w, dt: jax.ShapeDtypeStruct((n_batch, n, seq_len, w), dt)
    hspec = lambda n, w: pl.BlockSpec((1, n, tm, w), lambda b, i: (b, 0, i, 0))
    vec = pl.BlockSpec((1, LANES), lambda b, i: (0, 0))
    return pl.pallas_call(
        functools.partial(_prep_kernel, tm=tm),
        out_shape=[
            hshape(FOX_HEADS, HEAD_DIM, bf16), hshape(FOX_HEADS, HEAD_DIM, bf16), hshape(FOX_HEADS, HEAD_DIM, bf16),
            jax.ShapeDtypeStruct((n_batch, FOX_HEADS, seq_len), f32),
            hshape(NSA_HEADS, HEAD_DIM, bf16), hshape(NSA_HEADS, HEAD_DIM, bf16),
            hshape(NSA_KV_HEADS, HEAD_DIM, f32), hshape(NSA_KV_HEADS, HEAD_DIM, f32),
            hshape(NSA_KV_HEADS, HEAD_DIM + SEL_LANES, bf16), hshape(NSA_KV_HEADS, HEAD_DIM, bf16),
            hshape(NSA_KV_HEADS, HEAD_DIM, bf16), hshape(NSA_KV_HEADS, HEAD_DIM, bf16),
            jax.ShapeDtypeStruct((n_batch, seq_len, LANES), f32),
        ],
        grid=(n_batch, per_b),
        in_specs=[
            pl.BlockSpec((tm, D_IN_PAD), lambda b, i: (b * per_b + i, 0)),
            vec, vec, vec, vec, vec,
            pl.BlockSpec((FOX_HEADS, 1), lambda b, i: (0, 0)),
            vec,
        ],
        out_specs=[
            hspec(FOX_HEADS, HEAD_DIM), hspec(FOX_HEADS, HEAD_DIM), hspec(FOX_HEADS, HEAD_DIM),
            pl.BlockSpec((1, FOX_HEADS, tm), lambda b, i: (b, 0, i)),
            hspec(NSA_HEADS, HEAD_DIM), hspec(NSA_HEADS, HEAD_DIM),
            hspec(NSA_KV_HEADS, HEAD_DIM), hspec(NSA_KV_HEADS, HEAD_DIM),
            hspec(NSA_KV_HEADS, HEAD_DIM + SEL_LANES), hspec(NSA_KV_HEADS, HEAD_DIM),
            hspec(NSA_KV_HEADS, HEAD_DIM), hspec(NSA_KV_HEADS, HEAD_DIM),
            pl.BlockSpec((1, tm, LANES), lambda b, i: (b, i, 0)),
        ],
        scratch_shapes=[pltpu.VMEM((FOX_HEADS, LANES), f32)],
        compiler_params=_params("arbitrary", "arbitrary"),
        name="prep",
    )(proj, gq, gk, gn, gs, gw, b_forget, inv_freq)


def _compress_kernel(x_ref, pe_ref, w1_ref, w2_ref, g_ref, o_ref, *, n_rows, do_norm):
    half = CMP_STRIDE * HEAD_DIM
    x = x_ref[0, 0]
    pe = pe_ref[...]
    xa = (x + pe[:, :half]).astype(bf16)
    xb = (x + pe[:, half:]).astype(bf16)
    a = _dot(xa, w1_ref[0:half, :])
    b = _dot(xb, w1_ref[half:2 * half, :])
    pre = a + pltpu.roll(b, n_rows - 1, 0)
    hid = pre * jax.nn.sigmoid(pre)
    out = _dot(hid.astype(bf16), w2_ref[...])
    if do_norm:
        out = _rms(out, g_ref[...])
    o_ref[0, 0] = out.astype(bf16)


def _compress(x4, pe, w1_bf16, w2_bf16, gain, do_norm):
    n_batch, n_g, seq_len, _ = x4.shape
    n_rows = seq_len // CMP_STRIDE
    half = CMP_STRIDE * HEAD_DIM
    xr = x4.reshape(n_batch, n_g, n_rows, half)
    return pl.pallas_call(
        functools.partial(_compress_kernel, n_rows=n_rows, do_norm=do_norm),
        out_shape=jax.ShapeDtypeStruct((n_batch, n_g, n_rows, HEAD_DIM), bf16),
        grid=(n_batch, n_g),
        in_specs=[
            pl.BlockSpec((1, 1, n_rows, half), lambda b, g: (b, g, 0, 0)),
            pl.BlockSpec((1, 2 * half), lambda b, g: (0, 0)),
            pl.BlockSpec((2 * half, CMP_HIDDEN), lambda b, g: (0, 0)),
            pl.BlockSpec((CMP_HIDDEN, HEAD_DIM), lambda b, g: (0, 0)),
            pl.BlockSpec((1, HEAD_DIM), lambda b, g: (0, 0)),
        ],
        out_specs=pl.BlockSpec((1, 1, n_rows, HEAD_DIM), lambda b, g: (b, g, 0, 0)),
        compiler_params=_params("arbitrary", "arbitrary"),
        name="compress",
    )(xr, pe.reshape(1, 2 * half), w1_bf16, w2_bf16, gain)


def _flash_update(s, v, m_prev, l_prev, acc_prev):
    m_new = jnp.maximum(m_prev, jnp.max(s, axis=-1, keepdims=True))
    alpha = jnp.exp(m_prev - m_new)
    p = jnp.exp(s - m_new)
    l_new = alpha * l_prev + jnp.sum(p, axis=-1, keepdims=True)
    acc_new = alpha * acc_prev + _dot(p.astype(bf16), v)
    return m_new, l_new, acc_new


def _fox_kernel(q_ref, k_ref, v_ref, c_ref, o_ref, *, tq):
    i = pl.program_id(2)
    q = q_ref[0, 0]
    c0 = c_ref[0, 0, i][:, 0:1]

    def scores(j):
        k = k_ref[0, 0, pl.ds(pl.multiple_of(j * tq, tq), tq), :]
        return _dot_nt(q, k) + (c0 - c_ref[0, 0, j])

    def values(j):
        return v_ref[0, 0, pl.ds(pl.multiple_of(j * tq, tq), tq), :]

    def body(j, carry):
        return _flash_update(scores(j), values(j), *carry)

    init = (jnp.full((tq, 1), -jnp.inf, f32), jnp.zeros((tq, 1), f32), jnp.zeros((tq, HEAD_DIM), f32))
    carry = lax.fori_loop(0, i, body, init)
    r = lax.broadcasted_iota(jnp.int32, (tq, tq), 0)
    c = lax.broadcasted_iota(jnp.int32, (tq, tq), 1)
    s = jnp.where(c <= r, scores(i), -jnp.inf)
    _, l, acc = _flash_update(s, values(i), *carry)
    o_ref[0] = acc / l


def _fox(fq, fk, fv, cum):
    n_batch, n_h, seq_len, _ = fq.shape
    tq = 512
    nkb = seq_len // tq
    cum5 = cum.reshape(n_batch, n_h, nkb, 1, tq)
    return pl.pallas_call(
        functools.partial(_fox_kernel, tq=tq),
        out_shape=jax.ShapeDtypeStruct((n_batch, seq_len, n_h * HEAD_DIM), f32),
        grid=(n_batch, n_h, nkb),
        in_specs=[
            pl.BlockSpec((1, 1, tq, HEAD_DIM), lambda b, h, i: (b, h, i, 0)),
            pl.BlockSpec((1, 1, seq_len, HEAD_DIM), lambda b, h, i: (b, h, 0, 0)),
            pl.BlockSpec((1, 1, seq_len, HEAD_DIM), lambda b, h, i: (b, h, 0, 0)),
            pl.BlockSpec((1, 1, nkb, 1, tq), lambda b, h, i: (b, h, 0, 0, 0)),
        ],
        out_specs=pl.BlockSpec((1, tq, HEAD_DIM), lambda b, h, i: (b, i, h)),
        compiler_params=_params("arbitrary", "arbitrary", "arbitrary"),
        name="fox",
    )(fq, fk, fv, cum5)


def _cmp_kernel(q_ref, k_ref, v_ref, gate_ref, ov_ref, o_ref, sel_ref, *, tq, n_rows, n_cmp, n_sel):
    i = pl.program_id(2)
    g = pl.program_id(1)
    hg = NSA_GROUP
    q = q_ref[0].reshape(hg * tq, HEAD_DIM)
    s = _dot_nt(q, k_ref[0, 0]).reshape(hg, tq, n_rows)
    t = i * tq + lax.broadcasted_iota(jnp.int32, (1, tq, 1), 1)
    n = lax.broadcasted_iota(jnp.int32, (1, 1, n_rows), 2)
    valid = (n * CMP_STRIDE + (CMP_LEN - 1) <= t) & (n < n_cmp)
    s = jnp.where(valid, s, -jnp.inf)
    m = jnp.max(s, axis=-1, keepdims=True)
    m = jnp.where(m == -jnp.inf, 0.0, m)
    e = jnp.exp(s - m)
    p = e / jnp.maximum(jnp.sum(e, axis=-1, keepdims=True), 1e-30)
    o = _dot(p.reshape(hg * tq, n_rows).astype(bf16), v_ref[0, 0])
    _write_gated(o_ref, gate_ref, o, g, 0, tq)

    lane = lax.broadcasted_iota(jnp.int32, (tq, LANES), 1)
    psum = p[0] + p[1] + p[2] + p[3]
    imp = _dot_f32_by_exact(psum, ov_ref[...])
    tq_pos = i * tq + lax.broadcasted_iota(jnp.int32, (tq, 1), 0)
    q_blk = lax.shift_right_logical(tq_pos, SEL_SHIFT)
    causal = lane <= q_blk
    forced = (lane == 0) | (lane == q_blk) | (lane == q_blk - 1)
    key = jnp.where(forced, jnp.inf, jnp.where(causal, imp, -1.0))
    key = jnp.where(lane < n_sel, key, -3.0)
    lane_f = lane.astype(f32)
    sel = jnp.zeros((tq, LANES), jnp.bool_)
    for _ in range(min(SEL_TOPK, n_sel)):
        mx = jnp.max(key, axis=-1, keepdims=True)
        first = jnp.min(jnp.where(key == mx, lane_f, float(LANES)), axis=-1, keepdims=True)
        pick = lane_f == first
        sel = sel | pick
        key = jnp.where(pick, -2.0, key)
    sel_ref[0, 0] = jnp.where(sel & causal, 0.0, -MASK_BIG).astype(bf16)


def _cmp(nqu, kcmp, vcmp, gates, n_cmp):
    n_batch, _, seq_len, _ = nqu.shape
    n_rows = kcmp.shape[2]
    n_sel = seq_len // SEL_BLOCK
    assert n_sel <= SEL_LANES
    tq = 128
    c0 = np.arange(n_rows) * CMP_STRIDE
    s0 = np.arange(SEL_LANES) * SEL_BLOCK
    overlap = np.clip(np.minimum(c0[:, None] + CMP_LEN, s0[None, :] + SEL_BLOCK)
                      - np.maximum(c0[:, None], s0[None, :]), 0, None).astype(np.float32) / CMP_LEN
    overlap[n_cmp:, :] = 0.0
    overlap[:, n_sel:] = 0.0
    return pl.pallas_call(
        functools.partial(_cmp_kernel, tq=tq, n_rows=n_rows, n_cmp=n_cmp, n_sel=n_sel),
        out_shape=[
            jax.ShapeDtypeStruct((n_batch, seq_len, NSA_W), f32),
            jax.ShapeDtypeStruct((n_batch, NSA_KV_HEADS, seq_len, SEL_LANES), bf16),
        ],
        grid=(n_batch, NSA_KV_HEADS, seq_len // tq),
        in_specs=[
            pl.BlockSpec((1, NSA_GROUP, tq, HEAD_DIM), lambda b, g, i: (b, g, i, 0)),
            pl.BlockSpec((1, 1, n_rows, HEAD_DIM), lambda b, g, i: (b, g, 0, 0)),
            pl.BlockSpec((1, 1, n_rows, HEAD_DIM), lambda b, g, i: (b, g, 0, 0)),
            pl.BlockSpec((1, tq, LANES), lambda b, g, i: (b, i, 0)),
            pl.BlockSpec((n_rows, SEL_LANES), lambda b, g, i: (0, 0)),
        ],
        out_specs=[
            pl.BlockSpec((1, tq, NSA_GROUP * HEAD_DIM), lambda b, g, i: (b, i, g)),
            pl.BlockSpec((1, 1, tq, SEL_LANES), lambda b, g, i: (b, g, i, 0)),
        ],
        compiler_params=_params("arbitrary", "arbitrary", "arbitrary"),
        name="cmp",
    )(nqu, kcmp, vcmp, gates, jnp.asarray(overlap, dtype=bf16))


def _write_gated(o_ref, gate_ref, out, g, branch, tq):
    gates = gate_ref[0]
    lane = lax.broadcasted_iota(jnp.int32, (tq, LANES), 1)
    for h in range(NSA_GROUP):
        gcol = FOX_HEADS + (g * NSA_GROUP + h) * N_BRANCH + branch
        gh = jnp.sum(jnp.where(lane == gcol, gates, 0.0), axis=-1, keepdims=True)
        o_ref[0, :, h * HEAD_DIM:(h + 1) * HEAD_DIM] = gh * out[h * tq:(h + 1) * tq]


def _slc_kernel(q_ref, sel_ref, k_ref, v_ref, gate_ref, o_ref, *, tq, tk):
    i = pl.program_id(2)
    g = pl.program_id(1)
    hg = NSA_GROUP
    q = q_ref[0].reshape(hg * tq, HEAD_DIM)
    sel = sel_ref[0, 0]
    qa = jnp.concatenate([q, jnp.concatenate([sel] * hg, axis=0)], axis=1)
    q0 = i * tq
    jd = q0 // tk

    def scores(j):
        k = k_ref[0, 0, pl.ds(pl.multiple_of(j * tk, tk), tk), :]
        return _dot_nt(qa, k)

    def values(j):
        return v_ref[0, 0, pl.ds(pl.multiple_of(j * tk, tk), tk), :]

    def body(j, carry):
        return _flash_update(scores(j), values(j), *carry)

    rows = hg * tq
    init = (jnp.full((rows, 1), -jnp.inf, f32), jnp.zeros((rows, 1), f32), jnp.zeros((rows, HEAD_DIM), f32))
    carry = lax.fori_loop(0, jd, body, init)
    t = q0 + lax.broadcasted_iota(jnp.int32, (1, tq, tk), 1)
    kp = jd * tk + lax.broadcasted_iota(jnp.int32, (1, tq, tk), 2)
    s = jnp.where(kp <= t, scores(jd).reshape(hg, tq, tk), -jnp.inf).reshape(rows, tk)
    _, l, acc = _flash_update(s, values(jd), *carry)
    _write_gated(o_ref, gate_ref, acc / l, g, 1, tq)


def _slc(nqr, selneg, kaug, vsl, gates):
    n_batch, _, seq_len, _ = nqr.shape
    tq, tk = 128, 512
    return pl.pallas_call(
        functools.partial(_slc_kernel, tq=tq, tk=tk),
        out_shape=jax.ShapeDtypeStruct((n_batch, seq_len, NSA_W), f32),
        grid=(n_batch, NSA_KV_HEADS, seq_len // tq),
        in_specs=[
            pl.BlockSpec((1, NSA_GROUP, tq, HEAD_DIM), lambda b, g, i: (b, g, i, 0)),
            pl.BlockSpec((1, 1, tq, SEL_LANES), lambda b, g, i: (b, g, i, 0)),
            pl.BlockSpec((1, 1, seq_len, HEAD_DIM + SEL_LANES), lambda b, g, i: (b, g, 0, 0)),
            pl.BlockSpec((1, 1, seq_len, HEAD_DIM), lambda b, g, i: (b, g, 0, 0)),
            pl.BlockSpec((1, tq, LANES), lambda b, g, i: (b, i, 0)),
        ],
        out_specs=pl.BlockSpec((1, tq, NSA_GROUP * HEAD_DIM), lambda b, g, i: (b, i, g)),
        compiler_params=_params("arbitrary", "arbitrary", "arbitrary"),
        name="slc",
    )(nqr, selneg, kaug, vsl, gates)


def _win_kernel(q_ref, k_ref, v_ref, gate_ref, o_ref, *, tq, span):
    i = pl.program_id(2)
    g = pl.program_id(1)
    hg = NSA_GROUP
    q = q_ref[0].reshape(hg * tq, HEAD_DIM)
    q0 = i * tq
    k0 = pl.multiple_of(jnp.maximum(q0 - WINDOW, 0), tq)
    k = k_ref[0, 0, pl.ds(k0, span), :]
    v = v_ref[0, 0, pl.ds(k0, span), :]
    s = _dot_nt(q, k).reshape(hg, tq, span)
    t = q0 + lax.broadcasted_iota(jnp.int32, (1, tq, span), 1)
    kp = k0 + lax.broadcasted_iota(jnp.int32, (1, tq, span), 2)
    diff = t - kp
    s = jnp.where((diff >= 0) & (diff < WINDOW), s, -jnp.inf)
    m = jnp.max(s, axis=-1, keepdims=True)
    e = jnp.exp(s - m)
    l = jnp.sum(e, axis=-1, keepdims=True)
    o = _dot(e.reshape(hg * tq, span).astype(bf16), v) / l.reshape(hg * tq, 1)
    _write_gated(o_ref, gate_ref, o, g, 2, tq)


def _win(nqr, kw, vw, gates):
    n_batch, _, seq_len, _ = nqr.shape
    tq = 128
    span = WINDOW + tq
    assert seq_len >= span
    return pl.pallas_call(
        functools.partial(_win_kernel, tq=tq, span=span),
        out_shape=jax.ShapeDtypeStruct((n_batch, seq_len, NSA_W), f32),
        grid=(n_batch, NSA_KV_HEADS, seq_len // tq),
        in_specs=[
            pl.BlockSpec((1, NSA_GROUP, tq, HEAD_DIM), lambda b, g, i: (b, g, i, 0)),
            pl.BlockSpec((1, 1, seq_len, HEAD_DIM), lambda b, g, i: (b, g, 0, 0)),
            pl.BlockSpec((1, 1, seq_len, HEAD_DIM), lambda b, g, i: (b, g, 0, 0)),
            pl.BlockSpec((1, tq, LANES), lambda b, g, i: (b, i, 0)),
        ],
        out_specs=pl.BlockSpec((1, tq, NSA_GROUP * HEAD_DIM), lambda b, g, i: (b, i, g)),
        compiler_params=_params("arbitrary", "arbitrary", "arbitrary"),
        name="win",
    )(nqr, kw, vw, gates)


def _outproj_kernel(fox_ref, c_ref, s_ref, w_ref, x_ref, mod_ref, wo_ref, o_ref, a_ref):
    @pl.when(pl.program_id(1) == 0)
    def _():
        a_ref[:, 0:FOX_W] = fox_ref[...].astype(bf16)
        a_ref[:, FOX_W:FOX_W + NSA_W] = (c_ref[...] + s_ref[...] + w_ref[...]).astype(bf16)

    gate1 = mod_ref[0][2:3]
    o_ref[...] = x_ref[...] + gate1 * _dot(a_ref[...], wo_ref[...])


def _outproj(ofox, ocmp, oslc, owin, x2d, mod3, wo_bf16, seq_len):
    m_rows = x2d.shape[0]
    tm, tn = 512, 1024
    per_b = seq_len // tm
    half = pl.BlockSpec((tm, FOX_W), lambda i, j: (i, 0))
    return pl.pallas_call(
        _outproj_kernel,
        out_shape=jax.ShapeDtypeStruct((m_rows, D_MODEL), f32),
        grid=(m_rows // tm, D_MODEL // tn),
        in_specs=[
            half, half, half, half,
            pl.BlockSpec((tm, tn), lambda i, j: (i, j)),
            pl.BlockSpec((1, 6, tn), lambda i, j: (i // per_b, 0, j)),
            pl.BlockSpec((D_MODEL, tn), lambda i, j: (0, j)),
        ],
        out_specs=pl.BlockSpec((tm, tn), lambda i, j: (i, j)),
        scratch_shapes=[pltpu.VMEM((tm, D_MODEL), bf16)],
        compiler_params=_params("arbitrary", "arbitrary"),
        name="outproj",
    )(ofox, ocmp, oslc, owin, x2d, mod3, wo_bf16)


def _mlp_kernel(x_ref, mod_ref, g_ref, wu_ref, wd_ref, o_ref, h_ref, acc_ref):
    f = pl.program_id(1)

    @pl.when(f == 0)
    def _():
        md = mod_ref[0]
        y = _rms(x_ref[...], g_ref[...])
        h_ref[...] = (y * (1.0 + md[4:5]) + md[3:4]).astype(bf16)
        acc_ref[...] = jnp.zeros_like(acc_ref)

    u = jnp.maximum(_dot(h_ref[...], wu_ref[...]), 0.0)
    acc_ref[...] += _dot((u * u).astype(bf16), wd_ref[...])

    @pl.when(f == pl.num_programs(1) - 1)
    def _():
        o_ref[...] = x_ref[...] + mod_ref[0][5:6] * acc_ref[...]


def _mlp(x2d, mod3, norm_g, wu_bf16, wd_bf16, seq_len):
    m_rows = x2d.shape[0]
    tm, tf = 512, 512
    per_b = seq_len // tm
    return pl.pallas_call(
        _mlp_kernel,
        out_shape=jax.ShapeDtypeStruct((m_rows, D_MODEL), f32),
        grid=(m_rows // tm, D_FF // tf),
        in_specs=[
            pl.BlockSpec((tm, D_MODEL), lambda i, f: (i, 0)),
            pl.BlockSpec((1, 6, D_MODEL), lambda i, f: (i // per_b, 0, 0)),
            pl.BlockSpec((1, D_MODEL), lambda i, f: (0, 0)),
            pl.BlockSpec((D_MODEL, tf), lambda i, f: (0, f)),
            pl.BlockSpec((tf, D_MODEL), lambda i, f: (f, 0)),
        ],
        out_specs=pl.BlockSpec((tm, D_MODEL), lambda i, f: (i, 0)),
        scratch_shapes=[pltpu.VMEM((tm, D_MODEL), bf16), pltpu.VMEM((tm, D_MODEL), f32)],
        compiler_params=_params("arbitrary", "arbitrary"),
        name="mlp",
    )(x2d, mod3, norm_g, wu_bf16, wd_bf16)


def _layer(x, c, w_ada, b_ada, norm1_g, w_in, b_forget, fox_q_norm, fox_k_norm, nsa_q_norm,
           cmp_k_norm, slc_k_norm, win_k_norm, cmp_pe_k, cmp_w1_k, cmp_w2_k, cmp_pe_v, cmp_w1_v,
           cmp_w2_v, w_out, norm2_g, w_up, w_down):
    n_batch, seq_len, _ = x.shape
    n_cmp = (seq_len - CMP_LEN) // CMP_STRIDE + 1
    row = lambda v: v.reshape(1, -1)

    z0 = COL_NQ
    nq0 = z0 + FOX_HEADS
    gz0 = nq0 + NSA_W + 6 * KV_W
    w_perm = jnp.concatenate([
        w_in[:, :z0], w_in[:, nq0:gz0], w_in[:, z0:nq0], w_in[:, gz0:],
        jnp.zeros((D_MODEL, LANES - FOX_HEADS - N_BRANCH * NSA_HEADS), w_in.dtype)], axis=1).astype(bf16)
    half = HEAD_DIM // 2
    inv_freq = ROPE_THETA ** (-jnp.arange(half, dtype=f32) / half)
    inv_freq = jnp.concatenate([inv_freq, inv_freq]).reshape(1, HEAD_DIM)

    mod3 = _ada(c, w_ada, b_ada).reshape(n_batch, 6, D_MODEL)
    x2d = x.reshape(n_batch * seq_len, D_MODEL)
    proj = _proj(x2d, mod3, row(norm1_g), w_perm, seq_len)
    (fq, fk, fv, cum, nqu, nqr, kc, vc, kaug, vsl, kw, vw, gates) = _prep(
        proj, n_batch, seq_len, row(fox_q_norm), row(fox_k_norm), row(nsa_q_norm), row(slc_k_norm),
        row(win_k_norm), b_forget.reshape(FOX_HEADS, 1), inv_freq)
    kcmp = _compress(kc, cmp_pe_k, cmp_w1_k.astype(bf16), cmp_w2_k.astype(bf16), row(cmp_k_norm), True)
    vcmp = _compress(vc, cmp_pe_v, cmp_w1_v.astype(bf16), cmp_w2_v.astype(bf16), row(cmp_k_norm), False)
    ofox = _fox(fq, fk, fv, cum)
    ocmp, selneg = _cmp(nqu, kcmp, vcmp, gates, n_cmp)
    oslc = _slc(nqr, selneg, kaug, vsl, gates)
    owin = _win(nqr, kw, vw, gates)
    x1 = _outproj(ofox.reshape(-1, FOX_W), ocmp.reshape(-1, NSA_W), oslc.reshape(-1, NSA_W),
                  owin.reshape(-1, NSA_W), x2d, mod3, w_out.astype(bf16), seq_len)
    x2 = _mlp(x1, mod3, row(norm2_g), w_up.astype(bf16), w_down.astype(bf16), seq_len)
    return x2.reshape(n_batch, seq_len, D_MODEL)


def kernel(x, c, w_ada, b_ada, norm1_g, w_in, b_forget, fox_q_norm, fox_k_norm, nsa_q_norm, cmp_k_norm,
           slc_k_norm, win_k_norm, cmp_pe_k, cmp_w1_k, cmp_w2_k, cmp_pe_v, cmp_w1_v, cmp_w2_v, w_out,
           norm2_g, w_up, w_down):
    depth = w_ada.shape[0]
    for l in range(depth):
        x = _layer(x, c, w_ada[l], b_ada[l], norm1_g[l], w_in[l], b_forget[l], fox_q_norm[l], fox_k_norm[l],
                   nsa_q_norm[l], cmp_k_norm[l], slc_k_norm[l], win_k_norm[l], cmp_pe_k[l], cmp_w1_k[l],
                   cmp_w2_k[l], cmp_pe_v[l], cmp_w1_v[l], cmp_w2_v[l], w_out[l], norm2_g[l], w_up[l],
                   w_down[l])
    return x
```

```python
import functools

import numpy as np
import jax
import jax.numpy as jnp
from jax import lax
from jax.experimental import pallas as pl
from jax.experimental.pallas import tpu as pltpu

D_MODEL = 2048
HEAD_DIM = 128
FOX_HEADS = 8
NSA_HEADS = 8
NSA_KV_HEADS = 2
NSA_GROUP = NSA_HEADS // NSA_KV_HEADS
N_BRANCH = 3
D_FF = 4 * D_MODEL
ROPE_THETA = 10000.0
CMP_LEN = 32
CMP_STRIDE = 16
CMP_HIDDEN = 2 * HEAD_DIM
SEL_BLOCK = 64
SEL_SHIFT = 6
SEL_TOPK = 16
WINDOW = 512
NORM_EPS = 1e-6
ATTN_SCALE = HEAD_DIM ** -0.5
FOX_W = FOX_HEADS * HEAD_DIM
NSA_W = NSA_HEADS * HEAD_DIM
KV_W = NSA_KV_HEADS * HEAD_DIM

LANES = 128
SEL_LANES = LANES
MASK_BIG = 1e30

COL_FQ = 0
COL_FK = COL_FQ + FOX_W
COL_FV = COL_FK + FOX_W
COL_NQ = COL_FV + FOX_W
COL_KC = COL_NQ + NSA_W
COL_VC = COL_KC + KV_W
COL_KS = COL_VC + KV_W
COL_VS = COL_KS + KV_W
COL_KW = COL_VS + KV_W
COL_VW = COL_KW + KV_W
COL_SMALL = COL_VW + KV_W
D_IN_PAD = COL_SMALL + LANES

VMEM_LIMIT = 56 * 1024 * 1024

f32 = jnp.float32
bf16 = jnp.bfloat16


def _params(*sem):
    return pltpu.CompilerParams(dimension_semantics=sem, vmem_limit_bytes=VMEM_LIMIT)


def _dot_nt(a, b):
    return lax.dot_general(a, b, (((1,), (1,)), ((), ())), preferred_element_type=f32)


def _dot(a, b):
    return jnp.dot(a, b, preferred_element_type=f32)


def _split3(x):
    hi = x.astype(bf16)
    r1 = x - hi.astype(f32)
    mid = r1.astype(bf16)
    lo = (r1 - mid.astype(f32)).astype(bf16)
    return hi, mid, lo


def _dot_f32_by_exact(x, w_bf16):
    hi, mid, lo = _split3(x)
    return _dot(hi, w_bf16) + (_dot(mid, w_bf16) + _dot(lo, w_bf16))


def _rms(x, gain):
    ms = jnp.mean(x * x, axis=-1, keepdims=True)
    return x * lax.rsqrt(ms + NORM_EPS) * gain


def _ada_kernel(ct_ref, w_ref, b_ref, o_ref, *, n_batch, k_chunk):
    ct = ct_ref[...]
    act = ct * jax.nn.sigmoid(ct)
    rows = []
    for b in range(n_batch):
        col = act[:, b:b + 1]
        acc = b_ref[...]
        for k0 in range(0, D_MODEL, k_chunk):
            acc = acc + jnp.sum(w_ref[k0:k0 + k_chunk, :] * col[k0:k0 + k_chunk], axis=0, keepdims=True)
        rows.append(acc)
    o_ref[...] = jnp.concatenate(rows, axis=0)


def _ada(c, w_ada, b_ada):
    n_batch = c.shape[0]
    n_out = w_ada.shape[1]
    tn = 1024
    return pl.pallas_call(
        functools.partial(_ada_kernel, n_batch=n_batch, k_chunk=256),
        out_shape=jax.ShapeDtypeStruct((n_batch, n_out), f32),
        grid=(n_out // tn,),
        in_specs=[
            pl.BlockSpec((D_MODEL, n_batch), lambda j: (0, 0)),
            pl.BlockSpec((D_MODEL, tn), lambda j: (0, j)),
            pl.BlockSpec((1, tn), lambda j: (0, j)),
        ],
        out_specs=pl.BlockSpec((n_batch, tn), lambda j: (0, j)),
        compiler_params=_params("arbitrary"),
        name="ada",
    )(c.T, w_ada, b_ada.reshape(1, n_out))


def _proj_kernel(x_ref, mod_ref, g_ref, w_ref, o_ref, h_ref):
    @pl.when(pl.program_id(1) == 0)
    def _():
        md = mod_ref[0]
        y = _rms(x_ref[...], g_ref[...])
        h_ref[...] = (y * (1.0 + md[1:2]) + md[0:1]).astype(bf16)

    o_ref[...] = _dot(h_ref[...], w_ref[...])


def _proj(x2d, mod3, norm_g, w_bf16, seq_len):
    m_rows = x2d.shape[0]
    n_cols = w_bf16.shape[1]
    tm, tn = 512, 1152
    assert seq_len % tm == 0 and n_cols % tn == 0
    per_b = seq_len // tm
    return pl.pallas_call(
        _proj_kernel,
        out_shape=jax.ShapeDtypeStruct((m_rows, n_cols), f32),
        grid=(m_rows // tm, n_cols // tn),
        in_specs=[
            pl.BlockSpec((tm, D_MODEL), lambda i, j: (i, 0)),
            pl.BlockSpec((1, 6, D_MODEL), lambda i, j: (i // per_b, 0, 0)),
            pl.BlockSpec((1, D_MODEL), lambda i, j: (0, 0)),
            pl.BlockSpec((D_MODEL, tn), lambda i, j: (0, j)),
        ],
        out_specs=pl.BlockSpec((tm, tn), lambda i, j: (i, j)),
        scratch_shapes=[pltpu.VMEM((tm, D_MODEL), bf16)],
        compiler_params=_params("arbitrary", "arbitrary"),
        name="proj",
    )(x2d, mod3, norm_g, w_bf16)


def _prep_kernel(p_ref, gq_ref, gk_ref, gn_ref, gs_ref, gw_ref, bf_ref, inv_ref,
                 fq_ref, fk_ref, fv_ref, cum_ref, nqu_ref, nqr_ref, kc_ref, vc_ref,
                 ks_ref, vs_ref, kw_ref, vw_ref, gate_ref, carry_ref, *, tm):
    i = pl.program_id(1)

    @pl.when(i == 0)
    def _():
        carry_ref[...] = jnp.zeros_like(carry_ref)

    def head(col, h):
        return p_ref[:, col + h * HEAD_DIM: col + (h + 1) * HEAD_DIM]

    row = lax.broadcasted_iota(jnp.int32, (tm, LANES), 0)
    lane = lax.broadcasted_iota(jnp.int32, (tm, LANES), 1)
    pos = i * tm + row
    ang = pos.astype(f32) * inv_ref[...]
    cos = jnp.cos(ang)
    sin = jnp.sin(ang)
    sin_signed = jnp.where(lane < HEAD_DIM // 2, -sin, sin)

    def rope(x):
        return x * cos + pltpu.roll(x, HEAD_DIM // 2, 1) * sin_signed

    for h in range(FOX_HEADS):
        fq_ref[0, h] = (_rms(head(COL_FQ, h), gq_ref[...]) * ATTN_SCALE).astype(bf16)
        fk_ref[0, h] = _rms(head(COL_FK, h), gk_ref[...]).astype(bf16)
        fv_ref[0, h] = head(COL_FV, h).astype(bf16)

    small = p_ref[:, COL_SMALL:COL_SMALL + LANES]
    z = small.T[0:FOX_HEADS, :] + bf_ref[...]
    logf = jnp.minimum(z, 0.0) - jnp.log1p(jnp.exp(-jnp.abs(z)))
    s_idx = lax.broadcasted_iota(jnp.int32, (tm, tm), 0)
    t_idx = lax.broadcasted_iota(jnp.int32, (tm, tm), 1)
    tri = jnp.where(s_idx <= t_idx, 1.0, 0.0).astype(bf16)
    cum = _dot_f32_by_exact(logf, tri) + carry_ref[:, 0:1]
    cum_ref[0] = cum
    carry_ref[...] = jnp.broadcast_to(cum[:, tm - 1:tm], carry_ref.shape)

    gate_ref[0] = jax.nn.sigmoid(small)

    for h in range(NSA_HEADS):
        qn = _rms(head(COL_NQ, h), gn_ref[...]) * ATTN_SCALE
        nqu_ref[0, h] = qn.astype(bf16)
        nqr_ref[0, h] = rope(qn).astype(bf16)
    onehot = jnp.where(lane == lax.shift_right_logical(pos, SEL_SHIFT), 1.0, 0.0).astype(bf16)
    for g in range(NSA_KV_HEADS):
        kc_ref[0, g] = head(COL_KC, g)
        vc_ref[0, g] = head(COL_VC, g)
        ks = rope(_rms(head(COL_KS, g), gs_ref[...])).astype(bf16)
        ks_ref[0, g] = jnp.concatenate([ks, onehot], axis=1)
        vs_ref[0, g] = head(COL_VS, g).astype(bf16)
        kw_ref[0, g] = rope(_rms(head(COL_KW, g), gw_ref[...])).astype(bf16)
        vw_ref[0, g] = head(COL_VW, g).astype(bf16)


def _prep(proj, n_batch, seq_len, gq, gk, gn, gs, gw, b_forget, inv_freq):
    tm = 256
    per_b = seq_len // tm
    hshape = lambda n, w, dt: jax.ShapeDtypeStruct((n_batch, n, seq_len, w), dt)
    hspec = lambda n, w: pl.BlockSpec((1, n, tm, w), lambda b, i: (b, 0, i, 0))
    vec = pl.BlockSpec((1, LANES), lambda b, i: (0, 0))
    return pl.pallas_call(
        functools.partial(_prep_kernel, tm=tm),
        out_shape=[
            hshape(FOX_HEADS, HEAD_DIM, bf16), hshape(FOX_HEADS, HEAD_DIM, bf16), hshape(FOX_HEADS, HEAD_DIM, bf16),
            jax.ShapeDtypeStruct((n_batch, FOX_HEADS, seq_len), f32),
            hshape(NSA_HEADS, HEAD_DIM, bf16), hshape(NSA_HEADS, HEAD_DIM, bf16),
            hshape(NSA_KV_HEADS, HEAD_DIM, f32), hshape(NSA_KV_HEADS, HEAD_DIM, f32),
            hshape(NSA_KV_HEADS, HEAD_DIM + SEL_LANES, bf16), hshape(NSA_KV_HEADS, HEAD_DIM, bf16),
            hshape(NSA_KV_HEADS, HEAD_DIM, bf16), hshape(NSA_KV_HEADS, HEAD_DIM, bf16),
            jax.ShapeDtypeStruct((n_batch, seq_len, LANES), f32),
        ],
        grid=(n_batch, per_b),
        in_specs=[
            pl.BlockSpec((tm, D_IN_PAD), lambda b, i: (b * per_b + i, 0)),
            vec, vec, vec, vec, vec,
            pl.BlockSpec((FOX_HEADS, 1), lambda b, i: (0, 0)),
            vec,
        ],
        out_specs=[
            hspec(FOX_HEADS, HEAD_DIM), hspec(FOX_HEADS, HEAD_DIM), hspec(FOX_HEADS, HEAD_DIM),
            pl.BlockSpec((1, FOX_HEADS, tm), lambda b, i: (b, 0, i)),
            hspec(NSA_HEADS, HEAD_DIM), hspec(NSA_HEADS, HEAD_DIM),
            hspec(NSA_KV_HEADS, HEAD_DIM), hspec(NSA_KV_HEADS, HEAD_DIM),
            hspec(NSA_KV_HEADS, HEAD_DIM + SEL_LANES), hspec(NSA_KV_HEADS, HEAD_DIM),
            hspec(NSA_KV_HEADS, HEAD_DIM), hspec(NSA_KV_HEADS, HEAD_DIM),
            pl.BlockSpec((1, tm, LANES), lambda b, i: (b, i, 0)),
        ],
        scratch_shapes=[pltpu.VMEM((FOX_HEADS, LANES), f32)],
        compiler_params=_params("arbitrary", "arbitrary"),
        name="prep",
    )(proj, gq, gk, gn, gs, gw, b_forget, inv_freq)


def _compress_kernel(x_ref, pe_ref, w1_ref, w2_ref, g_ref, o_ref, *, n_rows, do_norm):
    half = CMP_STRIDE * HEAD_DIM
    x = x_ref[0, 0]
    pe = pe_ref[...]
    xa = (x + pe[:, :half]).astype(bf16)
    xb = (x + pe[:, half:]).astype(bf16)
    a = _dot(xa, w1_ref[0:half, :])
    b = _dot(xb, w1_ref[half:2 * half, :])
    pre = a + pltpu.roll(b, n_rows - 1, 0)
    hid = pre * jax.nn.sigmoid(pre)
    out = _dot(hid.astype(bf16), w2_ref[...])
    if do_norm:
        out = _rms(out, g_ref[...])
    o_ref[0, 0] = out.astype(bf16)


def _compress(x4, pe, w1_bf16, w2_bf16, gain, do_norm):
    n_batch, n_g, seq_len, _ = x4.shape
    n_rows = seq_len // CMP_STRIDE
    half = CMP_STRIDE * HEAD_DIM
    xr = x4.reshape(n_batch, n_g, n_rows, half)
    return pl.pallas_call(
        functools.partial(_compress_kernel, n_rows=n_rows, do_norm=do_norm),
        out_shape=jax.ShapeDtypeStruct((n_batch, n_g, n_rows, HEAD_DIM), bf16),
        grid=(n_batch, n_g),
        in_specs=[
            pl.BlockSpec((1, 1, n_rows, half), lambda b, g: (b, g, 0, 0)),
            pl.BlockSpec((1, 2 * half), lambda b, g: (0, 0)),
            pl.BlockSpec((2 * half, CMP_HIDDEN), lambda b, g: (0, 0)),
            pl.BlockSpec((CMP_HIDDEN, HEAD_DIM), lambda b, g: (0, 0)),
            pl.BlockSpec((1, HEAD_DIM), lambda b, g: (0, 0)),
        ],
        out_specs=pl.BlockSpec((1, 1, n_rows, HEAD_DIM), lambda b, g: (b, g, 0, 0)),
        compiler_params=_params("arbitrary", "arbitrary"),
        name="compress",
    )(xr, pe.reshape(1, 2 * half), w1_bf16, w2_bf16, gain)


def _flash_update(s, v, m_prev, l_prev, acc_prev):
    m_new = jnp.maximum(m_prev, jnp.max(s, axis=-1, keepdims=True))
    alpha = jnp.exp(m_prev - m_new)
    p = jnp.exp(s - m_new)
    l_new = alpha * l_prev + jnp.sum(p, axis=-1, keepdims=True)
    acc_new = alpha * acc_prev + _dot(p.astype(bf16), v)
    return m_new, l_new, acc_new


def _fox_kernel(q_ref, k_ref, v_ref, c_ref, o_ref, *, tq):
    i = pl.program_id(2)
    q = q_ref[0, 0]
    c0 = c_ref[0, 0, i][:, 0:1]

    def scores(j):
        k = k_ref[0, 0, pl.ds(pl.multiple_of(j * tq, tq), tq), :]
        return _dot_nt(q, k) + (c0 - c_ref[0, 0, j])

    def values(j):
        return v_ref[0, 0, pl.ds(pl.multiple_of(j * tq, tq), tq), :]

    def body(j, carry):
        return _flash_update(scores(j), values(j), *carry)

    init = (jnp.full((tq, 1), -jnp.inf, f32), jnp.zeros((tq, 1), f32), jnp.zeros((tq, HEAD_DIM), f32))
    carry = lax.fori_loop(0, i, body, init)
    r = lax.broadcasted_iota(jnp.int32, (tq, tq), 0)
    c = lax.broadcasted_iota(jnp.int32, (tq, tq), 1)
    s = jnp.where(c <= r, scores(i), -jnp.inf)
    _, l, acc = _flash_update(s, values(i), *carry)
    o_ref[0] = acc / l


def _fox(fq, fk, fv, cum):
    n_batch, n_h, seq_len, _ = fq.shape
    tq = 512
    nkb = seq_len // tq
    cum5 = cum.reshape(n_batch, n_h, nkb, 1, tq)
    return pl.pallas_call(
        functools.partial(_fox_kernel, tq=tq),
        out_shape=jax.ShapeDtypeStruct((n_batch, seq_len, n_h * HEAD_DIM), f32),
        grid=(n_batch, n_h, nkb),
        in_specs=[
            pl.BlockSpec((1, 1, tq, HEAD_DIM), lambda b, h, i: (b, h, i, 0)),
            pl.BlockSpec((1, 1, seq_len, HEAD_DIM), lambda b, h, i: (b, h, 0, 0)),
            pl.BlockSpec((1, 1, seq_len, HEAD_DIM), lambda b, h, i: (b, h, 0, 0)),
            pl.BlockSpec((1, 1, nkb, 1, tq), lambda b, h, i: (b, h, 0, 0, 0)),
        ],
        out_specs=pl.BlockSpec((1, tq, HEAD_DIM), lambda b, h, i: (b, i, h)),
        compiler_params=_params("arbitrary", "arbitrary", "arbitrary"),
        name="fox",
    )(fq, fk, fv, cum5)


def _cmp_kernel(q_ref, k_ref, v_ref, gate_ref, ov_ref, o_ref, sel_ref, *, tq, n_rows, n_cmp, n_sel):
    i = pl.program_id(2)
    g = pl.program_id(1)
    hg = NSA_GROUP
    q = q_ref[0].reshape(hg * tq, HEAD_DIM)
    s = _dot_nt(q, k_ref[0, 0]).reshape(hg, tq, n_rows)
    t = i * tq + lax.broadcasted_iota(jnp.int32, (1, tq, 1), 1)
    n = lax.broadcasted_iota(jnp.int32, (1, 1, n_rows), 2)
    valid = (n * CMP_STRIDE + (CMP_LEN - 1) <= t) & (n < n_cmp)
    s = jnp.where(valid, s, -jnp.inf)
    m = jnp.max(s, axis=-1, keepdims=True)
    m = jnp.where(m == -jnp.inf, 0.0, m)
    e = jnp.exp(s - m)
    p = e / jnp.maximum(jnp.sum(e, axis=-1, keepdims=True), 1e-30)
    o = _dot(p.reshape(hg * tq, n_rows).astype(bf16), v_ref[0, 0])
    _write_gated(o_ref, gate_ref, o, g, 0, tq)

    lane = lax.broadcasted_iota(jnp.int32, (tq, LANES), 1)
    psum = p[0] + p[1] + p[2] + p[3]
    imp = _dot_f32_by_exact(psum, ov_ref[...])
    tq_pos = i * tq + lax.broadcasted_iota(jnp.int32, (tq, 1), 0)
    q_blk = lax.shift_right_logical(tq_pos, SEL_SHIFT)
    causal = lane <= q_blk
    forced = (lane == 0) | (lane == q_blk) | (lane == q_blk - 1)
    key = jnp.where(forced, jnp.inf, jnp.where(causal, imp, -1.0))
    key = jnp.where(lane < n_sel, key, -3.0)
    lane_f = lane.astype(f32)
    sel = jnp.zeros((tq, LANES), jnp.bool_)
    for _ in range(min(SEL_TOPK, n_sel)):
        mx = jnp.max(key, axis=-1, keepdims=True)
        first = jnp.min(jnp.where(key == mx, lane_f, float(LANES)), axis=-1, keepdims=True)
        pick = lane_f == first
        sel = sel | pick
        key = jnp.where(pick, -2.0, key)
    sel_ref[0, 0] = jnp.where(sel & causal, 0.0, -MASK_BIG).astype(bf16)


def _cmp(nqu, kcmp, vcmp, gates, n_cmp):
    n_batch, _, seq_len, _ = nqu.shape
    n_rows = kcmp.shape[2]
    n_sel = seq_len // SEL_BLOCK
    assert n_sel <= SEL_LANES
    tq = 1024
    c0 = np.arange(n_rows) * CMP_STRIDE
    s0 = np.arange(SEL_LANES) * SEL_BLOCK
    overlap = np.clip(np.minimum(c0[:, None] + CMP_LEN, s0[None, :] + SEL_BLOCK)
                      - np.maximum(c0[:, None], s0[None, :]), 0, None).astype(np.float32) / CMP_LEN
    overlap[n_cmp:, :] = 0.0
    overlap[:, n_sel:] = 0.0
    return pl.pallas_call(
        functools.partial(_cmp_kernel, tq=tq, n_rows=n_rows, n_cmp=n_cmp, n_sel=n_sel),
        out_shape=[
            jax.ShapeDtypeStruct((n_batch, seq_len, NSA_W), f32),
            jax.ShapeDtypeStruct((n_batch, NSA_KV_HEADS, seq_len, SEL_LANES), bf16),
        ],
        grid=(n_batch, NSA_KV_HEADS, seq_len // tq),
        in_specs=[
            pl.BlockSpec((1, NSA_GROUP, tq, HEAD_DIM), lambda b, g, i: (b, g, i, 0)),
            pl.BlockSpec((1, 1, n_rows, HEAD_DIM), lambda b, g, i: (b, g, 0, 0)),
            pl.BlockSpec((1, 1, n_rows, HEAD_DIM), lambda b, g, i: (b, g, 0, 0)),
            pl.BlockSpec((1, tq, LANES), lambda b, g, i: (b, i, 0)),
            pl.BlockSpec((n_rows, SEL_LANES), lambda b, g, i: (0, 0)),
        ],
        out_specs=[
            pl.BlockSpec((1, tq, NSA_GROUP * HEAD_DIM), lambda b, g, i: (b, i, g)),
            pl.BlockSpec((1, 1, tq, SEL_LANES), lambda b, g, i: (b, g, i, 0)),
        ],
        compiler_params=_params("arbitrary", "arbitrary", "arbitrary"),
        name="cmp",
    )(nqu, kcmp, vcmp, gates, jnp.asarray(overlap, dtype=bf16))


def _write_gated(o_ref, gate_ref, out, g, branch, tq):
    gates = gate_ref[0]
    lane = lax.broadcasted_iota(jnp.int32, (tq, LANES), 1)
    for h in range(NSA_GROUP):
        gcol = FOX_HEADS + (g * NSA_GROUP + h) * N_BRANCH + branch
        gh = jnp.sum(jnp.where(lane == gcol, gates, 0.0), axis=-1, keepdims=True)
        o_ref[0, :, h * HEAD_DIM:(h + 1) * HEAD_DIM] = gh * out[h * tq:(h + 1) * tq]


def _slc_kernel(q_ref, sel_ref, k_ref, v_ref, gate_ref, o_ref, *, tq, tk):
    i = pl.program_id(2)
    g = pl.program_id(1)
    hg = NSA_GROUP
    q = q_ref[0].reshape(hg * tq, HEAD_DIM)
    sel = sel_ref[0, 0]
    qa = jnp.concatenate([q, jnp.concatenate([sel] * hg, axis=0)], axis=1)
    q0 = i * tq
    jd = q0 // tk

    def scores(j):
        k = k_ref[0, 0, pl.ds(pl.multiple_of(j * tk, tk), tk), :]
        return _dot_nt(qa, k)

    def values(j):
        return v_ref[0, 0, pl.ds(pl.multiple_of(j * tk, tk), tk), :]

    def body(j, carry):
        return _flash_update(scores(j), values(j), *carry)

    rows = hg * tq
    init = (jnp.full((rows, 1), -jnp.inf, f32), jnp.zeros((rows, 1), f32), jnp.zeros((rows, HEAD_DIM), f32))
    carry = lax.fori_loop(0, jd, body, init)
    t = q0 + lax.broadcasted_iota(jnp.int32, (1, tq, tk), 1)
    kp = jd * tk + lax.broadcasted_iota(jnp.int32, (1, tq, tk), 2)
    s = jnp.where(kp <= t, scores(jd).reshape(hg, tq, tk), -jnp.inf).reshape(rows, tk)
    _, l, acc = _flash_update(s, values(jd), *carry)
    _write_gated(o_ref, gate_ref, acc / l, g, 1, tq)


def _slc(nqr, selneg, kaug, vsl, gates):
    n_batch, _, seq_len, _ = nqr.shape
    tq, tk = 128, 512
    return pl.pallas_call(
        functools.partial(_slc_kernel, tq=tq, tk=tk),
        out_shape=jax.ShapeDtypeStruct((n_batch, seq_len, NSA_W), f32),
        grid=(n_batch, NSA_KV_HEADS, seq_len // tq),
        in_specs=[
            pl.BlockSpec((1, NSA_GROUP, tq, HEAD_DIM), lambda b, g, i: (b, g, i, 0)),
            pl.BlockSpec((1, 1, tq, SEL_LANES), lambda b, g, i: (b, g, i, 0)),
            pl.BlockSpec((1, 1, seq_len, HEAD_DIM + SEL_LANES), lambda b, g, i: (b, g, 0, 0)),
            pl.BlockSpec((1, 1, seq_len, HEAD_DIM), lambda b, g, i: (b, g, 0, 0)),
            pl.BlockSpec((1, tq, LANES), lambda b, g, i: (b, i, 0)),
        ],
        out_specs=pl.BlockSpec((1, tq, NSA_GROUP * HEAD_DIM), lambda b, g, i: (b, i, g)),
        compiler_params=_params("arbitrary", "arbitrary", "arbitrary"),
        name="slc",
    )(nqr, selneg, kaug, vsl, gates)


def _win_kernel(q_ref, k_ref, v_ref, gate_ref, o_ref, *, tq, span):
    i = pl.program_id(2)
    g = pl.program_id(1)
    hg = NSA_GROUP
    q = q_ref[0].reshape(hg * tq, HEAD_DIM)
    q0 = i * tq
    k0 = pl.multiple_of(jnp.maximum(q0 - WINDOW, 0), tq)
    k = k_ref[0, 0, pl.ds(k0, span), :]
    v = v_ref[0, 0, pl.ds(k0, span), :]
    s = _dot_nt(q, k).reshape(hg, tq, span)
    t = q0 + lax.broadcasted_iota(jnp.int32, (1, tq, span), 1)
    kp = k0 + lax.broadcasted_iota(jnp.int32, (1, tq, span), 2)
    diff = t - kp
    s = jnp.where((diff >= 0) & (diff < WINDOW), s, -jnp.inf)
    m = jnp.max(s, axis=-1, keepdims=True)
    e = jnp.exp(s - m)
    l = jnp.sum(e, axis=-1, keepdims=True)
    o = _dot(e.reshape(hg * tq, span).astype(bf16), v) / l.reshape(hg * tq, 1)
    _write_gated(o_ref, gate_ref, o, g, 2, tq)


def _win(nqr, kw, vw, gates):
    n_batch, _, seq_len, _ = nqr.shape
    tq = 128
    span = WINDOW + tq
    assert seq_len >= span
    return pl.pallas_call(
        functools.partial(_win_kernel, tq=tq, span=span),
        out_shape=jax.ShapeDtypeStruct((n_batch, seq_len, NSA_W), f32),
        grid=(n_batch, NSA_KV_HEADS, seq_len // tq),
        in_specs=[
            pl.BlockSpec((1, NSA_GROUP, tq, HEAD_DIM), lambda b, g, i: (b, g, i, 0)),
            pl.BlockSpec((1, 1, seq_len, HEAD_DIM), lambda b, g, i: (b, g, 0, 0)),
            pl.BlockSpec((1, 1, seq_len, HEAD_DIM), lambda b, g, i: (b, g, 0, 0)),
            pl.BlockSpec((1, tq, LANES), lambda b, g, i: (b, i, 0)),
        ],
        out_specs=pl.BlockSpec((1, tq, NSA_GROUP * HEAD_DIM), lambda b, g, i: (b, i, g)),
        compiler_params=_params("arbitrary", "arbitrary", "arbitrary"),
        name="win",
    )(nqr, kw, vw, gates)


def _outproj_kernel(fox_ref, c_ref, s_ref, w_ref, x_ref, mod_ref, wo_ref, o_ref, a_ref):
    @pl.when(pl.program_id(1) == 0)
    def _():
        a_ref[:, 0:FOX_W] = fox_ref[...].astype(bf16)
        a_ref[:, FOX_W:FOX_W + NSA_W] = (c_ref[...] + s_ref[...] + w_ref[...]).astype(bf16)

    gate1 = mod_ref[0][2:3]
    o_ref[...] = x_ref[...] + gate1 * _dot(a_ref[...], wo_ref[...])


def _outproj(ofox, ocmp, oslc, owin, x2d, mod3, wo_bf16, seq_len):
    m_rows = x2d.shape[0]
    tm, tn = 512, 1024
    per_b = seq_len // tm
    half = pl.BlockSpec((tm, FOX_W), lambda i, j: (i, 0))
    return pl.pallas_call(
        _outproj_kernel,
        out_shape=jax.ShapeDtypeStruct((m_rows, D_MODEL), f32),
        grid=(m_rows // tm, D_MODEL // tn),
        in_specs=[
            half, half, half, half,
            pl.BlockSpec((tm, tn), lambda i, j: (i, j)),
            pl.BlockSpec((1, 6, tn), lambda i, j: (i // per_b, 0, j)),
            pl.BlockSpec((D_MODEL, tn), lambda i, j: (0, j)),
        ],
        out_specs=pl.BlockSpec((tm, tn), lambda i, j: (i, j)),
        scratch_shapes=[pltpu.VMEM((tm, D_MODEL), bf16)],
        compiler_params=_params("arbitrary", "arbitrary"),
        name="outproj",
    )(ofox, ocmp, oslc, owin, x2d, mod3, wo_bf16)


def _mlp_kernel(x_ref, mod_ref, g_ref, wu_ref, wd_ref, o_ref, h_ref, acc_ref):
    f = pl.program_id(1)

    @pl.when(f == 0)
    def _():
        md = mod_ref[0]
        y = _rms(x_ref[...], g_ref[...])
        h_ref[...] = (y * (1.0 + md[4:5]) + md[3:4]).astype(bf16)
        acc_ref[...] = jnp.zeros_like(acc_ref)

    u = jnp.maximum(_dot(h_ref[...], wu_ref[...]), 0.0)
    acc_ref[...] += _dot((u * u).astype(bf16), wd_ref[...])

    @pl.when(f == pl.num_programs(1) - 1)
    def _():
        o_ref[...] = x_ref[...] + mod_ref[0][5:6] * acc_ref[...]


def _mlp(x2d, mod3, norm_g, wu_bf16, wd_bf16, seq_len):
    m_rows = x2d.shape[0]
    tm, tf = 512, 512
    per_b = seq_len // tm
    return pl.pallas_call(
        _mlp_kernel,
        out_shape=jax.ShapeDtypeStruct((m_rows, D_MODEL), f32),
        grid=(m_rows // tm, D_FF // tf),
        in_specs=[
            pl.BlockSpec((tm, D_MODEL), lambda i, f: (i, 0)),
            pl.BlockSpec((1, 6, D_MODEL), lambda i, f: (i // per_b, 0, 0)),
            pl.BlockSpec((1, D_MODEL), lambda i, f: (0, 0)),
            pl.BlockSpec((D_MODEL, tf), lambda i, f: (0, f)),
            pl.BlockSpec((tf, D_MODEL), lambda i, f: (f, 0)),
        ],
        out_specs=pl.BlockSpec((tm, D_MODEL), lambda i, f: (i, 0)),
        scratch_shapes=[pltpu.VMEM((tm, D_MODEL), bf16), pltpu.VMEM((tm, D_MODEL), f32)],
        compiler_params=_params("arbitrary", "arbitrary"),
        name="mlp",
    )(x2d, mod3, norm_g, wu_bf16, wd_bf16)


def _layer(x, c, w_ada, b_ada, norm1_g, w_in, b_forget, fox_q_norm, fox_k_norm, nsa_q_norm,
           cmp_k_norm, slc_k_norm, win_k_norm, cmp_pe_k, cmp_w1_k, cmp_w2_k, cmp_pe_v, cmp_w1_v,
           cmp_w2_v, w_out, norm2_g, w_up, w_down):
    n_batch, seq_len, _ = x.shape
    n_cmp = (seq_len - CMP_LEN) // CMP_STRIDE + 1
    row = lambda v: v.reshape(1, -1)

    z0 = COL_NQ
    nq0 = z0 + FOX_HEADS
    gz0 = nq0 + NSA_W + 6 * KV_W
    w_perm = jnp.concatenate([
        w_in[:, :z0], w_in[:, nq0:gz0], w_in[:, z0:nq0], w_in[:, gz0:],
        jnp.zeros((D_MODEL, LANES - FOX_HEADS - N_BRANCH * NSA_HEADS), w_in.dtype)], axis=1).astype(bf16)
    half = HEAD_DIM // 2
    inv_freq = ROPE_THETA ** (-jnp.arange(half, dtype=f32) / half)
    inv_freq = jnp.concatenate([inv_freq, inv_freq]).reshape(1, HEAD_DIM)

    mod3 = _ada(c, w_ada, b_ada).reshape(n_batch, 6, D_MODEL)
    x2d = x.reshape(n_batch * seq_len, D_MODEL)
    proj = _proj(x2d, mod3, row(norm1_g), w_perm, seq_len)
    (fq, fk, fv, cum, nqu, nqr, kc, vc, kaug, vsl, kw, vw, gates) = _prep(
        proj, n_batch, seq_len, row(fox_q_norm), row(fox_k_norm), row(nsa_q_norm), row(slc_k_norm),
        row(win_k_norm), b_forget.reshape(FOX_HEADS, 1), inv_freq)
    kcmp = _compress(kc, cmp_pe_k, cmp_w1_k.astype(bf16), cmp_w2_k.astype(bf16), row(cmp_k_norm), True)
    vcmp = _compress(vc, cmp_pe_v, cmp_w1_v.astype(bf16), cmp_w2_v.astype(bf16), row(cmp_k_norm), False)
    ofox = _fox(fq, fk, fv, cum)
    ocmp, selneg = _cmp(nqu, kcmp, vcmp, gates, n_cmp)
    oslc = _slc(nqr, selneg, kaug, vsl, gates)
    owin = _win(nqr, kw, vw, gates)
    x1 = _outproj(ofox.reshape(-1, FOX_W), ocmp.reshape(-1, NSA_W), oslc.reshape(-1, NSA_W),
                  owin.reshape(-1, NSA_W), x2d, mod3, w_out.astype(bf16), seq_len)
    x2 = _mlp(x1, mod3, row(norm2_g), w_up.astype(bf16), w_down.astype(bf16), seq_len)
    return x2.reshape(n_batch, seq_len, D_MODEL)


def kernel(x, c, w_ada, b_ada, norm1_g, w_in, b_forget, fox_q_norm, fox_k_norm, nsa_q_norm, cmp_k_norm,
           slc_k_norm, win_k_norm, cmp_pe_k, cmp_w1_k, cmp_w2_k, cmp_pe_v, cmp_w1_v, cmp_w2_v, w_out,
           norm2_g, w_up, w_down):
    depth = w_ada.shape[0]
    for l in range(depth):
        x = _layer(x, c, w_ada[l], b_ada[l], norm1_g[l], w_in[l], b_forget[l], fox_q_norm[l], fox_k_norm[l],
                   nsa_q_norm[l], cmp_k_norm[l], slc_k_norm[l], win_k_norm[l], cmp_pe_k[l], cmp_w1_k[l],
                   cmp_w2_k[l], cmp_pe_v[l], cmp_w1_v[l], cmp_w2_v[l], w_out[l], norm2_g[l], w_up[l],
                   w_down[l])
    return x
```

```python
import functools
import math

import numpy as np
import jax
import jax.numpy as jnp
from jax import lax
from jax.experimental import pallas as pl
from jax.experimental.pallas import tpu as pltpu

D_MODEL = 2048
HEAD_DIM = 128
FOX_HEADS = 8
NSA_HEADS = 8
NSA_KV_HEADS = 2
NSA_GROUP = NSA_HEADS // NSA_KV_HEADS
N_BRANCH = 3
D_FF = 4 * D_MODEL
ROPE_THETA = 10000.0
CMP_LEN = 32
CMP_STRIDE = 16
CMP_HIDDEN = 2 * HEAD_DIM
SEL_BLOCK = 64
SEL_SHIFT = 6
SEL_TOPK = 16
WINDOW = 512
NORM_EPS = 1e-6
ATTN_SCALE = HEAD_DIM ** -0.5
FOX_W = FOX_HEADS * HEAD_DIM
NSA_W = NSA_HEADS * HEAD_DIM
KV_W = NSA_KV_HEADS * HEAD_DIM

LANES = 128
SEL_LANES = LANES
MASK_BIG = 1e30
KV_BLOCK = 512
LOG2E = math.log2(math.e)

COL_FQ = 0
COL_FK = COL_FQ + FOX_W
COL_FV = COL_FK + FOX_W
COL_NQ = COL_FV + FOX_W
COL_KC = COL_NQ + NSA_W
COL_VC = COL_KC + KV_W
COL_KS = COL_VC + KV_W
COL_VS = COL_KS + KV_W
COL_KW = COL_VS + KV_W
COL_VW = COL_KW + KV_W
COL_SMALL = COL_VW + KV_W
D_IN_PAD = COL_SMALL + LANES

VMEM_LIMIT = 56 * 1024 * 1024

f32 = jnp.float32
bf16 = jnp.bfloat16


def _params(*sem):
    return pltpu.CompilerParams(dimension_semantics=sem, vmem_limit_bytes=VMEM_LIMIT)


def _dot_nt(a, b):
    return lax.dot_general(a, b, (((1,), (1,)), ((), ())), preferred_element_type=f32)


def _dot(a, b):
    return jnp.dot(a, b, preferred_element_type=f32)


def _split3(x):
    hi = x.astype(bf16)
    r1 = x - hi.astype(f32)
    mid = r1.astype(bf16)
    lo = (r1 - mid.astype(f32)).astype(bf16)
    return hi, mid, lo


def _dot_f32_by_exact(x, w_bf16):
    hi, mid, lo = _split3(x)
    return _dot(hi, w_bf16) + (_dot(mid, w_bf16) + _dot(lo, w_bf16))


def _rms(x, gain):
    ms = jnp.mean(x * x, axis=-1, keepdims=True)
    return x * lax.rsqrt(ms + NORM_EPS) * gain


def _ada_kernel(ct_ref, w_ref, b_ref, o_ref, *, n_batch, k_chunk):
    ct = ct_ref[...]
    act = ct * jax.nn.sigmoid(ct)
    rows = []
    for b in range(n_batch):
        col = act[:, b:b + 1]
        acc = b_ref[...]
        for k0 in range(0, D_MODEL, k_chunk):
            acc = acc + jnp.sum(w_ref[k0:k0 + k_chunk, :] * col[k0:k0 + k_chunk], axis=0, keepdims=True)
        rows.append(acc)
    o_ref[...] = jnp.concatenate(rows, axis=0)


def _ada(c, w_ada, b_ada):
    n_batch = c.shape[0]
    n_out = w_ada.shape[1]
    tn = 1024
    return pl.pallas_call(
        functools.partial(_ada_kernel, n_batch=n_batch, k_chunk=256),
        out_shape=jax.ShapeDtypeStruct((n_batch, n_out), f32),
        grid=(n_out // tn,),
        in_specs=[
            pl.BlockSpec((D_MODEL, n_batch), lambda j: (0, 0)),
            pl.BlockSpec((D_MODEL, tn), lambda j: (0, j)),
            pl.BlockSpec((1, tn), lambda j: (0, j)),
        ],
        out_specs=pl.BlockSpec((n_batch, tn), lambda j: (0, j)),
        compiler_params=_params("arbitrary"),
        name="ada",
    )(c.T, w_ada, b_ada.reshape(1, n_out))


def _proj_kernel(x_ref, mod_ref, g_ref, w_ref, o_ref, h_ref):
    @pl.when(pl.program_id(1) == 0)
    def _():
        md = mod_ref[0]
        y = _rms(x_ref[...], g_ref[...])
        h_ref[...] = (y * (1.0 + md[1:2]) + md[0:1]).astype(bf16)

    o_ref[...] = _dot(h_ref[...], w_ref[...])


def _proj(x2d, mod3, norm_g, w_bf16, seq_len):
    m_rows = x2d.shape[0]
    n_cols = w_bf16.shape[1]
    tm, tn = 512, 1152
    assert seq_len % tm == 0 and n_cols % tn == 0
    per_b = seq_len // tm
    return pl.pallas_call(
        _proj_kernel,
        out_shape=jax.ShapeDtypeStruct((m_rows, n_cols), f32),
        grid=(m_rows // tm, n_cols // tn),
        in_specs=[
            pl.BlockSpec((tm, D_MODEL), lambda i, j: (i, 0)),
            pl.BlockSpec((1, 6, D_MODEL), lambda i, j: (i // per_b, 0, 0)),
            pl.BlockSpec((1, D_MODEL), lambda i, j: (0, 0)),
            pl.BlockSpec((D_MODEL, tn), lambda i, j: (0, j)),
        ],
        out_specs=pl.BlockSpec((tm, tn), lambda i, j: (i, j)),
        scratch_shapes=[pltpu.VMEM((tm, D_MODEL), bf16)],
        compiler_params=_params("arbitrary", "arbitrary"),
        name="proj",
    )(x2d, mod3, norm_g, w_bf16)


def _prep_kernel(p_ref, gq_ref, gk_ref, gn_ref, gs_ref, gw_ref, bf_ref, inv_ref,
                 fq_ref, fk_ref, fvt_ref, cin_ref, nqu_ref, nqr_ref, kc_ref, vc_ref,
                 ks_ref, vst_ref, kw_ref, vw_ref, gate_ref, carry_ref, *, tm):
    i = pl.program_id(1)

    @pl.when(i == 0)
    def _():
        carry_ref[...] = jnp.zeros_like(carry_ref)

    def head(col, h):
        return p_ref[:, col + h * HEAD_DIM: col + (h + 1) * HEAD_DIM]

    row = lax.broadcasted_iota(jnp.int32, (tm, LANES), 0)
    lane = lax.broadcasted_iota(jnp.int32, (tm, LANES), 1)
    pos = i * tm + row
    ang = pos.astype(f32) * inv_ref[...]
    cos = jnp.cos(ang)
    sin = jnp.sin(ang)
    sin_signed = jnp.where(lane < HEAD_DIM // 2, -sin, sin)

    def rope(x):
        return x * cos + pltpu.roll(x, HEAD_DIM // 2, 1) * sin_signed

    small = p_ref[:, COL_SMALL:COL_SMALL + LANES]
    z = small + bf_ref[...]
    logf = jnp.minimum(z, 0.0) - jnp.log1p(jnp.exp(-jnp.abs(z)))
    t_idx = lax.broadcasted_iota(jnp.int32, (tm, tm), 0)
    s_idx = lax.broadcasted_iota(jnp.int32, (tm, tm), 1)
    tri = jnp.where(s_idx <= t_idx, 1.0, 0.0).astype(bf16)
    hi, mid, lo = _split3(logf)
    local = _dot(tri, hi) + (_dot(tri, mid) + _dot(tri, lo))
    cin_ref[0, 0] = carry_ref[0:1, :]
    carry_ref[...] = carry_ref[...] + local[tm - 1:tm, :]
    b_hi, b_mid, b_lo = (v.astype(f32) for v in _split3(local * (-LOG2E)))
    ones3 = jnp.where(lane < 3, 1.0, 0.0).astype(bf16)

    for h in range(FOX_HEADS):
        q = _rms(head(COL_FQ, h), gq_ref[...]) * (ATTN_SCALE * LOG2E)
        fq_ref[0, h] = jnp.concatenate([q.astype(bf16), ones3], axis=1)
        k = _rms(head(COL_FK, h), gk_ref[...])
        bias = jnp.where(lane == 0, b_hi[:, h:h + 1],
                         jnp.where(lane == 1, b_mid[:, h:h + 1],
                                   jnp.where(lane == 2, b_lo[:, h:h + 1], 0.0)))
        fk_ref[0, h] = jnp.concatenate([k.astype(bf16), bias.astype(bf16)], axis=1)
        fvt_ref[0, h, 0] = head(COL_FV, h).T.astype(bf16)

    gate_ref[0] = jax.nn.sigmoid(small)

    for h in range(NSA_HEADS):
        qn = _rms(head(COL_NQ, h), gn_ref[...])
        nqu_ref[0, h] = (qn * ATTN_SCALE).astype(bf16)
        nqr_ref[0, h] = rope(qn * (ATTN_SCALE * LOG2E)).astype(bf16)
    onehot = jnp.where(lane == lax.shift_right_logical(pos, SEL_SHIFT), 1.0, 0.0).astype(bf16)
    for g in range(NSA_KV_HEADS):
        kc_ref[0, g] = head(COL_KC, g)
        vc_ref[0, g] = head(COL_VC, g)
        ks = rope(_rms(head(COL_KS, g), gs_ref[...])).astype(bf16)
        ks_ref[0, g] = jnp.concatenate([ks, onehot], axis=1)
        vst_ref[0, g, 0] = head(COL_VS, g).T.astype(bf16)
        kw_ref[0, g] = rope(_rms(head(COL_KW, g), gw_ref[...])).astype(bf16)
        vw_ref[0, g] = head(COL_VW, g).astype(bf16)


def _prep(proj, n_batch, seq_len, gq, gk, gn, gs, gw, b_forget_row, inv_freq):
    tm = KV_BLOCK
    per_b = seq_len // tm
    hshape = lambda n, w, dt: jax.ShapeDtypeStruct((n_batch, n, seq_len, w), dt)
    hspec = lambda n, w: pl.BlockSpec((1, n, tm, w), lambda b, i: (b, 0, i, 0))
    tshape = lambda n: jax.ShapeDtypeStruct((n_batch, n, per_b, HEAD_DIM, tm), bf16)
    tspec = lambda n: pl.BlockSpec((1, n, 1, HEAD_DIM, tm), lambda b, i: (b, 0, i, 0, 0))
    vec = pl.BlockSpec((1, LANES), lambda b, i: (0, 0))
    return pl.pallas_call(
        functools.partial(_prep_kernel, tm=tm),
        out_shape=[
            hshape(FOX_HEADS, 2 * HEAD_DIM, bf16), hshape(FOX_HEADS, 2 * HEAD_DIM, bf16), tshape(FOX_HEADS),
            jax.ShapeDtypeStruct((n_batch, per_b, 1, LANES), f32),
            hshape(NSA_HEADS, HEAD_DIM, bf16), hshape(NSA_HEADS, HEAD_DIM, bf16),
            hshape(NSA_KV_HEADS, HEAD_DIM, f32), hshape(NSA_KV_HEADS, HEAD_DIM, f32),
            hshape(NSA_KV_HEADS, HEAD_DIM + SEL_LANES, bf16), tshape(NSA_KV_HEADS),
            hshape(NSA_KV_HEADS, HEAD_DIM, bf16), hshape(NSA_KV_HEADS, HEAD_DIM, bf16),
            jax.ShapeDtypeStruct((n_batch, seq_len, LANES), f32),
        ],
        grid=(n_batch, per_b),
        in_specs=[
            pl.BlockSpec((tm, D_IN_PAD), lambda b, i: (b * per_b + i, 0)),
            vec, vec, vec, vec, vec, vec, vec,
        ],
        out_specs=[
            hspec(FOX_HEADS, 2 * HEAD_DIM), hspec(FOX_HEADS, 2 * HEAD_DIM), tspec(FOX_HEADS),
            pl.BlockSpec((1, 1, 1, LANES), lambda b, i: (b, i, 0, 0)),
            hspec(NSA_HEADS, HEAD_DIM), hspec(NSA_HEADS, HEAD_DIM),
            hspec(NSA_KV_HEADS, HEAD_DIM), hspec(NSA_KV_HEADS, HEAD_DIM),
            hspec(NSA_KV_HEADS, HEAD_DIM + SEL_LANES), tspec(NSA_KV_HEADS),
            hspec(NSA_KV_HEADS, HEAD_DIM), hspec(NSA_KV_HEADS, HEAD_DIM),
            pl.BlockSpec((1, tm, LANES), lambda b, i: (b, i, 0)),
        ],
        scratch_shapes=[pltpu.VMEM((8, LANES), f32)],
        compiler_params=_params("arbitrary", "arbitrary"),
        name="prep",
    )(proj, gq, gk, gn, gs, gw, b_forget_row, inv_freq)


def _compress_kernel(x_ref, pe_ref, w1_ref, w2_ref, g_ref, o_ref, *, n_rows, do_norm):
    half = CMP_STRIDE * HEAD_DIM
    x = x_ref[0, 0]
    pe = pe_ref[...]
    xa = (x + pe[:, :half]).astype(bf16)
    xb = (x + pe[:, half:]).astype(bf16)
    a = _dot(xa, w1_ref[0:half, :])
    b = _dot(xb, w1_ref[half:2 * half, :])
    pre = a + pltpu.roll(b, n_rows - 1, 0)
    hid = pre * jax.nn.sigmoid(pre)
    out = _dot(hid.astype(bf16), w2_ref[...])
    if do_norm:
        out = _rms(out, g_ref[...])
    o_ref[0, 0] = out.astype(bf16)


def _compress(x4, pe, w1_bf16, w2_bf16, gain, do_norm):
    n_batch, n_g, seq_len, _ = x4.shape
    n_rows = seq_len // CMP_STRIDE
    half = CMP_STRIDE * HEAD_DIM
    xr = x4.reshape(n_batch, n_g, n_rows, half)
    return pl.pallas_call(
        functools.partial(_compress_kernel, n_rows=n_rows, do_norm=do_norm),
        out_shape=jax.ShapeDtypeStruct((n_batch, n_g, n_rows, HEAD_DIM), bf16),
        grid=(n_batch, n_g),
        in_specs=[
            pl.BlockSpec((1, 1, n_rows, half), lambda b, g: (b, g, 0, 0)),
            pl.BlockSpec((1, 2 * half), lambda b, g: (0, 0)),
            pl.BlockSpec((2 * half, CMP_HIDDEN), lambda b, g: (0, 0)),
            pl.BlockSpec((CMP_HIDDEN, HEAD_DIM), lambda b, g: (0, 0)),
            pl.BlockSpec((1, HEAD_DIM), lambda b, g: (0, 0)),
        ],
        out_specs=pl.BlockSpec((1, 1, n_rows, HEAD_DIM), lambda b, g: (b, g, 0, 0)),
        compiler_params=_params("arbitrary", "arbitrary"),
        name="compress",
    )(xr, pe.reshape(1, 2 * half), w1_bf16, w2_bf16, gain)


def _softmax_block(s, d, m_prev, l_prev):
    m_blk = jnp.max(s, axis=0, keepdims=True)
    m_new = jnp.maximum(m_prev, m_blk if d is None else m_blk + d)
    p = jnp.exp2(s - (m_new if d is None else m_new - d))
    alpha = jnp.exp2(m_prev - m_new)
    l_new = alpha * l_prev + jnp.sum(p, axis=0, keepdims=True)
    return m_new, l_new, alpha, p.astype(bf16)


def _flash_sweep(n_full, scores, values, offset, diag_mask, emit, s_buf, p_buf, n_q):
    s_buf[0] = scores(0)
    p_buf[1] = jnp.zeros(p_buf.shape[1:], p_buf.dtype)

    def accumulate(j, slot, alpha, acc):
        return alpha * acc + _dot(values(jnp.maximum(j, 0)), p_buf[slot])

    def step(j, cur, state):
        m, l, acc, alpha_prev = state
        acc = accumulate(j - 1, 1 - cur, alpha_prev, acc)
        m, l, alpha, p = _softmax_block(s_buf[cur], offset(j), m, l)
        p_buf[cur] = p
        s_buf[1 - cur] = scores(j + 1)
        return m, l, acc, alpha

    def finish(cur, state):
        m, l, acc, alpha_prev = state
        acc = accumulate(n_full - 1, 1 - cur, alpha_prev, acc)
        m, l, alpha, p = _softmax_block(diag_mask(s_buf[cur]), None, m, l)
        acc = alpha * acc + _dot(values(n_full), p)
        emit(acc / l)

    init = (jnp.full((1, n_q), -jnp.inf, f32), jnp.zeros((1, n_q), f32),
            jnp.zeros((HEAD_DIM, n_q), f32), jnp.ones((1, n_q), f32))
    state = lax.fori_loop(0, n_full // 2, lambda jj, st: step(2 * jj + 1, 1, step(2 * jj, 0, st)), init)

    @pl.when(n_full % 2 == 0)
    def _():
        finish(0, state)

    @pl.when(n_full % 2 == 1)
    def _():
        finish(1, step(n_full - 1, 0, state))


def _fox_kernel(q_ref, k_ref, vt_ref, c_ref, o_ref, s_buf, p_buf, *, tq):
    h = pl.program_id(1)
    i = pl.program_id(2)
    qa = q_ref[0, 0]
    lane = lax.broadcasted_iota(jnp.int32, (1, LANES), 1)

    def cin(j):
        return jnp.sum(jnp.where(lane == h, c_ref[0, j], 0.0), axis=-1, keepdims=True)

    ci = cin(i)

    def scores(j):
        return _dot_nt(k_ref[0, 0, pl.ds(pl.multiple_of(j * tq, tq), tq), :], qa)

    def diag_mask(s):
        kk = lax.broadcasted_iota(jnp.int32, (tq, tq), 0)
        qq = lax.broadcasted_iota(jnp.int32, (tq, tq), 1)
        return jnp.where(kk <= qq, s, -jnp.inf)

    def emit(out_t):
        o_ref[0] = out_t.T

    _flash_sweep(i, scores, lambda j: vt_ref[0, 0, j], lambda j: (ci - cin(j)) * LOG2E,
                 diag_mask, emit, s_buf, p_buf, tq)


def _fox(fq, fk, fvt, cin):
    n_batch, n_h, seq_len, _ = fq.shape
    tq = KV_BLOCK
    nkb = seq_len // tq
    return pl.pallas_call(
        functools.partial(_fox_kernel, tq=tq),
        out_shape=jax.ShapeDtypeStruct((n_batch, seq_len, n_h * HEAD_DIM), f32),
        grid=(n_batch, n_h, nkb),
        in_specs=[
            pl.BlockSpec((1, 1, tq, 2 * HEAD_DIM), lambda b, h, i: (b, h, i, 0)),
            pl.BlockSpec((1, 1, seq_len, 2 * HEAD_DIM), lambda b, h, i: (b, h, 0, 0)),
            pl.BlockSpec((1, 1, nkb, HEAD_DIM, tq), lambda b, h, i: (b, h, 0, 0, 0)),
            pl.BlockSpec((1, nkb, 1, LANES), lambda b, h, i: (b, 0, 0, 0)),
        ],
        out_specs=pl.BlockSpec((1, tq, HEAD_DIM), lambda b, h, i: (b, i, h)),
        scratch_shapes=[pltpu.VMEM((2, tq, tq), f32), pltpu.VMEM((2, tq, tq), bf16)],
        compiler_params=_params("arbitrary", "arbitrary", "arbitrary"),
        name="fox",
    )(fq, fk, fvt, cin)


def _cmp_kernel(q_ref, k_ref, v_ref, gate_ref, ov_ref, o_ref, sel_ref, *, tq, n_rows, n_cmp, n_sel):
    i = pl.program_id(2)
    g = pl.program_id(1)
    hg = NSA_GROUP
    q = q_ref[0].reshape(hg * tq, HEAD_DIM)
    s = _dot_nt(q, k_ref[0, 0]).reshape(hg, tq, n_rows)
    t = i * tq + lax.broadcasted_iota(jnp.int32, (1, tq, 1), 1)
    n = lax.broadcasted_iota(jnp.int32, (1, 1, n_rows), 2)
    valid = (n * CMP_STRIDE + (CMP_LEN - 1) <= t) & (n < n_cmp)
    s = jnp.where(valid, s, -jnp.inf)
    m = jnp.max(s, axis=-1, keepdims=True)
    m = jnp.where(m == -jnp.inf, 0.0, m)
    e = jnp.exp(s - m)
    p = e / jnp.maximum(jnp.sum(e, axis=-1, keepdims=True), 1e-30)
    o = _dot(p.reshape(hg * tq, n_rows).astype(bf16), v_ref[0, 0])
    _write_gated(o_ref, gate_ref, o, g, 0, tq)

    lane = lax.broadcasted_iota(jnp.int32, (tq, LANES), 1)
    psum = p[0] + p[1] + p[2] + p[3]
    imp = _dot_f32_by_exact(psum, ov_ref[...])
    tq_pos = i * tq + lax.broadcasted_iota(jnp.int32, (tq, 1), 0)
    q_blk = lax.shift_right_logical(tq_pos, SEL_SHIFT)
    causal = lane <= q_blk
    forced = (lane == 0) | (lane == q_blk) | (lane == q_blk - 1)
    key = jnp.where(forced, jnp.inf, jnp.where(causal, imp, -1.0))
    key = jnp.where(lane < n_sel, key, -3.0)
    lane_f = lane.astype(f32)
    sel = jnp.zeros((tq, LANES), jnp.bool_)
    for _ in range(min(SEL_TOPK, n_sel)):
        mx = jnp.max(key, axis=-1, keepdims=True)
        first = jnp.min(jnp.where(key == mx, lane_f, float(LANES)), axis=-1, keepdims=True)
        pick = lane_f == first
        sel = sel | pick
        key = jnp.where(pick, -2.0, key)
    sel_ref[0, 0] = jnp.where(sel & causal, 0.0, -MASK_BIG).astype(bf16)


def _cmp(nqu, kcmp, vcmp, gates, n_cmp):
    n_batch, _, seq_len, _ = nqu.shape
    n_rows = kcmp.shape[2]
    n_sel = seq_len // SEL_BLOCK
    assert n_sel <= SEL_LANES
    tq = 1024
    c0 = np.arange(n_rows) * CMP_STRIDE
    s0 = np.arange(SEL_LANES) * SEL_BLOCK
    overlap = np.clip(np.minimum(c0[:, None] + CMP_LEN, s0[None, :] + SEL_BLOCK)
                      - np.maximum(c0[:, None], s0[None, :]), 0, None).astype(np.float32) / CMP_LEN
    overlap[n_cmp:, :] = 0.0
    overlap[:, n_sel:] = 0.0
    return pl.pallas_call(
        functools.partial(_cmp_kernel, tq=tq, n_rows=n_rows, n_cmp=n_cmp, n_sel=n_sel),
        out_shape=[
            jax.ShapeDtypeStruct((n_batch, seq_len, NSA_W), f32),
            jax.ShapeDtypeStruct((n_batch, NSA_KV_HEADS, seq_len, SEL_LANES), bf16),
        ],
        grid=(n_batch, NSA_KV_HEADS, seq_len // tq),
        in_specs=[
            pl.BlockSpec((1, NSA_GROUP, tq, HEAD_DIM), lambda b, g, i: (b, g, i, 0)),
            pl.BlockSpec((1, 1, n_rows, HEAD_DIM), lambda b, g, i: (b, g, 0, 0)),
            pl.BlockSpec((1, 1, n_rows, HEAD_DIM), lambda b, g, i: (b, g, 0, 0)),
            pl.BlockSpec((1, tq, LANES), lambda b, g, i: (b, i, 0)),
            pl.BlockSpec((n_rows, SEL_LANES), lambda b, g, i: (0, 0)),
        ],
        out_specs=[
            pl.BlockSpec((1, tq, NSA_GROUP * HEAD_DIM), lambda b, g, i: (b, i, g)),
            pl.BlockSpec((1, 1, tq, SEL_LANES), lambda b, g, i: (b, g, i, 0)),
        ],
        compiler_params=_params("arbitrary", "arbitrary", "arbitrary"),
        name="cmp",
    )(nqu, kcmp, vcmp, gates, jnp.asarray(overlap, dtype=bf16))


def _write_gated(o_ref, gate_ref, out, g, branch, tq):
    gates = gate_ref[0]
    lane = lax.broadcasted_iota(jnp.int32, (tq, LANES), 1)
    for h in range(NSA_GROUP):
        gcol = FOX_HEADS + (g * NSA_GROUP + h) * N_BRANCH + branch
        gh = jnp.sum(jnp.where(lane == gcol, gates, 0.0), axis=-1, keepdims=True)
        o_ref[0, :, h * HEAD_DIM:(h + 1) * HEAD_DIM] = gh * out[h * tq:(h + 1) * tq]


def _slc_kernel(q_ref, sel_ref, k_ref, vt_ref, gate_ref, o_ref, s_buf, p_buf, *, tq, tk):
    i = pl.program_id(2)
    g = pl.program_id(1)
    hg = NSA_GROUP
    n_q = hg * tq
    q = q_ref[0].reshape(n_q, HEAD_DIM)
    sel = sel_ref[0, 0]
    qa = jnp.concatenate([q, jnp.concatenate([sel] * hg, axis=0)], axis=1)
    q0 = i * tq
    jd = q0 // tk

    def scores(j):
        return _dot_nt(k_ref[0, 0, pl.ds(pl.multiple_of(j * tk, tk), tk), :], qa)

    def diag_mask(s):
        kp = jd * tk + lax.broadcasted_iota(jnp.int32, (tk, n_q), 0)
        t = q0 + (lax.broadcasted_iota(jnp.int32, (tk, n_q), 1) & (tq - 1))
        return jnp.where(kp <= t, s, -jnp.inf)

    def emit(out):
        gates = gate_ref[0]
        lane = lax.broadcasted_iota(jnp.int32, (tq, LANES), 1)
        for h in range(hg):
            gcol = FOX_HEADS + (g * hg + h) * N_BRANCH + 1
            gh = jnp.sum(jnp.where(lane == gcol, gates, 0.0), axis=-1, keepdims=True)
            o_ref[0, :, h * HEAD_DIM:(h + 1) * HEAD_DIM] = gh * out[:, h * tq:(h + 1) * tq].T

    _flash_sweep(jd, scores, lambda j: vt_ref[0, 0, j], lambda j: None, diag_mask, emit, s_buf, p_buf, n_q)


def _slc(nqr, selneg, kaug, vst, gates):
    n_batch, _, seq_len, _ = nqr.shape
    tq, tk = 128, KV_BLOCK
    nkb = seq_len // tk
    assert tq & (tq - 1) == 0
    return pl.pallas_call(
        functools.partial(_slc_kernel, tq=tq, tk=tk),
        out_shape=jax.ShapeDtypeStruct((n_batch, seq_len, NSA_W), f32),
        grid=(n_batch, NSA_KV_HEADS, seq_len // tq),
        in_specs=[
            pl.BlockSpec((1, NSA_GROUP, tq, HEAD_DIM), lambda b, g, i: (b, g, i, 0)),
            pl.BlockSpec((1, 1, tq, SEL_LANES), lambda b, g, i: (b, g, i, 0)),
            pl.BlockSpec((1, 1, seq_len, HEAD_DIM + SEL_LANES), lambda b, g, i: (b, g, 0, 0)),
            pl.BlockSpec((1, 1, nkb, HEAD_DIM, tk), lambda b, g, i: (b, g, 0, 0, 0)),
            pl.BlockSpec((1, tq, LANES), lambda b, g, i: (b, i, 0)),
        ],
        out_specs=pl.BlockSpec((1, tq, NSA_GROUP * HEAD_DIM), lambda b, g, i: (b, i, g)),
        scratch_shapes=[pltpu.VMEM((2, tk, NSA_GROUP * tq), f32), pltpu.VMEM((2, tk, NSA_GROUP * tq), bf16)],
        compiler_params=_params("arbitrary", "arbitrary", "arbitrary"),
        name="slc",
    )(nqr, selneg, kaug, vst, gates)


def _win_kernel(q_ref, k_ref, v_ref, gate_ref, o_ref, *, tq, span):
    i = pl.program_id(2)
    g = pl.program_id(1)
    hg = NSA_GROUP
    q = q_ref[0].reshape(hg * tq, HEAD_DIM)
    q0 = i * tq
    k0 = pl.multiple_of(jnp.maximum(q0 - WINDOW, 0), tq)
    k = k_ref[0, 0, pl.ds(k0, span), :]
    v = v_ref[0, 0, pl.ds(k0, span), :]
    s = _dot_nt(q, k).reshape(hg, tq, span)
    t = q0 + lax.broadcasted_iota(jnp.int32, (1, tq, span), 1)
    kp = k0 + lax.broadcasted_iota(jnp.int32, (1, tq, span), 2)
    diff = t - kp
    s = jnp.where((diff >= 0) & (diff < WINDOW), s, -jnp.inf)
    m = jnp.max(s, axis=-1, keepdims=True)
    e = jnp.exp2(s - m)
    l = jnp.sum(e, axis=-1, keepdims=True)
    o = _dot(e.reshape(hg * tq, span).astype(bf16), v) / l.reshape(hg * tq, 1)
    _write_gated(o_ref, gate_ref, o, g, 2, tq)


def _win(nqr, kw, vw, gates):
    n_batch, _, seq_len, _ = nqr.shape
    tq = 128
    span = WINDOW + tq
    assert seq_len >= span
    return pl.pallas_call(
        functools.partial(_win_kernel, tq=tq, span=span),
        out_shape=jax.ShapeDtypeStruct((n_batch, seq_len, NSA_W), f32),
        grid=(n_batch, NSA_KV_HEADS, seq_len // tq),
        in_specs=[
            pl.BlockSpec((1, NSA_GROUP, tq, HEAD_DIM), lambda b, g, i: (b, g, i, 0)),
            pl.BlockSpec((1, 1, seq_len, HEAD_DIM), lambda b, g, i: (b, g, 0, 0)),
            pl.BlockSpec((1, 1, seq_len, HEAD_DIM), lambda b, g, i: (b, g, 0, 0)),
            pl.BlockSpec((1, tq, LANES), lambda b, g, i: (b, i, 0)),
        ],
        out_specs=pl.BlockSpec((1, tq, NSA_GROUP * HEAD_DIM), lambda b, g, i: (b, i, g)),
        compiler_params=_params("arbitrary", "arbitrary", "arbitrary"),
        name="win",
    )(nqr, kw, vw, gates)


def _outproj_kernel(fox_ref, c_ref, s_ref, w_ref, x_ref, mod_ref, wo_ref, o_ref, a_ref):
    @pl.when(pl.program_id(1) == 0)
    def _():
        a_ref[:, 0:FOX_W] = fox_ref[...].astype(bf16)
        a_ref[:, FOX_W:FOX_W + NSA_W] = (c_ref[...] + s_ref[...] + w_ref[...]).astype(bf16)

    gate1 = mod_ref[0][2:3]
    o_ref[...] = x_ref[...] + gate1 * _dot(a_ref[...], wo_ref[...])


def _outproj(ofox, ocmp, oslc, owin, x2d, mod3, wo_bf16, seq_len):
    m_rows = x2d.shape[0]
    tm, tn = 512, 1024
    per_b = seq_len // tm
    half = pl.BlockSpec((tm, FOX_W), lambda i, j: (i, 0))
    return pl.pallas_call(
        _outproj_kernel,
        out_shape=jax.ShapeDtypeStruct((m_rows, D_MODEL), f32),
        grid=(m_rows // tm, D_MODEL // tn),
        in_specs=[
            half, half, half, half,
            pl.BlockSpec((tm, tn), lambda i, j: (i, j)),
            pl.BlockSpec((1, 6, tn), lambda i, j: (i // per_b, 0, j)),
            pl.BlockSpec((D_MODEL, tn), lambda i, j: (0, j)),
        ],
        out_specs=pl.BlockSpec((tm, tn), lambda i, j: (i, j)),
        scratch_shapes=[pltpu.VMEM((tm, D_MODEL), bf16)],
        compiler_params=_params("arbitrary", "arbitrary"),
        name="outproj",
    )(ofox, ocmp, oslc, owin, x2d, mod3, wo_bf16)


def _mlp_kernel(x_ref, mod_ref, g_ref, wu_ref, wd_ref, o_ref, h_ref, acc_ref):
    f = pl.program_id(1)

    @pl.when(f == 0)
    def _():
        md = mod_ref[0]
        y = _rms(x_ref[...], g_ref[...])
        h_ref[...] = (y * (1.0 + md[4:5]) + md[3:4]).astype(bf16)
        acc_ref[...] = jnp.zeros_like(acc_ref)

    u = jnp.maximum(_dot(h_ref[...], wu_ref[...]), 0.0)
    acc_ref[...] += _dot((u * u).astype(bf16), wd_ref[...])

    @pl.when(f == pl.num_programs(1) - 1)
    def _():
        o_ref[...] = x_ref[...] + mod_ref[0][5:6] * acc_ref[...]


def _mlp(x2d, mod3, norm_g, wu_bf16, wd_bf16, seq_len):
    m_rows = x2d.shape[0]
    tm, tf = 512, 512
    per_b = seq_len // tm
    return pl.pallas_call(
        _mlp_kernel,
        out_shape=jax.ShapeDtypeStruct((m_rows, D_MODEL), f32),
        grid=(m_rows // tm, D_FF // tf),
        in_specs=[
            pl.BlockSpec((tm, D_MODEL), lambda i, f: (i, 0)),
            pl.BlockSpec((1, 6, D_MODEL), lambda i, f: (i // per_b, 0, 0)),
            pl.BlockSpec((1, D_MODEL), lambda i, f: (0, 0)),
            pl.BlockSpec((D_MODEL, tf), lambda i, f: (0, f)),
            pl.BlockSpec((tf, D_MODEL), lambda i, f: (f, 0)),
        ],
        out_specs=pl.BlockSpec((tm, D_MODEL), lambda i, f: (i, 0)),
        scratch_shapes=[pltpu.VMEM((tm, D_MODEL), bf16), pltpu.VMEM((tm, D_MODEL), f32)],
        compiler_params=_params("arbitrary", "arbitrary"),
        name="mlp",
    )(x2d, mod3, norm_g, wu_bf16, wd_bf16)


def _layer(x, c, w_ada, b_ada, norm1_g, w_in, b_forget, fox_q_norm, fox_k_norm, nsa_q_norm,
           cmp_k_norm, slc_k_norm, win_k_norm, cmp_pe_k, cmp_w1_k, cmp_w2_k, cmp_pe_v, cmp_w1_v,
           cmp_w2_v, w_out, norm2_g, w_up, w_down):
    n_batch, seq_len, _ = x.shape
    n_cmp = (seq_len - CMP_LEN) // CMP_STRIDE + 1
    row = lambda v: v.reshape(1, -1)

    z0 = COL_NQ
    nq0 = z0 + FOX_HEADS
    gz0 = nq0 + NSA_W + 6 * KV_W
    w_perm = jnp.concatenate([
        w_in[:, :z0], w_in[:, nq0:gz0], w_in[:, z0:nq0], w_in[:, gz0:],
        jnp.zeros((D_MODEL, LANES - FOX_HEADS - N_BRANCH * NSA_HEADS), w_in.dtype)], axis=1).astype(bf16)
    half = HEAD_DIM // 2
    inv_freq = ROPE_THETA ** (-jnp.arange(half, dtype=f32) / half)
    inv_freq = jnp.concatenate([inv_freq, inv_freq]).reshape(1, HEAD_DIM)

    mod3 = _ada(c, w_ada, b_ada).reshape(n_batch, 6, D_MODEL)
    x2d = x.reshape(n_batch * seq_len, D_MODEL)
    proj = _proj(x2d, mod3, row(norm1_g), w_perm, seq_len)
    (fq, fk, fvt, cin, nqu, nqr, kc, vc, kaug, vst, kw, vw, gates) = _prep(
        proj, n_batch, seq_len, row(fox_q_norm), row(fox_k_norm), row(nsa_q_norm), row(slc_k_norm),
        row(win_k_norm), jnp.pad(b_forget, (0, LANES - FOX_HEADS)).reshape(1, LANES), inv_freq)
    kcmp = _compress(kc, cmp_pe_k, cmp_w1_k.astype(bf16), cmp_w2_k.astype(bf16), row(cmp_k_norm), True)
    vcmp = _compress(vc, cmp_pe_v, cmp_w1_v.astype(bf16), cmp_w2_v.astype(bf16), row(cmp_k_norm), False)
    ofox = _fox(fq, fk, fvt, cin)
    ocmp, selneg = _cmp(nqu, kcmp, vcmp, gates, n_cmp)
    oslc = _slc(nqr, selneg, kaug, vst, gates)
    owin = _win(nqr, kw, vw, gates)
    x1 = _outproj(ofox.reshape(-1, FOX_W), ocmp.reshape(-1, NSA_W), oslc.reshape(-1, NSA_W),
                  owin.reshape(-1, NSA_W), x2d, mod3, w_out.astype(bf16), seq_len)
    x2 = _mlp(x1, mod3, row(norm2_g), w_up.astype(bf16), w_down.astype(bf16), seq_len)
    return x2.reshape(n_batch, seq_len, D_MODEL)


def kernel(x, c, w_ada, b_ada, norm1_g, w_in, b_forget, fox_q_norm, fox_k_norm, nsa_q_norm, cmp_k_norm,
           slc_k_norm, win_k_norm, cmp_pe_k, cmp_w1_k, cmp_w2_k, cmp_pe_v, cmp_w1_v, cmp_w2_v, w_out,
           norm2_g, w_up, w_down):
    depth = w_ada.shape[0]
    for l in range(depth):
        x = _layer(x, c, w_ada[l], b_ada[l], norm1_g[l], w_in[l], b_forget[l], fox_q_norm[l], fox_k_norm[l],
                   nsa_q_norm[l], cmp_k_norm[l], slc_k_norm[l], win_k_norm[l], cmp_pe_k[l], cmp_w1_k[l],
                   cmp_w2_k[l], cmp_pe_v[l], cmp_w1_v[l], cmp_w2_v[l], w_out[l], norm2_g[l], w_up[l],
                   w_down[l])
    return x
```

```python
import functools
import math

import numpy as np
import jax
import jax.numpy as jnp
from jax import lax
from jax.experimental import pallas as pl
from jax.experimental.pallas import tpu as pltpu

D_MODEL = 2048
HEAD_DIM = 128
FOX_HEADS = 8
NSA_HEADS = 8
NSA_KV_HEADS = 2
NSA_GROUP = NSA_HEADS // NSA_KV_HEADS
N_BRANCH = 3
D_FF = 4 * D_MODEL
ROPE_THETA = 10000.0
CMP_LEN = 32
CMP_STRIDE = 16
CMP_HIDDEN = 2 * HEAD_DIM
SEL_BLOCK = 64
SEL_SHIFT = 6
SEL_TOPK = 16
WINDOW = 512
NORM_EPS = 1e-6
ATTN_SCALE = HEAD_DIM ** -0.5
FOX_W = FOX_HEADS * HEAD_DIM
NSA_W = NSA_HEADS * HEAD_DIM
KV_W = NSA_KV_HEADS * HEAD_DIM

LANES = 128
SEL_LANES = LANES
MASK_BIG = 1e30
KV_BLOCK = 512
WIN_TILE = 128
LOG2E = math.log2(math.e)

COL_FQ = 0
COL_FK = COL_FQ + FOX_W
COL_FV = COL_FK + FOX_W
COL_NQ = COL_FV + FOX_W
COL_KC = COL_NQ + NSA_W
COL_VC = COL_KC + KV_W
COL_KS = COL_VC + KV_W
COL_VS = COL_KS + KV_W
COL_KW = COL_VS + KV_W
COL_VW = COL_KW + KV_W
COL_SMALL = COL_VW + KV_W
D_IN_PAD = COL_SMALL + LANES

VMEM_LIMIT = 56 * 1024 * 1024

f32 = jnp.float32
bf16 = jnp.bfloat16


def _params(*sem):
    return pltpu.CompilerParams(dimension_semantics=sem, vmem_limit_bytes=VMEM_LIMIT)


def _dot_nt(a, b):
    return lax.dot_general(a, b, (((1,), (1,)), ((), ())), preferred_element_type=f32)


def _dot(a, b):
    return jnp.dot(a, b, preferred_element_type=f32)


def _split3(x):
    hi = x.astype(bf16)
    r1 = x - hi.astype(f32)
    mid = r1.astype(bf16)
    lo = (r1 - mid.astype(f32)).astype(bf16)
    return hi, mid, lo


def _dot_f32_by_exact(x, w_bf16):
    hi, mid, lo = _split3(x)
    return _dot(hi, w_bf16) + (_dot(mid, w_bf16) + _dot(lo, w_bf16))


def _rms(x, gain):
    ms = jnp.mean(x * x, axis=-1, keepdims=True)
    return x * lax.rsqrt(ms + NORM_EPS) * gain


def _ada_kernel(ct_ref, w_ref, b_ref, o_ref, *, n_batch, k_chunk):
    ct = ct_ref[...]
    act = ct * jax.nn.sigmoid(ct)
    rows = []
    for b in range(n_batch):
        col = act[:, b:b + 1]
        acc = b_ref[...]
        for k0 in range(0, D_MODEL, k_chunk):
            acc = acc + jnp.sum(w_ref[k0:k0 + k_chunk, :] * col[k0:k0 + k_chunk], axis=0, keepdims=True)
        rows.append(acc)
    o_ref[...] = jnp.concatenate(rows, axis=0)


def _ada(c, w_ada, b_ada):
    n_batch = c.shape[0]
    n_out = w_ada.shape[1]
    tn = 1024
    return pl.pallas_call(
        functools.partial(_ada_kernel, n_batch=n_batch, k_chunk=256),
        out_shape=jax.ShapeDtypeStruct((n_batch, n_out), f32),
        grid=(n_out // tn,),
        in_specs=[
            pl.BlockSpec((D_MODEL, n_batch), lambda j: (0, 0)),
            pl.BlockSpec((D_MODEL, tn), lambda j: (0, j)),
            pl.BlockSpec((1, tn), lambda j: (0, j)),
        ],
        out_specs=pl.BlockSpec((n_batch, tn), lambda j: (0, j)),
        compiler_params=_params("arbitrary"),
        name="ada",
    )(c.T, w_ada, b_ada.reshape(1, n_out))


def _proj_kernel(x_ref, mod_ref, g_ref, w_ref, o_ref, small_ref, h_ref, *, tn):
    j = pl.program_id(1)

    @pl.when(j == 0)
    def _():
        md = mod_ref[0]
        y = _rms(x_ref[...], g_ref[...])
        h_ref[...] = (y * (1.0 + md[1:2]) + md[0:1]).astype(bf16)

    res = _dot(h_ref[...], w_ref[...])
    o_ref[...] = res.astype(bf16)

    @pl.when(j == pl.num_programs(1) - 1)
    def _():
        small_ref[...] = res[:, tn - LANES:]


def _proj(x2d, mod3, norm_g, w_bf16, seq_len):
    m_rows = x2d.shape[0]
    n_cols = w_bf16.shape[1]
    tm, tn = 512, 1152
    assert seq_len % tm == 0 and n_cols % tn == 0
    per_b = seq_len // tm
    return pl.pallas_call(
        functools.partial(_proj_kernel, tn=tn),
        out_shape=[jax.ShapeDtypeStruct((m_rows, n_cols), bf16), jax.ShapeDtypeStruct((m_rows, LANES), f32)],
        grid=(m_rows // tm, n_cols // tn),
        in_specs=[
            pl.BlockSpec((tm, D_MODEL), lambda i, j: (i, 0)),
            pl.BlockSpec((1, 6, D_MODEL), lambda i, j: (i // per_b, 0, 0)),
            pl.BlockSpec((1, D_MODEL), lambda i, j: (0, 0)),
            pl.BlockSpec((D_MODEL, tn), lambda i, j: (0, j)),
        ],
        out_specs=[pl.BlockSpec((tm, tn), lambda i, j: (i, j)), pl.BlockSpec((tm, LANES), lambda i, j: (i, 0))],
        scratch_shapes=[pltpu.VMEM((tm, D_MODEL), bf16)],
        compiler_params=_params("arbitrary", "arbitrary"),
        name="proj",
    )(x2d, mod3, norm_g, w_bf16)


def _prep_kernel(p_ref, small_ref, gq_ref, gk_ref, gn_ref, gs_ref, gw_ref, bf_ref, inv_ref,
                 fq_ref, fk_ref, fvt_ref, cin_ref, nqu_ref, nqr_ref, kc_ref, vc_ref,
                 ks_ref, vst_ref, kw_ref, vw_ref, gate_ref, carry_ref, *, tm):
    i = pl.program_id(1)

    @pl.when(i == 0)
    def _():
        carry_ref[...] = jnp.zeros_like(carry_ref)

    def head(col, h):
        return p_ref[:, col + h * HEAD_DIM: col + (h + 1) * HEAD_DIM].astype(f32)

    row = lax.broadcasted_iota(jnp.int32, (tm, LANES), 0)
    lane = lax.broadcasted_iota(jnp.int32, (tm, LANES), 1)
    pos = i * tm + row
    ang = pos.astype(f32) * inv_ref[...]
    cos = jnp.cos(ang)
    sin = jnp.sin(ang)
    sin_signed = jnp.where(lane < HEAD_DIM // 2, -sin, sin)

    def rope(x):
        return x * cos + pltpu.roll(x, HEAD_DIM // 2, 1) * sin_signed

    small = small_ref[...]
    z = small + bf_ref[...]
    logf = jnp.minimum(z, 0.0) - jnp.log1p(jnp.exp(-jnp.abs(z)))
    t_idx = lax.broadcasted_iota(jnp.int32, (tm, tm), 0)
    s_idx = lax.broadcasted_iota(jnp.int32, (tm, tm), 1)
    tri = jnp.where(s_idx <= t_idx, 1.0, 0.0).astype(bf16)
    hi, mid, lo = _split3(logf)
    local = _dot(tri, hi) + (_dot(tri, mid) + _dot(tri, lo))
    cin_ref[0, 0] = carry_ref[0:1, :]
    carry_ref[...] = carry_ref[...] + local[tm - 1:tm, :]
    b_hi, b_mid, b_lo = (v.astype(f32) for v in _split3(local * (-LOG2E)))
    ones3 = jnp.where(lane < 3, 1.0, 0.0).astype(bf16)

    for h in range(FOX_HEADS):
        q = _rms(head(COL_FQ, h), gq_ref[...]) * (ATTN_SCALE * LOG2E)
        fq_ref[0, h] = jnp.concatenate([q.astype(bf16), ones3], axis=1)
        k = _rms(head(COL_FK, h), gk_ref[...])
        bias = jnp.where(lane == 0, b_hi[:, h:h + 1],
                         jnp.where(lane == 1, b_mid[:, h:h + 1],
                                   jnp.where(lane == 2, b_lo[:, h:h + 1], 0.0)))
        fk_ref[0, h] = jnp.concatenate([k.astype(bf16), bias.astype(bf16)], axis=1)
        fvt_ref[0, h, 0] = head(COL_FV, h).T.astype(bf16)

    gate_ref[0] = jax.nn.sigmoid(small)

    for h in range(NSA_HEADS):
        qn = _rms(head(COL_NQ, h), gn_ref[...])
        nqu_ref[0, h] = (qn * ATTN_SCALE).astype(bf16)
        nqr_ref[0, h] = rope(qn * (ATTN_SCALE * LOG2E)).astype(bf16)
    onehot = jnp.where(lane == lax.shift_right_logical(pos, SEL_SHIFT), 1.0, 0.0).astype(bf16)
    for g in range(NSA_KV_HEADS):
        kc_ref[0, g] = head(COL_KC, g).astype(bf16)
        vc_ref[0, g] = head(COL_VC, g).astype(bf16)
        ks = rope(_rms(head(COL_KS, g), gs_ref[...])).astype(bf16)
        ks_ref[0, g] = jnp.concatenate([ks, onehot], axis=1)
        vst_ref[0, g, 0] = head(COL_VS, g).T.astype(bf16)
        kw_ref[0, g] = rope(_rms(head(COL_KW, g), gw_ref[...])).astype(bf16)
        vwt = head(COL_VW, g).T.astype(bf16)
        for c in range(tm // WIN_TILE):
            vw_ref[0, g, c] = vwt[:, c * WIN_TILE:(c + 1) * WIN_TILE]


def _prep(proj, small, n_batch, seq_len, gq, gk, gn, gs, gw, b_forget_row, inv_freq):
    tm = KV_BLOCK
    per_b = seq_len // tm
    hshape = lambda n, w, dt: jax.ShapeDtypeStruct((n_batch, n, seq_len, w), dt)
    hspec = lambda n, w: pl.BlockSpec((1, n, tm, w), lambda b, i: (b, 0, i, 0))
    tshape = lambda n: jax.ShapeDtypeStruct((n_batch, n, per_b, HEAD_DIM, tm), bf16)
    tspec = lambda n: pl.BlockSpec((1, n, 1, HEAD_DIM, tm), lambda b, i: (b, 0, i, 0, 0))
    vec = pl.BlockSpec((1, LANES), lambda b, i: (0, 0))
    return pl.pallas_call(
        functools.partial(_prep_kernel, tm=tm),
        out_shape=[
            hshape(FOX_HEADS, 2 * HEAD_DIM, bf16), hshape(FOX_HEADS, 2 * HEAD_DIM, bf16), tshape(FOX_HEADS),
            jax.ShapeDtypeStruct((n_batch, per_b, 1, LANES), f32),
            hshape(NSA_HEADS, HEAD_DIM, bf16), hshape(NSA_HEADS, HEAD_DIM, bf16),
            hshape(NSA_KV_HEADS, HEAD_DIM, bf16), hshape(NSA_KV_HEADS, HEAD_DIM, bf16),
            hshape(NSA_KV_HEADS, HEAD_DIM + SEL_LANES, bf16), tshape(NSA_KV_HEADS),
            hshape(NSA_KV_HEADS, HEAD_DIM, bf16),
            jax.ShapeDtypeStruct((n_batch, NSA_KV_HEADS, seq_len // WIN_TILE, HEAD_DIM, WIN_TILE), bf16),
            jax.ShapeDtypeStruct((n_batch, seq_len, LANES), f32),
        ],
        grid=(n_batch, per_b),
        in_specs=[
            pl.BlockSpec((tm, D_IN_PAD), lambda b, i: (b * per_b + i, 0)),
            pl.BlockSpec((tm, LANES), lambda b, i: (b * per_b + i, 0)),
            vec, vec, vec, vec, vec, vec, vec,
        ],
        out_specs=[
            hspec(FOX_HEADS, 2 * HEAD_DIM), hspec(FOX_HEADS, 2 * HEAD_DIM), tspec(FOX_HEADS),
            pl.BlockSpec((1, 1, 1, LANES), lambda b, i: (b, i, 0, 0)),
            hspec(NSA_HEADS, HEAD_DIM), hspec(NSA_HEADS, HEAD_DIM),
            hspec(NSA_KV_HEADS, HEAD_DIM), hspec(NSA_KV_HEADS, HEAD_DIM),
            hspec(NSA_KV_HEADS, HEAD_DIM + SEL_LANES), tspec(NSA_KV_HEADS),
            hspec(NSA_KV_HEADS, HEAD_DIM),
            pl.BlockSpec((1, NSA_KV_HEADS, tm // WIN_TILE, HEAD_DIM, WIN_TILE), lambda b, i: (b, 0, i, 0, 0)),
            pl.BlockSpec((1, tm, LANES), lambda b, i: (b, i, 0)),
        ],
        scratch_shapes=[pltpu.VMEM((8, LANES), f32)],
        compiler_params=_params("arbitrary", "arbitrary"),
        name="prep",
    )(proj, small, gq, gk, gn, gs, gw, b_forget_row, inv_freq)


def _compress_kernel(x_ref, pe_ref, w1_ref, w2_ref, g_ref, o_ref, *, n_rows, do_norm):
    half = CMP_STRIDE * HEAD_DIM
    x = x_ref[0, 0]
    pe = pe_ref[...]
    xa = (x + pe[:, :half]).astype(bf16)
    xb = (x + pe[:, half:]).astype(bf16)
    a = _dot(xa, w1_ref[0:half, :])
    b = _dot(xb, w1_ref[half:2 * half, :])
    pre = a + pltpu.roll(b, n_rows - 1, 0)
    hid = pre * jax.nn.sigmoid(pre)
    out = _dot(hid.astype(bf16), w2_ref[...])
    if do_norm:
        out = _rms(out, g_ref[...])
    o_ref[0, 0] = out.astype(bf16)


def _compress(x4, pe, w1_bf16, w2_bf16, gain, do_norm):
    n_batch, n_g, seq_len, _ = x4.shape
    n_rows = seq_len // CMP_STRIDE
    half = CMP_STRIDE * HEAD_DIM
    xr = x4.reshape(n_batch, n_g, n_rows, half)
    return pl.pallas_call(
        functools.partial(_compress_kernel, n_rows=n_rows, do_norm=do_norm),
        out_shape=jax.ShapeDtypeStruct((n_batch, n_g, n_rows, HEAD_DIM), bf16),
        grid=(n_batch, n_g),
        in_specs=[
            pl.BlockSpec((1, 1, n_rows, half), lambda b, g: (b, g, 0, 0)),
            pl.BlockSpec((1, 2 * half), lambda b, g: (0, 0)),
            pl.BlockSpec((2 * half, CMP_HIDDEN), lambda b, g: (0, 0)),
            pl.BlockSpec((CMP_HIDDEN, HEAD_DIM), lambda b, g: (0, 0)),
            pl.BlockSpec((1, HEAD_DIM), lambda b, g: (0, 0)),
        ],
        out_specs=pl.BlockSpec((1, 1, n_rows, HEAD_DIM), lambda b, g: (b, g, 0, 0)),
        compiler_params=_params("arbitrary", "arbitrary"),
        name="compress",
    )(xr, pe.reshape(1, 2 * half), w1_bf16, w2_bf16, gain)


def _softmax_block(s, d, m_prev, l_prev):
    m_blk = jnp.max(s, axis=0, keepdims=True)
    m_new = jnp.maximum(m_prev, m_blk if d is None else m_blk + d)
    p = jnp.exp2(s - (m_new if d is None else m_new - d))
    alpha = jnp.exp2(m_prev - m_new)
    l_new = alpha * l_prev + jnp.sum(p, axis=0, keepdims=True)
    return m_new, l_new, alpha, p.astype(bf16)


def _flash_sweep(n_full, scores, values, offset, diag_mask, emit, s_buf, p_buf, n_q):
    s_buf[0] = scores(0)
    p_buf[1] = jnp.zeros(p_buf.shape[1:], p_buf.dtype)

    def accumulate(j, slot, alpha, acc):
        return alpha * acc + _dot(values(jnp.maximum(j, 0)), p_buf[slot])

    def step(j, cur, state):
        m, l, acc, alpha_prev = state
        acc = accumulate(j - 1, 1 - cur, alpha_prev, acc)
        m, l, alpha, p = _softmax_block(s_buf[cur], offset(j), m, l)
        p_buf[cur] = p
        s_buf[1 - cur] = scores(j + 1)
        return m, l, acc, alpha

    def finish(cur, state):
        m, l, acc, alpha_prev = state
        acc = accumulate(n_full - 1, 1 - cur, alpha_prev, acc)
        m, l, alpha, p = _softmax_block(diag_mask(s_buf[cur]), None, m, l)
        acc = alpha * acc + _dot(values(n_full), p)
        emit(acc / l)

    init = (jnp.full((1, n_q), -jnp.inf, f32), jnp.zeros((1, n_q), f32),
            jnp.zeros((HEAD_DIM, n_q), f32), jnp.ones((1, n_q), f32))
    state = lax.fori_loop(0, n_full // 2, lambda jj, st: step(2 * jj + 1, 1, step(2 * jj, 0, st)), init)

    @pl.when(n_full % 2 == 0)
    def _():
        finish(0, state)

    @pl.when(n_full % 2 == 1)
    def _():
        finish(1, step(n_full - 1, 0, state))


def _fox_kernel(q_ref, k_ref, vt_ref, c_ref, o_ref, s_buf, p_buf, *, tq):
    h = pl.program_id(1)
    i = pl.program_id(2)
    qa = q_ref[0, 0]
    lane = lax.broadcasted_iota(jnp.int32, (1, LANES), 1)

    def cin(j):
        return jnp.sum(jnp.where(lane == h, c_ref[0, j], 0.0), axis=-1, keepdims=True)

    ci = cin(i)

    def scores(j):
        return _dot_nt(k_ref[0, 0, pl.ds(pl.multiple_of(j * tq, tq), tq), :], qa)

    def diag_mask(s):
        kk = lax.broadcasted_iota(jnp.int32, (tq, tq), 0)
        qq = lax.broadcasted_iota(jnp.int32, (tq, tq), 1)
        return jnp.where(kk <= qq, s, -jnp.inf)

    def emit(out_t):
        o_ref[0] = out_t.T.astype(bf16)

    _flash_sweep(i, scores, lambda j: vt_ref[0, 0, j], lambda j: (ci - cin(j)) * LOG2E,
                 diag_mask, emit, s_buf, p_buf, tq)


def _fox(fq, fk, fvt, cin):
    n_batch, n_h, seq_len, _ = fq.shape
    tq = KV_BLOCK
    nkb = seq_len // tq
    return pl.pallas_call(
        functools.partial(_fox_kernel, tq=tq),
        out_shape=jax.ShapeDtypeStruct((n_batch, seq_len, n_h * HEAD_DIM), bf16),
        grid=(n_batch, n_h, nkb),
        in_specs=[
            pl.BlockSpec((1, 1, tq, 2 * HEAD_DIM), lambda b, h, i: (b, h, i, 0)),
            pl.BlockSpec((1, 1, seq_len, 2 * HEAD_DIM), lambda b, h, i: (b, h, 0, 0)),
            pl.BlockSpec((1, 1, nkb, HEAD_DIM, tq), lambda b, h, i: (b, h, 0, 0, 0)),
            pl.BlockSpec((1, nkb, 1, LANES), lambda b, h, i: (b, 0, 0, 0)),
        ],
        out_specs=pl.BlockSpec((1, tq, HEAD_DIM), lambda b, h, i: (b, i, h)),
        scratch_shapes=[pltpu.VMEM((2, tq, tq), f32), pltpu.VMEM((2, tq, tq), bf16)],
        compiler_params=_params("arbitrary", "arbitrary", "arbitrary"),
        name="fox",
    )(fq, fk, fvt, cin)


def _cmp_kernel(q_ref, k_ref, v_ref, gate_ref, ov_ref, o_ref, sel_ref, *, tq, n_rows, n_cmp, n_sel):
    i = pl.program_id(2)
    g = pl.program_id(1)
    hg = NSA_GROUP
    q = q_ref[0].reshape(hg * tq, HEAD_DIM)
    s = _dot_nt(q, k_ref[0, 0]).reshape(hg, tq, n_rows)
    t = i * tq + lax.broadcasted_iota(jnp.int32, (1, tq, 1), 1)
    n = lax.broadcasted_iota(jnp.int32, (1, 1, n_rows), 2)
    valid = (n * CMP_STRIDE + (CMP_LEN - 1) <= t) & (n < n_cmp)
    s = jnp.where(valid, s, -jnp.inf)
    m = jnp.max(s, axis=-1, keepdims=True)
    m = jnp.where(m == -jnp.inf, 0.0, m)
    e = jnp.exp(s - m)
    p = e / jnp.maximum(jnp.sum(e, axis=-1, keepdims=True), 1e-30)
    o = _dot(p.reshape(hg * tq, n_rows).astype(bf16), v_ref[0, 0])
    _write_gated(o_ref, gate_ref, o, g, 0, tq)

    lane = lax.broadcasted_iota(jnp.int32, (tq, LANES), 1)
    psum = p[0] + p[1] + p[2] + p[3]
    imp = _dot_f32_by_exact(psum, ov_ref[...])
    tq_pos = i * tq + lax.broadcasted_iota(jnp.int32, (tq, 1), 0)
    q_blk = lax.shift_right_logical(tq_pos, SEL_SHIFT)
    causal = lane <= q_blk
    forced = (lane == 0) | (lane == q_blk) | (lane == q_blk - 1)
    key = jnp.where(forced, jnp.inf, jnp.where(causal, imp, -1.0))
    key = jnp.where(lane < n_sel, key, -3.0)
    lane_f = lane.astype(f32)
    sel = jnp.zeros((tq, LANES), jnp.bool_)
    for _ in range(min(SEL_TOPK, n_sel)):
        mx = jnp.max(key, axis=-1, keepdims=True)
        first = jnp.min(jnp.where(key == mx, lane_f, float(LANES)), axis=-1, keepdims=True)
        pick = lane_f == first
        sel = sel | pick
        key = jnp.where(pick, -2.0, key)
    sel_ref[0, 0] = jnp.where(sel & causal, 0.0, -MASK_BIG).astype(bf16)


def _cmp(nqu, kcmp, vcmp, gates, n_cmp):
    n_batch, _, seq_len, _ = nqu.shape
    n_rows = kcmp.shape[2]
    n_sel = seq_len // SEL_BLOCK
    assert n_sel <= SEL_LANES
    tq = 1024
    c0 = np.arange(n_rows) * CMP_STRIDE
    s0 = np.arange(SEL_LANES) * SEL_BLOCK
    overlap = np.clip(np.minimum(c0[:, None] + CMP_LEN, s0[None, :] + SEL_BLOCK)
                      - np.maximum(c0[:, None], s0[None, :]), 0, None).astype(np.float32) / CMP_LEN
    overlap[n_cmp:, :] = 0.0
    overlap[:, n_sel:] = 0.0
    return pl.pallas_call(
        functools.partial(_cmp_kernel, tq=tq, n_rows=n_rows, n_cmp=n_cmp, n_sel=n_sel),
        out_shape=[
            jax.ShapeDtypeStruct((n_batch, seq_len, NSA_W), bf16),
            jax.ShapeDtypeStruct((n_batch, NSA_KV_HEADS, seq_len, SEL_LANES), bf16),
        ],
        grid=(n_batch, NSA_KV_HEADS, seq_len // tq),
        in_specs=[
            pl.BlockSpec((1, NSA_GROUP, tq, HEAD_DIM), lambda b, g, i: (b, g, i, 0)),
            pl.BlockSpec((1, 1, n_rows, HEAD_DIM), lambda b, g, i: (b, g, 0, 0)),
            pl.BlockSpec((1, 1, n_rows, HEAD_DIM), lambda b, g, i: (b, g, 0, 0)),
            pl.BlockSpec((1, tq, LANES), lambda b, g, i: (b, i, 0)),
            pl.BlockSpec((n_rows, SEL_LANES), lambda b, g, i: (0, 0)),
        ],
        out_specs=[
            pl.BlockSpec((1, tq, NSA_GROUP * HEAD_DIM), lambda b, g, i: (b, i, g)),
            pl.BlockSpec((1, 1, tq, SEL_LANES), lambda b, g, i: (b, g, i, 0)),
        ],
        compiler_params=_params("arbitrary", "arbitrary", "arbitrary"),
        name="cmp",
    )(nqu, kcmp, vcmp, gates, jnp.asarray(overlap, dtype=bf16))


def _write_gated(o_ref, gate_ref, out, g, branch, tq):
    gates = gate_ref[0]
    lane = lax.broadcasted_iota(jnp.int32, (tq, LANES), 1)
    for h in range(NSA_GROUP):
        gcol = FOX_HEADS + (g * NSA_GROUP + h) * N_BRANCH + branch
        gh = jnp.sum(jnp.where(lane == gcol, gates, 0.0), axis=-1, keepdims=True)
        o_ref[0, :, h * HEAD_DIM:(h + 1) * HEAD_DIM] = (gh * out[h * tq:(h + 1) * tq]).astype(bf16)


def _slc_kernel(q_ref, sel_ref, k_ref, vt_ref, gate_ref, o_ref, s_buf, p_buf, *, tq, tk):
    i = pl.program_id(2)
    g = pl.program_id(1)
    hg = NSA_GROUP
    n_q = hg * tq
    q = q_ref[0].reshape(n_q, HEAD_DIM)
    sel = sel_ref[0, 0]
    qa = jnp.concatenate([q, jnp.concatenate([sel] * hg, axis=0)], axis=1)
    q0 = i * tq
    jd = q0 // tk

    def scores(j):
        return _dot_nt(k_ref[0, 0, pl.ds(pl.multiple_of(j * tk, tk), tk), :], qa)

    def diag_mask(s):
        kp = jd * tk + lax.broadcasted_iota(jnp.int32, (tk, n_q), 0)
        t = q0 + (lax.broadcasted_iota(jnp.int32, (tk, n_q), 1) & (tq - 1))
        return jnp.where(kp <= t, s, -jnp.inf)

    def emit(out):
        gates = gate_ref[0]
        lane = lax.broadcasted_iota(jnp.int32, (tq, LANES), 1)
        for h in range(hg):
            gcol = FOX_HEADS + (g * hg + h) * N_BRANCH + 1
            gh = jnp.sum(jnp.where(lane == gcol, gates, 0.0), axis=-1, keepdims=True)
            o_ref[0, :, h * HEAD_DIM:(h + 1) * HEAD_DIM] = (gh * out[:, h * tq:(h + 1) * tq].T).astype(bf16)

    _flash_sweep(jd, scores, lambda j: vt_ref[0, 0, j], lambda j: None, diag_mask, emit, s_buf, p_buf, n_q)


def _slc(nqr, selneg, kaug, vst, gates):
    n_batch, _, seq_len, _ = nqr.shape
    tq, tk = 128, KV_BLOCK
    nkb = seq_len // tk
    assert tq & (tq - 1) == 0
    return pl.pallas_call(
        functools.partial(_slc_kernel, tq=tq, tk=tk),
        out_shape=jax.ShapeDtypeStruct((n_batch, seq_len, NSA_W), bf16),
        grid=(n_batch, NSA_KV_HEADS, seq_len // tq),
        in_specs=[
            pl.BlockSpec((1, NSA_GROUP, tq, HEAD_DIM), lambda b, g, i: (b, g, i, 0)),
            pl.BlockSpec((1, 1, tq, SEL_LANES), lambda b, g, i: (b, g, i, 0)),
            pl.BlockSpec((1, 1, seq_len, HEAD_DIM + SEL_LANES), lambda b, g, i: (b, g, 0, 0)),
            pl.BlockSpec((1, 1, nkb, HEAD_DIM, tk), lambda b, g, i: (b, g, 0, 0, 0)),
            pl.BlockSpec((1, tq, LANES), lambda b, g, i: (b, i, 0)),
        ],
        out_specs=pl.BlockSpec((1, tq, NSA_GROUP * HEAD_DIM), lambda b, g, i: (b, i, g)),
        scratch_shapes=[pltpu.VMEM((2, tk, NSA_GROUP * tq), f32), pltpu.VMEM((2, tk, NSA_GROUP * tq), bf16)],
        compiler_params=_params("arbitrary", "arbitrary", "arbitrary"),
        name="slc",
    )(nqr, selneg, kaug, vst, gates)


def _win_kernel(q_ref, k_ref, vt_ref, gate_ref, o_ref, *, tq, n_sub):
    i = pl.program_id(2)
    g = pl.program_id(1)
    hg = NSA_GROUP
    span = WINDOW + tq
    q0s = [(i * n_sub + u) * tq for u in range(n_sub)]
    k0s = [pl.multiple_of(jnp.maximum(q0 - WINDOW, 0), tq) for q0 in q0s]
    scores = []
    for u in range(n_sub):
        q = q_ref[0, :, u * tq:(u + 1) * tq, :].reshape(hg * tq, HEAD_DIM)
        scores.append(_dot_nt(k_ref[0, 0, pl.ds(k0s[u], span), :], q))
    probs, denoms = [], []
    for u in range(n_sub):
        kp = k0s[u] + lax.broadcasted_iota(jnp.int32, (span, tq), 0)
        t = q0s[u] + lax.broadcasted_iota(jnp.int32, (span, tq), 1)
        diff = t - kp
        bias = jnp.where((diff >= 0) & (diff < WINDOW), 0.0, -jnp.inf)
        s = jnp.concatenate([scores[u][:, h * tq:(h + 1) * tq] + bias for h in range(hg)], axis=1)
        e = jnp.exp2(s - jnp.max(s, axis=0, keepdims=True))
        denoms.append(jnp.sum(e, axis=0, keepdims=True))
        probs.append(e.astype(bf16))
    gates = gate_ref[0]
    lane = lax.broadcasted_iota(jnp.int32, (tq, LANES), 1)
    for u in range(n_sub):
        jb = k0s[u] // tq
        vt = jnp.concatenate([vt_ref[0, 0, jb + c] for c in range(span // tq)], axis=1)
        out = _dot(vt, probs[u]) / denoms[u]
        for h in range(hg):
            gcol = FOX_HEADS + (g * hg + h) * N_BRANCH + 2
            gh = jnp.sum(jnp.where(lane == gcol, gates[u * tq:(u + 1) * tq], 0.0), axis=-1, keepdims=True)
            o_ref[0, u * tq:(u + 1) * tq, h * HEAD_DIM:(h + 1) * HEAD_DIM] = (
                gh * out[:, h * tq:(h + 1) * tq].T).astype(bf16)


def _win(nqr, kw, vwt, gates):
    n_batch, _, seq_len, _ = nqr.shape
    tq, n_sub = WIN_TILE, 2
    assert seq_len >= WINDOW + tq and WINDOW % tq == 0
    return pl.pallas_call(
        functools.partial(_win_kernel, tq=tq, n_sub=n_sub),
        out_shape=jax.ShapeDtypeStruct((n_batch, seq_len, NSA_W), bf16),
        grid=(n_batch, NSA_KV_HEADS, seq_len // (tq * n_sub)),
        in_specs=[
            pl.BlockSpec((1, NSA_GROUP, tq * n_sub, HEAD_DIM), lambda b, g, i: (b, g, i, 0)),
            pl.BlockSpec((1, 1, seq_len, HEAD_DIM), lambda b, g, i: (b, g, 0, 0)),
            pl.BlockSpec((1, 1, seq_len // tq, HEAD_DIM, tq), lambda b, g, i: (b, g, 0, 0, 0)),
            pl.BlockSpec((1, tq * n_sub, LANES), lambda b, g, i: (b, i, 0)),
        ],
        out_specs=pl.BlockSpec((1, tq * n_sub, NSA_GROUP * HEAD_DIM), lambda b, g, i: (b, i, g)),
        compiler_params=_params("arbitrary", "arbitrary", "arbitrary"),
        name="win",
    )(nqr, kw, vwt, gates)


def _outproj_kernel(fox_ref, c_ref, s_ref, w_ref, x_ref, mod_ref, wo_ref, o_ref):
    nsa = c_ref[...].astype(f32) + s_ref[...].astype(f32) + w_ref[...].astype(f32)
    a = jnp.concatenate([fox_ref[...], nsa.astype(bf16)], axis=1)
    o_ref[...] = x_ref[...] + mod_ref[0][2:3] * _dot(a, wo_ref[...])


def _outproj(ofox, ocmp, oslc, owin, x2d, mod3, wo_bf16, seq_len):
    m_rows = x2d.shape[0]
    tm = 512
    per_b = seq_len // tm
    half = pl.BlockSpec((tm, FOX_W), lambda i: (i, 0))
    return pl.pallas_call(
        _outproj_kernel,
        out_shape=jax.ShapeDtypeStruct((m_rows, D_MODEL), f32),
        grid=(m_rows // tm,),
        in_specs=[
            half, half, half, half,
            pl.BlockSpec((tm, D_MODEL), lambda i: (i, 0)),
            pl.BlockSpec((1, 6, D_MODEL), lambda i: (i // per_b, 0, 0)),
            pl.BlockSpec((D_MODEL, D_MODEL), lambda i: (0, 0)),
        ],
        out_specs=pl.BlockSpec((tm, D_MODEL), lambda i: (i, 0)),
        compiler_params=_params("arbitrary"),
        name="outproj",
    )(ofox, ocmp, oslc, owin, x2d, mod3, wo_bf16)


def _mlp_kernel(x_ref, mod_ref, g_ref, wu_ref, wd_ref, o_ref, h_ref, acc_ref):
    f = pl.program_id(1)

    @pl.when(f == 0)
    def _():
        md = mod_ref[0]
        y = _rms(x_ref[...], g_ref[...])
        h_ref[...] = (y * (1.0 + md[4:5]) + md[3:4]).astype(bf16)
        acc_ref[...] = jnp.zeros_like(acc_ref)

    u = jnp.maximum(_dot(h_ref[...], wu_ref[...]), 0.0)
    acc_ref[...] += _dot((u * u).astype(bf16), wd_ref[...])

    @pl.when(f == pl.num_programs(1) - 1)
    def _():
        o_ref[...] = x_ref[...] + mod_ref[0][5:6] * acc_ref[...]


def _mlp(x2d, mod3, norm_g, wu_bf16, wd_bf16, seq_len):
    m_rows = x2d.shape[0]
    tm, tf = 512, 1024
    per_b = seq_len // tm
    return pl.pallas_call(
        _mlp_kernel,
        out_shape=jax.ShapeDtypeStruct((m_rows, D_MODEL), f32),
        grid=(m_rows // tm, D_FF // tf),
        in_specs=[
            pl.BlockSpec((tm, D_MODEL), lambda i, f: (i, 0)),
            pl.BlockSpec((1, 6, D_MODEL), lambda i, f: (i // per_b, 0, 0)),
            pl.BlockSpec((1, D_MODEL), lambda i, f: (0, 0)),
            pl.BlockSpec((D_MODEL, tf), lambda i, f: (0, f)),
            pl.BlockSpec((tf, D_MODEL), lambda i, f: (f, 0)),
        ],
        out_specs=pl.BlockSpec((tm, D_MODEL), lambda i, f: (i, 0)),
        scratch_shapes=[pltpu.VMEM((tm, D_MODEL), bf16), pltpu.VMEM((tm, D_MODEL), f32)],
        compiler_params=_params("arbitrary", "arbitrary"),
        name="mlp",
    )(x2d, mod3, norm_g, wu_bf16, wd_bf16)


def _layer(x, c, w_ada, b_ada, norm1_g, w_in, b_forget, fox_q_norm, fox_k_norm, nsa_q_norm,
           cmp_k_norm, slc_k_norm, win_k_norm, cmp_pe_k, cmp_w1_k, cmp_w2_k, cmp_pe_v, cmp_w1_v,
           cmp_w2_v, w_out, norm2_g, w_up, w_down):
    n_batch, seq_len, _ = x.shape
    n_cmp = (seq_len - CMP_LEN) // CMP_STRIDE + 1
    row = lambda v: v.reshape(1, -1)

    z0 = COL_NQ
    nq0 = z0 + FOX_HEADS
    gz0 = nq0 + NSA_W + 6 * KV_W
    w_perm = jnp.concatenate([
        w_in[:, :z0], w_in[:, nq0:gz0], w_in[:, z0:nq0], w_in[:, gz0:],
        jnp.zeros((D_MODEL, LANES - FOX_HEADS - N_BRANCH * NSA_HEADS), w_in.dtype)], axis=1).astype(bf16)
    half = HEAD_DIM // 2
    inv_freq = ROPE_THETA ** (-jnp.arange(half, dtype=f32) / half)
    inv_freq = jnp.concatenate([inv_freq, inv_freq]).reshape(1, HEAD_DIM)

    mod3 = _ada(c, w_ada, b_ada).reshape(n_batch, 6, D_MODEL)
    x2d = x.reshape(n_batch * seq_len, D_MODEL)
    proj, small = _proj(x2d, mod3, row(norm1_g), w_perm, seq_len)
    (fq, fk, fvt, cin, nqu, nqr, kc, vc, kaug, vst, kw, vw, gates) = _prep(
        proj, small, n_batch, seq_len, row(fox_q_norm), row(fox_k_norm), row(nsa_q_norm), row(slc_k_norm),
        row(win_k_norm), jnp.pad(b_forget, (0, LANES - FOX_HEADS)).reshape(1, LANES), inv_freq)
    kcmp = _compress(kc, cmp_pe_k, cmp_w1_k.astype(bf16), cmp_w2_k.astype(bf16), row(cmp_k_norm), True)
    vcmp = _compress(vc, cmp_pe_v, cmp_w1_v.astype(bf16), cmp_w2_v.astype(bf16), row(cmp_k_norm), False)
    ofox = _fox(fq, fk, fvt, cin)
    ocmp, selneg = _cmp(nqu, kcmp, vcmp, gates, n_cmp)
    oslc = _slc(nqr, selneg, kaug, vst, gates)
    owin = _win(nqr, kw, vw, gates)
    x1 = _outproj(ofox.reshape(-1, FOX_W), ocmp.reshape(-1, NSA_W), oslc.reshape(-1, NSA_W),
                  owin.reshape(-1, NSA_W), x2d, mod3, w_out.astype(bf16), seq_len)
    x2 = _mlp(x1, mod3, row(norm2_g), w_up.astype(bf16), w_down.astype(bf16), seq_len)
    return x2.reshape(n_batch, seq_len, D_MODEL)


def kernel(x, c, w_ada, b_ada, norm1_g, w_in, b_forget, fox_q_norm, fox_k_norm, nsa_q_norm, cmp_k_norm,
           slc_k_norm, win_k_norm, cmp_pe_k, cmp_w1_k, cmp_w2_k, cmp_pe_v, cmp_w1_v, cmp_w2_v, w_out,
           norm2_g, w_up, w_down):
    depth = w_ada.shape[0]
    for l in range(depth):
        x = _layer(x, c, w_ada[l], b_ada[l], norm1_g[l], w_in[l], b_forget[l], fox_q_norm[l], fox_k_norm[l],
                   nsa_q_norm[l], cmp_k_norm[l], slc_k_norm[l], win_k_norm[l], cmp_pe_k[l], cmp_w1_k[l],
                   cmp_w2_k[l], cmp_pe_v[l], cmp_w1_v[l], cmp_w2_v[l], w_out[l], norm2_g[l], w_up[l],
                   w_down[l])
    return x
```

```python
import functools
import math
from typing import Any, Callable, NamedTuple

import numpy as np
import jax
import jax.numpy as jnp
from jax import lax
from jax.experimental import pallas as pl
from jax.experimental.pallas import tpu as pltpu

D_MODEL = 2048
HEAD_DIM = 128
FOX_HEADS = 8
NSA_HEADS = 8
NSA_KV_HEADS = 2
NSA_GROUP = NSA_HEADS // NSA_KV_HEADS
N_BRANCH = 3
D_FF = 4 * D_MODEL
ROPE_THETA = 10000.0
CMP_LEN = 32
CMP_STRIDE = 16
CMP_HIDDEN = 2 * HEAD_DIM
SEL_BLOCK = 64
SEL_SHIFT = 6
SEL_TOPK = 16
WINDOW = 512
NORM_EPS = 1e-6
ATTN_SCALE = HEAD_DIM ** -0.5
FOX_W = FOX_HEADS * HEAD_DIM
NSA_W = NSA_HEADS * HEAD_DIM
KV_W = NSA_KV_HEADS * HEAD_DIM

LANES = 128
SEL_LANES = LANES
MASK_BIG = 1e30
KV_BLOCK = 512
WIN_TILE = 128
LOG2E = math.log2(math.e)

COL_FQ = 0
COL_FK = COL_FQ + FOX_W
COL_FV = COL_FK + FOX_W
COL_NQ = COL_FV + FOX_W
COL_KC = COL_NQ + NSA_W
COL_VC = COL_KC + KV_W
COL_KS = COL_VC + KV_W
COL_VS = COL_KS + KV_W
COL_KW = COL_VS + KV_W
COL_VW = COL_KW + KV_W
COL_SMALL = COL_VW + KV_W
D_IN_PAD = COL_SMALL + LANES

VMEM_LIMIT = 56 * 1024 * 1024

f32 = jnp.float32
bf16 = jnp.bfloat16


def _params(*sem):
    return pltpu.CompilerParams(dimension_semantics=sem, vmem_limit_bytes=VMEM_LIMIT)


def _dot_nt(a, b):
    return lax.dot_general(a, b, (((1,), (1,)), ((), ())), preferred_element_type=f32)


def _dot(a, b):
    return jnp.dot(a, b, preferred_element_type=f32)


def _split3(x):
    hi = x.astype(bf16)
    r1 = x - hi.astype(f32)
    mid = r1.astype(bf16)
    lo = (r1 - mid.astype(f32)).astype(bf16)
    return hi, mid, lo


def _dot_f32_by_exact(x, w_bf16):
    hi, mid, lo = _split3(x)
    return _dot(hi, w_bf16) + (_dot(mid, w_bf16) + _dot(lo, w_bf16))


def _rms(x, gain):
    ms = jnp.mean(x * x, axis=-1, keepdims=True)
    return x * lax.rsqrt(ms + NORM_EPS) * gain


def _ada_kernel(ct_ref, w_ref, b_ref, o_ref, *, n_batch, k_chunk):
    ct = ct_ref[...]
    act = ct * jax.nn.sigmoid(ct)
    rows = []
    for b in range(n_batch):
        col = act[:, b:b + 1]
        acc = b_ref[...]
        for k0 in range(0, D_MODEL, k_chunk):
            acc = acc + jnp.sum(w_ref[k0:k0 + k_chunk, :] * col[k0:k0 + k_chunk], axis=0, keepdims=True)
        rows.append(acc)
    o_ref[...] = jnp.concatenate(rows, axis=0)


def _ada(c, w_ada, b_ada):
    n_batch = c.shape[0]
    n_out = w_ada.shape[1]
    tn = 1024
    return pl.pallas_call(
        functools.partial(_ada_kernel, n_batch=n_batch, k_chunk=256),
        out_shape=jax.ShapeDtypeStruct((n_batch, n_out), f32),
        grid=(n_out // tn,),
        in_specs=[
            pl.BlockSpec((D_MODEL, n_batch), lambda j: (0, 0)),
            pl.BlockSpec((D_MODEL, tn), lambda j: (0, j)),
            pl.BlockSpec((1, tn), lambda j: (0, j)),
        ],
        out_specs=pl.BlockSpec((n_batch, tn), lambda j: (0, j)),
        compiler_params=_params("arbitrary"),
        name="ada",
    )(c.T, w_ada, b_ada.reshape(1, n_out))


def _proj_kernel(x_ref, mod_ref, g_ref, w_ref, o_ref, small_ref, h_ref, *, tn):
    j = pl.program_id(1)

    @pl.when(j == 0)
    def _():
        md = mod_ref[0]
        y = _rms(x_ref[...], g_ref[...])
        h_ref[...] = (y * (1.0 + md[1:2]) + md[0:1]).astype(bf16)

    res = _dot(h_ref[...], w_ref[...])
    o_ref[...] = res.astype(bf16)

    @pl.when(j == pl.num_programs(1) - 1)
    def _():
        small_ref[...] = res[:, tn - LANES:]


def _proj(x2d, mod3, norm_g, w_bf16, seq_len):
    m_rows = x2d.shape[0]
    n_cols = w_bf16.shape[1]
    tm, tn = 512, 1152
    assert seq_len % tm == 0 and n_cols % tn == 0
    per_b = seq_len // tm
    return pl.pallas_call(
        functools.partial(_proj_kernel, tn=tn),
        out_shape=[jax.ShapeDtypeStruct((m_rows, n_cols), bf16), jax.ShapeDtypeStruct((m_rows, LANES), f32)],
        grid=(m_rows // tm, n_cols // tn),
        in_specs=[
            pl.BlockSpec((tm, D_MODEL), lambda i, j: (i, 0)),
            pl.BlockSpec((1, 6, D_MODEL), lambda i, j: (i // per_b, 0, 0)),
            pl.BlockSpec((1, D_MODEL), lambda i, j: (0, 0)),
            pl.BlockSpec((D_MODEL, tn), lambda i, j: (0, j)),
        ],
        out_specs=[pl.BlockSpec((tm, tn), lambda i, j: (i, j)), pl.BlockSpec((tm, LANES), lambda i, j: (i, 0))],
        scratch_shapes=[pltpu.VMEM((tm, D_MODEL), bf16)],
        compiler_params=_params("arbitrary", "arbitrary"),
        name="proj",
    )(x2d, mod3, norm_g, w_bf16)


def _prep_kernel(p_ref, small_ref, gq_ref, gk_ref, gn_ref, gs_ref, gw_ref, bf_ref, inv_ref,
                 fq_ref, fk_ref, fvt_ref, cin_ref, nqu_ref, nqr_ref, kc_ref, vc_ref,
                 ks_ref, vst_ref, kw_ref, vw_ref, gate_ref, carry_ref, *, tm):
    i = pl.program_id(1)

    @pl.when(i == 0)
    def _():
        carry_ref[...] = jnp.zeros_like(carry_ref)

    def head(col, h):
        return p_ref[:, col + h * HEAD_DIM: col + (h + 1) * HEAD_DIM].astype(f32)

    row = lax.broadcasted_iota(jnp.int32, (tm, LANES), 0)
    lane = lax.broadcasted_iota(jnp.int32, (tm, LANES), 1)
    pos = i * tm + row
    ang = pos.astype(f32) * inv_ref[...]
    cos = jnp.cos(ang)
    sin = jnp.sin(ang)
    sin_signed = jnp.where(lane < HEAD_DIM // 2, -sin, sin)

    def rope(x):
        return x * cos + pltpu.roll(x, HEAD_DIM // 2, 1) * sin_signed

    small = small_ref[...]
    z = small + bf_ref[...]
    logf = jnp.minimum(z, 0.0) - jnp.log1p(jnp.exp(-jnp.abs(z)))
    t_idx = lax.broadcasted_iota(jnp.int32, (tm, tm), 0)
    s_idx = lax.broadcasted_iota(jnp.int32, (tm, tm), 1)
    tri = jnp.where(s_idx <= t_idx, 1.0, 0.0).astype(bf16)
    hi, mid, lo = _split3(logf)
    local = _dot(tri, hi) + (_dot(tri, mid) + _dot(tri, lo))
    cin_ref[0, 0] = carry_ref[0:1, :]
    carry_ref[...] = carry_ref[...] + local[tm - 1:tm, :]
    b_hi, b_mid, b_lo = (v.astype(f32) for v in _split3(local * (-LOG2E)))
    ones3 = jnp.where(lane < 3, 1.0, 0.0).astype(bf16)

    for h in range(FOX_HEADS):
        q = _rms(head(COL_FQ, h), gq_ref[...]) * (ATTN_SCALE * LOG2E)
        fq_ref[0, h] = jnp.concatenate([q.astype(bf16), ones3], axis=1)
        k = _rms(head(COL_FK, h), gk_ref[...])
        bias = jnp.where(lane == 0, b_hi[:, h:h + 1],
                         jnp.where(lane == 1, b_mid[:, h:h + 1],
                                   jnp.where(lane == 2, b_lo[:, h:h + 1], 0.0)))
        fk_ref[0, h] = jnp.concatenate([k.astype(bf16), bias.astype(bf16)], axis=1)
        fvt_ref[0, h, 0] = head(COL_FV, h).T.astype(bf16)

    gate_ref[0] = jax.nn.sigmoid(small)

    for h in range(NSA_HEADS):
        qn = _rms(head(COL_NQ, h), gn_ref[...])
        nqu_ref[0, h] = (qn * ATTN_SCALE).astype(bf16)
        nqr_ref[0, h] = rope(qn * (ATTN_SCALE * LOG2E)).astype(bf16)
    onehot = jnp.where(lane == lax.shift_right_logical(pos, SEL_SHIFT), 1.0, 0.0).astype(bf16)
    for g in range(NSA_KV_HEADS):
        kc_ref[0, g] = head(COL_KC, g).astype(bf16)
        vc_ref[0, g] = head(COL_VC, g).astype(bf16)
        ks = rope(_rms(head(COL_KS, g), gs_ref[...])).astype(bf16)
        ks_ref[0, g] = jnp.concatenate([ks, onehot], axis=1)
        vst_ref[0, g, 0] = head(COL_VS, g).T.astype(bf16)
        kw_ref[0, g] = rope(_rms(head(COL_KW, g), gw_ref[...])).astype(bf16)
        vwt = head(COL_VW, g).T.astype(bf16)
        for c in range(tm // WIN_TILE):
            vw_ref[0, g, c] = vwt[:, c * WIN_TILE:(c + 1) * WIN_TILE]


def _prep(proj, small, n_batch, seq_len, gq, gk, gn, gs, gw, b_forget_row, inv_freq):
    tm = KV_BLOCK
    per_b = seq_len // tm
    hshape = lambda n, w, dt: jax.ShapeDtypeStruct((n_batch, n, seq_len, w), dt)
    hspec = lambda n, w: pl.BlockSpec((1, n, tm, w), lambda b, i: (b, 0, i, 0))
    tshape = lambda n: jax.ShapeDtypeStruct((n_batch, n, per_b, HEAD_DIM, tm), bf16)
    tspec = lambda n: pl.BlockSpec((1, n, 1, HEAD_DIM, tm), lambda b, i: (b, 0, i, 0, 0))
    vec = pl.BlockSpec((1, LANES), lambda b, i: (0, 0))
    return pl.pallas_call(
        functools.partial(_prep_kernel, tm=tm),
        out_shape=[
            hshape(FOX_HEADS, 2 * HEAD_DIM, bf16), hshape(FOX_HEADS, 2 * HEAD_DIM, bf16), tshape(FOX_HEADS),
            jax.ShapeDtypeStruct((n_batch, per_b, 1, LANES), f32),
            hshape(NSA_HEADS, HEAD_DIM, bf16), hshape(NSA_HEADS, HEAD_DIM, bf16),
            hshape(NSA_KV_HEADS, HEAD_DIM, bf16), hshape(NSA_KV_HEADS, HEAD_DIM, bf16),
            hshape(NSA_KV_HEADS, HEAD_DIM + SEL_LANES, bf16), tshape(NSA_KV_HEADS),
            hshape(NSA_KV_HEADS, HEAD_DIM, bf16),
            jax.ShapeDtypeStruct((n_batch, NSA_KV_HEADS, seq_len // WIN_TILE, HEAD_DIM, WIN_TILE), bf16),
            jax.ShapeDtypeStruct((n_batch, seq_len, LANES), f32),
        ],
        grid=(n_batch, per_b),
        in_specs=[
            pl.BlockSpec((tm, D_IN_PAD), lambda b, i: (b * per_b + i, 0)),
            pl.BlockSpec((tm, LANES), lambda b, i: (b * per_b + i, 0)),
            vec, vec, vec, vec, vec, vec, vec,
        ],
        out_specs=[
            hspec(FOX_HEADS, 2 * HEAD_DIM), hspec(FOX_HEADS, 2 * HEAD_DIM), tspec(FOX_HEADS),
            pl.BlockSpec((1, 1, 1, LANES), lambda b, i: (b, i, 0, 0)),
            hspec(NSA_HEADS, HEAD_DIM), hspec(NSA_HEADS, HEAD_DIM),
            hspec(NSA_KV_HEADS, HEAD_DIM), hspec(NSA_KV_HEADS, HEAD_DIM),
            hspec(NSA_KV_HEADS, HEAD_DIM + SEL_LANES), tspec(NSA_KV_HEADS),
            hspec(NSA_KV_HEADS, HEAD_DIM),
            pl.BlockSpec((1, NSA_KV_HEADS, tm // WIN_TILE, HEAD_DIM, WIN_TILE), lambda b, i: (b, 0, i, 0, 0)),
            pl.BlockSpec((1, tm, LANES), lambda b, i: (b, i, 0)),
        ],
        scratch_shapes=[pltpu.VMEM((8, LANES), f32)],
        compiler_params=_params("arbitrary", "arbitrary"),
        name="prep",
    )(proj, small, gq, gk, gn, gs, gw, b_forget_row, inv_freq)


def _compress_kernel(x_ref, pe_ref, w1_ref, w2_ref, g_ref, o_ref, *, n_rows, do_norm):
    half = CMP_STRIDE * HEAD_DIM
    x = x_ref[0, 0]
    pe = pe_ref[...]
    xa = (x + pe[:, :half]).astype(bf16)
    xb = (x + pe[:, half:]).astype(bf16)
    a = _dot(xa, w1_ref[0:half, :])
    b = _dot(xb, w1_ref[half:2 * half, :])
    pre = a + pltpu.roll(b, n_rows - 1, 0)
    hid = pre * jax.nn.sigmoid(pre)
    out = _dot(hid.astype(bf16), w2_ref[...])
    if do_norm:
        out = _rms(out, g_ref[...])
    o_ref[0, 0] = out.astype(bf16)


def _compress(x4, pe, w1_bf16, w2_bf16, gain, do_norm):
    n_batch, n_g, seq_len, _ = x4.shape
    n_rows = seq_len // CMP_STRIDE
    half = CMP_STRIDE * HEAD_DIM
    xr = x4.reshape(n_batch, n_g, n_rows, half)
    return pl.pallas_call(
        functools.partial(_compress_kernel, n_rows=n_rows, do_norm=do_norm),
        out_shape=jax.ShapeDtypeStruct((n_batch, n_g, n_rows, HEAD_DIM), bf16),
        grid=(n_batch, n_g),
        in_specs=[
            pl.BlockSpec((1, 1, n_rows, half), lambda b, g: (b, g, 0, 0)),
            pl.BlockSpec((1, 2 * half), lambda b, g: (0, 0)),
            pl.BlockSpec((2 * half, CMP_HIDDEN), lambda b, g: (0, 0)),
            pl.BlockSpec((CMP_HIDDEN, HEAD_DIM), lambda b, g: (0, 0)),
            pl.BlockSpec((1, HEAD_DIM), lambda b, g: (0, 0)),
        ],
        out_specs=pl.BlockSpec((1, 1, n_rows, HEAD_DIM), lambda b, g: (b, g, 0, 0)),
        compiler_params=_params("arbitrary", "arbitrary"),
        name="compress",
    )(xr, pe.reshape(1, 2 * half), w1_bf16, w2_bf16, gain)


def _softmax_block(s, d, m_prev, l_prev):
    m_blk = jnp.max(s, axis=0, keepdims=True)
    m_new = jnp.maximum(m_prev, m_blk if d is None else m_blk + d)
    p = jnp.exp2(s - (m_new if d is None else m_new - d))
    alpha = jnp.exp2(m_prev - m_new)
    l_new = alpha * l_prev + jnp.sum(p, axis=0, keepdims=True)
    return m_new, l_new, alpha, p.astype(bf16)


class _Chain(NamedTuple):
    scores: Callable
    values: Callable
    offset: Callable
    diag_mask: Callable
    emit: Callable
    s_buf: Any
    p_buf: Any


def _flash_sweep(n_full, chains, n_q):
    for c in chains:
        c.s_buf[0] = c.scores(0)
        c.p_buf[1] = jnp.zeros(c.p_buf.shape[1:], c.p_buf.dtype)

    def accumulate(c, j, slot, alpha, acc):
        return alpha * acc + _dot(c.values(jnp.maximum(j, 0)), c.p_buf[slot])

    def step(j, cur, states):
        out = []
        for c, (m, l, acc, alpha_prev) in zip(chains, states):
            acc = accumulate(c, j - 1, 1 - cur, alpha_prev, acc)
            m, l, alpha, p = _softmax_block(c.s_buf[cur], c.offset(j), m, l)
            c.p_buf[cur] = p
            c.s_buf[1 - cur] = c.scores(j + 1)
            out.append((m, l, acc, alpha))
        return tuple(out)

    def finish(cur, states):
        for c, (m, l, acc, alpha_prev) in zip(chains, states):
            acc = accumulate(c, n_full - 1, 1 - cur, alpha_prev, acc)
            m, l, alpha, p = _softmax_block(c.diag_mask(c.s_buf[cur]), None, m, l)
            acc = alpha * acc + _dot(c.values(n_full), p)
            c.emit(acc / l)

    init = tuple((jnp.full((1, n_q), -jnp.inf, f32), jnp.zeros((1, n_q), f32),
                  jnp.zeros((HEAD_DIM, n_q), f32), jnp.ones((1, n_q), f32)) for _ in chains)
    states = lax.fori_loop(0, n_full // 2, lambda jj, st: step(2 * jj + 1, 1, step(2 * jj, 0, st)), init)

    @pl.when(n_full % 2 == 0)
    def _():
        finish(0, states)

    @pl.when(n_full % 2 == 1)
    def _():
        finish(1, step(n_full - 1, 0, states))


def _fox_kernel(q_ref, k_ref, vt_ref, c_ref, o_ref, *bufs, tq, n_heads):
    hb = pl.program_id(1)
    i = pl.program_id(2)
    lane = lax.broadcasted_iota(jnp.int32, (1, LANES), 1)

    def diag_mask(s):
        kk = lax.broadcasted_iota(jnp.int32, (tq, tq), 0)
        qq = lax.broadcasted_iota(jnp.int32, (tq, tq), 1)
        return jnp.where(kk <= qq, s, -jnp.inf)

    def chain(u):
        qa = q_ref[0, u]

        def cin(j):
            return jnp.sum(jnp.where(lane == hb * n_heads + u, c_ref[0, j], 0.0), axis=-1, keepdims=True)

        ci = cin(i)

        def scores(j):
            return _dot_nt(k_ref[0, u, pl.ds(pl.multiple_of(j * tq, tq), tq), :], qa)

        def emit(out_t):
            o_ref[0, :, u * HEAD_DIM:(u + 1) * HEAD_DIM] = out_t.T.astype(bf16)

        return _Chain(scores, lambda j: vt_ref[0, u, j], lambda j: (ci - cin(j)) * LOG2E, diag_mask, emit,
                      bufs[2 * u], bufs[2 * u + 1])

    _flash_sweep(i, [chain(u) for u in range(n_heads)], tq)


def _fox(fq, fk, fvt, cin):
    n_batch, n_h, seq_len, _ = fq.shape
    tq = KV_BLOCK
    nkb = seq_len // tq
    n_heads = 2
    return pl.pallas_call(
        functools.partial(_fox_kernel, tq=tq, n_heads=n_heads),
        out_shape=jax.ShapeDtypeStruct((n_batch, seq_len, n_h * HEAD_DIM), bf16),
        grid=(n_batch, n_h // n_heads, nkb),
        in_specs=[
            pl.BlockSpec((1, n_heads, tq, 2 * HEAD_DIM), lambda b, h, i: (b, h, i, 0)),
            pl.BlockSpec((1, n_heads, seq_len, 2 * HEAD_DIM), lambda b, h, i: (b, h, 0, 0)),
            pl.BlockSpec((1, n_heads, nkb, HEAD_DIM, tq), lambda b, h, i: (b, h, 0, 0, 0)),
            pl.BlockSpec((1, nkb, 1, LANES), lambda b, h, i: (b, 0, 0, 0)),
        ],
        out_specs=pl.BlockSpec((1, tq, n_heads * HEAD_DIM), lambda b, h, i: (b, i, h)),
        scratch_shapes=[pltpu.VMEM((2, tq, tq), f32), pltpu.VMEM((2, tq, tq), bf16)] * n_heads,
        compiler_params=_params("arbitrary", "arbitrary", "arbitrary"),
        name="fox",
    )(fq, fk, fvt, cin)


def _cmp_kernel(q_ref, k_ref, v_ref, gate_ref, ov_ref, o_ref, sel_ref, *, tq, n_rows, n_cmp, n_sel):
    i = pl.program_id(2)
    g = pl.program_id(1)
    hg = NSA_GROUP
    q = q_ref[0].reshape(hg * tq, HEAD_DIM)
    s = _dot_nt(q, k_ref[0, 0]).reshape(hg, tq, n_rows)
    t = i * tq + lax.broadcasted_iota(jnp.int32, (1, tq, 1), 1)
    n = lax.broadcasted_iota(jnp.int32, (1, 1, n_rows), 2)
    valid = (n * CMP_STRIDE + (CMP_LEN - 1) <= t) & (n < n_cmp)
    s = jnp.where(valid, s, -jnp.inf)
    m = jnp.max(s, axis=-1, keepdims=True)
    m = jnp.where(m == -jnp.inf, 0.0, m)
    e = jnp.exp(s - m)
    p = e / jnp.maximum(jnp.sum(e, axis=-1, keepdims=True), 1e-30)
    o = _dot(p.reshape(hg * tq, n_rows).astype(bf16), v_ref[0, 0])
    _write_gated(o_ref, gate_ref, o, g, 0, tq)

    lane = lax.broadcasted_iota(jnp.int32, (tq, LANES), 1)
    psum = p[0] + p[1] + p[2] + p[3]
    imp = _dot_f32_by_exact(psum, ov_ref[...])
    tq_pos = i * tq + lax.broadcasted_iota(jnp.int32, (tq, 1), 0)
    q_blk = lax.shift_right_logical(tq_pos, SEL_SHIFT)
    causal = lane <= q_blk
    forced = (lane == 0) | (lane == q_blk) | (lane == q_blk - 1)
    key = jnp.where(forced, jnp.inf, jnp.where(causal, imp, -1.0))
    key = jnp.where(lane < n_sel, key, -3.0)
    lane_f = lane.astype(f32)
    sel = jnp.zeros((tq, LANES), jnp.bool_)
    for _ in range(min(SEL_TOPK, n_sel)):
        mx = jnp.max(key, axis=-1, keepdims=True)
        first = jnp.min(jnp.where(key == mx, lane_f, float(LANES)), axis=-1, keepdims=True)
        pick = lane_f == first
        sel = sel | pick
        key = jnp.where(pick, -2.0, key)
    sel_ref[0, 0] = jnp.where(sel & causal, 0.0, -MASK_BIG).astype(bf16)


def _cmp(nqu, kcmp, vcmp, gates, n_cmp):
    n_batch, _, seq_len, _ = nqu.shape
    n_rows = kcmp.shape[2]
    n_sel = seq_len // SEL_BLOCK
    assert n_sel <= SEL_LANES
    tq = 1024
    c0 = np.arange(n_rows) * CMP_STRIDE
    s0 = np.arange(SEL_LANES) * SEL_BLOCK
    overlap = np.clip(np.minimum(c0[:, None] + CMP_LEN, s0[None, :] + SEL_BLOCK)
                      - np.maximum(c0[:, None], s0[None, :]), 0, None).astype(np.float32) / CMP_LEN
    overlap[n_cmp:, :] = 0.0
    overlap[:, n_sel:] = 0.0
    return pl.pallas_call(
        functools.partial(_cmp_kernel, tq=tq, n_rows=n_rows, n_cmp=n_cmp, n_sel=n_sel),
        out_shape=[
            jax.ShapeDtypeStruct((n_batch, seq_len, NSA_W), bf16),
            jax.ShapeDtypeStruct((n_batch, NSA_KV_HEADS, seq_len, SEL_LANES), bf16),
        ],
        grid=(n_batch, NSA_KV_HEADS, seq_len // tq),
        in_specs=[
            pl.BlockSpec((1, NSA_GROUP, tq, HEAD_DIM), lambda b, g, i: (b, g, i, 0)),
            pl.BlockSpec((1, 1, n_rows, HEAD_DIM), lambda b, g, i: (b, g, 0, 0)),
            pl.BlockSpec((1, 1, n_rows, HEAD_DIM), lambda b, g, i: (b, g, 0, 0)),
            pl.BlockSpec((1, tq, LANES), lambda b, g, i: (b, i, 0)),
            pl.BlockSpec((n_rows, SEL_LANES), lambda b, g, i: (0, 0)),
        ],
        out_specs=[
            pl.BlockSpec((1, tq, NSA_GROUP * HEAD_DIM), lambda b, g, i: (b, i, g)),
            pl.BlockSpec((1, 1, tq, SEL_LANES), lambda b, g, i: (b, g, i, 0)),
        ],
        compiler_params=_params("arbitrary", "arbitrary", "arbitrary"),
        name="cmp",
    )(nqu, kcmp, vcmp, gates, jnp.asarray(overlap, dtype=bf16))


def _write_gated(o_ref, gate_ref, out, g, branch, tq):
    gates = gate_ref[0]
    lane = lax.broadcasted_iota(jnp.int32, (tq, LANES), 1)
    for h in range(NSA_GROUP):
        gcol = FOX_HEADS + (g * NSA_GROUP + h) * N_BRANCH + branch
        gh = jnp.sum(jnp.where(lane == gcol, gates, 0.0), axis=-1, keepdims=True)
        o_ref[0, :, h * HEAD_DIM:(h + 1) * HEAD_DIM] = (gh * out[h * tq:(h + 1) * tq]).astype(bf16)


def _slc_kernel(q_ref, sel_ref, k_ref, vt_ref, gate_ref, o_ref, *bufs, tq, tk):
    i = pl.program_id(1)
    hg = NSA_GROUP
    n_q = hg * tq
    q0 = i * tq
    jd = q0 // tk

    def diag_mask(s):
        kp = jd * tk + lax.broadcasted_iota(jnp.int32, (tk, n_q), 0)
        t = q0 + (lax.broadcasted_iota(jnp.int32, (tk, n_q), 1) & (tq - 1))
        return jnp.where(kp <= t, s, -jnp.inf)

    def chain(g):
        q = q_ref[0, g * hg:(g + 1) * hg].reshape(n_q, HEAD_DIM)
        qa = jnp.concatenate([q, jnp.concatenate([sel_ref[0, g]] * hg, axis=0)], axis=1)

        def scores(j):
            return _dot_nt(k_ref[0, g, pl.ds(pl.multiple_of(j * tk, tk), tk), :], qa)

        def emit(out):
            gates = gate_ref[0]
            lane = lax.broadcasted_iota(jnp.int32, (tq, LANES), 1)
            for h in range(hg):
                head = g * hg + h
                gh = jnp.sum(jnp.where(lane == FOX_HEADS + head * N_BRANCH + 1, gates, 0.0),
                             axis=-1, keepdims=True)
                o_ref[0, :, head * HEAD_DIM:(head + 1) * HEAD_DIM] = (
                    gh * out[:, h * tq:(h + 1) * tq].T).astype(bf16)

        return _Chain(scores, lambda j: vt_ref[0, g, j], lambda j: None, diag_mask, emit,
                      bufs[2 * g], bufs[2 * g + 1])

    _flash_sweep(jd, [chain(g) for g in range(NSA_KV_HEADS)], n_q)


def _slc(nqr, selneg, kaug, vst, gates):
    n_batch, _, seq_len, _ = nqr.shape
    tq, tk = 128, KV_BLOCK
    nkb = seq_len // tk
    assert tq & (tq - 1) == 0
    n_q = NSA_GROUP * tq
    return pl.pallas_call(
        functools.partial(_slc_kernel, tq=tq, tk=tk),
        out_shape=jax.ShapeDtypeStruct((n_batch, seq_len, NSA_W), bf16),
        grid=(n_batch, seq_len // tq),
        in_specs=[
            pl.BlockSpec((1, NSA_HEADS, tq, HEAD_DIM), lambda b, i: (b, 0, i, 0)),
            pl.BlockSpec((1, NSA_KV_HEADS, tq, SEL_LANES), lambda b, i: (b, 0, i, 0)),
            pl.BlockSpec((1, NSA_KV_HEADS, seq_len, HEAD_DIM + SEL_LANES), lambda b, i: (b, 0, 0, 0)),
            pl.BlockSpec((1, NSA_KV_HEADS, nkb, HEAD_DIM, tk), lambda b, i: (b, 0, 0, 0, 0)),
            pl.BlockSpec((1, tq, LANES), lambda b, i: (b, i, 0)),
        ],
        out_specs=pl.BlockSpec((1, tq, NSA_W), lambda b, i: (b, i, 0)),
        scratch_shapes=[pltpu.VMEM((2, tk, n_q), f32), pltpu.VMEM((2, tk, n_q), bf16)] * NSA_KV_HEADS,
        compiler_params=_params("arbitrary", "arbitrary"),
        name="slc",
    )(nqr, selneg, kaug, vst, gates)


def _win_kernel(q_ref, k_ref, vt_ref, gate_ref, o_ref, *, tq, n_sub):
    i = pl.program_id(2)
    g = pl.program_id(1)
    hg = NSA_GROUP
    span = WINDOW + tq
    q0s = [(i * n_sub + u) * tq for u in range(n_sub)]
    k0s = [pl.multiple_of(jnp.maximum(q0 - WINDOW, 0), tq) for q0 in q0s]
    scores = []
    for u in range(n_sub):
        q = q_ref[0, :, u * tq:(u + 1) * tq, :].reshape(hg * tq, HEAD_DIM)
        scores.append(_dot_nt(k_ref[0, 0, pl.ds(k0s[u], span), :], q))
    probs, denoms = [], []
    for u in range(n_sub):
        kp = k0s[u] + lax.broadcasted_iota(jnp.int32, (span, tq), 0)
        t = q0s[u] + lax.broadcasted_iota(jnp.int32, (span, tq), 1)
        diff = t - kp
        bias = jnp.where((diff >= 0) & (diff < WINDOW), 0.0, -jnp.inf)
        s = jnp.concatenate([scores[u][:, h * tq:(h + 1) * tq] + bias for h in range(hg)], axis=1)
        e = jnp.exp2(s - jnp.max(s, axis=0, keepdims=True))
        denoms.append(jnp.sum(e, axis=0, keepdims=True))
        probs.append(e.astype(bf16))
    gates = gate_ref[0]
    lane = lax.broadcasted_iota(jnp.int32, (tq, LANES), 1)
    for u in range(n_sub):
        jb = k0s[u] // tq
        vt = jnp.concatenate([vt_ref[0, 0, jb + c] for c in range(span // tq)], axis=1)
        out = _dot(vt, probs[u]) / denoms[u]
        for h in range(hg):
            gcol = FOX_HEADS + (g * hg + h) * N_BRANCH + 2
            gh = jnp.sum(jnp.where(lane == gcol, gates[u * tq:(u + 1) * tq], 0.0), axis=-1, keepdims=True)
            o_ref[0, u * tq:(u + 1) * tq, h * HEAD_DIM:(h + 1) * HEAD_DIM] = (
                gh * out[:, h * tq:(h + 1) * tq].T).astype(bf16)


def _win(nqr, kw, vwt, gates):
    n_batch, _, seq_len, _ = nqr.shape
    tq, n_sub = WIN_TILE, 2
    assert seq_len >= WINDOW + tq and WINDOW % tq == 0
    return pl.pallas_call(
        functools.partial(_win_kernel, tq=tq, n_sub=n_sub),
        out_shape=jax.ShapeDtypeStruct((n_batch, seq_len, NSA_W), bf16),
        grid=(n_batch, NSA_KV_HEADS, seq_len // (tq * n_sub)),
        in_specs=[
            pl.BlockSpec((1, NSA_GROUP, tq * n_sub, HEAD_DIM), lambda b, g, i: (b, g, i, 0)),
            pl.BlockSpec((1, 1, seq_len, HEAD_DIM), lambda b, g, i: (b, g, 0, 0)),
            pl.BlockSpec((1, 1, seq_len // tq, HEAD_DIM, tq), lambda b, g, i: (b, g, 0, 0, 0)),
            pl.BlockSpec((1, tq * n_sub, LANES), lambda b, g, i: (b, i, 0)),
        ],
        out_specs=pl.BlockSpec((1, tq * n_sub, NSA_GROUP * HEAD_DIM), lambda b, g, i: (b, i, g)),
        compiler_params=_params("arbitrary", "arbitrary", "arbitrary"),
        name="win",
    )(nqr, kw, vwt, gates)


def _outproj_kernel(fox_ref, c_ref, s_ref, w_ref, x_ref, mod_ref, wo_ref, o_ref):
    nsa = c_ref[...].astype(f32) + s_ref[...].astype(f32) + w_ref[...].astype(f32)
    a = jnp.concatenate([fox_ref[...], nsa.astype(bf16)], axis=1)
    o_ref[...] = x_ref[...] + mod_ref[0][2:3] * _dot(a, wo_ref[...])


def _outproj(ofox, ocmp, oslc, owin, x2d, mod3, wo_bf16, seq_len):
    m_rows = x2d.shape[0]
    tm = 512
    per_b = seq_len // tm
    half = pl.BlockSpec((tm, FOX_W), lambda i: (i, 0))
    return pl.pallas_call(
        _outproj_kernel,
        out_shape=jax.ShapeDtypeStruct((m_rows, D_MODEL), f32),
        grid=(m_rows // tm,),
        in_specs=[
            half, half, half, half,
            pl.BlockSpec((tm, D_MODEL), lambda i: (i, 0)),
            pl.BlockSpec((1, 6, D_MODEL), lambda i: (i // per_b, 0, 0)),
            pl.BlockSpec((D_MODEL, D_MODEL), lambda i: (0, 0)),
        ],
        out_specs=pl.BlockSpec((tm, D_MODEL), lambda i: (i, 0)),
        compiler_params=_params("arbitrary"),
        name="outproj",
    )(ofox, ocmp, oslc, owin, x2d, mod3, wo_bf16)


def _mlp_kernel(x_ref, mod_ref, g_ref, wu_ref, wd_ref, o_ref, h_ref, acc_ref):
    f = pl.program_id(1)

    @pl.when(f == 0)
    def _():
        md = mod_ref[0]
        y = _rms(x_ref[...], g_ref[...])
        h_ref[...] = (y * (1.0 + md[4:5]) + md[3:4]).astype(bf16)
        acc_ref[...] = jnp.zeros_like(acc_ref)

    u = jnp.maximum(_dot(h_ref[...], wu_ref[...]), 0.0)
    acc_ref[...] += _dot((u * u).astype(bf16), wd_ref[...])

    @pl.when(f == pl.num_programs(1) - 1)
    def _():
        o_ref[...] = x_ref[...] + mod_ref[0][5:6] * acc_ref[...]


def _mlp(x2d, mod3, norm_g, wu_bf16, wd_bf16, seq_len):
    m_rows = x2d.shape[0]
    tm, tf = 512, 1024
    per_b = seq_len // tm
    return pl.pallas_call(
        _mlp_kernel,
        out_shape=jax.ShapeDtypeStruct((m_rows, D_MODEL), f32),
        grid=(m_rows // tm, D_FF // tf),
        in_specs=[
            pl.BlockSpec((tm, D_MODEL), lambda i, f: (i, 0)),
            pl.BlockSpec((1, 6, D_MODEL), lambda i, f: (i // per_b, 0, 0)),
            pl.BlockSpec((1, D_MODEL), lambda i, f: (0, 0)),
            pl.BlockSpec((D_MODEL, tf), lambda i, f: (0, f)),
            pl.BlockSpec((tf, D_MODEL), lambda i, f: (f, 0)),
        ],
        out_specs=pl.BlockSpec((tm, D_MODEL), lambda i, f: (i, 0)),
        scratch_shapes=[pltpu.VMEM((tm, D_MODEL), bf16), pltpu.VMEM((tm, D_MODEL), f32)],
        compiler_params=_params("arbitrary", "arbitrary"),
        name="mlp",
    )(x2d, mod3, norm_g, wu_bf16, wd_bf16)


def _layer(x, c, w_ada, b_ada, norm1_g, w_in, b_forget, fox_q_norm, fox_k_norm, nsa_q_norm,
           cmp_k_norm, slc_k_norm, win_k_norm, cmp_pe_k, cmp_w1_k, cmp_w2_k, cmp_pe_v, cmp_w1_v,
           cmp_w2_v, w_out, norm2_g, w_up, w_down):
    n_batch, seq_len, _ = x.shape
    n_cmp = (seq_len - CMP_LEN) // CMP_STRIDE + 1
    row = lambda v: v.reshape(1, -1)

    z0 = COL_NQ
    nq0 = z0 + FOX_HEADS
    gz0 = nq0 + NSA_W + 6 * KV_W
    w_perm = jnp.concatenate([
        w_in[:, :z0].astype(bf16), w_in[:, nq0:gz0].astype(bf16), w_in[:, z0:nq0].astype(bf16),
        w_in[:, gz0:].astype(bf16),
        jnp.zeros((D_MODEL, LANES - FOX_HEADS - N_BRANCH * NSA_HEADS), bf16)], axis=1)
    half = HEAD_DIM // 2
    inv_freq = ROPE_THETA ** (-jnp.arange(half, dtype=f32) / half)
    inv_freq = jnp.concatenate([inv_freq, inv_freq]).reshape(1, HEAD_DIM)

    mod3 = _ada(c, w_ada, b_ada).reshape(n_batch, 6, D_MODEL)
    x2d = x.reshape(n_batch * seq_len, D_MODEL)
    proj, small = _proj(x2d, mod3, row(norm1_g), w_perm, seq_len)
    (fq, fk, fvt, cin, nqu, nqr, kc, vc, kaug, vst, kw, vw, gates) = _prep(
        proj, small, n_batch, seq_len, row(fox_q_norm), row(fox_k_norm), row(nsa_q_norm), row(slc_k_norm),
        row(win_k_norm), jnp.pad(b_forget, (0, LANES - FOX_HEADS)).reshape(1, LANES), inv_freq)
    kcmp = _compress(kc, cmp_pe_k, cmp_w1_k.astype(bf16), cmp_w2_k.astype(bf16), row(cmp_k_norm), True)
    vcmp = _compress(vc, cmp_pe_v, cmp_w1_v.astype(bf16), cmp_w2_v.astype(bf16), row(cmp_k_norm), False)
    ofox = _fox(fq, fk, fvt, cin)
    ocmp, selneg = _cmp(nqu, kcmp, vcmp, gates, n_cmp)
    oslc = _slc(nqr, selneg, kaug, vst, gates)
    owin = _win(nqr, kw, vw, gates)
    x1 = _outproj(ofox.reshape(-1, FOX_W), ocmp.reshape(-1, NSA_W), oslc.reshape(-1, NSA_W),
                  owin.reshape(-1, NSA_W), x2d, mod3, w_out.astype(bf16), seq_len)
    x2 = _mlp(x1, mod3, row(norm2_g), w_up.astype(bf16), w_down.astype(bf16), seq_len)
    return x2.reshape(n_batch, seq_len, D_MODEL)


def kernel(x, c, w_ada, b_ada, norm1_g, w_in, b_forget, fox_q_norm, fox_k_norm, nsa_q_norm, cmp_k_norm,
           slc_k_norm, win_k_norm, cmp_pe_k, cmp_w1_k, cmp_w2_k, cmp_pe_v, cmp_w1_v, cmp_w2_v, w_out,
           norm2_g, w_up, w_down):
    depth = w_ada.shape[0]
    for l in range(depth):
        x = _layer(x, c, w_ada[l], b_ada[l], norm1_g[l], w_in[l], b_forget[l], fox_q_norm[l], fox_k_norm[l],
                   nsa_q_norm[l], cmp_k_norm[l], slc_k_norm[l], win_k_norm[l], cmp_pe_k[l], cmp_w1_k[l],
                   cmp_w2_k[l], cmp_pe_v[l], cmp_w1_v[l], cmp_w2_v[l], w_out[l], norm2_g[l], w_up[l],
                   w_down[l])
    return x
```

```python
import functools
import math
from typing import Any, Callable, NamedTuple

import numpy as np
import jax
import jax.numpy as jnp
from jax import lax
from jax.experimental import pallas as pl
from jax.experimental.pallas import tpu as pltpu

D_MODEL = 2048
HEAD_DIM = 128
FOX_HEADS = 8
NSA_HEADS = 8
NSA_KV_HEADS = 2
NSA_GROUP = NSA_HEADS // NSA_KV_HEADS
N_BRANCH = 3
D_FF = 4 * D_MODEL
ROPE_THETA = 10000.0
CMP_LEN = 32
CMP_STRIDE = 16
CMP_HIDDEN = 2 * HEAD_DIM
SEL_BLOCK = 64
SEL_SHIFT = 6
SEL_TOPK = 16
WINDOW = 512
NORM_EPS = 1e-6
ATTN_SCALE = HEAD_DIM ** -0.5
FOX_W = FOX_HEADS * HEAD_DIM
NSA_W = NSA_HEADS * HEAD_DIM
KV_W = NSA_KV_HEADS * HEAD_DIM

LANES = 128
SEL_LANES = LANES
MASK_BIG = 1e30
KV_BLOCK = 512
WIN_TILE = 128
LOG2E = math.log2(math.e)

COL_FQ = 0
COL_FK = COL_FQ + FOX_W
COL_FV = COL_FK + FOX_W
COL_NQ = COL_FV + FOX_W
COL_KC = COL_NQ + NSA_W
COL_VC = COL_KC + KV_W
COL_KS = COL_VC + KV_W
COL_VS = COL_KS + KV_W
COL_KW = COL_VS + KV_W
COL_VW = COL_KW + KV_W
COL_SMALL = COL_VW + KV_W

VMEM_LIMIT = 56 * 1024 * 1024

f32 = jnp.float32
bf16 = jnp.bfloat16


def _params(*sem):
    return pltpu.CompilerParams(dimension_semantics=sem, vmem_limit_bytes=VMEM_LIMIT)


def _dot_nt(a, b):
    return lax.dot_general(a, b, (((1,), (1,)), ((), ())), preferred_element_type=f32)


def _dot(a, b):
    return jnp.dot(a, b, preferred_element_type=f32)


def _split3(x):
    hi = x.astype(bf16)
    r1 = x - hi.astype(f32)
    mid = r1.astype(bf16)
    lo = (r1 - mid.astype(f32)).astype(bf16)
    return hi, mid, lo


def _dot_f32_by_exact(x, w_bf16):
    hi, mid, lo = _split3(x)
    return _dot(hi, w_bf16) + (_dot(mid, w_bf16) + _dot(lo, w_bf16))


def _rms(x, gain):
    ms = jnp.mean(x * x, axis=-1, keepdims=True)
    return x * lax.rsqrt(ms + NORM_EPS) * gain


def _ada_kernel(ct_ref, w_ref, b_ref, o_ref, *, n_batch, k_chunk):
    ct = ct_ref[...]
    act = ct * jax.nn.sigmoid(ct)
    rows = []
    for b in range(n_batch):
        col = act[:, b:b + 1]
        acc = b_ref[...]
        for k0 in range(0, D_MODEL, k_chunk):
            acc = acc + jnp.sum(w_ref[k0:k0 + k_chunk, :] * col[k0:k0 + k_chunk], axis=0, keepdims=True)
        rows.append(acc)
    o_ref[...] = jnp.concatenate(rows, axis=0)


def _ada(c, w_ada, b_ada):
    n_batch = c.shape[0]
    n_out = w_ada.shape[1]
    tn = 1024
    return pl.pallas_call(
        functools.partial(_ada_kernel, n_batch=n_batch, k_chunk=256),
        out_shape=jax.ShapeDtypeStruct((n_batch, n_out), f32),
        grid=(n_out // tn,),
        in_specs=[
            pl.BlockSpec((D_MODEL, n_batch), lambda j: (0, 0)),
            pl.BlockSpec((D_MODEL, tn), lambda j: (0, j)),
            pl.BlockSpec((1, tn), lambda j: (0, j)),
        ],
        out_specs=pl.BlockSpec((n_batch, tn), lambda j: (0, j)),
        compiler_params=_params("arbitrary"),
        name="ada",
    )(c.T, w_ada, b_ada.reshape(1, n_out))


def _proj_kernel(x_ref, mod_ref, g_ref, wa_ref, wb_ref, ws_ref, wu32_ref, wd32_ref,
                 o_ref, small_ref, wu16_ref, wd16_ref, h_ref, *, na, nb):
    j = pl.program_id(1)
    wu16_ref[...] = wu32_ref[...].astype(bf16)
    wd16_ref[...] = wd32_ref[...].astype(bf16)

    @pl.when(j == 0)
    def _():
        md = mod_ref[0]
        y = _rms(x_ref[...], g_ref[...])
        h_ref[...] = (y * (1.0 + md[1:2]) + md[0:1]).astype(bf16)

    @pl.when(j < na)
    def _():
        o_ref[...] = _dot(h_ref[...], wa_ref[...]).astype(bf16)

    @pl.when((j >= na) & (j < na + nb))
    def _():
        o_ref[...] = _dot(h_ref[...], wb_ref[...]).astype(bf16)

    @pl.when(j == na + nb)
    def _():
        small_ref[...] = _dot(h_ref[...], ws_ref[...])


def _proj(x2d, mod3, norm_g, wa, wb, ws, w_up, w_down, seq_len):
    m_rows = x2d.shape[0]
    tm, tn = 1024, 512
    assert seq_len % tm == 0 and wa.shape[1] % tn == 0 and wb.shape[1] % tn == 0
    na, nb = wa.shape[1] // tn, wb.shape[1] // tn
    per_b = seq_len // tm
    n_j = na + nb + 1
    n_steps = (m_rows // tm) * n_j
    cast_block = next(c for c in range(LANES, D_FF + 1, LANES) if D_FF % c == 0 and D_FF // c <= n_steps)
    n_cast = D_FF // cast_block
    cast_idx = lambda i, j: jnp.minimum(i * n_j + j, n_cast - 1)
    return pl.pallas_call(
        functools.partial(_proj_kernel, na=na, nb=nb),
        out_shape=[jax.ShapeDtypeStruct((m_rows, (na + nb) * tn), bf16),
                   jax.ShapeDtypeStruct((m_rows, LANES), f32),
                   jax.ShapeDtypeStruct(w_up.shape, bf16), jax.ShapeDtypeStruct(w_down.shape, bf16)],
        grid=(m_rows // tm, n_j),
        in_specs=[
            pl.BlockSpec((tm, D_MODEL), lambda i, j: (i, 0)),
            pl.BlockSpec((1, 6, D_MODEL), lambda i, j: (i // per_b, 0, 0)),
            pl.BlockSpec((1, D_MODEL), lambda i, j: (0, 0)),
            pl.BlockSpec((D_MODEL, tn), lambda i, j: (0, jnp.minimum(j, na - 1))),
            pl.BlockSpec((D_MODEL, tn), lambda i, j: (0, jnp.clip(j - na, 0, nb - 1))),
            pl.BlockSpec((D_MODEL, LANES), lambda i, j: (0, 0)),
            pl.BlockSpec((D_MODEL, cast_block), lambda i, j: (0, cast_idx(i, j))),
            pl.BlockSpec((cast_block, D_MODEL), lambda i, j: (cast_idx(i, j), 0)),
        ],
        out_specs=[pl.BlockSpec((tm, tn), lambda i, j: (i, jnp.minimum(j, na + nb - 1))),
                   pl.BlockSpec((tm, LANES), lambda i, j: (i, 0)),
                   pl.BlockSpec((D_MODEL, cast_block), lambda i, j: (0, cast_idx(i, j))),
                   pl.BlockSpec((cast_block, D_MODEL), lambda i, j: (cast_idx(i, j), 0))],
        scratch_shapes=[pltpu.VMEM((tm, D_MODEL), bf16)],
        compiler_params=_params("arbitrary", "arbitrary"),
        name="proj",
    )(x2d, mod3, norm_g, wa, wb, ws, w_up, w_down)


def _prep_kernel(p_ref, small_ref, gq_ref, gk_ref, gn_ref, gs_ref, gw_ref, bf_ref, inv_ref,
                 fq_ref, fk_ref, fvt_ref, cin_ref, nqu_ref, nqr_ref, kc_ref, vc_ref,
                 ks_ref, vst_ref, kw_ref, vw_ref, gate_ref, carry_ref, *, tm):
    i = pl.program_id(1)

    @pl.when(i == 0)
    def _():
        carry_ref[...] = jnp.zeros_like(carry_ref)

    def head(col, h):
        return p_ref[:, col + h * HEAD_DIM: col + (h + 1) * HEAD_DIM].astype(f32)

    row = lax.broadcasted_iota(jnp.int32, (tm, LANES), 0)
    lane = lax.broadcasted_iota(jnp.int32, (tm, LANES), 1)
    pos = i * tm + row
    ang = pos.astype(f32) * inv_ref[...]
    cos = jnp.cos(ang)
    sin = jnp.sin(ang)
    sin_signed = jnp.where(lane < HEAD_DIM // 2, -sin, sin)

    def rope(x):
        return x * cos + pltpu.roll(x, HEAD_DIM // 2, 1) * sin_signed

    small = small_ref[...]
    z = small + bf_ref[...]
    logf = jnp.minimum(z, 0.0) - jnp.log1p(jnp.exp(-jnp.abs(z)))
    t_idx = lax.broadcasted_iota(jnp.int32, (tm, tm), 0)
    s_idx = lax.broadcasted_iota(jnp.int32, (tm, tm), 1)
    tri = jnp.where(s_idx <= t_idx, 1.0, 0.0).astype(bf16)
    hi, mid, lo = _split3(logf)
    local = _dot(tri, hi) + (_dot(tri, mid) + _dot(tri, lo))
    cin_ref[0, 0] = carry_ref[0:1, :]
    carry_ref[...] = carry_ref[...] + local[tm - 1:tm, :]
    b_hi, b_mid, b_lo = (v.astype(f32) for v in _split3(local * (-LOG2E)))
    ones3 = jnp.where(lane < 3, 1.0, 0.0).astype(bf16)

    for h in range(FOX_HEADS):
        q = _rms(head(COL_FQ, h), gq_ref[...]) * (ATTN_SCALE * LOG2E)
        fq_ref[0, h] = jnp.concatenate([q.astype(bf16), ones3], axis=1)
        k = _rms(head(COL_FK, h), gk_ref[...])
        bias = jnp.where(lane == 0, b_hi[:, h:h + 1],
                         jnp.where(lane == 1, b_mid[:, h:h + 1],
                                   jnp.where(lane == 2, b_lo[:, h:h + 1], 0.0)))
        fk_ref[0, h] = jnp.concatenate([k.astype(bf16), bias.astype(bf16)], axis=1)
        fvt_ref[0, h, 0] = head(COL_FV, h).T.astype(bf16)

    gate_ref[0] = jax.nn.sigmoid(small)

    for h in range(NSA_HEADS):
        qn = _rms(head(COL_NQ, h), gn_ref[...])
        nqu_ref[0, h] = (qn * ATTN_SCALE).astype(bf16)
        nqr_ref[0, h] = rope(qn * (ATTN_SCALE * LOG2E)).astype(bf16)
    onehot = jnp.where(lane == lax.shift_right_logical(pos, SEL_SHIFT), 1.0, 0.0).astype(bf16)
    for g in range(NSA_KV_HEADS):
        kc_ref[0, g] = head(COL_KC, g).astype(bf16)
        vc_ref[0, g] = head(COL_VC, g).astype(bf16)
        ks = rope(_rms(head(COL_KS, g), gs_ref[...])).astype(bf16)
        ks_ref[0, g] = jnp.concatenate([ks, onehot], axis=1)
        vst_ref[0, g, 0] = head(COL_VS, g).T.astype(bf16)
        kw_ref[0, g] = rope(_rms(head(COL_KW, g), gw_ref[...])).astype(bf16)
        vwt = head(COL_VW, g).T.astype(bf16)
        for c in range(tm // WIN_TILE):
            vw_ref[0, g, c] = vwt[:, c * WIN_TILE:(c + 1) * WIN_TILE]


def _prep(proj, small, n_batch, seq_len, gq, gk, gn, gs, gw, b_forget_row, inv_freq):
    tm = KV_BLOCK
    per_b = seq_len // tm
    hshape = lambda n, w, dt: jax.ShapeDtypeStruct((n_batch, n, seq_len, w), dt)
    hspec = lambda n, w: pl.BlockSpec((1, n, tm, w), lambda b, i: (b, 0, i, 0))
    tshape = lambda n: jax.ShapeDtypeStruct((n_batch, n, per_b, HEAD_DIM, tm), bf16)
    tspec = lambda n: pl.BlockSpec((1, n, 1, HEAD_DIM, tm), lambda b, i: (b, 0, i, 0, 0))
    vec = pl.BlockSpec((1, LANES), lambda b, i: (0, 0))
    return pl.pallas_call(
        functools.partial(_prep_kernel, tm=tm),
        out_shape=[
            hshape(FOX_HEADS, 2 * HEAD_DIM, bf16), hshape(FOX_HEADS, 2 * HEAD_DIM, bf16), tshape(FOX_HEADS),
            jax.ShapeDtypeStruct((n_batch, per_b, 1, LANES), f32),
            hshape(NSA_HEADS, HEAD_DIM, bf16), hshape(NSA_HEADS, HEAD_DIM, bf16),
            hshape(NSA_KV_HEADS, HEAD_DIM, bf16), hshape(NSA_KV_HEADS, HEAD_DIM, bf16),
            hshape(NSA_KV_HEADS, HEAD_DIM + SEL_LANES, bf16), tshape(NSA_KV_HEADS),
            hshape(NSA_KV_HEADS, HEAD_DIM, bf16),
            jax.ShapeDtypeStruct((n_batch, NSA_KV_HEADS, seq_len // WIN_TILE, HEAD_DIM, WIN_TILE), bf16),
            jax.ShapeDtypeStruct((n_batch, seq_len, LANES), f32),
        ],
        grid=(n_batch, per_b),
        in_specs=[
            pl.BlockSpec((tm, COL_SMALL), lambda b, i: (b * per_b + i, 0)),
            pl.BlockSpec((tm, LANES), lambda b, i: (b * per_b + i, 0)),
            vec, vec, vec, vec, vec, vec, vec,
        ],
        out_specs=[
            hspec(FOX_HEADS, 2 * HEAD_DIM), hspec(FOX_HEADS, 2 * HEAD_DIM), tspec(FOX_HEADS),
            pl.BlockSpec((1, 1, 1, LANES), lambda b, i: (b, i, 0, 0)),
            hspec(NSA_HEADS, HEAD_DIM), hspec(NSA_HEADS, HEAD_DIM),
            hspec(NSA_KV_HEADS, HEAD_DIM), hspec(NSA_KV_HEADS, HEAD_DIM),
            hspec(NSA_KV_HEADS, HEAD_DIM + SEL_LANES), tspec(NSA_KV_HEADS),
            hspec(NSA_KV_HEADS, HEAD_DIM),
            pl.BlockSpec((1, NSA_KV_HEADS, tm // WIN_TILE, HEAD_DIM, WIN_TILE), lambda b, i: (b, 0, i, 0, 0)),
            pl.BlockSpec((1, tm, LANES), lambda b, i: (b, i, 0)),
        ],
        scratch_shapes=[pltpu.VMEM((8, LANES), f32)],
        compiler_params=_params("arbitrary", "arbitrary"),
        name="prep",
    )(proj, small, gq, gk, gn, gs, gw, b_forget_row, inv_freq)


def _compress_kernel(x_ref, pe_ref, w1_ref, w2_ref, g_ref, o_ref, *, n_rows, do_norm):
    half = CMP_STRIDE * HEAD_DIM
    x = x_ref[0, 0]
    pe = pe_ref[...]
    xa = (x + pe[:, :half]).astype(bf16)
    xb = (x + pe[:, half:]).astype(bf16)
    a = _dot(xa, w1_ref[0:half, :])
    b = _dot(xb, w1_ref[half:2 * half, :])
    pre = a + pltpu.roll(b, n_rows - 1, 0)
    hid = pre * jax.nn.sigmoid(pre)
    out = _dot(hid.astype(bf16), w2_ref[...])
    if do_norm:
        out = _rms(out, g_ref[...])
    o_ref[0, 0] = out.astype(bf16)


def _compress(x4, pe, w1_bf16, w2_bf16, gain, do_norm):
    n_batch, n_g, seq_len, _ = x4.shape
    n_rows = seq_len // CMP_STRIDE
    half = CMP_STRIDE * HEAD_DIM
    xr = x4.reshape(n_batch, n_g, n_rows, half)
    return pl.pallas_call(
        functools.partial(_compress_kernel, n_rows=n_rows, do_norm=do_norm),
        out_shape=jax.ShapeDtypeStruct((n_batch, n_g, n_rows, HEAD_DIM), bf16),
        grid=(n_batch, n_g),
        in_specs=[
            pl.BlockSpec((1, 1, n_rows, half), lambda b, g: (b, g, 0, 0)),
            pl.BlockSpec((1, 2 * half), lambda b, g: (0, 0)),
            pl.BlockSpec((2 * half, CMP_HIDDEN), lambda b, g: (0, 0)),
            pl.BlockSpec((CMP_HIDDEN, HEAD_DIM), lambda b, g: (0, 0)),
            pl.BlockSpec((1, HEAD_DIM), lambda b, g: (0, 0)),
        ],
        out_specs=pl.BlockSpec((1, 1, n_rows, HEAD_DIM), lambda b, g: (b, g, 0, 0)),
        compiler_params=_params("arbitrary", "arbitrary"),
        name="compress",
    )(xr, pe.reshape(1, 2 * half), w1_bf16, w2_bf16, gain)


def _softmax_block(s, d, m_prev, l_prev):
    m_blk = jnp.max(s, axis=0, keepdims=True)
    m_new = jnp.maximum(m_prev, m_blk if d is None else m_blk + d)
    p = jnp.exp2(s - (m_new if d is None else m_new - d))
    alpha = jnp.exp2(m_prev - m_new)
    l_new = alpha * l_prev + jnp.sum(p, axis=0, keepdims=True)
    return m_new, l_new, alpha, p.astype(bf16)


class _Chain(NamedTuple):
    scores: Callable
    values: Callable
    offset: Callable
    diag_mask: Callable
    emit: Callable
    s_buf: Any
    p_buf: Any


def _flash_sweep(n_full, chains, n_q):
    for c in chains:
        c.s_buf[0] = c.scores(0)
        c.p_buf[1] = jnp.zeros(c.p_buf.shape[1:], c.p_buf.dtype)

    def accumulate(c, j, slot, alpha, acc):
        return alpha * acc + _dot(c.values(jnp.maximum(j, 0)), c.p_buf[slot])

    def step(j, cur, states):
        out = []
        for c, (m, l, acc, alpha_prev) in zip(chains, states):
            acc = accumulate(c, j - 1, 1 - cur, alpha_prev, acc)
            m, l, alpha, p = _softmax_block(c.s_buf[cur], c.offset(j), m, l)
            c.p_buf[cur] = p
            c.s_buf[1 - cur] = c.scores(j + 1)
            out.append((m, l, acc, alpha))
        return tuple(out)

    def finish(cur, states):
        for c, (m, l, acc, alpha_prev) in zip(chains, states):
            acc = accumulate(c, n_full - 1, 1 - cur, alpha_prev, acc)
            m, l, alpha, p = _softmax_block(c.diag_mask(c.s_buf[cur]), None, m, l)
            acc = alpha * acc + _dot(c.values(n_full), p)
            c.emit(acc / l)

    init = tuple((jnp.full((1, n_q), -jnp.inf, f32), jnp.zeros((1, n_q), f32),
                  jnp.zeros((HEAD_DIM, n_q), f32), jnp.ones((1, n_q), f32)) for _ in chains)
    states = lax.fori_loop(0, n_full // 2, lambda jj, st: step(2 * jj + 1, 1, step(2 * jj, 0, st)), init)

    @pl.when(n_full % 2 == 0)
    def _():
        finish(0, states)

    @pl.when(n_full % 2 == 1)
    def _():
        finish(1, step(n_full - 1, 0, states))


def _fox_kernel(q_ref, k_ref, vt_ref, c_ref, o_ref, *bufs, tq, n_heads):
    hb = pl.program_id(1)
    i = pl.program_id(2)
    lane = lax.broadcasted_iota(jnp.int32, (1, LANES), 1)

    def diag_mask(s):
        kk = lax.broadcasted_iota(jnp.int32, (tq, tq), 0)
        qq = lax.broadcasted_iota(jnp.int32, (tq, tq), 1)
        return jnp.where(kk <= qq, s, -jnp.inf)

    def chain(u):
        qa = q_ref[0, u]

        def cin(j):
            return jnp.sum(jnp.where(lane == hb * n_heads + u, c_ref[0, j], 0.0), axis=-1, keepdims=True)

        ci = cin(i)

        def scores(j):
            return _dot_nt(k_ref[0, u, pl.ds(pl.multiple_of(j * tq, tq), tq), :], qa)

        def emit(out_t):
            o_ref[0, :, u * HEAD_DIM:(u + 1) * HEAD_DIM] = out_t.T.astype(bf16)

        return _Chain(scores, lambda j: vt_ref[0, u, j], lambda j: (ci - cin(j)) * LOG2E, diag_mask, emit,
                      bufs[2 * u], bufs[2 * u + 1])

    _flash_sweep(i, [chain(u) for u in range(n_heads)], tq)


def _fox(fq, fk, fvt, cin):
    n_batch, n_h, seq_len, _ = fq.shape
    tq = KV_BLOCK
    nkb = seq_len // tq
    n_heads = 2
    return pl.pallas_call(
        functools.partial(_fox_kernel, tq=tq, n_heads=n_heads),
        out_shape=jax.ShapeDtypeStruct((n_batch, seq_len, n_h * HEAD_DIM), bf16),
        grid=(n_batch, n_h // n_heads, nkb),
        in_specs=[
            pl.BlockSpec((1, n_heads, tq, 2 * HEAD_DIM), lambda b, h, i: (b, h, i, 0)),
            pl.BlockSpec((1, n_heads, seq_len, 2 * HEAD_DIM), lambda b, h, i: (b, h, 0, 0)),
            pl.BlockSpec((1, n_heads, nkb, HEAD_DIM, tq), lambda b, h, i: (b, h, 0, 0, 0)),
            pl.BlockSpec((1, nkb, 1, LANES), lambda b, h, i: (b, 0, 0, 0)),
        ],
        out_specs=pl.BlockSpec((1, tq, n_heads * HEAD_DIM), lambda b, h, i: (b, i, h)),
        scratch_shapes=[pltpu.VMEM((2, tq, tq), f32), pltpu.VMEM((2, tq, tq), bf16)] * n_heads,
        compiler_params=_params("arbitrary", "arbitrary", "arbitrary"),
        name="fox",
    )(fq, fk, fvt, cin)


def _cmp_kernel(q_ref, k_ref, v_ref, gate_ref, ov_ref, o_ref, sel_ref, *, tq, n_rows, n_cmp, n_sel):
    i = pl.program_id(2)
    g = pl.program_id(1)
    hg = NSA_GROUP
    q = q_ref[0].reshape(hg * tq, HEAD_DIM)
    s = _dot_nt(q, k_ref[0, 0]).reshape(hg, tq, n_rows)
    t = i * tq + lax.broadcasted_iota(jnp.int32, (1, tq, 1), 1)
    n = lax.broadcasted_iota(jnp.int32, (1, 1, n_rows), 2)
    valid = (n * CMP_STRIDE + (CMP_LEN - 1) <= t) & (n < n_cmp)
    s = jnp.where(valid, s, -jnp.inf)
    m = jnp.max(s, axis=-1, keepdims=True)
    m = jnp.where(m == -jnp.inf, 0.0, m)
    e = jnp.exp(s - m)
    p = e / jnp.maximum(jnp.sum(e, axis=-1, keepdims=True), 1e-30)
    o = _dot(p.reshape(hg * tq, n_rows).astype(bf16), v_ref[0, 0])
    _write_gated(o_ref, gate_ref, o, g, 0, tq)

    lane = lax.broadcasted_iota(jnp.int32, (tq, LANES), 1)
    psum = p[0] + p[1] + p[2] + p[3]
    imp = _dot_f32_by_exact(psum, ov_ref[...])
    tq_pos = i * tq + lax.broadcasted_iota(jnp.int32, (tq, 1), 0)
    q_blk = lax.shift_right_logical(tq_pos, SEL_SHIFT)
    causal = lane <= q_blk
    forced = (lane == 0) | (lane == q_blk) | (lane == q_blk - 1)
    key = jnp.where(forced, jnp.inf, jnp.where(causal, imp, -1.0))
    key = jnp.where(lane < n_sel, key, -3.0)
    lane_f = lane.astype(f32)
    sel = jnp.zeros((tq, LANES), jnp.bool_)
    for _ in range(min(SEL_TOPK, n_sel)):
        mx = jnp.max(key, axis=-1, keepdims=True)
        first = jnp.min(jnp.where(key == mx, lane_f, float(LANES)), axis=-1, keepdims=True)
        pick = lane_f == first
        sel = sel | pick
        key = jnp.where(pick, -2.0, key)
    sel_ref[0, 0] = jnp.where(sel & causal, 0.0, -MASK_BIG).astype(bf16)


def _cmp(nqu, kcmp, vcmp, gates, n_cmp):
    n_batch, _, seq_len, _ = nqu.shape
    n_rows = kcmp.shape[2]
    n_sel = seq_len // SEL_BLOCK
    assert n_sel <= SEL_LANES
    tq = 1024
    c0 = np.arange(n_rows) * CMP_STRIDE
    s0 = np.arange(SEL_LANES) * SEL_BLOCK
    overlap = np.clip(np.minimum(c0[:, None] + CMP_LEN, s0[None, :] + SEL_BLOCK)
                      - np.maximum(c0[:, None], s0[None, :]), 0, None).astype(np.float32) / CMP_LEN
    overlap[n_cmp:, :] = 0.0
    overlap[:, n_sel:] = 0.0
    return pl.pallas_call(
        functools.partial(_cmp_kernel, tq=tq, n_rows=n_rows, n_cmp=n_cmp, n_sel=n_sel),
        out_shape=[
            jax.ShapeDtypeStruct((n_batch, seq_len, NSA_W), bf16),
            jax.ShapeDtypeStruct((n_batch, NSA_KV_HEADS, seq_len, SEL_LANES), bf16),
        ],
        grid=(n_batch, NSA_KV_HEADS, seq_len // tq),
        in_specs=[
            pl.BlockSpec((1, NSA_GROUP, tq, HEAD_DIM), lambda b, g, i: (b, g, i, 0)),
            pl.BlockSpec((1, 1, n_rows, HEAD_DIM), lambda b, g, i: (b, g, 0, 0)),
            pl.BlockSpec((1, 1, n_rows, HEAD_DIM), lambda b, g, i: (b, g, 0, 0)),
            pl.BlockSpec((1, tq, LANES), lambda b, g, i: (b, i, 0)),
            pl.BlockSpec((n_rows, SEL_LANES), lambda b, g, i: (0, 0)),
        ],
        out_specs=[
            pl.BlockSpec((1, tq, NSA_GROUP * HEAD_DIM), lambda b, g, i: (b, i, g)),
            pl.BlockSpec((1, 1, tq, SEL_LANES), lambda b, g, i: (b, g, i, 0)),
        ],
        compiler_params=_params("arbitrary", "arbitrary", "arbitrary"),
        name="cmp",
    )(nqu, kcmp, vcmp, gates, jnp.asarray(overlap, dtype=bf16))


def _write_gated(o_ref, gate_ref, out, g, branch, tq):
    gates = gate_ref[0]
    lane = lax.broadcasted_iota(jnp.int32, (tq, LANES), 1)
    for h in range(NSA_GROUP):
        gcol = FOX_HEADS + (g * NSA_GROUP + h) * N_BRANCH + branch
        gh = jnp.sum(jnp.where(lane == gcol, gates, 0.0), axis=-1, keepdims=True)
        o_ref[0, :, h * HEAD_DIM:(h + 1) * HEAD_DIM] = (gh * out[h * tq:(h + 1) * tq]).astype(bf16)


def _slc_kernel(q_ref, sel_ref, k_ref, vt_ref, gate_ref, o_ref, *bufs, tq, tk):
    i = pl.program_id(1)
    hg = NSA_GROUP
    n_q = hg * tq
    q0 = i * tq
    jd = q0 // tk

    def diag_mask(s):
        kp = jd * tk + lax.broadcasted_iota(jnp.int32, (tk, n_q), 0)
        t = q0 + (lax.broadcasted_iota(jnp.int32, (tk, n_q), 1) & (tq - 1))
        return jnp.where(kp <= t, s, -jnp.inf)

    def chain(g):
        q = q_ref[0, g * hg:(g + 1) * hg].reshape(n_q, HEAD_DIM)
        qa = jnp.concatenate([q, jnp.concatenate([sel_ref[0, g]] * hg, axis=0)], axis=1)

        def scores(j):
            return _dot_nt(k_ref[0, g, pl.ds(pl.multiple_of(j * tk, tk), tk), :], qa)

        def emit(out):
            gates = gate_ref[0]
            lane = lax.broadcasted_iota(jnp.int32, (tq, LANES), 1)
            for h in range(hg):
                head = g * hg + h
                gh = jnp.sum(jnp.where(lane == FOX_HEADS + head * N_BRANCH + 1, gates, 0.0),
                             axis=-1, keepdims=True)
                o_ref[0, :, head * HEAD_DIM:(head + 1) * HEAD_DIM] = (
                    gh * out[:, h * tq:(h + 1) * tq].T).astype(bf16)

        return _Chain(scores, lambda j: vt_ref[0, g, j], lambda j: None, diag_mask, emit,
                      bufs[2 * g], bufs[2 * g + 1])

    _flash_sweep(jd, [chain(g) for g in range(NSA_KV_HEADS)], n_q)


def _slc(nqr, selneg, kaug, vst, gates):
    n_batch, _, seq_len, _ = nqr.shape
    tq, tk = 128, KV_BLOCK
    nkb = seq_len // tk
    assert tq & (tq - 1) == 0
    n_q = NSA_GROUP * tq
    return pl.pallas_call(
        functools.partial(_slc_kernel, tq=tq, tk=tk),
        out_shape=jax.ShapeDtypeStruct((n_batch, seq_len, NSA_W), bf16),
        grid=(n_batch, seq_len // tq),
        in_specs=[
            pl.BlockSpec((1, NSA_HEADS, tq, HEAD_DIM), lambda b, i: (b, 0, i, 0)),
            pl.BlockSpec((1, NSA_KV_HEADS, tq, SEL_LANES), lambda b, i: (b, 0, i, 0)),
            pl.BlockSpec((1, NSA_KV_HEADS, seq_len, HEAD_DIM + SEL_LANES), lambda b, i: (b, 0, 0, 0)),
            pl.BlockSpec((1, NSA_KV_HEADS, nkb, HEAD_DIM, tk), lambda b, i: (b, 0, 0, 0, 0)),
            pl.BlockSpec((1, tq, LANES), lambda b, i: (b, i, 0)),
        ],
        out_specs=pl.BlockSpec((1, tq, NSA_W), lambda b, i: (b, i, 0)),
        scratch_shapes=[pltpu.VMEM((2, tk, n_q), f32), pltpu.VMEM((2, tk, n_q), bf16)] * NSA_KV_HEADS,
        compiler_params=_params("arbitrary", "arbitrary"),
        name="slc",
    )(nqr, selneg, kaug, vst, gates)


def _win_kernel(q_ref, k_ref, vt_ref, gate_ref, o_ref, *, tq, n_sub):
    i = pl.program_id(2)
    g = pl.program_id(1)
    hg = NSA_GROUP
    span = WINDOW + tq
    q0s = [(i * n_sub + u) * tq for u in range(n_sub)]
    k0s = [pl.multiple_of(jnp.maximum(q0 - WINDOW, 0), tq) for q0 in q0s]
    scores = []
    for u in range(n_sub):
        q = q_ref[0, :, u * tq:(u + 1) * tq, :].reshape(hg * tq, HEAD_DIM)
        scores.append(_dot_nt(k_ref[0, 0, pl.ds(k0s[u], span), :], q))
    probs, denoms = [], []
    for u in range(n_sub):
        kp = k0s[u] + lax.broadcasted_iota(jnp.int32, (span, tq), 0)
        t = q0s[u] + lax.broadcasted_iota(jnp.int32, (span, tq), 1)
        diff = t - kp
        bias = jnp.where((diff >= 0) & (diff < WINDOW), 0.0, -jnp.inf)
        s = jnp.concatenate([scores[u][:, h * tq:(h + 1) * tq] + bias for h in range(hg)], axis=1)
        e = jnp.exp2(s - jnp.max(s, axis=0, keepdims=True))
        denoms.append(jnp.sum(e, axis=0, keepdims=True))
        probs.append(e.astype(bf16))
    gates = gate_ref[0]
    lane = lax.broadcasted_iota(jnp.int32, (tq, LANES), 1)
    for u in range(n_sub):
        jb = k0s[u] // tq
        vt = jnp.concatenate([vt_ref[0, 0, jb + c] for c in range(span // tq)], axis=1)
        out = _dot(vt, probs[u]) / denoms[u]
        for h in range(hg):
            gcol = FOX_HEADS + (g * hg + h) * N_BRANCH + 2
            gh = jnp.sum(jnp.where(lane == gcol, gates[u * tq:(u + 1) * tq], 0.0), axis=-1, keepdims=True)
            o_ref[0, u * tq:(u + 1) * tq, h * HEAD_DIM:(h + 1) * HEAD_DIM] = (
                gh * out[:, h * tq:(h + 1) * tq].T).astype(bf16)


def _win(nqr, kw, vwt, gates):
    n_batch, _, seq_len, _ = nqr.shape
    tq, n_sub = WIN_TILE, 2
    assert seq_len >= WINDOW + tq and WINDOW % tq == 0
    return pl.pallas_call(
        functools.partial(_win_kernel, tq=tq, n_sub=n_sub),
        out_shape=jax.ShapeDtypeStruct((n_batch, seq_len, NSA_W), bf16),
        grid=(n_batch, NSA_KV_HEADS, seq_len // (tq * n_sub)),
        in_specs=[
            pl.BlockSpec((1, NSA_GROUP, tq * n_sub, HEAD_DIM), lambda b, g, i: (b, g, i, 0)),
            pl.BlockSpec((1, 1, seq_len, HEAD_DIM), lambda b, g, i: (b, g, 0, 0)),
            pl.BlockSpec((1, 1, seq_len // tq, HEAD_DIM, tq), lambda b, g, i: (b, g, 0, 0, 0)),
            pl.BlockSpec((1, tq * n_sub, LANES), lambda b, g, i: (b, i, 0)),
        ],
        out_specs=pl.BlockSpec((1, tq * n_sub, NSA_GROUP * HEAD_DIM), lambda b, g, i: (b, i, g)),
        compiler_params=_params("arbitrary", "arbitrary", "arbitrary"),
        name="win",
    )(nqr, kw, vwt, gates)


def _outproj_kernel(fox_ref, c_ref, s_ref, w_ref, x_ref, mod_ref, wo_ref, o_ref):
    nsa = c_ref[...].astype(f32) + s_ref[...].astype(f32) + w_ref[...].astype(f32)
    a = jnp.concatenate([fox_ref[...], nsa.astype(bf16)], axis=1)
    o_ref[...] = x_ref[...] + mod_ref[0][2:3] * _dot(a, wo_ref[...])


def _outproj(ofox, ocmp, oslc, owin, x2d, mod3, wo_bf16, seq_len):
    m_rows = x2d.shape[0]
    tm = 512
    per_b = seq_len // tm
    half = pl.BlockSpec((tm, FOX_W), lambda i: (i, 0))
    return pl.pallas_call(
        _outproj_kernel,
        out_shape=jax.ShapeDtypeStruct((m_rows, D_MODEL), f32),
        grid=(m_rows // tm,),
        in_specs=[
            half, half, half, half,
            pl.BlockSpec((tm, D_MODEL), lambda i: (i, 0)),
            pl.BlockSpec((1, 6, D_MODEL), lambda i: (i // per_b, 0, 0)),
            pl.BlockSpec((D_MODEL, D_MODEL), lambda i: (0, 0)),
        ],
        out_specs=pl.BlockSpec((tm, D_MODEL), lambda i: (i, 0)),
        compiler_params=_params("arbitrary"),
        name="outproj",
    )(ofox, ocmp, oslc, owin, x2d, mod3, wo_bf16)


def _mlp_kernel(x_ref, mod_ref, g_ref, wu_ref, wd_ref, o_ref, h_ref, acc_ref):
    f = pl.program_id(1)

    @pl.when(f == 0)
    def _():
        md = mod_ref[0]
        y = _rms(x_ref[...], g_ref[...])
        h_ref[...] = (y * (1.0 + md[4:5]) + md[3:4]).astype(bf16)
        acc_ref[...] = jnp.zeros_like(acc_ref)

    u = jnp.maximum(_dot(h_ref[...], wu_ref[...]), 0.0)
    acc_ref[...] += _dot((u * u).astype(bf16), wd_ref[...])

    @pl.when(f == pl.num_programs(1) - 1)
    def _():
        o_ref[...] = x_ref[...] + mod_ref[0][5:6] * acc_ref[...]


def _mlp(x2d, mod3, norm_g, wu_bf16, wd_bf16, seq_len):
    m_rows = x2d.shape[0]
    tm, tf = 512, 1024
    per_b = seq_len // tm
    return pl.pallas_call(
        _mlp_kernel,
        out_shape=jax.ShapeDtypeStruct((m_rows, D_MODEL), f32),
        grid=(m_rows // tm, D_FF // tf),
        in_specs=[
            pl.BlockSpec((tm, D_MODEL), lambda i, f: (i, 0)),
            pl.BlockSpec((1, 6, D_MODEL), lambda i, f: (i // per_b, 0, 0)),
            pl.BlockSpec((1, D_MODEL), lambda i, f: (0, 0)),
            pl.BlockSpec((D_MODEL, tf), lambda i, f: (0, f)),
            pl.BlockSpec((tf, D_MODEL), lambda i, f: (f, 0)),
        ],
        out_specs=pl.BlockSpec((tm, D_MODEL), lambda i, f: (i, 0)),
        scratch_shapes=[pltpu.VMEM((tm, D_MODEL), bf16), pltpu.VMEM((tm, D_MODEL), f32)],
        compiler_params=_params("arbitrary", "arbitrary"),
        name="mlp",
    )(x2d, mod3, norm_g, wu_bf16, wd_bf16)


def _layer(x, c, w_ada, b_ada, norm1_g, w_in, b_forget, fox_q_norm, fox_k_norm, nsa_q_norm,
           cmp_k_norm, slc_k_norm, win_k_norm, cmp_pe_k, cmp_w1_k, cmp_w2_k, cmp_pe_v, cmp_w1_v,
           cmp_w2_v, w_out, norm2_g, w_up, w_down):
    n_batch, seq_len, _ = x.shape
    n_cmp = (seq_len - CMP_LEN) // CMP_STRIDE + 1
    row = lambda v: v.reshape(1, -1)

    z0 = COL_NQ
    nq0 = z0 + FOX_HEADS
    gz0 = nq0 + NSA_W + 6 * KV_W
    w_a = w_in[:, :z0].astype(bf16)
    w_b = w_in[:, nq0:gz0].astype(bf16)
    w_s = jnp.concatenate([w_in[:, z0:nq0], w_in[:, gz0:],
                           jnp.zeros((D_MODEL, LANES - FOX_HEADS - N_BRANCH * NSA_HEADS), w_in.dtype)],
                          axis=1).astype(bf16)
    half = HEAD_DIM // 2
    inv_freq = ROPE_THETA ** (-jnp.arange(half, dtype=f32) / half)
    inv_freq = jnp.concatenate([inv_freq, inv_freq]).reshape(1, HEAD_DIM)

    mod3 = _ada(c, w_ada, b_ada).reshape(n_batch, 6, D_MODEL)
    x2d = x.reshape(n_batch * seq_len, D_MODEL)
    proj, small, w_up16, w_down16 = _proj(x2d, mod3, row(norm1_g), w_a, w_b, w_s, w_up, w_down, seq_len)
    (fq, fk, fvt, cin, nqu, nqr, kc, vc, kaug, vst, kw, vw, gates) = _prep(
        proj, small, n_batch, seq_len, row(fox_q_norm), row(fox_k_norm), row(nsa_q_norm), row(slc_k_norm),
        row(win_k_norm), jnp.pad(b_forget, (0, LANES - FOX_HEADS)).reshape(1, LANES), inv_freq)
    kcmp = _compress(kc, cmp_pe_k, cmp_w1_k.astype(bf16), cmp_w2_k.astype(bf16), row(cmp_k_norm), True)
    vcmp = _compress(vc, cmp_pe_v, cmp_w1_v.astype(bf16), cmp_w2_v.astype(bf16), row(cmp_k_norm), False)
    ofox = _fox(fq, fk, fvt, cin)
    ocmp, selneg = _cmp(nqu, kcmp, vcmp, gates, n_cmp)
    oslc = _slc(nqr, selneg, kaug, vst, gates)
    owin = _win(nqr, kw, vw, gates)
    x1 = _outproj(ofox.reshape(-1, FOX_W), ocmp.reshape(-1, NSA_W), oslc.reshape(-1, NSA_W),
                  owin.reshape(-1, NSA_W), x2d, mod3, w_out.astype(bf16), seq_len)
    x2 = _mlp(x1, mod3, row(norm2_g), w_up16, w_down16, seq_len)
    return x2.reshape(n_batch, seq_len, D_MODEL)


def kernel(x, c, w_ada, b_ada, norm1_g, w_in, b_forget, fox_q_norm, fox_k_norm, nsa_q_norm, cmp_k_norm,
           slc_k_norm, win_k_norm, cmp_pe_k, cmp_w1_k, cmp_w2_k, cmp_pe_v, cmp_w1_v, cmp_w2_v, w_out,
           norm2_g, w_up, w_down):
    depth = w_ada.shape[0]
    for l in range(depth):
        x = _layer(x, c, w_ada[l], b_ada[l], norm1_g[l], w_in[l], b_forget[l], fox_q_norm[l], fox_k_norm[l],
                   nsa_q_norm[l], cmp_k_norm[l], slc_k_norm[l], win_k_norm[l], cmp_pe_k[l], cmp_w1_k[l],
                   cmp_w2_k[l], cmp_pe_v[l], cmp_w1_v[l], cmp_w2_v[l], w_out[l], norm2_g[l], w_up[l],
                   w_down[l])
    return x
```

```python
import functools
import math
from typing import Any, Callable, NamedTuple

import numpy as np
import jax
import jax.numpy as jnp
from jax import lax
from jax.experimental import pallas as pl
from jax.experimental.pallas import tpu as pltpu

D_MODEL = 2048
HEAD_DIM = 128
FOX_HEADS = 8
NSA_HEADS = 8
NSA_KV_HEADS = 2
NSA_GROUP = NSA_HEADS // NSA_KV_HEADS
N_BRANCH = 3
D_FF = 4 * D_MODEL
ROPE_THETA = 10000.0
CMP_LEN = 32
CMP_STRIDE = 16
CMP_HIDDEN = 2 * HEAD_DIM
SEL_BLOCK = 64
SEL_SHIFT = 6
SEL_TOPK = 16
WINDOW = 512
NORM_EPS = 1e-6
ATTN_SCALE = HEAD_DIM ** -0.5
FOX_W = FOX_HEADS * HEAD_DIM
NSA_W = NSA_HEADS * HEAD_DIM
KV_W = NSA_KV_HEADS * HEAD_DIM

LANES = 128
SEL_LANES = LANES
MASK_BIG = 1e30
KV_BLOCK = 512
WIN_TILE = 128
LOG2E = math.log2(math.e)

COL_FQ = 0
COL_FK = COL_FQ + FOX_W
COL_FV = COL_FK + FOX_W
COL_NQ = COL_FV + FOX_W
COL_KC = COL_NQ + NSA_W
COL_VC = COL_KC + KV_W
COL_KS = COL_VC + KV_W
COL_VS = COL_KS + KV_W
COL_KW = COL_VS + KV_W
COL_VW = COL_KW + KV_W
COL_SMALL = COL_VW + KV_W

VMEM_LIMIT = 56 * 1024 * 1024

f32 = jnp.float32
bf16 = jnp.bfloat16


def _params(*sem):
    return pltpu.CompilerParams(dimension_semantics=sem, vmem_limit_bytes=VMEM_LIMIT)


def _dot_nt(a, b):
    return lax.dot_general(a, b, (((1,), (1,)), ((), ())), preferred_element_type=f32)


def _dot(a, b):
    return jnp.dot(a, b, preferred_element_type=f32)


def _split3(x):
    hi = x.astype(bf16)
    r1 = x - hi.astype(f32)
    mid = r1.astype(bf16)
    lo = (r1 - mid.astype(f32)).astype(bf16)
    return hi, mid, lo


def _dot_f32_by_exact(x, w_bf16):
    hi, mid, lo = _split3(x)
    return _dot(hi, w_bf16) + (_dot(mid, w_bf16) + _dot(lo, w_bf16))


def _rms(x, gain):
    ms = jnp.mean(x * x, axis=-1, keepdims=True)
    return x * lax.rsqrt(ms + NORM_EPS) * gain


def _ada_kernel(ct_ref, w_ref, b_ref, o_ref, *, n_batch, k_chunk):
    ct = ct_ref[...]
    act = ct * jax.nn.sigmoid(ct)
    rows = []
    for b in range(n_batch):
        col = act[:, b:b + 1]
        acc = b_ref[...]
        for k0 in range(0, D_MODEL, k_chunk):
            acc = acc + jnp.sum(w_ref[k0:k0 + k_chunk, :] * col[k0:k0 + k_chunk], axis=0, keepdims=True)
        rows.append(acc)
    o_ref[...] = jnp.concatenate(rows, axis=0)


def _ada(c, w_ada, b_ada):
    n_batch = c.shape[0]
    n_out = w_ada.shape[1]
    tn = 1024
    return pl.pallas_call(
        functools.partial(_ada_kernel, n_batch=n_batch, k_chunk=256),
        out_shape=jax.ShapeDtypeStruct((n_batch, n_out), f32),
        grid=(n_out // tn,),
        in_specs=[
            pl.BlockSpec((D_MODEL, n_batch), lambda j: (0, 0)),
            pl.BlockSpec((D_MODEL, tn), lambda j: (0, j)),
            pl.BlockSpec((1, tn), lambda j: (0, j)),
        ],
        out_specs=pl.BlockSpec((n_batch, tn), lambda j: (0, j)),
        compiler_params=_params("arbitrary"),
        name="ada",
    )(c.T, w_ada, b_ada.reshape(1, n_out))


def _proj_kernel(x_ref, mod_ref, g_ref, wa_ref, wb_ref, ws_ref, wu32_ref, wd32_ref, wo32_ref,
                 o_ref, small_ref, wu16_ref, wd16_ref, wo16_ref, h_ref, *, na, nb):
    j = pl.program_id(1)
    wu16_ref[...] = wu32_ref[...].astype(bf16)
    wd16_ref[...] = wd32_ref[...].astype(bf16)
    wo16_ref[...] = wo32_ref[...].astype(bf16)

    @pl.when(j == 0)
    def _():
        md = mod_ref[0]
        y = _rms(x_ref[...], g_ref[...])
        h_ref[...] = (y * (1.0 + md[1:2]) + md[0:1]).astype(bf16)

    @pl.when(j < na)
    def _():
        o_ref[...] = _dot(h_ref[...], wa_ref[...].astype(bf16)).astype(bf16)

    @pl.when((j >= na) & (j < na + nb))
    def _():
        o_ref[...] = _dot(h_ref[...], wb_ref[...]).astype(bf16)

    @pl.when(j == na + nb)
    def _():
        small_ref[...] = _dot(h_ref[...], ws_ref[...])


def _proj(x2d, mod3, norm_g, w_in, wb, ws, w_up, w_down, w_out, seq_len):
    m_rows = x2d.shape[0]
    tm, tn = 1024, 512
    assert seq_len % tm == 0 and COL_NQ % tn == 0 and wb.shape[1] % tn == 0
    na, nb = COL_NQ // tn, wb.shape[1] // tn
    per_b = seq_len // tm
    n_j = na + nb + 1
    n_steps = (m_rows // tm) * n_j
    cast_block = next(c for c in range(LANES, D_FF + 1, LANES) if D_FF % c == 0 and D_FF // c <= n_steps)
    n_cast = D_FF // cast_block
    cast_idx = lambda i, j: jnp.minimum(i * n_j + j, n_cast - 1)
    return pl.pallas_call(
        functools.partial(_proj_kernel, na=na, nb=nb),
        out_shape=[jax.ShapeDtypeStruct((m_rows, (na + nb) * tn), bf16),
                   jax.ShapeDtypeStruct((m_rows, LANES), f32),
                   jax.ShapeDtypeStruct(w_up.shape, bf16), jax.ShapeDtypeStruct(w_down.shape, bf16),
                   jax.ShapeDtypeStruct(w_out.shape, bf16)],
        grid=(m_rows // tm, n_j),
        in_specs=[
            pl.BlockSpec((tm, D_MODEL), lambda i, j: (i, 0)),
            pl.BlockSpec((1, 6, D_MODEL), lambda i, j: (i // per_b, 0, 0)),
            pl.BlockSpec((1, D_MODEL), lambda i, j: (0, 0)),
            pl.BlockSpec((D_MODEL, tn), lambda i, j: (0, jnp.minimum(j, na - 1))),
            pl.BlockSpec((D_MODEL, tn), lambda i, j: (0, jnp.clip(j - na, 0, nb - 1))),
            pl.BlockSpec((D_MODEL, LANES), lambda i, j: (0, 0)),
            pl.BlockSpec((D_MODEL // n_cast, D_FF), lambda i, j: (cast_idx(i, j), 0)),
            pl.BlockSpec((cast_block, D_MODEL), lambda i, j: (cast_idx(i, j), 0)),
            pl.BlockSpec((D_MODEL // n_cast, D_MODEL), lambda i, j: (cast_idx(i, j), 0)),
        ],
        out_specs=[pl.BlockSpec((tm, tn), lambda i, j: (i, jnp.minimum(j, na + nb - 1))),
                   pl.BlockSpec((tm, LANES), lambda i, j: (i, 0)),
                   pl.BlockSpec((D_MODEL // n_cast, D_FF), lambda i, j: (cast_idx(i, j), 0)),
                   pl.BlockSpec((cast_block, D_MODEL), lambda i, j: (cast_idx(i, j), 0)),
                   pl.BlockSpec((D_MODEL // n_cast, D_MODEL), lambda i, j: (cast_idx(i, j), 0))],
        scratch_shapes=[pltpu.VMEM((tm, D_MODEL), bf16)],
        compiler_params=_params("arbitrary", "arbitrary"),
        name="proj",
    )(x2d, mod3, norm_g, w_in, wb, ws, w_up, w_down, w_out)


def _prep_kernel(p_ref, small_ref, gq_ref, gk_ref, gn_ref, gs_ref, gw_ref, bf_ref, inv_ref,
                 fq_ref, fk_ref, fvt_ref, cin_ref, nqu_ref, nqr_ref, kc_ref, vc_ref,
                 ks_ref, vst_ref, kw_ref, vw_ref, gate_ref, carry_ref, *, tm):
    i = pl.program_id(1)

    @pl.when(i == 0)
    def _():
        carry_ref[...] = jnp.zeros_like(carry_ref)

    def head(col, h):
        return p_ref[:, col + h * HEAD_DIM: col + (h + 1) * HEAD_DIM].astype(f32)

    row = lax.broadcasted_iota(jnp.int32, (tm, LANES), 0)
    lane = lax.broadcasted_iota(jnp.int32, (tm, LANES), 1)
    pos = i * tm + row
    ang = pos.astype(f32) * inv_ref[...]
    cos = jnp.cos(ang)
    sin = jnp.sin(ang)
    sin_signed = jnp.where(lane < HEAD_DIM // 2, -sin, sin)

    def rope(x):
        return x * cos + pltpu.roll(x, HEAD_DIM // 2, 1) * sin_signed

    small = small_ref[...]
    z = small + bf_ref[...]
    logf = jnp.minimum(z, 0.0) - jnp.log1p(jnp.exp(-jnp.abs(z)))
    t_idx = lax.broadcasted_iota(jnp.int32, (tm, tm), 0)
    s_idx = lax.broadcasted_iota(jnp.int32, (tm, tm), 1)
    tri = jnp.where(s_idx <= t_idx, 1.0, 0.0).astype(bf16)
    hi, mid, lo = _split3(logf)
    local = _dot(tri, hi) + (_dot(tri, mid) + _dot(tri, lo))
    cin_ref[0, 0] = carry_ref[0:1, :]
    carry_ref[...] = carry_ref[...] + local[tm - 1:tm, :]
    b_hi, b_mid, b_lo = (v.astype(f32) for v in _split3(local * (-LOG2E)))
    ones3 = jnp.where(lane < 3, 1.0, 0.0).astype(bf16)

    for h in range(FOX_HEADS):
        q = _rms(head(COL_FQ, h), gq_ref[...]) * (ATTN_SCALE * LOG2E)
        fq_ref[0, h] = jnp.concatenate([q.astype(bf16), ones3], axis=1)
        k = _rms(head(COL_FK, h), gk_ref[...])
        bias = jnp.where(lane == 0, b_hi[:, h:h + 1],
                         jnp.where(lane == 1, b_mid[:, h:h + 1],
                                   jnp.where(lane == 2, b_lo[:, h:h + 1], 0.0)))
        fk_ref[0, h] = jnp.concatenate([k.astype(bf16), bias.astype(bf16)], axis=1)
        fvt_ref[0, h, 0] = head(COL_FV, h).T.astype(bf16)

    gate_ref[0] = jax.nn.sigmoid(small)

    for h in range(NSA_HEADS):
        qn = _rms(head(COL_NQ, h), gn_ref[...])
        nqu_ref[0, h] = (qn * ATTN_SCALE).astype(bf16)
        nqr_ref[0, h] = rope(qn * (ATTN_SCALE * LOG2E)).astype(bf16)
    onehot = jnp.where(lane == lax.shift_right_logical(pos, SEL_SHIFT), 1.0, 0.0).astype(bf16)
    for g in range(NSA_KV_HEADS):
        kc_ref[0, g] = head(COL_KC, g).astype(bf16)
        vc_ref[0, g] = head(COL_VC, g).astype(bf16)
        ks = rope(_rms(head(COL_KS, g), gs_ref[...])).astype(bf16)
        ks_ref[0, g] = jnp.concatenate([ks, onehot], axis=1)
        vst_ref[0, g, 0] = head(COL_VS, g).T.astype(bf16)
        kw_ref[0, g] = rope(_rms(head(COL_KW, g), gw_ref[...])).astype(bf16)
        vwt = head(COL_VW, g).T.astype(bf16)
        for c in range(tm // WIN_TILE):
            vw_ref[0, g, c] = vwt[:, c * WIN_TILE:(c + 1) * WIN_TILE]


def _prep(proj, small, n_batch, seq_len, gq, gk, gn, gs, gw, b_forget_row, inv_freq):
    tm = KV_BLOCK
    per_b = seq_len // tm
    hshape = lambda n, w, dt: jax.ShapeDtypeStruct((n_batch, n, seq_len, w), dt)
    hspec = lambda n, w: pl.BlockSpec((1, n, tm, w), lambda b, i: (b, 0, i, 0))
    tshape = lambda n: jax.ShapeDtypeStruct((n_batch, n, per_b, HEAD_DIM, tm), bf16)
    tspec = lambda n: pl.BlockSpec((1, n, 1, HEAD_DIM, tm), lambda b, i: (b, 0, i, 0, 0))
    vec = pl.BlockSpec((1, LANES), lambda b, i: (0, 0))
    return pl.pallas_call(
        functools.partial(_prep_kernel, tm=tm),
        out_shape=[
            hshape(FOX_HEADS, 2 * HEAD_DIM, bf16), hshape(FOX_HEADS, 2 * HEAD_DIM, bf16), tshape(FOX_HEADS),
            jax.ShapeDtypeStruct((n_batch, per_b, 1, LANES), f32),
            hshape(NSA_HEADS, HEAD_DIM, bf16), hshape(NSA_HEADS, HEAD_DIM, bf16),
            hshape(NSA_KV_HEADS, HEAD_DIM, bf16), hshape(NSA_KV_HEADS, HEAD_DIM, bf16),
            hshape(NSA_KV_HEADS, HEAD_DIM + SEL_LANES, bf16), tshape(NSA_KV_HEADS),
            hshape(NSA_KV_HEADS, HEAD_DIM, bf16),
            jax.ShapeDtypeStruct((n_batch, NSA_KV_HEADS, seq_len // WIN_TILE, HEAD_DIM, WIN_TILE), bf16),
            jax.ShapeDtypeStruct((n_batch, seq_len, LANES), f32),
        ],
        grid=(n_batch, per_b),
        in_specs=[
            pl.BlockSpec((tm, COL_SMALL), lambda b, i: (b * per_b + i, 0)),
            pl.BlockSpec((tm, LANES), lambda b, i: (b * per_b + i, 0)),
            vec, vec, vec, vec, vec, vec, vec,
        ],
        out_specs=[
            hspec(FOX_HEADS, 2 * HEAD_DIM), hspec(FOX_HEADS, 2 * HEAD_DIM), tspec(FOX_HEADS),
            pl.BlockSpec((1, 1, 1, LANES), lambda b, i: (b, i, 0, 0)),
            hspec(NSA_HEADS, HEAD_DIM), hspec(NSA_HEADS, HEAD_DIM),
            hspec(NSA_KV_HEADS, HEAD_DIM), hspec(NSA_KV_HEADS, HEAD_DIM),
            hspec(NSA_KV_HEADS, HEAD_DIM + SEL_LANES), tspec(NSA_KV_HEADS),
            hspec(NSA_KV_HEADS, HEAD_DIM),
            pl.BlockSpec((1, NSA_KV_HEADS, tm // WIN_TILE, HEAD_DIM, WIN_TILE), lambda b, i: (b, 0, i, 0, 0)),
            pl.BlockSpec((1, tm, LANES), lambda b, i: (b, i, 0)),
        ],
        scratch_shapes=[pltpu.VMEM((8, LANES), f32)],
        compiler_params=_params("arbitrary", "arbitrary"),
        name="prep",
    )(proj, small, gq, gk, gn, gs, gw, b_forget_row, inv_freq)


def _compress_kernel(x_ref, pe_ref, w1_ref, w2_ref, g_ref, o_ref, *, n_rows, do_norm):
    half = CMP_STRIDE * HEAD_DIM
    x = x_ref[0, 0]
    pe = pe_ref[...]
    xa = (x + pe[:, :half]).astype(bf16)
    xb = (x + pe[:, half:]).astype(bf16)
    a = _dot(xa, w1_ref[0:half, :])
    b = _dot(xb, w1_ref[half:2 * half, :])
    pre = a + pltpu.roll(b, n_rows - 1, 0)
    hid = pre * jax.nn.sigmoid(pre)
    out = _dot(hid.astype(bf16), w2_ref[...])
    if do_norm:
        out = _rms(out, g_ref[...])
    o_ref[0, 0] = out.astype(bf16)


def _compress(x4, pe, w1_bf16, w2_bf16, gain, do_norm):
    n_batch, n_g, seq_len, _ = x4.shape
    n_rows = seq_len // CMP_STRIDE
    half = CMP_STRIDE * HEAD_DIM
    xr = x4.reshape(n_batch, n_g, n_rows, half)
    return pl.pallas_call(
        functools.partial(_compress_kernel, n_rows=n_rows, do_norm=do_norm),
        out_shape=jax.ShapeDtypeStruct((n_batch, n_g, n_rows, HEAD_DIM), bf16),
        grid=(n_batch, n_g),
        in_specs=[
            pl.BlockSpec((1, 1, n_rows, half), lambda b, g: (b, g, 0, 0)),
            pl.BlockSpec((1, 2 * half), lambda b, g: (0, 0)),
            pl.BlockSpec((2 * half, CMP_HIDDEN), lambda b, g: (0, 0)),
            pl.BlockSpec((CMP_HIDDEN, HEAD_DIM), lambda b, g: (0, 0)),
            pl.BlockSpec((1, HEAD_DIM), lambda b, g: (0, 0)),
        ],
        out_specs=pl.BlockSpec((1, 1, n_rows, HEAD_DIM), lambda b, g: (b, g, 0, 0)),
        compiler_params=_params("arbitrary", "arbitrary"),
        name="compress",
    )(xr, pe.reshape(1, 2 * half), w1_bf16, w2_bf16, gain)


def _softmax_block(s, d, m_prev, l_prev):
    m_blk = jnp.max(s, axis=0, keepdims=True)
    m_new = jnp.maximum(m_prev, m_blk if d is None else m_blk + d)
    p = jnp.exp2(s - (m_new if d is None else m_new - d))
    alpha = jnp.exp2(m_prev - m_new)
    l_new = alpha * l_prev + jnp.sum(p, axis=0, keepdims=True)
    return m_new, l_new, alpha, p.astype(bf16)


class _Chain(NamedTuple):
    scores: Callable
    values: Callable
    offset: Callable
    diag_mask: Callable
    emit: Callable
    s_buf: Any
    p_buf: Any


def _flash_sweep(n_full, chains, n_q):
    for c in chains:
        c.s_buf[0] = c.scores(0)
        c.p_buf[1] = jnp.zeros(c.p_buf.shape[1:], c.p_buf.dtype)

    def accumulate(c, j, slot, alpha, acc):
        return alpha * acc + _dot(c.values(jnp.maximum(j, 0)), c.p_buf[slot])

    def step(j, cur, states):
        out = []
        for c, (m, l, acc, alpha_prev) in zip(chains, states):
            acc = accumulate(c, j - 1, 1 - cur, alpha_prev, acc)
            m, l, alpha, p = _softmax_block(c.s_buf[cur], c.offset(j), m, l)
            c.p_buf[cur] = p
            c.s_buf[1 - cur] = c.scores(j + 1)
            out.append((m, l, acc, alpha))
        return tuple(out)

    def finish(cur, states):
        for c, (m, l, acc, alpha_prev) in zip(chains, states):
            acc = accumulate(c, n_full - 1, 1 - cur, alpha_prev, acc)
            m, l, alpha, p = _softmax_block(c.diag_mask(c.s_buf[cur]), None, m, l)
            acc = alpha * acc + _dot(c.values(n_full), p)
            c.emit(acc / l)

    init = tuple((jnp.full((1, n_q), -jnp.inf, f32), jnp.zeros((1, n_q), f32),
                  jnp.zeros((HEAD_DIM, n_q), f32), jnp.ones((1, n_q), f32)) for _ in chains)
    states = lax.fori_loop(0, n_full // 2, lambda jj, st: step(2 * jj + 1, 1, step(2 * jj, 0, st)), init)

    @pl.when(n_full % 2 == 0)
    def _():
        finish(0, states)

    @pl.when(n_full % 2 == 1)
    def _():
        finish(1, step(n_full - 1, 0, states))


def _fox_kernel(q_ref, k_ref, vt_ref, c_ref, o_ref, *bufs, tq, n_heads):
    hb = pl.program_id(1)
    i = pl.program_id(2)
    lane = lax.broadcasted_iota(jnp.int32, (1, LANES), 1)

    def diag_mask(s):
        kk = lax.broadcasted_iota(jnp.int32, (tq, tq), 0)
        qq = lax.broadcasted_iota(jnp.int32, (tq, tq), 1)
        return jnp.where(kk <= qq, s, -jnp.inf)

    def chain(u):
        qa = q_ref[0, u]

        def cin(j):
            return jnp.sum(jnp.where(lane == hb * n_heads + u, c_ref[0, j], 0.0), axis=-1, keepdims=True)

        ci = cin(i)

        def scores(j):
            return _dot_nt(k_ref[0, u, pl.ds(pl.multiple_of(j * tq, tq), tq), :], qa)

        def emit(out_t):
            o_ref[0, :, u * HEAD_DIM:(u + 1) * HEAD_DIM] = out_t.T.astype(bf16)

        return _Chain(scores, lambda j: vt_ref[0, u, j], lambda j: (ci - cin(j)) * LOG2E, diag_mask, emit,
                      bufs[2 * u], bufs[2 * u + 1])

    _flash_sweep(i, [chain(u) for u in range(n_heads)], tq)


def _fox(fq, fk, fvt, cin):
    n_batch, n_h, seq_len, _ = fq.shape
    tq = KV_BLOCK
    nkb = seq_len // tq
    n_heads = 2
    return pl.pallas_call(
        functools.partial(_fox_kernel, tq=tq, n_heads=n_heads),
        out_shape=jax.ShapeDtypeStruct((n_batch, seq_len, n_h * HEAD_DIM), bf16),
        grid=(n_batch, n_h // n_heads, nkb),
        in_specs=[
            pl.BlockSpec((1, n_heads, tq, 2 * HEAD_DIM), lambda b, h, i: (b, h, i, 0)),
            pl.BlockSpec((1, n_heads, seq_len, 2 * HEAD_DIM), lambda b, h, i: (b, h, 0, 0)),
            pl.BlockSpec((1, n_heads, nkb, HEAD_DIM, tq), lambda b, h, i: (b, h, 0, 0, 0)),
            pl.BlockSpec((1, nkb, 1, LANES), lambda b, h, i: (b, 0, 0, 0)),
        ],
        out_specs=pl.BlockSpec((1, tq, n_heads * HEAD_DIM), lambda b, h, i: (b, i, h)),
        scratch_shapes=[pltpu.VMEM((2, tq, tq), f32), pltpu.VMEM((2, tq, tq), bf16)] * n_heads,
        compiler_params=_params("arbitrary", "arbitrary", "arbitrary"),
        name="fox",
    )(fq, fk, fvt, cin)


def _cmp_kernel(q_ref, k_ref, v_ref, gate_ref, ov_ref, o_ref, sel_ref, *, tq, n_rows, n_cmp, n_sel):
    i = pl.program_id(2)
    g = pl.program_id(1)
    hg = NSA_GROUP
    q = q_ref[0].reshape(hg * tq, HEAD_DIM)
    s = _dot_nt(q, k_ref[0, 0]).reshape(hg, tq, n_rows)
    t = i * tq + lax.broadcasted_iota(jnp.int32, (1, tq, 1), 1)
    n = lax.broadcasted_iota(jnp.int32, (1, 1, n_rows), 2)
    valid = (n * CMP_STRIDE + (CMP_LEN - 1) <= t) & (n < n_cmp)
    s = jnp.where(valid, s, -jnp.inf)
    m = jnp.max(s, axis=-1, keepdims=True)
    m = jnp.where(m == -jnp.inf, 0.0, m)
    e = jnp.exp(s - m)
    p = e / jnp.maximum(jnp.sum(e, axis=-1, keepdims=True), 1e-30)
    o = _dot(p.reshape(hg * tq, n_rows).astype(bf16), v_ref[0, 0])
    _write_gated(o_ref, gate_ref, o, g, 0, tq)

    lane = lax.broadcasted_iota(jnp.int32, (tq, LANES), 1)
    psum = p[0] + p[1] + p[2] + p[3]
    imp = _dot_f32_by_exact(psum, ov_ref[...])
    tq_pos = i * tq + lax.broadcasted_iota(jnp.int32, (tq, 1), 0)
    q_blk = lax.shift_right_logical(tq_pos, SEL_SHIFT)
    causal = lane <= q_blk
    forced = (lane == 0) | (lane == q_blk) | (lane == q_blk - 1)
    key = jnp.where(forced, jnp.inf, jnp.where(causal, imp, -1.0))
    key = jnp.where(lane < n_sel, key, -3.0)
    lane_f = lane.astype(f32)
    sel = jnp.zeros((tq, LANES), jnp.bool_)
    for _ in range(min(SEL_TOPK, n_sel)):
        mx = jnp.max(key, axis=-1, keepdims=True)
        first = jnp.min(jnp.where(key == mx, lane_f, float(LANES)), axis=-1, keepdims=True)
        pick = lane_f == first
        sel = sel | pick
        key = jnp.where(pick, -2.0, key)
    sel_ref[0, 0] = jnp.where(sel & causal, 0.0, -MASK_BIG).astype(bf16)


def _cmp(nqu, kcmp, vcmp, gates, n_cmp):
    n_batch, _, seq_len, _ = nqu.shape
    n_rows = kcmp.shape[2]
    n_sel = seq_len // SEL_BLOCK
    assert n_sel <= SEL_LANES
    tq = 1024
    c0 = np.arange(n_rows) * CMP_STRIDE
    s0 = np.arange(SEL_LANES) * SEL_BLOCK
    overlap = np.clip(np.minimum(c0[:, None] + CMP_LEN, s0[None, :] + SEL_BLOCK)
                      - np.maximum(c0[:, None], s0[None, :]), 0, None).astype(np.float32) / CMP_LEN
    overlap[n_cmp:, :] = 0.0
    overlap[:, n_sel:] = 0.0
    return pl.pallas_call(
        functools.partial(_cmp_kernel, tq=tq, n_rows=n_rows, n_cmp=n_cmp, n_sel=n_sel),
        out_shape=[
            jax.ShapeDtypeStruct((n_batch, seq_len, NSA_W), bf16),
            jax.ShapeDtypeStruct((n_batch, NSA_KV_HEADS, seq_len, SEL_LANES), bf16),
        ],
        grid=(n_batch, NSA_KV_HEADS, seq_len // tq),
        in_specs=[
            pl.BlockSpec((1, NSA_GROUP, tq, HEAD_DIM), lambda b, g, i: (b, g, i, 0)),
            pl.BlockSpec((1, 1, n_rows, HEAD_DIM), lambda b, g, i: (b, g, 0, 0)),
            pl.BlockSpec((1, 1, n_rows, HEAD_DIM), lambda b, g, i: (b, g, 0, 0)),
            pl.BlockSpec((1, tq, LANES), lambda b, g, i: (b, i, 0)),
            pl.BlockSpec((n_rows, SEL_LANES), lambda b, g, i: (0, 0)),
        ],
        out_specs=[
            pl.BlockSpec((1, tq, NSA_GROUP * HEAD_DIM), lambda b, g, i: (b, i, g)),
            pl.BlockSpec((1, 1, tq, SEL_LANES), lambda b, g, i: (b, g, i, 0)),
        ],
        compiler_params=_params("arbitrary", "arbitrary", "arbitrary"),
        name="cmp",
    )(nqu, kcmp, vcmp, gates, jnp.asarray(overlap, dtype=bf16))


def _write_gated(o_ref, gate_ref, out, g, branch, tq):
    gates = gate_ref[0]
    lane = lax.broadcasted_iota(jnp.int32, (tq, LANES), 1)
    for h in range(NSA_GROUP):
        gcol = FOX_HEADS + (g * NSA_GROUP + h) * N_BRANCH + branch
        gh = jnp.sum(jnp.where(lane == gcol, gates, 0.0), axis=-1, keepdims=True)
        o_ref[0, :, h * HEAD_DIM:(h + 1) * HEAD_DIM] = (gh * out[h * tq:(h + 1) * tq]).astype(bf16)


def _slc_kernel(q_ref, sel_ref, k_ref, vt_ref, gate_ref, o_ref, *bufs, tq, tk):
    i = pl.program_id(1)
    hg = NSA_GROUP
    n_q = hg * tq
    q0 = i * tq
    jd = q0 // tk

    def diag_mask(s):
        kp = jd * tk + lax.broadcasted_iota(jnp.int32, (tk, n_q), 0)
        t = q0 + (lax.broadcasted_iota(jnp.int32, (tk, n_q), 1) & (tq - 1))
        return jnp.where(kp <= t, s, -jnp.inf)

    def chain(g):
        q = q_ref[0, g * hg:(g + 1) * hg].reshape(n_q, HEAD_DIM)
        qa = jnp.concatenate([q, jnp.concatenate([sel_ref[0, g]] * hg, axis=0)], axis=1)

        def scores(j):
            return _dot_nt(k_ref[0, g, pl.ds(pl.multiple_of(j * tk, tk), tk), :], qa)

        def emit(out):
            gates = gate_ref[0]
            lane = lax.broadcasted_iota(jnp.int32, (tq, LANES), 1)
            for h in range(hg):
                head = g * hg + h
                gh = jnp.sum(jnp.where(lane == FOX_HEADS + head * N_BRANCH + 1, gates, 0.0),
                             axis=-1, keepdims=True)
                o_ref[0, :, head * HEAD_DIM:(head + 1) * HEAD_DIM] = (
                    gh * out[:, h * tq:(h + 1) * tq].T).astype(bf16)

        return _Chain(scores, lambda j: vt_ref[0, g, j], lambda j: None, diag_mask, emit,
                      bufs[2 * g], bufs[2 * g + 1])

    _flash_sweep(jd, [chain(g) for g in range(NSA_KV_HEADS)], n_q)


def _slc(nqr, selneg, kaug, vst, gates):
    n_batch, _, seq_len, _ = nqr.shape
    tq, tk = 256, KV_BLOCK
    nkb = seq_len // tk
    assert tq & (tq - 1) == 0
    n_q = NSA_GROUP * tq
    return pl.pallas_call(
        functools.partial(_slc_kernel, tq=tq, tk=tk),
        out_shape=jax.ShapeDtypeStruct((n_batch, seq_len, NSA_W), bf16),
        grid=(n_batch, seq_len // tq),
        in_specs=[
            pl.BlockSpec((1, NSA_HEADS, tq, HEAD_DIM), lambda b, i: (b, 0, i, 0)),
            pl.BlockSpec((1, NSA_KV_HEADS, tq, SEL_LANES), lambda b, i: (b, 0, i, 0)),
            pl.BlockSpec((1, NSA_KV_HEADS, seq_len, HEAD_DIM + SEL_LANES), lambda b, i: (b, 0, 0, 0)),
            pl.BlockSpec((1, NSA_KV_HEADS, nkb, HEAD_DIM, tk), lambda b, i: (b, 0, 0, 0, 0)),
            pl.BlockSpec((1, tq, LANES), lambda b, i: (b, i, 0)),
        ],
        out_specs=pl.BlockSpec((1, tq, NSA_W), lambda b, i: (b, i, 0)),
        scratch_shapes=[pltpu.VMEM((2, tk, n_q), f32), pltpu.VMEM((2, tk, n_q), bf16)] * NSA_KV_HEADS,
        compiler_params=_params("arbitrary", "arbitrary"),
        name="slc",
    )(nqr, selneg, kaug, vst, gates)


def _win_kernel(q_ref, k_ref, vt_ref, gate_ref, o_ref, *, tq, n_sub):
    i = pl.program_id(2)
    g = pl.program_id(1)
    hg = NSA_GROUP
    span = WINDOW + tq
    q0s = [(i * n_sub + u) * tq for u in range(n_sub)]
    k0s = [pl.multiple_of(jnp.maximum(q0 - WINDOW, 0), tq) for q0 in q0s]
    scores = []
    for u in range(n_sub):
        q = q_ref[0, :, u * tq:(u + 1) * tq, :].reshape(hg * tq, HEAD_DIM)
        scores.append(_dot_nt(k_ref[0, 0, pl.ds(k0s[u], span), :], q))
    probs, denoms = [], []
    for u in range(n_sub):
        kp = k0s[u] + lax.broadcasted_iota(jnp.int32, (span, tq), 0)
        t = q0s[u] + lax.broadcasted_iota(jnp.int32, (span, tq), 1)
        diff = t - kp
        bias = jnp.where((diff >= 0) & (diff < WINDOW), 0.0, -jnp.inf)
        s = jnp.concatenate([scores[u][:, h * tq:(h + 1) * tq] + bias for h in range(hg)], axis=1)
        e = jnp.exp2(s - jnp.max(s, axis=0, keepdims=True))
        denoms.append(jnp.sum(e, axis=0, keepdims=True))
        probs.append(e.astype(bf16))
    gates = gate_ref[0]
    lane = lax.broadcasted_iota(jnp.int32, (tq, LANES), 1)
    for u in range(n_sub):
        jb = k0s[u] // tq
        vt = jnp.concatenate([vt_ref[0, 0, jb + c] for c in range(span // tq)], axis=1)
        out = _dot(vt, probs[u]) / denoms[u]
        for h in range(hg):
            gcol = FOX_HEADS + (g * hg + h) * N_BRANCH + 2
            gh = jnp.sum(jnp.where(lane == gcol, gates[u * tq:(u + 1) * tq], 0.0), axis=-1, keepdims=True)
            o_ref[0, u * tq:(u + 1) * tq, h * HEAD_DIM:(h + 1) * HEAD_DIM] = (
                gh * out[:, h * tq:(h + 1) * tq].T).astype(bf16)


def _win(nqr, kw, vwt, gates):
    n_batch, _, seq_len, _ = nqr.shape
    tq, n_sub = WIN_TILE, 4
    assert seq_len >= WINDOW + tq and WINDOW % tq == 0
    return pl.pallas_call(
        functools.partial(_win_kernel, tq=tq, n_sub=n_sub),
        out_shape=jax.ShapeDtypeStruct((n_batch, seq_len, NSA_W), bf16),
        grid=(n_batch, NSA_KV_HEADS, seq_len // (tq * n_sub)),
        in_specs=[
            pl.BlockSpec((1, NSA_GROUP, tq * n_sub, HEAD_DIM), lambda b, g, i: (b, g, i, 0)),
            pl.BlockSpec((1, 1, seq_len, HEAD_DIM), lambda b, g, i: (b, g, 0, 0)),
            pl.BlockSpec((1, 1, seq_len // tq, HEAD_DIM, tq), lambda b, g, i: (b, g, 0, 0, 0)),
            pl.BlockSpec((1, tq * n_sub, LANES), lambda b, g, i: (b, i, 0)),
        ],
        out_specs=pl.BlockSpec((1, tq * n_sub, NSA_GROUP * HEAD_DIM), lambda b, g, i: (b, i, g)),
        compiler_params=_params("arbitrary", "arbitrary", "arbitrary"),
        name="win",
    )(nqr, kw, vwt, gates)


def _outproj_kernel(fox_ref, c_ref, s_ref, w_ref, x_ref, mod_ref, wo_ref, o_ref):
    nsa = c_ref[...].astype(f32) + s_ref[...].astype(f32) + w_ref[...].astype(f32)
    a = jnp.concatenate([fox_ref[...], nsa.astype(bf16)], axis=1)
    o_ref[...] = x_ref[...] + mod_ref[0][2:3] * _dot(a, wo_ref[...])


def _outproj(ofox, ocmp, oslc, owin, x2d, mod3, wo_bf16, seq_len):
    m_rows = x2d.shape[0]
    tm = 512
    per_b = seq_len // tm
    half = pl.BlockSpec((tm, FOX_W), lambda i: (i, 0))
    return pl.pallas_call(
        _outproj_kernel,
        out_shape=jax.ShapeDtypeStruct((m_rows, D_MODEL), f32),
        grid=(m_rows // tm,),
        in_specs=[
            half, half, half, half,
            pl.BlockSpec((tm, D_MODEL), lambda i: (i, 0)),
            pl.BlockSpec((1, 6, D_MODEL), lambda i: (i // per_b, 0, 0)),
            pl.BlockSpec((D_MODEL, D_MODEL), lambda i: (0, 0)),
        ],
        out_specs=pl.BlockSpec((tm, D_MODEL), lambda i: (i, 0)),
        compiler_params=_params("arbitrary"),
        name="outproj",
    )(ofox, ocmp, oslc, owin, x2d, mod3, wo_bf16)


def _mlp_kernel(x_ref, mod_ref, g_ref, wu_ref, wd_ref, o_ref, h_ref, acc_ref):
    f = pl.program_id(1)

    @pl.when(f == 0)
    def _():
        md = mod_ref[0]
        y = _rms(x_ref[...], g_ref[...])
        h_ref[...] = (y * (1.0 + md[4:5]) + md[3:4]).astype(bf16)
        acc_ref[...] = jnp.zeros_like(acc_ref)

    u = jnp.maximum(_dot(h_ref[...], wu_ref[...]), 0.0)
    acc_ref[...] += _dot((u * u).astype(bf16), wd_ref[...])

    @pl.when(f == pl.num_programs(1) - 1)
    def _():
        o_ref[...] = x_ref[...] + mod_ref[0][5:6] * acc_ref[...]


def _mlp(x2d, mod3, norm_g, wu_bf16, wd_bf16, seq_len):
    m_rows = x2d.shape[0]
    tm, tf = 512, 1024
    per_b = seq_len // tm
    return pl.pallas_call(
        _mlp_kernel,
        out_shape=jax.ShapeDtypeStruct((m_rows, D_MODEL), f32),
        grid=(m_rows // tm, D_FF // tf),
        in_specs=[
            pl.BlockSpec((tm, D_MODEL), lambda i, f: (i, 0)),
            pl.BlockSpec((1, 6, D_MODEL), lambda i, f: (i // per_b, 0, 0)),
            pl.BlockSpec((1, D_MODEL), lambda i, f: (0, 0)),
            pl.BlockSpec((D_MODEL, tf), lambda i, f: (0, f)),
            pl.BlockSpec((tf, D_MODEL), lambda i, f: (f, 0)),
        ],
        out_specs=pl.BlockSpec((tm, D_MODEL), lambda i, f: (i, 0)),
        scratch_shapes=[pltpu.VMEM((tm, D_MODEL), bf16), pltpu.VMEM((tm, D_MODEL), f32)],
        compiler_params=_params("arbitrary", "arbitrary"),
        name="mlp",
    )(x2d, mod3, norm_g, wu_bf16, wd_bf16)


def _layer(x, c, w_ada, b_ada, norm1_g, w_in, b_forget, fox_q_norm, fox_k_norm, nsa_q_norm,
           cmp_k_norm, slc_k_norm, win_k_norm, cmp_pe_k, cmp_w1_k, cmp_w2_k, cmp_pe_v, cmp_w1_v,
           cmp_w2_v, w_out, norm2_g, w_up, w_down):
    n_batch, seq_len, _ = x.shape
    n_cmp = (seq_len - CMP_LEN) // CMP_STRIDE + 1
    row = lambda v: v.reshape(1, -1)

    z0 = COL_NQ
    nq0 = z0 + FOX_HEADS
    gz0 = nq0 + NSA_W + 6 * KV_W
    w_b = w_in[:, nq0:gz0].astype(bf16)
    w_s = jnp.concatenate([w_in[:, z0:nq0], w_in[:, gz0:],
                           jnp.zeros((D_MODEL, LANES - FOX_HEADS - N_BRANCH * NSA_HEADS), w_in.dtype)],
                          axis=1).astype(bf16)
    half = HEAD_DIM // 2
    inv_freq = ROPE_THETA ** (-jnp.arange(half, dtype=f32) / half)
    inv_freq = jnp.concatenate([inv_freq, inv_freq]).reshape(1, HEAD_DIM)

    mod3 = _ada(c, w_ada, b_ada).reshape(n_batch, 6, D_MODEL)
    x2d = x.reshape(n_batch * seq_len, D_MODEL)
    proj, small, w_up16, w_down16, w_out16 = _proj(x2d, mod3, row(norm1_g), w_in, w_b, w_s, w_up, w_down,
                                                    w_out, seq_len)
    (fq, fk, fvt, cin, nqu, nqr, kc, vc, kaug, vst, kw, vw, gates) = _prep(
        proj, small, n_batch, seq_len, row(fox_q_norm), row(fox_k_norm), row(nsa_q_norm), row(slc_k_norm),
        row(win_k_norm), jnp.pad(b_forget, (0, LANES - FOX_HEADS)).reshape(1, LANES), inv_freq)
    kcmp = _compress(kc, cmp_pe_k, cmp_w1_k.astype(bf16), cmp_w2_k.astype(bf16), row(cmp_k_norm), True)
    vcmp = _compress(vc, cmp_pe_v, cmp_w1_v.astype(bf16), cmp_w2_v.astype(bf16), row(cmp_k_norm), False)
    ofox = _fox(fq, fk, fvt, cin)
    ocmp, selneg = _cmp(nqu, kcmp, vcmp, gates, n_cmp)
    oslc = _slc(nqr, selneg, kaug, vst, gates)
    owin = _win(nqr, kw, vw, gates)
    x1 = _outproj(ofox.reshape(-1, FOX_W), ocmp.reshape(-1, NSA_W), oslc.reshape(-1, NSA_W),
                  owin.reshape(-1, NSA_W), x2d, mod3, w_out16, seq_len)
    x2 = _mlp(x1, mod3, row(norm2_g), w_up16, w_down16, seq_len)
    return x2.reshape(n_batch, seq_len, D_MODEL)


def kernel(x, c, w_ada, b_ada, norm1_g, w_in, b_forget, fox_q_norm, fox_k_norm, nsa_q_norm, cmp_k_norm,
           slc_k_norm, win_k_norm, cmp_pe_k, cmp_w1_k, cmp_w2_k, cmp_pe_v, cmp_w1_v, cmp_w2_v, w_out,
           norm2_g, w_up, w_down):
    depth = w_ada.shape[0]
    for l in range(depth):
        x = _layer(x, c, w_ada[l], b_ada[l], norm1_g[l], w_in[l], b_forget[l], fox_q_norm[l], fox_k_norm[l],
                   nsa_q_norm[l], cmp_k_norm[l], slc_k_norm[l], win_k_norm[l], cmp_pe_k[l], cmp_w1_k[l],
                   cmp_w2_k[l], cmp_pe_v[l], cmp_w1_v[l], cmp_w2_v[l], w_out[l], norm2_g[l], w_up[l],
                   w_down[l])
    return x
```

```python
import functools
import math
from typing import Any, Callable, NamedTuple

import numpy as np
import jax
import jax.numpy as jnp
from jax import lax
from jax.experimental import pallas as pl
from jax.experimental.pallas import tpu as pltpu

D_MODEL = 2048
HEAD_DIM = 128
FOX_HEADS = 8
NSA_HEADS = 8
NSA_KV_HEADS = 2
NSA_GROUP = NSA_HEADS // NSA_KV_HEADS
N_BRANCH = 3
D_FF = 4 * D_MODEL
ROPE_THETA = 10000.0
CMP_LEN = 32
CMP_STRIDE = 16
CMP_HIDDEN = 2 * HEAD_DIM
SEL_BLOCK = 64
SEL_SHIFT = 6
SEL_TOPK = 16
WINDOW = 512
NORM_EPS = 1e-6
ATTN_SCALE = HEAD_DIM ** -0.5
FOX_W = FOX_HEADS * HEAD_DIM
NSA_W = NSA_HEADS * HEAD_DIM
KV_W = NSA_KV_HEADS * HEAD_DIM

LANES = 128
SEL_LANES = LANES
MASK_BIG = 1e30
KV_BLOCK = 512
WIN_TILE = 128
LOG2E = math.log2(math.e)

COL_FQ = 0
COL_FK = COL_FQ + FOX_W
COL_FV = COL_FK + FOX_W
COL_NQ = COL_FV + FOX_W
COL_KC = COL_NQ + NSA_W
COL_VC = COL_KC + KV_W
COL_KS = COL_VC + KV_W
COL_VS = COL_KS + KV_W
COL_KW = COL_VS + KV_W
COL_VW = COL_KW + KV_W
COL_SMALL = COL_VW + KV_W
W_IN_Z0 = 3 * FOX_W
W_IN_NQ0 = W_IN_Z0 + FOX_HEADS
W_IN_GZ0 = W_IN_NQ0 + NSA_W + 6 * KV_W

VMEM_LIMIT = 56 * 1024 * 1024

f32 = jnp.float32
bf16 = jnp.bfloat16


def _params(*sem):
    return pltpu.CompilerParams(dimension_semantics=sem, vmem_limit_bytes=VMEM_LIMIT)


def _dot_nt(a, b):
    return lax.dot_general(a, b, (((1,), (1,)), ((), ())), preferred_element_type=f32)


def _dot(a, b):
    return jnp.dot(a, b, preferred_element_type=f32)


def _split3(x):
    hi = x.astype(bf16)
    r1 = x - hi.astype(f32)
    mid = r1.astype(bf16)
    lo = (r1 - mid.astype(f32)).astype(bf16)
    return hi, mid, lo


def _dot_f32_by_exact(x, w_bf16):
    hi, mid, lo = _split3(x)
    return _dot(hi, w_bf16) + (_dot(mid, w_bf16) + _dot(lo, w_bf16))


def _rms(x, gain):
    ms = jnp.mean(x * x, axis=-1, keepdims=True)
    return x * lax.rsqrt(ms + NORM_EPS) * gain


def _ada_kernel(ct_ref, w_ref, b_ref, win_ref, o_ref, wa_ref, wb_ref, ws_ref, *, n_batch, k_chunk):
    ct = ct_ref[...]
    act = ct * jax.nn.sigmoid(ct)
    rows = []
    for b in range(n_batch):
        col = act[:, b:b + 1]
        acc = b_ref[...]
        for k0 in range(0, D_MODEL, k_chunk):
            acc = acc + jnp.sum(w_ref[k0:k0 + k_chunk, :] * col[k0:k0 + k_chunk], axis=0, keepdims=True)
        rows.append(acc)
    o_ref[...] = jnp.concatenate(rows, axis=0)

    x = win_ref[...]
    wa_ref[...] = x[:, :W_IN_Z0].astype(bf16)
    wb_ref[...] = x[:, W_IN_NQ0:W_IN_GZ0].astype(bf16)
    pad = jnp.zeros((x.shape[0], LANES - (W_IN_NQ0 - W_IN_Z0) - (x.shape[1] - W_IN_GZ0)), f32)
    ws_ref[...] = jnp.concatenate([x[:, W_IN_Z0:W_IN_NQ0], x[:, W_IN_GZ0:], pad], axis=1).astype(bf16)


def _ada(c, w_ada, b_ada, w_in):
    n_batch = c.shape[0]
    n_out = w_ada.shape[1]
    tn = 768
    n_steps = n_out // tn
    assert n_out % tn == 0 and D_MODEL % n_steps == 0
    rows = D_MODEL // n_steps
    d_in = w_in.shape[1]
    return pl.pallas_call(
        functools.partial(_ada_kernel, n_batch=n_batch, k_chunk=256),
        out_shape=[jax.ShapeDtypeStruct((n_batch, n_out), f32),
                   jax.ShapeDtypeStruct((D_MODEL, W_IN_Z0), bf16),
                   jax.ShapeDtypeStruct((D_MODEL, W_IN_GZ0 - W_IN_NQ0), bf16),
                   jax.ShapeDtypeStruct((D_MODEL, LANES), bf16)],
        grid=(n_steps,),
        in_specs=[
            pl.BlockSpec((D_MODEL, n_batch), lambda j: (0, 0)),
            pl.BlockSpec((D_MODEL, tn), lambda j: (0, j)),
            pl.BlockSpec((1, tn), lambda j: (0, j)),
            pl.BlockSpec((rows, d_in), lambda j: (j, 0)),
        ],
        out_specs=[pl.BlockSpec((n_batch, tn), lambda j: (0, j)),
                   pl.BlockSpec((rows, W_IN_Z0), lambda j: (j, 0)),
                   pl.BlockSpec((rows, W_IN_GZ0 - W_IN_NQ0), lambda j: (j, 0)),
                   pl.BlockSpec((rows, LANES), lambda j: (j, 0))],
        compiler_params=_params("arbitrary"),
        name="ada",
    )(c.T, w_ada, b_ada.reshape(1, n_out), w_in)


def _proj_kernel(x_ref, mod_ref, g_ref, wa_ref, wb_ref, ws_ref, wu32_ref, wd32_ref, wo32_ref,
                 o_ref, small_ref, wu16_ref, wd16_ref, wo16_ref, h_ref, *, na, nb):
    j = pl.program_id(1)
    wu16_ref[...] = wu32_ref[...].astype(bf16)
    wd16_ref[...] = wd32_ref[...].astype(bf16)
    wo16_ref[...] = wo32_ref[...].astype(bf16)

    @pl.when(j == 0)
    def _():
        md = mod_ref[0]
        y = _rms(x_ref[...], g_ref[...])
        h_ref[...] = (y * (1.0 + md[1:2]) + md[0:1]).astype(bf16)

    @pl.when(j < na)
    def _():
        o_ref[...] = _dot(h_ref[...], wa_ref[...]).astype(bf16)

    @pl.when((j >= na) & (j < na + nb))
    def _():
        o_ref[...] = _dot(h_ref[...], wb_ref[...]).astype(bf16)

    @pl.when(j == na + nb)
    def _():
        small_ref[...] = _dot(h_ref[...], ws_ref[...])


def _proj(x2d, mod3, norm_g, wa, wb, ws, w_up, w_down, w_out, seq_len):
    m_rows = x2d.shape[0]
    tm, tn = 1024, 512
    assert seq_len % tm == 0 and wa.shape[1] % tn == 0 and wb.shape[1] % tn == 0
    na, nb = wa.shape[1] // tn, wb.shape[1] // tn
    per_b = seq_len // tm
    n_j = na + nb + 1
    n_steps = (m_rows // tm) * n_j
    cast_block = next(c for c in range(LANES, D_FF + 1, LANES) if D_FF % c == 0 and D_FF // c <= n_steps)
    n_cast = D_FF // cast_block
    cast_idx = lambda i, j: jnp.minimum(i * n_j + j, n_cast - 1)
    return pl.pallas_call(
        functools.partial(_proj_kernel, na=na, nb=nb),
        out_shape=[jax.ShapeDtypeStruct((m_rows, (na + nb) * tn), bf16),
                   jax.ShapeDtypeStruct((m_rows, LANES), f32),
                   jax.ShapeDtypeStruct(w_up.shape, bf16), jax.ShapeDtypeStruct(w_down.shape, bf16),
                   jax.ShapeDtypeStruct(w_out.shape, bf16)],
        grid=(m_rows // tm, n_j),
        in_specs=[
            pl.BlockSpec((tm, D_MODEL), lambda i, j: (i, 0)),
            pl.BlockSpec((1, 6, D_MODEL), lambda i, j: (i // per_b, 0, 0)),
            pl.BlockSpec((1, D_MODEL), lambda i, j: (0, 0)),
            pl.BlockSpec((D_MODEL, tn), lambda i, j: (0, jnp.minimum(j, na - 1))),
            pl.BlockSpec((D_MODEL, tn), lambda i, j: (0, jnp.clip(j - na, 0, nb - 1))),
            pl.BlockSpec((D_MODEL, LANES), lambda i, j: (0, 0)),
            pl.BlockSpec((D_MODEL // n_cast, D_FF), lambda i, j: (cast_idx(i, j), 0)),
            pl.BlockSpec((cast_block, D_MODEL), lambda i, j: (cast_idx(i, j), 0)),
            pl.BlockSpec((D_MODEL // n_cast, D_MODEL), lambda i, j: (cast_idx(i, j), 0)),
        ],
        out_specs=[pl.BlockSpec((tm, tn), lambda i, j: (i, jnp.minimum(j, na + nb - 1))),
                   pl.BlockSpec((tm, LANES), lambda i, j: (i, 0)),
                   pl.BlockSpec((D_MODEL // n_cast, D_FF), lambda i, j: (cast_idx(i, j), 0)),
                   pl.BlockSpec((cast_block, D_MODEL), lambda i, j: (cast_idx(i, j), 0)),
                   pl.BlockSpec((D_MODEL // n_cast, D_MODEL), lambda i, j: (cast_idx(i, j), 0))],
        scratch_shapes=[pltpu.VMEM((tm, D_MODEL), bf16)],
        compiler_params=_params("arbitrary", "arbitrary"),
        name="proj",
    )(x2d, mod3, norm_g, wa, wb, ws, w_up, w_down, w_out)


def _prep_kernel(p_ref, small_ref, gq_ref, gk_ref, gn_ref, gs_ref, gw_ref, bf_ref, inv_ref,
                 fq_ref, fk_ref, fvt_ref, cin_ref, nqu_ref, nqr_ref, kc_ref, vc_ref,
                 ks_ref, vst_ref, kw_ref, vw_ref, gate_ref, carry_ref, rot_ref, *, tm):
    i = pl.program_id(1)

    @pl.when(i == 0)
    def _():
        carry_ref[...] = jnp.zeros_like(carry_ref)

    def head(col, h):
        return p_ref[:, col + h * HEAD_DIM: col + (h + 1) * HEAD_DIM].astype(f32)

    row = lax.broadcasted_iota(jnp.int32, (tm, LANES), 0)
    lane = lax.broadcasted_iota(jnp.int32, (tm, LANES), 1)
    pos = i * tm + row

    @pl.when((pl.program_id(0) == 0) & (i == 0))
    def _():
        ang_row = row.astype(f32) * inv_ref[...]
        rot_ref[0] = jnp.cos(ang_row)
        rot_ref[1] = jnp.sin(ang_row)

    ang0 = (i * tm).astype(f32) * inv_ref[...]
    cos0, sin0 = jnp.cos(ang0), jnp.sin(ang0)
    cos = cos0 * rot_ref[0] - sin0 * rot_ref[1]
    sin = sin0 * rot_ref[0] + cos0 * rot_ref[1]
    sin_signed = jnp.where(lane < HEAD_DIM // 2, -sin, sin)

    def rope(x):
        return x * cos + pltpu.roll(x, HEAD_DIM // 2, 1) * sin_signed

    small = small_ref[...]
    z = small + bf_ref[...]
    logf = jnp.minimum(z, 0.0) - jnp.log1p(jnp.exp(-jnp.abs(z)))
    t_idx = lax.broadcasted_iota(jnp.int32, (tm, tm), 0)
    s_idx = lax.broadcasted_iota(jnp.int32, (tm, tm), 1)
    tri = jnp.where(s_idx <= t_idx, 1.0, 0.0).astype(bf16)
    hi, mid, lo = _split3(logf)
    local = _dot(tri, hi) + (_dot(tri, mid) + _dot(tri, lo))
    cin_ref[0, 0] = carry_ref[0:1, :]
    carry_ref[...] = carry_ref[...] + local[tm - 1:tm, :]
    b_hi, b_mid, b_lo = (v.astype(f32) for v in _split3(local * (-LOG2E)))
    ones3 = jnp.where(lane < 3, 1.0, 0.0).astype(bf16)

    for h in range(FOX_HEADS):
        q = _rms(head(COL_FQ, h), gq_ref[...]) * (ATTN_SCALE * LOG2E)
        fq_ref[0, h] = jnp.concatenate([q.astype(bf16), ones3], axis=1)
        k = _rms(head(COL_FK, h), gk_ref[...])
        bias = jnp.where(lane == 0, b_hi[:, h:h + 1],
                         jnp.where(lane == 1, b_mid[:, h:h + 1],
                                   jnp.where(lane == 2, b_lo[:, h:h + 1], 0.0)))
        fk_ref[0, h] = jnp.concatenate([k.astype(bf16), bias.astype(bf16)], axis=1)
        fvt_ref[0, h, 0] = head(COL_FV, h).T.astype(bf16)

    gate_ref[0] = jax.nn.sigmoid(small)

    for h in range(NSA_HEADS):
        qn = _rms(head(COL_NQ, h), gn_ref[...])
        nqu_ref[0, h] = (qn * ATTN_SCALE).astype(bf16)
        nqr_ref[0, h] = rope(qn * (ATTN_SCALE * LOG2E)).astype(bf16)
    onehot = jnp.where(lane == lax.shift_right_logical(pos, SEL_SHIFT), 1.0, 0.0).astype(bf16)
    for g in range(NSA_KV_HEADS):
        kc_ref[0, g] = head(COL_KC, g).astype(bf16)
        vc_ref[0, g] = head(COL_VC, g).astype(bf16)
        ks = rope(_rms(head(COL_KS, g), gs_ref[...])).astype(bf16)
        ks_ref[0, g] = jnp.concatenate([ks, onehot], axis=1)
        vst_ref[0, g, 0] = head(COL_VS, g).T.astype(bf16)
        kw_ref[0, g] = rope(_rms(head(COL_KW, g), gw_ref[...])).astype(bf16)
        vwt = head(COL_VW, g).T.astype(bf16)
        for c in range(tm // WIN_TILE):
            vw_ref[0, g, c] = vwt[:, c * WIN_TILE:(c + 1) * WIN_TILE]


def _prep(proj, small, n_batch, seq_len, gq, gk, gn, gs, gw, b_forget_row, inv_freq):
    tm = KV_BLOCK
    per_b = seq_len // tm
    hshape = lambda n, w, dt: jax.ShapeDtypeStruct((n_batch, n, seq_len, w), dt)
    hspec = lambda n, w: pl.BlockSpec((1, n, tm, w), lambda b, i: (b, 0, i, 0))
    tshape = lambda n: jax.ShapeDtypeStruct((n_batch, n, per_b, HEAD_DIM, tm), bf16)
    tspec = lambda n: pl.BlockSpec((1, n, 1, HEAD_DIM, tm), lambda b, i: (b, 0, i, 0, 0))
    vec = pl.BlockSpec((1, LANES), lambda b, i: (0, 0))
    return pl.pallas_call(
        functools.partial(_prep_kernel, tm=tm),
        out_shape=[
            hshape(FOX_HEADS, 2 * HEAD_DIM, bf16), hshape(FOX_HEADS, 2 * HEAD_DIM, bf16), tshape(FOX_HEADS),
            jax.ShapeDtypeStruct((n_batch, per_b, 1, LANES), f32),
            hshape(NSA_HEADS, HEAD_DIM, bf16), hshape(NSA_HEADS, HEAD_DIM, bf16),
            hshape(NSA_KV_HEADS, HEAD_DIM, bf16), hshape(NSA_KV_HEADS, HEAD_DIM, bf16),
            hshape(NSA_KV_HEADS, HEAD_DIM + SEL_LANES, bf16), tshape(NSA_KV_HEADS),
            hshape(NSA_KV_HEADS, HEAD_DIM, bf16),
            jax.ShapeDtypeStruct((n_batch, NSA_KV_HEADS, seq_len // WIN_TILE, HEAD_DIM, WIN_TILE), bf16),
            jax.ShapeDtypeStruct((n_batch, seq_len, LANES), f32),
        ],
        grid=(n_batch, per_b),
        in_specs=[
            pl.BlockSpec((tm, COL_SMALL), lambda b, i: (b * per_b + i, 0)),
            pl.BlockSpec((tm, LANES), lambda b, i: (b * per_b + i, 0)),
            vec, vec, vec, vec, vec, vec, vec,
        ],
        out_specs=[
            hspec(FOX_HEADS, 2 * HEAD_DIM), hspec(FOX_HEADS, 2 * HEAD_DIM), tspec(FOX_HEADS),
            pl.BlockSpec((1, 1, 1, LANES), lambda b, i: (b, i, 0, 0)),
            hspec(NSA_HEADS, HEAD_DIM), hspec(NSA_HEADS, HEAD_DIM),
            hspec(NSA_KV_HEADS, HEAD_DIM), hspec(NSA_KV_HEADS, HEAD_DIM),
            hspec(NSA_KV_HEADS, HEAD_DIM + SEL_LANES), tspec(NSA_KV_HEADS),
            hspec(NSA_KV_HEADS, HEAD_DIM),
            pl.BlockSpec((1, NSA_KV_HEADS, tm // WIN_TILE, HEAD_DIM, WIN_TILE), lambda b, i: (b, 0, i, 0, 0)),
            pl.BlockSpec((1, tm, LANES), lambda b, i: (b, i, 0)),
        ],
        scratch_shapes=[pltpu.VMEM((8, LANES), f32), pltpu.VMEM((2, tm, LANES), f32)],
        compiler_params=_params("arbitrary", "arbitrary"),
        name="prep",
    )(proj, small, gq, gk, gn, gs, gw, b_forget_row, inv_freq)


def _compress_kernel(x_ref, pe_ref, w1_ref, w2_ref, g_ref, o_ref, *, n_rows, do_norm):
    half = CMP_STRIDE * HEAD_DIM
    x = x_ref[0, 0]
    pe = pe_ref[...]
    xa = (x + pe[:, :half]).astype(bf16)
    xb = (x + pe[:, half:]).astype(bf16)
    a = _dot(xa, w1_ref[0:half, :])
    b = _dot(xb, w1_ref[half:2 * half, :])
    pre = a + pltpu.roll(b, n_rows - 1, 0)
    hid = pre * jax.nn.sigmoid(pre)
    out = _dot(hid.astype(bf16), w2_ref[...])
    if do_norm:
        out = _rms(out, g_ref[...])
    o_ref[0, 0] = out.astype(bf16)


def _compress(x4, pe, w1_bf16, w2_bf16, gain, do_norm):
    n_batch, n_g, seq_len, _ = x4.shape
    n_rows = seq_len // CMP_STRIDE
    half = CMP_STRIDE * HEAD_DIM
    xr = x4.reshape(n_batch, n_g, n_rows, half)
    return pl.pallas_call(
        functools.partial(_compress_kernel, n_rows=n_rows, do_norm=do_norm),
        out_shape=jax.ShapeDtypeStruct((n_batch, n_g, n_rows, HEAD_DIM), bf16),
        grid=(n_batch, n_g),
        in_specs=[
            pl.BlockSpec((1, 1, n_rows, half), lambda b, g: (b, g, 0, 0)),
            pl.BlockSpec((1, 2 * half), lambda b, g: (0, 0)),
            pl.BlockSpec((2 * half, CMP_HIDDEN), lambda b, g: (0, 0)),
            pl.BlockSpec((CMP_HIDDEN, HEAD_DIM), lambda b, g: (0, 0)),
            pl.BlockSpec((1, HEAD_DIM), lambda b, g: (0, 0)),
        ],
        out_specs=pl.BlockSpec((1, 1, n_rows, HEAD_DIM), lambda b, g: (b, g, 0, 0)),
        compiler_params=_params("arbitrary", "arbitrary"),
        name="compress",
    )(xr, pe.reshape(1, 2 * half), w1_bf16, w2_bf16, gain)


def _softmax_block(s, d, m_prev, l_prev):
    m_blk = jnp.max(s, axis=0, keepdims=True)
    m_new = jnp.maximum(m_prev, m_blk if d is None else m_blk + d)
    p = jnp.exp2(s - (m_new if d is None else m_new - d))
    alpha = jnp.exp2(m_prev - m_new)
    l_new = alpha * l_prev + jnp.sum(p, axis=0, keepdims=True)
    return m_new, l_new, alpha, p.astype(bf16)


class _Chain(NamedTuple):
    scores: Callable
    values: Callable
    offset: Callable
    diag_mask: Callable
    emit: Callable
    s_buf: Any
    p_buf: Any


def _flash_sweep(n_full, chains, n_q):
    for c in chains:
        c.s_buf[0] = c.scores(0)
        c.p_buf[1] = jnp.zeros(c.p_buf.shape[1:], c.p_buf.dtype)

    def accumulate(c, j, slot, alpha, acc):
        return alpha * acc + _dot(c.values(jnp.maximum(j, 0)), c.p_buf[slot])

    def step(j, cur, states):
        out = []
        for c, (m, l, acc, alpha_prev) in zip(chains, states):
            acc = accumulate(c, j - 1, 1 - cur, alpha_prev, acc)
            m, l, alpha, p = _softmax_block(c.s_buf[cur], c.offset(j), m, l)
            c.p_buf[cur] = p
            c.s_buf[1 - cur] = c.scores(j + 1)
            out.append((m, l, acc, alpha))
        return tuple(out)

    def finish(cur, states):
        for c, (m, l, acc, alpha_prev) in zip(chains, states):
            acc = accumulate(c, n_full - 1, 1 - cur, alpha_prev, acc)
            m, l, alpha, p = _softmax_block(c.diag_mask(c.s_buf[cur]), None, m, l)
            acc = alpha * acc + _dot(c.values(n_full), p)
            c.emit(acc / l)

    init = tuple((jnp.full((1, n_q), -jnp.inf, f32), jnp.zeros((1, n_q), f32),
                  jnp.zeros((HEAD_DIM, n_q), f32), jnp.ones((1, n_q), f32)) for _ in chains)
    states = lax.fori_loop(0, n_full // 2, lambda jj, st: step(2 * jj + 1, 1, step(2 * jj, 0, st)), init)

    @pl.when(n_full % 2 == 0)
    def _():
        finish(0, states)

    @pl.when(n_full % 2 == 1)
    def _():
        finish(1, step(n_full - 1, 0, states))


def _fox_kernel(q_ref, k_ref, vt_ref, c_ref, o_ref, *bufs, tq, n_heads):
    hb = pl.program_id(1)
    i = pl.program_id(2)
    lane = lax.broadcasted_iota(jnp.int32, (1, LANES), 1)

    def diag_mask(s):
        kk = lax.broadcasted_iota(jnp.int32, (tq, tq), 0)
        qq = lax.broadcasted_iota(jnp.int32, (tq, tq), 1)
        return jnp.where(kk <= qq, s, -jnp.inf)

    def chain(u):
        qa = q_ref[0, u]

        def cin(j):
            return jnp.sum(jnp.where(lane == hb * n_heads + u, c_ref[0, j], 0.0), axis=-1, keepdims=True)

        ci = cin(i)

        def scores(j):
            return _dot_nt(k_ref[0, u, pl.ds(pl.multiple_of(j * tq, tq), tq), :], qa)

        def emit(out_t):
            o_ref[0, :, u * HEAD_DIM:(u + 1) * HEAD_DIM] = out_t.T.astype(bf16)

        return _Chain(scores, lambda j: vt_ref[0, u, j], lambda j: (ci - cin(j)) * LOG2E, diag_mask, emit,
                      bufs[2 * u], bufs[2 * u + 1])

    _flash_sweep(i, [chain(u) for u in range(n_heads)], tq)


def _fox(fq, fk, fvt, cin):
    n_batch, n_h, seq_len, _ = fq.shape
    tq = KV_BLOCK
    nkb = seq_len // tq
    n_heads = 4
    return pl.pallas_call(
        functools.partial(_fox_kernel, tq=tq, n_heads=n_heads),
        out_shape=jax.ShapeDtypeStruct((n_batch, seq_len, n_h * HEAD_DIM), bf16),
        grid=(n_batch, n_h // n_heads, nkb),
        in_specs=[
            pl.BlockSpec((1, n_heads, tq, 2 * HEAD_DIM), lambda b, h, i: (b, h, i, 0)),
            pl.BlockSpec((1, n_heads, seq_len, 2 * HEAD_DIM), lambda b, h, i: (b, h, 0, 0)),
            pl.BlockSpec((1, n_heads, nkb, HEAD_DIM, tq), lambda b, h, i: (b, h, 0, 0, 0)),
            pl.BlockSpec((1, nkb, 1, LANES), lambda b, h, i: (b, 0, 0, 0)),
        ],
        out_specs=pl.BlockSpec((1, tq, n_heads * HEAD_DIM), lambda b, h, i: (b, i, h)),
        scratch_shapes=[pltpu.VMEM((2, tq, tq), f32), pltpu.VMEM((2, tq, tq), bf16)] * n_heads,
        compiler_params=_params("arbitrary", "arbitrary", "arbitrary"),
        name="fox",
    )(fq, fk, fvt, cin)


def _cmp_kernel(q_ref, k_ref, v_ref, gate_ref, ov_ref, o_ref, sel_ref, *, tq, n_rows, n_cmp, n_sel):
    i = pl.program_id(2)
    g = pl.program_id(1)
    hg = NSA_GROUP
    q = q_ref[0].reshape(hg * tq, HEAD_DIM)
    s = _dot_nt(q, k_ref[0, 0]).reshape(hg, tq, n_rows)
    t = i * tq + lax.broadcasted_iota(jnp.int32, (1, tq, 1), 1)
    n = lax.broadcasted_iota(jnp.int32, (1, 1, n_rows), 2)
    valid = (n * CMP_STRIDE + (CMP_LEN - 1) <= t) & (n < n_cmp)
    s = jnp.where(valid, s, -jnp.inf)
    m = jnp.max(s, axis=-1, keepdims=True)
    m = jnp.where(m == -jnp.inf, 0.0, m)
    e = jnp.exp(s - m)
    p = e / jnp.maximum(jnp.sum(e, axis=-1, keepdims=True), 1e-30)
    o = _dot(p.reshape(hg * tq, n_rows).astype(bf16), v_ref[0, 0])
    _write_gated(o_ref, gate_ref, o, g, 0, tq)

    lane = lax.broadcasted_iota(jnp.int32, (tq, LANES), 1)
    psum = p[0] + p[1] + p[2] + p[3]
    imp = _dot_f32_by_exact(psum, ov_ref[...])
    tq_pos = i * tq + lax.broadcasted_iota(jnp.int32, (tq, 1), 0)
    q_blk = lax.shift_right_logical(tq_pos, SEL_SHIFT)
    causal = lane <= q_blk
    forced = (lane == 0) | (lane == q_blk) | (lane == q_blk - 1)
    key = jnp.where(forced, jnp.inf, jnp.where(causal, imp, -1.0))
    key = jnp.where(lane < n_sel, key, -3.0)
    lane_f = lane.astype(f32)
    sel = jnp.zeros((tq, LANES), jnp.bool_)
    for _ in range(min(SEL_TOPK, n_sel)):
        mx = jnp.max(key, axis=-1, keepdims=True)
        first = jnp.min(jnp.where(key == mx, lane_f, float(LANES)), axis=-1, keepdims=True)
        pick = lane_f == first
        sel = sel | pick
        key = jnp.where(pick, -2.0, key)
    sel_ref[0, 0] = jnp.where(sel & causal, 0.0, -MASK_BIG).astype(bf16)


def _cmp(nqu, kcmp, vcmp, gates, n_cmp):
    n_batch, _, seq_len, _ = nqu.shape
    n_rows = kcmp.shape[2]
    n_sel = seq_len // SEL_BLOCK
    assert n_sel <= SEL_LANES
    tq = 1024
    c0 = np.arange(n_rows) * CMP_STRIDE
    s0 = np.arange(SEL_LANES) * SEL_BLOCK
    overlap = np.clip(np.minimum(c0[:, None] + CMP_LEN, s0[None, :] + SEL_BLOCK)
                      - np.maximum(c0[:, None], s0[None, :]), 0, None).astype(np.float32) / CMP_LEN
    overlap[n_cmp:, :] = 0.0
    overlap[:, n_sel:] = 0.0
    return pl.pallas_call(
        functools.partial(_cmp_kernel, tq=tq, n_rows=n_rows, n_cmp=n_cmp, n_sel=n_sel),
        out_shape=[
            jax.ShapeDtypeStruct((n_batch, seq_len, NSA_W), bf16),
            jax.ShapeDtypeStruct((n_batch, NSA_KV_HEADS, seq_len, SEL_LANES), bf16),
        ],
        grid=(n_batch, NSA_KV_HEADS, seq_len // tq),
        in_specs=[
            pl.BlockSpec((1, NSA_GROUP, tq, HEAD_DIM), lambda b, g, i: (b, g, i, 0)),
            pl.BlockSpec((1, 1, n_rows, HEAD_DIM), lambda b, g, i: (b, g, 0, 0)),
            pl.BlockSpec((1, 1, n_rows, HEAD_DIM), lambda b, g, i: (b, g, 0, 0)),
            pl.BlockSpec((1, tq, LANES), lambda b, g, i: (b, i, 0)),
            pl.BlockSpec((n_rows, SEL_LANES), lambda b, g, i: (0, 0)),
        ],
        out_specs=[
            pl.BlockSpec((1, tq, NSA_GROUP * HEAD_DIM), lambda b, g, i: (b, i, g)),
            pl.BlockSpec((1, 1, tq, SEL_LANES), lambda b, g, i: (b, g, i, 0)),
        ],
        compiler_params=_params("arbitrary", "arbitrary", "arbitrary"),
        name="cmp",
    )(nqu, kcmp, vcmp, gates, jnp.asarray(overlap, dtype=bf16))


def _write_gated(o_ref, gate_ref, out, g, branch, tq):
    gates = gate_ref[0]
    lane = lax.broadcasted_iota(jnp.int32, (tq, LANES), 1)
    for h in range(NSA_GROUP):
        gcol = FOX_HEADS + (g * NSA_GROUP + h) * N_BRANCH + branch
        gh = jnp.sum(jnp.where(lane == gcol, gates, 0.0), axis=-1, keepdims=True)
        o_ref[0, :, h * HEAD_DIM:(h + 1) * HEAD_DIM] = (gh * out[h * tq:(h + 1) * tq]).astype(bf16)


def _slc_kernel(q_ref, sel_ref, k_ref, vt_ref, gate_ref, o_ref, *bufs, tq, tk):
    i = pl.program_id(1)
    hg = NSA_GROUP
    n_q = hg * tq
    q0 = i * tq
    jd = q0 // tk

    def diag_mask(s):
        kp = jd * tk + lax.broadcasted_iota(jnp.int32, (tk, n_q), 0)
        t = q0 + (lax.broadcasted_iota(jnp.int32, (tk, n_q), 1) & (tq - 1))
        return jnp.where(kp <= t, s, -jnp.inf)

    def chain(g):
        q = q_ref[0, g * hg:(g + 1) * hg].reshape(n_q, HEAD_DIM)
        qa = jnp.concatenate([q, jnp.concatenate([sel_ref[0, g]] * hg, axis=0)], axis=1)

        def scores(j):
            return _dot_nt(k_ref[0, g, pl.ds(pl.multiple_of(j * tk, tk), tk), :], qa)

        def emit(out):
            gates = gate_ref[0]
            lane = lax.broadcasted_iota(jnp.int32, (tq, LANES), 1)
            for h in range(hg):
                head = g * hg + h
                gh = jnp.sum(jnp.where(lane == FOX_HEADS + head * N_BRANCH + 1, gates, 0.0),
                             axis=-1, keepdims=True)
                o_ref[0, :, head * HEAD_DIM:(head + 1) * HEAD_DIM] = (
                    gh * out[:, h * tq:(h + 1) * tq].T).astype(bf16)

        return _Chain(scores, lambda j: vt_ref[0, g, j], lambda j: None, diag_mask, emit,
                      bufs[2 * g], bufs[2 * g + 1])

    _flash_sweep(jd, [chain(g) for g in range(NSA_KV_HEADS)], n_q)


def _slc(nqr, selneg, kaug, vst, gates):
    n_batch, _, seq_len, _ = nqr.shape
    tq, tk = 256, KV_BLOCK
    nkb = seq_len // tk
    assert tq & (tq - 1) == 0
    n_q = NSA_GROUP * tq
    return pl.pallas_call(
        functools.partial(_slc_kernel, tq=tq, tk=tk),
        out_shape=jax.ShapeDtypeStruct((n_batch, seq_len, NSA_W), bf16),
        grid=(n_batch, seq_len // tq),
        in_specs=[
            pl.BlockSpec((1, NSA_HEADS, tq, HEAD_DIM), lambda b, i: (b, 0, i, 0)),
            pl.BlockSpec((1, NSA_KV_HEADS, tq, SEL_LANES), lambda b, i: (b, 0, i, 0)),
            pl.BlockSpec((1, NSA_KV_HEADS, seq_len, HEAD_DIM + SEL_LANES), lambda b, i: (b, 0, 0, 0)),
            pl.BlockSpec((1, NSA_KV_HEADS, nkb, HEAD_DIM, tk), lambda b, i: (b, 0, 0, 0, 0)),
            pl.BlockSpec((1, tq, LANES), lambda b, i: (b, i, 0)),
        ],
        out_specs=pl.BlockSpec((1, tq, NSA_W), lambda b, i: (b, i, 0)),
        scratch_shapes=[pltpu.VMEM((2, tk, n_q), f32), pltpu.VMEM((2, tk, n_q), bf16)] * NSA_KV_HEADS,
        compiler_params=_params("arbitrary", "arbitrary"),
        name="slc",
    )(nqr, selneg, kaug, vst, gates)


def _win_kernel(q_ref, k_ref, vt_ref, gate_ref, o_ref, *, tq, n_sub):
    i = pl.program_id(2)
    g = pl.program_id(1)
    hg = NSA_GROUP
    span = WINDOW + tq
    q0s = [(i * n_sub + u) * tq for u in range(n_sub)]
    k0s = [pl.multiple_of(jnp.maximum(q0 - WINDOW, 0), tq) for q0 in q0s]
    scores = []
    for u in range(n_sub):
        q = q_ref[0, :, u * tq:(u + 1) * tq, :].reshape(hg * tq, HEAD_DIM)
        scores.append(_dot_nt(k_ref[0, 0, pl.ds(k0s[u], span), :], q))
    probs, denoms = [], []
    for u in range(n_sub):
        kp = k0s[u] + lax.broadcasted_iota(jnp.int32, (span, tq), 0)
        t = q0s[u] + lax.broadcasted_iota(jnp.int32, (span, tq), 1)
        diff = t - kp
        bias = jnp.where((diff >= 0) & (diff < WINDOW), 0.0, -jnp.inf)
        s = jnp.concatenate([scores[u][:, h * tq:(h + 1) * tq] + bias for h in range(hg)], axis=1)
        e = jnp.exp2(s - jnp.max(s, axis=0, keepdims=True))
        denoms.append(jnp.sum(e, axis=0, keepdims=True))
        probs.append(e.astype(bf16))
    gates = gate_ref[0]
    lane = lax.broadcasted_iota(jnp.int32, (tq, LANES), 1)
    for u in range(n_sub):
        jb = k0s[u] // tq
        vt = jnp.concatenate([vt_ref[0, 0, jb + c] for c in range(span // tq)], axis=1)
        out = _dot(vt, probs[u]) / denoms[u]
        for h in range(hg):
            gcol = FOX_HEADS + (g * hg + h) * N_BRANCH + 2
            gh = jnp.sum(jnp.where(lane == gcol, gates[u * tq:(u + 1) * tq], 0.0), axis=-1, keepdims=True)
            o_ref[0, u * tq:(u + 1) * tq, h * HEAD_DIM:(h + 1) * HEAD_DIM] = (
                gh * out[:, h * tq:(h + 1) * tq].T).astype(bf16)


def _win(nqr, kw, vwt, gates):
    n_batch, _, seq_len, _ = nqr.shape
    tq, n_sub = WIN_TILE, 4
    assert seq_len >= WINDOW + tq and WINDOW % tq == 0
    return pl.pallas_call(
        functools.partial(_win_kernel, tq=tq, n_sub=n_sub),
        out_shape=jax.ShapeDtypeStruct((n_batch, seq_len, NSA_W), bf16),
        grid=(n_batch, NSA_KV_HEADS, seq_len // (tq * n_sub)),
        in_specs=[
            pl.BlockSpec((1, NSA_GROUP, tq * n_sub, HEAD_DIM), lambda b, g, i: (b, g, i, 0)),
            pl.BlockSpec((1, 1, seq_len, HEAD_DIM), lambda b, g, i: (b, g, 0, 0)),
            pl.BlockSpec((1, 1, seq_len // tq, HEAD_DIM, tq), lambda b, g, i: (b, g, 0, 0, 0)),
            pl.BlockSpec((1, tq * n_sub, LANES), lambda b, g, i: (b, i, 0)),
        ],
        out_specs=pl.BlockSpec((1, tq * n_sub, NSA_GROUP * HEAD_DIM), lambda b, g, i: (b, i, g)),
        compiler_params=_params("arbitrary", "arbitrary", "arbitrary"),
        name="win",
    )(nqr, kw, vwt, gates)


def _outproj_kernel(fox_ref, c_ref, s_ref, w_ref, x_ref, mod_ref, wo_ref, o_ref):
    nsa = c_ref[...].astype(f32) + s_ref[...].astype(f32) + w_ref[...].astype(f32)
    a = jnp.concatenate([fox_ref[...], nsa.astype(bf16)], axis=1)
    o_ref[...] = x_ref[...] + mod_ref[0][2:3] * _dot(a, wo_ref[...])


def _outproj(ofox, ocmp, oslc, owin, x2d, mod3, wo_bf16, seq_len):
    m_rows = x2d.shape[0]
    tm = 512
    per_b = seq_len // tm
    half = pl.BlockSpec((tm, FOX_W), lambda i: (i, 0))
    return pl.pallas_call(
        _outproj_kernel,
        out_shape=jax.ShapeDtypeStruct((m_rows, D_MODEL), f32),
        grid=(m_rows // tm,),
        in_specs=[
            half, half, half, half,
            pl.BlockSpec((tm, D_MODEL), lambda i: (i, 0)),
            pl.BlockSpec((1, 6, D_MODEL), lambda i: (i // per_b, 0, 0)),
            pl.BlockSpec((D_MODEL, D_MODEL), lambda i: (0, 0)),
        ],
        out_specs=pl.BlockSpec((tm, D_MODEL), lambda i: (i, 0)),
        compiler_params=_params("arbitrary"),
        name="outproj",
    )(ofox, ocmp, oslc, owin, x2d, mod3, wo_bf16)


def _mlp_kernel(x_ref, mod_ref, g_ref, wu_ref, wd_ref, o_ref, h_ref, acc_ref):
    f = pl.program_id(1)

    @pl.when(f == 0)
    def _():
        md = mod_ref[0]
        y = _rms(x_ref[...], g_ref[...])
        h_ref[...] = (y * (1.0 + md[4:5]) + md[3:4]).astype(bf16)
        acc_ref[...] = jnp.zeros_like(acc_ref)

    u = jnp.maximum(_dot(h_ref[...], wu_ref[...]), 0.0)
    acc_ref[...] += _dot((u * u).astype(bf16), wd_ref[...])

    @pl.when(f == pl.num_programs(1) - 1)
    def _():
        o_ref[...] = x_ref[...] + mod_ref[0][5:6] * acc_ref[...]


def _mlp(x2d, mod3, norm_g, wu_bf16, wd_bf16, seq_len):
    m_rows = x2d.shape[0]
    tm, tf = 512, 1024
    per_b = seq_len // tm
    return pl.pallas_call(
        _mlp_kernel,
        out_shape=jax.ShapeDtypeStruct((m_rows, D_MODEL), f32),
        grid=(m_rows // tm, D_FF // tf),
        in_specs=[
            pl.BlockSpec((tm, D_MODEL), lambda i, f: (i, 0)),
            pl.BlockSpec((1, 6, D_MODEL), lambda i, f: (i // per_b, 0, 0)),
            pl.BlockSpec((1, D_MODEL), lambda i, f: (0, 0)),
            pl.BlockSpec((D_MODEL, tf), lambda i, f: (0, f)),
            pl.BlockSpec((tf, D_MODEL), lambda i, f: (f, 0)),
        ],
        out_specs=pl.BlockSpec((tm, D_MODEL), lambda i, f: (i, 0)),
        scratch_shapes=[pltpu.VMEM((tm, D_MODEL), bf16), pltpu.VMEM((tm, D_MODEL), f32)],
        compiler_params=_params("arbitrary", "arbitrary"),
        name="mlp",
    )(x2d, mod3, norm_g, wu_bf16, wd_bf16)


def _layer(x, c, w_ada, b_ada, norm1_g, w_in, b_forget, fox_q_norm, fox_k_norm, nsa_q_norm,
           cmp_k_norm, slc_k_norm, win_k_norm, cmp_pe_k, cmp_w1_k, cmp_w2_k, cmp_pe_v, cmp_w1_v,
           cmp_w2_v, w_out, norm2_g, w_up, w_down):
    n_batch, seq_len, _ = x.shape
    n_cmp = (seq_len - CMP_LEN) // CMP_STRIDE + 1
    row = lambda v: v.reshape(1, -1)

    half = HEAD_DIM // 2
    inv_freq = ROPE_THETA ** (-jnp.arange(half, dtype=f32) / half)
    inv_freq = jnp.concatenate([inv_freq, inv_freq]).reshape(1, HEAD_DIM)

    mod, w_a, w_b, w_s = _ada(c, w_ada, b_ada, w_in)
    mod3 = mod.reshape(n_batch, 6, D_MODEL)
    x2d = x.reshape(n_batch * seq_len, D_MODEL)
    proj, small, w_up16, w_down16, w_out16 = _proj(x2d, mod3, row(norm1_g), w_a, w_b, w_s, w_up, w_down,
                                                    w_out, seq_len)
    (fq, fk, fvt, cin, nqu, nqr, kc, vc, kaug, vst, kw, vw, gates) = _prep(
        proj, small, n_batch, seq_len, row(fox_q_norm), row(fox_k_norm), row(nsa_q_norm), row(slc_k_norm),
        row(win_k_norm), jnp.pad(b_forget, (0, LANES - FOX_HEADS)).reshape(1, LANES), inv_freq)
    kcmp = _compress(kc, cmp_pe_k, cmp_w1_k.astype(bf16), cmp_w2_k.astype(bf16), row(cmp_k_norm), True)
    vcmp = _compress(vc, cmp_pe_v, cmp_w1_v.astype(bf16), cmp_w2_v.astype(bf16), row(cmp_k_norm), False)
    ofox = _fox(fq, fk, fvt, cin)
    ocmp, selneg = _cmp(nqu, kcmp, vcmp, gates, n_cmp)
    oslc = _slc(nqr, selneg, kaug, vst, gates)
    owin = _win(nqr, kw, vw, gates)
    x1 = _outproj(ofox.reshape(-1, FOX_W), ocmp.reshape(-1, NSA_W), oslc.reshape(-1, NSA_W),
                  owin.reshape(-1, NSA_W), x2d, mod3, w_out16, seq_len)
    x2 = _mlp(x1, mod3, row(norm2_g), w_up16, w_down16, seq_len)
    return x2.reshape(n_batch, seq_len, D_MODEL)


def kernel(x, c, w_ada, b_ada, norm1_g, w_in, b_forget, fox_q_norm, fox_k_norm, nsa_q_norm, cmp_k_norm,
           slc_k_norm, win_k_norm, cmp_pe_k, cmp_w1_k, cmp_w2_k, cmp_pe_v, cmp_w1_v, cmp_w2_v, w_out,
           norm2_g, w_up, w_down):
    depth = w_ada.shape[0]
    for l in range(depth):
        x = _layer(x, c, w_ada[l], b_ada[l], norm1_g[l], w_in[l], b_forget[l], fox_q_norm[l], fox_k_norm[l],
                   nsa_q_norm[l], cmp_k_norm[l], slc_k_norm[l], win_k_norm[l], cmp_pe_k[l], cmp_w1_k[l],
                   cmp_w2_k[l], cmp_pe_v[l], cmp_w1_v[l], cmp_w2_v[l], w_out[l], norm2_g[l], w_up[l],
                   w_down[l])
    return x
```

```python
import functools
import math
from typing import Any, Callable, NamedTuple

import numpy as np
import jax
import jax.numpy as jnp
from jax import lax
from jax.experimental import pallas as pl
from jax.experimental.pallas import tpu as pltpu

D_MODEL = 2048
HEAD_DIM = 128
FOX_HEADS = 8
NSA_HEADS = 8
NSA_KV_HEADS = 2
NSA_GROUP = NSA_HEADS // NSA_KV_HEADS
N_BRANCH = 3
D_FF = 4 * D_MODEL
ROPE_THETA = 10000.0
CMP_LEN = 32
CMP_STRIDE = 16
CMP_HIDDEN = 2 * HEAD_DIM
SEL_BLOCK = 64
SEL_SHIFT = 6
SEL_TOPK = 16
WINDOW = 512
NORM_EPS = 1e-6
ATTN_SCALE = HEAD_DIM ** -0.5
FOX_W = FOX_HEADS * HEAD_DIM
NSA_W = NSA_HEADS * HEAD_DIM
KV_W = NSA_KV_HEADS * HEAD_DIM

LANES = 128
SEL_LANES = LANES
MASK_BIG = 1e30
KV_BLOCK = 512
WIN_TILE = 128
LOG2E = math.log2(math.e)

COL_FQ = 0
COL_FK = COL_FQ + FOX_W
COL_FV = COL_FK + FOX_W
COL_NQ = COL_FV + FOX_W
COL_KC = COL_NQ + NSA_W
COL_VC = COL_KC + KV_W
COL_KS = COL_VC + KV_W
COL_VS = COL_KS + KV_W
COL_KW = COL_VS + KV_W
COL_VW = COL_KW + KV_W
COL_SMALL = COL_VW + KV_W
W_IN_Z0 = 3 * FOX_W
W_IN_NQ0 = W_IN_Z0 + FOX_HEADS
W_IN_GZ0 = W_IN_NQ0 + NSA_W + 6 * KV_W

VMEM_LIMIT = 56 * 1024 * 1024

f32 = jnp.float32
bf16 = jnp.bfloat16


def _params(*sem):
    return pltpu.CompilerParams(dimension_semantics=sem, vmem_limit_bytes=VMEM_LIMIT)


def _dot_nt(a, b):
    return lax.dot_general(a, b, (((1,), (1,)), ((), ())), preferred_element_type=f32)


def _dot(a, b):
    return jnp.dot(a, b, preferred_element_type=f32)


def _split3(x):
    hi = x.astype(bf16)
    r1 = x - hi.astype(f32)
    mid = r1.astype(bf16)
    lo = (r1 - mid.astype(f32)).astype(bf16)
    return hi, mid, lo


def _dot_f32_by_exact(x, w_bf16):
    hi, mid, lo = _split3(x)
    return _dot(hi, w_bf16) + (_dot(mid, w_bf16) + _dot(lo, w_bf16))


def _rms(x, gain):
    ms = jnp.mean(x * x, axis=-1, keepdims=True)
    return x * lax.rsqrt(ms + NORM_EPS) * gain


def _ada_kernel(ct_ref, w_ref, b_ref, win_ref, o_ref, wa_ref, wb_ref, ws_ref, *, n_batch, k_chunk):
    ct = ct_ref[...]
    act = ct * jax.nn.sigmoid(ct)
    rows = []
    for b in range(n_batch):
        col = act[:, b:b + 1]
        acc = b_ref[...]
        for k0 in range(0, D_MODEL, k_chunk):
            acc = acc + jnp.sum(w_ref[k0:k0 + k_chunk, :] * col[k0:k0 + k_chunk], axis=0, keepdims=True)
        rows.append(acc)
    o_ref[...] = jnp.concatenate(rows, axis=0)

    x = win_ref[0]
    wa_ref[...] = x[:, :W_IN_Z0].astype(bf16)
    wb_ref[...] = x[:, W_IN_NQ0:W_IN_GZ0].astype(bf16)
    pad = jnp.zeros((x.shape[0], LANES - (W_IN_NQ0 - W_IN_Z0) - (x.shape[1] - W_IN_GZ0)), f32)
    ws_ref[...] = jnp.concatenate([x[:, W_IN_Z0:W_IN_NQ0], x[:, W_IN_GZ0:], pad], axis=1).astype(bf16)


def _ada(c, w_ada, b_ada, w_in_all, layer):
    n_batch = c.shape[0]
    n_out = w_ada.shape[1]
    tn = 768
    n_steps = n_out // tn
    assert n_out % tn == 0 and D_MODEL % n_steps == 0
    rows = D_MODEL // n_steps
    d_in = w_in_all.shape[2]
    return pl.pallas_call(
        functools.partial(_ada_kernel, n_batch=n_batch, k_chunk=256),
        out_shape=[jax.ShapeDtypeStruct((n_batch, n_out), f32),
                   jax.ShapeDtypeStruct((D_MODEL, W_IN_Z0), bf16),
                   jax.ShapeDtypeStruct((D_MODEL, W_IN_GZ0 - W_IN_NQ0), bf16),
                   jax.ShapeDtypeStruct((D_MODEL, LANES), bf16)],
        grid=(n_steps,),
        in_specs=[
            pl.BlockSpec((D_MODEL, n_batch), lambda j: (0, 0)),
            pl.BlockSpec((D_MODEL, tn), lambda j: (0, j)),
            pl.BlockSpec((1, tn), lambda j: (0, j)),
            pl.BlockSpec((1, rows, d_in), lambda j: (layer, j, 0)),
        ],
        out_specs=[pl.BlockSpec((n_batch, tn), lambda j: (0, j)),
                   pl.BlockSpec((rows, W_IN_Z0), lambda j: (j, 0)),
                   pl.BlockSpec((rows, W_IN_GZ0 - W_IN_NQ0), lambda j: (j, 0)),
                   pl.BlockSpec((rows, LANES), lambda j: (j, 0))],
        compiler_params=_params("arbitrary"),
        name="ada",
    )(c.T, w_ada, b_ada.reshape(1, n_out), w_in_all)


def _proj_kernel(x_ref, mod_ref, g_ref, wa_ref, wb_ref, ws_ref, o_ref, small_ref, h_ref, *, na, nb):
    j = pl.program_id(1)

    @pl.when(j == 0)
    def _():
        md = mod_ref[0]
        y = _rms(x_ref[...], g_ref[...])
        h_ref[...] = (y * (1.0 + md[1:2]) + md[0:1]).astype(bf16)

    @pl.when(j < na)
    def _():
        o_ref[...] = _dot(h_ref[...], wa_ref[...]).astype(bf16)

    @pl.when((j >= na) & (j < na + nb))
    def _():
        o_ref[...] = _dot(h_ref[...], wb_ref[...]).astype(bf16)

    @pl.when(j == na + nb)
    def _():
        small_ref[...] = _dot(h_ref[...], ws_ref[...])


def _cast_side_job(weights, n_steps, step_of):
    n_cast = max(c for c in range(1, n_steps + 1) if all(w.shape[0] % (16 * c) == 0 for w in weights))
    idx = lambda *ids: (jnp.minimum(step_of(*ids), n_cast - 1), 0)
    specs = [pl.BlockSpec((w.shape[0] // n_cast, w.shape[1]), idx) for w in weights]
    return specs, specs, [jax.ShapeDtypeStruct(w.shape, bf16) for w in weights]


def _proj(x2d, mod3, norm_g, wa, wb, ws, seq_len):
    m_rows = x2d.shape[0]
    tm, tn = 1024, 512
    assert seq_len % tm == 0 and wa.shape[1] % tn == 0 and wb.shape[1] % tn == 0
    na, nb = wa.shape[1] // tn, wb.shape[1] // tn
    per_b = seq_len // tm
    return pl.pallas_call(
        functools.partial(_proj_kernel, na=na, nb=nb),
        out_shape=[jax.ShapeDtypeStruct((m_rows, (na + nb) * tn), bf16),
                   jax.ShapeDtypeStruct((m_rows, LANES), f32)],
        grid=(m_rows // tm, na + nb + 1),
        in_specs=[
            pl.BlockSpec((tm, D_MODEL), lambda i, j: (i, 0)),
            pl.BlockSpec((1, 6, D_MODEL), lambda i, j: (i // per_b, 0, 0)),
            pl.BlockSpec((1, D_MODEL), lambda i, j: (0, 0)),
            pl.BlockSpec((D_MODEL, tn), lambda i, j: (0, jnp.minimum(j, na - 1))),
            pl.BlockSpec((D_MODEL, tn), lambda i, j: (0, jnp.clip(j - na, 0, nb - 1))),
            pl.BlockSpec((D_MODEL, LANES), lambda i, j: (0, 0)),
        ],
        out_specs=[pl.BlockSpec((tm, tn), lambda i, j: (i, jnp.minimum(j, na + nb - 1))),
                   pl.BlockSpec((tm, LANES), lambda i, j: (i, 0))],
        scratch_shapes=[pltpu.VMEM((tm, D_MODEL), bf16)],
        compiler_params=_params("arbitrary", "arbitrary"),
        name="proj",
    )(x2d, mod3, norm_g, wa, wb, ws)


def _prep_kernel(p_ref, small_ref, gq_ref, gk_ref, gn_ref, gs_ref, gw_ref, bf_ref, inv_ref,
                 fq_ref, fk_ref, fvt_ref, cin_ref, nqu_ref, nqr_ref, kc_ref, vc_ref,
                 ks_ref, vst_ref, kw_ref, vw_ref, gate_ref, carry_ref, rot_ref, *, tm):
    i = pl.program_id(1)

    @pl.when(i == 0)
    def _():
        carry_ref[...] = jnp.zeros_like(carry_ref)

    def head(col, h):
        return p_ref[:, col + h * HEAD_DIM: col + (h + 1) * HEAD_DIM].astype(f32)

    row = lax.broadcasted_iota(jnp.int32, (tm, LANES), 0)
    lane = lax.broadcasted_iota(jnp.int32, (tm, LANES), 1)
    pos = i * tm + row

    @pl.when((pl.program_id(0) == 0) & (i == 0))
    def _():
        ang_row = row.astype(f32) * inv_ref[...]
        rot_ref[0] = jnp.cos(ang_row)
        rot_ref[1] = jnp.sin(ang_row)

    ang0 = (i * tm).astype(f32) * inv_ref[...]
    cos0, sin0 = jnp.cos(ang0), jnp.sin(ang0)
    cos = cos0 * rot_ref[0] - sin0 * rot_ref[1]
    sin = sin0 * rot_ref[0] + cos0 * rot_ref[1]
    sin_signed = jnp.where(lane < HEAD_DIM // 2, -sin, sin)

    def rope(x):
        return x * cos + pltpu.roll(x, HEAD_DIM // 2, 1) * sin_signed

    small = small_ref[...]
    z = small + bf_ref[...]
    logf = jnp.minimum(z, 0.0) - jnp.log1p(jnp.exp(-jnp.abs(z)))
    t_idx = lax.broadcasted_iota(jnp.int32, (tm, tm), 0)
    s_idx = lax.broadcasted_iota(jnp.int32, (tm, tm), 1)
    tri = jnp.where(s_idx <= t_idx, 1.0, 0.0).astype(bf16)
    hi, mid, lo = _split3(logf)
    local = _dot(tri, hi) + (_dot(tri, mid) + _dot(tri, lo))
    cin_ref[0, 0] = carry_ref[0:1, :]
    carry_ref[...] = carry_ref[...] + local[tm - 1:tm, :]
    b_hi, b_mid, b_lo = (v.astype(f32) for v in _split3(local * (-LOG2E)))
    ones3 = jnp.where(lane < 3, 1.0, 0.0).astype(bf16)

    for h in range(FOX_HEADS):
        q = _rms(head(COL_FQ, h), gq_ref[...]) * (ATTN_SCALE * LOG2E)
        fq_ref[0, h] = jnp.concatenate([q.astype(bf16), ones3], axis=1)
        k = _rms(head(COL_FK, h), gk_ref[...])
        bias = jnp.where(lane == 0, b_hi[:, h:h + 1],
                         jnp.where(lane == 1, b_mid[:, h:h + 1],
                                   jnp.where(lane == 2, b_lo[:, h:h + 1], 0.0)))
        fk_ref[0, h] = jnp.concatenate([k.astype(bf16), bias.astype(bf16)], axis=1)
        fvt_ref[0, h, 0] = head(COL_FV, h).T.astype(bf16)

    gate_ref[0] = jax.nn.sigmoid(small)

    for h in range(NSA_HEADS):
        qn = _rms(head(COL_NQ, h), gn_ref[...])
        nqu_ref[0, h] = (qn * ATTN_SCALE).astype(bf16)
        nqr_ref[0, h] = rope(qn * (ATTN_SCALE * LOG2E)).astype(bf16)
    onehot = jnp.where(lane == lax.shift_right_logical(pos, SEL_SHIFT), 1.0, 0.0).astype(bf16)
    for g in range(NSA_KV_HEADS):
        kc_ref[0, g] = head(COL_KC, g).astype(bf16)
        vc_ref[0, g] = head(COL_VC, g).astype(bf16)
        ks = rope(_rms(head(COL_KS, g), gs_ref[...])).astype(bf16)
        ks_ref[0, g] = jnp.concatenate([ks, onehot], axis=1)
        vst_ref[0, g, 0] = head(COL_VS, g).T.astype(bf16)
        kw_ref[0, g] = rope(_rms(head(COL_KW, g), gw_ref[...])).astype(bf16)
        vwt = head(COL_VW, g).T.astype(bf16)
        for c in range(tm // WIN_TILE):
            vw_ref[0, g, c] = vwt[:, c * WIN_TILE:(c + 1) * WIN_TILE]


def _prep(proj, small, n_batch, seq_len, gq, gk, gn, gs, gw, b_forget_row, inv_freq):
    tm = KV_BLOCK
    per_b = seq_len // tm
    hshape = lambda n, w, dt: jax.ShapeDtypeStruct((n_batch, n, seq_len, w), dt)
    hspec = lambda n, w: pl.BlockSpec((1, n, tm, w), lambda b, i: (b, 0, i, 0))
    tshape = lambda n: jax.ShapeDtypeStruct((n_batch, n, per_b, HEAD_DIM, tm), bf16)
    tspec = lambda n: pl.BlockSpec((1, n, 1, HEAD_DIM, tm), lambda b, i: (b, 0, i, 0, 0))
    vec = pl.BlockSpec((1, LANES), lambda b, i: (0, 0))
    return pl.pallas_call(
        functools.partial(_prep_kernel, tm=tm),
        out_shape=[
            hshape(FOX_HEADS, 2 * HEAD_DIM, bf16), hshape(FOX_HEADS, 2 * HEAD_DIM, bf16), tshape(FOX_HEADS),
            jax.ShapeDtypeStruct((n_batch, per_b, 1, LANES), f32),
            hshape(NSA_HEADS, HEAD_DIM, bf16), hshape(NSA_HEADS, HEAD_DIM, bf16),
            hshape(NSA_KV_HEADS, HEAD_DIM, bf16), hshape(NSA_KV_HEADS, HEAD_DIM, bf16),
            hshape(NSA_KV_HEADS, HEAD_DIM + SEL_LANES, bf16), tshape(NSA_KV_HEADS),
            hshape(NSA_KV_HEADS, HEAD_DIM, bf16),
            jax.ShapeDtypeStruct((n_batch, NSA_KV_HEADS, seq_len // WIN_TILE, HEAD_DIM, WIN_TILE), bf16),
            jax.ShapeDtypeStruct((n_batch, seq_len, LANES), f32),
        ],
        grid=(n_batch, per_b),
        in_specs=[
            pl.BlockSpec((tm, COL_SMALL), lambda b, i: (b * per_b + i, 0)),
            pl.BlockSpec((tm, LANES), lambda b, i: (b * per_b + i, 0)),
            vec, vec, vec, vec, vec, vec, vec,
        ],
        out_specs=[
            hspec(FOX_HEADS, 2 * HEAD_DIM), hspec(FOX_HEADS, 2 * HEAD_DIM), tspec(FOX_HEADS),
            pl.BlockSpec((1, 1, 1, LANES), lambda b, i: (b, i, 0, 0)),
            hspec(NSA_HEADS, HEAD_DIM), hspec(NSA_HEADS, HEAD_DIM),
            hspec(NSA_KV_HEADS, HEAD_DIM), hspec(NSA_KV_HEADS, HEAD_DIM),
            hspec(NSA_KV_HEADS, HEAD_DIM + SEL_LANES), tspec(NSA_KV_HEADS),
            hspec(NSA_KV_HEADS, HEAD_DIM),
            pl.BlockSpec((1, NSA_KV_HEADS, tm // WIN_TILE, HEAD_DIM, WIN_TILE), lambda b, i: (b, 0, i, 0, 0)),
            pl.BlockSpec((1, tm, LANES), lambda b, i: (b, i, 0)),
        ],
        scratch_shapes=[pltpu.VMEM((8, LANES), f32), pltpu.VMEM((2, tm, LANES), f32)],
        compiler_params=_params("arbitrary", "arbitrary"),
        name="prep",
    )(proj, small, gq, gk, gn, gs, gw, b_forget_row, inv_freq)


def _compress_kernel(x_ref, pe_ref, w1_ref, w2_ref, g_ref, o_ref, *, n_rows, do_norm):
    half = CMP_STRIDE * HEAD_DIM
    x = x_ref[0, 0]
    pe = pe_ref[...]
    xa = (x + pe[:, :half]).astype(bf16)
    xb = (x + pe[:, half:]).astype(bf16)
    a = _dot(xa, w1_ref[0:half, :])
    b = _dot(xb, w1_ref[half:2 * half, :])
    pre = a + pltpu.roll(b, n_rows - 1, 0)
    hid = pre * jax.nn.sigmoid(pre)
    out = _dot(hid.astype(bf16), w2_ref[...])
    if do_norm:
        out = _rms(out, g_ref[...])
    o_ref[0, 0] = out.astype(bf16)


def _compress(x4, pe, w1_bf16, w2_bf16, gain, do_norm):
    n_batch, n_g, seq_len, _ = x4.shape
    n_rows = seq_len // CMP_STRIDE
    half = CMP_STRIDE * HEAD_DIM
    xr = x4.reshape(n_batch, n_g, n_rows, half)
    return pl.pallas_call(
        functools.partial(_compress_kernel, n_rows=n_rows, do_norm=do_norm),
        out_shape=jax.ShapeDtypeStruct((n_batch, n_g, n_rows, HEAD_DIM), bf16),
        grid=(n_batch, n_g),
        in_specs=[
            pl.BlockSpec((1, 1, n_rows, half), lambda b, g: (b, g, 0, 0)),
            pl.BlockSpec((1, 2 * half), lambda b, g: (0, 0)),
            pl.BlockSpec((2 * half, CMP_HIDDEN), lambda b, g: (0, 0)),
            pl.BlockSpec((CMP_HIDDEN, HEAD_DIM), lambda b, g: (0, 0)),
            pl.BlockSpec((1, HEAD_DIM), lambda b, g: (0, 0)),
        ],
        out_specs=pl.BlockSpec((1, 1, n_rows, HEAD_DIM), lambda b, g: (b, g, 0, 0)),
        compiler_params=_params("arbitrary", "arbitrary"),
        name="compress",
    )(xr, pe.reshape(1, 2 * half), w1_bf16, w2_bf16, gain)


def _softmax_block(s, d, m_prev, l_prev):
    m_blk = jnp.max(s, axis=0, keepdims=True)
    m_new = jnp.maximum(m_prev, m_blk if d is None else m_blk + d)
    p = jnp.exp2(s - (m_new if d is None else m_new - d))
    alpha = jnp.exp2(m_prev - m_new)
    l_new = alpha * l_prev + jnp.sum(p, axis=0, keepdims=True)
    return m_new, l_new, alpha, p.astype(bf16)


class _Chain(NamedTuple):
    scores: Callable
    values: Callable
    offset: Callable
    diag_mask: Callable
    emit: Callable
    s_buf: Any
    p_buf: Any


def _flash_sweep(n_full, chains, n_q):
    for c in chains:
        c.s_buf[0] = c.scores(0)
        c.p_buf[1] = jnp.zeros(c.p_buf.shape[1:], c.p_buf.dtype)

    def accumulate(c, j, slot, alpha, acc):
        return alpha * acc + _dot(c.values(jnp.maximum(j, 0)), c.p_buf[slot])

    def step(j, cur, states):
        out = []
        for c, (m, l, acc, alpha_prev) in zip(chains, states):
            acc = accumulate(c, j - 1, 1 - cur, alpha_prev, acc)
            m, l, alpha, p = _softmax_block(c.s_buf[cur], c.offset(j), m, l)
            c.p_buf[cur] = p
            c.s_buf[1 - cur] = c.scores(j + 1)
            out.append((m, l, acc, alpha))
        return tuple(out)

    def finish(cur, states):
        for c, (m, l, acc, alpha_prev) in zip(chains, states):
            acc = accumulate(c, n_full - 1, 1 - cur, alpha_prev, acc)
            m, l, alpha, p = _softmax_block(c.diag_mask(c.s_buf[cur]), None, m, l)
            acc = alpha * acc + _dot(c.values(n_full), p)
            c.emit(acc / l)

    init = tuple((jnp.full((1, n_q), -jnp.inf, f32), jnp.zeros((1, n_q), f32),
                  jnp.zeros((HEAD_DIM, n_q), f32), jnp.ones((1, n_q), f32)) for _ in chains)
    states = lax.fori_loop(0, n_full // 2, lambda jj, st: step(2 * jj + 1, 1, step(2 * jj, 0, st)), init)

    @pl.when(n_full % 2 == 0)
    def _():
        finish(0, states)

    @pl.when(n_full % 2 == 1)
    def _():
        finish(1, step(n_full - 1, 0, states))


def _fox_kernel(q_ref, k_ref, vt_ref, c_ref, o_ref, *bufs, tq, n_heads):
    hb = pl.program_id(1)
    i = pl.program_id(2)
    lane = lax.broadcasted_iota(jnp.int32, (1, LANES), 1)

    def diag_mask(s):
        kk = lax.broadcasted_iota(jnp.int32, (tq, tq), 0)
        qq = lax.broadcasted_iota(jnp.int32, (tq, tq), 1)
        return jnp.where(kk <= qq, s, -jnp.inf)

    def chain(u):
        qa = q_ref[0, u]

        def cin(j):
            return jnp.sum(jnp.where(lane == hb * n_heads + u, c_ref[0, j], 0.0), axis=-1, keepdims=True)

        ci = cin(i)

        def scores(j):
            return _dot_nt(k_ref[0, u, pl.ds(pl.multiple_of(j * tq, tq), tq), :], qa)

        def emit(out_t):
            o_ref[0, :, u * HEAD_DIM:(u + 1) * HEAD_DIM] = out_t.T.astype(bf16)

        return _Chain(scores, lambda j: vt_ref[0, u, j], lambda j: (ci - cin(j)) * LOG2E, diag_mask, emit,
                      bufs[2 * u], bufs[2 * u + 1])

    _flash_sweep(i, [chain(u) for u in range(n_heads)], tq)


def _fox(fq, fk, fvt, cin):
    n_batch, n_h, seq_len, _ = fq.shape
    tq = KV_BLOCK
    nkb = seq_len // tq
    n_heads = 4
    return pl.pallas_call(
        functools.partial(_fox_kernel, tq=tq, n_heads=n_heads),
        out_shape=jax.ShapeDtypeStruct((n_batch, seq_len, n_h * HEAD_DIM), bf16),
        grid=(n_batch, n_h // n_heads, nkb),
        in_specs=[
            pl.BlockSpec((1, n_heads, tq, 2 * HEAD_DIM), lambda b, h, i: (b, h, i, 0)),
            pl.BlockSpec((1, n_heads, seq_len, 2 * HEAD_DIM), lambda b, h, i: (b, h, 0, 0)),
            pl.BlockSpec((1, n_heads, nkb, HEAD_DIM, tq), lambda b, h, i: (b, h, 0, 0, 0)),
            pl.BlockSpec((1, nkb, 1, LANES), lambda b, h, i: (b, 0, 0, 0)),
        ],
        out_specs=pl.BlockSpec((1, tq, n_heads * HEAD_DIM), lambda b, h, i: (b, i, h)),
        scratch_shapes=[pltpu.VMEM((2, tq, tq), f32), pltpu.VMEM((2, tq, tq), bf16)] * n_heads,
        compiler_params=_params("arbitrary", "arbitrary", "arbitrary"),
        name="fox",
    )(fq, fk, fvt, cin)


def _cmp_kernel(q_ref, k_ref, v_ref, gate_ref, ov_ref, o_ref, sel_ref, *, tq, n_rows, n_cmp, n_sel):
    i = pl.program_id(2)
    g = pl.program_id(1)
    hg = NSA_GROUP
    q = q_ref[0].reshape(hg * tq, HEAD_DIM)
    s = _dot_nt(q, k_ref[0, 0]).reshape(hg, tq, n_rows)
    t = i * tq + lax.broadcasted_iota(jnp.int32, (1, tq, 1), 1)
    n = lax.broadcasted_iota(jnp.int32, (1, 1, n_rows), 2)
    valid = (n * CMP_STRIDE + (CMP_LEN - 1) <= t) & (n < n_cmp)
    s = jnp.where(valid, s, -jnp.inf)
    m = jnp.max(s, axis=-1, keepdims=True)
    m = jnp.where(m == -jnp.inf, 0.0, m)
    e = jnp.exp(s - m)
    p = e / jnp.maximum(jnp.sum(e, axis=-1, keepdims=True), 1e-30)
    o = _dot(p.reshape(hg * tq, n_rows).astype(bf16), v_ref[0, 0])
    _write_gated(o_ref, gate_ref, o, g, 0, tq)

    lane = lax.broadcasted_iota(jnp.int32, (tq, LANES), 1)
    psum = p[0] + p[1] + p[2] + p[3]
    imp = _dot_f32_by_exact(psum, ov_ref[...])
    tq_pos = i * tq + lax.broadcasted_iota(jnp.int32, (tq, 1), 0)
    q_blk = lax.shift_right_logical(tq_pos, SEL_SHIFT)
    causal = lane <= q_blk
    forced = (lane == 0) | (lane == q_blk) | (lane == q_blk - 1)
    key = jnp.where(forced, jnp.inf, jnp.where(causal, imp, -1.0))
    key = jnp.where(lane < n_sel, key, -3.0)
    lane_f = lane.astype(f32)
    sel = jnp.zeros((tq, LANES), jnp.bool_)
    for _ in range(min(SEL_TOPK, n_sel)):
        mx = jnp.max(key, axis=-1, keepdims=True)
        first = jnp.min(jnp.where(key == mx, lane_f, float(LANES)), axis=-1, keepdims=True)
        pick = lane_f == first
        sel = sel | pick
        key = jnp.where(pick, -2.0, key)
    sel_ref[0, 0] = jnp.where(sel & causal, 0.0, -MASK_BIG).astype(bf16)


def _cmp(nqu, kcmp, vcmp, gates, n_cmp):
    n_batch, _, seq_len, _ = nqu.shape
    n_rows = kcmp.shape[2]
    n_sel = seq_len // SEL_BLOCK
    assert n_sel <= SEL_LANES
    tq = 1024
    c0 = np.arange(n_rows) * CMP_STRIDE
    s0 = np.arange(SEL_LANES) * SEL_BLOCK
    overlap = np.clip(np.minimum(c0[:, None] + CMP_LEN, s0[None, :] + SEL_BLOCK)
                      - np.maximum(c0[:, None], s0[None, :]), 0, None).astype(np.float32) / CMP_LEN
    overlap[n_cmp:, :] = 0.0
    overlap[:, n_sel:] = 0.0
    return pl.pallas_call(
        functools.partial(_cmp_kernel, tq=tq, n_rows=n_rows, n_cmp=n_cmp, n_sel=n_sel),
        out_shape=[
            jax.ShapeDtypeStruct((n_batch, seq_len, NSA_W), bf16),
            jax.ShapeDtypeStruct((n_batch, NSA_KV_HEADS, seq_len, SEL_LANES), bf16),
        ],
        grid=(n_batch, NSA_KV_HEADS, seq_len // tq),
        in_specs=[
            pl.BlockSpec((1, NSA_GROUP, tq, HEAD_DIM), lambda b, g, i: (b, g, i, 0)),
            pl.BlockSpec((1, 1, n_rows, HEAD_DIM), lambda b, g, i: (b, g, 0, 0)),
            pl.BlockSpec((1, 1, n_rows, HEAD_DIM), lambda b, g, i: (b, g, 0, 0)),
            pl.BlockSpec((1, tq, LANES), lambda b, g, i: (b, i, 0)),
            pl.BlockSpec((n_rows, SEL_LANES), lambda b, g, i: (0, 0)),
        ],
        out_specs=[
            pl.BlockSpec((1, tq, NSA_GROUP * HEAD_DIM), lambda b, g, i: (b, i, g)),
            pl.BlockSpec((1, 1, tq, SEL_LANES), lambda b, g, i: (b, g, i, 0)),
        ],
        compiler_params=_params("arbitrary", "arbitrary", "arbitrary"),
        name="cmp",
    )(nqu, kcmp, vcmp, gates, jnp.asarray(overlap, dtype=bf16))


def _write_gated(o_ref, gate_ref, out, g, branch, tq):
    gates = gate_ref[0]
    lane = lax.broadcasted_iota(jnp.int32, (tq, LANES), 1)
    for h in range(NSA_GROUP):
        gcol = FOX_HEADS + (g * NSA_GROUP + h) * N_BRANCH + branch
        gh = jnp.sum(jnp.where(lane == gcol, gates, 0.0), axis=-1, keepdims=True)
        o_ref[0, :, h * HEAD_DIM:(h + 1) * HEAD_DIM] = (gh * out[h * tq:(h + 1) * tq]).astype(bf16)


def _slc_kernel(q_ref, sel_ref, k_ref, vt_ref, gate_ref, *rest, tq, tk, n_cast_w):
    o_ref = rest[n_cast_w]
    bufs = rest[2 * n_cast_w + 1:]
    for w32_ref, w16_ref in zip(rest[:n_cast_w], rest[n_cast_w + 1:2 * n_cast_w + 1]):
        w16_ref[...] = w32_ref[...].astype(bf16)
    i = pl.program_id(1)
    hg = NSA_GROUP
    n_q = hg * tq
    q0 = i * tq
    jd = q0 // tk

    def diag_mask(s):
        kp = jd * tk + lax.broadcasted_iota(jnp.int32, (tk, n_q), 0)
        t = q0 + (lax.broadcasted_iota(jnp.int32, (tk, n_q), 1) & (tq - 1))
        return jnp.where(kp <= t, s, -jnp.inf)

    def chain(g):
        q = q_ref[0, g * hg:(g + 1) * hg].reshape(n_q, HEAD_DIM)
        qa = jnp.concatenate([q, jnp.concatenate([sel_ref[0, g]] * hg, axis=0)], axis=1)

        def scores(j):
            return _dot_nt(k_ref[0, g, pl.ds(pl.multiple_of(j * tk, tk), tk), :], qa)

        def emit(out):
            gates = gate_ref[0]
            lane = lax.broadcasted_iota(jnp.int32, (tq, LANES), 1)
            for h in range(hg):
                head = g * hg + h
                gh = jnp.sum(jnp.where(lane == FOX_HEADS + head * N_BRANCH + 1, gates, 0.0),
                             axis=-1, keepdims=True)
                o_ref[0, :, head * HEAD_DIM:(head + 1) * HEAD_DIM] = (
                    gh * out[:, h * tq:(h + 1) * tq].T).astype(bf16)

        return _Chain(scores, lambda j: vt_ref[0, g, j], lambda j: None, diag_mask, emit,
                      bufs[2 * g], bufs[2 * g + 1])

    _flash_sweep(jd, [chain(g) for g in range(NSA_KV_HEADS)], n_q)


def _slc(nqr, selneg, kaug, vst, gates, cast_weights):
    n_batch, _, seq_len, _ = nqr.shape
    tq, tk = 256, KV_BLOCK
    nkb = seq_len // tk
    assert tq & (tq - 1) == 0
    n_q = NSA_GROUP * tq
    n_i = seq_len // tq
    cast_in, cast_out, cast_shapes = _cast_side_job(cast_weights, n_batch * n_i, lambda b, i: b * n_i + i)
    return pl.pallas_call(
        functools.partial(_slc_kernel, tq=tq, tk=tk, n_cast_w=len(cast_weights)),
        out_shape=[jax.ShapeDtypeStruct((n_batch, seq_len, NSA_W), bf16)] + cast_shapes,
        grid=(n_batch, n_i),
        in_specs=[
            pl.BlockSpec((1, NSA_HEADS, tq, HEAD_DIM), lambda b, i: (b, 0, i, 0)),
            pl.BlockSpec((1, NSA_KV_HEADS, tq, SEL_LANES), lambda b, i: (b, 0, i, 0)),
            pl.BlockSpec((1, NSA_KV_HEADS, seq_len, HEAD_DIM + SEL_LANES), lambda b, i: (b, 0, 0, 0)),
            pl.BlockSpec((1, NSA_KV_HEADS, nkb, HEAD_DIM, tk), lambda b, i: (b, 0, 0, 0, 0)),
            pl.BlockSpec((1, tq, LANES), lambda b, i: (b, i, 0)),
        ] + cast_in,
        out_specs=[pl.BlockSpec((1, tq, NSA_W), lambda b, i: (b, i, 0))] + cast_out,
        scratch_shapes=[pltpu.VMEM((2, tk, n_q), f32), pltpu.VMEM((2, tk, n_q), bf16)] * NSA_KV_HEADS,
        compiler_params=_params("arbitrary", "arbitrary"),
        name="slc",
    )(nqr, selneg, kaug, vst, gates, *cast_weights)


def _win_kernel(q_ref, k_ref, vt_ref, gate_ref, o_ref, *, tq, n_sub):
    i = pl.program_id(2)
    g = pl.program_id(1)
    hg = NSA_GROUP
    span = WINDOW + tq
    q0s = [(i * n_sub + u) * tq for u in range(n_sub)]
    k0s = [pl.multiple_of(jnp.maximum(q0 - WINDOW, 0), tq) for q0 in q0s]
    scores = []
    for u in range(n_sub):
        q = q_ref[0, :, u * tq:(u + 1) * tq, :].reshape(hg * tq, HEAD_DIM)
        scores.append(_dot_nt(k_ref[0, 0, pl.ds(k0s[u], span), :], q))
    probs, denoms = [], []
    for u in range(n_sub):
        kp = k0s[u] + lax.broadcasted_iota(jnp.int32, (span, tq), 0)
        t = q0s[u] + lax.broadcasted_iota(jnp.int32, (span, tq), 1)
        diff = t - kp
        bias = jnp.where((diff >= 0) & (diff < WINDOW), 0.0, -jnp.inf)
        s = jnp.concatenate([scores[u][:, h * tq:(h + 1) * tq] + bias for h in range(hg)], axis=1)
        e = jnp.exp2(s - jnp.max(s, axis=0, keepdims=True))
        denoms.append(jnp.sum(e, axis=0, keepdims=True))
        probs.append(e.astype(bf16))
    gates = gate_ref[0]
    lane = lax.broadcasted_iota(jnp.int32, (tq, LANES), 1)
    for u in range(n_sub):
        jb = k0s[u] // tq
        vt = jnp.concatenate([vt_ref[0, 0, jb + c] for c in range(span // tq)], axis=1)
        out = _dot(vt, probs[u]) / denoms[u]
        for h in range(hg):
            gcol = FOX_HEADS + (g * hg + h) * N_BRANCH + 2
            gh = jnp.sum(jnp.where(lane == gcol, gates[u * tq:(u + 1) * tq], 0.0), axis=-1, keepdims=True)
            o_ref[0, u * tq:(u + 1) * tq, h * HEAD_DIM:(h + 1) * HEAD_DIM] = (
                gh * out[:, h * tq:(h + 1) * tq].T).astype(bf16)


def _win(nqr, kw, vwt, gates):
    n_batch, _, seq_len, _ = nqr.shape
    tq, n_sub = WIN_TILE, 4
    assert seq_len >= WINDOW + tq and WINDOW % tq == 0
    return pl.pallas_call(
        functools.partial(_win_kernel, tq=tq, n_sub=n_sub),
        out_shape=jax.ShapeDtypeStruct((n_batch, seq_len, NSA_W), bf16),
        grid=(n_batch, NSA_KV_HEADS, seq_len // (tq * n_sub)),
        in_specs=[
            pl.BlockSpec((1, NSA_GROUP, tq * n_sub, HEAD_DIM), lambda b, g, i: (b, g, i, 0)),
            pl.BlockSpec((1, 1, seq_len, HEAD_DIM), lambda b, g, i: (b, g, 0, 0)),
            pl.BlockSpec((1, 1, seq_len // tq, HEAD_DIM, tq), lambda b, g, i: (b, g, 0, 0, 0)),
            pl.BlockSpec((1, tq * n_sub, LANES), lambda b, g, i: (b, i, 0)),
        ],
        out_specs=pl.BlockSpec((1, tq * n_sub, NSA_GROUP * HEAD_DIM), lambda b, g, i: (b, i, g)),
        compiler_params=_params("arbitrary", "arbitrary", "arbitrary"),
        name="win",
    )(nqr, kw, vwt, gates)


def _outproj_kernel(fox_ref, c_ref, s_ref, w_ref, x_ref, mod_ref, wo_ref, o_ref):
    nsa = c_ref[...].astype(f32) + s_ref[...].astype(f32) + w_ref[...].astype(f32)
    a = jnp.concatenate([fox_ref[...], nsa.astype(bf16)], axis=1)
    o_ref[...] = x_ref[...] + mod_ref[0][2:3] * _dot(a, wo_ref[...])


def _outproj(ofox, ocmp, oslc, owin, x2d, mod3, wo_bf16, seq_len):
    m_rows = x2d.shape[0]
    tm = 512
    per_b = seq_len // tm
    half = pl.BlockSpec((tm, FOX_W), lambda i: (i, 0))
    return pl.pallas_call(
        _outproj_kernel,
        out_shape=jax.ShapeDtypeStruct((m_rows, D_MODEL), f32),
        grid=(m_rows // tm,),
        in_specs=[
            half, half, half, half,
            pl.BlockSpec((tm, D_MODEL), lambda i: (i, 0)),
            pl.BlockSpec((1, 6, D_MODEL), lambda i: (i // per_b, 0, 0)),
            pl.BlockSpec((D_MODEL, D_MODEL), lambda i: (0, 0)),
        ],
        out_specs=pl.BlockSpec((tm, D_MODEL), lambda i: (i, 0)),
        compiler_params=_params("arbitrary"),
        name="outproj",
    )(ofox, ocmp, oslc, owin, x2d, mod3, wo_bf16)


def _mlp_kernel(x_ref, mod_ref, g_ref, wu_ref, wd_ref, o_ref, h_ref, acc_ref):
    f = pl.program_id(1)

    @pl.when(f == 0)
    def _():
        md = mod_ref[0]
        y = _rms(x_ref[...], g_ref[...])
        h_ref[...] = (y * (1.0 + md[4:5]) + md[3:4]).astype(bf16)
        acc_ref[...] = jnp.zeros_like(acc_ref)

    u = jnp.maximum(_dot(h_ref[...], wu_ref[...]), 0.0)
    acc_ref[...] += _dot((u * u).astype(bf16), wd_ref[...])

    @pl.when(f == pl.num_programs(1) - 1)
    def _():
        o_ref[...] = x_ref[...] + mod_ref[0][5:6] * acc_ref[...]


def _mlp(x2d, mod3, norm_g, wu_bf16, wd_bf16, seq_len):
    m_rows = x2d.shape[0]
    tm, tf = 512, 1024
    per_b = seq_len // tm
    return pl.pallas_call(
        _mlp_kernel,
        out_shape=jax.ShapeDtypeStruct((m_rows, D_MODEL), f32),
        grid=(m_rows // tm, D_FF // tf),
        in_specs=[
            pl.BlockSpec((tm, D_MODEL), lambda i, f: (i, 0)),
            pl.BlockSpec((1, 6, D_MODEL), lambda i, f: (i // per_b, 0, 0)),
            pl.BlockSpec((1, D_MODEL), lambda i, f: (0, 0)),
            pl.BlockSpec((D_MODEL, tf), lambda i, f: (0, f)),
            pl.BlockSpec((tf, D_MODEL), lambda i, f: (f, 0)),
        ],
        out_specs=pl.BlockSpec((tm, D_MODEL), lambda i, f: (i, 0)),
        scratch_shapes=[pltpu.VMEM((tm, D_MODEL), bf16), pltpu.VMEM((tm, D_MODEL), f32)],
        compiler_params=_params("arbitrary", "arbitrary"),
        name="mlp",
    )(x2d, mod3, norm_g, wu_bf16, wd_bf16)


def _layer(layer, x, c, w_ada, b_ada, norm1_g, w_in_all, b_forget, fox_q_norm, fox_k_norm, nsa_q_norm,
           cmp_k_norm, slc_k_norm, win_k_norm, cmp_pe_k, cmp_w1_k, cmp_w2_k, cmp_pe_v, cmp_w1_v,
           cmp_w2_v, w_out, norm2_g, w_up, w_down):
    n_batch, seq_len, _ = x.shape
    n_cmp = (seq_len - CMP_LEN) // CMP_STRIDE + 1
    row = lambda v: v.reshape(1, -1)

    half = HEAD_DIM // 2
    inv_freq = ROPE_THETA ** (-jnp.arange(half, dtype=f32) / half)
    inv_freq = jnp.concatenate([inv_freq, inv_freq]).reshape(1, HEAD_DIM)

    mod, w_a, w_b, w_s = _ada(c, w_ada, b_ada, w_in_all, layer)
    mod3 = mod.reshape(n_batch, 6, D_MODEL)
    x2d = x.reshape(n_batch * seq_len, D_MODEL)
    proj, small = _proj(x2d, mod3, row(norm1_g), w_a, w_b, w_s, seq_len)
    (fq, fk, fvt, cin, nqu, nqr, kc, vc, kaug, vst, kw, vw, gates) = _prep(
        proj, small, n_batch, seq_len, row(fox_q_norm), row(fox_k_norm), row(nsa_q_norm), row(slc_k_norm),
        row(win_k_norm), jnp.pad(b_forget, (0, LANES - FOX_HEADS)).reshape(1, LANES), inv_freq)
    kcmp = _compress(kc, cmp_pe_k, cmp_w1_k.astype(bf16), cmp_w2_k.astype(bf16), row(cmp_k_norm), True)
    vcmp = _compress(vc, cmp_pe_v, cmp_w1_v.astype(bf16), cmp_w2_v.astype(bf16), row(cmp_k_norm), False)
    ofox = _fox(fq, fk, fvt, cin)
    ocmp, selneg = _cmp(nqu, kcmp, vcmp, gates, n_cmp)
    oslc, w_up16, w_down16, w_out16 = _slc(nqr, selneg, kaug, vst, gates, [w_up, w_down, w_out])
    owin = _win(nqr, kw, vw, gates)
    x1 = _outproj(ofox.reshape(-1, FOX_W), ocmp.reshape(-1, NSA_W), oslc.reshape(-1, NSA_W),
                  owin.reshape(-1, NSA_W), x2d, mod3, w_out16, seq_len)
    x2 = _mlp(x1, mod3, row(norm2_g), w_up16, w_down16, seq_len)
    return x2.reshape(n_batch, seq_len, D_MODEL)


def kernel(x, c, w_ada, b_ada, norm1_g, w_in, b_forget, fox_q_norm, fox_k_norm, nsa_q_norm, cmp_k_norm,
           slc_k_norm, win_k_norm, cmp_pe_k, cmp_w1_k, cmp_w2_k, cmp_pe_v, cmp_w1_v, cmp_w2_v, w_out,
           norm2_g, w_up, w_down):
    depth = w_ada.shape[0]
    for l in range(depth):
        x = _layer(l, x, c, w_ada[l], b_ada[l], norm1_g[l], w_in, b_forget[l], fox_q_norm[l], fox_k_norm[l],
                   nsa_q_norm[l], cmp_k_norm[l], slc_k_norm[l], win_k_norm[l], cmp_pe_k[l], cmp_w1_k[l],
                   cmp_w2_k[l], cmp_pe_v[l], cmp_w1_v[l], cmp_w2_v[l], w_out[l], norm2_g[l], w_up[l],
                   w_down[l])
    return x
```

```python
import functools
import math
from typing import Any, Callable, NamedTuple

import numpy as np
import jax
import jax.numpy as jnp
from jax import lax
from jax.experimental import pallas as pl
from jax.experimental.pallas import tpu as pltpu

D_MODEL = 2048
HEAD_DIM = 128
FOX_HEADS = 8
NSA_HEADS = 8
NSA_KV_HEADS = 2
NSA_GROUP = NSA_HEADS // NSA_KV_HEADS
N_BRANCH = 3
D_FF = 4 * D_MODEL
ROPE_THETA = 10000.0
CMP_LEN = 32
CMP_STRIDE = 16
CMP_SHIFT = 4
CMP_HIDDEN = 2 * HEAD_DIM
SEL_BLOCK = 64
SEL_SHIFT = 6
SEL_TOPK = 16
WINDOW = 512
NORM_EPS = 1e-6
ATTN_SCALE = HEAD_DIM ** -0.5
FOX_W = FOX_HEADS * HEAD_DIM
NSA_W = NSA_HEADS * HEAD_DIM
KV_W = NSA_KV_HEADS * HEAD_DIM

LANES = 128
SEL_LANES = LANES
MASK_BIG = 1e30
KV_BLOCK = 512
WIN_TILE = 128
LOG2E = math.log2(math.e)

COL_FQ = 0
COL_FK = COL_FQ + FOX_W
COL_FV = COL_FK + FOX_W
COL_NQ = COL_FV + FOX_W
COL_KC = COL_NQ + NSA_W
COL_VC = COL_KC + KV_W
COL_KS = COL_VC + KV_W
COL_VS = COL_KS + KV_W
COL_KW = COL_VS + KV_W
COL_VW = COL_KW + KV_W
COL_SMALL = COL_VW + KV_W
W_IN_Z0 = 3 * FOX_W
W_IN_NQ0 = W_IN_Z0 + FOX_HEADS
W_IN_GZ0 = W_IN_NQ0 + NSA_W + 6 * KV_W

VMEM_LIMIT = 56 * 1024 * 1024

f32 = jnp.float32
bf16 = jnp.bfloat16


def _params(*sem):
    return pltpu.CompilerParams(dimension_semantics=sem, vmem_limit_bytes=VMEM_LIMIT)


def _dot_nt(a, b):
    return lax.dot_general(a, b, (((1,), (1,)), ((), ())), preferred_element_type=f32)


def _dot(a, b):
    return jnp.dot(a, b, preferred_element_type=f32)


def _split3(x):
    hi = x.astype(bf16)
    r1 = x - hi.astype(f32)
    mid = r1.astype(bf16)
    lo = (r1 - mid.astype(f32)).astype(bf16)
    return hi, mid, lo


def _dot_f32_by_exact(x, w_bf16):
    hi, mid, lo = _split3(x)
    return _dot(hi, w_bf16) + (_dot(mid, w_bf16) + _dot(lo, w_bf16))


def _rms(x, gain):
    ms = jnp.mean(x * x, axis=-1, keepdims=True)
    return x * lax.rsqrt(ms + NORM_EPS) * gain


def _ada_kernel(ct_ref, w_ref, b_ref, win_ref, o_ref, wa_ref, wb_ref, ws_ref, *, n_batch, k_chunk):
    ct = ct_ref[...]
    act = ct * jax.nn.sigmoid(ct)
    rows = []
    for b in range(n_batch):
        col = act[:, b:b + 1]
        acc = b_ref[...]
        for k0 in range(0, D_MODEL, k_chunk):
            acc = acc + jnp.sum(w_ref[k0:k0 + k_chunk, :] * col[k0:k0 + k_chunk], axis=0, keepdims=True)
        rows.append(acc)
    o_ref[...] = jnp.concatenate(rows, axis=0)

    x = win_ref[0]
    wa_ref[...] = x[:, :W_IN_Z0].astype(bf16)
    wb_ref[...] = x[:, W_IN_NQ0:W_IN_GZ0].astype(bf16)
    pad = jnp.zeros((x.shape[0], LANES - (W_IN_NQ0 - W_IN_Z0) - (x.shape[1] - W_IN_GZ0)), f32)
    ws_ref[...] = jnp.concatenate([x[:, W_IN_Z0:W_IN_NQ0], x[:, W_IN_GZ0:], pad], axis=1).astype(bf16)


def _ada(c, w_ada, b_ada, w_in_all, layer):
    n_batch = c.shape[0]
    n_out = w_ada.shape[1]
    tn = 768
    n_steps = n_out // tn
    assert n_out % tn == 0 and D_MODEL % n_steps == 0
    rows = D_MODEL // n_steps
    d_in = w_in_all.shape[2]
    return pl.pallas_call(
        functools.partial(_ada_kernel, n_batch=n_batch, k_chunk=256),
        out_shape=[jax.ShapeDtypeStruct((n_batch, n_out), f32),
                   jax.ShapeDtypeStruct((D_MODEL, W_IN_Z0), bf16),
                   jax.ShapeDtypeStruct((D_MODEL, W_IN_GZ0 - W_IN_NQ0), bf16),
                   jax.ShapeDtypeStruct((D_MODEL, LANES), bf16)],
        grid=(n_steps,),
        in_specs=[
            pl.BlockSpec((D_MODEL, n_batch), lambda j: (0, 0)),
            pl.BlockSpec((D_MODEL, tn), lambda j: (0, j)),
            pl.BlockSpec((1, tn), lambda j: (0, j)),
            pl.BlockSpec((1, rows, d_in), lambda j: (layer, j, 0)),
        ],
        out_specs=[pl.BlockSpec((n_batch, tn), lambda j: (0, j)),
                   pl.BlockSpec((rows, W_IN_Z0), lambda j: (j, 0)),
                   pl.BlockSpec((rows, W_IN_GZ0 - W_IN_NQ0), lambda j: (j, 0)),
                   pl.BlockSpec((rows, LANES), lambda j: (j, 0))],
        compiler_params=_params("arbitrary"),
        name="ada",
    )(c.T, w_ada, b_ada.reshape(1, n_out), w_in_all)


def _proj_kernel(x_ref, mod_ref, g_ref, wa_ref, wb_ref, ws_ref, o_ref, small_ref):
    md = mod_ref[0]
    h = (_rms(x_ref[...], g_ref[...]) * (1.0 + md[1:2]) + md[0:1]).astype(bf16)
    n_a = wa_ref.shape[1]
    o_ref[:, :n_a] = _dot(h, wa_ref[...]).astype(bf16)
    o_ref[:, n_a:] = _dot(h, wb_ref[...]).astype(bf16)
    small_ref[...] = _dot(h, ws_ref[...])


def _cast_side_job(weights, n_steps, step_of):
    n_cast = max(c for c in range(1, n_steps + 1) if all(w.shape[0] % (16 * c) == 0 for w in weights))
    idx = lambda *ids: (jnp.minimum(step_of(*ids), n_cast - 1), 0)
    specs = [pl.BlockSpec((w.shape[0] // n_cast, w.shape[1]), idx) for w in weights]
    return specs, specs, [jax.ShapeDtypeStruct(w.shape, bf16) for w in weights]


def _proj(x2d, mod3, norm_g, wa, wb, ws, seq_len):
    m_rows = x2d.shape[0]
    tm = 512
    assert seq_len % tm == 0
    per_b = seq_len // tm
    n_main = wa.shape[1] + wb.shape[1]
    resident = lambda w: pl.BlockSpec(w.shape, lambda i: (0, 0), pipeline_mode=pl.Buffered(1))
    return pl.pallas_call(
        _proj_kernel,
        out_shape=[jax.ShapeDtypeStruct((m_rows, n_main), bf16), jax.ShapeDtypeStruct((m_rows, LANES), f32)],
        grid=(m_rows // tm,),
        in_specs=[
            pl.BlockSpec((tm, D_MODEL), lambda i: (i, 0)),
            pl.BlockSpec((1, 6, D_MODEL), lambda i: (i // per_b, 0, 0)),
            pl.BlockSpec((1, D_MODEL), lambda i: (0, 0)),
            resident(wa), resident(wb), resident(ws),
        ],
        out_specs=[pl.BlockSpec((tm, n_main), lambda i: (i, 0)), pl.BlockSpec((tm, LANES), lambda i: (i, 0))],
        compiler_params=_params("arbitrary"),
        name="proj",
    )(x2d, mod3, norm_g, wa, wb, ws)


def _prep_kernel(p_ref, small_ref, gq_ref, gk_ref, gn_ref, gs_ref, gw_ref, bf_ref, inv_ref,
                 fq_ref, fk_ref, fvt_ref, cin_ref, nqu_ref, nqr_ref, kc_ref, vc_ref,
                 ks_ref, vst_ref, kw_ref, vw_ref, gate_ref, carry_ref, rot_ref, *, tm):
    i = pl.program_id(1)

    @pl.when(i == 0)
    def _():
        carry_ref[...] = jnp.zeros_like(carry_ref)

    def head(col, h):
        return p_ref[:, col + h * HEAD_DIM: col + (h + 1) * HEAD_DIM].astype(f32)

    row = lax.broadcasted_iota(jnp.int32, (tm, LANES), 0)
    lane = lax.broadcasted_iota(jnp.int32, (tm, LANES), 1)
    pos = i * tm + row

    @pl.when((pl.program_id(0) == 0) & (i == 0))
    def _():
        ang_row = row.astype(f32) * inv_ref[...]
        rot_ref[0] = jnp.cos(ang_row)
        rot_ref[1] = jnp.sin(ang_row)

    ang0 = (i * tm).astype(f32) * inv_ref[...]
    cos0, sin0 = jnp.cos(ang0), jnp.sin(ang0)
    cos = cos0 * rot_ref[0] - sin0 * rot_ref[1]
    sin = sin0 * rot_ref[0] + cos0 * rot_ref[1]
    sin_signed = jnp.where(lane < HEAD_DIM // 2, -sin, sin)

    def rope(x):
        return x * cos + pltpu.roll(x, HEAD_DIM // 2, 1) * sin_signed

    small = small_ref[...]
    z = small + bf_ref[...]
    logf = jnp.minimum(z, 0.0) - jnp.log1p(jnp.exp(-jnp.abs(z)))
    t_idx = lax.broadcasted_iota(jnp.int32, (tm, tm), 0)
    s_idx = lax.broadcasted_iota(jnp.int32, (tm, tm), 1)
    tri = jnp.where(s_idx <= t_idx, 1.0, 0.0).astype(bf16)
    hi, mid, lo = _split3(logf)
    local = _dot(tri, hi) + (_dot(tri, mid) + _dot(tri, lo))
    cin_ref[0, 0] = carry_ref[0:1, :]
    carry_ref[...] = carry_ref[...] + local[tm - 1:tm, :]
    b_hi, b_mid, b_lo = (v.astype(f32) for v in _split3(local * (-LOG2E)))
    ones3 = jnp.where(lane < 3, 1.0, 0.0).astype(bf16)

    for h in range(FOX_HEADS):
        q = _rms(head(COL_FQ, h), gq_ref[...]) * (ATTN_SCALE * LOG2E)
        fq_ref[0, h] = jnp.concatenate([q.astype(bf16), ones3], axis=1)
        k = _rms(head(COL_FK, h), gk_ref[...])
        bias = jnp.where(lane == 0, b_hi[:, h:h + 1],
                         jnp.where(lane == 1, b_mid[:, h:h + 1],
                                   jnp.where(lane == 2, b_lo[:, h:h + 1], 0.0)))
        fk_ref[0, h] = jnp.concatenate([k.astype(bf16), bias.astype(bf16)], axis=1)
        fvt_ref[0, h, 0] = head(COL_FV, h).T.astype(bf16)

    gate_ref[0] = jax.nn.sigmoid(small)

    for h in range(NSA_HEADS):
        qn = _rms(head(COL_NQ, h), gn_ref[...])
        nqu_ref[0, h] = (qn * ATTN_SCALE).astype(bf16)
        nqr_ref[0, h] = rope(qn * (ATTN_SCALE * LOG2E)).astype(bf16)
    onehot = jnp.where(lane == lax.shift_right_logical(pos, SEL_SHIFT), 1.0, 0.0).astype(bf16)
    for g in range(NSA_KV_HEADS):
        kc_ref[0, g] = head(COL_KC, g).astype(bf16)
        vc_ref[0, g] = head(COL_VC, g).astype(bf16)
        ks = rope(_rms(head(COL_KS, g), gs_ref[...])).astype(bf16)
        ks_ref[0, g] = jnp.concatenate([ks, onehot], axis=1)
        vst_ref[0, g, 0] = head(COL_VS, g).T.astype(bf16)
        kw_ref[0, g] = rope(_rms(head(COL_KW, g), gw_ref[...])).astype(bf16)
        vwt = head(COL_VW, g).T.astype(bf16)
        for c in range(tm // WIN_TILE):
            vw_ref[0, g, c] = vwt[:, c * WIN_TILE:(c + 1) * WIN_TILE]


def _prep(proj, small, n_batch, seq_len, gq, gk, gn, gs, gw, b_forget_row, inv_freq):
    tm = KV_BLOCK
    per_b = seq_len // tm
    hshape = lambda n, w, dt: jax.ShapeDtypeStruct((n_batch, n, seq_len, w), dt)
    hspec = lambda n, w: pl.BlockSpec((1, n, tm, w), lambda b, i: (b, 0, i, 0))
    tshape = lambda n: jax.ShapeDtypeStruct((n_batch, n, per_b, HEAD_DIM, tm), bf16)
    tspec = lambda n: pl.BlockSpec((1, n, 1, HEAD_DIM, tm), lambda b, i: (b, 0, i, 0, 0))
    vec = pl.BlockSpec((1, LANES), lambda b, i: (0, 0))
    return pl.pallas_call(
        functools.partial(_prep_kernel, tm=tm),
        out_shape=[
            hshape(FOX_HEADS, 2 * HEAD_DIM, bf16), hshape(FOX_HEADS, 2 * HEAD_DIM, bf16), tshape(FOX_HEADS),
            jax.ShapeDtypeStruct((n_batch, per_b, 1, LANES), f32),
            hshape(NSA_HEADS, HEAD_DIM, bf16), hshape(NSA_HEADS, HEAD_DIM, bf16),
            hshape(NSA_KV_HEADS, HEAD_DIM, bf16), hshape(NSA_KV_HEADS, HEAD_DIM, bf16),
            hshape(NSA_KV_HEADS, HEAD_DIM + SEL_LANES, bf16), tshape(NSA_KV_HEADS),
            hshape(NSA_KV_HEADS, HEAD_DIM, bf16),
            jax.ShapeDtypeStruct((n_batch, NSA_KV_HEADS, seq_len // WIN_TILE, HEAD_DIM, WIN_TILE), bf16),
            jax.ShapeDtypeStruct((n_batch, seq_len, LANES), f32),
        ],
        grid=(n_batch, per_b),
        in_specs=[
            pl.BlockSpec((tm, COL_SMALL), lambda b, i: (b * per_b + i, 0)),
            pl.BlockSpec((tm, LANES), lambda b, i: (b * per_b + i, 0)),
            vec, vec, vec, vec, vec, vec, vec,
        ],
        out_specs=[
            hspec(FOX_HEADS, 2 * HEAD_DIM), hspec(FOX_HEADS, 2 * HEAD_DIM), tspec(FOX_HEADS),
            pl.BlockSpec((1, 1, 1, LANES), lambda b, i: (b, i, 0, 0)),
            hspec(NSA_HEADS, HEAD_DIM), hspec(NSA_HEADS, HEAD_DIM),
            hspec(NSA_KV_HEADS, HEAD_DIM), hspec(NSA_KV_HEADS, HEAD_DIM),
            hspec(NSA_KV_HEADS, HEAD_DIM + SEL_LANES), tspec(NSA_KV_HEADS),
            hspec(NSA_KV_HEADS, HEAD_DIM),
            pl.BlockSpec((1, NSA_KV_HEADS, tm // WIN_TILE, HEAD_DIM, WIN_TILE), lambda b, i: (b, 0, i, 0, 0)),
            pl.BlockSpec((1, tm, LANES), lambda b, i: (b, i, 0)),
        ],
        scratch_shapes=[pltpu.VMEM((8, LANES), f32), pltpu.VMEM((2, tm, LANES), f32)],
        compiler_params=_params("arbitrary", "arbitrary"),
        name="prep",
    )(proj, small, gq, gk, gn, gs, gw, b_forget_row, inv_freq)


def _compress_kernel(x_ref, pe_ref, w1_ref, w2_ref, g_ref, o_ref, *, n_rows, do_norm):
    half = CMP_STRIDE * HEAD_DIM
    x = x_ref[0, 0]
    pe = pe_ref[...]
    xa = (x + pe[:, :half]).astype(bf16)
    xb = (x + pe[:, half:]).astype(bf16)
    a = _dot(xa, w1_ref[0:half, :])
    b = _dot(xb, w1_ref[half:2 * half, :])
    pre = a + pltpu.roll(b, n_rows - 1, 0)
    hid = pre * jax.nn.sigmoid(pre)
    out = _dot(hid.astype(bf16), w2_ref[...])
    if do_norm:
        out = _rms(out, g_ref[...])
    o_ref[0, 0] = out.astype(bf16)


def _compress(x4, pe, w1_bf16, w2_bf16, gain, do_norm):
    n_batch, n_g, seq_len, _ = x4.shape
    n_rows = seq_len // CMP_STRIDE
    half = CMP_STRIDE * HEAD_DIM
    xr = x4.reshape(n_batch, n_g, n_rows, half)
    return pl.pallas_call(
        functools.partial(_compress_kernel, n_rows=n_rows, do_norm=do_norm),
        out_shape=jax.ShapeDtypeStruct((n_batch, n_g, n_rows, HEAD_DIM), bf16),
        grid=(n_batch, n_g),
        in_specs=[
            pl.BlockSpec((1, 1, n_rows, half), lambda b, g: (b, g, 0, 0)),
            pl.BlockSpec((1, 2 * half), lambda b, g: (0, 0)),
            pl.BlockSpec((2 * half, CMP_HIDDEN), lambda b, g: (0, 0)),
            pl.BlockSpec((CMP_HIDDEN, HEAD_DIM), lambda b, g: (0, 0)),
            pl.BlockSpec((1, HEAD_DIM), lambda b, g: (0, 0)),
        ],
        out_specs=pl.BlockSpec((1, 1, n_rows, HEAD_DIM), lambda b, g: (b, g, 0, 0)),
        compiler_params=_params("arbitrary", "arbitrary"),
        name="compress",
    )(xr, pe.reshape(1, 2 * half), w1_bf16, w2_bf16, gain)


def _softmax_block(s, d, m_prev, l_prev):
    m_blk = jnp.max(s, axis=0, keepdims=True)
    m_new = jnp.maximum(m_prev, m_blk if d is None else m_blk + d)
    p = jnp.exp2(s - (m_new if d is None else m_new - d))
    alpha = jnp.exp2(m_prev - m_new)
    l_new = alpha * l_prev + jnp.sum(p, axis=0, keepdims=True)
    return m_new, l_new, alpha, p.astype(bf16)


class _Chain(NamedTuple):
    scores: Callable
    values: Callable
    offset: Callable
    diag_mask: Callable
    emit: Callable
    s_buf: Any
    p_buf: Any


def _flash_sweep(n_full, chains, n_q):
    for c in chains:
        c.s_buf[0] = c.scores(0)
        c.p_buf[1] = jnp.zeros(c.p_buf.shape[1:], c.p_buf.dtype)

    def accumulate(c, j, slot, alpha, acc):
        return alpha * acc + _dot(c.values(jnp.maximum(j, 0)), c.p_buf[slot])

    def step(j, cur, states):
        out = []
        for c, (m, l, acc, alpha_prev) in zip(chains, states):
            acc = accumulate(c, j - 1, 1 - cur, alpha_prev, acc)
            m, l, alpha, p = _softmax_block(c.s_buf[cur], c.offset(j), m, l)
            c.p_buf[cur] = p
            c.s_buf[1 - cur] = c.scores(j + 1)
            out.append((m, l, acc, alpha))
        return tuple(out)

    def finish(cur, states):
        for c, (m, l, acc, alpha_prev) in zip(chains, states):
            acc = accumulate(c, n_full - 1, 1 - cur, alpha_prev, acc)
            m, l, alpha, p = _softmax_block(c.diag_mask(c.s_buf[cur]), None, m, l)
            acc = alpha * acc + _dot(c.values(n_full), p)
            c.emit(acc / l)

    init = tuple((jnp.full((1, n_q), -jnp.inf, f32), jnp.zeros((1, n_q), f32),
                  jnp.zeros((HEAD_DIM, n_q), f32), jnp.ones((1, n_q), f32)) for _ in chains)
    states = lax.fori_loop(0, n_full // 2, lambda jj, st: step(2 * jj + 1, 1, step(2 * jj, 0, st)), init)

    @pl.when(n_full % 2 == 0)
    def _():
        finish(0, states)

    @pl.when(n_full % 2 == 1)
    def _():
        finish(1, step(n_full - 1, 0, states))


def _fox_kernel(q_ref, k_ref, vt_ref, c_ref, o_ref, *bufs, tq, n_heads):
    hb = pl.program_id(1)
    i = pl.program_id(2)
    lane = lax.broadcasted_iota(jnp.int32, (1, LANES), 1)

    def diag_mask(s):
        kk = lax.broadcasted_iota(jnp.int32, (tq, tq), 0)
        qq = lax.broadcasted_iota(jnp.int32, (tq, tq), 1)
        return jnp.where(kk <= qq, s, -jnp.inf)

    def chain(u):
        qa = q_ref[0, u]

        def cin(j):
            return jnp.sum(jnp.where(lane == hb * n_heads + u, c_ref[0, j], 0.0), axis=-1, keepdims=True)

        ci = cin(i)

        def scores(j):
            return _dot_nt(k_ref[0, u, pl.ds(pl.multiple_of(j * tq, tq), tq), :], qa)

        def emit(out_t):
            o_ref[0, :, u * HEAD_DIM:(u + 1) * HEAD_DIM] = out_t.T.astype(bf16)

        return _Chain(scores, lambda j: vt_ref[0, u, j], lambda j: (ci - cin(j)) * LOG2E, diag_mask, emit,
                      bufs[2 * u], bufs[2 * u + 1])

    _flash_sweep(i, [chain(u) for u in range(n_heads)], tq)


def _fox(fq, fk, fvt, cin):
    n_batch, n_h, seq_len, _ = fq.shape
    tq = KV_BLOCK
    nkb = seq_len // tq
    n_heads = 4
    return pl.pallas_call(
        functools.partial(_fox_kernel, tq=tq, n_heads=n_heads),
        out_shape=jax.ShapeDtypeStruct((n_batch, seq_len, n_h * HEAD_DIM), bf16),
        grid=(n_batch, n_h // n_heads, nkb),
        in_specs=[
            pl.BlockSpec((1, n_heads, tq, 2 * HEAD_DIM), lambda b, h, i: (b, h, i, 0)),
            pl.BlockSpec((1, n_heads, seq_len, 2 * HEAD_DIM), lambda b, h, i: (b, h, 0, 0)),
            pl.BlockSpec((1, n_heads, nkb, HEAD_DIM, tq), lambda b, h, i: (b, h, 0, 0, 0)),
            pl.BlockSpec((1, nkb, 1, LANES), lambda b, h, i: (b, 0, 0, 0)),
        ],
        out_specs=pl.BlockSpec((1, tq, n_heads * HEAD_DIM), lambda b, h, i: (b, i, h)),
        scratch_shapes=[pltpu.VMEM((2, tq, tq), f32), pltpu.VMEM((2, tq, tq), bf16)] * n_heads,
        compiler_params=_params("arbitrary", "arbitrary", "arbitrary"),
        name="fox",
    )(fq, fk, fvt, cin)


def _cmp_kernel(q_ref, k_ref, v_ref, gate_ref, ov_ref, o_ref, sel_ref, *, tq, n_rows, n_cmp, n_sel):
    i = pl.program_id(2)
    g = pl.program_id(1)
    hg = NSA_GROUP
    q = q_ref[0].reshape(hg * tq, HEAD_DIM)
    s = _dot_nt(q, k_ref[0, 0]).reshape(hg, tq, n_rows)
    t = i * tq + lax.broadcasted_iota(jnp.int32, (1, tq, 1), 1)
    n = lax.broadcasted_iota(jnp.int32, (1, 1, n_rows), 2)
    last = jnp.minimum(lax.shift_right_arithmetic(t - (CMP_LEN - 1), CMP_SHIFT), n_cmp - 1)
    s = jnp.where(n <= last, s, -jnp.inf)
    m = jnp.max(s, axis=-1, keepdims=True)
    m = jnp.where(m == -jnp.inf, 0.0, m)
    e = jnp.exp(s - m)
    p = e * (1.0 / jnp.maximum(jnp.sum(e, axis=-1, keepdims=True), 1e-30))
    o = _dot(p.reshape(hg * tq, n_rows).astype(bf16), v_ref[0, 0])
    _write_gated(o_ref, gate_ref, o, g, 0, tq)

    lane = lax.broadcasted_iota(jnp.int32, (tq, LANES), 1)
    psum = p[0] + p[1] + p[2] + p[3]
    imp = _dot_f32_by_exact(psum, ov_ref[...])
    tq_pos = i * tq + lax.broadcasted_iota(jnp.int32, (tq, 1), 0)
    q_blk = lax.shift_right_logical(tq_pos, SEL_SHIFT)
    causal = lane <= q_blk
    forced = (lane == 0) | (lane == q_blk) | (lane == q_blk - 1)
    key = jnp.where(forced, jnp.inf, jnp.where(causal, imp, -1.0))
    key = jnp.where(lane < n_sel, key, -3.0)
    lane_f = lane.astype(f32)
    sel = jnp.zeros((tq, LANES), jnp.bool_)
    for _ in range(min(SEL_TOPK, n_sel)):
        mx = jnp.max(key, axis=-1, keepdims=True)
        first = jnp.min(jnp.where(key == mx, lane_f, float(LANES)), axis=-1, keepdims=True)
        pick = lane_f == first
        sel = sel | pick
        key = jnp.where(pick, -2.0, key)
    sel_ref[0, 0] = jnp.where(sel & causal, 0.0, -MASK_BIG).astype(bf16)


def _cmp(nqu, kcmp, vcmp, gates, n_cmp):
    n_batch, _, seq_len, _ = nqu.shape
    n_rows = kcmp.shape[2]
    n_sel = seq_len // SEL_BLOCK
    assert n_sel <= SEL_LANES
    tq = 1024
    c0 = np.arange(n_rows) * CMP_STRIDE
    s0 = np.arange(SEL_LANES) * SEL_BLOCK
    overlap = np.clip(np.minimum(c0[:, None] + CMP_LEN, s0[None, :] + SEL_BLOCK)
                      - np.maximum(c0[:, None], s0[None, :]), 0, None).astype(np.float32) / CMP_LEN
    overlap[n_cmp:, :] = 0.0
    overlap[:, n_sel:] = 0.0
    return pl.pallas_call(
        functools.partial(_cmp_kernel, tq=tq, n_rows=n_rows, n_cmp=n_cmp, n_sel=n_sel),
        out_shape=[
            jax.ShapeDtypeStruct((n_batch, seq_len, NSA_W), bf16),
            jax.ShapeDtypeStruct((n_batch, NSA_KV_HEADS, seq_len, SEL_LANES), bf16),
        ],
        grid=(n_batch, NSA_KV_HEADS, seq_len // tq),
        in_specs=[
            pl.BlockSpec((1, NSA_GROUP, tq, HEAD_DIM), lambda b, g, i: (b, g, i, 0)),
            pl.BlockSpec((1, 1, n_rows, HEAD_DIM), lambda b, g, i: (b, g, 0, 0)),
            pl.BlockSpec((1, 1, n_rows, HEAD_DIM), lambda b, g, i: (b, g, 0, 0)),
            pl.BlockSpec((1, tq, LANES), lambda b, g, i: (b, i, 0)),
            pl.BlockSpec((n_rows, SEL_LANES), lambda b, g, i: (0, 0)),
        ],
        out_specs=[
            pl.BlockSpec((1, tq, NSA_GROUP * HEAD_DIM), lambda b, g, i: (b, i, g)),
            pl.BlockSpec((1, 1, tq, SEL_LANES), lambda b, g, i: (b, g, i, 0)),
        ],
        compiler_params=_params("arbitrary", "arbitrary", "arbitrary"),
        name="cmp",
    )(nqu, kcmp, vcmp, gates, jnp.asarray(overlap, dtype=bf16))


def _write_gated(o_ref, gate_ref, out, g, branch, tq):
    gates = gate_ref[0]
    lane = lax.broadcasted_iota(jnp.int32, (tq, LANES), 1)
    for h in range(NSA_GROUP):
        gcol = FOX_HEADS + (g * NSA_GROUP + h) * N_BRANCH + branch
        gh = jnp.sum(jnp.where(lane == gcol, gates, 0.0), axis=-1, keepdims=True)
        o_ref[0, :, h * HEAD_DIM:(h + 1) * HEAD_DIM] = (gh * out[h * tq:(h + 1) * tq]).astype(bf16)


def _slc_kernel(q_ref, sel_ref, k_ref, vt_ref, gate_ref, *rest, tq, tk, n_cast_w):
    o_ref = rest[n_cast_w]
    bufs = rest[2 * n_cast_w + 1:]
    for w32_ref, w16_ref in zip(rest[:n_cast_w], rest[n_cast_w + 1:2 * n_cast_w + 1]):
        w16_ref[...] = w32_ref[...].astype(bf16)
    i = pl.program_id(1)
    hg = NSA_GROUP
    n_q = hg * tq
    q0 = i * tq
    jd = q0 // tk

    def diag_mask(s):
        kp = jd * tk + lax.broadcasted_iota(jnp.int32, (tk, n_q), 0)
        t = q0 + (lax.broadcasted_iota(jnp.int32, (tk, n_q), 1) & (tq - 1))
        return jnp.where(kp <= t, s, -jnp.inf)

    def chain(g):
        q = q_ref[0, g * hg:(g + 1) * hg].reshape(n_q, HEAD_DIM)
        qa = jnp.concatenate([q, jnp.concatenate([sel_ref[0, g]] * hg, axis=0)], axis=1)

        def scores(j):
            return _dot_nt(k_ref[0, g, pl.ds(pl.multiple_of(j * tk, tk), tk), :], qa)

        def emit(out):
            gates = gate_ref[0]
            lane = lax.broadcasted_iota(jnp.int32, (tq, LANES), 1)
            for h in range(hg):
                head = g * hg + h
                gh = jnp.sum(jnp.where(lane == FOX_HEADS + head * N_BRANCH + 1, gates, 0.0),
                             axis=-1, keepdims=True)
                o_ref[0, :, head * HEAD_DIM:(head + 1) * HEAD_DIM] = (
                    gh * out[:, h * tq:(h + 1) * tq].T).astype(bf16)

        return _Chain(scores, lambda j: vt_ref[0, g, j], lambda j: None, diag_mask, emit,
                      bufs[2 * g], bufs[2 * g + 1])

    _flash_sweep(jd, [chain(g) for g in range(NSA_KV_HEADS)], n_q)


def _slc(nqr, selneg, kaug, vst, gates, cast_weights):
    n_batch, _, seq_len, _ = nqr.shape
    tq, tk = 256, KV_BLOCK
    nkb = seq_len // tk
    assert tq & (tq - 1) == 0
    n_q = NSA_GROUP * tq
    n_i = seq_len // tq
    cast_in, cast_out, cast_shapes = _cast_side_job(cast_weights, n_batch * n_i, lambda b, i: b * n_i + i)
    return pl.pallas_call(
        functools.partial(_slc_kernel, tq=tq, tk=tk, n_cast_w=len(cast_weights)),
        out_shape=[jax.ShapeDtypeStruct((n_batch, seq_len, NSA_W), bf16)] + cast_shapes,
        grid=(n_batch, n_i),
        in_specs=[
            pl.BlockSpec((1, NSA_HEADS, tq, HEAD_DIM), lambda b, i: (b, 0, i, 0)),
            pl.BlockSpec((1, NSA_KV_HEADS, tq, SEL_LANES), lambda b, i: (b, 0, i, 0)),
            pl.BlockSpec((1, NSA_KV_HEADS, seq_len, HEAD_DIM + SEL_LANES), lambda b, i: (b, 0, 0, 0)),
            pl.BlockSpec((1, NSA_KV_HEADS, nkb, HEAD_DIM, tk), lambda b, i: (b, 0, 0, 0, 0)),
            pl.BlockSpec((1, tq, LANES), lambda b, i: (b, i, 0)),
        ] + cast_in,
        out_specs=[pl.BlockSpec((1, tq, NSA_W), lambda b, i: (b, i, 0))] + cast_out,
        scratch_shapes=[pltpu.VMEM((2, tk, n_q), f32), pltpu.VMEM((2, tk, n_q), bf16)] * NSA_KV_HEADS,
        compiler_params=_params("arbitrary", "arbitrary"),
        name="slc",
    )(nqr, selneg, kaug, vst, gates, *cast_weights)


def _win_kernel(q_ref, k_ref, vt_ref, gate_ref, o_ref, *, tq, n_sub):
    i = pl.program_id(2)
    g = pl.program_id(1)
    hg = NSA_GROUP
    span = WINDOW + tq
    q0s = [(i * n_sub + u) * tq for u in range(n_sub)]
    k0s = [pl.multiple_of(jnp.maximum(q0 - WINDOW, 0), tq) for q0 in q0s]
    scores = []
    for u in range(n_sub):
        q = q_ref[0, :, u * tq:(u + 1) * tq, :].reshape(hg * tq, HEAD_DIM)
        scores.append(_dot_nt(k_ref[0, 0, pl.ds(k0s[u], span), :], q))
    probs, denoms = [], []
    for u in range(n_sub):
        kp = k0s[u] + lax.broadcasted_iota(jnp.int32, (span, tq), 0)
        t = q0s[u] + lax.broadcasted_iota(jnp.int32, (span, tq), 1)
        diff = t - kp
        bias = jnp.where((diff >= 0) & (diff < WINDOW), 0.0, -jnp.inf)
        s = jnp.concatenate([scores[u][:, h * tq:(h + 1) * tq] + bias for h in range(hg)], axis=1)
        e = jnp.exp2(s - jnp.max(s, axis=0, keepdims=True))
        denoms.append(jnp.sum(e, axis=0, keepdims=True))
        probs.append(e.astype(bf16))
    gates = gate_ref[0]
    lane = lax.broadcasted_iota(jnp.int32, (tq, LANES), 1)
    for u in range(n_sub):
        jb = k0s[u] // tq
        vt = jnp.concatenate([vt_ref[0, 0, jb + c] for c in range(span // tq)], axis=1)
        out = _dot(vt, probs[u]) / denoms[u]
        for h in range(hg):
            gcol = FOX_HEADS + (g * hg + h) * N_BRANCH + 2
            gh = jnp.sum(jnp.where(lane == gcol, gates[u * tq:(u + 1) * tq], 0.0), axis=-1, keepdims=True)
            o_ref[0, u * tq:(u + 1) * tq, h * HEAD_DIM:(h + 1) * HEAD_DIM] = (
                gh * out[:, h * tq:(h + 1) * tq].T).astype(bf16)


def _win(nqr, kw, vwt, gates):
    n_batch, _, seq_len, _ = nqr.shape
    tq, n_sub = WIN_TILE, 4
    assert seq_len >= WINDOW + tq and WINDOW % tq == 0
    return pl.pallas_call(
        functools.partial(_win_kernel, tq=tq, n_sub=n_sub),
        out_shape=jax.ShapeDtypeStruct((n_batch, seq_len, NSA_W), bf16),
        grid=(n_batch, NSA_KV_HEADS, seq_len // (tq * n_sub)),
        in_specs=[
            pl.BlockSpec((1, NSA_GROUP, tq * n_sub, HEAD_DIM), lambda b, g, i: (b, g, i, 0)),
            pl.BlockSpec((1, 1, seq_len, HEAD_DIM), lambda b, g, i: (b, g, 0, 0)),
            pl.BlockSpec((1, 1, seq_len // tq, HEAD_DIM, tq), lambda b, g, i: (b, g, 0, 0, 0)),
            pl.BlockSpec((1, tq * n_sub, LANES), lambda b, g, i: (b, i, 0)),
        ],
        out_specs=pl.BlockSpec((1, tq * n_sub, NSA_GROUP * HEAD_DIM), lambda b, g, i: (b, i, g)),
        compiler_params=_params("arbitrary", "arbitrary", "arbitrary"),
        name="win",
    )(nqr, kw, vwt, gates)


def _outproj_kernel(fox_ref, c_ref, s_ref, w_ref, x_ref, mod_ref, wo_ref, o_ref):
    nsa = c_ref[...].astype(f32) + s_ref[...].astype(f32) + w_ref[...].astype(f32)
    a = jnp.concatenate([fox_ref[...], nsa.astype(bf16)], axis=1)
    o_ref[...] = x_ref[...] + mod_ref[0][2:3] * _dot(a, wo_ref[...])


def _outproj(ofox, ocmp, oslc, owin, x2d, mod3, wo_bf16, seq_len):
    m_rows = x2d.shape[0]
    tm = 512
    per_b = seq_len // tm
    half = pl.BlockSpec((tm, FOX_W), lambda i: (i, 0))
    return pl.pallas_call(
        _outproj_kernel,
        out_shape=jax.ShapeDtypeStruct((m_rows, D_MODEL), f32),
        grid=(m_rows // tm,),
        in_specs=[
            half, half, half, half,
            pl.BlockSpec((tm, D_MODEL), lambda i: (i, 0)),
            pl.BlockSpec((1, 6, D_MODEL), lambda i: (i // per_b, 0, 0)),
            pl.BlockSpec((D_MODEL, D_MODEL), lambda i: (0, 0)),
        ],
        out_specs=pl.BlockSpec((tm, D_MODEL), lambda i: (i, 0)),
        compiler_params=_params("arbitrary"),
        name="outproj",
    )(ofox, ocmp, oslc, owin, x2d, mod3, wo_bf16)


def _mlp_kernel(x_ref, xn_ref, mod_ref, g_ref, wu_ref, wd_ref, o_ref, h_ref, hn_ref, acc_ref, *, per_b, rows):
    i = pl.program_id(0)
    f = pl.program_id(1)
    n_i = pl.num_programs(0)

    def normed(x, b):
        md = mod_ref[b]
        return (_rms(x, g_ref[...]) * (1.0 + md[4:5]) + md[3:4]).astype(bf16)

    @pl.when(f == 0)
    def _():
        @pl.when(i == 0)
        def _():
            h_ref[...] = normed(x_ref[...], 0)

        @pl.when(i > 0)
        def _():
            h_ref[...] = hn_ref[...]

        acc_ref[...] = jnp.zeros_like(acc_ref)

    r0 = pl.multiple_of(f * rows, rows)
    hn_ref[pl.ds(r0, rows), :] = normed(xn_ref[pl.ds(r0, rows), :], jnp.minimum(i + 1, n_i - 1) // per_b)

    u = jnp.maximum(_dot(h_ref[...], wu_ref[...]), 0.0)
    acc_ref[...] += _dot((u * u).astype(bf16), wd_ref[...])

    @pl.when(f == pl.num_programs(1) - 1)
    def _():
        o_ref[...] = x_ref[...] + mod_ref[i // per_b][5:6] * acc_ref[...]


def _mlp(x2d, mod3, norm_g, wu_bf16, wd_bf16, seq_len):
    m_rows = x2d.shape[0]
    tm, tf = 512, 1024
    per_b = seq_len // tm
    n_i, n_f = m_rows // tm, D_FF // tf
    assert tm % n_f == 0 and (tm // n_f) % 8 == 0
    return pl.pallas_call(
        functools.partial(_mlp_kernel, per_b=per_b, rows=tm // n_f),
        out_shape=jax.ShapeDtypeStruct((m_rows, D_MODEL), f32),
        grid=(n_i, n_f),
        in_specs=[
            pl.BlockSpec((tm, D_MODEL), lambda i, f: (i, 0)),
            pl.BlockSpec((tm, D_MODEL), lambda i, f: (jnp.minimum(i + 1, n_i - 1), 0)),
            pl.BlockSpec(mod3.shape, lambda i, f: (0, 0, 0)),
            pl.BlockSpec((1, D_MODEL), lambda i, f: (0, 0)),
            pl.BlockSpec((D_MODEL, tf), lambda i, f: (0, f)),
            pl.BlockSpec((tf, D_MODEL), lambda i, f: (f, 0)),
        ],
        out_specs=pl.BlockSpec((tm, D_MODEL), lambda i, f: (i, 0)),
        scratch_shapes=[pltpu.VMEM((tm, D_MODEL), bf16), pltpu.VMEM((tm, D_MODEL), bf16),
                        pltpu.VMEM((tm, D_MODEL), f32)],
        compiler_params=_params("arbitrary", "arbitrary"),
        name="mlp",
    )(x2d, x2d, mod3, norm_g, wu_bf16, wd_bf16)


def _layer(layer, x, c, w_ada, b_ada, norm1_g, w_in_all, b_forget, fox_q_norm, fox_k_norm, nsa_q_norm,
           cmp_k_norm, slc_k_norm, win_k_norm, cmp_pe_k, cmp_w1_k, cmp_w2_k, cmp_pe_v, cmp_w1_v,
           cmp_w2_v, w_out, norm2_g, w_up, w_down):
    n_batch, seq_len, _ = x.shape
    n_cmp = (seq_len - CMP_LEN) // CMP_STRIDE + 1
    row = lambda v: v.reshape(1, -1)

    half = HEAD_DIM // 2
    inv_freq = ROPE_THETA ** (-jnp.arange(half, dtype=f32) / half)
    inv_freq = jnp.concatenate([inv_freq, inv_freq]).reshape(1, HEAD_DIM)

    mod, w_a, w_b, w_s = _ada(c, w_ada, b_ada, w_in_all, layer)
    mod3 = mod.reshape(n_batch, 6, D_MODEL)
    x2d = x.reshape(n_batch * seq_len, D_MODEL)
    proj, small = _proj(x2d, mod3, row(norm1_g), w_a, w_b, w_s, seq_len)
    (fq, fk, fvt, cin, nqu, nqr, kc, vc, kaug, vst, kw, vw, gates) = _prep(
        proj, small, n_batch, seq_len, row(fox_q_norm), row(fox_k_norm), row(nsa_q_norm), row(slc_k_norm),
        row(win_k_norm), jnp.pad(b_forget, (0, LANES - FOX_HEADS)).reshape(1, LANES), inv_freq)
    kcmp = _compress(kc, cmp_pe_k, cmp_w1_k.astype(bf16), cmp_w2_k.astype(bf16), row(cmp_k_norm), True)
    vcmp = _compress(vc, cmp_pe_v, cmp_w1_v.astype(bf16), cmp_w2_v.astype(bf16), row(cmp_k_norm), False)
    ofox = _fox(fq, fk, fvt, cin)
    ocmp, selneg = _cmp(nqu, kcmp, vcmp, gates, n_cmp)
    oslc, w_up16, w_down16, w_out16 = _slc(nqr, selneg, kaug, vst, gates, [w_up, w_down, w_out])
    owin = _win(nqr, kw, vw, gates)
    x1 = _outproj(ofox.reshape(-1, FOX_W), ocmp.reshape(-1, NSA_W), oslc.reshape(-1, NSA_W),
                  owin.reshape(-1, NSA_W), x2d, mod3, w_out16, seq_len)
    x2 = _mlp(x1, mod3, row(norm2_g), w_up16, w_down16, seq_len)
    return x2.reshape(n_batch, seq_len, D_MODEL)


def kernel(x, c, w_ada, b_ada, norm1_g, w_in, b_forget, fox_q_norm, fox_k_norm, nsa_q_norm, cmp_k_norm,
           slc_k_norm, win_k_norm, cmp_pe_k, cmp_w1_k, cmp_w2_k, cmp_pe_v, cmp_w1_v, cmp_w2_v, w_out,
           norm2_g, w_up, w_down):
    depth = w_ada.shape[0]
    for l in range(depth):
        x = _layer(l, x, c, w_ada[l], b_ada[l], norm1_g[l], w_in, b_forget[l], fox_q_norm[l], fox_k_norm[l],
                   nsa_q_norm[l], cmp_k_norm[l], slc_k_norm[l], win_k_norm[l], cmp_pe_k[l], cmp_w1_k[l],
                   cmp_w2_k[l], cmp_pe_v[l], cmp_w1_v[l], cmp_w2_v[l], w_out[l], norm2_g[l], w_up[l],
                   w_down[l])
    return x
```

```python
import functools
import math
from typing import Any, Callable, NamedTuple

import numpy as np
import jax
import jax.numpy as jnp
from jax import lax
from jax.experimental import pallas as pl
from jax.experimental.pallas import tpu as pltpu

D_MODEL = 2048
HEAD_DIM = 128
FOX_HEADS = 8
NSA_HEADS = 8
NSA_KV_HEADS = 2
NSA_GROUP = NSA_HEADS // NSA_KV_HEADS
N_BRANCH = 3
D_FF = 4 * D_MODEL
ROPE_THETA = 10000.0
CMP_LEN = 32
CMP_STRIDE = 16
CMP_SHIFT = 4
CMP_HIDDEN = 2 * HEAD_DIM
SEL_BLOCK = 64
SEL_SHIFT = 6
SEL_TOPK = 16
WINDOW = 512
NORM_EPS = 1e-6
ATTN_SCALE = HEAD_DIM ** -0.5
FOX_W = FOX_HEADS * HEAD_DIM
NSA_W = NSA_HEADS * HEAD_DIM
KV_W = NSA_KV_HEADS * HEAD_DIM

LANES = 128
SEL_LANES = LANES
MASK_BIG = 1e30
KV_BLOCK = 512
WIN_TILE = 128
LOG2E = math.log2(math.e)

COL_FQ = 0
COL_FK = COL_FQ + FOX_W
COL_FV = COL_FK + FOX_W
COL_NQ = COL_FV + FOX_W
COL_KC = COL_NQ + NSA_W
COL_VC = COL_KC + KV_W
COL_KS = COL_VC + KV_W
COL_VS = COL_KS + KV_W
COL_KW = COL_VS + KV_W
COL_VW = COL_KW + KV_W
COL_SMALL = COL_VW + KV_W
W_IN_Z0 = 3 * FOX_W
W_IN_NQ0 = W_IN_Z0 + FOX_HEADS
W_IN_GZ0 = W_IN_NQ0 + NSA_W + 6 * KV_W

VMEM_LIMIT = 56 * 1024 * 1024

f32 = jnp.float32
bf16 = jnp.bfloat16


def _params(*sem):
    return pltpu.CompilerParams(dimension_semantics=sem, vmem_limit_bytes=VMEM_LIMIT)


def _dot_nt(a, b):
    return lax.dot_general(a, b, (((1,), (1,)), ((), ())), preferred_element_type=f32)


def _dot(a, b):
    return jnp.dot(a, b, preferred_element_type=f32)


def _split3(x):
    hi = x.astype(bf16)
    r1 = x - hi.astype(f32)
    mid = r1.astype(bf16)
    lo = (r1 - mid.astype(f32)).astype(bf16)
    return hi, mid, lo


def _dot_f32_by_exact(x, w_bf16):
    hi, mid, lo = _split3(x)
    return _dot(hi, w_bf16) + (_dot(mid, w_bf16) + _dot(lo, w_bf16))


def _rms(x, gain):
    ms = jnp.mean(x * x, axis=-1, keepdims=True)
    return x * lax.rsqrt(ms + NORM_EPS) * gain


def _ada_kernel(ct_ref, w_ref, b_ref, o_ref, *, n_batch, k_chunk):
    ct = ct_ref[...]
    act = ct * jax.nn.sigmoid(ct)
    rows = []
    for b in range(n_batch):
        col = act[:, b:b + 1]
        acc = b_ref[...]
        for k0 in range(0, D_MODEL, k_chunk):
            acc = acc + jnp.sum(w_ref[k0:k0 + k_chunk, :] * col[k0:k0 + k_chunk], axis=0, keepdims=True)
        rows.append(acc)
    o_ref[...] = jnp.concatenate(rows, axis=0)


def _ada(c, w_ada, b_ada):
    n_batch = c.shape[0]
    n_out = w_ada.shape[1]
    tn = 1024
    return pl.pallas_call(
        functools.partial(_ada_kernel, n_batch=n_batch, k_chunk=256),
        out_shape=jax.ShapeDtypeStruct((n_batch, n_out), f32),
        grid=(n_out // tn,),
        in_specs=[
            pl.BlockSpec((D_MODEL, n_batch), lambda j: (0, 0)),
            pl.BlockSpec((D_MODEL, tn), lambda j: (0, j)),
            pl.BlockSpec((1, tn), lambda j: (0, j)),
        ],
        out_specs=pl.BlockSpec((n_batch, tn), lambda j: (0, j)),
        compiler_params=_params("arbitrary"),
        name="ada",
    )(c.T, w_ada, b_ada.reshape(1, n_out))


def _proj_kernel(x_ref, mod_ref, g_ref, wa_ref, wb_ref, ws_ref, o_ref, small_ref):
    md = mod_ref[0]
    h = (_rms(x_ref[...], g_ref[...]) * (1.0 + md[1:2]) + md[0:1]).astype(bf16)
    n_a = wa_ref.shape[0]
    o_ref[:, :n_a] = _dot_nt(h, wa_ref[...]).astype(bf16)
    o_ref[:, n_a:] = _dot_nt(h, wb_ref[...]).astype(bf16)
    small_ref[...] = _dot_nt(h, ws_ref[...])


def _cast_side_job(weights, n_steps, step_of):
    n_cast = max(c for c in range(1, n_steps + 1) if all(w.shape[0] % (16 * c) == 0 for w in weights))
    idx = lambda *ids: (jnp.minimum(step_of(*ids), n_cast - 1), 0)
    specs = [pl.BlockSpec((w.shape[0] // n_cast, w.shape[1]), idx) for w in weights]
    return specs, specs, [jax.ShapeDtypeStruct(w.shape, bf16) for w in weights]


def _proj(x2d, mod3, norm_g, wa, wb, ws, seq_len):
    m_rows = x2d.shape[0]
    tm = 512
    assert seq_len % tm == 0
    per_b = seq_len // tm
    n_main = wa.shape[0] + wb.shape[0]
    resident = lambda w: pl.BlockSpec(w.shape, lambda i: (0, 0), pipeline_mode=pl.Buffered(1))
    return pl.pallas_call(
        _proj_kernel,
        out_shape=[jax.ShapeDtypeStruct((m_rows, n_main), bf16), jax.ShapeDtypeStruct((m_rows, LANES), f32)],
        grid=(m_rows // tm,),
        in_specs=[
            pl.BlockSpec((tm, D_MODEL), lambda i: (i, 0)),
            pl.BlockSpec((1, 6, D_MODEL), lambda i: (i // per_b, 0, 0)),
            pl.BlockSpec((1, D_MODEL), lambda i: (0, 0)),
            resident(wa), resident(wb), resident(ws),
        ],
        out_specs=[pl.BlockSpec((tm, n_main), lambda i: (i, 0)), pl.BlockSpec((tm, LANES), lambda i: (i, 0))],
        compiler_params=_params("arbitrary"),
        name="proj",
    )(x2d, mod3, norm_g, wa, wb, ws)


def _prep_kernel(p_ref, small_ref, gq_ref, gk_ref, gn_ref, gs_ref, gw_ref, bf_ref, inv_ref,
                 fq_ref, fk_ref, fvt_ref, cin_ref, nqu_ref, nqr_ref, kc_ref, vc_ref,
                 ks_ref, vst_ref, kw_ref, vw_ref, gate_ref, carry_ref, rot_ref, *, tm):
    i = pl.program_id(1)

    @pl.when(i == 0)
    def _():
        carry_ref[...] = jnp.zeros_like(carry_ref)

    def head(col, h):
        return p_ref[:, col + h * HEAD_DIM: col + (h + 1) * HEAD_DIM].astype(f32)

    row = lax.broadcasted_iota(jnp.int32, (tm, LANES), 0)
    lane = lax.broadcasted_iota(jnp.int32, (tm, LANES), 1)
    pos = i * tm + row

    @pl.when((pl.program_id(0) == 0) & (i == 0))
    def _():
        ang_row = row.astype(f32) * inv_ref[...]
        rot_ref[0] = jnp.cos(ang_row)
        rot_ref[1] = jnp.sin(ang_row)

    ang0 = (i * tm).astype(f32) * inv_ref[...]
    cos0, sin0 = jnp.cos(ang0), jnp.sin(ang0)
    cos = cos0 * rot_ref[0] - sin0 * rot_ref[1]
    sin = sin0 * rot_ref[0] + cos0 * rot_ref[1]
    sin_signed = jnp.where(lane < HEAD_DIM // 2, -sin, sin)

    def rope(x):
        return x * cos + pltpu.roll(x, HEAD_DIM // 2, 1) * sin_signed

    small = small_ref[...]
    z = small + bf_ref[...]
    logf = jnp.minimum(z, 0.0) - jnp.log1p(jnp.exp(-jnp.abs(z)))
    t_idx = lax.broadcasted_iota(jnp.int32, (tm, tm), 0)
    s_idx = lax.broadcasted_iota(jnp.int32, (tm, tm), 1)
    tri = jnp.where(s_idx <= t_idx, 1.0, 0.0).astype(bf16)
    hi, mid, lo = _split3(logf)
    local = _dot(tri, hi) + (_dot(tri, mid) + _dot(tri, lo))
    cin_ref[0, 0] = carry_ref[0:1, :]
    carry_ref[...] = carry_ref[...] + local[tm - 1:tm, :]
    b_hi, b_mid, b_lo = (v.astype(f32) for v in _split3(local * (-LOG2E)))
    ones3 = jnp.where(lane < 3, 1.0, 0.0).astype(bf16)

    for h in range(FOX_HEADS):
        q = _rms(head(COL_FQ, h), gq_ref[...]) * (ATTN_SCALE * LOG2E)
        fq_ref[0, h] = jnp.concatenate([q.astype(bf16), ones3], axis=1)
        k = _rms(head(COL_FK, h), gk_ref[...])
        bias = jnp.where(lane == 0, b_hi[:, h:h + 1],
                         jnp.where(lane == 1, b_mid[:, h:h + 1],
                                   jnp.where(lane == 2, b_lo[:, h:h + 1], 0.0)))
        fk_ref[0, h] = jnp.concatenate([k.astype(bf16), bias.astype(bf16)], axis=1)
        fvt_ref[0, h, 0] = head(COL_FV, h).T.astype(bf16)

    gate_ref[0] = jax.nn.sigmoid(small)

    for h in range(NSA_HEADS):
        qn = _rms(head(COL_NQ, h), gn_ref[...])
        nqu_ref[0, h] = (qn * ATTN_SCALE).astype(bf16)
        nqr_ref[0, h] = rope(qn * (ATTN_SCALE * LOG2E)).astype(bf16)
    onehot = jnp.where(lane == lax.shift_right_logical(pos, SEL_SHIFT), 1.0, 0.0).astype(bf16)
    for g in range(NSA_KV_HEADS):
        kc_ref[0, g] = head(COL_KC, g).astype(bf16)
        vc_ref[0, g] = head(COL_VC, g).astype(bf16)
        ks = rope(_rms(head(COL_KS, g), gs_ref[...])).astype(bf16)
        ks_ref[0, g] = jnp.concatenate([ks, onehot], axis=1)
        vst_ref[0, g, 0] = head(COL_VS, g).T.astype(bf16)
        kw_ref[0, g] = rope(_rms(head(COL_KW, g), gw_ref[...])).astype(bf16)
        vwt = head(COL_VW, g).T.astype(bf16)
        for c in range(tm // WIN_TILE):
            vw_ref[0, g, c] = vwt[:, c * WIN_TILE:(c + 1) * WIN_TILE]


def _prep(proj, small, n_batch, seq_len, gq, gk, gn, gs, gw, b_forget_row, inv_freq):
    tm = KV_BLOCK
    per_b = seq_len // tm
    hshape = lambda n, w, dt: jax.ShapeDtypeStruct((n_batch, n, seq_len, w), dt)
    hspec = lambda n, w: pl.BlockSpec((1, n, tm, w), lambda b, i: (b, 0, i, 0))
    tshape = lambda n: jax.ShapeDtypeStruct((n_batch, n, per_b, HEAD_DIM, tm), bf16)
    tspec = lambda n: pl.BlockSpec((1, n, 1, HEAD_DIM, tm), lambda b, i: (b, 0, i, 0, 0))
    vec = pl.BlockSpec((1, LANES), lambda b, i: (0, 0))
    return pl.pallas_call(
        functools.partial(_prep_kernel, tm=tm),
        out_shape=[
            hshape(FOX_HEADS, 2 * HEAD_DIM, bf16), hshape(FOX_HEADS, 2 * HEAD_DIM, bf16), tshape(FOX_HEADS),
            jax.ShapeDtypeStruct((n_batch, per_b, 1, LANES), f32),
            hshape(NSA_HEADS, HEAD_DIM, bf16), hshape(NSA_HEADS, HEAD_DIM, bf16),
            hshape(NSA_KV_HEADS, HEAD_DIM, bf16), hshape(NSA_KV_HEADS, HEAD_DIM, bf16),
            hshape(NSA_KV_HEADS, HEAD_DIM + SEL_LANES, bf16), tshape(NSA_KV_HEADS),
            hshape(NSA_KV_HEADS, HEAD_DIM, bf16),
            jax.ShapeDtypeStruct((n_batch, NSA_KV_HEADS, seq_len // WIN_TILE, HEAD_DIM, WIN_TILE), bf16),
            jax.ShapeDtypeStruct((n_batch, seq_len, LANES), f32),
        ],
        grid=(n_batch, per_b),
        in_specs=[
            pl.BlockSpec((tm, COL_SMALL), lambda b, i: (b * per_b + i, 0)),
            pl.BlockSpec((tm, LANES), lambda b, i: (b * per_b + i, 0)),
            vec, vec, vec, vec, vec, vec, vec,
        ],
        out_specs=[
            hspec(FOX_HEADS, 2 * HEAD_DIM), hspec(FOX_HEADS, 2 * HEAD_DIM), tspec(FOX_HEADS),
            pl.BlockSpec((1, 1, 1, LANES), lambda b, i: (b, i, 0, 0)),
            hspec(NSA_HEADS, HEAD_DIM), hspec(NSA_HEADS, HEAD_DIM),
            hspec(NSA_KV_HEADS, HEAD_DIM), hspec(NSA_KV_HEADS, HEAD_DIM),
            hspec(NSA_KV_HEADS, HEAD_DIM + SEL_LANES), tspec(NSA_KV_HEADS),
            hspec(NSA_KV_HEADS, HEAD_DIM),
            pl.BlockSpec((1, NSA_KV_HEADS, tm // WIN_TILE, HEAD_DIM, WIN_TILE), lambda b, i: (b, 0, i, 0, 0)),
            pl.BlockSpec((1, tm, LANES), lambda b, i: (b, i, 0)),
        ],
        scratch_shapes=[pltpu.VMEM((8, LANES), f32), pltpu.VMEM((2, tm, LANES), f32)],
        compiler_params=_params("arbitrary", "arbitrary"),
        name="prep",
    )(proj, small, gq, gk, gn, gs, gw, b_forget_row, inv_freq)


def _compress_kernel(x_ref, pe_ref, w1_ref, w2_ref, g_ref, o_ref, *, n_rows, do_norm):
    half = CMP_STRIDE * HEAD_DIM
    x = x_ref[0, 0]
    pe = pe_ref[...]
    xa = (x + pe[:, :half]).astype(bf16)
    xb = (x + pe[:, half:]).astype(bf16)
    a = _dot(xa, w1_ref[0:half, :])
    b = _dot(xb, w1_ref[half:2 * half, :])
    pre = a + pltpu.roll(b, n_rows - 1, 0)
    hid = pre * jax.nn.sigmoid(pre)
    out = _dot(hid.astype(bf16), w2_ref[...])
    if do_norm:
        out = _rms(out, g_ref[...])
    o_ref[0, 0] = out.astype(bf16)


def _compress(x4, pe, w1_bf16, w2_bf16, gain, do_norm):
    n_batch, n_g, seq_len, _ = x4.shape
    n_rows = seq_len // CMP_STRIDE
    half = CMP_STRIDE * HEAD_DIM
    xr = x4.reshape(n_batch, n_g, n_rows, half)
    return pl.pallas_call(
        functools.partial(_compress_kernel, n_rows=n_rows, do_norm=do_norm),
        out_shape=jax.ShapeDtypeStruct((n_batch, n_g, n_rows, HEAD_DIM), bf16),
        grid=(n_batch, n_g),
        in_specs=[
            pl.BlockSpec((1, 1, n_rows, half), lambda b, g: (b, g, 0, 0)),
            pl.BlockSpec((1, 2 * half), lambda b, g: (0, 0)),
            pl.BlockSpec((2 * half, CMP_HIDDEN), lambda b, g: (0, 0)),
            pl.BlockSpec((CMP_HIDDEN, HEAD_DIM), lambda b, g: (0, 0)),
            pl.BlockSpec((1, HEAD_DIM), lambda b, g: (0, 0)),
        ],
        out_specs=pl.BlockSpec((1, 1, n_rows, HEAD_DIM), lambda b, g: (b, g, 0, 0)),
        compiler_params=_params("arbitrary", "arbitrary"),
        name="compress",
    )(xr, pe.reshape(1, 2 * half), w1_bf16, w2_bf16, gain)


def _softmax_block(s, d, m_prev, l_prev):
    m_blk = jnp.max(s, axis=0, keepdims=True)
    m_new = jnp.maximum(m_prev, m_blk if d is None else m_blk + d)
    p = jnp.exp2(s - (m_new if d is None else m_new - d))
    alpha = jnp.exp2(m_prev - m_new)
    l_new = alpha * l_prev + jnp.sum(p, axis=0, keepdims=True)
    return m_new, l_new, alpha, p.astype(bf16)


class _Chain(NamedTuple):
    scores: Callable
    values: Callable
    offset: Callable
    diag_mask: Callable
    emit: Callable
    s_buf: Any
    p_buf: Any


def _flash_sweep(n_full, chains, n_q):
    for c in chains:
        c.s_buf[0] = c.scores(0)
        c.p_buf[1] = jnp.zeros(c.p_buf.shape[1:], c.p_buf.dtype)

    def accumulate(c, j, slot, alpha, acc):
        return alpha * acc + _dot(c.values(jnp.maximum(j, 0)), c.p_buf[slot])

    def step(j, cur, states):
        out = []
        for c, (m, l, acc, alpha_prev) in zip(chains, states):
            acc = accumulate(c, j - 1, 1 - cur, alpha_prev, acc)
            m, l, alpha, p = _softmax_block(c.s_buf[cur], c.offset(j), m, l)
            c.p_buf[cur] = p
            c.s_buf[1 - cur] = c.scores(j + 1)
            out.append((m, l, acc, alpha))
        return tuple(out)

    def finish(cur, states):
        for c, (m, l, acc, alpha_prev) in zip(chains, states):
            acc = accumulate(c, n_full - 1, 1 - cur, alpha_prev, acc)
            m, l, alpha, p = _softmax_block(c.diag_mask(c.s_buf[cur]), None, m, l)
            acc = alpha * acc + _dot(c.values(n_full), p)
            c.emit(acc / l)

    init = tuple((jnp.full((1, n_q), -jnp.inf, f32), jnp.zeros((1, n_q), f32),
                  jnp.zeros((HEAD_DIM, n_q), f32), jnp.ones((1, n_q), f32)) for _ in chains)
    states = lax.fori_loop(0, n_full // 2, lambda jj, st: step(2 * jj + 1, 1, step(2 * jj, 0, st)), init)

    @pl.when(n_full % 2 == 0)
    def _():
        finish(0, states)

    @pl.when(n_full % 2 == 1)
    def _():
        finish(1, step(n_full - 1, 0, states))


def _fox_kernel(q_ref, k_ref, vt_ref, c_ref, o_ref, *bufs, tq, n_heads):
    hb = pl.program_id(1)
    i = pl.program_id(2)
    lane = lax.broadcasted_iota(jnp.int32, (1, LANES), 1)

    def diag_mask(s):
        kk = lax.broadcasted_iota(jnp.int32, (tq, tq), 0)
        qq = lax.broadcasted_iota(jnp.int32, (tq, tq), 1)
        return jnp.where(kk <= qq, s, -jnp.inf)

    def chain(u):
        qa = q_ref[0, u]

        def cin(j):
            return jnp.sum(jnp.where(lane == hb * n_heads + u, c_ref[0, j], 0.0), axis=-1, keepdims=True)

        ci = cin(i)

        def scores(j):
            return _dot_nt(k_ref[0, u, pl.ds(pl.multiple_of(j * tq, tq), tq), :], qa)

        def emit(out_t):
            o_ref[0, :, u * HEAD_DIM:(u + 1) * HEAD_DIM] = out_t.T.astype(bf16)

        return _Chain(scores, lambda j: vt_ref[0, u, j], lambda j: (ci - cin(j)) * LOG2E, diag_mask, emit,
                      bufs[2 * u], bufs[2 * u + 1])

    _flash_sweep(i, [chain(u) for u in range(n_heads)], tq)


def _fox(fq, fk, fvt, cin):
    n_batch, n_h, seq_len, _ = fq.shape
    tq = KV_BLOCK
    nkb = seq_len // tq
    n_heads = 4
    return pl.pallas_call(
        functools.partial(_fox_kernel, tq=tq, n_heads=n_heads),
        out_shape=jax.ShapeDtypeStruct((n_batch, seq_len, n_h * HEAD_DIM), bf16),
        grid=(n_batch, n_h // n_heads, nkb),
        in_specs=[
            pl.BlockSpec((1, n_heads, tq, 2 * HEAD_DIM), lambda b, h, i: (b, h, i, 0)),
            pl.BlockSpec((1, n_heads, seq_len, 2 * HEAD_DIM), lambda b, h, i: (b, h, 0, 0)),
            pl.BlockSpec((1, n_heads, nkb, HEAD_DIM, tq), lambda b, h, i: (b, h, 0, 0, 0)),
            pl.BlockSpec((1, nkb, 1, LANES), lambda b, h, i: (b, 0, 0, 0)),
        ],
        out_specs=pl.BlockSpec((1, tq, n_heads * HEAD_DIM), lambda b, h, i: (b, i, h)),
        scratch_shapes=[pltpu.VMEM((2, tq, tq), f32), pltpu.VMEM((2, tq, tq), bf16)] * n_heads,
        compiler_params=_params("arbitrary", "arbitrary", "arbitrary"),
        name="fox",
    )(fq, fk, fvt, cin)


def _cmp_kernel(q_ref, k_ref, v_ref, gate_ref, ov_ref, o_ref, sel_ref, *, tq, n_rows, n_cmp, n_sel):
    i = pl.program_id(2)
    g = pl.program_id(1)
    hg = NSA_GROUP
    q = q_ref[0].reshape(hg * tq, HEAD_DIM)
    s = _dot_nt(q, k_ref[0, 0]).reshape(hg, tq, n_rows)
    t = i * tq + lax.broadcasted_iota(jnp.int32, (1, tq, 1), 1)
    n = lax.broadcasted_iota(jnp.int32, (1, 1, n_rows), 2)
    last = jnp.minimum(lax.shift_right_arithmetic(t - (CMP_LEN - 1), CMP_SHIFT), n_cmp - 1)
    s = jnp.where(n <= last, s, -jnp.inf)
    m = jnp.max(s, axis=-1, keepdims=True)
    m = jnp.where(m == -jnp.inf, 0.0, m)
    e = jnp.exp(s - m)
    p = e * (1.0 / jnp.maximum(jnp.sum(e, axis=-1, keepdims=True), 1e-30))
    o = _dot(p.reshape(hg * tq, n_rows).astype(bf16), v_ref[0, 0])
    _write_gated(o_ref, gate_ref, o, g, 0, tq)

    lane = lax.broadcasted_iota(jnp.int32, (tq, LANES), 1)
    psum = p[0] + p[1] + p[2] + p[3]
    imp = _dot_f32_by_exact(psum, ov_ref[...])
    tq_pos = i * tq + lax.broadcasted_iota(jnp.int32, (tq, 1), 0)
    q_blk = lax.shift_right_logical(tq_pos, SEL_SHIFT)
    causal = lane <= q_blk
    forced = (lane == 0) | (lane == q_blk) | (lane == q_blk - 1)
    key = jnp.where(forced, jnp.inf, jnp.where(causal, imp, -1.0))
    key = jnp.where(lane < n_sel, key, -3.0)
    lane_f = lane.astype(f32)
    sel = jnp.zeros((tq, LANES), jnp.bool_)
    for _ in range(min(SEL_TOPK, n_sel)):
        mx = jnp.max(key, axis=-1, keepdims=True)
        first = jnp.min(jnp.where(key == mx, lane_f, float(LANES)), axis=-1, keepdims=True)
        pick = lane_f == first
        sel = sel | pick
        key = jnp.where(pick, -2.0, key)
    sel_ref[0, 0] = jnp.where(sel & causal, 0.0, -MASK_BIG).astype(bf16)


def _cmp(nqu, kcmp, vcmp, gates, n_cmp):
    n_batch, _, seq_len, _ = nqu.shape
    n_rows = kcmp.shape[2]
    n_sel = seq_len // SEL_BLOCK
    assert n_sel <= SEL_LANES
    tq = 1024
    c0 = np.arange(n_rows) * CMP_STRIDE
    s0 = np.arange(SEL_LANES) * SEL_BLOCK
    overlap = np.clip(np.minimum(c0[:, None] + CMP_LEN, s0[None, :] + SEL_BLOCK)
                      - np.maximum(c0[:, None], s0[None, :]), 0, None).astype(np.float32) / CMP_LEN
    overlap[n_cmp:, :] = 0.0
    overlap[:, n_sel:] = 0.0
    return pl.pallas_call(
        functools.partial(_cmp_kernel, tq=tq, n_rows=n_rows, n_cmp=n_cmp, n_sel=n_sel),
        out_shape=[
            jax.ShapeDtypeStruct((n_batch, seq_len, NSA_W), bf16),
            jax.ShapeDtypeStruct((n_batch, NSA_KV_HEADS, seq_len, SEL_LANES), bf16),
        ],
        grid=(n_batch, NSA_KV_HEADS, seq_len // tq),
        in_specs=[
            pl.BlockSpec((1, NSA_GROUP, tq, HEAD_DIM), lambda b, g, i: (b, g, i, 0)),
            pl.BlockSpec((1, 1, n_rows, HEAD_DIM), lambda b, g, i: (b, g, 0, 0)),
            pl.BlockSpec((1, 1, n_rows, HEAD_DIM), lambda b, g, i: (b, g, 0, 0)),
            pl.BlockSpec((1, tq, LANES), lambda b, g, i: (b, i, 0)),
            pl.BlockSpec((n_rows, SEL_LANES), lambda b, g, i: (0, 0)),
        ],
        out_specs=[
            pl.BlockSpec((1, tq, NSA_GROUP * HEAD_DIM), lambda b, g, i: (b, i, g)),
            pl.BlockSpec((1, 1, tq, SEL_LANES), lambda b, g, i: (b, g, i, 0)),
        ],
        compiler_params=_params("arbitrary", "arbitrary", "arbitrary"),
        name="cmp",
    )(nqu, kcmp, vcmp, gates, jnp.asarray(overlap, dtype=bf16))


def _write_gated(o_ref, gate_ref, out, g, branch, tq):
    gates = gate_ref[0]
    lane = lax.broadcasted_iota(jnp.int32, (tq, LANES), 1)
    for h in range(NSA_GROUP):
        gcol = FOX_HEADS + (g * NSA_GROUP + h) * N_BRANCH + branch
        gh = jnp.sum(jnp.where(lane == gcol, gates, 0.0), axis=-1, keepdims=True)
        o_ref[0, :, h * HEAD_DIM:(h + 1) * HEAD_DIM] = (gh * out[h * tq:(h + 1) * tq]).astype(bf16)


def _slc_kernel(q_ref, sel_ref, k_ref, vt_ref, gate_ref, *rest, tq, tk, n_cast_w):
    o_ref = rest[n_cast_w]
    bufs = rest[2 * n_cast_w + 1:]
    for w32_ref, w16_ref in zip(rest[:n_cast_w], rest[n_cast_w + 1:2 * n_cast_w + 1]):
        w16_ref[...] = w32_ref[...].astype(bf16)
    i = pl.program_id(1)
    hg = NSA_GROUP
    n_q = hg * tq
    q0 = i * tq
    jd = q0 // tk

    def diag_mask(s):
        kp = jd * tk + lax.broadcasted_iota(jnp.int32, (tk, n_q), 0)
        t = q0 + (lax.broadcasted_iota(jnp.int32, (tk, n_q), 1) & (tq - 1))
        return jnp.where(kp <= t, s, -jnp.inf)

    def chain(g):
        q = q_ref[0, g * hg:(g + 1) * hg].reshape(n_q, HEAD_DIM)
        qa = jnp.concatenate([q, jnp.concatenate([sel_ref[0, g]] * hg, axis=0)], axis=1)

        def scores(j):
            return _dot_nt(k_ref[0, g, pl.ds(pl.multiple_of(j * tk, tk), tk), :], qa)

        def emit(out):
            gates = gate_ref[0]
            lane = lax.broadcasted_iota(jnp.int32, (tq, LANES), 1)
            for h in range(hg):
                head = g * hg + h
                gh = jnp.sum(jnp.where(lane == FOX_HEADS + head * N_BRANCH + 1, gates, 0.0),
                             axis=-1, keepdims=True)
                o_ref[0, :, head * HEAD_DIM:(head + 1) * HEAD_DIM] = (
                    gh * out[:, h * tq:(h + 1) * tq].T).astype(bf16)

        return _Chain(scores, lambda j: vt_ref[0, g, j], lambda j: None, diag_mask, emit,
                      bufs[2 * g], bufs[2 * g + 1])

    _flash_sweep(jd, [chain(g) for g in range(NSA_KV_HEADS)], n_q)


def _slc(nqr, selneg, kaug, vst, gates, cast_weights):
    n_batch, _, seq_len, _ = nqr.shape
    tq, tk = 256, KV_BLOCK
    nkb = seq_len // tk
    assert tq & (tq - 1) == 0
    n_q = NSA_GROUP * tq
    n_i = seq_len // tq
    cast_in, cast_out, cast_shapes = _cast_side_job(cast_weights, n_batch * n_i, lambda b, i: b * n_i + i)
    return pl.pallas_call(
        functools.partial(_slc_kernel, tq=tq, tk=tk, n_cast_w=len(cast_weights)),
        out_shape=[jax.ShapeDtypeStruct((n_batch, seq_len, NSA_W), bf16)] + cast_shapes,
        grid=(n_batch, n_i),
        in_specs=[
            pl.BlockSpec((1, NSA_HEADS, tq, HEAD_DIM), lambda b, i: (b, 0, i, 0)),
            pl.BlockSpec((1, NSA_KV_HEADS, tq, SEL_LANES), lambda b, i: (b, 0, i, 0)),
            pl.BlockSpec((1, NSA_KV_HEADS, seq_len, HEAD_DIM + SEL_LANES), lambda b, i: (b, 0, 0, 0)),
            pl.BlockSpec((1, NSA_KV_HEADS, nkb, HEAD_DIM, tk), lambda b, i: (b, 0, 0, 0, 0)),
            pl.BlockSpec((1, tq, LANES), lambda b, i: (b, i, 0)),
        ] + cast_in,
        out_specs=[pl.BlockSpec((1, tq, NSA_W), lambda b, i: (b, i, 0))] + cast_out,
        scratch_shapes=[pltpu.VMEM((2, tk, n_q), f32), pltpu.VMEM((2, tk, n_q), bf16)] * NSA_KV_HEADS,
        compiler_params=_params("arbitrary", "arbitrary"),
        name="slc",
    )(nqr, selneg, kaug, vst, gates, *cast_weights)


def _win_kernel(q_ref, k_ref, vt_ref, gate_ref, o_ref, *, tq, n_sub):
    i = pl.program_id(2)
    g = pl.program_id(1)
    hg = NSA_GROUP
    span = WINDOW + tq
    q0s = [(i * n_sub + u) * tq for u in range(n_sub)]
    k0s = [pl.multiple_of(jnp.maximum(q0 - WINDOW, 0), tq) for q0 in q0s]
    scores = []
    for u in range(n_sub):
        q = q_ref[0, :, u * tq:(u + 1) * tq, :].reshape(hg * tq, HEAD_DIM)
        scores.append(_dot_nt(k_ref[0, 0, pl.ds(k0s[u], span), :], q))
    probs, denoms = [], []
    for u in range(n_sub):
        kp = k0s[u] + lax.broadcasted_iota(jnp.int32, (span, tq), 0)
        t = q0s[u] + lax.broadcasted_iota(jnp.int32, (span, tq), 1)
        diff = t - kp
        bias = jnp.where((diff >= 0) & (diff < WINDOW), 0.0, -jnp.inf)
        s = jnp.concatenate([scores[u][:, h * tq:(h + 1) * tq] + bias for h in range(hg)], axis=1)
        e = jnp.exp2(s - jnp.max(s, axis=0, keepdims=True))
        denoms.append(jnp.sum(e, axis=0, keepdims=True))
        probs.append(e.astype(bf16))
    gates = gate_ref[0]
    lane = lax.broadcasted_iota(jnp.int32, (tq, LANES), 1)
    for u in range(n_sub):
        jb = k0s[u] // tq
        vt = jnp.concatenate([vt_ref[0, 0, jb + c] for c in range(span // tq)], axis=1)
        out = _dot(vt, probs[u]) / denoms[u]
        for h in range(hg):
            gcol = FOX_HEADS + (g * hg + h) * N_BRANCH + 2
            gh = jnp.sum(jnp.where(lane == gcol, gates[u * tq:(u + 1) * tq], 0.0), axis=-1, keepdims=True)
            o_ref[0, u * tq:(u + 1) * tq, h * HEAD_DIM:(h + 1) * HEAD_DIM] = (
                gh * out[:, h * tq:(h + 1) * tq].T).astype(bf16)


def _win(nqr, kw, vwt, gates):
    n_batch, _, seq_len, _ = nqr.shape
    tq, n_sub = WIN_TILE, 4
    assert seq_len >= WINDOW + tq and WINDOW % tq == 0
    return pl.pallas_call(
        functools.partial(_win_kernel, tq=tq, n_sub=n_sub),
        out_shape=jax.ShapeDtypeStruct((n_batch, seq_len, NSA_W), bf16),
        grid=(n_batch, NSA_KV_HEADS, seq_len // (tq * n_sub)),
        in_specs=[
            pl.BlockSpec((1, NSA_GROUP, tq * n_sub, HEAD_DIM), lambda b, g, i: (b, g, i, 0)),
            pl.BlockSpec((1, 1, seq_len, HEAD_DIM), lambda b, g, i: (b, g, 0, 0)),
            pl.BlockSpec((1, 1, seq_len // tq, HEAD_DIM, tq), lambda b, g, i: (b, g, 0, 0, 0)),
            pl.BlockSpec((1, tq * n_sub, LANES), lambda b, g, i: (b, i, 0)),
        ],
        out_specs=pl.BlockSpec((1, tq * n_sub, NSA_GROUP * HEAD_DIM), lambda b, g, i: (b, i, g)),
        compiler_params=_params("arbitrary", "arbitrary", "arbitrary"),
        name="win",
    )(nqr, kw, vwt, gates)


def _outproj_kernel(fox_ref, c_ref, s_ref, w_ref, x_ref, mod_ref, wo_ref, o_ref):
    nsa = c_ref[...].astype(f32) + s_ref[...].astype(f32) + w_ref[...].astype(f32)
    a = jnp.concatenate([fox_ref[...], nsa.astype(bf16)], axis=1)
    o_ref[...] = x_ref[...] + mod_ref[0][2:3] * _dot(a, wo_ref[...])


def _outproj(ofox, ocmp, oslc, owin, x2d, mod3, wo_bf16, seq_len):
    m_rows = x2d.shape[0]
    tm = 512
    per_b = seq_len // tm
    half = pl.BlockSpec((tm, FOX_W), lambda i: (i, 0))
    return pl.pallas_call(
        _outproj_kernel,
        out_shape=jax.ShapeDtypeStruct((m_rows, D_MODEL), f32),
        grid=(m_rows // tm,),
        in_specs=[
            half, half, half, half,
            pl.BlockSpec((tm, D_MODEL), lambda i: (i, 0)),
            pl.BlockSpec((1, 6, D_MODEL), lambda i: (i // per_b, 0, 0)),
            pl.BlockSpec((D_MODEL, D_MODEL), lambda i: (0, 0)),
        ],
        out_specs=pl.BlockSpec((tm, D_MODEL), lambda i: (i, 0)),
        compiler_params=_params("arbitrary"),
        name="outproj",
    )(ofox, ocmp, oslc, owin, x2d, mod3, wo_bf16)


def _mlp_kernel(x_ref, mod_ref, g_ref, wu_ref, wd_ref, o_ref, h_ref, acc_ref):
    f = pl.program_id(1)

    @pl.when(f == 0)
    def _():
        md = mod_ref[0]
        y = _rms(x_ref[...], g_ref[...])
        h_ref[...] = (y * (1.0 + md[4:5]) + md[3:4]).astype(bf16)
        acc_ref[...] = jnp.zeros_like(acc_ref)

    u = jnp.maximum(_dot(h_ref[...], wu_ref[...]), 0.0)
    acc_ref[...] += _dot((u * u).astype(bf16), wd_ref[...])

    @pl.when(f == pl.num_programs(1) - 1)
    def _():
        o_ref[...] = x_ref[...] + mod_ref[0][5:6] * acc_ref[...]


def _mlp(x2d, mod3, norm_g, wu_bf16, wd_bf16, seq_len):
    m_rows = x2d.shape[0]
    tm, tf = 512, 1024
    per_b = seq_len // tm
    return pl.pallas_call(
        _mlp_kernel,
        out_shape=jax.ShapeDtypeStruct((m_rows, D_MODEL), f32),
        grid=(m_rows // tm, D_FF // tf),
        in_specs=[
            pl.BlockSpec((tm, D_MODEL), lambda i, f: (i, 0)),
            pl.BlockSpec((1, 6, D_MODEL), lambda i, f: (i // per_b, 0, 0)),
            pl.BlockSpec((1, D_MODEL), lambda i, f: (0, 0)),
            pl.BlockSpec((D_MODEL, tf), lambda i, f: (0, f)),
            pl.BlockSpec((tf, D_MODEL), lambda i, f: (f, 0)),
        ],
        out_specs=pl.BlockSpec((tm, D_MODEL), lambda i, f: (i, 0)),
        scratch_shapes=[pltpu.VMEM((tm, D_MODEL), bf16), pltpu.VMEM((tm, D_MODEL), f32)],
        compiler_params=_params("arbitrary", "arbitrary"),
        name="mlp",
    )(x2d, mod3, norm_g, wu_bf16, wd_bf16)


def _layer(x, c, w_ada, b_ada, norm1_g, w_in, b_forget, fox_q_norm, fox_k_norm, nsa_q_norm,
           cmp_k_norm, slc_k_norm, win_k_norm, cmp_pe_k, cmp_w1_k, cmp_w2_k, cmp_pe_v, cmp_w1_v,
           cmp_w2_v, w_out, norm2_g, w_up, w_down):
    n_batch, seq_len, _ = x.shape
    n_cmp = (seq_len - CMP_LEN) // CMP_STRIDE + 1
    row = lambda v: v.reshape(1, -1)

    half = HEAD_DIM // 2
    inv_freq = ROPE_THETA ** (-jnp.arange(half, dtype=f32) / half)
    inv_freq = jnp.concatenate([inv_freq, inv_freq]).reshape(1, HEAD_DIM)

    w_t = w_in.T
    w_a = w_t[:W_IN_Z0].astype(bf16)
    w_b = w_t[W_IN_NQ0:W_IN_GZ0].astype(bf16)
    w_s = jnp.concatenate([w_t[W_IN_Z0:W_IN_NQ0], w_t[W_IN_GZ0:],
                           jnp.zeros((LANES - FOX_HEADS - N_BRANCH * NSA_HEADS, D_MODEL), w_t.dtype)],
                          axis=0).astype(bf16)

    mod3 = _ada(c, w_ada, b_ada).reshape(n_batch, 6, D_MODEL)
    x2d = x.reshape(n_batch * seq_len, D_MODEL)
    proj, small = _proj(x2d, mod3, row(norm1_g), w_a, w_b, w_s, seq_len)
    (fq, fk, fvt, cin, nqu, nqr, kc, vc, kaug, vst, kw, vw, gates) = _prep(
        proj, small, n_batch, seq_len, row(fox_q_norm), row(fox_k_norm), row(nsa_q_norm), row(slc_k_norm),
        row(win_k_norm), jnp.pad(b_forget, (0, LANES - FOX_HEADS)).reshape(1, LANES), inv_freq)
    kcmp = _compress(kc, cmp_pe_k, cmp_w1_k.astype(bf16), cmp_w2_k.astype(bf16), row(cmp_k_norm), True)
    vcmp = _compress(vc, cmp_pe_v, cmp_w1_v.astype(bf16), cmp_w2_v.astype(bf16), row(cmp_k_norm), False)
    ofox = _fox(fq, fk, fvt, cin)
    ocmp, selneg = _cmp(nqu, kcmp, vcmp, gates, n_cmp)
    oslc, w_up16, w_down16, w_out16 = _slc(nqr, selneg, kaug, vst, gates, [w_up, w_down, w_out])
    owin = _win(nqr, kw, vw, gates)
    x1 = _outproj(ofox.reshape(-1, FOX_W), ocmp.reshape(-1, NSA_W), oslc.reshape(-1, NSA_W),
                  owin.reshape(-1, NSA_W), x2d, mod3, w_out16, seq_len)
    x2 = _mlp(x1, mod3, row(norm2_g), w_up16, w_down16, seq_len)
    return x2.reshape(n_batch, seq_len, D_MODEL)


def kernel(x, c, w_ada, b_ada, norm1_g, w_in, b_forget, fox_q_norm, fox_k_norm, nsa_q_norm, cmp_k_norm,
           slc_k_norm, win_k_norm, cmp_pe_k, cmp_w1_k, cmp_w2_k, cmp_pe_v, cmp_w1_v, cmp_w2_v, w_out,
           norm2_g, w_up, w_down):
    depth = w_ada.shape[0]
    for l in range(depth):
        x = _layer(x, c, w_ada[l], b_ada[l], norm1_g[l], w_in[l], b_forget[l], fox_q_norm[l], fox_k_norm[l],
                   nsa_q_norm[l], cmp_k_norm[l], slc_k_norm[l], win_k_norm[l], cmp_pe_k[l], cmp_w1_k[l],
                   cmp_w2_k[l], cmp_pe_v[l], cmp_w1_v[l], cmp_w2_v[l], w_out[l], norm2_g[l], w_up[l],
                   w_down[l])
    return x
```

```python
import functools
import math
from typing import Any, Callable, NamedTuple

import numpy as np
import jax
import jax.numpy as jnp
from jax import lax
from jax.experimental import pallas as pl
from jax.experimental.pallas import tpu as pltpu

D_MODEL = 2048
HEAD_DIM = 128
FOX_HEADS = 8
NSA_HEADS = 8
NSA_KV_HEADS = 2
NSA_GROUP = NSA_HEADS // NSA_KV_HEADS
N_BRANCH = 3
D_FF = 4 * D_MODEL
ROPE_THETA = 10000.0
CMP_LEN = 32
CMP_STRIDE = 16
CMP_SHIFT = 4
CMP_HIDDEN = 2 * HEAD_DIM
SEL_BLOCK = 64
SEL_SHIFT = 6
SEL_TOPK = 16
WINDOW = 512
NORM_EPS = 1e-6
ATTN_SCALE = HEAD_DIM ** -0.5
FOX_W = FOX_HEADS * HEAD_DIM
NSA_W = NSA_HEADS * HEAD_DIM
KV_W = NSA_KV_HEADS * HEAD_DIM

LANES = 128
SEL_LANES = LANES
MASK_BIG = 1e30
KV_BLOCK = 512
WIN_TILE = 128
LOG2E = math.log2(math.e)

COL_FQ = 0
COL_FK = COL_FQ + FOX_W
COL_FV = COL_FK + FOX_W
COL_NQ = COL_FV + FOX_W
COL_KC = COL_NQ + NSA_W
COL_VC = COL_KC + KV_W
COL_KS = COL_VC + KV_W
COL_VS = COL_KS + KV_W
COL_KW = COL_VS + KV_W
COL_VW = COL_KW + KV_W
COL_SMALL = COL_VW + KV_W
W_IN_Z0 = 3 * FOX_W
W_IN_NQ0 = W_IN_Z0 + FOX_HEADS
W_IN_GZ0 = W_IN_NQ0 + NSA_W + 6 * KV_W

VMEM_LIMIT = 56 * 1024 * 1024

f32 = jnp.float32
bf16 = jnp.bfloat16


def _params(*sem):
    return pltpu.CompilerParams(dimension_semantics=sem, vmem_limit_bytes=VMEM_LIMIT)


def _dot_nt(a, b):
    return lax.dot_general(a, b, (((1,), (1,)), ((), ())), preferred_element_type=f32)


def _dot(a, b):
    return jnp.dot(a, b, preferred_element_type=f32)


def _split3(x):
    hi = x.astype(bf16)
    r1 = x - hi.astype(f32)
    mid = r1.astype(bf16)
    lo = (r1 - mid.astype(f32)).astype(bf16)
    return hi, mid, lo


def _dot_f32_by_exact(x, w_bf16):
    hi, mid, lo = _split3(x)
    return _dot(hi, w_bf16) + (_dot(mid, w_bf16) + _dot(lo, w_bf16))


def _rms(x, gain):
    ms = jnp.mean(x * x, axis=-1, keepdims=True)
    return x * lax.rsqrt(ms + NORM_EPS) * gain


def _ada_kernel(ct_ref, w_ref, b_ref, o_ref, *, n_batch, k_chunk):
    ct = ct_ref[...]
    act = ct * jax.nn.sigmoid(ct)
    rows = []
    for b in range(n_batch):
        col = act[:, b:b + 1]
        acc = b_ref[...]
        for k0 in range(0, D_MODEL, k_chunk):
            acc = acc + jnp.sum(w_ref[k0:k0 + k_chunk, :] * col[k0:k0 + k_chunk], axis=0, keepdims=True)
        rows.append(acc)
    o_ref[...] = jnp.concatenate(rows, axis=0)


def _ada(c, w_ada, b_ada):
    n_batch = c.shape[0]
    n_out = w_ada.shape[1]
    tn = 1024
    return pl.pallas_call(
        functools.partial(_ada_kernel, n_batch=n_batch, k_chunk=256),
        out_shape=jax.ShapeDtypeStruct((n_batch, n_out), f32),
        grid=(n_out // tn,),
        in_specs=[
            pl.BlockSpec((D_MODEL, n_batch), lambda j: (0, 0)),
            pl.BlockSpec((D_MODEL, tn), lambda j: (0, j)),
            pl.BlockSpec((1, tn), lambda j: (0, j)),
        ],
        out_specs=pl.BlockSpec((n_batch, tn), lambda j: (0, j)),
        compiler_params=_params("arbitrary"),
        name="ada",
    )(c.T, w_ada, b_ada.reshape(1, n_out))


def _proj_kernel(x_ref, mod_ref, g_ref, wa_ref, wb_ref, ws_ref, o_ref, small_ref):
    md = mod_ref[0]
    h = (_rms(x_ref[...], g_ref[...]) * (1.0 + md[1:2]) + md[0:1]).astype(bf16)
    n_a = wa_ref.shape[0]
    o_ref[:, :n_a] = _dot_nt(h, wa_ref[...]).astype(bf16)
    o_ref[:, n_a:] = _dot_nt(h, wb_ref[...]).astype(bf16)
    small_ref[...] = _dot_nt(h, ws_ref[...])


def _cast_side_job(weights, n_steps, step_of):
    n_cast = max(c for c in range(1, n_steps + 1) if all(w.shape[0] % (16 * c) == 0 for w in weights))
    idx = lambda *ids: (jnp.minimum(step_of(*ids), n_cast - 1), 0)
    specs = [pl.BlockSpec((w.shape[0] // n_cast, w.shape[1]), idx) for w in weights]
    return specs, specs, [jax.ShapeDtypeStruct(w.shape, bf16) for w in weights]


def _proj(x2d, mod3, norm_g, wa, wb, ws, seq_len):
    m_rows = x2d.shape[0]
    tm = 512
    assert seq_len % tm == 0
    per_b = seq_len // tm
    n_main = wa.shape[0] + wb.shape[0]
    resident = lambda w: pl.BlockSpec(w.shape, lambda i: (0, 0), pipeline_mode=pl.Buffered(1))
    return pl.pallas_call(
        _proj_kernel,
        out_shape=[jax.ShapeDtypeStruct((m_rows, n_main), bf16), jax.ShapeDtypeStruct((m_rows, LANES), f32)],
        grid=(m_rows // tm,),
        in_specs=[
            pl.BlockSpec((tm, D_MODEL), lambda i: (i, 0)),
            pl.BlockSpec((1, 6, D_MODEL), lambda i: (i // per_b, 0, 0)),
            pl.BlockSpec((1, D_MODEL), lambda i: (0, 0)),
            resident(wa), resident(wb), resident(ws),
        ],
        out_specs=[pl.BlockSpec((tm, n_main), lambda i: (i, 0)), pl.BlockSpec((tm, LANES), lambda i: (i, 0))],
        compiler_params=_params("arbitrary"),
        name="proj",
    )(x2d, mod3, norm_g, wa, wb, ws)


def _prep_kernel(p_ref, small_ref, gq_ref, gk_ref, gn_ref, gs_ref, gw_ref, bf_ref, inv_ref,
                 fq_ref, fk_ref, fvt_ref, cin_ref, nqu_ref, nqr_ref, kc_ref, vc_ref,
                 ks_ref, vst_ref, kw_ref, vw_ref, gate_ref, carry_ref, rot_ref, *, tm):
    i = pl.program_id(1)

    @pl.when(i == 0)
    def _():
        carry_ref[...] = jnp.zeros_like(carry_ref)

    def head(col, h):
        return p_ref[:, col + h * HEAD_DIM: col + (h + 1) * HEAD_DIM].astype(f32)

    row = lax.broadcasted_iota(jnp.int32, (tm, LANES), 0)
    lane = lax.broadcasted_iota(jnp.int32, (tm, LANES), 1)
    pos = i * tm + row

    @pl.when((pl.program_id(0) == 0) & (i == 0))
    def _():
        ang_row = row.astype(f32) * inv_ref[...]
        rot_ref[0] = jnp.cos(ang_row)
        rot_ref[1] = jnp.sin(ang_row)

    ang0 = (i * tm).astype(f32) * inv_ref[...]
    cos0, sin0 = jnp.cos(ang0), jnp.sin(ang0)
    cos = cos0 * rot_ref[0] - sin0 * rot_ref[1]
    sin = sin0 * rot_ref[0] + cos0 * rot_ref[1]
    sin_signed = jnp.where(lane < HEAD_DIM // 2, -sin, sin)

    def rope(x):
        return x * cos + pltpu.roll(x, HEAD_DIM // 2, 1) * sin_signed

    small = small_ref[...]
    z = small + bf_ref[...]
    logf = jnp.minimum(z, 0.0) - jnp.log1p(jnp.exp(-jnp.abs(z)))
    t_idx = lax.broadcasted_iota(jnp.int32, (tm, tm), 0)
    s_idx = lax.broadcasted_iota(jnp.int32, (tm, tm), 1)
    tri = jnp.where(s_idx <= t_idx, 1.0, 0.0).astype(bf16)
    hi, mid, lo = _split3(logf)
    local = _dot(tri, hi) + (_dot(tri, mid) + _dot(tri, lo))
    cin_ref[0, 0] = carry_ref[0:1, :]
    carry_ref[...] = carry_ref[...] + local[tm - 1:tm, :]
    b_hi, b_mid, b_lo = (v.astype(f32) for v in _split3(local * (-LOG2E)))
    ones3 = jnp.where(lane < 3, 1.0, 0.0).astype(bf16)

    for h in range(FOX_HEADS):
        q = _rms(head(COL_FQ, h), gq_ref[...]) * (ATTN_SCALE * LOG2E)
        fq_ref[0, h] = jnp.concatenate([q.astype(bf16), ones3], axis=1)
        k = _rms(head(COL_FK, h), gk_ref[...])
        bias = jnp.where(lane == 0, b_hi[:, h:h + 1],
                         jnp.where(lane == 1, b_mid[:, h:h + 1],
                                   jnp.where(lane == 2, b_lo[:, h:h + 1], 0.0)))
        fk_ref[0, h] = jnp.concatenate([k.astype(bf16), bias.astype(bf16)], axis=1)
        fvt_ref[0, h, 0] = head(COL_FV, h).T.astype(bf16)

    gate_ref[0] = jax.nn.sigmoid(small)

    for h in range(NSA_HEADS):
        qn = _rms(head(COL_NQ, h), gn_ref[...])
        nqu_ref[0, h] = (qn * ATTN_SCALE).astype(bf16)
        nqr_ref[0, h] = rope(qn * (ATTN_SCALE * LOG2E)).astype(bf16)
    onehot = jnp.where(lane == lax.shift_right_logical(pos, SEL_SHIFT), 1.0, 0.0).astype(bf16)
    for g in range(NSA_KV_HEADS):
        kc_ref[0, g] = head(COL_KC, g).astype(bf16)
        vc_ref[0, g] = head(COL_VC, g).astype(bf16)
        ks = rope(_rms(head(COL_KS, g), gs_ref[...])).astype(bf16)
        ks_ref[0, g] = jnp.concatenate([ks, onehot], axis=1)
        vst_ref[0, g, 0] = head(COL_VS, g).T.astype(bf16)
        kw_ref[0, g] = rope(_rms(head(COL_KW, g), gw_ref[...])).astype(bf16)
        vwt = head(COL_VW, g).T.astype(bf16)
        for c in range(tm // WIN_TILE):
            vw_ref[0, g, c] = vwt[:, c * WIN_TILE:(c + 1) * WIN_TILE]


def _prep(proj, small, n_batch, seq_len, gq, gk, gn, gs, gw, b_forget_row, inv_freq):
    tm = KV_BLOCK
    per_b = seq_len // tm
    hshape = lambda n, w, dt: jax.ShapeDtypeStruct((n_batch, n, seq_len, w), dt)
    hspec = lambda n, w: pl.BlockSpec((1, n, tm, w), lambda b, i: (b, 0, i, 0))
    tshape = lambda n: jax.ShapeDtypeStruct((n_batch, n, per_b, HEAD_DIM, tm), bf16)
    tspec = lambda n: pl.BlockSpec((1, n, 1, HEAD_DIM, tm), lambda b, i: (b, 0, i, 0, 0))
    vec = pl.BlockSpec((1, LANES), lambda b, i: (0, 0))
    return pl.pallas_call(
        functools.partial(_prep_kernel, tm=tm),
        out_shape=[
            hshape(FOX_HEADS, 2 * HEAD_DIM, bf16), hshape(FOX_HEADS, 2 * HEAD_DIM, bf16), tshape(FOX_HEADS),
            jax.ShapeDtypeStruct((n_batch, per_b, 1, LANES), f32),
            hshape(NSA_HEADS, HEAD_DIM, bf16), hshape(NSA_HEADS, HEAD_DIM, bf16),
            hshape(NSA_KV_HEADS, HEAD_DIM, bf16), hshape(NSA_KV_HEADS, HEAD_DIM, bf16),
            hshape(NSA_KV_HEADS, HEAD_DIM + SEL_LANES, bf16), tshape(NSA_KV_HEADS),
            hshape(NSA_KV_HEADS, HEAD_DIM, bf16),
            jax.ShapeDtypeStruct((n_batch, NSA_KV_HEADS, seq_len // WIN_TILE, HEAD_DIM, WIN_TILE), bf16),
            jax.ShapeDtypeStruct((n_batch, seq_len, LANES), f32),
        ],
        grid=(n_batch, per_b),
        in_specs=[
            pl.BlockSpec((tm, COL_SMALL), lambda b, i: (b * per_b + i, 0)),
            pl.BlockSpec((tm, LANES), lambda b, i: (b * per_b + i, 0)),
            vec, vec, vec, vec, vec, vec, vec,
        ],
        out_specs=[
            hspec(FOX_HEADS, 2 * HEAD_DIM), hspec(FOX_HEADS, 2 * HEAD_DIM), tspec(FOX_HEADS),
            pl.BlockSpec((1, 1, 1, LANES), lambda b, i: (b, i, 0, 0)),
            hspec(NSA_HEADS, HEAD_DIM), hspec(NSA_HEADS, HEAD_DIM),
            hspec(NSA_KV_HEADS, HEAD_DIM), hspec(NSA_KV_HEADS, HEAD_DIM),
            hspec(NSA_KV_HEADS, HEAD_DIM + SEL_LANES), tspec(NSA_KV_HEADS),
            hspec(NSA_KV_HEADS, HEAD_DIM),
            pl.BlockSpec((1, NSA_KV_HEADS, tm // WIN_TILE, HEAD_DIM, WIN_TILE), lambda b, i: (b, 0, i, 0, 0)),
            pl.BlockSpec((1, tm, LANES), lambda b, i: (b, i, 0)),
        ],
        scratch_shapes=[pltpu.VMEM((8, LANES), f32), pltpu.VMEM((2, tm, LANES), f32)],
        compiler_params=_params("arbitrary", "arbitrary"),
        name="prep",
    )(proj, small, gq, gk, gn, gs, gw, b_forget_row, inv_freq)


def _compress_kernel(x_ref, pe_ref, w1_ref, w2_ref, g_ref, o_ref, *, n_rows, do_norm):
    half = CMP_STRIDE * HEAD_DIM
    x = x_ref[0, 0]
    pe = pe_ref[...]
    xa = (x + pe[:, :half]).astype(bf16)
    xb = (x + pe[:, half:]).astype(bf16)
    a = _dot(xa, w1_ref[0:half, :])
    b = _dot(xb, w1_ref[half:2 * half, :])
    pre = a + pltpu.roll(b, n_rows - 1, 0)
    hid = pre * jax.nn.sigmoid(pre)
    out = _dot(hid.astype(bf16), w2_ref[...])
    if do_norm:
        out = _rms(out, g_ref[...])
    o_ref[0, 0] = out.astype(bf16)


def _compress(x4, pe, w1_bf16, w2_bf16, gain, do_norm):
    n_batch, n_g, seq_len, _ = x4.shape
    n_rows = seq_len // CMP_STRIDE
    half = CMP_STRIDE * HEAD_DIM
    xr = x4.reshape(n_batch, n_g, n_rows, half)
    return pl.pallas_call(
        functools.partial(_compress_kernel, n_rows=n_rows, do_norm=do_norm),
        out_shape=jax.ShapeDtypeStruct((n_batch, n_g, n_rows, HEAD_DIM), bf16),
        grid=(n_batch, n_g),
        in_specs=[
            pl.BlockSpec((1, 1, n_rows, half), lambda b, g: (b, g, 0, 0)),
            pl.BlockSpec((1, 2 * half), lambda b, g: (0, 0)),
            pl.BlockSpec((2 * half, CMP_HIDDEN), lambda b, g: (0, 0)),
            pl.BlockSpec((CMP_HIDDEN, HEAD_DIM), lambda b, g: (0, 0)),
            pl.BlockSpec((1, HEAD_DIM), lambda b, g: (0, 0)),
        ],
        out_specs=pl.BlockSpec((1, 1, n_rows, HEAD_DIM), lambda b, g: (b, g, 0, 0)),
        compiler_params=_params("arbitrary", "arbitrary"),
        name="compress",
    )(xr, pe.reshape(1, 2 * half), w1_bf16, w2_bf16, gain)


def _softmax_block(s, d, m_prev, l_prev):
    m_blk = jnp.max(s, axis=0, keepdims=True)
    m_new = jnp.maximum(m_prev, m_blk if d is None else m_blk + d)
    p = jnp.exp2(s - (m_new if d is None else m_new - d))
    alpha = jnp.exp2(m_prev - m_new)
    l_new = alpha * l_prev + jnp.sum(p, axis=0, keepdims=True)
    return m_new, l_new, alpha, p.astype(bf16)


class _Chain(NamedTuple):
    scores: Callable
    values: Callable
    offset: Callable
    diag_mask: Callable
    emit: Callable
    next_first: Callable
    s_buf: Any
    p_buf: Any


def _flash_sweep(n_full, chains, n_q, is_first):
    @pl.when(is_first)
    def _():
        for c in chains:
            c.s_buf[0] = c.scores(0)

    for c in chains:
        c.p_buf[1] = jnp.zeros(c.p_buf.shape[1:], c.p_buf.dtype)

    def accumulate(c, j, slot, alpha, acc):
        return alpha * acc + _dot(c.values(jnp.maximum(j, 0)), c.p_buf[slot])

    def step(j, cur, states):
        out = []
        for c, (m, l, acc, alpha_prev) in zip(chains, states):
            acc = accumulate(c, j - 1, 1 - cur, alpha_prev, acc)
            m, l, alpha, p = _softmax_block(c.s_buf[cur], c.offset(j), m, l)
            c.p_buf[cur] = p
            c.s_buf[1 - cur] = c.scores(j + 1)
            out.append((m, l, acc, alpha))
        return tuple(out)

    def finish(cur, states):
        nxt = [c.next_first() for c in chains]
        for c, s_next, (m, l, acc, alpha_prev) in zip(chains, nxt, states):
            if cur == 1:
                c.s_buf[0] = s_next
            acc = accumulate(c, n_full - 1, 1 - cur, alpha_prev, acc)
            m, l, alpha, p = _softmax_block(c.diag_mask(c.s_buf[cur]), None, m, l)
            if cur == 0:
                c.s_buf[0] = s_next
            acc = alpha * acc + _dot(c.values(n_full), p)
            c.emit(acc / l)

    init = tuple((jnp.full((1, n_q), -jnp.inf, f32), jnp.zeros((1, n_q), f32),
                  jnp.zeros((HEAD_DIM, n_q), f32), jnp.ones((1, n_q), f32)) for _ in chains)
    states = lax.fori_loop(0, n_full // 2, lambda jj, st: step(2 * jj + 1, 1, step(2 * jj, 0, st)), init)

    @pl.when(n_full % 2 == 0)
    def _():
        finish(0, states)

    @pl.when(n_full % 2 == 1)
    def _():
        finish(1, step(n_full - 1, 0, states))


def _fox_kernel(q_ref, qn_ref, k_ref, vt_ref, c_ref, o_ref, *bufs, tq, n_heads):
    hb = pl.program_id(1)
    i = pl.program_id(2)
    lane = lax.broadcasted_iota(jnp.int32, (1, LANES), 1)

    def diag_mask(s):
        kk = lax.broadcasted_iota(jnp.int32, (tq, tq), 0)
        qq = lax.broadcasted_iota(jnp.int32, (tq, tq), 1)
        return jnp.where(kk <= qq, s, -jnp.inf)

    def chain(u):
        qa = q_ref[0, u]

        def cin(j):
            return jnp.sum(jnp.where(lane == hb * n_heads + u, c_ref[0, j], 0.0), axis=-1, keepdims=True)

        ci = cin(i)

        def scores(j):
            return _dot_nt(k_ref[0, u, pl.ds(pl.multiple_of(j * tq, tq), tq), :], qa)

        def emit(out_t):
            o_ref[0, :, u * HEAD_DIM:(u + 1) * HEAD_DIM] = out_t.T.astype(bf16)

        def next_first():
            return _dot_nt(k_ref[0, u, 0:tq, :], qn_ref[0, u])

        return _Chain(scores, lambda j: vt_ref[0, u, j], lambda j: (ci - cin(j)) * LOG2E, diag_mask, emit,
                      next_first, bufs[2 * u], bufs[2 * u + 1])

    _flash_sweep(i, [chain(u) for u in range(n_heads)], tq, i == 0)


def _fox(fq, fk, fvt, cin):
    n_batch, n_h, seq_len, _ = fq.shape
    tq = KV_BLOCK
    nkb = seq_len // tq
    n_heads = 4
    return pl.pallas_call(
        functools.partial(_fox_kernel, tq=tq, n_heads=n_heads),
        out_shape=jax.ShapeDtypeStruct((n_batch, seq_len, n_h * HEAD_DIM), bf16),
        grid=(n_batch, n_h // n_heads, nkb),
        in_specs=[
            pl.BlockSpec((1, n_heads, tq, 2 * HEAD_DIM), lambda b, h, i: (b, h, i, 0)),
            pl.BlockSpec((1, n_heads, tq, 2 * HEAD_DIM), lambda b, h, i: (b, h, jnp.minimum(i + 1, nkb - 1), 0)),
            pl.BlockSpec((1, n_heads, seq_len, 2 * HEAD_DIM), lambda b, h, i: (b, h, 0, 0)),
            pl.BlockSpec((1, n_heads, nkb, HEAD_DIM, tq), lambda b, h, i: (b, h, 0, 0, 0)),
            pl.BlockSpec((1, nkb, 1, LANES), lambda b, h, i: (b, 0, 0, 0)),
        ],
        out_specs=pl.BlockSpec((1, tq, n_heads * HEAD_DIM), lambda b, h, i: (b, i, h)),
        scratch_shapes=[pltpu.VMEM((2, tq, tq), f32), pltpu.VMEM((2, tq, tq), bf16)] * n_heads,
        compiler_params=_params("arbitrary", "arbitrary", "arbitrary"),
        name="fox",
    )(fq, fq, fk, fvt, cin)


def _cmp_kernel(q_ref, k_ref, v_ref, gate_ref, ov_ref, o_ref, sel_ref, *, tq, n_rows, n_cmp, n_sel):
    i = pl.program_id(2)
    g = pl.program_id(1)
    hg = NSA_GROUP
    q = q_ref[0].reshape(hg * tq, HEAD_DIM)
    s = _dot_nt(q, k_ref[0, 0]).reshape(hg, tq, n_rows)
    t = i * tq + lax.broadcasted_iota(jnp.int32, (1, tq, 1), 1)
    n = lax.broadcasted_iota(jnp.int32, (1, 1, n_rows), 2)
    last = jnp.minimum(lax.shift_right_arithmetic(t - (CMP_LEN - 1), CMP_SHIFT), n_cmp - 1)
    s = jnp.where(n <= last, s, -jnp.inf)
    m = jnp.max(s, axis=-1, keepdims=True)
    m = jnp.where(m == -jnp.inf, 0.0, m)
    e = jnp.exp(s - m)
    p = e * (1.0 / jnp.maximum(jnp.sum(e, axis=-1, keepdims=True), 1e-30))
    o = _dot(p.reshape(hg * tq, n_rows).astype(bf16), v_ref[0, 0])
    _write_gated(o_ref, gate_ref, o, g, 0, tq)

    lane = lax.broadcasted_iota(jnp.int32, (tq, LANES), 1)
    psum = p[0] + p[1] + p[2] + p[3]
    imp = _dot_f32_by_exact(psum, ov_ref[...])
    tq_pos = i * tq + lax.broadcasted_iota(jnp.int32, (tq, 1), 0)
    q_blk = lax.shift_right_logical(tq_pos, SEL_SHIFT)
    causal = lane <= q_blk
    forced = (lane == 0) | (lane == q_blk) | (lane == q_blk - 1)
    key = jnp.where(forced, jnp.inf, jnp.where(causal, imp, -1.0))
    key = jnp.where(lane < n_sel, key, -3.0)
    lane_f = lane.astype(f32)
    sel = jnp.zeros((tq, LANES), jnp.bool_)
    for _ in range(min(SEL_TOPK, n_sel)):
        mx = jnp.max(key, axis=-1, keepdims=True)
        first = jnp.min(jnp.where(key == mx, lane_f, float(LANES)), axis=-1, keepdims=True)
        pick = lane_f == first
        sel = sel | pick
        key = jnp.where(pick, -2.0, key)
    sel_ref[0, 0] = jnp.where(sel & causal, 0.0, -MASK_BIG).astype(bf16)


def _cmp(nqu, kcmp, vcmp, gates, n_cmp):
    n_batch, _, seq_len, _ = nqu.shape
    n_rows = kcmp.shape[2]
    n_sel = seq_len // SEL_BLOCK
    assert n_sel <= SEL_LANES
    tq = 1024
    c0 = np.arange(n_rows) * CMP_STRIDE
    s0 = np.arange(SEL_LANES) * SEL_BLOCK
    overlap = np.clip(np.minimum(c0[:, None] + CMP_LEN, s0[None, :] + SEL_BLOCK)
                      - np.maximum(c0[:, None], s0[None, :]), 0, None).astype(np.float32) / CMP_LEN
    overlap[n_cmp:, :] = 0.0
    overlap[:, n_sel:] = 0.0
    return pl.pallas_call(
        functools.partial(_cmp_kernel, tq=tq, n_rows=n_rows, n_cmp=n_cmp, n_sel=n_sel),
        out_shape=[
            jax.ShapeDtypeStruct((n_batch, seq_len, NSA_W), bf16),
            jax.ShapeDtypeStruct((n_batch, NSA_KV_HEADS, seq_len, SEL_LANES), bf16),
        ],
        grid=(n_batch, NSA_KV_HEADS, seq_len // tq),
        in_specs=[
            pl.BlockSpec((1, NSA_GROUP, tq, HEAD_DIM), lambda b, g, i: (b, g, i, 0)),
            pl.BlockSpec((1, 1, n_rows, HEAD_DIM), lambda b, g, i: (b, g, 0, 0)),
            pl.BlockSpec((1, 1, n_rows, HEAD_DIM), lambda b, g, i: (b, g, 0, 0)),
            pl.BlockSpec((1, tq, LANES), lambda b, g, i: (b, i, 0)),
            pl.BlockSpec((n_rows, SEL_LANES), lambda b, g, i: (0, 0)),
        ],
        out_specs=[
            pl.BlockSpec((1, tq, NSA_GROUP * HEAD_DIM), lambda b, g, i: (b, i, g)),
            pl.BlockSpec((1, 1, tq, SEL_LANES), lambda b, g, i: (b, g, i, 0)),
        ],
        compiler_params=_params("arbitrary", "arbitrary", "arbitrary"),
        name="cmp",
    )(nqu, kcmp, vcmp, gates, jnp.asarray(overlap, dtype=bf16))


def _write_gated(o_ref, gate_ref, out, g, branch, tq):
    gates = gate_ref[0]
    lane = lax.broadcasted_iota(jnp.int32, (tq, LANES), 1)
    for h in range(NSA_GROUP):
        gcol = FOX_HEADS + (g * NSA_GROUP + h) * N_BRANCH + branch
        gh = jnp.sum(jnp.where(lane == gcol, gates, 0.0), axis=-1, keepdims=True)
        o_ref[0, :, h * HEAD_DIM:(h + 1) * HEAD_DIM] = (gh * out[h * tq:(h + 1) * tq]).astype(bf16)


def _slc_kernel(q_ref, sel_ref, qn_ref, seln_ref, k_ref, vt_ref, gate_ref, *rest, tq, tk, n_cast_w):
    o_ref = rest[n_cast_w]
    bufs = rest[2 * n_cast_w + 1:]
    for w32_ref, w16_ref in zip(rest[:n_cast_w], rest[n_cast_w + 1:2 * n_cast_w + 1]):
        w16_ref[...] = w32_ref[...].astype(bf16)
    i = pl.program_id(1)
    hg = NSA_GROUP
    n_q = hg * tq
    q0 = i * tq
    jd = q0 // tk

    def diag_mask(s):
        kp = jd * tk + lax.broadcasted_iota(jnp.int32, (tk, n_q), 0)
        t = q0 + (lax.broadcasted_iota(jnp.int32, (tk, n_q), 1) & (tq - 1))
        return jnp.where(kp <= t, s, -jnp.inf)

    def chain(g):
        def q_aug(qr, sr):
            q = qr[0, g * hg:(g + 1) * hg].reshape(n_q, HEAD_DIM)
            return jnp.concatenate([q, jnp.concatenate([sr[0, g]] * hg, axis=0)], axis=1)

        qa = q_aug(q_ref, sel_ref)

        def scores(j):
            return _dot_nt(k_ref[0, g, pl.ds(pl.multiple_of(j * tk, tk), tk), :], qa)

        def next_first():
            return _dot_nt(k_ref[0, g, 0:tk, :], q_aug(qn_ref, seln_ref))

        def emit(out):
            gates = gate_ref[0]
            lane = lax.broadcasted_iota(jnp.int32, (tq, LANES), 1)
            for h in range(hg):
                head = g * hg + h
                gh = jnp.sum(jnp.where(lane == FOX_HEADS + head * N_BRANCH + 1, gates, 0.0),
                             axis=-1, keepdims=True)
                o_ref[0, :, head * HEAD_DIM:(head + 1) * HEAD_DIM] = (
                    gh * out[:, h * tq:(h + 1) * tq].T).astype(bf16)

        return _Chain(scores, lambda j: vt_ref[0, g, j], lambda j: None, diag_mask, emit, next_first,
                      bufs[2 * g], bufs[2 * g + 1])

    _flash_sweep(jd, [chain(g) for g in range(NSA_KV_HEADS)], n_q, i == 0)


def _slc(nqr, selneg, kaug, vst, gates, cast_weights):
    n_batch, _, seq_len, _ = nqr.shape
    tq, tk = 256, KV_BLOCK
    nkb = seq_len // tk
    assert tq & (tq - 1) == 0
    n_q = NSA_GROUP * tq
    n_i = seq_len // tq
    cast_in, cast_out, cast_shapes = _cast_side_job(cast_weights, n_batch * n_i, lambda b, i: b * n_i + i)
    return pl.pallas_call(
        functools.partial(_slc_kernel, tq=tq, tk=tk, n_cast_w=len(cast_weights)),
        out_shape=[jax.ShapeDtypeStruct((n_batch, seq_len, NSA_W), bf16)] + cast_shapes,
        grid=(n_batch, n_i),
        in_specs=[
            pl.BlockSpec((1, NSA_HEADS, tq, HEAD_DIM), lambda b, i: (b, 0, i, 0)),
            pl.BlockSpec((1, NSA_KV_HEADS, tq, SEL_LANES), lambda b, i: (b, 0, i, 0)),
            pl.BlockSpec((1, NSA_HEADS, tq, HEAD_DIM), lambda b, i: (b, 0, jnp.minimum(i + 1, n_i - 1), 0)),
            pl.BlockSpec((1, NSA_KV_HEADS, tq, SEL_LANES), lambda b, i: (b, 0, jnp.minimum(i + 1, n_i - 1), 0)),
            pl.BlockSpec((1, NSA_KV_HEADS, seq_len, HEAD_DIM + SEL_LANES), lambda b, i: (b, 0, 0, 0)),
            pl.BlockSpec((1, NSA_KV_HEADS, nkb, HEAD_DIM, tk), lambda b, i: (b, 0, 0, 0, 0)),
            pl.BlockSpec((1, tq, LANES), lambda b, i: (b, i, 0)),
        ] + cast_in,
        out_specs=[pl.BlockSpec((1, tq, NSA_W), lambda b, i: (b, i, 0))] + cast_out,
        scratch_shapes=[pltpu.VMEM((2, tk, n_q), f32), pltpu.VMEM((2, tk, n_q), bf16)] * NSA_KV_HEADS,
        compiler_params=_params("arbitrary", "arbitrary"),
        name="slc",
    )(nqr, selneg, nqr, selneg, kaug, vst, gates, *cast_weights)


def _win_kernel(q_ref, k_ref, vt_ref, gate_ref, o_ref, *, tq, n_sub):
    i = pl.program_id(2)
    g = pl.program_id(1)
    hg = NSA_GROUP
    span = WINDOW + tq
    q0s = [(i * n_sub + u) * tq for u in range(n_sub)]
    k0s = [pl.multiple_of(jnp.maximum(q0 - WINDOW, 0), tq) for q0 in q0s]
    scores = []
    for u in range(n_sub):
        q = q_ref[0, :, u * tq:(u + 1) * tq, :].reshape(hg * tq, HEAD_DIM)
        scores.append(_dot_nt(k_ref[0, 0, pl.ds(k0s[u], span), :], q))
    probs, denoms = [], []
    for u in range(n_sub):
        kp = k0s[u] + lax.broadcasted_iota(jnp.int32, (span, tq), 0)
        t = q0s[u] + lax.broadcasted_iota(jnp.int32, (span, tq), 1)
        diff = t - kp
        bias = jnp.where((diff >= 0) & (diff < WINDOW), 0.0, -jnp.inf)
        s = jnp.concatenate([scores[u][:, h * tq:(h + 1) * tq] + bias for h in range(hg)], axis=1)
        e = jnp.exp2(s - jnp.max(s, axis=0, keepdims=True))
        denoms.append(jnp.sum(e, axis=0, keepdims=True))
        probs.append(e.astype(bf16))
    gates = gate_ref[0]
    lane = lax.broadcasted_iota(jnp.int32, (tq, LANES), 1)
    for u in range(n_sub):
        jb = k0s[u] // tq
        vt = jnp.concatenate([vt_ref[0, 0, jb + c] for c in range(span // tq)], axis=1)
        out = _dot(vt, probs[u]) / denoms[u]
        for h in range(hg):
            gcol = FOX_HEADS + (g * hg + h) * N_BRANCH + 2
            gh = jnp.sum(jnp.where(lane == gcol, gates[u * tq:(u + 1) * tq], 0.0), axis=-1, keepdims=True)
            o_ref[0, u * tq:(u + 1) * tq, h * HEAD_DIM:(h + 1) * HEAD_DIM] = (
                gh * out[:, h * tq:(h + 1) * tq].T).astype(bf16)


def _win(nqr, kw, vwt, gates):
    n_batch, _, seq_len, _ = nqr.shape
    tq, n_sub = WIN_TILE, 4
    assert seq_len >= WINDOW + tq and WINDOW % tq == 0
    return pl.pallas_call(
        functools.partial(_win_kernel, tq=tq, n_sub=n_sub),
        out_shape=jax.ShapeDtypeStruct((n_batch, seq_len, NSA_W), bf16),
        grid=(n_batch, NSA_KV_HEADS, seq_len // (tq * n_sub)),
        in_specs=[
            pl.BlockSpec((1, NSA_GROUP, tq * n_sub, HEAD_DIM), lambda b, g, i: (b, g, i, 0)),
            pl.BlockSpec((1, 1, seq_len, HEAD_DIM), lambda b, g, i: (b, g, 0, 0)),
            pl.BlockSpec((1, 1, seq_len // tq, HEAD_DIM, tq), lambda b, g, i: (b, g, 0, 0, 0)),
            pl.BlockSpec((1, tq * n_sub, LANES), lambda b, g, i: (b, i, 0)),
        ],
        out_specs=pl.BlockSpec((1, tq * n_sub, NSA_GROUP * HEAD_DIM), lambda b, g, i: (b, i, g)),
        compiler_params=_params("arbitrary", "arbitrary", "arbitrary"),
        name="win",
    )(nqr, kw, vwt, gates)


def _outproj_kernel(fox_ref, c_ref, s_ref, w_ref, x_ref, mod_ref, wo_ref, o_ref):
    nsa = c_ref[...].astype(f32) + s_ref[...].astype(f32) + w_ref[...].astype(f32)
    a = jnp.concatenate([fox_ref[...], nsa.astype(bf16)], axis=1)
    o_ref[...] = x_ref[...] + mod_ref[0][2:3] * _dot(a, wo_ref[...])


def _outproj(ofox, ocmp, oslc, owin, x2d, mod3, wo_bf16, seq_len):
    m_rows = x2d.shape[0]
    tm = 512
    per_b = seq_len // tm
    half = pl.BlockSpec((tm, FOX_W), lambda i: (i, 0))
    return pl.pallas_call(
        _outproj_kernel,
        out_shape=jax.ShapeDtypeStruct((m_rows, D_MODEL), f32),
        grid=(m_rows // tm,),
        in_specs=[
            half, half, half, half,
            pl.BlockSpec((tm, D_MODEL), lambda i: (i, 0)),
            pl.BlockSpec((1, 6, D_MODEL), lambda i: (i // per_b, 0, 0)),
            pl.BlockSpec((D_MODEL, D_MODEL), lambda i: (0, 0)),
        ],
        out_specs=pl.BlockSpec((tm, D_MODEL), lambda i: (i, 0)),
        compiler_params=_params("arbitrary"),
        name="outproj",
    )(ofox, ocmp, oslc, owin, x2d, mod3, wo_bf16)


def _mlp_kernel(x_ref, mod_ref, g_ref, wu_ref, wd_ref, o_ref, h_ref, acc_ref):
    f = pl.program_id(1)

    @pl.when(f == 0)
    def _():
        md = mod_ref[0]
        y = _rms(x_ref[...], g_ref[...])
        h_ref[...] = (y * (1.0 + md[4:5]) + md[3:4]).astype(bf16)
        acc_ref[...] = jnp.zeros_like(acc_ref)

    u = jnp.maximum(_dot(h_ref[...], wu_ref[...]), 0.0)
    acc_ref[...] += _dot((u * u).astype(bf16), wd_ref[...])

    @pl.when(f == pl.num_programs(1) - 1)
    def _():
        o_ref[...] = x_ref[...] + mod_ref[0][5:6] * acc_ref[...]


def _mlp(x2d, mod3, norm_g, wu_bf16, wd_bf16, seq_len):
    m_rows = x2d.shape[0]
    tm, tf = 512, 1024
    per_b = seq_len // tm
    return pl.pallas_call(
        _mlp_kernel,
        out_shape=jax.ShapeDtypeStruct((m_rows, D_MODEL), f32),
        grid=(m_rows // tm, D_FF // tf),
        in_specs=[
            pl.BlockSpec((tm, D_MODEL), lambda i, f: (i, 0)),
            pl.BlockSpec((1, 6, D_MODEL), lambda i, f: (i // per_b, 0, 0)),
            pl.BlockSpec((1, D_MODEL), lambda i, f: (0, 0)),
            pl.BlockSpec((D_MODEL, tf), lambda i, f: (0, f)),
            pl.BlockSpec((tf, D_MODEL), lambda i, f: (f, 0)),
        ],
        out_specs=pl.BlockSpec((tm, D_MODEL), lambda i, f: (i, 0)),
        scratch_shapes=[pltpu.VMEM((tm, D_MODEL), bf16), pltpu.VMEM((tm, D_MODEL), f32)],
        compiler_params=_params("arbitrary", "arbitrary"),
        name="mlp",
    )(x2d, mod3, norm_g, wu_bf16, wd_bf16)


def _layer(x, c, w_ada, b_ada, norm1_g, w_in, b_forget, fox_q_norm, fox_k_norm, nsa_q_norm,
           cmp_k_norm, slc_k_norm, win_k_norm, cmp_pe_k, cmp_w1_k, cmp_w2_k, cmp_pe_v, cmp_w1_v,
           cmp_w2_v, w_out, norm2_g, w_up, w_down):
    n_batch, seq_len, _ = x.shape
    n_cmp = (seq_len - CMP_LEN) // CMP_STRIDE + 1
    row = lambda v: v.reshape(1, -1)

    half = HEAD_DIM // 2
    inv_freq = ROPE_THETA ** (-jnp.arange(half, dtype=f32) / half)
    inv_freq = jnp.concatenate([inv_freq, inv_freq]).reshape(1, HEAD_DIM)

    w_t = w_in.T
    w_a = w_t[:W_IN_Z0].astype(bf16)
    w_b = w_t[W_IN_NQ0:W_IN_GZ0].astype(bf16)
    w_s = jnp.concatenate([w_t[W_IN_Z0:W_IN_NQ0], w_t[W_IN_GZ0:],
                           jnp.zeros((LANES - FOX_HEADS - N_BRANCH * NSA_HEADS, D_MODEL), w_t.dtype)],
                          axis=0).astype(bf16)

    mod3 = _ada(c, w_ada, b_ada).reshape(n_batch, 6, D_MODEL)
    x2d = x.reshape(n_batch * seq_len, D_MODEL)
    proj, small = _proj(x2d, mod3, row(norm1_g), w_a, w_b, w_s, seq_len)
    (fq, fk, fvt, cin, nqu, nqr, kc, vc, kaug, vst, kw, vw, gates) = _prep(
        proj, small, n_batch, seq_len, row(fox_q_norm), row(fox_k_norm), row(nsa_q_norm), row(slc_k_norm),
        row(win_k_norm), jnp.pad(b_forget, (0, LANES - FOX_HEADS)).reshape(1, LANES), inv_freq)
    kcmp = _compress(kc, cmp_pe_k, cmp_w1_k.astype(bf16), cmp_w2_k.astype(bf16), row(cmp_k_norm), True)
    vcmp = _compress(vc, cmp_pe_v, cmp_w1_v.astype(bf16), cmp_w2_v.astype(bf16), row(cmp_k_norm), False)
    ofox = _fox(fq, fk, fvt, cin)
    ocmp, selneg = _cmp(nqu, kcmp, vcmp, gates, n_cmp)
    oslc, w_up16, w_down16, w_out16 = _slc(nqr, selneg, kaug, vst, gates, [w_up, w_down, w_out])
    owin = _win(nqr, kw, vw, gates)
    x1 = _outproj(ofox.reshape(-1, FOX_W), ocmp.reshape(-1, NSA_W), oslc.reshape(-1, NSA_W),
                  owin.reshape(-1, NSA_W), x2d, mod3, w_out16, seq_len)
    x2 = _mlp(x1, mod3, row(norm2_g), w_up16, w_down16, seq_len)
    return x2.reshape(n_batch, seq_len, D_MODEL)


def kernel(x, c, w_ada, b_ada, norm1_g, w_in, b_forget, fox_q_norm, fox_k_norm, nsa_q_norm, cmp_k_norm,
           slc_k_norm, win_k_norm, cmp_pe_k, cmp_w1_k, cmp_w2_k, cmp_pe_v, cmp_w1_v, cmp_w2_v, w_out,
           norm2_g, w_up, w_down):
    depth = w_ada.shape[0]
    for l in range(depth):
        x = _layer(x, c, w_ada[l], b_ada[l], norm1_g[l], w_in[l], b_forget[l], fox_q_norm[l], fox_k_norm[l],
                   nsa_q_norm[l], cmp_k_norm[l], slc_k_norm[l], win_k_norm[l], cmp_pe_k[l], cmp_w1_k[l],
                   cmp_w2_k[l], cmp_pe_v[l], cmp_w1_v[l], cmp_w2_v[l], w_out[l], norm2_g[l], w_up[l],
                   w_down[l])
    return x
```

```python
import functools
import math
from typing import Any, Callable, NamedTuple

import numpy as np
import jax
import jax.numpy as jnp
from jax import lax
from jax.experimental import pallas as pl
from jax.experimental.pallas import tpu as pltpu

D_MODEL = 2048
HEAD_DIM = 128
FOX_HEADS = 8
NSA_HEADS = 8
NSA_KV_HEADS = 2
NSA_GROUP = NSA_HEADS // NSA_KV_HEADS
N_BRANCH = 3
D_FF = 4 * D_MODEL
ROPE_THETA = 10000.0
CMP_LEN = 32
CMP_STRIDE = 16
CMP_SHIFT = 4
CMP_HIDDEN = 2 * HEAD_DIM
SEL_BLOCK = 64
SEL_SHIFT = 6
SEL_TOPK = 16
WINDOW = 512
NORM_EPS = 1e-6
ATTN_SCALE = HEAD_DIM ** -0.5
FOX_W = FOX_HEADS * HEAD_DIM
NSA_W = NSA_HEADS * HEAD_DIM
KV_W = NSA_KV_HEADS * HEAD_DIM

LANES = 128
SEL_LANES = LANES
MASK_BIG = 1e30
KV_BLOCK = 512
WIN_TILE = 128
LOG2E = math.log2(math.e)

COL_FQ = 0
COL_FK = COL_FQ + FOX_W
COL_FV = COL_FK + FOX_W
COL_NQ = COL_FV + FOX_W
COL_KC = COL_NQ + NSA_W
COL_VC = COL_KC + KV_W
COL_KS = COL_VC + KV_W
COL_VS = COL_KS + KV_W
COL_KW = COL_VS + KV_W
COL_VW = COL_KW + KV_W
COL_SMALL = COL_VW + KV_W
W_IN_Z0 = 3 * FOX_W
W_IN_NQ0 = W_IN_Z0 + FOX_HEADS
W_IN_GZ0 = W_IN_NQ0 + NSA_W + 6 * KV_W

VMEM_LIMIT = 56 * 1024 * 1024

f32 = jnp.float32
bf16 = jnp.bfloat16


def _params(*sem):
    return pltpu.CompilerParams(dimension_semantics=sem, vmem_limit_bytes=VMEM_LIMIT)


def _dot_nt(a, b):
    return lax.dot_general(a, b, (((1,), (1,)), ((), ())), preferred_element_type=f32)


def _dot(a, b):
    return jnp.dot(a, b, preferred_element_type=f32)


def _split3(x):
    hi = x.astype(bf16)
    r1 = x - hi.astype(f32)
    mid = r1.astype(bf16)
    lo = (r1 - mid.astype(f32)).astype(bf16)
    return hi, mid, lo


def _dot_f32_by_exact(x, w_bf16):
    hi, mid, lo = _split3(x)
    return _dot(hi, w_bf16) + (_dot(mid, w_bf16) + _dot(lo, w_bf16))


def _rms(x, gain):
    ms = jnp.mean(x * x, axis=-1, keepdims=True)
    return x * lax.rsqrt(ms + NORM_EPS) * gain


def _ada_kernel(ct_ref, w_ref, b_ref, o_ref, *, n_batch, k_chunk):
    ct = ct_ref[...]
    act = ct * jax.nn.sigmoid(ct)
    rows = []
    for b in range(n_batch):
        col = act[:, b:b + 1]
        acc = b_ref[...]
        for k0 in range(0, D_MODEL, k_chunk):
            acc = acc + jnp.sum(w_ref[k0:k0 + k_chunk, :] * col[k0:k0 + k_chunk], axis=0, keepdims=True)
        rows.append(acc)
    o_ref[...] = jnp.concatenate(rows, axis=0)


def _ada(c, w_ada, b_ada):
    n_batch = c.shape[0]
    n_out = w_ada.shape[1]
    tn = 1024
    return pl.pallas_call(
        functools.partial(_ada_kernel, n_batch=n_batch, k_chunk=256),
        out_shape=jax.ShapeDtypeStruct((n_batch, n_out), f32),
        grid=(n_out // tn,),
        in_specs=[
            pl.BlockSpec((D_MODEL, n_batch), lambda j: (0, 0)),
            pl.BlockSpec((D_MODEL, tn), lambda j: (0, j)),
            pl.BlockSpec((1, tn), lambda j: (0, j)),
        ],
        out_specs=pl.BlockSpec((n_batch, tn), lambda j: (0, j)),
        compiler_params=_params("arbitrary"),
        name="ada",
    )(c.T, w_ada, b_ada.reshape(1, n_out))


def _proj_kernel(x_ref, mod_ref, g_ref, wa_ref, wb_ref, ws_ref, o_ref, small_ref):
    md = mod_ref[0]
    h = (_rms(x_ref[...], g_ref[...]) * (1.0 + md[1:2]) + md[0:1]).astype(bf16)
    n_a = wa_ref.shape[0]
    o_ref[:, :n_a] = _dot_nt(h, wa_ref[...]).astype(bf16)
    o_ref[:, n_a:] = _dot_nt(h, wb_ref[...]).astype(bf16)
    small_ref[...] = _dot_nt(h, ws_ref[...])


def _cast_side_job(weights, n_steps, step_of):
    n_cast = max(c for c in range(1, n_steps + 1) if all(w.shape[0] % (16 * c) == 0 for w in weights))
    idx = lambda *ids: (jnp.minimum(step_of(*ids), n_cast - 1), 0)
    specs = [pl.BlockSpec((w.shape[0] // n_cast, w.shape[1]), idx) for w in weights]
    return specs, specs, [jax.ShapeDtypeStruct(w.shape, bf16) for w in weights]


def _proj(x2d, mod3, norm_g, wa, wb, ws, seq_len):
    m_rows = x2d.shape[0]
    tm = 512
    assert seq_len % tm == 0
    per_b = seq_len // tm
    n_main = wa.shape[0] + wb.shape[0]
    resident = lambda w: pl.BlockSpec(w.shape, lambda i: (0, 0), pipeline_mode=pl.Buffered(1))
    return pl.pallas_call(
        _proj_kernel,
        out_shape=[jax.ShapeDtypeStruct((m_rows, n_main), bf16), jax.ShapeDtypeStruct((m_rows, LANES), f32)],
        grid=(m_rows // tm,),
        in_specs=[
            pl.BlockSpec((tm, D_MODEL), lambda i: (i, 0)),
            pl.BlockSpec((1, 6, D_MODEL), lambda i: (i // per_b, 0, 0)),
            pl.BlockSpec((1, D_MODEL), lambda i: (0, 0)),
            resident(wa), resident(wb), resident(ws),
        ],
        out_specs=[pl.BlockSpec((tm, n_main), lambda i: (i, 0)), pl.BlockSpec((tm, LANES), lambda i: (i, 0))],
        compiler_params=_params("arbitrary"),
        name="proj",
    )(x2d, mod3, norm_g, wa, wb, ws)


def _prep_kernel(p_ref, small_ref, gq_ref, gk_ref, gn_ref, gs_ref, gw_ref, bf_ref, inv_ref,
                 fq_ref, fk_ref, fvt_ref, cin_ref, nqu_ref, nqr_ref, kc_ref, vc_ref,
                 ks_ref, vst_ref, kw_ref, vw_ref, gate_ref, carry_ref, rot_ref, *, tm):
    i = pl.program_id(1)

    @pl.when(i == 0)
    def _():
        carry_ref[...] = jnp.zeros_like(carry_ref)

    def head(col, h):
        return p_ref[:, col + h * HEAD_DIM: col + (h + 1) * HEAD_DIM].astype(f32)

    row = lax.broadcasted_iota(jnp.int32, (tm, LANES), 0)
    lane = lax.broadcasted_iota(jnp.int32, (tm, LANES), 1)
    pos = i * tm + row

    @pl.when((pl.program_id(0) == 0) & (i == 0))
    def _():
        ang_row = row.astype(f32) * inv_ref[...]
        rot_ref[0] = jnp.cos(ang_row)
        rot_ref[1] = jnp.sin(ang_row)

    ang0 = (i * tm).astype(f32) * inv_ref[...]
    cos0, sin0 = jnp.cos(ang0), jnp.sin(ang0)
    cos = cos0 * rot_ref[0] - sin0 * rot_ref[1]
    sin = sin0 * rot_ref[0] + cos0 * rot_ref[1]
    sin_signed = jnp.where(lane < HEAD_DIM // 2, -sin, sin)

    def rope(x):
        return x * cos + pltpu.roll(x, HEAD_DIM // 2, 1) * sin_signed

    small = small_ref[...]
    z = small + bf_ref[...]
    logf = jnp.minimum(z, 0.0) - jnp.log1p(jnp.exp(-jnp.abs(z)))
    t_idx = lax.broadcasted_iota(jnp.int32, (tm, tm), 0)
    s_idx = lax.broadcasted_iota(jnp.int32, (tm, tm), 1)
    tri = jnp.where(s_idx <= t_idx, 1.0, 0.0).astype(bf16)
    hi, mid, lo = _split3(logf)
    local = _dot(tri, hi) + (_dot(tri, mid) + _dot(tri, lo))
    cin_ref[0, 0] = carry_ref[0:1, :]
    carry_ref[...] = carry_ref[...] + local[tm - 1:tm, :]
    b_hi, b_mid, b_lo = (v.astype(f32) for v in _split3(local * (-LOG2E)))
    ones3 = jnp.where(lane < 3, 1.0, 0.0).astype(bf16)

    for h in range(FOX_HEADS):
        q = _rms(head(COL_FQ, h), gq_ref[...]) * (ATTN_SCALE * LOG2E)
        fq_ref[0, h] = jnp.concatenate([q.astype(bf16), ones3], axis=1)
        k = _rms(head(COL_FK, h), gk_ref[...])
        bias = jnp.where(lane == 0, b_hi[:, h:h + 1],
                         jnp.where(lane == 1, b_mid[:, h:h + 1],
                                   jnp.where(lane == 2, b_lo[:, h:h + 1], 0.0)))
        fk_ref[0, h] = jnp.concatenate([k.astype(bf16), bias.astype(bf16)], axis=1)
        fvt_ref[0, h, 0] = head(COL_FV, h).T.astype(bf16)

    gate_ref[0] = jax.nn.sigmoid(small)

    for h in range(NSA_HEADS):
        qn = _rms(head(COL_NQ, h), gn_ref[...])
        nqu_ref[0, h] = (qn * ATTN_SCALE).astype(bf16)
        nqr_ref[0, h] = rope(qn * (ATTN_SCALE * LOG2E)).astype(bf16)
    onehot = jnp.where(lane == lax.shift_right_logical(pos, SEL_SHIFT), 1.0, 0.0).astype(bf16)
    for g in range(NSA_KV_HEADS):
        kc_ref[0, g] = head(COL_KC, g)
        vc_ref[0, g] = head(COL_VC, g)
        ks = rope(_rms(head(COL_KS, g), gs_ref[...])).astype(bf16)
        ks_ref[0, g] = jnp.concatenate([ks, onehot], axis=1)
        vst_ref[0, g, 0] = head(COL_VS, g).T.astype(bf16)
        kw_ref[0, g] = rope(_rms(head(COL_KW, g), gw_ref[...])).astype(bf16)
        vwt = head(COL_VW, g).T.astype(bf16)
        for c in range(tm // WIN_TILE):
            vw_ref[0, g, c] = vwt[:, c * WIN_TILE:(c + 1) * WIN_TILE]


def _prep(proj, small, n_batch, seq_len, gq, gk, gn, gs, gw, b_forget_row, inv_freq):
    tm = KV_BLOCK
    per_b = seq_len // tm
    hshape = lambda n, w, dt: jax.ShapeDtypeStruct((n_batch, n, seq_len, w), dt)
    hspec = lambda n, w: pl.BlockSpec((1, n, tm, w), lambda b, i: (b, 0, i, 0))
    tshape = lambda n: jax.ShapeDtypeStruct((n_batch, n, per_b, HEAD_DIM, tm), bf16)
    tspec = lambda n: pl.BlockSpec((1, n, 1, HEAD_DIM, tm), lambda b, i: (b, 0, i, 0, 0))
    vec = pl.BlockSpec((1, LANES), lambda b, i: (0, 0))
    return pl.pallas_call(
        functools.partial(_prep_kernel, tm=tm),
        out_shape=[
            hshape(FOX_HEADS, 2 * HEAD_DIM, bf16), hshape(FOX_HEADS, 2 * HEAD_DIM, bf16), tshape(FOX_HEADS),
            jax.ShapeDtypeStruct((n_batch, per_b, 1, LANES), f32),
            hshape(NSA_HEADS, HEAD_DIM, bf16), hshape(NSA_HEADS, HEAD_DIM, bf16),
            hshape(NSA_KV_HEADS, HEAD_DIM, f32), hshape(NSA_KV_HEADS, HEAD_DIM, f32),
            hshape(NSA_KV_HEADS, HEAD_DIM + SEL_LANES, bf16), tshape(NSA_KV_HEADS),
            hshape(NSA_KV_HEADS, HEAD_DIM, bf16),
            jax.ShapeDtypeStruct((n_batch, NSA_KV_HEADS, seq_len // WIN_TILE, HEAD_DIM, WIN_TILE), bf16),
            jax.ShapeDtypeStruct((n_batch, seq_len, LANES), f32),
        ],
        grid=(n_batch, per_b),
        in_specs=[
            pl.BlockSpec((tm, COL_SMALL), lambda b, i: (b * per_b + i, 0)),
            pl.BlockSpec((tm, LANES), lambda b, i: (b * per_b + i, 0)),
            vec, vec, vec, vec, vec, vec, vec,
        ],
        out_specs=[
            hspec(FOX_HEADS, 2 * HEAD_DIM), hspec(FOX_HEADS, 2 * HEAD_DIM), tspec(FOX_HEADS),
            pl.BlockSpec((1, 1, 1, LANES), lambda b, i: (b, i, 0, 0)),
            hspec(NSA_HEADS, HEAD_DIM), hspec(NSA_HEADS, HEAD_DIM),
            hspec(NSA_KV_HEADS, HEAD_DIM), hspec(NSA_KV_HEADS, HEAD_DIM),
            hspec(NSA_KV_HEADS, HEAD_DIM + SEL_LANES), tspec(NSA_KV_HEADS),
            hspec(NSA_KV_HEADS, HEAD_DIM),
            pl.BlockSpec((1, NSA_KV_HEADS, tm // WIN_TILE, HEAD_DIM, WIN_TILE), lambda b, i: (b, 0, i, 0, 0)),
            pl.BlockSpec((1, tm, LANES), lambda b, i: (b, i, 0)),
        ],
        scratch_shapes=[pltpu.VMEM((8, LANES), f32), pltpu.VMEM((2, tm, LANES), f32)],
        compiler_params=_params("arbitrary", "arbitrary"),
        name="prep",
    )(proj, small, gq, gk, gn, gs, gw, b_forget_row, inv_freq)


def _compress_kernel(x_ref, pe_ref, w1_ref, w2_ref, *rest, n_rows, do_norm):
    o_ref = rest[-1]
    a = jnp.zeros((n_rows, CMP_HIDDEN), f32)
    b = jnp.zeros((n_rows, CMP_HIDDEN), f32)
    for l in range(CMP_STRIDE):
        x = x_ref[0, 0, pl.ds(l, n_rows, stride=CMP_STRIDE), :]
        a = a + _dot((x + pe_ref[l:l + 1, :]).astype(bf16), w1_ref[l * HEAD_DIM:(l + 1) * HEAD_DIM, :])
        lb = CMP_STRIDE + l
        b = b + _dot((x + pe_ref[lb:lb + 1, :]).astype(bf16), w1_ref[lb * HEAD_DIM:(lb + 1) * HEAD_DIM, :])
    pre = a + pltpu.roll(b, n_rows - 1, 0)
    hid = pre * jax.nn.sigmoid(pre)
    out = _dot(hid.astype(bf16), w2_ref[...])
    if do_norm:
        out = _rms(out, rest[0][...])
    o_ref[0, 0] = out.astype(bf16)


def _compress(x4, pe, w1_bf16, w2_bf16, gain=None):
    n_batch, n_g, seq_len, _ = x4.shape
    n_rows = seq_len // CMP_STRIDE
    do_norm = gain is not None
    return pl.pallas_call(
        functools.partial(_compress_kernel, n_rows=n_rows, do_norm=do_norm),
        out_shape=jax.ShapeDtypeStruct((n_batch, n_g, n_rows, HEAD_DIM), bf16),
        grid=(n_batch, n_g),
        in_specs=[
            pl.BlockSpec((1, 1, seq_len, HEAD_DIM), lambda b, g: (b, g, 0, 0)),
            pl.BlockSpec((CMP_LEN, HEAD_DIM), lambda b, g: (0, 0)),
            pl.BlockSpec((CMP_LEN * HEAD_DIM, CMP_HIDDEN), lambda b, g: (0, 0)),
            pl.BlockSpec((CMP_HIDDEN, HEAD_DIM), lambda b, g: (0, 0)),
        ] + ([pl.BlockSpec((1, HEAD_DIM), lambda b, g: (0, 0))] if do_norm else []),
        out_specs=pl.BlockSpec((1, 1, n_rows, HEAD_DIM), lambda b, g: (b, g, 0, 0)),
        compiler_params=_params("arbitrary", "arbitrary"),
        name="compress",
    )(x4, pe, w1_bf16, w2_bf16, *([gain] if do_norm else []))


def _softmax_block(s, d, m_prev, l_prev):
    m_blk = jnp.max(s, axis=0, keepdims=True)
    m_new = jnp.maximum(m_prev, m_blk if d is None else m_blk + d)
    p = jnp.exp2(s - (m_new if d is None else m_new - d))
    alpha = jnp.exp2(m_prev - m_new)
    l_new = alpha * l_prev + jnp.sum(p, axis=0, keepdims=True)
    return m_new, l_new, alpha, p.astype(bf16)


class _Chain(NamedTuple):
    scores: Callable
    values: Callable
    offset: Callable
    diag_mask: Callable
    emit: Callable
    next_first: Callable
    s_buf: Any
    p_buf: Any


def _flash_sweep(n_full, chains, n_q, is_first):
    @pl.when(is_first)
    def _():
        for c in chains:
            c.s_buf[0] = c.scores(0)

    for c in chains:
        c.p_buf[1] = jnp.zeros(c.p_buf.shape[1:], c.p_buf.dtype)

    def accumulate(c, j, slot, alpha, acc):
        return alpha * acc + _dot(c.values(jnp.maximum(j, 0)), c.p_buf[slot])

    def step(j, cur, states):
        out = []
        for c, (m, l, acc, alpha_prev) in zip(chains, states):
            acc = accumulate(c, j - 1, 1 - cur, alpha_prev, acc)
            m, l, alpha, p = _softmax_block(c.s_buf[cur], c.offset(j), m, l)
            c.p_buf[cur] = p
            c.s_buf[1 - cur] = c.scores(j + 1)
            out.append((m, l, acc, alpha))
        return tuple(out)

    def finish(cur, states):
        nxt = [c.next_first() for c in chains]
        for c, s_next, (m, l, acc, alpha_prev) in zip(chains, nxt, states):
            if cur == 1:
                c.s_buf[0] = s_next
            acc = accumulate(c, n_full - 1, 1 - cur, alpha_prev, acc)
            m, l, alpha, p = _softmax_block(c.diag_mask(c.s_buf[cur]), None, m, l)
            if cur == 0:
                c.s_buf[0] = s_next
            acc = alpha * acc + _dot(c.values(n_full), p)
            c.emit(acc / l)

    init = tuple((jnp.full((1, n_q), -jnp.inf, f32), jnp.zeros((1, n_q), f32),
                  jnp.zeros((HEAD_DIM, n_q), f32), jnp.ones((1, n_q), f32)) for _ in chains)
    states = lax.fori_loop(0, n_full // 2, lambda jj, st: step(2 * jj + 1, 1, step(2 * jj, 0, st)), init)

    @pl.when(n_full % 2 == 0)
    def _():
        finish(0, states)

    @pl.when(n_full % 2 == 1)
    def _():
        finish(1, step(n_full - 1, 0, states))


def _fox_kernel(q_ref, qn_ref, k_ref, vt_ref, c_ref, o_ref, *bufs, tq, n_heads):
    hb = pl.program_id(1)
    i = pl.program_id(2)
    lane = lax.broadcasted_iota(jnp.int32, (1, LANES), 1)

    def diag_mask(s):
        kk = lax.broadcasted_iota(jnp.int32, (tq, tq), 0)
        qq = lax.broadcasted_iota(jnp.int32, (tq, tq), 1)
        return jnp.where(kk <= qq, s, -jnp.inf)

    def chain(u):
        qa = q_ref[0, u]

        def cin(j):
            return jnp.sum(jnp.where(lane == hb * n_heads + u, c_ref[0, j], 0.0), axis=-1, keepdims=True)

        ci = cin(i)

        def scores(j):
            return _dot_nt(k_ref[0, u, pl.ds(pl.multiple_of(j * tq, tq), tq), :], qa)

        def emit(out_t):
            o_ref[0, :, u * HEAD_DIM:(u + 1) * HEAD_DIM] = out_t.T.astype(bf16)

        def next_first():
            return _dot_nt(k_ref[0, u, 0:tq, :], qn_ref[0, u])

        return _Chain(scores, lambda j: vt_ref[0, u, j], lambda j: (ci - cin(j)) * LOG2E, diag_mask, emit,
                      next_first, bufs[2 * u], bufs[2 * u + 1])

    _flash_sweep(i, [chain(u) for u in range(n_heads)], tq, i == 0)


def _fox(fq, fk, fvt, cin):
    n_batch, n_h, seq_len, _ = fq.shape
    tq = KV_BLOCK
    nkb = seq_len // tq
    n_heads = 4
    return pl.pallas_call(
        functools.partial(_fox_kernel, tq=tq, n_heads=n_heads),
        out_shape=jax.ShapeDtypeStruct((n_batch, seq_len, n_h * HEAD_DIM), bf16),
        grid=(n_batch, n_h // n_heads, nkb),
        in_specs=[
            pl.BlockSpec((1, n_heads, tq, 2 * HEAD_DIM), lambda b, h, i: (b, h, i, 0)),
            pl.BlockSpec((1, n_heads, tq, 2 * HEAD_DIM), lambda b, h, i: (b, h, jnp.minimum(i + 1, nkb - 1), 0)),
            pl.BlockSpec((1, n_heads, seq_len, 2 * HEAD_DIM), lambda b, h, i: (b, h, 0, 0)),
            pl.BlockSpec((1, n_heads, nkb, HEAD_DIM, tq), lambda b, h, i: (b, h, 0, 0, 0)),
            pl.BlockSpec((1, nkb, 1, LANES), lambda b, h, i: (b, 0, 0, 0)),
        ],
        out_specs=pl.BlockSpec((1, tq, n_heads * HEAD_DIM), lambda b, h, i: (b, i, h)),
        scratch_shapes=[pltpu.VMEM((2, tq, tq), f32), pltpu.VMEM((2, tq, tq), bf16)] * n_heads,
        compiler_params=_params("arbitrary", "arbitrary", "arbitrary"),
        name="fox",
    )(fq, fq, fk, fvt, cin)


def _cmp_kernel(q_ref, k_ref, v_ref, gate_ref, ov_ref, o_ref, sel_ref, *, tq, n_rows, n_cmp, n_sel):
    i = pl.program_id(2)
    g = pl.program_id(1)
    hg = NSA_GROUP
    q = q_ref[0].reshape(hg * tq, HEAD_DIM)
    s = _dot_nt(q, k_ref[0, 0]).reshape(hg, tq, n_rows)
    t = i * tq + lax.broadcasted_iota(jnp.int32, (1, tq, 1), 1)
    n = lax.broadcasted_iota(jnp.int32, (1, 1, n_rows), 2)
    last = jnp.minimum(lax.shift_right_arithmetic(t - (CMP_LEN - 1), CMP_SHIFT), n_cmp - 1)
    s = jnp.where(n <= last, s, -jnp.inf)
    m = jnp.max(s, axis=-1, keepdims=True)
    m = jnp.where(m == -jnp.inf, 0.0, m)
    e = jnp.exp(s - m)
    p = e * (1.0 / jnp.maximum(jnp.sum(e, axis=-1, keepdims=True), 1e-30))
    o = _dot(p.reshape(hg * tq, n_rows).astype(bf16), v_ref[0, 0])
    _write_gated(o_ref, gate_ref, o, g, 0, tq)

    lane = lax.broadcasted_iota(jnp.int32, (tq, LANES), 1)
    tq_pos = i * tq + lax.broadcasted_iota(jnp.int32, (tq, 1), 0)
    q_blk = lax.shift_right_logical(tq_pos, SEL_SHIFT)
    causal = lane <= q_blk
    n_top = min(SEL_TOPK, n_sel)

    @pl.when((i + 1) * tq <= n_top * SEL_BLOCK)
    def _():
        sel_ref[0, 0] = jnp.where(causal, 0.0, -MASK_BIG).astype(bf16)

    @pl.when((i + 1) * tq > n_top * SEL_BLOCK)
    def _():
        psum = p[0] + p[1] + p[2] + p[3]
        imp = _dot_f32_by_exact(psum, ov_ref[...])
        forced = (lane == 0) | (lane == q_blk) | (lane == q_blk - 1)
        key = jnp.where(forced, -2.0, jnp.where(causal, imp, -1.0))
        key = jnp.where(lane < n_sel, key, -3.0)
        lane_f = lane.astype(f32)
        sel = forced
        for _ in range(n_top - 3):
            mx = jnp.max(key, axis=-1, keepdims=True)
            first = jnp.min(jnp.where(key == mx, lane_f, float(LANES)), axis=-1, keepdims=True)
            pick = lane_f == first
            sel = sel | pick
            key = jnp.where(pick, -2.0, key)
        sel = ((q_blk < n_top) | sel) & causal
        sel_ref[0, 0] = jnp.where(sel, 0.0, -MASK_BIG).astype(bf16)


def _cmp(nqu, kcmp, vcmp, gates, n_cmp):
    n_batch, _, seq_len, _ = nqu.shape
    n_rows = kcmp.shape[2]
    n_sel = seq_len // SEL_BLOCK
    assert 3 <= n_sel <= SEL_LANES
    tq = 1024
    c0 = np.arange(n_rows) * CMP_STRIDE
    s0 = np.arange(SEL_LANES) * SEL_BLOCK
    overlap = np.clip(np.minimum(c0[:, None] + CMP_LEN, s0[None, :] + SEL_BLOCK)
                      - np.maximum(c0[:, None], s0[None, :]), 0, None).astype(np.float32) / CMP_LEN
    overlap[n_cmp:, :] = 0.0
    overlap[:, n_sel:] = 0.0
    return pl.pallas_call(
        functools.partial(_cmp_kernel, tq=tq, n_rows=n_rows, n_cmp=n_cmp, n_sel=n_sel),
        out_shape=[
            jax.ShapeDtypeStruct((n_batch, seq_len, NSA_W), bf16),
            jax.ShapeDtypeStruct((n_batch, NSA_KV_HEADS, seq_len, SEL_LANES), bf16),
        ],
        grid=(n_batch, NSA_KV_HEADS, seq_len // tq),
        in_specs=[
            pl.BlockSpec((1, NSA_GROUP, tq, HEAD_DIM), lambda b, g, i: (b, g, i, 0)),
            pl.BlockSpec((1, 1, n_rows, HEAD_DIM), lambda b, g, i: (b, g, 0, 0)),
            pl.BlockSpec((1, 1, n_rows, HEAD_DIM), lambda b, g, i: (b, g, 0, 0)),
            pl.BlockSpec((1, tq, LANES), lambda b, g, i: (b, i, 0)),
            pl.BlockSpec((n_rows, SEL_LANES), lambda b, g, i: (0, 0)),
        ],
        out_specs=[
            pl.BlockSpec((1, tq, NSA_GROUP * HEAD_DIM), lambda b, g, i: (b, i, g)),
            pl.BlockSpec((1, 1, tq, SEL_LANES), lambda b, g, i: (b, g, i, 0)),
        ],
        compiler_params=_params("arbitrary", "arbitrary", "arbitrary"),
        name="cmp",
    )(nqu, kcmp, vcmp, gates, jnp.asarray(overlap, dtype=bf16))


def _write_gated(o_ref, gate_ref, out, g, branch, tq):
    gates = gate_ref[0]
    lane = lax.broadcasted_iota(jnp.int32, (tq, LANES), 1)
    for h in range(NSA_GROUP):
        gcol = FOX_HEADS + (g * NSA_GROUP + h) * N_BRANCH + branch
        gh = jnp.sum(jnp.where(lane == gcol, gates, 0.0), axis=-1, keepdims=True)
        o_ref[0, :, h * HEAD_DIM:(h + 1) * HEAD_DIM] = (gh * out[h * tq:(h + 1) * tq]).astype(bf16)


def _slc_kernel(q_ref, sel_ref, qn_ref, seln_ref, k_ref, vt_ref, gate_ref, *rest, tq, tk, n_cast_w):
    o_ref = rest[n_cast_w]
    bufs = rest[2 * n_cast_w + 1:]
    for w32_ref, w16_ref in zip(rest[:n_cast_w], rest[n_cast_w + 1:2 * n_cast_w + 1]):
        w16_ref[...] = w32_ref[...].astype(bf16)
    i = pl.program_id(1)
    hg = NSA_GROUP
    n_q = hg * tq
    q0 = i * tq
    jd = q0 // tk

    def diag_mask(s):
        kp = jd * tk + lax.broadcasted_iota(jnp.int32, (tk, n_q), 0)
        t = q0 + (lax.broadcasted_iota(jnp.int32, (tk, n_q), 1) & (tq - 1))
        return jnp.where(kp <= t, s, -jnp.inf)

    def chain(g):
        def q_aug(qr, sr):
            q = qr[0, g * hg:(g + 1) * hg].reshape(n_q, HEAD_DIM)
            return jnp.concatenate([q, jnp.concatenate([sr[0, g]] * hg, axis=0)], axis=1)

        qa = q_aug(q_ref, sel_ref)

        def scores(j):
            return _dot_nt(k_ref[0, g, pl.ds(pl.multiple_of(j * tk, tk), tk), :], qa)

        def next_first():
            return _dot_nt(k_ref[0, g, 0:tk, :], q_aug(qn_ref, seln_ref))

        def emit(out):
            gates = gate_ref[0]
            lane = lax.broadcasted_iota(jnp.int32, (tq, LANES), 1)
            for h in range(hg):
                head = g * hg + h
                gh = jnp.sum(jnp.where(lane == FOX_HEADS + head * N_BRANCH + 1, gates, 0.0),
                             axis=-1, keepdims=True)
                o_ref[0, :, head * HEAD_DIM:(head + 1) * HEAD_DIM] = (
                    gh * out[:, h * tq:(h + 1) * tq].T).astype(bf16)

        return _Chain(scores, lambda j: vt_ref[0, g, j], lambda j: None, diag_mask, emit, next_first,
                      bufs[2 * g], bufs[2 * g + 1])

    _flash_sweep(jd, [chain(g) for g in range(NSA_KV_HEADS)], n_q, i == 0)


def _slc(nqr, selneg, kaug, vst, gates, cast_weights):
    n_batch, _, seq_len, _ = nqr.shape
    tq, tk = 256, KV_BLOCK
    nkb = seq_len // tk
    assert tq & (tq - 1) == 0
    n_q = NSA_GROUP * tq
    n_i = seq_len // tq
    cast_in, cast_out, cast_shapes = _cast_side_job(cast_weights, n_batch * n_i, lambda b, i: b * n_i + i)
    return pl.pallas_call(
        functools.partial(_slc_kernel, tq=tq, tk=tk, n_cast_w=len(cast_weights)),
        out_shape=[jax.ShapeDtypeStruct((n_batch, seq_len, NSA_W), bf16)] + cast_shapes,
        grid=(n_batch, n_i),
        in_specs=[
            pl.BlockSpec((1, NSA_HEADS, tq, HEAD_DIM), lambda b, i: (b, 0, i, 0)),
            pl.BlockSpec((1, NSA_KV_HEADS, tq, SEL_LANES), lambda b, i: (b, 0, i, 0)),
            pl.BlockSpec((1, NSA_HEADS, tq, HEAD_DIM), lambda b, i: (b, 0, jnp.minimum(i + 1, n_i - 1), 0)),
            pl.BlockSpec((1, NSA_KV_HEADS, tq, SEL_LANES), lambda b, i: (b, 0, jnp.minimum(i + 1, n_i - 1), 0)),
            pl.BlockSpec((1, NSA_KV_HEADS, seq_len, HEAD_DIM + SEL_LANES), lambda b, i: (b, 0, 0, 0)),
            pl.BlockSpec((1, NSA_KV_HEADS, nkb, HEAD_DIM, tk), lambda b, i: (b, 0, 0, 0, 0)),
            pl.BlockSpec((1, tq, LANES), lambda b, i: (b, i, 0)),
        ] + cast_in,
        out_specs=[pl.BlockSpec((1, tq, NSA_W), lambda b, i: (b, i, 0))] + cast_out,
        scratch_shapes=[pltpu.VMEM((2, tk, n_q), f32), pltpu.VMEM((2, tk, n_q), bf16)] * NSA_KV_HEADS,
        compiler_params=_params("arbitrary", "arbitrary"),
        name="slc",
    )(nqr, selneg, nqr, selneg, kaug, vst, gates, *cast_weights)


def _win_kernel(q_ref, k_ref, vt_ref, gate_ref, o_ref, *, tq, n_sub):
    i = pl.program_id(2)
    g = pl.program_id(1)
    hg = NSA_GROUP
    span = WINDOW + tq
    q0s = [(i * n_sub + u) * tq for u in range(n_sub)]
    k0s = [pl.multiple_of(jnp.maximum(q0 - WINDOW, 0), tq) for q0 in q0s]
    scores = []
    for u in range(n_sub):
        q = q_ref[0, :, u * tq:(u + 1) * tq, :].reshape(hg * tq, HEAD_DIM)
        scores.append(_dot_nt(k_ref[0, 0, pl.ds(k0s[u], span), :], q))
    probs, denoms = [], []
    for u in range(n_sub):
        kp = k0s[u] + lax.broadcasted_iota(jnp.int32, (span, tq), 0)
        t = q0s[u] + lax.broadcasted_iota(jnp.int32, (span, tq), 1)
        diff = t - kp
        bias = jnp.where((diff >= 0) & (diff < WINDOW), 0.0, -jnp.inf)
        s = jnp.concatenate([scores[u][:, h * tq:(h + 1) * tq] + bias for h in range(hg)], axis=1)
        e = jnp.exp2(s - jnp.max(s, axis=0, keepdims=True))
        denoms.append(jnp.sum(e, axis=0, keepdims=True))
        probs.append(e.astype(bf16))
    gates = gate_ref[0]
    lane = lax.broadcasted_iota(jnp.int32, (tq, LANES), 1)
    for u in range(n_sub):
        jb = k0s[u] // tq
        vt = jnp.concatenate([vt_ref[0, 0, jb + c] for c in range(span // tq)], axis=1)
        out = _dot(vt, probs[u]) / denoms[u]
        for h in range(hg):
            gcol = FOX_HEADS + (g * hg + h) * N_BRANCH + 2
            gh = jnp.sum(jnp.where(lane == gcol, gates[u * tq:(u + 1) * tq], 0.0), axis=-1, keepdims=True)
            o_ref[0, u * tq:(u + 1) * tq, h * HEAD_DIM:(h + 1) * HEAD_DIM] = (
                gh * out[:, h * tq:(h + 1) * tq].T).astype(bf16)


def _win(nqr, kw, vwt, gates):
    n_batch, _, seq_len, _ = nqr.shape
    tq, n_sub = WIN_TILE, 4
    assert seq_len >= WINDOW + tq and WINDOW % tq == 0
    return pl.pallas_call(
        functools.partial(_win_kernel, tq=tq, n_sub=n_sub),
        out_shape=jax.ShapeDtypeStruct((n_batch, seq_len, NSA_W), bf16),
        grid=(n_batch, NSA_KV_HEADS, seq_len // (tq * n_sub)),
        in_specs=[
            pl.BlockSpec((1, NSA_GROUP, tq * n_sub, HEAD_DIM), lambda b, g, i: (b, g, i, 0)),
            pl.BlockSpec((1, 1, seq_len, HEAD_DIM), lambda b, g, i: (b, g, 0, 0)),
            pl.BlockSpec((1, 1, seq_len // tq, HEAD_DIM, tq), lambda b, g, i: (b, g, 0, 0, 0)),
            pl.BlockSpec((1, tq * n_sub, LANES), lambda b, g, i: (b, i, 0)),
        ],
        out_specs=pl.BlockSpec((1, tq * n_sub, NSA_GROUP * HEAD_DIM), lambda b, g, i: (b, i, g)),
        compiler_params=_params("arbitrary", "arbitrary", "arbitrary"),
        name="win",
    )(nqr, kw, vwt, gates)


def _outproj_kernel(fox_ref, c_ref, s_ref, w_ref, x_ref, mod_ref, wo_ref, o_ref):
    nsa = c_ref[...].astype(f32) + s_ref[...].astype(f32) + w_ref[...].astype(f32)
    a = jnp.concatenate([fox_ref[...], nsa.astype(bf16)], axis=1)
    o_ref[...] = x_ref[...] + mod_ref[0][2:3] * _dot(a, wo_ref[...])


def _outproj(ofox, ocmp, oslc, owin, x2d, mod3, wo_bf16, seq_len):
    m_rows = x2d.shape[0]
    tm = 512
    per_b = seq_len // tm
    half = pl.BlockSpec((tm, FOX_W), lambda i: (i, 0))
    return pl.pallas_call(
        _outproj_kernel,
        out_shape=jax.ShapeDtypeStruct((m_rows, D_MODEL), f32),
        grid=(m_rows // tm,),
        in_specs=[
            half, half, half, half,
            pl.BlockSpec((tm, D_MODEL), lambda i: (i, 0)),
            pl.BlockSpec((1, 6, D_MODEL), lambda i: (i // per_b, 0, 0)),
            pl.BlockSpec((D_MODEL, D_MODEL), lambda i: (0, 0)),
        ],
        out_specs=pl.BlockSpec((tm, D_MODEL), lambda i: (i, 0)),
        compiler_params=_params("arbitrary"),
        name="outproj",
    )(ofox, ocmp, oslc, owin, x2d, mod3, wo_bf16)


def _mlp_kernel(x_ref, mod_ref, g_ref, wu_ref, wd_ref, o_ref, h_ref, acc_ref):
    f = pl.program_id(1)

    @pl.when(f == 0)
    def _():
        md = mod_ref[0]
        y = _rms(x_ref[...], g_ref[...])
        h_ref[...] = (y * (1.0 + md[4:5]) + md[3:4]).astype(bf16)
        acc_ref[...] = jnp.zeros_like(acc_ref)

    u = jnp.maximum(_dot(h_ref[...], wu_ref[...]), 0.0)
    acc_ref[...] += _dot((u * u).astype(bf16), wd_ref[...])

    @pl.when(f == pl.num_programs(1) - 1)
    def _():
        o_ref[...] = x_ref[...] + mod_ref[0][5:6] * acc_ref[...]


def _mlp(x2d, mod3, norm_g, wu_bf16, wd_bf16, seq_len):
    m_rows = x2d.shape[0]
    tm, tf = 512, 1024
    per_b = seq_len // tm
    return pl.pallas_call(
        _mlp_kernel,
        out_shape=jax.ShapeDtypeStruct((m_rows, D_MODEL), f32),
        grid=(m_rows // tm, D_FF // tf),
        in_specs=[
            pl.BlockSpec((tm, D_MODEL), lambda i, f: (i, 0)),
            pl.BlockSpec((1, 6, D_MODEL), lambda i, f: (i // per_b, 0, 0)),
            pl.BlockSpec((1, D_MODEL), lambda i, f: (0, 0)),
            pl.BlockSpec((D_MODEL, tf), lambda i, f: (0, f)),
            pl.BlockSpec((tf, D_MODEL), lambda i, f: (f, 0)),
        ],
        out_specs=pl.BlockSpec((tm, D_MODEL), lambda i, f: (i, 0)),
        scratch_shapes=[pltpu.VMEM((tm, D_MODEL), bf16), pltpu.VMEM((tm, D_MODEL), f32)],
        compiler_params=_params("arbitrary", "arbitrary"),
        name="mlp",
    )(x2d, mod3, norm_g, wu_bf16, wd_bf16)


def _layer(x, c, w_ada, b_ada, norm1_g, w_in, b_forget, fox_q_norm, fox_k_norm, nsa_q_norm,
           cmp_k_norm, slc_k_norm, win_k_norm, cmp_pe_k, cmp_w1_k, cmp_w2_k, cmp_pe_v, cmp_w1_v,
           cmp_w2_v, w_out, norm2_g, w_up, w_down):
    n_batch, seq_len, _ = x.shape
    n_cmp = (seq_len - CMP_LEN) // CMP_STRIDE + 1
    row = lambda v: v.reshape(1, -1)

    half = HEAD_DIM // 2
    inv_freq = ROPE_THETA ** (-jnp.arange(half, dtype=f32) / half)
    inv_freq = jnp.concatenate([inv_freq, inv_freq]).reshape(1, HEAD_DIM)

    w_t = w_in.T
    w_a = w_t[:W_IN_Z0].astype(bf16)
    w_b = w_t[W_IN_NQ0:W_IN_GZ0].astype(bf16)
    w_s = jnp.concatenate([w_t[W_IN_Z0:W_IN_NQ0], w_t[W_IN_GZ0:],
                           jnp.zeros((LANES - FOX_HEADS - N_BRANCH * NSA_HEADS, D_MODEL), w_t.dtype)],
                          axis=0).astype(bf16)

    mod3 = _ada(c, w_ada, b_ada).reshape(n_batch, 6, D_MODEL)
    x2d = x.reshape(n_batch * seq_len, D_MODEL)
    proj, small = _proj(x2d, mod3, row(norm1_g), w_a, w_b, w_s, seq_len)
    (fq, fk, fvt, cin, nqu, nqr, kc, vc, kaug, vst, kw, vw, gates) = _prep(
        proj, small, n_batch, seq_len, row(fox_q_norm), row(fox_k_norm), row(nsa_q_norm), row(slc_k_norm),
        row(win_k_norm), jnp.pad(b_forget, (0, LANES - FOX_HEADS)).reshape(1, LANES), inv_freq)
    kcmp = _compress(kc, cmp_pe_k, cmp_w1_k.astype(bf16), cmp_w2_k.astype(bf16), row(cmp_k_norm))
    vcmp = _compress(vc, cmp_pe_v, cmp_w1_v.astype(bf16), cmp_w2_v.astype(bf16))
    ofox = _fox(fq, fk, fvt, cin)
    ocmp, selneg = _cmp(nqu, kcmp, vcmp, gates, n_cmp)
    oslc, w_up16, w_down16, w_out16 = _slc(nqr, selneg, kaug, vst, gates, [w_up, w_down, w_out])
    owin = _win(nqr, kw, vw, gates)
    x1 = _outproj(ofox.reshape(-1, FOX_W), ocmp.reshape(-1, NSA_W), oslc.reshape(-1, NSA_W),
                  owin.reshape(-1, NSA_W), x2d, mod3, w_out16, seq_len)
    x2 = _mlp(x1, mod3, row(norm2_g), w_up16, w_down16, seq_len)
    return x2.reshape(n_batch, seq_len, D_MODEL)


def kernel(x, c, w_ada, b_ada, norm1_g, w_in, b_forget, fox_q_norm, fox_k_norm, nsa_q_norm, cmp_k_norm,
           slc_k_norm, win_k_norm, cmp_pe_k, cmp_w1_k, cmp_w2_k, cmp_pe_v, cmp_w1_v, cmp_w2_v, w_out,
           norm2_g, w_up, w_down):
    depth = w_ada.shape[0]
    for l in range(depth):
        x = _layer(x, c, w_ada[l], b_ada[l], norm1_g[l], w_in[l], b_forget[l], fox_q_norm[l], fox_k_norm[l],
                   nsa_q_norm[l], cmp_k_norm[l], slc_k_norm[l], win_k_norm[l], cmp_pe_k[l], cmp_w1_k[l],
                   cmp_w2_k[l], cmp_pe_v[l], cmp_w1_v[l], cmp_w2_v[l], w_out[l], norm2_g[l], w_up[l],
                   w_down[l])
    return x
```

```python
import functools
import math
from typing import Any, Callable, NamedTuple

import numpy as np
import jax
import jax.numpy as jnp
from jax import lax
from jax.experimental import pallas as pl
from jax.experimental.pallas import tpu as pltpu

D_MODEL = 2048
HEAD_DIM = 128
FOX_HEADS = 8
NSA_HEADS = 8
NSA_KV_HEADS = 2
NSA_GROUP = NSA_HEADS // NSA_KV_HEADS
N_BRANCH = 3
D_FF = 4 * D_MODEL
ROPE_THETA = 10000.0
CMP_LEN = 32
CMP_STRIDE = 16
CMP_SHIFT = 4
CMP_HIDDEN = 2 * HEAD_DIM
SEL_BLOCK = 64
SEL_SHIFT = 6
SEL_TOPK = 16
WINDOW = 512
NORM_EPS = 1e-6
ATTN_SCALE = HEAD_DIM ** -0.5
FOX_W = FOX_HEADS * HEAD_DIM
NSA_W = NSA_HEADS * HEAD_DIM
KV_W = NSA_KV_HEADS * HEAD_DIM

LANES = 128
SEL_LANES = LANES
MASK_BIG = 1e30
KV_BLOCK = 512
WIN_TILE = 128
LOG2E = math.log2(math.e)

COL_FQ = 0
COL_FK = COL_FQ + FOX_W
COL_FV = COL_FK + FOX_W
COL_NQ = COL_FV + FOX_W
COL_KC = COL_NQ + NSA_W
COL_VC = COL_KC + KV_W
COL_KS = COL_VC + KV_W
COL_VS = COL_KS + KV_W
COL_KW = COL_VS + KV_W
COL_VW = COL_KW + KV_W
COL_SMALL = COL_VW + KV_W
W_IN_Z0 = 3 * FOX_W
W_IN_NQ0 = W_IN_Z0 + FOX_HEADS
W_IN_GZ0 = W_IN_NQ0 + NSA_W + 6 * KV_W

VMEM_LIMIT = 56 * 1024 * 1024
VMEM_LIMIT_PREP = 60 * 1024 * 1024

f32 = jnp.float32
bf16 = jnp.bfloat16


def _params(*sem):
    return pltpu.CompilerParams(dimension_semantics=sem, vmem_limit_bytes=VMEM_LIMIT)


def _dot_nt(a, b):
    return lax.dot_general(a, b, (((1,), (1,)), ((), ())), preferred_element_type=f32)


def _dot(a, b):
    return jnp.dot(a, b, preferred_element_type=f32)


def _split3(x):
    hi = x.astype(bf16)
    r1 = x - hi.astype(f32)
    mid = r1.astype(bf16)
    lo = (r1 - mid.astype(f32)).astype(bf16)
    return hi, mid, lo


def _dot_f32_by_exact(x, w_bf16):
    hi, mid, lo = _split3(x)
    return _dot(hi, w_bf16) + (_dot(mid, w_bf16) + _dot(lo, w_bf16))


def _rms(x, gain):
    ms = jnp.mean(x * x, axis=-1, keepdims=True)
    return x * lax.rsqrt(ms + NORM_EPS) * gain


def _ada_kernel(ct_ref, w_ref, b_ref, o_ref, *, n_batch, k_chunk):
    ct = ct_ref[...]
    act = ct * jax.nn.sigmoid(ct)
    rows = []
    for b in range(n_batch):
        col = act[:, b:b + 1]
        acc = b_ref[...]
        for k0 in range(0, D_MODEL, k_chunk):
            acc = acc + jnp.sum(w_ref[k0:k0 + k_chunk, :] * col[k0:k0 + k_chunk], axis=0, keepdims=True)
        rows.append(acc)
    o_ref[...] = jnp.concatenate(rows, axis=0)


def _ada(c, w_ada, b_ada):
    n_batch = c.shape[0]
    n_out = w_ada.shape[1]
    tn = 1024
    return pl.pallas_call(
        functools.partial(_ada_kernel, n_batch=n_batch, k_chunk=256),
        out_shape=jax.ShapeDtypeStruct((n_batch, n_out), f32),
        grid=(n_out // tn,),
        in_specs=[
            pl.BlockSpec((D_MODEL, n_batch), lambda j: (0, 0)),
            pl.BlockSpec((D_MODEL, tn), lambda j: (0, j)),
            pl.BlockSpec((1, tn), lambda j: (0, j)),
        ],
        out_specs=pl.BlockSpec((n_batch, tn), lambda j: (0, j)),
        compiler_params=_params("arbitrary"),
        name="ada",
    )(c.T, w_ada, b_ada.reshape(1, n_out))


def _cast_side_job(weights, n_steps, step_of):
    n_cast = max(c for c in range(1, n_steps + 1) if all(w.shape[0] % (16 * c) == 0 for w in weights))
    idx = lambda *ids: (jnp.minimum(step_of(*ids), n_cast - 1), 0)
    specs = [pl.BlockSpec((w.shape[0] // n_cast, w.shape[1]), idx) for w in weights]
    return specs, specs, [jax.ShapeDtypeStruct(w.shape, bf16) for w in weights]


def _prep_kernel(x_ref, mod_ref, g1_ref, wa_ref, wb_ref, ws_ref,
                 gq_ref, gk_ref, gn_ref, gs_ref, gw_ref, bf_ref, inv_ref,
                 fq_ref, fk_ref, fvt_ref, cin_ref, nqu_ref, nqr_ref, kc_ref, vc_ref,
                 ks_ref, vst_ref, kw_ref, vw_ref, gate_ref, carry_ref, rot_ref, *, tm):
    i = pl.program_id(1)

    @pl.when(i == 0)
    def _():
        carry_ref[...] = jnp.zeros_like(carry_ref)

    md = mod_ref[0]
    hn = (_rms(x_ref[...], g1_ref[...]) * (1.0 + md[1:2]) + md[0:1]).astype(bf16)
    groups = {}

    def head(col, h):
        base, w_ref, r0 = next((b, w, r) for b, w, r in (
            (COL_KC, wb_ref, NSA_W), (COL_NQ, wb_ref, 0), (COL_FV, wa_ref, 2 * FOX_W),
            (COL_FK, wa_ref, FOX_W), (COL_FQ, wa_ref, 0)) if col >= b)
        if base not in groups:
            n = (COL_SMALL - COL_KC) if base == COL_KC else FOX_W
            groups[base] = _dot_nt(hn, w_ref[r0:r0 + n, :])
        c0 = col - base + h * HEAD_DIM
        return groups[base][:, c0:c0 + HEAD_DIM]

    row = lax.broadcasted_iota(jnp.int32, (tm, LANES), 0)
    lane = lax.broadcasted_iota(jnp.int32, (tm, LANES), 1)
    pos = i * tm + row

    @pl.when((pl.program_id(0) == 0) & (i == 0))
    def _():
        ang_row = row.astype(f32) * inv_ref[...]
        rot_ref[0] = jnp.cos(ang_row)
        rot_ref[1] = jnp.sin(ang_row)

    ang0 = (i * tm).astype(f32) * inv_ref[...]
    cos0, sin0 = jnp.cos(ang0), jnp.sin(ang0)
    cos = cos0 * rot_ref[0] - sin0 * rot_ref[1]
    sin = sin0 * rot_ref[0] + cos0 * rot_ref[1]
    sin_signed = jnp.where(lane < HEAD_DIM // 2, -sin, sin)

    def rope(x):
        return x * cos + pltpu.roll(x, HEAD_DIM // 2, 1) * sin_signed

    small = _dot_nt(hn, ws_ref[...])
    z = small + bf_ref[...]
    logf = jnp.minimum(z, 0.0) - jnp.log1p(jnp.exp(-jnp.abs(z)))
    t_idx = lax.broadcasted_iota(jnp.int32, (tm, tm), 0)
    s_idx = lax.broadcasted_iota(jnp.int32, (tm, tm), 1)
    tri = jnp.where(s_idx <= t_idx, 1.0, 0.0).astype(bf16)
    hi, mid, lo = _split3(logf)
    local = _dot(tri, hi) + (_dot(tri, mid) + _dot(tri, lo))
    cin_ref[0, 0] = carry_ref[0:1, :]
    carry_ref[...] = carry_ref[...] + local[tm - 1:tm, :]
    b_hi, b_mid, b_lo = (v.astype(f32) for v in _split3(local * (-LOG2E)))
    ones3 = jnp.where(lane < 3, 1.0, 0.0).astype(bf16)

    for h in range(FOX_HEADS):
        q = _rms(head(COL_FQ, h), gq_ref[...]) * (ATTN_SCALE * LOG2E)
        fq_ref[0, h] = jnp.concatenate([q.astype(bf16), ones3], axis=1)
        k = _rms(head(COL_FK, h), gk_ref[...])
        bias = jnp.where(lane == 0, b_hi[:, h:h + 1],
                         jnp.where(lane == 1, b_mid[:, h:h + 1],
                                   jnp.where(lane == 2, b_lo[:, h:h + 1], 0.0)))
        fk_ref[0, h] = jnp.concatenate([k.astype(bf16), bias.astype(bf16)], axis=1)
        fvt_ref[0, h, 0] = head(COL_FV, h).T.astype(bf16)

    gate_ref[0] = jax.nn.sigmoid(small)

    for h in range(NSA_HEADS):
        qn = _rms(head(COL_NQ, h), gn_ref[...])
        nqu_ref[0, h] = (qn * ATTN_SCALE).astype(bf16)
        nqr_ref[0, h] = rope(qn * (ATTN_SCALE * LOG2E)).astype(bf16)
    onehot = jnp.where(lane == lax.shift_right_logical(pos, SEL_SHIFT), 1.0, 0.0).astype(bf16)
    for g in range(NSA_KV_HEADS):
        kc_ref[0, g] = head(COL_KC, g)
        vc_ref[0, g] = head(COL_VC, g)
        ks = rope(_rms(head(COL_KS, g), gs_ref[...])).astype(bf16)
        ks_ref[0, g] = jnp.concatenate([ks, onehot], axis=1)
        vst_ref[0, g, 0] = head(COL_VS, g).T.astype(bf16)
        kw_ref[0, g] = rope(_rms(head(COL_KW, g), gw_ref[...])).astype(bf16)
        vwt = head(COL_VW, g).T.astype(bf16)
        for c in range(tm // WIN_TILE):
            vw_ref[0, g, c] = vwt[:, c * WIN_TILE:(c + 1) * WIN_TILE]


def _prep(x2d, mod3, norm_g, wa, wb, ws, n_batch, seq_len, gq, gk, gn, gs, gw, b_forget_row, inv_freq):
    resident = lambda w: pl.BlockSpec(w.shape, lambda b, i: (0, 0), pipeline_mode=pl.Buffered(1))
    tm = KV_BLOCK
    per_b = seq_len // tm
    hshape = lambda n, w, dt: jax.ShapeDtypeStruct((n_batch, n, seq_len, w), dt)
    hspec = lambda n, w: pl.BlockSpec((1, n, tm, w), lambda b, i: (b, 0, i, 0))
    tshape = lambda n: jax.ShapeDtypeStruct((n_batch, n, per_b, HEAD_DIM, tm), bf16)
    tspec = lambda n: pl.BlockSpec((1, n, 1, HEAD_DIM, tm), lambda b, i: (b, 0, i, 0, 0))
    vec = pl.BlockSpec((1, LANES), lambda b, i: (0, 0))
    return pl.pallas_call(
        functools.partial(_prep_kernel, tm=tm),
        out_shape=[
            hshape(FOX_HEADS, 2 * HEAD_DIM, bf16), hshape(FOX_HEADS, 2 * HEAD_DIM, bf16), tshape(FOX_HEADS),
            jax.ShapeDtypeStruct((n_batch, per_b, 1, LANES), f32),
            hshape(NSA_HEADS, HEAD_DIM, bf16), hshape(NSA_HEADS, HEAD_DIM, bf16),
            hshape(NSA_KV_HEADS, HEAD_DIM, f32), hshape(NSA_KV_HEADS, HEAD_DIM, f32),
            hshape(NSA_KV_HEADS, HEAD_DIM + SEL_LANES, bf16), tshape(NSA_KV_HEADS),
            hshape(NSA_KV_HEADS, HEAD_DIM, bf16),
            jax.ShapeDtypeStruct((n_batch, NSA_KV_HEADS, seq_len // WIN_TILE, HEAD_DIM, WIN_TILE), bf16),
            jax.ShapeDtypeStruct((n_batch, seq_len, LANES), f32),
        ],
        grid=(n_batch, per_b),
        in_specs=[
            pl.BlockSpec((tm, D_MODEL), lambda b, i: (b * per_b + i, 0)),
            pl.BlockSpec((1, 6, D_MODEL), lambda b, i: (b, 0, 0)),
            pl.BlockSpec((1, D_MODEL), lambda b, i: (0, 0)),
            resident(wa), resident(wb), resident(ws),
            vec, vec, vec, vec, vec, vec, vec,
        ],
        out_specs=[
            hspec(FOX_HEADS, 2 * HEAD_DIM), hspec(FOX_HEADS, 2 * HEAD_DIM), tspec(FOX_HEADS),
            pl.BlockSpec((1, 1, 1, LANES), lambda b, i: (b, i, 0, 0)),
            hspec(NSA_HEADS, HEAD_DIM), hspec(NSA_HEADS, HEAD_DIM),
            hspec(NSA_KV_HEADS, HEAD_DIM), hspec(NSA_KV_HEADS, HEAD_DIM),
            hspec(NSA_KV_HEADS, HEAD_DIM + SEL_LANES), tspec(NSA_KV_HEADS),
            hspec(NSA_KV_HEADS, HEAD_DIM),
            pl.BlockSpec((1, NSA_KV_HEADS, tm // WIN_TILE, HEAD_DIM, WIN_TILE), lambda b, i: (b, 0, i, 0, 0)),
            pl.BlockSpec((1, tm, LANES), lambda b, i: (b, i, 0)),
        ],
        scratch_shapes=[pltpu.VMEM((8, LANES), f32), pltpu.VMEM((2, tm, LANES), f32)],
        compiler_params=pltpu.CompilerParams(dimension_semantics=("arbitrary", "arbitrary"),
                                             vmem_limit_bytes=VMEM_LIMIT_PREP),
        name="prep",
    )(x2d, mod3, norm_g, wa, wb, ws, gq, gk, gn, gs, gw, b_forget_row, inv_freq)


def _compress_kernel(x_ref, pe_ref, w1_ref, w2_ref, *rest, n_rows, do_norm):
    o_ref = rest[-1]
    a = jnp.zeros((n_rows, CMP_HIDDEN), f32)
    b = jnp.zeros((n_rows, CMP_HIDDEN), f32)
    for l in range(CMP_STRIDE):
        x = x_ref[0, 0, pl.ds(l, n_rows, stride=CMP_STRIDE), :]
        a = a + _dot((x + pe_ref[l:l + 1, :]).astype(bf16), w1_ref[l * HEAD_DIM:(l + 1) * HEAD_DIM, :])
        lb = CMP_STRIDE + l
        b = b + _dot((x + pe_ref[lb:lb + 1, :]).astype(bf16), w1_ref[lb * HEAD_DIM:(lb + 1) * HEAD_DIM, :])
    pre = a + pltpu.roll(b, n_rows - 1, 0)
    hid = pre * jax.nn.sigmoid(pre)
    out = _dot(hid.astype(bf16), w2_ref[...])
    if do_norm:
        out = _rms(out, rest[0][...])
    o_ref[0, 0] = out.astype(bf16)


def _compress(x4, pe, w1_bf16, w2_bf16, gain=None):
    n_batch, n_g, seq_len, _ = x4.shape
    n_rows = seq_len // CMP_STRIDE
    do_norm = gain is not None
    return pl.pallas_call(
        functools.partial(_compress_kernel, n_rows=n_rows, do_norm=do_norm),
        out_shape=jax.ShapeDtypeStruct((n_batch, n_g, n_rows, HEAD_DIM), bf16),
        grid=(n_batch, n_g),
        in_specs=[
            pl.BlockSpec((1, 1, seq_len, HEAD_DIM), lambda b, g: (b, g, 0, 0)),
            pl.BlockSpec((CMP_LEN, HEAD_DIM), lambda b, g: (0, 0)),
            pl.BlockSpec((CMP_LEN * HEAD_DIM, CMP_HIDDEN), lambda b, g: (0, 0)),
            pl.BlockSpec((CMP_HIDDEN, HEAD_DIM), lambda b, g: (0, 0)),
        ] + ([pl.BlockSpec((1, HEAD_DIM), lambda b, g: (0, 0))] if do_norm else []),
        out_specs=pl.BlockSpec((1, 1, n_rows, HEAD_DIM), lambda b, g: (b, g, 0, 0)),
        compiler_params=_params("arbitrary", "arbitrary"),
        name="compress",
    )(x4, pe, w1_bf16, w2_bf16, *([gain] if do_norm else []))


def _softmax_block(s, d, m_prev, l_prev):
    m_blk = jnp.max(s, axis=0, keepdims=True)
    m_new = jnp.maximum(m_prev, m_blk if d is None else m_blk + d)
    p = jnp.exp2(s - (m_new if d is None else m_new - d))
    alpha = jnp.exp2(m_prev - m_new)
    l_new = alpha * l_prev + jnp.sum(p, axis=0, keepdims=True)
    return m_new, l_new, alpha, p.astype(bf16)


class _Chain(NamedTuple):
    scores: Callable
    values: Callable
    offset: Callable
    diag_mask: Callable
    emit: Callable
    next_first: Callable
    s_buf: Any
    p_buf: Any


def _flash_sweep(n_full, chains, n_q, is_first):
    @pl.when(is_first)
    def _():
        for c in chains:
            c.s_buf[0] = c.scores(0)

    for c in chains:
        c.p_buf[1] = jnp.zeros(c.p_buf.shape[1:], c.p_buf.dtype)

    def accumulate(c, j, slot, alpha, acc):
        return alpha * acc + _dot(c.values(jnp.maximum(j, 0)), c.p_buf[slot])

    def step(j, cur, states):
        out = []
        for c, (m, l, acc, alpha_prev) in zip(chains, states):
            acc = accumulate(c, j - 1, 1 - cur, alpha_prev, acc)
            m, l, alpha, p = _softmax_block(c.s_buf[cur], c.offset(j), m, l)
            c.p_buf[cur] = p
            c.s_buf[1 - cur] = c.scores(j + 1)
            out.append((m, l, acc, alpha))
        return tuple(out)

    def finish(cur, states):
        nxt = [c.next_first() for c in chains]
        for c, s_next, (m, l, acc, alpha_prev) in zip(chains, nxt, states):
            if cur == 1:
                c.s_buf[0] = s_next
            acc = accumulate(c, n_full - 1, 1 - cur, alpha_prev, acc)
            m, l, alpha, p = _softmax_block(c.diag_mask(c.s_buf[cur]), None, m, l)
            if cur == 0:
                c.s_buf[0] = s_next
            acc = alpha * acc + _dot(c.values(n_full), p)
            c.emit(acc / l)

    init = tuple((jnp.full((1, n_q), -jnp.inf, f32), jnp.zeros((1, n_q), f32),
                  jnp.zeros((HEAD_DIM, n_q), f32), jnp.ones((1, n_q), f32)) for _ in chains)
    states = lax.fori_loop(0, n_full // 2, lambda jj, st: step(2 * jj + 1, 1, step(2 * jj, 0, st)), init)

    @pl.when(n_full % 2 == 0)
    def _():
        finish(0, states)

    @pl.when(n_full % 2 == 1)
    def _():
        finish(1, step(n_full - 1, 0, states))


def _fox_kernel(q_ref, qn_ref, k_ref, vt_ref, c_ref, o_ref, *bufs, tq, n_heads):
    hb = pl.program_id(1)
    i = pl.program_id(2)
    lane = lax.broadcasted_iota(jnp.int32, (1, LANES), 1)

    def diag_mask(s):
        kk = lax.broadcasted_iota(jnp.int32, (tq, tq), 0)
        qq = lax.broadcasted_iota(jnp.int32, (tq, tq), 1)
        return jnp.where(kk <= qq, s, -jnp.inf)

    def chain(u):
        qa = q_ref[0, u]

        def cin(j):
            return jnp.sum(jnp.where(lane == hb * n_heads + u, c_ref[0, j], 0.0), axis=-1, keepdims=True)

        ci = cin(i)

        def scores(j):
            return _dot_nt(k_ref[0, u, pl.ds(pl.multiple_of(j * tq, tq), tq), :], qa)

        def emit(out_t):
            o_ref[0, :, u * HEAD_DIM:(u + 1) * HEAD_DIM] = out_t.T.astype(bf16)

        def next_first():
            return _dot_nt(k_ref[0, u, 0:tq, :], qn_ref[0, u])

        return _Chain(scores, lambda j: vt_ref[0, u, j], lambda j: (ci - cin(j)) * LOG2E, diag_mask, emit,
                      next_first, bufs[2 * u], bufs[2 * u + 1])

    _flash_sweep(i, [chain(u) for u in range(n_heads)], tq, i == 0)


def _fox(fq, fk, fvt, cin):
    n_batch, n_h, seq_len, _ = fq.shape
    tq = KV_BLOCK
    nkb = seq_len // tq
    n_heads = 4
    return pl.pallas_call(
        functools.partial(_fox_kernel, tq=tq, n_heads=n_heads),
        out_shape=jax.ShapeDtypeStruct((n_batch, seq_len, n_h * HEAD_DIM), bf16),
        grid=(n_batch, n_h // n_heads, nkb),
        in_specs=[
            pl.BlockSpec((1, n_heads, tq, 2 * HEAD_DIM), lambda b, h, i: (b, h, i, 0)),
            pl.BlockSpec((1, n_heads, tq, 2 * HEAD_DIM), lambda b, h, i: (b, h, jnp.minimum(i + 1, nkb - 1), 0)),
            pl.BlockSpec((1, n_heads, seq_len, 2 * HEAD_DIM), lambda b, h, i: (b, h, 0, 0)),
            pl.BlockSpec((1, n_heads, nkb, HEAD_DIM, tq), lambda b, h, i: (b, h, 0, 0, 0)),
            pl.BlockSpec((1, nkb, 1, LANES), lambda b, h, i: (b, 0, 0, 0)),
        ],
        out_specs=pl.BlockSpec((1, tq, n_heads * HEAD_DIM), lambda b, h, i: (b, i, h)),
        scratch_shapes=[pltpu.VMEM((2, tq, tq), f32), pltpu.VMEM((2, tq, tq), bf16)] * n_heads,
        compiler_params=_params("arbitrary", "arbitrary", "arbitrary"),
        name="fox",
    )(fq, fq, fk, fvt, cin)


def _cmp_kernel(q_ref, k_ref, v_ref, gate_ref, ov_ref, o_ref, sel_ref, *, tq, n_rows, n_cmp, n_sel):
    i = pl.program_id(2)
    g = pl.program_id(1)
    hg = NSA_GROUP
    q = q_ref[0].reshape(hg * tq, HEAD_DIM)
    s = _dot_nt(q, k_ref[0, 0]).reshape(hg, tq, n_rows)
    t = i * tq + lax.broadcasted_iota(jnp.int32, (1, tq, 1), 1)
    n = lax.broadcasted_iota(jnp.int32, (1, 1, n_rows), 2)
    last = jnp.minimum(lax.shift_right_arithmetic(t - (CMP_LEN - 1), CMP_SHIFT), n_cmp - 1)
    s = jnp.where(n <= last, s, -jnp.inf)
    m = jnp.max(s, axis=-1, keepdims=True)
    m = jnp.where(m == -jnp.inf, 0.0, m)
    e = jnp.exp(s - m)
    p = e * (1.0 / jnp.maximum(jnp.sum(e, axis=-1, keepdims=True), 1e-30))
    o = _dot(p.reshape(hg * tq, n_rows).astype(bf16), v_ref[0, 0])
    _write_gated(o_ref, gate_ref, o, g, 0, tq)

    lane = lax.broadcasted_iota(jnp.int32, (tq, LANES), 1)
    tq_pos = i * tq + lax.broadcasted_iota(jnp.int32, (tq, 1), 0)
    q_blk = lax.shift_right_logical(tq_pos, SEL_SHIFT)
    causal = lane <= q_blk
    n_top = min(SEL_TOPK, n_sel)

    @pl.when((i + 1) * tq <= n_top * SEL_BLOCK)
    def _():
        sel_ref[0, 0] = jnp.where(causal, 0.0, -MASK_BIG).astype(bf16)

    @pl.when((i + 1) * tq > n_top * SEL_BLOCK)
    def _():
        psum = p[0] + p[1] + p[2] + p[3]
        imp = _dot_f32_by_exact(psum, ov_ref[...])
        forced = (lane == 0) | (lane == q_blk) | (lane == q_blk - 1)
        key = jnp.where(forced, -2.0, jnp.where(causal, imp, -1.0))
        key = jnp.where(lane < n_sel, key, -3.0)
        lane_f = lane.astype(f32)
        sel = forced
        for _ in range(n_top - 3):
            mx = jnp.max(key, axis=-1, keepdims=True)
            first = jnp.min(jnp.where(key == mx, lane_f, float(LANES)), axis=-1, keepdims=True)
            pick = lane_f == first
            sel = sel | pick
            key = jnp.where(pick, -2.0, key)
        sel = ((q_blk < n_top) | sel) & causal
        sel_ref[0, 0] = jnp.where(sel, 0.0, -MASK_BIG).astype(bf16)


def _cmp(nqu, kcmp, vcmp, gates, n_cmp):
    n_batch, _, seq_len, _ = nqu.shape
    n_rows = kcmp.shape[2]
    n_sel = seq_len // SEL_BLOCK
    assert 3 <= n_sel <= SEL_LANES
    tq = 1024
    c0 = np.arange(n_rows) * CMP_STRIDE
    s0 = np.arange(SEL_LANES) * SEL_BLOCK
    overlap = np.clip(np.minimum(c0[:, None] + CMP_LEN, s0[None, :] + SEL_BLOCK)
                      - np.maximum(c0[:, None], s0[None, :]), 0, None).astype(np.float32) / CMP_LEN
    overlap[n_cmp:, :] = 0.0
    overlap[:, n_sel:] = 0.0
    return pl.pallas_call(
        functools.partial(_cmp_kernel, tq=tq, n_rows=n_rows, n_cmp=n_cmp, n_sel=n_sel),
        out_shape=[
            jax.ShapeDtypeStruct((n_batch, seq_len, NSA_W), bf16),
            jax.ShapeDtypeStruct((n_batch, NSA_KV_HEADS, seq_len, SEL_LANES), bf16),
        ],
        grid=(n_batch, NSA_KV_HEADS, seq_len // tq),
        in_specs=[
            pl.BlockSpec((1, NSA_GROUP, tq, HEAD_DIM), lambda b, g, i: (b, g, i, 0)),
            pl.BlockSpec((1, 1, n_rows, HEAD_DIM), lambda b, g, i: (b, g, 0, 0)),
            pl.BlockSpec((1, 1, n_rows, HEAD_DIM), lambda b, g, i: (b, g, 0, 0)),
            pl.BlockSpec((1, tq, LANES), lambda b, g, i: (b, i, 0)),
            pl.BlockSpec((n_rows, SEL_LANES), lambda b, g, i: (0, 0)),
        ],
        out_specs=[
            pl.BlockSpec((1, tq, NSA_GROUP * HEAD_DIM), lambda b, g, i: (b, i, g)),
            pl.BlockSpec((1, 1, tq, SEL_LANES), lambda b, g, i: (b, g, i, 0)),
        ],
        compiler_params=_params("arbitrary", "arbitrary", "arbitrary"),
        name="cmp",
    )(nqu, kcmp, vcmp, gates, jnp.asarray(overlap, dtype=bf16))


def _write_gated(o_ref, gate_ref, out, g, branch, tq):
    gates = gate_ref[0]
    lane = lax.broadcasted_iota(jnp.int32, (tq, LANES), 1)
    for h in range(NSA_GROUP):
        gcol = FOX_HEADS + (g * NSA_GROUP + h) * N_BRANCH + branch
        gh = jnp.sum(jnp.where(lane == gcol, gates, 0.0), axis=-1, keepdims=True)
        o_ref[0, :, h * HEAD_DIM:(h + 1) * HEAD_DIM] = (gh * out[h * tq:(h + 1) * tq]).astype(bf16)


def _slc_kernel(q_ref, sel_ref, qn_ref, seln_ref, k_ref, vt_ref, gate_ref, *rest, tq, tk, n_cast_w):
    o_ref = rest[n_cast_w]
    bufs = rest[2 * n_cast_w + 1:]
    for w32_ref, w16_ref in zip(rest[:n_cast_w], rest[n_cast_w + 1:2 * n_cast_w + 1]):
        w16_ref[...] = w32_ref[...].astype(bf16)
    i = pl.program_id(1)
    hg = NSA_GROUP
    n_q = hg * tq
    q0 = i * tq
    jd = q0 // tk

    def diag_mask(s):
        kp = jd * tk + lax.broadcasted_iota(jnp.int32, (tk, n_q), 0)
        t = q0 + (lax.broadcasted_iota(jnp.int32, (tk, n_q), 1) & (tq - 1))
        return jnp.where(kp <= t, s, -jnp.inf)

    def chain(g):
        def q_aug(qr, sr):
            q = qr[0, g * hg:(g + 1) * hg].reshape(n_q, HEAD_DIM)
            return jnp.concatenate([q, jnp.concatenate([sr[0, g]] * hg, axis=0)], axis=1)

        qa = q_aug(q_ref, sel_ref)

        def scores(j):
            return _dot_nt(k_ref[0, g, pl.ds(pl.multiple_of(j * tk, tk), tk), :], qa)

        def next_first():
            return _dot_nt(k_ref[0, g, 0:tk, :], q_aug(qn_ref, seln_ref))

        def emit(out):
            gates = gate_ref[0]
            lane = lax.broadcasted_iota(jnp.int32, (tq, LANES), 1)
            for h in range(hg):
                head = g * hg + h
                gh = jnp.sum(jnp.where(lane == FOX_HEADS + head * N_BRANCH + 1, gates, 0.0),
                             axis=-1, keepdims=True)
                o_ref[0, :, head * HEAD_DIM:(head + 1) * HEAD_DIM] = (
                    gh * out[:, h * tq:(h + 1) * tq].T).astype(bf16)

        return _Chain(scores, lambda j: vt_ref[0, g, j], lambda j: None, diag_mask, emit, next_first,
                      bufs[2 * g], bufs[2 * g + 1])

    _flash_sweep(jd, [chain(g) for g in range(NSA_KV_HEADS)], n_q, i == 0)


def _slc(nqr, selneg, kaug, vst, gates, cast_weights):
    n_batch, _, seq_len, _ = nqr.shape
    tq, tk = 256, KV_BLOCK
    nkb = seq_len // tk
    assert tq & (tq - 1) == 0
    n_q = NSA_GROUP * tq
    n_i = seq_len // tq
    cast_in, cast_out, cast_shapes = _cast_side_job(cast_weights, n_batch * n_i, lambda b, i: b * n_i + i)
    return pl.pallas_call(
        functools.partial(_slc_kernel, tq=tq, tk=tk, n_cast_w=len(cast_weights)),
        out_shape=[jax.ShapeDtypeStruct((n_batch, seq_len, NSA_W), bf16)] + cast_shapes,
        grid=(n_batch, n_i),
        in_specs=[
            pl.BlockSpec((1, NSA_HEADS, tq, HEAD_DIM), lambda b, i: (b, 0, i, 0)),
            pl.BlockSpec((1, NSA_KV_HEADS, tq, SEL_LANES), lambda b, i: (b, 0, i, 0)),
            pl.BlockSpec((1, NSA_HEADS, tq, HEAD_DIM), lambda b, i: (b, 0, jnp.minimum(i + 1, n_i - 1), 0)),
            pl.BlockSpec((1, NSA_KV_HEADS, tq, SEL_LANES), lambda b, i: (b, 0, jnp.minimum(i + 1, n_i - 1), 0)),
            pl.BlockSpec((1, NSA_KV_HEADS, seq_len, HEAD_DIM + SEL_LANES), lambda b, i: (b, 0, 0, 0)),
            pl.BlockSpec((1, NSA_KV_HEADS, nkb, HEAD_DIM, tk), lambda b, i: (b, 0, 0, 0, 0)),
            pl.BlockSpec((1, tq, LANES), lambda b, i: (b, i, 0)),
        ] + cast_in,
        out_specs=[pl.BlockSpec((1, tq, NSA_W), lambda b, i: (b, i, 0))] + cast_out,
        scratch_shapes=[pltpu.VMEM((2, tk, n_q), f32), pltpu.VMEM((2, tk, n_q), bf16)] * NSA_KV_HEADS,
        compiler_params=_params("arbitrary", "arbitrary"),
        name="slc",
    )(nqr, selneg, nqr, selneg, kaug, vst, gates, *cast_weights)


def _win_kernel(q_ref, k_ref, vt_ref, gate_ref, o_ref, *, tq, n_sub):
    i = pl.program_id(2)
    g = pl.program_id(1)
    hg = NSA_GROUP
    span = WINDOW + tq
    q0s = [(i * n_sub + u) * tq for u in range(n_sub)]
    k0s = [pl.multiple_of(jnp.maximum(q0 - WINDOW, 0), tq) for q0 in q0s]
    scores = []
    for u in range(n_sub):
        q = q_ref[0, :, u * tq:(u + 1) * tq, :].reshape(hg * tq, HEAD_DIM)
        scores.append(_dot_nt(k_ref[0, 0, pl.ds(k0s[u], span), :], q))
    probs, denoms = [], []
    for u in range(n_sub):
        kp = k0s[u] + lax.broadcasted_iota(jnp.int32, (span, tq), 0)
        t = q0s[u] + lax.broadcasted_iota(jnp.int32, (span, tq), 1)
        diff = t - kp
        bias = jnp.where((diff >= 0) & (diff < WINDOW), 0.0, -jnp.inf)
        s = jnp.concatenate([scores[u][:, h * tq:(h + 1) * tq] + bias for h in range(hg)], axis=1)
        e = jnp.exp2(s - jnp.max(s, axis=0, keepdims=True))
        denoms.append(jnp.sum(e, axis=0, keepdims=True))
        probs.append(e.astype(bf16))
    gates = gate_ref[0]
    lane = lax.broadcasted_iota(jnp.int32, (tq, LANES), 1)
    for u in range(n_sub):
        jb = k0s[u] // tq
        vt = jnp.concatenate([vt_ref[0, 0, jb + c] for c in range(span // tq)], axis=1)
        out = _dot(vt, probs[u]) / denoms[u]
        for h in range(hg):
            gcol = FOX_HEADS + (g * hg + h) * N_BRANCH + 2
            gh = jnp.sum(jnp.where(lane == gcol, gates[u * tq:(u + 1) * tq], 0.0), axis=-1, keepdims=True)
            o_ref[0, u * tq:(u + 1) * tq, h * HEAD_DIM:(h + 1) * HEAD_DIM] = (
                gh * out[:, h * tq:(h + 1) * tq].T).astype(bf16)


def _win(nqr, kw, vwt, gates):
    n_batch, _, seq_len, _ = nqr.shape
    tq, n_sub = WIN_TILE, 4
    assert seq_len >= WINDOW + tq and WINDOW % tq == 0
    return pl.pallas_call(
        functools.partial(_win_kernel, tq=tq, n_sub=n_sub),
        out_shape=jax.ShapeDtypeStruct((n_batch, seq_len, NSA_W), bf16),
        grid=(n_batch, NSA_KV_HEADS, seq_len // (tq * n_sub)),
        in_specs=[
            pl.BlockSpec((1, NSA_GROUP, tq * n_sub, HEAD_DIM), lambda b, g, i: (b, g, i, 0)),
            pl.BlockSpec((1, 1, seq_len, HEAD_DIM), lambda b, g, i: (b, g, 0, 0)),
            pl.BlockSpec((1, 1, seq_len // tq, HEAD_DIM, tq), lambda b, g, i: (b, g, 0, 0, 0)),
            pl.BlockSpec((1, tq * n_sub, LANES), lambda b, g, i: (b, i, 0)),
        ],
        out_specs=pl.BlockSpec((1, tq * n_sub, NSA_GROUP * HEAD_DIM), lambda b, g, i: (b, i, g)),
        compiler_params=_params("arbitrary", "arbitrary", "arbitrary"),
        name="win",
    )(nqr, kw, vwt, gates)


def _outproj_kernel(fox_ref, c_ref, s_ref, w_ref, x_ref, mod_ref, wo_ref, o_ref):
    nsa = c_ref[...].astype(f32) + s_ref[...].astype(f32) + w_ref[...].astype(f32)
    a = jnp.concatenate([fox_ref[...], nsa.astype(bf16)], axis=1)
    o_ref[...] = x_ref[...] + mod_ref[0][2:3] * _dot(a, wo_ref[...])


def _outproj(ofox, ocmp, oslc, owin, x2d, mod3, wo_bf16, seq_len):
    m_rows = x2d.shape[0]
    tm = 512
    per_b = seq_len // tm
    half = pl.BlockSpec((tm, FOX_W), lambda i: (i, 0))
    return pl.pallas_call(
        _outproj_kernel,
        out_shape=jax.ShapeDtypeStruct((m_rows, D_MODEL), f32),
        grid=(m_rows // tm,),
        in_specs=[
            half, half, half, half,
            pl.BlockSpec((tm, D_MODEL), lambda i: (i, 0)),
            pl.BlockSpec((1, 6, D_MODEL), lambda i: (i // per_b, 0, 0)),
            pl.BlockSpec((D_MODEL, D_MODEL), lambda i: (0, 0)),
        ],
        out_specs=pl.BlockSpec((tm, D_MODEL), lambda i: (i, 0)),
        compiler_params=_params("arbitrary"),
        name="outproj",
    )(ofox, ocmp, oslc, owin, x2d, mod3, wo_bf16)


def _mlp_kernel(x_ref, mod_ref, g_ref, wu_ref, wd_ref, o_ref, h_ref, acc_ref):
    f = pl.program_id(1)

    @pl.when(f == 0)
    def _():
        md = mod_ref[0]
        y = _rms(x_ref[...], g_ref[...])
        h_ref[...] = (y * (1.0 + md[4:5]) + md[3:4]).astype(bf16)
        acc_ref[...] = jnp.zeros_like(acc_ref)

    u = jnp.maximum(_dot(h_ref[...], wu_ref[...]), 0.0)
    acc_ref[...] += _dot((u * u).astype(bf16), wd_ref[...])

    @pl.when(f == pl.num_programs(1) - 1)
    def _():
        o_ref[...] = x_ref[...] + mod_ref[0][5:6] * acc_ref[...]


def _mlp(x2d, mod3, norm_g, wu_bf16, wd_bf16, seq_len):
    m_rows = x2d.shape[0]
    tm, tf = 512, 1024
    per_b = seq_len // tm
    return pl.pallas_call(
        _mlp_kernel,
        out_shape=jax.ShapeDtypeStruct((m_rows, D_MODEL), f32),
        grid=(m_rows // tm, D_FF // tf),
        in_specs=[
            pl.BlockSpec((tm, D_MODEL), lambda i, f: (i, 0)),
            pl.BlockSpec((1, 6, D_MODEL), lambda i, f: (i // per_b, 0, 0)),
            pl.BlockSpec((1, D_MODEL), lambda i, f: (0, 0)),
            pl.BlockSpec((D_MODEL, tf), lambda i, f: (0, f)),
            pl.BlockSpec((tf, D_MODEL), lambda i, f: (f, 0)),
        ],
        out_specs=pl.BlockSpec((tm, D_MODEL), lambda i, f: (i, 0)),
        scratch_shapes=[pltpu.VMEM((tm, D_MODEL), bf16), pltpu.VMEM((tm, D_MODEL), f32)],
        compiler_params=_params("arbitrary", "arbitrary"),
        name="mlp",
    )(x2d, mod3, norm_g, wu_bf16, wd_bf16)


def _layer(x, c, w_ada, b_ada, norm1_g, w_in, b_forget, fox_q_norm, fox_k_norm, nsa_q_norm,
           cmp_k_norm, slc_k_norm, win_k_norm, cmp_pe_k, cmp_w1_k, cmp_w2_k, cmp_pe_v, cmp_w1_v,
           cmp_w2_v, w_out, norm2_g, w_up, w_down):
    n_batch, seq_len, _ = x.shape
    n_cmp = (seq_len - CMP_LEN) // CMP_STRIDE + 1
    row = lambda v: v.reshape(1, -1)

    half = HEAD_DIM // 2
    inv_freq = ROPE_THETA ** (-jnp.arange(half, dtype=f32) / half)
    inv_freq = jnp.concatenate([inv_freq, inv_freq]).reshape(1, HEAD_DIM)

    w_t = w_in.T
    w_a = w_t[:W_IN_Z0].astype(bf16)
    w_b = w_t[W_IN_NQ0:W_IN_GZ0].astype(bf16)
    w_s = jnp.concatenate([w_t[W_IN_Z0:W_IN_NQ0], w_t[W_IN_GZ0:],
                           jnp.zeros((LANES - FOX_HEADS - N_BRANCH * NSA_HEADS, D_MODEL), w_t.dtype)],
                          axis=0).astype(bf16)

    mod3 = _ada(c, w_ada, b_ada).reshape(n_batch, 6, D_MODEL)
    x2d = x.reshape(n_batch * seq_len, D_MODEL)
    (fq, fk, fvt, cin, nqu, nqr, kc, vc, kaug, vst, kw, vw, gates) = _prep(
        x2d, mod3, row(norm1_g), w_a, w_b, w_s, n_batch, seq_len, row(fox_q_norm), row(fox_k_norm), row(nsa_q_norm), row(slc_k_norm),
        row(win_k_norm), jnp.pad(b_forget, (0, LANES - FOX_HEADS)).reshape(1, LANES), inv_freq)
    kcmp = _compress(kc, cmp_pe_k, cmp_w1_k.astype(bf16), cmp_w2_k.astype(bf16), row(cmp_k_norm))
    vcmp = _compress(vc, cmp_pe_v, cmp_w1_v.astype(bf16), cmp_w2_v.astype(bf16))
    ofox = _fox(fq, fk, fvt, cin)
    ocmp, selneg = _cmp(nqu, kcmp, vcmp, gates, n_cmp)
    oslc, w_up16, w_down16, w_out16 = _slc(nqr, selneg, kaug, vst, gates, [w_up, w_down, w_out])
    owin = _win(nqr, kw, vw, gates)
    x1 = _outproj(ofox.reshape(-1, FOX_W), ocmp.reshape(-1, NSA_W), oslc.reshape(-1, NSA_W),
                  owin.reshape(-1, NSA_W), x2d, mod3, w_out16, seq_len)
    x2 = _mlp(x1, mod3, row(norm2_g), w_up16, w_down16, seq_len)
    return x2.reshape(n_batch, seq_len, D_MODEL)


def kernel(x, c, w_ada, b_ada, norm1_g, w_in, b_forget, fox_q_norm, fox_k_norm, nsa_q_norm, cmp_k_norm,
           slc_k_norm, win_k_norm, cmp_pe_k, cmp_w1_k, cmp_w2_k, cmp_pe_v, cmp_w1_v, cmp_w2_v, w_out,
           norm2_g, w_up, w_down):
    depth = w_ada.shape[0]
    for l in range(depth):
        x = _layer(x, c, w_ada[l], b_ada[l], norm1_g[l], w_in[l], b_forget[l], fox_q_norm[l], fox_k_norm[l],
                   nsa_q_norm[l], cmp_k_norm[l], slc_k_norm[l], win_k_norm[l], cmp_pe_k[l], cmp_w1_k[l],
                   cmp_w2_k[l], cmp_pe_v[l], cmp_w1_v[l], cmp_w2_v[l], w_out[l], norm2_g[l], w_up[l],
                   w_down[l])
    return x
```

```python
import functools
import math
from typing import Any, Callable, NamedTuple

import numpy as np
import jax
import jax.numpy as jnp
from jax import lax
from jax.experimental import pallas as pl
from jax.experimental.pallas import tpu as pltpu

D_MODEL = 2048
HEAD_DIM = 128
FOX_HEADS = 8
NSA_HEADS = 8
NSA_KV_HEADS = 2
NSA_GROUP = NSA_HEADS // NSA_KV_HEADS
N_BRANCH = 3
D_FF = 4 * D_MODEL
ROPE_THETA = 10000.0
CMP_LEN = 32
CMP_STRIDE = 16
CMP_SHIFT = 4
CMP_HIDDEN = 2 * HEAD_DIM
SEL_BLOCK = 64
SEL_SHIFT = 6
SEL_TOPK = 16
WINDOW = 512
NORM_EPS = 1e-6
ATTN_SCALE = HEAD_DIM ** -0.5
FOX_W = FOX_HEADS * HEAD_DIM
NSA_W = NSA_HEADS * HEAD_DIM
KV_W = NSA_KV_HEADS * HEAD_DIM

LANES = 128
SEL_LANES = LANES
MASK_BIG = 1e30
KV_BLOCK = 512
WIN_TILE = 128
LOG2E = math.log2(math.e)

COL_FQ = 0
COL_FK = COL_FQ + FOX_W
COL_FV = COL_FK + FOX_W
COL_NQ = COL_FV + FOX_W
COL_KC = COL_NQ + NSA_W
COL_VC = COL_KC + KV_W
COL_KS = COL_VC + KV_W
COL_VS = COL_KS + KV_W
COL_KW = COL_VS + KV_W
COL_VW = COL_KW + KV_W
COL_SMALL = COL_VW + KV_W
W_IN_Z0 = 3 * FOX_W
W_IN_NQ0 = W_IN_Z0 + FOX_HEADS
W_IN_GZ0 = W_IN_NQ0 + NSA_W + 6 * KV_W

VMEM_LIMIT = 56 * 1024 * 1024
VMEM_LIMIT_PREP = 60 * 1024 * 1024

f32 = jnp.float32
bf16 = jnp.bfloat16


def _params(*sem):
    return pltpu.CompilerParams(dimension_semantics=sem, vmem_limit_bytes=VMEM_LIMIT)


def _dot_nt(a, b):
    return lax.dot_general(a, b, (((1,), (1,)), ((), ())), preferred_element_type=f32)


def _dot(a, b):
    return jnp.dot(a, b, preferred_element_type=f32)


def _split3(x):
    hi = x.astype(bf16)
    r1 = x - hi.astype(f32)
    mid = r1.astype(bf16)
    lo = (r1 - mid.astype(f32)).astype(bf16)
    return hi, mid, lo


def _dot_f32_by_exact(x, w_bf16):
    hi, mid, lo = _split3(x)
    return _dot(hi, w_bf16) + (_dot(mid, w_bf16) + _dot(lo, w_bf16))


def _rms(x, gain):
    ms = jnp.mean(x * x, axis=-1, keepdims=True)
    return x * lax.rsqrt(ms + NORM_EPS) * gain


def _ada_kernel(ct_ref, w_ref, b_ref, o_ref, *, n_batch, k_chunk):
    ct = ct_ref[...]
    act = ct * jax.nn.sigmoid(ct)
    rows = []
    for b in range(n_batch):
        col = act[:, b:b + 1]
        acc = b_ref[...]
        for k0 in range(0, D_MODEL, k_chunk):
            acc = acc + jnp.sum(w_ref[k0:k0 + k_chunk, :] * col[k0:k0 + k_chunk], axis=0, keepdims=True)
        rows.append(acc)
    o_ref[...] = jnp.concatenate(rows, axis=0)


def _ada(c, w_ada, b_ada):
    n_batch = c.shape[0]
    n_out = w_ada.shape[1]
    tn = 1024
    return pl.pallas_call(
        functools.partial(_ada_kernel, n_batch=n_batch, k_chunk=256),
        out_shape=jax.ShapeDtypeStruct((n_batch, n_out), f32),
        grid=(n_out // tn,),
        in_specs=[
            pl.BlockSpec((D_MODEL, n_batch), lambda j: (0, 0)),
            pl.BlockSpec((D_MODEL, tn), lambda j: (0, j)),
            pl.BlockSpec((1, tn), lambda j: (0, j)),
        ],
        out_specs=pl.BlockSpec((n_batch, tn), lambda j: (0, j)),
        compiler_params=_params("arbitrary"),
        name="ada",
    )(c.T, w_ada, b_ada.reshape(1, n_out))


def _cast_side_job(weights, n_steps, step_of):
    n_cast = max(c for c in range(1, n_steps + 1) if all(w.shape[0] % (16 * c) == 0 for w in weights))
    idx = lambda *ids: (jnp.minimum(step_of(*ids), n_cast - 1), 0)
    specs = [pl.BlockSpec((w.shape[0] // n_cast, w.shape[1]), idx) for w in weights]
    return specs, specs, [jax.ShapeDtypeStruct(w.shape, bf16) for w in weights]


def _prep_kernel(x_ref, mod_ref, g1_ref, wa_ref, wb_ref, ws_ref,
                 gq_ref, gk_ref, gn_ref, gs_ref, gw_ref, bf_ref, inv_ref,
                 fq_ref, fk_ref, fvt_ref, cin_ref, nqu_ref, nqr_ref, kc_ref, vc_ref,
                 ks_ref, vst_ref, kw_ref, vw_ref, gate_ref, carry_ref, rot_ref, *, tm):
    i = pl.program_id(1)

    @pl.when(i == 0)
    def _():
        carry_ref[...] = jnp.zeros_like(carry_ref)

    md = mod_ref[0]
    hn = (_rms(x_ref[...], g1_ref[...]) * (1.0 + md[1:2]) + md[0:1]).astype(bf16)
    groups = {}

    def head(col, h):
        base, w_ref, r0 = next((b, w, r) for b, w, r in (
            (COL_KC, wb_ref, NSA_W), (COL_NQ, wb_ref, 0), (COL_FV, wa_ref, 2 * FOX_W),
            (COL_FK, wa_ref, FOX_W), (COL_FQ, wa_ref, 0)) if col >= b)
        if base not in groups:
            n = (COL_SMALL - COL_KC) if base == COL_KC else FOX_W
            groups[base] = _dot_nt(hn, w_ref[r0:r0 + n, :])
        c0 = col - base + h * HEAD_DIM
        return groups[base][:, c0:c0 + HEAD_DIM]

    row = lax.broadcasted_iota(jnp.int32, (tm, LANES), 0)
    lane = lax.broadcasted_iota(jnp.int32, (tm, LANES), 1)
    pos = i * tm + row

    @pl.when((pl.program_id(0) == 0) & (i == 0))
    def _():
        ang_row = row.astype(f32) * inv_ref[...]
        rot_ref[0] = jnp.cos(ang_row)
        rot_ref[1] = jnp.sin(ang_row)

    ang0 = (i * tm).astype(f32) * inv_ref[...]
    cos0, sin0 = jnp.cos(ang0), jnp.sin(ang0)
    cos = cos0 * rot_ref[0] - sin0 * rot_ref[1]
    sin = sin0 * rot_ref[0] + cos0 * rot_ref[1]
    sin_signed = jnp.where(lane < HEAD_DIM // 2, -sin, sin)

    def rope(x):
        return x * cos + pltpu.roll(x, HEAD_DIM // 2, 1) * sin_signed

    small = _dot_nt(hn, ws_ref[...])
    z = small + bf_ref[...]
    logf = jnp.minimum(z, 0.0) - jnp.log1p(jnp.exp(-jnp.abs(z)))
    t_idx = lax.broadcasted_iota(jnp.int32, (tm, tm), 0)
    s_idx = lax.broadcasted_iota(jnp.int32, (tm, tm), 1)
    tri = jnp.where(s_idx <= t_idx, 1.0, 0.0).astype(bf16)
    hi, mid, lo = _split3(logf)
    local = _dot(tri, hi) + (_dot(tri, mid) + _dot(tri, lo))
    cin_ref[0, 0] = carry_ref[0:1, :]
    carry_ref[...] = carry_ref[...] + local[tm - 1:tm, :]
    b_hi, b_mid, b_lo = (v.astype(f32) for v in _split3(local * (-LOG2E)))
    ones3 = jnp.where(lane < 3, 1.0, 0.0).astype(bf16)

    for h in range(FOX_HEADS):
        q = _rms(head(COL_FQ, h), gq_ref[...] * (ATTN_SCALE * LOG2E))
        fq_ref[0, h] = jnp.concatenate([q.astype(bf16), ones3], axis=1)
    for h in range(FOX_HEADS):
        k = _rms(head(COL_FK, h), gk_ref[...])
        bias = jnp.where(lane == 0, b_hi[:, h:h + 1],
                         jnp.where(lane == 1, b_mid[:, h:h + 1],
                                   jnp.where(lane == 2, b_lo[:, h:h + 1], 0.0)))
        fk_ref[0, h] = jnp.concatenate([k.astype(bf16), bias.astype(bf16)], axis=1)

    gate_ref[0] = jax.nn.sigmoid(small)

    for h in range(NSA_HEADS):
        qn = _rms(head(COL_NQ, h), gn_ref[...] * (ATTN_SCALE * LOG2E))
        nqu_ref[0, h] = qn.astype(bf16)
        nqr_ref[0, h] = rope(qn).astype(bf16)
    onehot = jnp.where(lane == lax.shift_right_logical(pos, SEL_SHIFT), 1.0, 0.0).astype(bf16)
    for g in range(NSA_KV_HEADS):
        kc_ref[0, g] = head(COL_KC, g)
        vc_ref[0, g] = head(COL_VC, g)
        ks = rope(_rms(head(COL_KS, g), gs_ref[...])).astype(bf16)
        ks_ref[0, g] = jnp.concatenate([ks, onehot], axis=1)
        vst_ref[0, g, 0] = head(COL_VS, g).T.astype(bf16)
        kw_ref[0, g] = rope(_rms(head(COL_KW, g), gw_ref[...])).astype(bf16)
        vwt = head(COL_VW, g).T.astype(bf16)
        for c in range(tm // WIN_TILE):
            vw_ref[0, g, c] = vwt[:, c * WIN_TILE:(c + 1) * WIN_TILE]
    for h in range(FOX_HEADS):
        fvt_ref[0, h, 0] = head(COL_FV, h).T.astype(bf16)


def _prep(x2d, mod3, norm_g, wa, wb, ws, n_batch, seq_len, gq, gk, gn, gs, gw, b_forget_row, inv_freq):
    resident = lambda w: pl.BlockSpec(w.shape, lambda b, i: (0, 0), pipeline_mode=pl.Buffered(1))
    tm = KV_BLOCK
    per_b = seq_len // tm
    hshape = lambda n, w, dt: jax.ShapeDtypeStruct((n_batch, n, seq_len, w), dt)
    hspec = lambda n, w: pl.BlockSpec((1, n, tm, w), lambda b, i: (b, 0, i, 0))
    tshape = lambda n: jax.ShapeDtypeStruct((n_batch, n, per_b, HEAD_DIM, tm), bf16)
    tspec = lambda n: pl.BlockSpec((1, n, 1, HEAD_DIM, tm), lambda b, i: (b, 0, i, 0, 0))
    vec = pl.BlockSpec((1, LANES), lambda b, i: (0, 0))
    return pl.pallas_call(
        functools.partial(_prep_kernel, tm=tm),
        out_shape=[
            hshape(FOX_HEADS, 2 * HEAD_DIM, bf16), hshape(FOX_HEADS, 2 * HEAD_DIM, bf16), tshape(FOX_HEADS),
            jax.ShapeDtypeStruct((n_batch, per_b, 1, LANES), f32),
            hshape(NSA_HEADS, HEAD_DIM, bf16), hshape(NSA_HEADS, HEAD_DIM, bf16),
            hshape(NSA_KV_HEADS, HEAD_DIM, f32), hshape(NSA_KV_HEADS, HEAD_DIM, f32),
            hshape(NSA_KV_HEADS, HEAD_DIM + SEL_LANES, bf16), tshape(NSA_KV_HEADS),
            hshape(NSA_KV_HEADS, HEAD_DIM, bf16),
            jax.ShapeDtypeStruct((n_batch, NSA_KV_HEADS, seq_len // WIN_TILE, HEAD_DIM, WIN_TILE), bf16),
            jax.ShapeDtypeStruct((n_batch, seq_len, LANES), f32),
        ],
        grid=(n_batch, per_b),
        in_specs=[
            pl.BlockSpec((tm, D_MODEL), lambda b, i: (b * per_b + i, 0)),
            pl.BlockSpec((1, 6, D_MODEL), lambda b, i: (b, 0, 0)),
            pl.BlockSpec((1, D_MODEL), lambda b, i: (0, 0)),
            resident(wa), resident(wb), resident(ws),
            vec, vec, vec, vec, vec, vec, vec,
        ],
        out_specs=[
            hspec(FOX_HEADS, 2 * HEAD_DIM), hspec(FOX_HEADS, 2 * HEAD_DIM), tspec(FOX_HEADS),
            pl.BlockSpec((1, 1, 1, LANES), lambda b, i: (b, i, 0, 0)),
            hspec(NSA_HEADS, HEAD_DIM), hspec(NSA_HEADS, HEAD_DIM),
            hspec(NSA_KV_HEADS, HEAD_DIM), hspec(NSA_KV_HEADS, HEAD_DIM),
            hspec(NSA_KV_HEADS, HEAD_DIM + SEL_LANES), tspec(NSA_KV_HEADS),
            hspec(NSA_KV_HEADS, HEAD_DIM),
            pl.BlockSpec((1, NSA_KV_HEADS, tm // WIN_TILE, HEAD_DIM, WIN_TILE), lambda b, i: (b, 0, i, 0, 0)),
            pl.BlockSpec((1, tm, LANES), lambda b, i: (b, i, 0)),
        ],
        scratch_shapes=[pltpu.VMEM((8, LANES), f32), pltpu.VMEM((2, tm, LANES), f32)],
        compiler_params=pltpu.CompilerParams(dimension_semantics=("arbitrary", "arbitrary"),
                                             vmem_limit_bytes=VMEM_LIMIT_PREP),
        name="prep",
    )(x2d, mod3, norm_g, wa, wb, ws, gq, gk, gn, gs, gw, b_forget_row, inv_freq)


def _compress_kernel(x_ref, pe_ref, w1_ref, w2_ref, *rest, n_rows, do_norm):
    o_ref = rest[-1]
    a = jnp.zeros((n_rows, CMP_HIDDEN), f32)
    b = jnp.zeros((n_rows, CMP_HIDDEN), f32)
    for l in range(CMP_STRIDE):
        x = x_ref[0, 0, pl.ds(l, n_rows, stride=CMP_STRIDE), :]
        a = a + _dot((x + pe_ref[l:l + 1, :]).astype(bf16), w1_ref[l * HEAD_DIM:(l + 1) * HEAD_DIM, :])
        lb = CMP_STRIDE + l
        b = b + _dot((x + pe_ref[lb:lb + 1, :]).astype(bf16), w1_ref[lb * HEAD_DIM:(lb + 1) * HEAD_DIM, :])
    pre = a + pltpu.roll(b, n_rows - 1, 0)
    hid = pre * jax.nn.sigmoid(pre)
    out = _dot(hid.astype(bf16), w2_ref[...])
    if do_norm:
        out = _rms(out, rest[0][...])
    o_ref[0, 0] = out.astype(bf16)


def _compress(x4, pe, w1_bf16, w2_bf16, gain=None):
    n_batch, n_g, seq_len, _ = x4.shape
    n_rows = seq_len // CMP_STRIDE
    do_norm = gain is not None
    return pl.pallas_call(
        functools.partial(_compress_kernel, n_rows=n_rows, do_norm=do_norm),
        out_shape=jax.ShapeDtypeStruct((n_batch, n_g, n_rows, HEAD_DIM), bf16),
        grid=(n_batch, n_g),
        in_specs=[
            pl.BlockSpec((1, 1, seq_len, HEAD_DIM), lambda b, g: (b, g, 0, 0)),
            pl.BlockSpec((CMP_LEN, HEAD_DIM), lambda b, g: (0, 0)),
            pl.BlockSpec((CMP_LEN * HEAD_DIM, CMP_HIDDEN), lambda b, g: (0, 0)),
            pl.BlockSpec((CMP_HIDDEN, HEAD_DIM), lambda b, g: (0, 0)),
        ] + ([pl.BlockSpec((1, HEAD_DIM), lambda b, g: (0, 0))] if do_norm else []),
        out_specs=pl.BlockSpec((1, 1, n_rows, HEAD_DIM), lambda b, g: (b, g, 0, 0)),
        compiler_params=_params("arbitrary", "arbitrary"),
        name="compress",
    )(x4, pe, w1_bf16, w2_bf16, *([gain] if do_norm else []))


def _softmax_block(s, d, m_prev, l_prev):
    m_blk = jnp.max(s, axis=0, keepdims=True)
    m_new = jnp.maximum(m_prev, m_blk if d is None else m_blk + d)
    p = jnp.exp2(s - (m_new if d is None else m_new - d))
    alpha = jnp.exp2(m_prev - m_new)
    l_new = alpha * l_prev + jnp.sum(p, axis=0, keepdims=True)
    return m_new, l_new, alpha, p.astype(bf16)


class _Chain(NamedTuple):
    scores: Callable
    values: Callable
    offset: Callable
    diag_mask: Callable
    emit: Callable
    next_first: Callable
    s_buf: Any
    p_buf: Any


def _flash_sweep(n_full, chains, n_q, is_first):
    @pl.when(is_first)
    def _():
        for c in chains:
            c.s_buf[0] = c.scores(0)

    for c in chains:
        c.p_buf[1] = jnp.zeros(c.p_buf.shape[1:], c.p_buf.dtype)

    def accumulate(c, j, slot, alpha, acc):
        return alpha * acc + _dot(c.values(jnp.maximum(j, 0)), c.p_buf[slot])

    def step(j, cur, states):
        out = []
        for c, (m, l, acc, alpha_prev) in zip(chains, states):
            acc = accumulate(c, j - 1, 1 - cur, alpha_prev, acc)
            m, l, alpha, p = _softmax_block(c.s_buf[cur], c.offset(j), m, l)
            c.p_buf[cur] = p
            c.s_buf[1 - cur] = c.scores(j + 1)
            out.append((m, l, acc, alpha))
        return tuple(out)

    def finish(cur, states):
        nxt = [c.next_first() for c in chains]
        for c, s_next, (m, l, acc, alpha_prev) in zip(chains, nxt, states):
            if cur == 1:
                c.s_buf[0] = s_next
            acc = accumulate(c, n_full - 1, 1 - cur, alpha_prev, acc)
            m, l, alpha, p = _softmax_block(c.diag_mask(c.s_buf[cur]), None, m, l)
            if cur == 0:
                c.s_buf[0] = s_next
            acc = alpha * acc + _dot(c.values(n_full), p)
            c.emit(acc / l)

    init = tuple((jnp.full((1, n_q), -jnp.inf, f32), jnp.zeros((1, n_q), f32),
                  jnp.zeros((HEAD_DIM, n_q), f32), jnp.ones((1, n_q), f32)) for _ in chains)
    states = lax.fori_loop(0, n_full // 2, lambda jj, st: step(2 * jj + 1, 1, step(2 * jj, 0, st)), init)

    @pl.when(n_full % 2 == 0)
    def _():
        finish(0, states)

    @pl.when(n_full % 2 == 1)
    def _():
        finish(1, step(n_full - 1, 0, states))


def _fox_kernel(q_ref, qn_ref, k_ref, vt_ref, c_ref, o_ref, *bufs, tq, n_heads):
    hb = pl.program_id(1)
    i = pl.program_id(2)
    lane = lax.broadcasted_iota(jnp.int32, (1, LANES), 1)

    def diag_mask(s):
        kk = lax.broadcasted_iota(jnp.int32, (tq, tq), 0)
        qq = lax.broadcasted_iota(jnp.int32, (tq, tq), 1)
        return jnp.where(kk <= qq, s, -jnp.inf)

    def chain(u):
        qa = q_ref[0, u]

        def cin(j):
            return jnp.sum(jnp.where(lane == hb * n_heads + u, c_ref[0, j], 0.0), axis=-1, keepdims=True)

        ci = cin(i)

        def scores(j):
            return _dot_nt(k_ref[0, u, pl.ds(pl.multiple_of(j * tq, tq), tq), :], qa)

        def emit(out_t):
            o_ref[0, :, u * HEAD_DIM:(u + 1) * HEAD_DIM] = out_t.T.astype(bf16)

        def next_first():
            return _dot_nt(k_ref[0, u, 0:tq, :], qn_ref[0, u])

        return _Chain(scores, lambda j: vt_ref[0, u, j], lambda j: (ci - cin(j)) * LOG2E, diag_mask, emit,
                      next_first, bufs[2 * u], bufs[2 * u + 1])

    _flash_sweep(i, [chain(u) for u in range(n_heads)], tq, i == 0)


def _fox(fq, fk, fvt, cin):
    n_batch, n_h, seq_len, _ = fq.shape
    tq = KV_BLOCK
    nkb = seq_len // tq
    n_heads = 4
    return pl.pallas_call(
        functools.partial(_fox_kernel, tq=tq, n_heads=n_heads),
        out_shape=jax.ShapeDtypeStruct((n_batch, seq_len, n_h * HEAD_DIM), bf16),
        grid=(n_batch, n_h // n_heads, nkb),
        in_specs=[
            pl.BlockSpec((1, n_heads, tq, 2 * HEAD_DIM), lambda b, h, i: (b, h, i, 0)),
            pl.BlockSpec((1, n_heads, tq, 2 * HEAD_DIM), lambda b, h, i: (b, h, jnp.minimum(i + 1, nkb - 1), 0)),
            pl.BlockSpec((1, n_heads, seq_len, 2 * HEAD_DIM), lambda b, h, i: (b, h, 0, 0)),
            pl.BlockSpec((1, n_heads, nkb, HEAD_DIM, tq), lambda b, h, i: (b, h, 0, 0, 0)),
            pl.BlockSpec((1, nkb, 1, LANES), lambda b, h, i: (b, 0, 0, 0)),
        ],
        out_specs=pl.BlockSpec((1, tq, n_heads * HEAD_DIM), lambda b, h, i: (b, i, h)),
        scratch_shapes=[pltpu.VMEM((2, tq, tq), f32), pltpu.VMEM((2, tq, tq), bf16)] * n_heads,
        compiler_params=_params("arbitrary", "arbitrary", "arbitrary"),
        name="fox",
    )(fq, fq, fk, fvt, cin)


def _cmp_kernel(q_ref, k_ref, v_ref, gate_ref, ov_ref, o_ref, sel_ref, *, tq, n_rows, n_cmp, n_sel):
    i = pl.program_id(2)
    g = pl.program_id(1)
    hg = NSA_GROUP
    q = q_ref[0].reshape(hg * tq, HEAD_DIM)
    s = _dot_nt(q, k_ref[0, 0]).reshape(hg, tq, n_rows)
    t = i * tq + lax.broadcasted_iota(jnp.int32, (1, tq, 1), 1)
    n = lax.broadcasted_iota(jnp.int32, (1, 1, n_rows), 2)
    last = jnp.minimum(lax.shift_right_arithmetic(t - (CMP_LEN - 1), CMP_SHIFT), n_cmp - 1)
    s = jnp.where(n <= last, s, -jnp.inf)
    m = jnp.max(s, axis=-1, keepdims=True)
    m = jnp.where(m == -jnp.inf, 0.0, m)
    e = jnp.exp2(s - m)
    p = e * (1.0 / jnp.maximum(jnp.sum(e, axis=-1, keepdims=True), 1e-30))
    o = _dot(p.reshape(hg * tq, n_rows).astype(bf16), v_ref[0, 0])
    _write_gated(o_ref, gate_ref, o, g, 0, tq)

    lane = lax.broadcasted_iota(jnp.int32, (tq, LANES), 1)
    tq_pos = i * tq + lax.broadcasted_iota(jnp.int32, (tq, 1), 0)
    q_blk = lax.shift_right_logical(tq_pos, SEL_SHIFT)
    causal = lane <= q_blk
    n_top = min(SEL_TOPK, n_sel)

    @pl.when((i + 1) * tq <= n_top * SEL_BLOCK)
    def _():
        sel_ref[0, 0] = jnp.where(causal, 0.0, -MASK_BIG).astype(bf16)

    @pl.when((i + 1) * tq > n_top * SEL_BLOCK)
    def _():
        psum = p[0] + p[1] + p[2] + p[3]
        imp = _dot_f32_by_exact(psum, ov_ref[...])
        forced = (lane == 0) | (lane == q_blk) | (lane == q_blk - 1)
        key = jnp.where(forced, -2.0, jnp.where(causal, imp, -1.0))
        key = jnp.where(lane < n_sel, key, -3.0)
        lane_f = lane.astype(f32)
        sel = forced
        for _ in range(n_top - 3):
            mx = jnp.max(key, axis=-1, keepdims=True)
            first = jnp.min(jnp.where(key == mx, lane_f, float(LANES)), axis=-1, keepdims=True)
            pick = lane_f == first
            sel = sel | pick
            key = jnp.where(pick, -2.0, key)
        sel = ((q_blk < n_top) | sel) & causal
        sel_ref[0, 0] = jnp.where(sel, 0.0, -MASK_BIG).astype(bf16)


def _cmp(nqu, kcmp, vcmp, gates, n_cmp):
    n_batch, _, seq_len, _ = nqu.shape
    n_rows = kcmp.shape[2]
    n_sel = seq_len // SEL_BLOCK
    assert 3 <= n_sel <= SEL_LANES
    tq = 1024
    c0 = np.arange(n_rows) * CMP_STRIDE
    s0 = np.arange(SEL_LANES) * SEL_BLOCK
    overlap = np.clip(np.minimum(c0[:, None] + CMP_LEN, s0[None, :] + SEL_BLOCK)
                      - np.maximum(c0[:, None], s0[None, :]), 0, None).astype(np.float32) / CMP_LEN
    overlap[n_cmp:, :] = 0.0
    overlap[:, n_sel:] = 0.0
    return pl.pallas_call(
        functools.partial(_cmp_kernel, tq=tq, n_rows=n_rows, n_cmp=n_cmp, n_sel=n_sel),
        out_shape=[
            jax.ShapeDtypeStruct((n_batch, seq_len, NSA_W), bf16),
            jax.ShapeDtypeStruct((n_batch, NSA_KV_HEADS, seq_len, SEL_LANES), bf16),
        ],
        grid=(n_batch, NSA_KV_HEADS, seq_len // tq),
        in_specs=[
            pl.BlockSpec((1, NSA_GROUP, tq, HEAD_DIM), lambda b, g, i: (b, g, i, 0)),
            pl.BlockSpec((1, 1, n_rows, HEAD_DIM), lambda b, g, i: (b, g, 0, 0)),
            pl.BlockSpec((1, 1, n_rows, HEAD_DIM), lambda b, g, i: (b, g, 0, 0)),
            pl.BlockSpec((1, tq, LANES), lambda b, g, i: (b, i, 0)),
            pl.BlockSpec((n_rows, SEL_LANES), lambda b, g, i: (0, 0)),
        ],
        out_specs=[
            pl.BlockSpec((1, tq, NSA_GROUP * HEAD_DIM), lambda b, g, i: (b, i, g)),
            pl.BlockSpec((1, 1, tq, SEL_LANES), lambda b, g, i: (b, g, i, 0)),
        ],
        compiler_params=_params("arbitrary", "arbitrary", "arbitrary"),
        name="cmp",
    )(nqu, kcmp, vcmp, gates, jnp.asarray(overlap, dtype=bf16))


def _write_gated(o_ref, gate_ref, out, g, branch, tq):
    gates = gate_ref[0]
    lane = lax.broadcasted_iota(jnp.int32, (tq, LANES), 1)
    for h in range(NSA_GROUP):
        gcol = FOX_HEADS + (g * NSA_GROUP + h) * N_BRANCH + branch
        gh = jnp.sum(jnp.where(lane == gcol, gates, 0.0), axis=-1, keepdims=True)
        o_ref[0, :, h * HEAD_DIM:(h + 1) * HEAD_DIM] = (gh * out[h * tq:(h + 1) * tq]).astype(bf16)


def _slc_kernel(q_ref, sel_ref, qn_ref, seln_ref, k_ref, vt_ref, gate_ref, *rest, tq, tk, n_cast_w):
    o_ref = rest[n_cast_w]
    bufs = rest[2 * n_cast_w + 1:]
    for w32_ref, w16_ref in zip(rest[:n_cast_w], rest[n_cast_w + 1:2 * n_cast_w + 1]):
        w16_ref[...] = w32_ref[...].astype(bf16)
    i = pl.program_id(1)
    hg = NSA_GROUP
    n_q = hg * tq
    q0 = i * tq
    jd = q0 // tk

    def diag_mask(s):
        kp = jd * tk + lax.broadcasted_iota(jnp.int32, (tk, n_q), 0)
        t = q0 + (lax.broadcasted_iota(jnp.int32, (tk, n_q), 1) & (tq - 1))
        return jnp.where(kp <= t, s, -jnp.inf)

    def chain(g):
        def q_aug(qr, sr):
            q = qr[0, g * hg:(g + 1) * hg].reshape(n_q, HEAD_DIM)
            return jnp.concatenate([q, jnp.concatenate([sr[0, g]] * hg, axis=0)], axis=1)

        qa = q_aug(q_ref, sel_ref)

        def scores(j):
            return _dot_nt(k_ref[0, g, pl.ds(pl.multiple_of(j * tk, tk), tk), :], qa)

        def next_first():
            return _dot_nt(k_ref[0, g, 0:tk, :], q_aug(qn_ref, seln_ref))

        def emit(out):
            gates = gate_ref[0]
            lane = lax.broadcasted_iota(jnp.int32, (tq, LANES), 1)
            for h in range(hg):
                head = g * hg + h
                gh = jnp.sum(jnp.where(lane == FOX_HEADS + head * N_BRANCH + 1, gates, 0.0),
                             axis=-1, keepdims=True)
                o_ref[0, :, head * HEAD_DIM:(head + 1) * HEAD_DIM] = (
                    gh * out[:, h * tq:(h + 1) * tq].T).astype(bf16)

        return _Chain(scores, lambda j: vt_ref[0, g, j], lambda j: None, diag_mask, emit, next_first,
                      bufs[2 * g], bufs[2 * g + 1])

    _flash_sweep(jd, [chain(g) for g in range(NSA_KV_HEADS)], n_q, i == 0)


def _slc(nqr, selneg, kaug, vst, gates, cast_weights):
    n_batch, _, seq_len, _ = nqr.shape
    tq, tk = 256, KV_BLOCK
    nkb = seq_len // tk
    assert tq & (tq - 1) == 0
    n_q = NSA_GROUP * tq
    n_i = seq_len // tq
    cast_in, cast_out, cast_shapes = _cast_side_job(cast_weights, n_batch * n_i, lambda b, i: b * n_i + i)
    return pl.pallas_call(
        functools.partial(_slc_kernel, tq=tq, tk=tk, n_cast_w=len(cast_weights)),
        out_shape=[jax.ShapeDtypeStruct((n_batch, seq_len, NSA_W), bf16)] + cast_shapes,
        grid=(n_batch, n_i),
        in_specs=[
            pl.BlockSpec((1, NSA_HEADS, tq, HEAD_DIM), lambda b, i: (b, 0, i, 0)),
            pl.BlockSpec((1, NSA_KV_HEADS, tq, SEL_LANES), lambda b, i: (b, 0, i, 0)),
            pl.BlockSpec((1, NSA_HEADS, tq, HEAD_DIM), lambda b, i: (b, 0, jnp.minimum(i + 1, n_i - 1), 0)),
            pl.BlockSpec((1, NSA_KV_HEADS, tq, SEL_LANES), lambda b, i: (b, 0, jnp.minimum(i + 1, n_i - 1), 0)),
            pl.BlockSpec((1, NSA_KV_HEADS, seq_len, HEAD_DIM + SEL_LANES), lambda b, i: (b, 0, 0, 0)),
            pl.BlockSpec((1, NSA_KV_HEADS, nkb, HEAD_DIM, tk), lambda b, i: (b, 0, 0, 0, 0)),
            pl.BlockSpec((1, tq, LANES), lambda b, i: (b, i, 0)),
        ] + cast_in,
        out_specs=[pl.BlockSpec((1, tq, NSA_W), lambda b, i: (b, i, 0))] + cast_out,
        scratch_shapes=[pltpu.VMEM((2, tk, n_q), f32), pltpu.VMEM((2, tk, n_q), bf16)] * NSA_KV_HEADS,
        compiler_params=_params("arbitrary", "arbitrary"),
        name="slc",
    )(nqr, selneg, nqr, selneg, kaug, vst, gates, *cast_weights)


def _win_kernel(q_ref, k_ref, vt_ref, gate_ref, o_ref, *, tq, n_sub):
    i = pl.program_id(2)
    g = pl.program_id(1)
    hg = NSA_GROUP
    span = WINDOW + tq
    q0s = [(i * n_sub + u) * tq for u in range(n_sub)]
    k0s = [pl.multiple_of(jnp.maximum(q0 - WINDOW, 0), tq) for q0 in q0s]
    scores = []
    for u in range(n_sub):
        q = q_ref[0, :, u * tq:(u + 1) * tq, :].reshape(hg * tq, HEAD_DIM)
        scores.append(_dot_nt(k_ref[0, 0, pl.ds(k0s[u], span), :], q))
    probs, denoms = [], []
    for u in range(n_sub):
        kp = k0s[u] + lax.broadcasted_iota(jnp.int32, (span, tq), 0)
        t = q0s[u] + lax.broadcasted_iota(jnp.int32, (span, tq), 1)
        diff = t - kp
        bias = jnp.where((diff >= 0) & (diff < WINDOW), 0.0, -jnp.inf)
        s = jnp.concatenate([scores[u][:, h * tq:(h + 1) * tq] + bias for h in range(hg)], axis=1)
        e = jnp.exp2(s - jnp.max(s, axis=0, keepdims=True))
        denoms.append(jnp.sum(e, axis=0, keepdims=True))
        probs.append(e.astype(bf16))
    gates = gate_ref[0]
    lane = lax.broadcasted_iota(jnp.int32, (tq, LANES), 1)
    for u in range(n_sub):
        jb = k0s[u] // tq
        vt = jnp.concatenate([vt_ref[0, 0, jb + c] for c in range(span // tq)], axis=1)
        out = _dot(vt, probs[u]) / denoms[u]
        for h in range(hg):
            gcol = FOX_HEADS + (g * hg + h) * N_BRANCH + 2
            gh = jnp.sum(jnp.where(lane == gcol, gates[u * tq:(u + 1) * tq], 0.0), axis=-1, keepdims=True)
            o_ref[0, u * tq:(u + 1) * tq, h * HEAD_DIM:(h + 1) * HEAD_DIM] = (
                gh * out[:, h * tq:(h + 1) * tq].T).astype(bf16)


def _win(nqr, kw, vwt, gates):
    n_batch, _, seq_len, _ = nqr.shape
    tq, n_sub = WIN_TILE, 4
    assert seq_len >= WINDOW + tq and WINDOW % tq == 0
    return pl.pallas_call(
        functools.partial(_win_kernel, tq=tq, n_sub=n_sub),
        out_shape=jax.ShapeDtypeStruct((n_batch, seq_len, NSA_W), bf16),
        grid=(n_batch, NSA_KV_HEADS, seq_len // (tq * n_sub)),
        in_specs=[
            pl.BlockSpec((1, NSA_GROUP, tq * n_sub, HEAD_DIM), lambda b, g, i: (b, g, i, 0)),
            pl.BlockSpec((1, 1, seq_len, HEAD_DIM), lambda b, g, i: (b, g, 0, 0)),
            pl.BlockSpec((1, 1, seq_len // tq, HEAD_DIM, tq), lambda b, g, i: (b, g, 0, 0, 0)),
            pl.BlockSpec((1, tq * n_sub, LANES), lambda b, g, i: (b, i, 0)),
        ],
        out_specs=pl.BlockSpec((1, tq * n_sub, NSA_GROUP * HEAD_DIM), lambda b, g, i: (b, i, g)),
        compiler_params=_params("arbitrary", "arbitrary", "arbitrary"),
        name="win",
    )(nqr, kw, vwt, gates)


def _outproj_kernel(fox_ref, c_ref, s_ref, w_ref, x_ref, mod_ref, wo_ref, o_ref):
    mix = _dot(fox_ref[...], wo_ref[0:FOX_W, :])
    nsa = c_ref[...].astype(f32) + s_ref[...].astype(f32) + w_ref[...].astype(f32)
    mix = mix + _dot(nsa.astype(bf16), wo_ref[FOX_W:FOX_W + NSA_W, :])
    o_ref[...] = x_ref[...] + mod_ref[0][2:3] * mix


def _outproj(ofox, ocmp, oslc, owin, x2d, mod3, wo_bf16, seq_len):
    m_rows = x2d.shape[0]
    tm = 512
    per_b = seq_len // tm
    half = pl.BlockSpec((tm, FOX_W), lambda i: (i, 0))
    return pl.pallas_call(
        _outproj_kernel,
        out_shape=jax.ShapeDtypeStruct((m_rows, D_MODEL), f32),
        grid=(m_rows // tm,),
        in_specs=[
            half, half, half, half,
            pl.BlockSpec((tm, D_MODEL), lambda i: (i, 0)),
            pl.BlockSpec((1, 6, D_MODEL), lambda i: (i // per_b, 0, 0)),
            pl.BlockSpec((D_MODEL, D_MODEL), lambda i: (0, 0)),
        ],
        out_specs=pl.BlockSpec((tm, D_MODEL), lambda i: (i, 0)),
        compiler_params=_params("arbitrary"),
        name="outproj",
    )(ofox, ocmp, oslc, owin, x2d, mod3, wo_bf16)


def _mlp_kernel(x_ref, mod_ref, g_ref, wu_ref, wd_ref, o_ref, h_ref, acc_ref):
    f = pl.program_id(1)

    @pl.when(f == 0)
    def _():
        md = mod_ref[0]
        y = _rms(x_ref[...], g_ref[...])
        h_ref[...] = (y * (1.0 + md[4:5]) + md[3:4]).astype(bf16)
        acc_ref[...] = jnp.zeros_like(acc_ref)

    u = jnp.maximum(_dot(h_ref[...], wu_ref[...]), 0.0)
    acc_ref[...] += _dot((u * u).astype(bf16), wd_ref[...])

    @pl.when(f == pl.num_programs(1) - 1)
    def _():
        o_ref[...] = x_ref[...] + mod_ref[0][5:6] * acc_ref[...]


def _mlp(x2d, mod3, norm_g, wu_bf16, wd_bf16, seq_len):
    m_rows = x2d.shape[0]
    tm, tf = 512, 1024
    per_b = seq_len // tm
    return pl.pallas_call(
        _mlp_kernel,
        out_shape=jax.ShapeDtypeStruct((m_rows, D_MODEL), f32),
        grid=(m_rows // tm, D_FF // tf),
        in_specs=[
            pl.BlockSpec((tm, D_MODEL), lambda i, f: (i, 0)),
            pl.BlockSpec((1, 6, D_MODEL), lambda i, f: (i // per_b, 0, 0)),
            pl.BlockSpec((1, D_MODEL), lambda i, f: (0, 0)),
            pl.BlockSpec((D_MODEL, tf), lambda i, f: (0, f)),
            pl.BlockSpec((tf, D_MODEL), lambda i, f: (f, 0)),
        ],
        out_specs=pl.BlockSpec((tm, D_MODEL), lambda i, f: (i, 0)),
        scratch_shapes=[pltpu.VMEM((tm, D_MODEL), bf16), pltpu.VMEM((tm, D_MODEL), f32)],
        compiler_params=_params("arbitrary", "arbitrary"),
        name="mlp",
    )(x2d, mod3, norm_g, wu_bf16, wd_bf16)


def _layer(x, c, w_ada, b_ada, norm1_g, w_in, b_forget, fox_q_norm, fox_k_norm, nsa_q_norm,
           cmp_k_norm, slc_k_norm, win_k_norm, cmp_pe_k, cmp_w1_k, cmp_w2_k, cmp_pe_v, cmp_w1_v,
           cmp_w2_v, w_out, norm2_g, w_up, w_down):
    n_batch, seq_len, _ = x.shape
    n_cmp = (seq_len - CMP_LEN) // CMP_STRIDE + 1
    row = lambda v: v.reshape(1, -1)

    half = HEAD_DIM // 2
    inv_freq = ROPE_THETA ** (-jnp.arange(half, dtype=f32) / half)
    inv_freq = jnp.concatenate([inv_freq, inv_freq]).reshape(1, HEAD_DIM)

    w_t = w_in.T
    w_a = w_t[:W_IN_Z0].astype(bf16)
    w_b = w_t[W_IN_NQ0:W_IN_GZ0].astype(bf16)
    w_s = jnp.concatenate([w_t[W_IN_Z0:W_IN_NQ0], w_t[W_IN_GZ0:],
                           jnp.zeros((LANES - FOX_HEADS - N_BRANCH * NSA_HEADS, D_MODEL), w_t.dtype)],
                          axis=0).astype(bf16)

    mod3 = _ada(c, w_ada, b_ada).reshape(n_batch, 6, D_MODEL)
    x2d = x.reshape(n_batch * seq_len, D_MODEL)
    (fq, fk, fvt, cin, nqu, nqr, kc, vc, kaug, vst, kw, vw, gates) = _prep(
        x2d, mod3, row(norm1_g), w_a, w_b, w_s, n_batch, seq_len, row(fox_q_norm), row(fox_k_norm), row(nsa_q_norm), row(slc_k_norm),
        row(win_k_norm), jnp.pad(b_forget, (0, LANES - FOX_HEADS)).reshape(1, LANES), inv_freq)
    kcmp = _compress(kc, cmp_pe_k, cmp_w1_k.astype(bf16), cmp_w2_k.astype(bf16), row(cmp_k_norm))
    vcmp = _compress(vc, cmp_pe_v, cmp_w1_v.astype(bf16), cmp_w2_v.astype(bf16))
    ofox = _fox(fq, fk, fvt, cin)
    ocmp, selneg = _cmp(nqu, kcmp, vcmp, gates, n_cmp)
    oslc, w_up16, w_down16, w_out16 = _slc(nqr, selneg, kaug, vst, gates, [w_up, w_down, w_out])
    owin = _win(nqr, kw, vw, gates)
    x1 = _outproj(ofox.reshape(-1, FOX_W), ocmp.reshape(-1, NSA_W), oslc.reshape(-1, NSA_W),
                  owin.reshape(-1, NSA_W), x2d, mod3, w_out16, seq_len)
    x2 = _mlp(x1, mod3, row(norm2_g), w_up16, w_down16, seq_len)
    return x2.reshape(n_batch, seq_len, D_MODEL)


def kernel(x, c, w_ada, b_ada, norm1_g, w_in, b_forget, fox_q_norm, fox_k_norm, nsa_q_norm, cmp_k_norm,
           slc_k_norm, win_k_norm, cmp_pe_k, cmp_w1_k, cmp_w2_k, cmp_pe_v, cmp_w1_v, cmp_w2_v, w_out,
           norm2_g, w_up, w_down):
    depth = w_ada.shape[0]
    for l in range(depth):
        x = _layer(x, c, w_ada[l], b_ada[l], norm1_g[l], w_in[l], b_forget[l], fox_q_norm[l], fox_k_norm[l],
                   nsa_q_norm[l], cmp_k_norm[l], slc_k_norm[l], win_k_norm[l], cmp_pe_k[l], cmp_w1_k[l],
                   cmp_w2_k[l], cmp_pe_v[l], cmp_w1_v[l], cmp_w2_v[l], w_out[l], norm2_g[l], w_up[l],
                   w_down[l])
    return x
```

```python
import functools
import math
from typing import Any, Callable, NamedTuple

import numpy as np
import jax
import jax.numpy as jnp
from jax import lax
from jax.experimental import pallas as pl
from jax.experimental.pallas import tpu as pltpu

D_MODEL = 2048
HEAD_DIM = 128
FOX_HEADS = 8
NSA_HEADS = 8
NSA_KV_HEADS = 2
NSA_GROUP = NSA_HEADS // NSA_KV_HEADS
N_BRANCH = 3
D_FF = 4 * D_MODEL
ROPE_THETA = 10000.0
CMP_LEN = 32
CMP_STRIDE = 16
CMP_SHIFT = 4
CMP_HIDDEN = 2 * HEAD_DIM
SEL_BLOCK = 64
SEL_SHIFT = 6
SEL_TOPK = 16
WINDOW = 512
NORM_EPS = 1e-6
ATTN_SCALE = HEAD_DIM ** -0.5
FOX_W = FOX_HEADS * HEAD_DIM
NSA_W = NSA_HEADS * HEAD_DIM
KV_W = NSA_KV_HEADS * HEAD_DIM

LANES = 128
SEL_LANES = LANES
MASK_BIG = 1e30
KV_BLOCK = 512
WIN_TILE = 128
LOG2E = math.log2(math.e)

COL_FQ = 0
COL_FK = COL_FQ + FOX_W
COL_FV = COL_FK + FOX_W
COL_NQ = COL_FV + FOX_W
COL_KC = COL_NQ + NSA_W
COL_VC = COL_KC + KV_W
COL_KS = COL_VC + KV_W
COL_VS = COL_KS + KV_W
COL_KW = COL_VS + KV_W
COL_VW = COL_KW + KV_W
COL_SMALL = COL_VW + KV_W
W_IN_Z0 = 3 * FOX_W
W_IN_NQ0 = W_IN_Z0 + FOX_HEADS
W_IN_GZ0 = W_IN_NQ0 + NSA_W + 6 * KV_W

VMEM_LIMIT = 56 * 1024 * 1024
VMEM_LIMIT_PREP = 60 * 1024 * 1024

f32 = jnp.float32
bf16 = jnp.bfloat16


def _params(*sem):
    return pltpu.CompilerParams(dimension_semantics=sem, vmem_limit_bytes=VMEM_LIMIT)


def _dot_nt(a, b):
    return lax.dot_general(a, b, (((1,), (1,)), ((), ())), preferred_element_type=f32)


def _dot(a, b):
    return jnp.dot(a, b, preferred_element_type=f32)


def _split3(x):
    hi = x.astype(bf16)
    r1 = x - hi.astype(f32)
    mid = r1.astype(bf16)
    lo = (r1 - mid.astype(f32)).astype(bf16)
    return hi, mid, lo


def _dot_f32_by_exact(x, w_bf16):
    hi, mid, lo = _split3(x)
    return _dot(hi, w_bf16) + (_dot(mid, w_bf16) + _dot(lo, w_bf16))


def _rms(x, gain):
    ms = jnp.mean(x * x, axis=-1, keepdims=True)
    return x * lax.rsqrt(ms + NORM_EPS) * gain


def _ada_kernel(ct_ref, w_ref, b_ref, o_ref, *, n_batch, k_chunk):
    ct = ct_ref[...]
    act = ct * jax.nn.sigmoid(ct)
    rows = []
    for b in range(n_batch):
        col = act[:, b:b + 1]
        acc = b_ref[...]
        for k0 in range(0, D_MODEL, k_chunk):
            acc = acc + jnp.sum(w_ref[k0:k0 + k_chunk, :] * col[k0:k0 + k_chunk], axis=0, keepdims=True)
        rows.append(acc)
    o_ref[...] = jnp.concatenate(rows, axis=0)


def _ada(c, w_ada, b_ada):
    n_batch = c.shape[0]
    n_out = w_ada.shape[1]
    tn = 1024
    return pl.pallas_call(
        functools.partial(_ada_kernel, n_batch=n_batch, k_chunk=256),
        out_shape=jax.ShapeDtypeStruct((n_batch, n_out), f32),
        grid=(n_out // tn,),
        in_specs=[
            pl.BlockSpec((D_MODEL, n_batch), lambda j: (0, 0)),
            pl.BlockSpec((D_MODEL, tn), lambda j: (0, j)),
            pl.BlockSpec((1, tn), lambda j: (0, j)),
        ],
        out_specs=pl.BlockSpec((n_batch, tn), lambda j: (0, j)),
        compiler_params=_params("arbitrary"),
        name="ada",
    )(c.T, w_ada, b_ada.reshape(1, n_out))


def _repack_kernel(a_ref, b0_ref, b1_ref, z_ref, gz_ref, wa_ref, wb_ref, ws_ref):
    wa_ref[...] = a_ref[...].astype(bf16)
    wb_ref[...] = jnp.concatenate([b0_ref[...], b1_ref[...]], axis=0).astype(bf16)
    pad = jnp.zeros((LANES - z_ref.shape[0] - gz_ref.shape[0], D_MODEL), f32)
    ws_ref[...] = jnp.concatenate([z_ref[...], gz_ref[...], pad], axis=0).astype(bf16)


def _repack_w_in(w_t):
    n_steps = 32
    ra, rb = W_IN_Z0 // n_steps, (W_IN_GZ0 - W_IN_NQ0) // n_steps
    hb = rb // 2
    n_z, n_gz = W_IN_NQ0 - W_IN_Z0, w_t.shape[0] - W_IN_GZ0
    assert W_IN_Z0 % n_steps == 0 and (W_IN_GZ0 - W_IN_NQ0) % n_steps == 0 and ra % 16 == 0 and rb % 16 == 0
    assert hb % 8 == 0 and W_IN_NQ0 % hb == 0 and W_IN_Z0 % n_z == 0 and W_IN_GZ0 % n_gz == 0
    b_off = W_IN_NQ0 // hb
    return pl.pallas_call(
        _repack_kernel,
        out_shape=[jax.ShapeDtypeStruct((W_IN_Z0, D_MODEL), bf16),
                   jax.ShapeDtypeStruct((W_IN_GZ0 - W_IN_NQ0, D_MODEL), bf16),
                   jax.ShapeDtypeStruct((LANES, D_MODEL), bf16)],
        grid=(n_steps,),
        in_specs=[
            pl.BlockSpec((ra, D_MODEL), lambda j: (j, 0)),
            pl.BlockSpec((hb, D_MODEL), lambda j: (b_off + 2 * j, 0)),
            pl.BlockSpec((hb, D_MODEL), lambda j: (b_off + 2 * j + 1, 0)),
            pl.BlockSpec((n_z, D_MODEL), lambda j: (W_IN_Z0 // n_z, 0)),
            pl.BlockSpec((n_gz, D_MODEL), lambda j: (W_IN_GZ0 // n_gz, 0)),
        ],
        out_specs=[pl.BlockSpec((ra, D_MODEL), lambda j: (j, 0)),
                   pl.BlockSpec((rb, D_MODEL), lambda j: (j, 0)),
                   pl.BlockSpec((LANES, D_MODEL), lambda j: (0, 0))],
        compiler_params=_params("arbitrary"),
        name="repack",
    )(w_t, w_t, w_t, w_t, w_t)


def _cast_side_job(weights, n_steps, step_of):
    n_cast = max(c for c in range(1, n_steps + 1) if all(w.shape[0] % (16 * c) == 0 for w in weights))
    idx = lambda *ids: (jnp.minimum(step_of(*ids), n_cast - 1), 0)
    specs = [pl.BlockSpec((w.shape[0] // n_cast, w.shape[1]), idx) for w in weights]
    return specs, specs, [jax.ShapeDtypeStruct(w.shape, bf16) for w in weights]


def _prep_kernel(x_ref, mod_ref, g1_ref, wa_ref, wb_ref, ws_ref,
                 gq_ref, gk_ref, gn_ref, gs_ref, gw_ref, bf_ref, inv_ref,
                 fq_ref, fk_ref, fvt_ref, cin_ref, nqu_ref, nqr_ref, kc_ref, vc_ref,
                 ks_ref, vst_ref, kw_ref, vw_ref, gate_ref, carry_ref, rot_ref, *, tm):
    i = pl.program_id(1)

    @pl.when(i == 0)
    def _():
        carry_ref[...] = jnp.zeros_like(carry_ref)

    md = mod_ref[0]
    hn = (_rms(x_ref[...], g1_ref[...]) * (1.0 + md[1:2]) + md[0:1]).astype(bf16)
    groups = {}

    def head(col, h):
        base, w_ref, r0 = next((b, w, r) for b, w, r in (
            (COL_KC, wb_ref, NSA_W), (COL_NQ, wb_ref, 0), (COL_FV, wa_ref, 2 * FOX_W),
            (COL_FK, wa_ref, FOX_W), (COL_FQ, wa_ref, 0)) if col >= b)
        if base not in groups:
            n = (COL_SMALL - COL_KC) if base == COL_KC else FOX_W
            groups[base] = _dot_nt(hn, w_ref[r0:r0 + n, :])
        c0 = col - base + h * HEAD_DIM
        return groups[base][:, c0:c0 + HEAD_DIM]

    row = lax.broadcasted_iota(jnp.int32, (tm, LANES), 0)
    lane = lax.broadcasted_iota(jnp.int32, (tm, LANES), 1)
    pos = i * tm + row

    @pl.when((pl.program_id(0) == 0) & (i == 0))
    def _():
        ang_row = row.astype(f32) * inv_ref[...]
        rot_ref[0] = jnp.cos(ang_row)
        rot_ref[1] = jnp.sin(ang_row)

    ang0 = (i * tm).astype(f32) * inv_ref[...]
    cos0, sin0 = jnp.cos(ang0), jnp.sin(ang0)
    cos = cos0 * rot_ref[0] - sin0 * rot_ref[1]
    sin = sin0 * rot_ref[0] + cos0 * rot_ref[1]
    sin_signed = jnp.where(lane < HEAD_DIM // 2, -sin, sin)

    def rope(x):
        return x * cos + pltpu.roll(x, HEAD_DIM // 2, 1) * sin_signed

    small = _dot_nt(hn, ws_ref[...])
    z = small + bf_ref[...]
    logf = jnp.minimum(z, 0.0) - jnp.log1p(jnp.exp(-jnp.abs(z)))
    t_idx = lax.broadcasted_iota(jnp.int32, (tm, tm), 0)
    s_idx = lax.broadcasted_iota(jnp.int32, (tm, tm), 1)
    tri = jnp.where(s_idx <= t_idx, 1.0, 0.0).astype(bf16)
    hi, mid, lo = _split3(logf)
    local = _dot(tri, hi) + (_dot(tri, mid) + _dot(tri, lo))
    cin_ref[0, 0] = carry_ref[0:1, :]
    carry_ref[...] = carry_ref[...] + local[tm - 1:tm, :]
    b_hi, b_mid, b_lo = (v.astype(f32) for v in _split3(local * (-LOG2E)))
    ones3 = jnp.where(lane < 3, 1.0, 0.0).astype(bf16)

    for h in range(FOX_HEADS):
        q = _rms(head(COL_FQ, h), gq_ref[...] * (ATTN_SCALE * LOG2E))
        fq_ref[0, h] = jnp.concatenate([q.astype(bf16), ones3], axis=1)
    for h in range(FOX_HEADS):
        k = _rms(head(COL_FK, h), gk_ref[...])
        bias = jnp.where(lane == 0, b_hi[:, h:h + 1],
                         jnp.where(lane == 1, b_mid[:, h:h + 1],
                                   jnp.where(lane == 2, b_lo[:, h:h + 1], 0.0)))
        fk_ref[0, h] = jnp.concatenate([k.astype(bf16), bias.astype(bf16)], axis=1)

    gate_ref[0] = jax.nn.sigmoid(small)

    for h in range(NSA_HEADS):
        qn = _rms(head(COL_NQ, h), gn_ref[...] * (ATTN_SCALE * LOG2E))
        nqu_ref[0, h] = qn.astype(bf16)
        nqr_ref[0, h] = rope(qn).astype(bf16)
    onehot = jnp.where(lane == lax.shift_right_logical(pos, SEL_SHIFT), 1.0, 0.0).astype(bf16)
    for g in range(NSA_KV_HEADS):
        kc_ref[0, g] = head(COL_KC, g)
        vc_ref[0, g] = head(COL_VC, g)
        ks = rope(_rms(head(COL_KS, g), gs_ref[...])).astype(bf16)
        ks_ref[0, g] = jnp.concatenate([ks, onehot], axis=1)
        vst_ref[0, g, 0] = head(COL_VS, g).T.astype(bf16)
        kw_ref[0, g] = rope(_rms(head(COL_KW, g), gw_ref[...])).astype(bf16)
        vwt = head(COL_VW, g).T.astype(bf16)
        for c in range(tm // WIN_TILE):
            vw_ref[0, g, c] = vwt[:, c * WIN_TILE:(c + 1) * WIN_TILE]
    for h in range(FOX_HEADS):
        fvt_ref[0, h, 0] = head(COL_FV, h).T.astype(bf16)


def _prep(x2d, mod3, norm_g, wa, wb, ws, n_batch, seq_len, gq, gk, gn, gs, gw, b_forget_row, inv_freq):
    resident = lambda w: pl.BlockSpec(w.shape, lambda b, i: (0, 0), pipeline_mode=pl.Buffered(1))
    tm = KV_BLOCK
    per_b = seq_len // tm
    hshape = lambda n, w, dt: jax.ShapeDtypeStruct((n_batch, n, seq_len, w), dt)
    hspec = lambda n, w: pl.BlockSpec((1, n, tm, w), lambda b, i: (b, 0, i, 0))
    tshape = lambda n: jax.ShapeDtypeStruct((n_batch, n, per_b, HEAD_DIM, tm), bf16)
    tspec = lambda n: pl.BlockSpec((1, n, 1, HEAD_DIM, tm), lambda b, i: (b, 0, i, 0, 0))
    vec = pl.BlockSpec((1, LANES), lambda b, i: (0, 0))
    return pl.pallas_call(
        functools.partial(_prep_kernel, tm=tm),
        out_shape=[
            hshape(FOX_HEADS, 2 * HEAD_DIM, bf16), hshape(FOX_HEADS, 2 * HEAD_DIM, bf16), tshape(FOX_HEADS),
            jax.ShapeDtypeStruct((n_batch, per_b, 1, LANES), f32),
            hshape(NSA_HEADS, HEAD_DIM, bf16), hshape(NSA_HEADS, HEAD_DIM, bf16),
            hshape(NSA_KV_HEADS, HEAD_DIM, f32), hshape(NSA_KV_HEADS, HEAD_DIM, f32),
            hshape(NSA_KV_HEADS, HEAD_DIM + SEL_LANES, bf16), tshape(NSA_KV_HEADS),
            hshape(NSA_KV_HEADS, HEAD_DIM, bf16),
            jax.ShapeDtypeStruct((n_batch, NSA_KV_HEADS, seq_len // WIN_TILE, HEAD_DIM, WIN_TILE), bf16),
            jax.ShapeDtypeStruct((n_batch, seq_len, LANES), f32),
        ],
        grid=(n_batch, per_b),
        in_specs=[
            pl.BlockSpec((tm, D_MODEL), lambda b, i: (b * per_b + i, 0)),
            pl.BlockSpec((1, 6, D_MODEL), lambda b, i: (b, 0, 0)),
            pl.BlockSpec((1, D_MODEL), lambda b, i: (0, 0)),
            resident(wa), resident(wb), resident(ws),
            vec, vec, vec, vec, vec, vec, vec,
        ],
        out_specs=[
            hspec(FOX_HEADS, 2 * HEAD_DIM), hspec(FOX_HEADS, 2 * HEAD_DIM), tspec(FOX_HEADS),
            pl.BlockSpec((1, 1, 1, LANES), lambda b, i: (b, i, 0, 0)),
            hspec(NSA_HEADS, HEAD_DIM), hspec(NSA_HEADS, HEAD_DIM),
            hspec(NSA_KV_HEADS, HEAD_DIM), hspec(NSA_KV_HEADS, HEAD_DIM),
            hspec(NSA_KV_HEADS, HEAD_DIM + SEL_LANES), tspec(NSA_KV_HEADS),
            hspec(NSA_KV_HEADS, HEAD_DIM),
            pl.BlockSpec((1, NSA_KV_HEADS, tm // WIN_TILE, HEAD_DIM, WIN_TILE), lambda b, i: (b, 0, i, 0, 0)),
            pl.BlockSpec((1, tm, LANES), lambda b, i: (b, i, 0)),
        ],
        scratch_shapes=[pltpu.VMEM((8, LANES), f32), pltpu.VMEM((2, tm, LANES), f32)],
        compiler_params=pltpu.CompilerParams(dimension_semantics=("arbitrary", "arbitrary"),
                                             vmem_limit_bytes=VMEM_LIMIT_PREP),
        name="prep",
    )(x2d, mod3, norm_g, wa, wb, ws, gq, gk, gn, gs, gw, b_forget_row, inv_freq)


def _compress_kernel(x_ref, pe_ref, w1_ref, w2_ref, *rest, n_rows, do_norm):
    o_ref = rest[-1]
    a = jnp.zeros((n_rows, CMP_HIDDEN), f32)
    b = jnp.zeros((n_rows, CMP_HIDDEN), f32)
    for l in range(CMP_STRIDE):
        x = x_ref[0, 0, pl.ds(l, n_rows, stride=CMP_STRIDE), :]
        a = a + _dot((x + pe_ref[l:l + 1, :]).astype(bf16), w1_ref[l * HEAD_DIM:(l + 1) * HEAD_DIM, :])
        lb = CMP_STRIDE + l
        b = b + _dot((x + pe_ref[lb:lb + 1, :]).astype(bf16), w1_ref[lb * HEAD_DIM:(lb + 1) * HEAD_DIM, :])
    pre = a + pltpu.roll(b, n_rows - 1, 0)
    hid = pre * jax.nn.sigmoid(pre)
    out = _dot(hid.astype(bf16), w2_ref[...])
    if do_norm:
        out = _rms(out, rest[0][...])
    o_ref[0, 0] = out.astype(bf16)


def _compress(x4, pe, w1_bf16, w2_bf16, gain=None):
    n_batch, n_g, seq_len, _ = x4.shape
    n_rows = seq_len // CMP_STRIDE
    do_norm = gain is not None
    return pl.pallas_call(
        functools.partial(_compress_kernel, n_rows=n_rows, do_norm=do_norm),
        out_shape=jax.ShapeDtypeStruct((n_batch, n_g, n_rows, HEAD_DIM), bf16),
        grid=(n_batch, n_g),
        in_specs=[
            pl.BlockSpec((1, 1, seq_len, HEAD_DIM), lambda b, g: (b, g, 0, 0)),
            pl.BlockSpec((CMP_LEN, HEAD_DIM), lambda b, g: (0, 0)),
            pl.BlockSpec((CMP_LEN * HEAD_DIM, CMP_HIDDEN), lambda b, g: (0, 0)),
            pl.BlockSpec((CMP_HIDDEN, HEAD_DIM), lambda b, g: (0, 0)),
        ] + ([pl.BlockSpec((1, HEAD_DIM), lambda b, g: (0, 0))] if do_norm else []),
        out_specs=pl.BlockSpec((1, 1, n_rows, HEAD_DIM), lambda b, g: (b, g, 0, 0)),
        compiler_params=_params("arbitrary", "arbitrary"),
        name="compress",
    )(x4, pe, w1_bf16, w2_bf16, *([gain] if do_norm else []))


def _softmax_block(s, d, m_prev, l_prev):
    m_blk = jnp.max(s, axis=0, keepdims=True)
    m_new = jnp.maximum(m_prev, m_blk if d is None else m_blk + d)
    p = jnp.exp2(s - (m_new if d is None else m_new - d))
    alpha = jnp.exp2(m_prev - m_new)
    l_new = alpha * l_prev + jnp.sum(p, axis=0, keepdims=True)
    return m_new, l_new, alpha, p.astype(bf16)


class _Chain(NamedTuple):
    scores: Callable
    values: Callable
    offset: Callable
    diag_mask: Callable
    emit: Callable
    next_first: Callable
    s_buf: Any
    p_buf: Any


def _flash_sweep(n_full, chains, n_q, is_first):
    @pl.when(is_first)
    def _():
        for c in chains:
            c.s_buf[0] = c.scores(0)

    for c in chains:
        c.p_buf[1] = jnp.zeros(c.p_buf.shape[1:], c.p_buf.dtype)

    def accumulate(c, j, slot, alpha, acc):
        return alpha * acc + _dot(c.values(jnp.maximum(j, 0)), c.p_buf[slot])

    def step(j, cur, states):
        out = []
        for c, (m, l, acc, alpha_prev) in zip(chains, states):
            acc = accumulate(c, j - 1, 1 - cur, alpha_prev, acc)
            m, l, alpha, p = _softmax_block(c.s_buf[cur], c.offset(j), m, l)
            c.p_buf[cur] = p
            c.s_buf[1 - cur] = c.scores(j + 1)
            out.append((m, l, acc, alpha))
        return tuple(out)

    def finish(cur, states):
        nxt = [c.next_first() for c in chains]
        for c, s_next, (m, l, acc, alpha_prev) in zip(chains, nxt, states):
            if cur == 1:
                c.s_buf[0] = s_next
            acc = accumulate(c, n_full - 1, 1 - cur, alpha_prev, acc)
            m, l, alpha, p = _softmax_block(c.diag_mask(c.s_buf[cur]), None, m, l)
            if cur == 0:
                c.s_buf[0] = s_next
            acc = alpha * acc + _dot(c.values(n_full), p)
            c.emit(acc / l)

    init = tuple((jnp.full((1, n_q), -jnp.inf, f32), jnp.zeros((1, n_q), f32),
                  jnp.zeros((HEAD_DIM, n_q), f32), jnp.ones((1, n_q), f32)) for _ in chains)
    states = lax.fori_loop(0, n_full // 2, lambda jj, st: step(2 * jj + 1, 1, step(2 * jj, 0, st)), init)

    @pl.when(n_full % 2 == 0)
    def _():
        finish(0, states)

    @pl.when(n_full % 2 == 1)
    def _():
        finish(1, step(n_full - 1, 0, states))


def _fox_kernel(q_ref, qn_ref, k_ref, vt_ref, c_ref, o_ref, *bufs, tq, n_heads):
    hb = pl.program_id(1)
    i = pl.program_id(2)
    lane = lax.broadcasted_iota(jnp.int32, (1, LANES), 1)

    def diag_mask(s):
        kk = lax.broadcasted_iota(jnp.int32, (tq, tq), 0)
        qq = lax.broadcasted_iota(jnp.int32, (tq, tq), 1)
        return jnp.where(kk <= qq, s, -jnp.inf)

    def chain(u):
        qa = q_ref[0, u]

        def cin(j):
            return jnp.sum(jnp.where(lane == hb * n_heads + u, c_ref[0, j], 0.0), axis=-1, keepdims=True)

        ci = cin(i)

        def scores(j):
            return _dot_nt(k_ref[0, u, pl.ds(pl.multiple_of(j * tq, tq), tq), :], qa)

        def emit(out_t):
            o_ref[0, :, u * HEAD_DIM:(u + 1) * HEAD_DIM] = out_t.T.astype(bf16)

        def next_first():
            return _dot_nt(k_ref[0, u, 0:tq, :], qn_ref[0, u])

        return _Chain(scores, lambda j: vt_ref[0, u, j], lambda j: (ci - cin(j)) * LOG2E, diag_mask, emit,
                      next_first, bufs[2 * u], bufs[2 * u + 1])

    _flash_sweep(i, [chain(u) for u in range(n_heads)], tq, i == 0)


def _fox(fq, fk, fvt, cin):
    n_batch, n_h, seq_len, _ = fq.shape
    tq = KV_BLOCK
    nkb = seq_len // tq
    n_heads = 4
    return pl.pallas_call(
        functools.partial(_fox_kernel, tq=tq, n_heads=n_heads),
        out_shape=jax.ShapeDtypeStruct((n_batch, seq_len, n_h * HEAD_DIM), bf16),
        grid=(n_batch, n_h // n_heads, nkb),
        in_specs=[
            pl.BlockSpec((1, n_heads, tq, 2 * HEAD_DIM), lambda b, h, i: (b, h, i, 0)),
            pl.BlockSpec((1, n_heads, tq, 2 * HEAD_DIM), lambda b, h, i: (b, h, jnp.minimum(i + 1, nkb - 1), 0)),
            pl.BlockSpec((1, n_heads, seq_len, 2 * HEAD_DIM), lambda b, h, i: (b, h, 0, 0)),
            pl.BlockSpec((1, n_heads, nkb, HEAD_DIM, tq), lambda b, h, i: (b, h, 0, 0, 0)),
            pl.BlockSpec((1, nkb, 1, LANES), lambda b, h, i: (b, 0, 0, 0)),
        ],
        out_specs=pl.BlockSpec((1, tq, n_heads * HEAD_DIM), lambda b, h, i: (b, i, h)),
        scratch_shapes=[pltpu.VMEM((2, tq, tq), f32), pltpu.VMEM((2, tq, tq), bf16)] * n_heads,
        compiler_params=_params("arbitrary", "arbitrary", "arbitrary"),
        name="fox",
    )(fq, fq, fk, fvt, cin)


def _cmp_kernel(q_ref, k_ref, v_ref, gate_ref, ov_ref, o_ref, sel_ref, *, tq, n_rows, n_cmp, n_sel):
    i = pl.program_id(2)
    g = pl.program_id(1)
    hg = NSA_GROUP
    q = q_ref[0].reshape(hg * tq, HEAD_DIM)
    s = _dot_nt(q, k_ref[0, 0]).reshape(hg, tq, n_rows)
    t = i * tq + lax.broadcasted_iota(jnp.int32, (1, tq, 1), 1)
    n = lax.broadcasted_iota(jnp.int32, (1, 1, n_rows), 2)
    last = jnp.minimum(lax.shift_right_arithmetic(t - (CMP_LEN - 1), CMP_SHIFT), n_cmp - 1)
    s = jnp.where(n <= last, s, -jnp.inf)
    m = jnp.max(s, axis=-1, keepdims=True)
    m = jnp.where(m == -jnp.inf, 0.0, m)
    e = jnp.exp2(s - m)
    p = e * (1.0 / jnp.maximum(jnp.sum(e, axis=-1, keepdims=True), 1e-30))
    o = _dot(p.reshape(hg * tq, n_rows).astype(bf16), v_ref[0, 0])
    _write_gated(o_ref, gate_ref, o, g, 0, tq)

    lane = lax.broadcasted_iota(jnp.int32, (tq, LANES), 1)
    tq_pos = i * tq + lax.broadcasted_iota(jnp.int32, (tq, 1), 0)
    q_blk = lax.shift_right_logical(tq_pos, SEL_SHIFT)
    causal = lane <= q_blk
    n_top = min(SEL_TOPK, n_sel)

    @pl.when((i + 1) * tq <= n_top * SEL_BLOCK)
    def _():
        sel_ref[0, 0] = jnp.where(causal, 0.0, -MASK_BIG).astype(bf16)

    @pl.when((i + 1) * tq > n_top * SEL_BLOCK)
    def _():
        psum = p[0] + p[1] + p[2] + p[3]
        imp = _dot_f32_by_exact(psum, ov_ref[...])
        forced = (lane == 0) | (lane == q_blk) | (lane == q_blk - 1)
        key = jnp.where(forced, -2.0, jnp.where(causal, imp, -1.0))
        key = jnp.where(lane < n_sel, key, -3.0)
        lane_f = lane.astype(f32)
        sel = forced
        for _ in range(n_top - 3):
            mx = jnp.max(key, axis=-1, keepdims=True)
            first = jnp.min(jnp.where(key == mx, lane_f, float(LANES)), axis=-1, keepdims=True)
            pick = lane_f == first
            sel = sel | pick
            key = jnp.where(pick, -2.0, key)
        sel = ((q_blk < n_top) | sel) & causal
        sel_ref[0, 0] = jnp.where(sel, 0.0, -MASK_BIG).astype(bf16)


def _cmp(nqu, kcmp, vcmp, gates, n_cmp):
    n_batch, _, seq_len, _ = nqu.shape
    n_rows = kcmp.shape[2]
    n_sel = seq_len // SEL_BLOCK
    assert 3 <= n_sel <= SEL_LANES
    tq = 1024
    c0 = np.arange(n_rows) * CMP_STRIDE
    s0 = np.arange(SEL_LANES) * SEL_BLOCK
    overlap = np.clip(np.minimum(c0[:, None] + CMP_LEN, s0[None, :] + SEL_BLOCK)
                      - np.maximum(c0[:, None], s0[None, :]), 0, None).astype(np.float32) / CMP_LEN
    overlap[n_cmp:, :] = 0.0
    overlap[:, n_sel:] = 0.0
    return pl.pallas_call(
        functools.partial(_cmp_kernel, tq=tq, n_rows=n_rows, n_cmp=n_cmp, n_sel=n_sel),
        out_shape=[
            jax.ShapeDtypeStruct((n_batch, seq_len, NSA_W), bf16),
            jax.ShapeDtypeStruct((n_batch, NSA_KV_HEADS, seq_len, SEL_LANES), bf16),
        ],
        grid=(n_batch, NSA_KV_HEADS, seq_len // tq),
        in_specs=[
            pl.BlockSpec((1, NSA_GROUP, tq, HEAD_DIM), lambda b, g, i: (b, g, i, 0)),
            pl.BlockSpec((1, 1, n_rows, HEAD_DIM), lambda b, g, i: (b, g, 0, 0)),
            pl.BlockSpec((1, 1, n_rows, HEAD_DIM), lambda b, g, i: (b, g, 0, 0)),
            pl.BlockSpec((1, tq, LANES), lambda b, g, i: (b, i, 0)),
            pl.BlockSpec((n_rows, SEL_LANES), lambda b, g, i: (0, 0)),
        ],
        out_specs=[
            pl.BlockSpec((1, tq, NSA_GROUP * HEAD_DIM), lambda b, g, i: (b, i, g)),
            pl.BlockSpec((1, 1, tq, SEL_LANES), lambda b, g, i: (b, g, i, 0)),
        ],
        compiler_params=_params("arbitrary", "arbitrary", "arbitrary"),
        name="cmp",
    )(nqu, kcmp, vcmp, gates, jnp.asarray(overlap, dtype=bf16))


def _write_gated(o_ref, gate_ref, out, g, branch, tq):
    gates = gate_ref[0]
    lane = lax.broadcasted_iota(jnp.int32, (tq, LANES), 1)
    for h in range(NSA_GROUP):
        gcol = FOX_HEADS + (g * NSA_GROUP + h) * N_BRANCH + branch
        gh = jnp.sum(jnp.where(lane == gcol, gates, 0.0), axis=-1, keepdims=True)
        o_ref[0, :, h * HEAD_DIM:(h + 1) * HEAD_DIM] = (gh * out[h * tq:(h + 1) * tq]).astype(bf16)


def _slc_kernel(q_ref, sel_ref, qn_ref, seln_ref, k_ref, vt_ref, gate_ref, *rest, tq, tk, n_cast_w):
    o_ref = rest[n_cast_w]
    bufs = rest[2 * n_cast_w + 1:]
    for w32_ref, w16_ref in zip(rest[:n_cast_w], rest[n_cast_w + 1:2 * n_cast_w + 1]):
        w16_ref[...] = w32_ref[...].astype(bf16)
    i = pl.program_id(1)
    hg = NSA_GROUP
    n_q = hg * tq
    q0 = i * tq
    jd = q0 // tk

    def diag_mask(s):
        kp = jd * tk + lax.broadcasted_iota(jnp.int32, (tk, n_q), 0)
        t = q0 + (lax.broadcasted_iota(jnp.int32, (tk, n_q), 1) & (tq - 1))
        return jnp.where(kp <= t, s, -jnp.inf)

    def chain(g):
        def q_aug(qr, sr):
            q = qr[0, g * hg:(g + 1) * hg].reshape(n_q, HEAD_DIM)
            return jnp.concatenate([q, jnp.concatenate([sr[0, g]] * hg, axis=0)], axis=1)

        qa = q_aug(q_ref, sel_ref)

        def scores(j):
            return _dot_nt(k_ref[0, g, pl.ds(pl.multiple_of(j * tk, tk), tk), :], qa)

        def next_first():
            return _dot_nt(k_ref[0, g, 0:tk, :], q_aug(qn_ref, seln_ref))

        def emit(out):
            gates = gate_ref[0]
            lane = lax.broadcasted_iota(jnp.int32, (tq, LANES), 1)
            for h in range(hg):
                head = g * hg + h
                gh = jnp.sum(jnp.where(lane == FOX_HEADS + head * N_BRANCH + 1, gates, 0.0),
                             axis=-1, keepdims=True)
                o_ref[0, :, head * HEAD_DIM:(head + 1) * HEAD_DIM] = (
                    gh * out[:, h * tq:(h + 1) * tq].T).astype(bf16)

        return _Chain(scores, lambda j: vt_ref[0, g, j], lambda j: None, diag_mask, emit, next_first,
                      bufs[2 * g], bufs[2 * g + 1])

    _flash_sweep(jd, [chain(g) for g in range(NSA_KV_HEADS)], n_q, i == 0)


def _slc(nqr, selneg, kaug, vst, gates, cast_weights):
    n_batch, _, seq_len, _ = nqr.shape
    tq, tk = 256, KV_BLOCK
    nkb = seq_len // tk
    assert tq & (tq - 1) == 0
    n_q = NSA_GROUP * tq
    n_i = seq_len // tq
    cast_in, cast_out, cast_shapes = _cast_side_job(cast_weights, n_batch * n_i, lambda b, i: b * n_i + i)
    return pl.pallas_call(
        functools.partial(_slc_kernel, tq=tq, tk=tk, n_cast_w=len(cast_weights)),
        out_shape=[jax.ShapeDtypeStruct((n_batch, seq_len, NSA_W), bf16)] + cast_shapes,
        grid=(n_batch, n_i),
        in_specs=[
            pl.BlockSpec((1, NSA_HEADS, tq, HEAD_DIM), lambda b, i: (b, 0, i, 0)),
            pl.BlockSpec((1, NSA_KV_HEADS, tq, SEL_LANES), lambda b, i: (b, 0, i, 0)),
            pl.BlockSpec((1, NSA_HEADS, tq, HEAD_DIM), lambda b, i: (b, 0, jnp.minimum(i + 1, n_i - 1), 0)),
            pl.BlockSpec((1, NSA_KV_HEADS, tq, SEL_LANES), lambda b, i: (b, 0, jnp.minimum(i + 1, n_i - 1), 0)),
            pl.BlockSpec((1, NSA_KV_HEADS, seq_len, HEAD_DIM + SEL_LANES), lambda b, i: (b, 0, 0, 0)),
            pl.BlockSpec((1, NSA_KV_HEADS, nkb, HEAD_DIM, tk), lambda b, i: (b, 0, 0, 0, 0)),
            pl.BlockSpec((1, tq, LANES), lambda b, i: (b, i, 0)),
        ] + cast_in,
        out_specs=[pl.BlockSpec((1, tq, NSA_W), lambda b, i: (b, i, 0))] + cast_out,
        scratch_shapes=[pltpu.VMEM((2, tk, n_q), f32), pltpu.VMEM((2, tk, n_q), bf16)] * NSA_KV_HEADS,
        compiler_params=_params("arbitrary", "arbitrary"),
        name="slc",
    )(nqr, selneg, nqr, selneg, kaug, vst, gates, *cast_weights)


def _win_kernel(q_ref, k_ref, vt_ref, gate_ref, o_ref, *, tq, n_sub):
    i = pl.program_id(2)
    g = pl.program_id(1)
    hg = NSA_GROUP
    span = WINDOW + tq
    q0s = [(i * n_sub + u) * tq for u in range(n_sub)]
    k0s = [pl.multiple_of(jnp.maximum(q0 - WINDOW, 0), tq) for q0 in q0s]
    scores = []
    for u in range(n_sub):
        q = q_ref[0, :, u * tq:(u + 1) * tq, :].reshape(hg * tq, HEAD_DIM)
        scores.append(_dot_nt(k_ref[0, 0, pl.ds(k0s[u], span), :], q))
    probs, denoms = [], []
    for u in range(n_sub):
        kp = k0s[u] + lax.broadcasted_iota(jnp.int32, (span, tq), 0)
        t = q0s[u] + lax.broadcasted_iota(jnp.int32, (span, tq), 1)
        diff = t - kp
        bias = jnp.where((diff >= 0) & (diff < WINDOW), 0.0, -jnp.inf)
        s = jnp.concatenate([scores[u][:, h * tq:(h + 1) * tq] + bias for h in range(hg)], axis=1)
        e = jnp.exp2(s - jnp.max(s, axis=0, keepdims=True))
        denoms.append(jnp.sum(e, axis=0, keepdims=True))
        probs.append(e.astype(bf16))
    gates = gate_ref[0]
    lane = lax.broadcasted_iota(jnp.int32, (tq, LANES), 1)
    for u in range(n_sub):
        jb = k0s[u] // tq
        vt = jnp.concatenate([vt_ref[0, 0, jb + c] for c in range(span // tq)], axis=1)
        out = _dot(vt, probs[u]) / denoms[u]
        for h in range(hg):
            gcol = FOX_HEADS + (g * hg + h) * N_BRANCH + 2
            gh = jnp.sum(jnp.where(lane == gcol, gates[u * tq:(u + 1) * tq], 0.0), axis=-1, keepdims=True)
            o_ref[0, u * tq:(u + 1) * tq, h * HEAD_DIM:(h + 1) * HEAD_DIM] = (
                gh * out[:, h * tq:(h + 1) * tq].T).astype(bf16)


def _win(nqr, kw, vwt, gates):
    n_batch, _, seq_len, _ = nqr.shape
    tq, n_sub = WIN_TILE, 8
    assert seq_len >= WINDOW + tq and WINDOW % tq == 0
    return pl.pallas_call(
        functools.partial(_win_kernel, tq=tq, n_sub=n_sub),
        out_shape=jax.ShapeDtypeStruct((n_batch, seq_len, NSA_W), bf16),
        grid=(n_batch, NSA_KV_HEADS, seq_len // (tq * n_sub)),
        in_specs=[
            pl.BlockSpec((1, NSA_GROUP, tq * n_sub, HEAD_DIM), lambda b, g, i: (b, g, i, 0)),
            pl.BlockSpec((1, 1, seq_len, HEAD_DIM), lambda b, g, i: (b, g, 0, 0)),
            pl.BlockSpec((1, 1, seq_len // tq, HEAD_DIM, tq), lambda b, g, i: (b, g, 0, 0, 0)),
            pl.BlockSpec((1, tq * n_sub, LANES), lambda b, g, i: (b, i, 0)),
        ],
        out_specs=pl.BlockSpec((1, tq * n_sub, NSA_GROUP * HEAD_DIM), lambda b, g, i: (b, i, g)),
        compiler_params=_params("arbitrary", "arbitrary", "arbitrary"),
        name="win",
    )(nqr, kw, vwt, gates)


def _outproj_kernel(fox_ref, c_ref, s_ref, w_ref, x_ref, mod_ref, wo_ref, o_ref):
    mix = _dot(fox_ref[...], wo_ref[0:FOX_W, :])
    nsa = c_ref[...].astype(f32) + s_ref[...].astype(f32) + w_ref[...].astype(f32)
    mix = mix + _dot(nsa.astype(bf16), wo_ref[FOX_W:FOX_W + NSA_W, :])
    o_ref[...] = x_ref[...] + mod_ref[0][2:3] * mix


def _outproj(ofox, ocmp, oslc, owin, x2d, mod3, wo_bf16, seq_len):
    m_rows = x2d.shape[0]
    tm = 512
    per_b = seq_len // tm
    half = pl.BlockSpec((tm, FOX_W), lambda i: (i, 0))
    return pl.pallas_call(
        _outproj_kernel,
        out_shape=jax.ShapeDtypeStruct((m_rows, D_MODEL), f32),
        grid=(m_rows // tm,),
        in_specs=[
            half, half, half, half,
            pl.BlockSpec((tm, D_MODEL), lambda i: (i, 0)),
            pl.BlockSpec((1, 6, D_MODEL), lambda i: (i // per_b, 0, 0)),
            pl.BlockSpec((D_MODEL, D_MODEL), lambda i: (0, 0)),
        ],
        out_specs=pl.BlockSpec((tm, D_MODEL), lambda i: (i, 0)),
        compiler_params=_params("arbitrary"),
        name="outproj",
    )(ofox, ocmp, oslc, owin, x2d, mod3, wo_bf16)


def _mlp_kernel(x_ref, mod_ref, g_ref, wu_ref, wd_ref, o_ref, h_ref, acc_ref):
    f = pl.program_id(1)

    @pl.when(f == 0)
    def _():
        md = mod_ref[0]
        y = _rms(x_ref[...], g_ref[...])
        h_ref[...] = (y * (1.0 + md[4:5]) + md[3:4]).astype(bf16)
        acc_ref[...] = jnp.zeros_like(acc_ref)

    u = jnp.maximum(_dot(h_ref[...], wu_ref[...]), 0.0)
    acc_ref[...] += _dot((u * u).astype(bf16), wd_ref[...])

    @pl.when(f == pl.num_programs(1) - 1)
    def _():
        o_ref[...] = x_ref[...] + mod_ref[0][5:6] * acc_ref[...]


def _mlp(x2d, mod3, norm_g, wu_bf16, wd_bf16, seq_len):
    m_rows = x2d.shape[0]
    tm, tf = 512, 1024
    per_b = seq_len // tm
    return pl.pallas_call(
        _mlp_kernel,
        out_shape=jax.ShapeDtypeStruct((m_rows, D_MODEL), f32),
        grid=(m_rows // tm, D_FF // tf),
        in_specs=[
            pl.BlockSpec((tm, D_MODEL), lambda i, f: (i, 0)),
            pl.BlockSpec((1, 6, D_MODEL), lambda i, f: (i // per_b, 0, 0)),
            pl.BlockSpec((1, D_MODEL), lambda i, f: (0, 0)),
            pl.BlockSpec((D_MODEL, tf), lambda i, f: (0, f)),
            pl.BlockSpec((tf, D_MODEL), lambda i, f: (f, 0)),
        ],
        out_specs=pl.BlockSpec((tm, D_MODEL), lambda i, f: (i, 0)),
        scratch_shapes=[pltpu.VMEM((tm, D_MODEL), bf16), pltpu.VMEM((tm, D_MODEL), f32)],
        compiler_params=_params("arbitrary", "arbitrary"),
        name="mlp",
    )(x2d, mod3, norm_g, wu_bf16, wd_bf16)


def _layer(x, c, w_ada, b_ada, norm1_g, w_in, b_forget, fox_q_norm, fox_k_norm, nsa_q_norm,
           cmp_k_norm, slc_k_norm, win_k_norm, cmp_pe_k, cmp_w1_k, cmp_w2_k, cmp_pe_v, cmp_w1_v,
           cmp_w2_v, w_out, norm2_g, w_up, w_down):
    n_batch, seq_len, _ = x.shape
    n_cmp = (seq_len - CMP_LEN) // CMP_STRIDE + 1
    row = lambda v: v.reshape(1, -1)

    half = HEAD_DIM // 2
    inv_freq = ROPE_THETA ** (-jnp.arange(half, dtype=f32) / half)
    inv_freq = jnp.concatenate([inv_freq, inv_freq]).reshape(1, HEAD_DIM)

    w_a, w_b, w_s = _repack_w_in(w_in.T)

    mod3 = _ada(c, w_ada, b_ada).reshape(n_batch, 6, D_MODEL)
    x2d = x.reshape(n_batch * seq_len, D_MODEL)
    (fq, fk, fvt, cin, nqu, nqr, kc, vc, kaug, vst, kw, vw, gates) = _prep(
        x2d, mod3, row(norm1_g), w_a, w_b, w_s, n_batch, seq_len, row(fox_q_norm), row(fox_k_norm), row(nsa_q_norm), row(slc_k_norm),
        row(win_k_norm), jnp.pad(b_forget, (0, LANES - FOX_HEADS)).reshape(1, LANES), inv_freq)
    kcmp = _compress(kc, cmp_pe_k, cmp_w1_k.astype(bf16), cmp_w2_k.astype(bf16), row(cmp_k_norm))
    vcmp = _compress(vc, cmp_pe_v, cmp_w1_v.astype(bf16), cmp_w2_v.astype(bf16))
    ofox = _fox(fq, fk, fvt, cin)
    ocmp, selneg = _cmp(nqu, kcmp, vcmp, gates, n_cmp)
    oslc, w_up16, w_down16, w_out16 = _slc(nqr, selneg, kaug, vst, gates, [w_up, w_down, w_out])
    owin = _win(nqr, kw, vw, gates)
    x1 = _outproj(ofox.reshape(-1, FOX_W), ocmp.reshape(-1, NSA_W), oslc.reshape(-1, NSA_W),
                  owin.reshape(-1, NSA_W), x2d, mod3, w_out16, seq_len)
    x2 = _mlp(x1, mod3, row(norm2_g), w_up16, w_down16, seq_len)
    return x2.reshape(n_batch, seq_len, D_MODEL)


def kernel(x, c, w_ada, b_ada, norm1_g, w_in, b_forget, fox_q_norm, fox_k_norm, nsa_q_norm, cmp_k_norm,
           slc_k_norm, win_k_norm, cmp_pe_k, cmp_w1_k, cmp_w2_k, cmp_pe_v, cmp_w1_v, cmp_w2_v, w_out,
           norm2_g, w_up, w_down):
    depth = w_ada.shape[0]
    for l in range(depth):
        x = _layer(x, c, w_ada[l], b_ada[l], norm1_g[l], w_in[l], b_forget[l], fox_q_norm[l], fox_k_norm[l],
                   nsa_q_norm[l], cmp_k_norm[l], slc_k_norm[l], win_k_norm[l], cmp_pe_k[l], cmp_w1_k[l],
                   cmp_w2_k[l], cmp_pe_v[l], cmp_w1_v[l], cmp_w2_v[l], w_out[l], norm2_g[l], w_up[l],
                   w_down[l])
    return x
```

```python
import functools
import math
from typing import Any, Callable, NamedTuple

import numpy as np
import jax
import jax.numpy as jnp
from jax import lax
from jax.experimental import pallas as pl
from jax.experimental.pallas import tpu as pltpu

D_MODEL = 2048
HEAD_DIM = 128
FOX_HEADS = 8
NSA_HEADS = 8
NSA_KV_HEADS = 2
NSA_GROUP = NSA_HEADS // NSA_KV_HEADS
N_BRANCH = 3
D_FF = 4 * D_MODEL
ROPE_THETA = 10000.0
CMP_LEN = 32
CMP_STRIDE = 16
CMP_SHIFT = 4
CMP_HIDDEN = 2 * HEAD_DIM
SEL_BLOCK = 64
SEL_SHIFT = 6
SEL_TOPK = 16
WINDOW = 512
NORM_EPS = 1e-6
ATTN_SCALE = HEAD_DIM ** -0.5
FOX_W = FOX_HEADS * HEAD_DIM
NSA_W = NSA_HEADS * HEAD_DIM
KV_W = NSA_KV_HEADS * HEAD_DIM

LANES = 128
SEL_LANES = LANES
MASK_BIG = 1e30
KV_BLOCK = 512
WIN_TILE = 128
LOG2E = math.log2(math.e)

COL_FQ = 0
COL_FK = COL_FQ + FOX_W
COL_FV = COL_FK + FOX_W
COL_NQ = COL_FV + FOX_W
COL_KC = COL_NQ + NSA_W
COL_VC = COL_KC + KV_W
COL_KS = COL_VC + KV_W
COL_VS = COL_KS + KV_W
COL_KW = COL_VS + KV_W
COL_VW = COL_KW + KV_W
COL_SMALL = COL_VW + KV_W
W_IN_Z0 = 3 * FOX_W
W_IN_NQ0 = W_IN_Z0 + FOX_HEADS
W_IN_GZ0 = W_IN_NQ0 + NSA_W + 6 * KV_W

VMEM_LIMIT = 56 * 1024 * 1024
VMEM_LIMIT_PREP = 60 * 1024 * 1024

f32 = jnp.float32
bf16 = jnp.bfloat16


def _params(*sem):
    return pltpu.CompilerParams(dimension_semantics=sem, vmem_limit_bytes=VMEM_LIMIT)


def _dot_nt(a, b):
    return lax.dot_general(a, b, (((1,), (1,)), ((), ())), preferred_element_type=f32)


def _dot(a, b):
    return jnp.dot(a, b, preferred_element_type=f32)


def _split3(x):
    hi = x.astype(bf16)
    r1 = x - hi.astype(f32)
    mid = r1.astype(bf16)
    lo = (r1 - mid.astype(f32)).astype(bf16)
    return hi, mid, lo


def _dot_f32_by_exact(x, w_bf16):
    hi, mid, lo = _split3(x)
    return _dot(hi, w_bf16) + (_dot(mid, w_bf16) + _dot(lo, w_bf16))


def _rms(x, gain):
    ms = jnp.mean(x * x, axis=-1, keepdims=True)
    return x * lax.rsqrt(ms + NORM_EPS) * gain


def _ada_kernel(ct_ref, w_ref, b_ref, o_ref, *, n_batch, k_chunk):
    ct = ct_ref[...]
    act = ct * jax.nn.sigmoid(ct)
    rows = []
    for b in range(n_batch):
        col = act[:, b:b + 1]
        acc = b_ref[...]
        for k0 in range(0, D_MODEL, k_chunk):
            acc = acc + jnp.sum(w_ref[k0:k0 + k_chunk, :] * col[k0:k0 + k_chunk], axis=0, keepdims=True)
        rows.append(acc)
    o_ref[...] = jnp.concatenate(rows, axis=0)


def _ada(c, w_ada, b_ada):
    n_batch = c.shape[0]
    n_out = w_ada.shape[1]
    tn = 1024
    return pl.pallas_call(
        functools.partial(_ada_kernel, n_batch=n_batch, k_chunk=256),
        out_shape=jax.ShapeDtypeStruct((n_batch, n_out), f32),
        grid=(n_out // tn,),
        in_specs=[
            pl.BlockSpec((D_MODEL, n_batch), lambda j: (0, 0)),
            pl.BlockSpec((D_MODEL, tn), lambda j: (0, j)),
            pl.BlockSpec((1, tn), lambda j: (0, j)),
        ],
        out_specs=pl.BlockSpec((n_batch, tn), lambda j: (0, j)),
        compiler_params=_params("arbitrary"),
        name="ada",
    )(c.T, w_ada, b_ada.reshape(1, n_out))


def _repack_kernel(a_ref, *rest, n_part):
    part_refs, (z_ref, gz_ref, wa_ref, wb_ref, ws_ref) = rest[:n_part], rest[n_part:]
    wa_ref[...] = a_ref[...].astype(bf16)
    wb_ref[...] = jnp.concatenate([r[...] for r in part_refs], axis=0).astype(bf16)
    pad = jnp.zeros((LANES - z_ref.shape[0] - gz_ref.shape[0], D_MODEL), f32)
    ws_ref[...] = jnp.concatenate([z_ref[...], gz_ref[...], pad], axis=0).astype(bf16)


def _repack_w_in(w_t):
    n_steps = 16
    ra, rb = W_IN_Z0 // n_steps, (W_IN_GZ0 - W_IN_NQ0) // n_steps
    hb = math.gcd(W_IN_NQ0, rb)
    n_part = rb // hb
    n_z, n_gz = W_IN_NQ0 - W_IN_Z0, w_t.shape[0] - W_IN_GZ0
    assert W_IN_Z0 % n_steps == 0 and (W_IN_GZ0 - W_IN_NQ0) % n_steps == 0 and ra % 16 == 0 and rb % 16 == 0
    assert hb % 8 == 0 and W_IN_Z0 % n_z == 0 and W_IN_GZ0 % n_gz == 0
    b_off = W_IN_NQ0 // hb
    part = lambda k: pl.BlockSpec((hb, D_MODEL), lambda j: (b_off + n_part * j + k, 0))
    return pl.pallas_call(
        functools.partial(_repack_kernel, n_part=n_part),
        out_shape=[jax.ShapeDtypeStruct((W_IN_Z0, D_MODEL), bf16),
                   jax.ShapeDtypeStruct((W_IN_GZ0 - W_IN_NQ0, D_MODEL), bf16),
                   jax.ShapeDtypeStruct((LANES, D_MODEL), bf16)],
        grid=(n_steps,),
        in_specs=[pl.BlockSpec((ra, D_MODEL), lambda j: (j, 0))] + [part(k) for k in range(n_part)] + [
            pl.BlockSpec((n_z, D_MODEL), lambda j: (W_IN_Z0 // n_z, 0)),
            pl.BlockSpec((n_gz, D_MODEL), lambda j: (W_IN_GZ0 // n_gz, 0)),
        ],
        out_specs=[pl.BlockSpec((ra, D_MODEL), lambda j: (j, 0)),
                   pl.BlockSpec((rb, D_MODEL), lambda j: (j, 0)),
                   pl.BlockSpec((LANES, D_MODEL), lambda j: (0, 0))],
        compiler_params=_params("arbitrary"),
        name="repack",
    )(*([w_t] * (n_part + 3)))


def _cast_side_job(weights, n_steps, step_of):
    n_cast = max(c for c in range(1, n_steps + 1) if all(w.shape[0] % (16 * c) == 0 for w in weights))
    idx = lambda *ids: (jnp.minimum(step_of(*ids), n_cast - 1), 0)
    specs = [pl.BlockSpec((w.shape[0] // n_cast, w.shape[1]), idx) for w in weights]
    return specs, specs, [jax.ShapeDtypeStruct(w.shape, bf16) for w in weights]


def _prep_kernel(x_ref, mod_ref, g1_ref, wa_ref, wb_ref, ws_ref,
                 gq_ref, gk_ref, gn_ref, gs_ref, gw_ref, bf_ref, inv_ref,
                 fq_ref, fk_ref, fvt_ref, cin_ref, nqu_ref, nqr_ref, kc_ref, vc_ref,
                 ks_ref, vst_ref, kw_ref, vw_ref, gate_ref, carry_ref, rot_ref, *, tm):
    i = pl.program_id(1)

    @pl.when(i == 0)
    def _():
        carry_ref[...] = jnp.zeros_like(carry_ref)

    md = mod_ref[0]
    hn = (_rms(x_ref[...], g1_ref[...]) * (1.0 + md[1:2]) + md[0:1]).astype(bf16)
    groups = {}

    def head(col, h):
        base, w_ref, r0 = next((b, w, r) for b, w, r in (
            (COL_KC, wb_ref, NSA_W), (COL_NQ, wb_ref, 0), (COL_FV, wa_ref, 2 * FOX_W),
            (COL_FK, wa_ref, FOX_W), (COL_FQ, wa_ref, 0)) if col >= b)
        if base not in groups:
            n = (COL_SMALL - COL_KC) if base == COL_KC else FOX_W
            groups[base] = _dot_nt(hn, w_ref[r0:r0 + n, :])
        c0 = col - base + h * HEAD_DIM
        return groups[base][:, c0:c0 + HEAD_DIM]

    row = lax.broadcasted_iota(jnp.int32, (tm, LANES), 0)
    lane = lax.broadcasted_iota(jnp.int32, (tm, LANES), 1)
    pos = i * tm + row

    @pl.when((pl.program_id(0) == 0) & (i == 0))
    def _():
        ang_row = row.astype(f32) * inv_ref[...]
        rot_ref[0] = jnp.cos(ang_row)
        rot_ref[1] = jnp.sin(ang_row)

    ang0 = (i * tm).astype(f32) * inv_ref[...]
    cos0, sin0 = jnp.cos(ang0), jnp.sin(ang0)
    cos = cos0 * rot_ref[0] - sin0 * rot_ref[1]
    sin = sin0 * rot_ref[0] + cos0 * rot_ref[1]
    sin_signed = jnp.where(lane < HEAD_DIM // 2, -sin, sin)

    def rope(x):
        return x * cos + pltpu.roll(x, HEAD_DIM // 2, 1) * sin_signed

    small = _dot_nt(hn, ws_ref[...])
    z = small + bf_ref[...]
    logf = jnp.minimum(z, 0.0) - jnp.log1p(jnp.exp(-jnp.abs(z)))
    t_idx = lax.broadcasted_iota(jnp.int32, (tm, tm), 0)
    s_idx = lax.broadcasted_iota(jnp.int32, (tm, tm), 1)
    tri = jnp.where(s_idx <= t_idx, 1.0, 0.0).astype(bf16)
    hi, mid, lo = _split3(logf)
    local = _dot(tri, hi) + (_dot(tri, mid) + _dot(tri, lo))
    cin_ref[0, 0] = carry_ref[0:1, :]
    carry_ref[...] = carry_ref[...] + local[tm - 1:tm, :]
    b_hi, b_mid, b_lo = (v.astype(f32) for v in _split3(local * (-LOG2E)))
    ones3 = jnp.where(lane < 3, 1.0, 0.0).astype(bf16)

    for h in range(FOX_HEADS):
        q = _rms(head(COL_FQ, h), gq_ref[...] * (ATTN_SCALE * LOG2E))
        fq_ref[0, h] = jnp.concatenate([q.astype(bf16), ones3], axis=1)
    for h in range(FOX_HEADS):
        k = _rms(head(COL_FK, h), gk_ref[...])
        bias = jnp.where(lane == 0, b_hi[:, h:h + 1],
                         jnp.where(lane == 1, b_mid[:, h:h + 1],
                                   jnp.where(lane == 2, b_lo[:, h:h + 1], 0.0)))
        fk_ref[0, h] = jnp.concatenate([k.astype(bf16), bias.astype(bf16)], axis=1)

    gate_ref[0] = jax.nn.sigmoid(small)

    for h in range(NSA_HEADS):
        qn = _rms(head(COL_NQ, h), gn_ref[...] * (ATTN_SCALE * LOG2E))
        nqu_ref[0, h] = qn.astype(bf16)
        nqr_ref[0, h] = rope(qn).astype(bf16)
    onehot = jnp.where(lane == lax.shift_right_logical(pos, SEL_SHIFT), 1.0, 0.0).astype(bf16)
    for g in range(NSA_KV_HEADS):
        kc_ref[0, g] = head(COL_KC, g)
        vc_ref[0, g] = head(COL_VC, g)
        ks = rope(_rms(head(COL_KS, g), gs_ref[...])).astype(bf16)
        ks_ref[0, g] = jnp.concatenate([ks, onehot], axis=1)
        vst_ref[0, g, 0] = head(COL_VS, g).T.astype(bf16)
        kw_ref[0, g] = rope(_rms(head(COL_KW, g), gw_ref[...])).astype(bf16)
        vwt = head(COL_VW, g).T.astype(bf16)
        for c in range(tm // WIN_TILE):
            vw_ref[0, g, c] = vwt[:, c * WIN_TILE:(c + 1) * WIN_TILE]
    for h in range(FOX_HEADS):
        fvt_ref[0, h, 0] = head(COL_FV, h).T.astype(bf16)


def _prep(x2d, mod3, norm_g, wa, wb, ws, n_batch, seq_len, gq, gk, gn, gs, gw, b_forget_row, inv_freq):
    resident = lambda w: pl.BlockSpec(w.shape, lambda b, i: (0, 0), pipeline_mode=pl.Buffered(1))
    tm = KV_BLOCK
    per_b = seq_len // tm
    hshape = lambda n, w, dt: jax.ShapeDtypeStruct((n_batch, n, seq_len, w), dt)
    hspec = lambda n, w: pl.BlockSpec((1, n, tm, w), lambda b, i: (b, 0, i, 0))
    tshape = lambda n: jax.ShapeDtypeStruct((n_batch, n, per_b, HEAD_DIM, tm), bf16)
    tspec = lambda n: pl.BlockSpec((1, n, 1, HEAD_DIM, tm), lambda b, i: (b, 0, i, 0, 0))
    vec = pl.BlockSpec((1, LANES), lambda b, i: (0, 0))
    return pl.pallas_call(
        functools.partial(_prep_kernel, tm=tm),
        out_shape=[
            hshape(FOX_HEADS, 2 * HEAD_DIM, bf16), hshape(FOX_HEADS, 2 * HEAD_DIM, bf16), tshape(FOX_HEADS),
            jax.ShapeDtypeStruct((n_batch, per_b, 1, LANES), f32),
            hshape(NSA_HEADS, HEAD_DIM, bf16), hshape(NSA_HEADS, HEAD_DIM, bf16),
            hshape(NSA_KV_HEADS, HEAD_DIM, f32), hshape(NSA_KV_HEADS, HEAD_DIM, f32),
            hshape(NSA_KV_HEADS, HEAD_DIM + SEL_LANES, bf16), tshape(NSA_KV_HEADS),
            hshape(NSA_KV_HEADS, HEAD_DIM, bf16),
            jax.ShapeDtypeStruct((n_batch, NSA_KV_HEADS, seq_len // WIN_TILE, HEAD_DIM, WIN_TILE), bf16),
            jax.ShapeDtypeStruct((n_batch, seq_len, LANES), f32),
        ],
        grid=(n_batch, per_b),
        in_specs=[
            pl.BlockSpec((tm, D_MODEL), lambda b, i: (b * per_b + i, 0)),
            pl.BlockSpec((1, 6, D_MODEL), lambda b, i: (b, 0, 0)),
            pl.BlockSpec((1, D_MODEL), lambda b, i: (0, 0)),
            resident(wa), resident(wb), resident(ws),
            vec, vec, vec, vec, vec, vec, vec,
        ],
        out_specs=[
            hspec(FOX_HEADS, 2 * HEAD_DIM), hspec(FOX_HEADS, 2 * HEAD_DIM), tspec(FOX_HEADS),
            pl.BlockSpec((1, 1, 1, LANES), lambda b, i: (b, i, 0, 0)),
            hspec(NSA_HEADS, HEAD_DIM), hspec(NSA_HEADS, HEAD_DIM),
            hspec(NSA_KV_HEADS, HEAD_DIM), hspec(NSA_KV_HEADS, HEAD_DIM),
            hspec(NSA_KV_HEADS, HEAD_DIM + SEL_LANES), tspec(NSA_KV_HEADS),
            hspec(NSA_KV_HEADS, HEAD_DIM),
            pl.BlockSpec((1, NSA_KV_HEADS, tm // WIN_TILE, HEAD_DIM, WIN_TILE), lambda b, i: (b, 0, i, 0, 0)),
            pl.BlockSpec((1, tm, LANES), lambda b, i: (b, i, 0)),
        ],
        scratch_shapes=[pltpu.VMEM((8, LANES), f32), pltpu.VMEM((2, tm, LANES), f32)],
        compiler_params=pltpu.CompilerParams(dimension_semantics=("arbitrary", "arbitrary"),
                                             vmem_limit_bytes=VMEM_LIMIT_PREP),
        name="prep",
    )(x2d, mod3, norm_g, wa, wb, ws, gq, gk, gn, gs, gw, b_forget_row, inv_freq)


def _compress_kernel(x_ref, pe_ref, w1_ref, w2_ref, *rest, n_rows, do_norm):
    o_ref = rest[-1]
    a = jnp.zeros((n_rows, CMP_HIDDEN), f32)
    b = jnp.zeros((n_rows, CMP_HIDDEN), f32)
    for l in range(CMP_STRIDE):
        x = x_ref[0, 0, pl.ds(l, n_rows, stride=CMP_STRIDE), :]
        a = a + _dot((x + pe_ref[l:l + 1, :]).astype(bf16), w1_ref[l * HEAD_DIM:(l + 1) * HEAD_DIM, :])
        lb = CMP_STRIDE + l
        b = b + _dot((x + pe_ref[lb:lb + 1, :]).astype(bf16), w1_ref[lb * HEAD_DIM:(lb + 1) * HEAD_DIM, :])
    pre = a + pltpu.roll(b, n_rows - 1, 0)
    hid = pre * jax.nn.sigmoid(pre)
    out = _dot(hid.astype(bf16), w2_ref[...])
    if do_norm:
        out = _rms(out, rest[0][...])
    o_ref[0, 0] = out.astype(bf16)


def _compress(x4, pe, w1_bf16, w2_bf16, gain=None):
    n_batch, n_g, seq_len, _ = x4.shape
    n_rows = seq_len // CMP_STRIDE
    do_norm = gain is not None
    return pl.pallas_call(
        functools.partial(_compress_kernel, n_rows=n_rows, do_norm=do_norm),
        out_shape=jax.ShapeDtypeStruct((n_batch, n_g, n_rows, HEAD_DIM), bf16),
        grid=(n_batch, n_g),
        in_specs=[
            pl.BlockSpec((1, 1, seq_len, HEAD_DIM), lambda b, g: (b, g, 0, 0)),
            pl.BlockSpec((CMP_LEN, HEAD_DIM), lambda b, g: (0, 0)),
            pl.BlockSpec((CMP_LEN * HEAD_DIM, CMP_HIDDEN), lambda b, g: (0, 0)),
            pl.BlockSpec((CMP_HIDDEN, HEAD_DIM), lambda b, g: (0, 0)),
        ] + ([pl.BlockSpec((1, HEAD_DIM), lambda b, g: (0, 0))] if do_norm else []),
        out_specs=pl.BlockSpec((1, 1, n_rows, HEAD_DIM), lambda b, g: (b, g, 0, 0)),
        compiler_params=_params("arbitrary", "arbitrary"),
        name="compress",
    )(x4, pe, w1_bf16, w2_bf16, *([gain] if do_norm else []))


def _softmax_block(s, d, m_prev, l_prev):
    m_blk = jnp.max(s, axis=0, keepdims=True)
    m_new = jnp.maximum(m_prev, m_blk if d is None else m_blk + d)
    p = jnp.exp2(s - (m_new if d is None else m_new - d))
    alpha = jnp.exp2(m_prev - m_new)
    l_new = alpha * l_prev + jnp.sum(p, axis=0, keepdims=True)
    return m_new, l_new, alpha, p.astype(bf16)


class _Chain(NamedTuple):
    scores: Callable
    values: Callable
    offset: Callable
    diag_mask: Callable
    emit: Callable
    next_first: Callable
    s_buf: Any
    p_buf: Any


def _flash_sweep(n_full, chains, n_q, is_first):
    @pl.when(is_first)
    def _():
        for c in chains:
            c.s_buf[0] = c.scores(0)

    for c in chains:
        c.p_buf[1] = jnp.zeros(c.p_buf.shape[1:], c.p_buf.dtype)

    def accumulate(c, j, slot, alpha, acc):
        return alpha * acc + _dot(c.values(jnp.maximum(j, 0)), c.p_buf[slot])

    def step(j, cur, states):
        out = []
        for c, (m, l, acc, alpha_prev) in zip(chains, states):
            acc = accumulate(c, j - 1, 1 - cur, alpha_prev, acc)
            m, l, alpha, p = _softmax_block(c.s_buf[cur], c.offset(j), m, l)
            c.p_buf[cur] = p
            c.s_buf[1 - cur] = c.scores(j + 1)
            out.append((m, l, acc, alpha))
        return tuple(out)

    def finish(cur, states):
        nxt = [c.next_first() for c in chains]
        for c, s_next, (m, l, acc, alpha_prev) in zip(chains, nxt, states):
            if cur == 1:
                c.s_buf[0] = s_next
            acc = accumulate(c, n_full - 1, 1 - cur, alpha_prev, acc)
            m, l, alpha, p = _softmax_block(c.diag_mask(c.s_buf[cur]), None, m, l)
            if cur == 0:
                c.s_buf[0] = s_next
            acc = alpha * acc + _dot(c.values(n_full), p)
            c.emit(acc / l)

    init = tuple((jnp.full((1, n_q), -jnp.inf, f32), jnp.zeros((1, n_q), f32),
                  jnp.zeros((HEAD_DIM, n_q), f32), jnp.ones((1, n_q), f32)) for _ in chains)
    states = lax.fori_loop(0, n_full // 2, lambda jj, st: step(2 * jj + 1, 1, step(2 * jj, 0, st)), init)

    @pl.when(n_full % 2 == 0)
    def _():
        finish(0, states)

    @pl.when(n_full % 2 == 1)
    def _():
        finish(1, step(n_full - 1, 0, states))


def _fox_kernel(q_ref, qn_ref, k_ref, vt_ref, c_ref, o_ref, *bufs, tq, n_heads):
    hb = pl.program_id(1)
    i = pl.program_id(2)
    lane = lax.broadcasted_iota(jnp.int32, (1, LANES), 1)

    def diag_mask(s):
        kk = lax.broadcasted_iota(jnp.int32, (tq, tq), 0)
        qq = lax.broadcasted_iota(jnp.int32, (tq, tq), 1)
        return jnp.where(kk <= qq, s, -jnp.inf)

    def chain(u):
        qa = q_ref[0, u]

        def cin(j):
            return jnp.sum(jnp.where(lane == hb * n_heads + u, c_ref[0, j], 0.0), axis=-1, keepdims=True)

        ci = cin(i)

        def scores(j):
            return _dot_nt(k_ref[0, u, pl.ds(pl.multiple_of(j * tq, tq), tq), :], qa)

        def emit(out_t):
            o_ref[0, :, u * HEAD_DIM:(u + 1) * HEAD_DIM] = out_t.T.astype(bf16)

        def next_first():
            return _dot_nt(k_ref[0, u, 0:tq, :], qn_ref[0, u])

        return _Chain(scores, lambda j: vt_ref[0, u, j], lambda j: (ci - cin(j)) * LOG2E, diag_mask, emit,
                      next_first, bufs[2 * u], bufs[2 * u + 1])

    _flash_sweep(i, [chain(u) for u in range(n_heads)], tq, i == 0)


def _fox(fq, fk, fvt, cin):
    n_batch, n_h, seq_len, _ = fq.shape
    tq = KV_BLOCK
    nkb = seq_len // tq
    n_heads = 4
    return pl.pallas_call(
        functools.partial(_fox_kernel, tq=tq, n_heads=n_heads),
        out_shape=jax.ShapeDtypeStruct((n_batch, seq_len, n_h * HEAD_DIM), bf16),
        grid=(n_batch, n_h // n_heads, nkb),
        in_specs=[
            pl.BlockSpec((1, n_heads, tq, 2 * HEAD_DIM), lambda b, h, i: (b, h, i, 0)),
            pl.BlockSpec((1, n_heads, tq, 2 * HEAD_DIM), lambda b, h, i: (b, h, jnp.minimum(i + 1, nkb - 1), 0)),
            pl.BlockSpec((1, n_heads, seq_len, 2 * HEAD_DIM), lambda b, h, i: (b, h, 0, 0)),
            pl.BlockSpec((1, n_heads, nkb, HEAD_DIM, tq), lambda b, h, i: (b, h, 0, 0, 0)),
            pl.BlockSpec((1, nkb, 1, LANES), lambda b, h, i: (b, 0, 0, 0)),
        ],
        out_specs=pl.BlockSpec((1, tq, n_heads * HEAD_DIM), lambda b, h, i: (b, i, h)),
        scratch_shapes=[pltpu.VMEM((2, tq, tq), f32), pltpu.VMEM((2, tq, tq), bf16)] * n_heads,
        compiler_params=_params("arbitrary", "arbitrary", "arbitrary"),
        name="fox",
    )(fq, fq, fk, fvt, cin)


def _cmp_kernel(q_ref, k_ref, v_ref, gate_ref, ov_ref, o_ref, sel_ref, *, tq, n_rows, n_cmp, n_sel):
    i = pl.program_id(2)
    g = pl.program_id(1)
    hg = NSA_GROUP
    q = q_ref[0].reshape(hg * tq, HEAD_DIM)
    s = _dot_nt(q, k_ref[0, 0]).reshape(hg, tq, n_rows)
    t = i * tq + lax.broadcasted_iota(jnp.int32, (1, tq, 1), 1)
    n = lax.broadcasted_iota(jnp.int32, (1, 1, n_rows), 2)
    last = jnp.minimum(lax.shift_right_arithmetic(t - (CMP_LEN - 1), CMP_SHIFT), n_cmp - 1)
    s = jnp.where(n <= last, s, -jnp.inf)
    m = jnp.max(s, axis=-1, keepdims=True)
    m = jnp.where(m == -jnp.inf, 0.0, m)
    e = jnp.exp2(s - m)
    p = e * (1.0 / jnp.maximum(jnp.sum(e, axis=-1, keepdims=True), 1e-30))
    o = _dot(p.reshape(hg * tq, n_rows).astype(bf16), v_ref[0, 0])
    _write_gated(o_ref, gate_ref, o, g, 0, tq)

    lane = lax.broadcasted_iota(jnp.int32, (tq, LANES), 1)
    tq_pos = i * tq + lax.broadcasted_iota(jnp.int32, (tq, 1), 0)
    q_blk = lax.shift_right_logical(tq_pos, SEL_SHIFT)
    causal = lane <= q_blk
    n_top = min(SEL_TOPK, n_sel)

    @pl.when((i + 1) * tq <= n_top * SEL_BLOCK)
    def _():
        sel_ref[0, 0] = jnp.where(causal, 0.0, -MASK_BIG).astype(bf16)

    @pl.when((i + 1) * tq > n_top * SEL_BLOCK)
    def _():
        psum = p[0] + p[1] + p[2] + p[3]
        imp = _dot_f32_by_exact(psum, ov_ref[...])
        forced = (lane == 0) | (lane == q_blk) | (lane == q_blk - 1)
        key = jnp.where(forced, -2.0, jnp.where(causal, imp, -1.0))
        key = jnp.where(lane < n_sel, key, -3.0)
        lane_f = lane.astype(f32)
        sel = forced
        for _ in range(n_top - 3):
            mx = jnp.max(key, axis=-1, keepdims=True)
            first = jnp.min(jnp.where(key == mx, lane_f, float(LANES)), axis=-1, keepdims=True)
            pick = lane_f == first
            sel = sel | pick
            key = jnp.where(pick, -2.0, key)
        sel = ((q_blk < n_top) | sel) & causal
        sel_ref[0, 0] = jnp.where(sel, 0.0, -MASK_BIG).astype(bf16)


def _cmp(nqu, kcmp, vcmp, gates, n_cmp):
    n_batch, _, seq_len, _ = nqu.shape
    n_rows = kcmp.shape[2]
    n_sel = seq_len // SEL_BLOCK
    assert 3 <= n_sel <= SEL_LANES
    tq = 1024
    c0 = np.arange(n_rows) * CMP_STRIDE
    s0 = np.arange(SEL_LANES) * SEL_BLOCK
    overlap = np.clip(np.minimum(c0[:, None] + CMP_LEN, s0[None, :] + SEL_BLOCK)
                      - np.maximum(c0[:, None], s0[None, :]), 0, None).astype(np.float32) / CMP_LEN
    overlap[n_cmp:, :] = 0.0
    overlap[:, n_sel:] = 0.0
    return pl.pallas_call(
        functools.partial(_cmp_kernel, tq=tq, n_rows=n_rows, n_cmp=n_cmp, n_sel=n_sel),
        out_shape=[
            jax.ShapeDtypeStruct((n_batch, seq_len, NSA_W), bf16),
            jax.ShapeDtypeStruct((n_batch, NSA_KV_HEADS, seq_len, SEL_LANES), bf16),
        ],
        grid=(n_batch, NSA_KV_HEADS, seq_len // tq),
        in_specs=[
            pl.BlockSpec((1, NSA_GROUP, tq, HEAD_DIM), lambda b, g, i: (b, g, i, 0)),
            pl.BlockSpec((1, 1, n_rows, HEAD_DIM), lambda b, g, i: (b, g, 0, 0)),
            pl.BlockSpec((1, 1, n_rows, HEAD_DIM), lambda b, g, i: (b, g, 0, 0)),
            pl.BlockSpec((1, tq, LANES), lambda b, g, i: (b, i, 0)),
            pl.BlockSpec((n_rows, SEL_LANES), lambda b, g, i: (0, 0)),
        ],
        out_specs=[
            pl.BlockSpec((1, tq, NSA_GROUP * HEAD_DIM), lambda b, g, i: (b, i, g)),
            pl.BlockSpec((1, 1, tq, SEL_LANES), lambda b, g, i: (b, g, i, 0)),
        ],
        compiler_params=_params("arbitrary", "arbitrary", "arbitrary"),
        name="cmp",
    )(nqu, kcmp, vcmp, gates, jnp.asarray(overlap, dtype=bf16))


def _write_gated(o_ref, gate_ref, out, g, branch, tq):
    gates = gate_ref[0]
    lane = lax.broadcasted_iota(jnp.int32, (tq, LANES), 1)
    for h in range(NSA_GROUP):
        gcol = FOX_HEADS + (g * NSA_GROUP + h) * N_BRANCH + branch
        gh = jnp.sum(jnp.where(lane == gcol, gates, 0.0), axis=-1, keepdims=True)
        o_ref[0, :, h * HEAD_DIM:(h + 1) * HEAD_DIM] = (gh * out[h * tq:(h + 1) * tq]).astype(bf16)


def _slc_kernel(q_ref, sel_ref, qn_ref, seln_ref, k_ref, vt_ref, gate_ref, *rest, tq, tk, n_cast_w):
    o_ref = rest[n_cast_w]
    bufs = rest[2 * n_cast_w + 1:]
    for w32_ref, w16_ref in zip(rest[:n_cast_w], rest[n_cast_w + 1:2 * n_cast_w + 1]):
        w16_ref[...] = w32_ref[...].astype(bf16)
    i = pl.program_id(1)
    hg = NSA_GROUP
    n_q = hg * tq
    q0 = i * tq
    jd = q0 // tk

    def diag_mask(s):
        kp = jd * tk + lax.broadcasted_iota(jnp.int32, (tk, n_q), 0)
        t = q0 + (lax.broadcasted_iota(jnp.int32, (tk, n_q), 1) & (tq - 1))
        return jnp.where(kp <= t, s, -jnp.inf)

    def chain(g):
        def q_aug(qr, sr):
            q = qr[0, g * hg:(g + 1) * hg].reshape(n_q, HEAD_DIM)
            return jnp.concatenate([q, jnp.concatenate([sr[0, g]] * hg, axis=0)], axis=1)

        qa = q_aug(q_ref, sel_ref)

        def scores(j):
            return _dot_nt(k_ref[0, g, pl.ds(pl.multiple_of(j * tk, tk), tk), :], qa)

        def next_first():
            return _dot_nt(k_ref[0, g, 0:tk, :], q_aug(qn_ref, seln_ref))

        def emit(out):
            gates = gate_ref[0]
            lane = lax.broadcasted_iota(jnp.int32, (tq, LANES), 1)
            for h in range(hg):
                head = g * hg + h
                gh = jnp.sum(jnp.where(lane == FOX_HEADS + head * N_BRANCH + 1, gates, 0.0),
                             axis=-1, keepdims=True)
                o_ref[0, :, head * HEAD_DIM:(head + 1) * HEAD_DIM] = (
                    gh * out[:, h * tq:(h + 1) * tq].T).astype(bf16)

        return _Chain(scores, lambda j: vt_ref[0, g, j], lambda j: None, diag_mask, emit, next_first,
                      bufs[2 * g], bufs[2 * g + 1])

    _flash_sweep(jd, [chain(g) for g in range(NSA_KV_HEADS)], n_q, i == 0)


def _slc(nqr, selneg, kaug, vst, gates, cast_weights):
    n_batch, _, seq_len, _ = nqr.shape
    tq, tk = 256, KV_BLOCK
    nkb = seq_len // tk
    assert tq & (tq - 1) == 0
    n_q = NSA_GROUP * tq
    n_i = seq_len // tq
    cast_in, cast_out, cast_shapes = _cast_side_job(cast_weights, n_batch * n_i, lambda b, i: b * n_i + i)
    return pl.pallas_call(
        functools.partial(_slc_kernel, tq=tq, tk=tk, n_cast_w=len(cast_weights)),
        out_shape=[jax.ShapeDtypeStruct((n_batch, seq_len, NSA_W), bf16)] + cast_shapes,
        grid=(n_batch, n_i),
        in_specs=[
            pl.BlockSpec((1, NSA_HEADS, tq, HEAD_DIM), lambda b, i: (b, 0, i, 0)),
            pl.BlockSpec((1, NSA_KV_HEADS, tq, SEL_LANES), lambda b, i: (b, 0, i, 0)),
            pl.BlockSpec((1, NSA_HEADS, tq, HEAD_DIM), lambda b, i: (b, 0, jnp.minimum(i + 1, n_i - 1), 0)),
            pl.BlockSpec((1, NSA_KV_HEADS, tq, SEL_LANES), lambda b, i: (b, 0, jnp.minimum(i + 1, n_i - 1), 0)),
            pl.BlockSpec((1, NSA_KV_HEADS, seq_len, HEAD_DIM + SEL_LANES), lambda b, i: (b, 0, 0, 0)),
            pl.BlockSpec((1, NSA_KV_HEADS, nkb, HEAD_DIM, tk), lambda b, i: (b, 0, 0, 0, 0)),
            pl.BlockSpec((1, tq, LANES), lambda b, i: (b, i, 0)),
        ] + cast_in,
        out_specs=[pl.BlockSpec((1, tq, NSA_W), lambda b, i: (b, i, 0))] + cast_out,
        scratch_shapes=[pltpu.VMEM((2, tk, n_q), f32), pltpu.VMEM((2, tk, n_q), bf16)] * NSA_KV_HEADS,
        compiler_params=_params("arbitrary", "arbitrary"),
        name="slc",
    )(nqr, selneg, nqr, selneg, kaug, vst, gates, *cast_weights)


def _win_kernel(q_ref, k_ref, vt_ref, gate_ref, o_ref, *, tq, n_sub):
    i = pl.program_id(2)
    g = pl.program_id(1)
    hg = NSA_GROUP
    span = WINDOW + tq
    q0s = [(i * n_sub + u) * tq for u in range(n_sub)]
    k0s = [pl.multiple_of(jnp.maximum(q0 - WINDOW, 0), tq) for q0 in q0s]
    scores = []
    for u in range(n_sub):
        q = q_ref[0, :, u * tq:(u + 1) * tq, :].reshape(hg * tq, HEAD_DIM)
        scores.append(_dot_nt(k_ref[0, 0, pl.ds(k0s[u], span), :], q))
    probs, denoms = [], []
    for u in range(n_sub):
        kp = k0s[u] + lax.broadcasted_iota(jnp.int32, (span, tq), 0)
        t = q0s[u] + lax.broadcasted_iota(jnp.int32, (span, tq), 1)
        diff = t - kp
        bias = jnp.where((diff >= 0) & (diff < WINDOW), 0.0, -jnp.inf)
        s = jnp.concatenate([scores[u][:, h * tq:(h + 1) * tq] + bias for h in range(hg)], axis=1)
        e = jnp.exp2(s - jnp.max(s, axis=0, keepdims=True))
        denoms.append(jnp.sum(e, axis=0, keepdims=True))
        probs.append(e.astype(bf16))
    gates = gate_ref[0]
    lane = lax.broadcasted_iota(jnp.int32, (tq, LANES), 1)
    for u in range(n_sub):
        jb = k0s[u] // tq
        vt = jnp.concatenate([vt_ref[0, 0, jb + c] for c in range(span // tq)], axis=1)
        out = _dot(vt, probs[u]) / denoms[u]
        for h in range(hg):
            gcol = FOX_HEADS + (g * hg + h) * N_BRANCH + 2
            gh = jnp.sum(jnp.where(lane == gcol, gates[u * tq:(u + 1) * tq], 0.0), axis=-1, keepdims=True)
            o_ref[0, u * tq:(u + 1) * tq, h * HEAD_DIM:(h + 1) * HEAD_DIM] = (
                gh * out[:, h * tq:(h + 1) * tq].T).astype(bf16)


def _win(nqr, kw, vwt, gates):
    n_batch, _, seq_len, _ = nqr.shape
    tq, n_sub = WIN_TILE, 8
    assert seq_len >= WINDOW + tq and WINDOW % tq == 0
    return pl.pallas_call(
        functools.partial(_win_kernel, tq=tq, n_sub=n_sub),
        out_shape=jax.ShapeDtypeStruct((n_batch, seq_len, NSA_W), bf16),
        grid=(n_batch, NSA_KV_HEADS, seq_len // (tq * n_sub)),
        in_specs=[
            pl.BlockSpec((1, NSA_GROUP, tq * n_sub, HEAD_DIM), lambda b, g, i: (b, g, i, 0)),
            pl.BlockSpec((1, 1, seq_len, HEAD_DIM), lambda b, g, i: (b, g, 0, 0)),
            pl.BlockSpec((1, 1, seq_len // tq, HEAD_DIM, tq), lambda b, g, i: (b, g, 0, 0, 0)),
            pl.BlockSpec((1, tq * n_sub, LANES), lambda b, g, i: (b, i, 0)),
        ],
        out_specs=pl.BlockSpec((1, tq * n_sub, NSA_GROUP * HEAD_DIM), lambda b, g, i: (b, i, g)),
        compiler_params=_params("arbitrary", "arbitrary", "arbitrary"),
        name="win",
    )(nqr, kw, vwt, gates)


def _outproj_kernel(fox_ref, c_ref, s_ref, w_ref, x_ref, mod_ref, wo_ref, o_ref):
    mix = _dot(fox_ref[...], wo_ref[0:FOX_W, :])
    nsa = c_ref[...].astype(f32) + s_ref[...].astype(f32) + w_ref[...].astype(f32)
    mix = mix + _dot(nsa.astype(bf16), wo_ref[FOX_W:FOX_W + NSA_W, :])
    o_ref[...] = x_ref[...] + mod_ref[0][2:3] * mix


def _outproj(ofox, ocmp, oslc, owin, x2d, mod3, wo_bf16, seq_len):
    m_rows = x2d.shape[0]
    tm = 512
    per_b = seq_len // tm
    half = pl.BlockSpec((tm, FOX_W), lambda i: (i, 0))
    return pl.pallas_call(
        _outproj_kernel,
        out_shape=jax.ShapeDtypeStruct((m_rows, D_MODEL), f32),
        grid=(m_rows // tm,),
        in_specs=[
            half, half, half, half,
            pl.BlockSpec((tm, D_MODEL), lambda i: (i, 0)),
            pl.BlockSpec((1, 6, D_MODEL), lambda i: (i // per_b, 0, 0)),
            pl.BlockSpec((D_MODEL, D_MODEL), lambda i: (0, 0)),
        ],
        out_specs=pl.BlockSpec((tm, D_MODEL), lambda i: (i, 0)),
        compiler_params=_params("arbitrary"),
        name="outproj",
    )(ofox, ocmp, oslc, owin, x2d, mod3, wo_bf16)


def _mlp_kernel(x_ref, mod_ref, g_ref, wu_ref, wd_ref, o_ref, h_ref, acc_ref):
    f = pl.program_id(1)

    @pl.when(f == 0)
    def _():
        md = mod_ref[0]
        y = _rms(x_ref[...], g_ref[...])
        h_ref[...] = (y * (1.0 + md[4:5]) + md[3:4]).astype(bf16)
        acc_ref[...] = jnp.zeros_like(acc_ref)

    u = jnp.maximum(_dot(h_ref[...], wu_ref[...]), 0.0)
    acc_ref[...] += _dot((u * u).astype(bf16), wd_ref[...])

    @pl.when(f == pl.num_programs(1) - 1)
    def _():
        o_ref[...] = x_ref[...] + mod_ref[0][5:6] * acc_ref[...]


def _mlp(x2d, mod3, norm_g, wu_bf16, wd_bf16, seq_len):
    m_rows = x2d.shape[0]
    tm, tf = 512, 1024
    per_b = seq_len // tm
    return pl.pallas_call(
        _mlp_kernel,
        out_shape=jax.ShapeDtypeStruct((m_rows, D_MODEL), f32),
        grid=(m_rows // tm, D_FF // tf),
        in_specs=[
            pl.BlockSpec((tm, D_MODEL), lambda i, f: (i, 0)),
            pl.BlockSpec((1, 6, D_MODEL), lambda i, f: (i // per_b, 0, 0)),
            pl.BlockSpec((1, D_MODEL), lambda i, f: (0, 0)),
            pl.BlockSpec((D_MODEL, tf), lambda i, f: (0, f)),
            pl.BlockSpec((tf, D_MODEL), lambda i, f: (f, 0)),
        ],
        out_specs=pl.BlockSpec((tm, D_MODEL), lambda i, f: (i, 0)),
        scratch_shapes=[pltpu.VMEM((tm, D_MODEL), bf16), pltpu.VMEM((tm, D_MODEL), f32)],
        compiler_params=_params("arbitrary", "arbitrary"),
        name="mlp",
    )(x2d, mod3, norm_g, wu_bf16, wd_bf16)


def _layer(x, c, w_ada, b_ada, norm1_g, w_in, b_forget, fox_q_norm, fox_k_norm, nsa_q_norm,
           cmp_k_norm, slc_k_norm, win_k_norm, cmp_pe_k, cmp_w1_k, cmp_w2_k, cmp_pe_v, cmp_w1_v,
           cmp_w2_v, w_out, norm2_g, w_up, w_down):
    n_batch, seq_len, _ = x.shape
    n_cmp = (seq_len - CMP_LEN) // CMP_STRIDE + 1
    row = lambda v: v.reshape(1, -1)

    half = HEAD_DIM // 2
    inv_freq = ROPE_THETA ** (-jnp.arange(half, dtype=f32) / half)
    inv_freq = jnp.concatenate([inv_freq, inv_freq]).reshape(1, HEAD_DIM)

    w_a, w_b, w_s = _repack_w_in(w_in.T)

    mod3 = _ada(c, w_ada, b_ada).reshape(n_batch, 6, D_MODEL)
    x2d = x.reshape(n_batch * seq_len, D_MODEL)
    (fq, fk, fvt, cin, nqu, nqr, kc, vc, kaug, vst, kw, vw, gates) = _prep(
        x2d, mod3, row(norm1_g), w_a, w_b, w_s, n_batch, seq_len, row(fox_q_norm), row(fox_k_norm), row(nsa_q_norm), row(slc_k_norm),
        row(win_k_norm), jnp.pad(b_forget, (0, LANES - FOX_HEADS)).reshape(1, LANES), inv_freq)
    kcmp = _compress(kc, cmp_pe_k, cmp_w1_k.astype(bf16), cmp_w2_k.astype(bf16), row(cmp_k_norm))
    vcmp = _compress(vc, cmp_pe_v, cmp_w1_v.astype(bf16), cmp_w2_v.astype(bf16))
    ofox = _fox(fq, fk, fvt, cin)
    ocmp, selneg = _cmp(nqu, kcmp, vcmp, gates, n_cmp)
    oslc, w_up16, w_down16, w_out16 = _slc(nqr, selneg, kaug, vst, gates, [w_up, w_down, w_out])
    owin = _win(nqr, kw, vw, gates)
    x1 = _outproj(ofox.reshape(-1, FOX_W), ocmp.reshape(-1, NSA_W), oslc.reshape(-1, NSA_W),
                  owin.reshape(-1, NSA_W), x2d, mod3, w_out16, seq_len)
    x2 = _mlp(x1, mod3, row(norm2_g), w_up16, w_down16, seq_len)
    return x2.reshape(n_batch, seq_len, D_MODEL)


def kernel(x, c, w_ada, b_ada, norm1_g, w_in, b_forget, fox_q_norm, fox_k_norm, nsa_q_norm, cmp_k_norm,
           slc_k_norm, win_k_norm, cmp_pe_k, cmp_w1_k, cmp_w2_k, cmp_pe_v, cmp_w1_v, cmp_w2_v, w_out,
           norm2_g, w_up, w_down):
    depth = w_ada.shape[0]
    for l in range(depth):
        x = _layer(x, c, w_ada[l], b_ada[l], norm1_g[l], w_in[l], b_forget[l], fox_q_norm[l], fox_k_norm[l],
                   nsa_q_norm[l], cmp_k_norm[l], slc_k_norm[l], win_k_norm[l], cmp_pe_k[l], cmp_w1_k[l],
                   cmp_w2_k[l], cmp_pe_v[l], cmp_w1_v[l], cmp_w2_v[l], w_out[l], norm2_g[l], w_up[l],
                   w_down[l])
    return x
```

```python
import functools
import math
from typing import Any, Callable, NamedTuple

import numpy as np
import jax
import jax.numpy as jnp
from jax import lax
from jax.experimental import pallas as pl
from jax.experimental.pallas import tpu as pltpu

D_MODEL = 2048
HEAD_DIM = 128
FOX_HEADS = 8
NSA_HEADS = 8
NSA_KV_HEADS = 2
NSA_GROUP = NSA_HEADS // NSA_KV_HEADS
N_BRANCH = 3
D_FF = 4 * D_MODEL
ROPE_THETA = 10000.0
CMP_LEN = 32
CMP_STRIDE = 16
CMP_SHIFT = 4
CMP_HIDDEN = 2 * HEAD_DIM
SEL_BLOCK = 64
SEL_SHIFT = 6
SEL_TOPK = 16
WINDOW = 512
NORM_EPS = 1e-6
ATTN_SCALE = HEAD_DIM ** -0.5
FOX_W = FOX_HEADS * HEAD_DIM
NSA_W = NSA_HEADS * HEAD_DIM
KV_W = NSA_KV_HEADS * HEAD_DIM

LANES = 128
F32_SUBLANES = 8
BF16_SUBLANES = 16
SEL_LANES = LANES
MASK_BIG = 1e30
KV_BLOCK = 512
WIN_TILE = 128
LOG2E = math.log2(math.e)

COL_FQ = 0
COL_FK = COL_FQ + FOX_W
COL_FV = COL_FK + FOX_W
COL_NQ = COL_FV + FOX_W
COL_KC = COL_NQ + NSA_W
COL_VC = COL_KC + KV_W
COL_KS = COL_VC + KV_W
COL_VS = COL_KS + KV_W
COL_KW = COL_VS + KV_W
COL_VW = COL_KW + KV_W
COL_SMALL = COL_VW + KV_W
W_IN_Z0 = 3 * FOX_W
W_IN_NQ0 = W_IN_Z0 + FOX_HEADS
W_IN_GZ0 = W_IN_NQ0 + NSA_W + 6 * KV_W

VMEM_LIMIT = 56 * 1024 * 1024
VMEM_LIMIT_PREP = 60 * 1024 * 1024

f32 = jnp.float32
bf16 = jnp.bfloat16


def _params(*sem):
    return pltpu.CompilerParams(dimension_semantics=sem, vmem_limit_bytes=VMEM_LIMIT)


def _dot_nt(a, b):
    return lax.dot_general(a, b, (((1,), (1,)), ((), ())), preferred_element_type=f32)


def _dot(a, b):
    return jnp.dot(a, b, preferred_element_type=f32)


def _split3(x):
    hi = x.astype(bf16)
    r1 = x - hi.astype(f32)
    mid = r1.astype(bf16)
    lo = (r1 - mid.astype(f32)).astype(bf16)
    return hi, mid, lo


def _dot_f32_by_exact(x, w_bf16):
    hi, mid, lo = _split3(x)
    return _dot(hi, w_bf16) + (_dot(mid, w_bf16) + _dot(lo, w_bf16))


def _rms(x, gain):
    ms = jnp.mean(x * x, axis=-1, keepdims=True)
    return x * lax.rsqrt(ms + NORM_EPS) * gain


def _ada_kernel(ct_ref, w_ref, b_ref, o_ref, *, n_batch, k_chunk):
    ct = ct_ref[...]
    act = ct * jax.nn.sigmoid(ct)
    rows = []
    for b in range(n_batch):
        col = act[:, b:b + 1]
        acc = b_ref[...]
        for k0 in range(0, D_MODEL, k_chunk):
            acc = acc + jnp.sum(w_ref[k0:k0 + k_chunk, :] * col[k0:k0 + k_chunk], axis=0, keepdims=True)
        rows.append(acc)
    o_ref[...] = jnp.concatenate(rows, axis=0)


def _ada(c, w_ada, b_ada):
    n_batch = c.shape[0]
    n_out = w_ada.shape[1]
    tn = 1024
    return pl.pallas_call(
        functools.partial(_ada_kernel, n_batch=n_batch, k_chunk=256),
        out_shape=jax.ShapeDtypeStruct((n_batch, n_out), f32),
        grid=(n_out // tn,),
        in_specs=[
            pl.BlockSpec((D_MODEL, n_batch), lambda j: (0, 0)),
            pl.BlockSpec((D_MODEL, tn), lambda j: (0, j)),
            pl.BlockSpec((1, tn), lambda j: (0, j)),
        ],
        out_specs=pl.BlockSpec((n_batch, tn), lambda j: (0, j)),
        compiler_params=_params("arbitrary"),
        name="ada",
    )(c.T, w_ada, b_ada.reshape(1, n_out))


def _repack_kernel(a_ref, *rest, n_part):
    part_refs, (z_ref, gz_ref, wa_ref, wb_ref, ws_ref) = rest[:n_part], rest[n_part:]
    wa_ref[...] = a_ref[...].astype(bf16)
    wb_ref[...] = jnp.concatenate([r[...] for r in part_refs], axis=0).astype(bf16)
    pad = jnp.zeros((LANES - z_ref.shape[0] - gz_ref.shape[0], D_MODEL), f32)
    ws_ref[...] = jnp.concatenate([z_ref[...], gz_ref[...], pad], axis=0).astype(bf16)


def _repack_w_in(w_t):
    n_steps = 16
    ra, rb = W_IN_Z0 // n_steps, (W_IN_GZ0 - W_IN_NQ0) // n_steps
    hb = math.gcd(W_IN_NQ0, rb)
    n_part = rb // hb
    n_z, n_gz = W_IN_NQ0 - W_IN_Z0, w_t.shape[0] - W_IN_GZ0
    assert W_IN_Z0 % n_steps == 0 and (W_IN_GZ0 - W_IN_NQ0) % n_steps == 0 and ra % BF16_SUBLANES == 0 and rb % BF16_SUBLANES == 0
    assert hb % F32_SUBLANES == 0 and W_IN_Z0 % n_z == 0 and W_IN_GZ0 % n_gz == 0
    b_off = W_IN_NQ0 // hb
    part = lambda k: pl.BlockSpec((hb, D_MODEL), lambda j: (b_off + n_part * j + k, 0))
    return pl.pallas_call(
        functools.partial(_repack_kernel, n_part=n_part),
        out_shape=[jax.ShapeDtypeStruct((W_IN_Z0, D_MODEL), bf16),
                   jax.ShapeDtypeStruct((W_IN_GZ0 - W_IN_NQ0, D_MODEL), bf16),
                   jax.ShapeDtypeStruct((LANES, D_MODEL), bf16)],
        grid=(n_steps,),
        in_specs=[pl.BlockSpec((ra, D_MODEL), lambda j: (j, 0))] + [part(k) for k in range(n_part)] + [
            pl.BlockSpec((n_z, D_MODEL), lambda j: (W_IN_Z0 // n_z, 0)),
            pl.BlockSpec((n_gz, D_MODEL), lambda j: (W_IN_GZ0 // n_gz, 0)),
        ],
        out_specs=[pl.BlockSpec((ra, D_MODEL), lambda j: (j, 0)),
                   pl.BlockSpec((rb, D_MODEL), lambda j: (j, 0)),
                   pl.BlockSpec((LANES, D_MODEL), lambda j: (0, 0))],
        compiler_params=_params("arbitrary"),
        name="repack",
    )(*([w_t] * (n_part + 3)))


def _cast_side_job(weights, n_steps, step_of):
    n_cast = max(c for c in range(1, n_steps + 1) if all(w.shape[0] % (BF16_SUBLANES * c) == 0 for w in weights))
    idx = lambda *ids: (jnp.minimum(step_of(*ids), n_cast - 1), 0)
    specs = [pl.BlockSpec((w.shape[0] // n_cast, w.shape[1]), idx) for w in weights]
    return specs, specs, [jax.ShapeDtypeStruct(w.shape, bf16) for w in weights]


def _prep_kernel(x_ref, mod_ref, g1_ref, wa_ref, wb_ref, ws_ref,
                 gq_ref, gk_ref, gn_ref, gs_ref, gw_ref, bf_ref, inv_ref,
                 fq_ref, fk_ref, fvt_ref, cin_ref, nqu_ref, nqr_ref, kc_ref, vc_ref,
                 ks_ref, vst_ref, kw_ref, vw_ref, gate_ref, carry_ref, rot_ref, *, tm):
    i = pl.program_id(1)

    @pl.when(i == 0)
    def _():
        carry_ref[...] = jnp.zeros_like(carry_ref)

    md = mod_ref[0]
    hn = (_rms(x_ref[...], g1_ref[...]) * (1.0 + md[1:2]) + md[0:1]).astype(bf16)
    groups = {}

    def head(col, h):
        base, w_ref, r0 = next((b, w, r) for b, w, r in (
            (COL_KC, wb_ref, NSA_W), (COL_NQ, wb_ref, 0), (COL_FV, wa_ref, 2 * FOX_W),
            (COL_FK, wa_ref, FOX_W), (COL_FQ, wa_ref, 0)) if col >= b)
        if base not in groups:
            n = (COL_SMALL - COL_KC) if base == COL_KC else FOX_W
            groups[base] = _dot_nt(hn, w_ref[r0:r0 + n, :])
        c0 = col - base + h * HEAD_DIM
        return groups[base][:, c0:c0 + HEAD_DIM]

    row = lax.broadcasted_iota(jnp.int32, (tm, LANES), 0)
    lane = lax.broadcasted_iota(jnp.int32, (tm, LANES), 1)
    pos = i * tm + row

    @pl.when((pl.program_id(0) == 0) & (i == 0))
    def _():
        ang_row = row.astype(f32) * inv_ref[...]
        rot_ref[0] = jnp.cos(ang_row)
        rot_ref[1] = jnp.sin(ang_row)

    ang0 = (i * tm).astype(f32) * inv_ref[...]
    cos0, sin0 = jnp.cos(ang0), jnp.sin(ang0)
    cos = cos0 * rot_ref[0] - sin0 * rot_ref[1]
    sin = sin0 * rot_ref[0] + cos0 * rot_ref[1]
    sin_signed = jnp.where(lane < HEAD_DIM // 2, -sin, sin)

    def rope(x):
        return x * cos + pltpu.roll(x, HEAD_DIM // 2, 1) * sin_signed

    small = _dot_nt(hn, ws_ref[...])
    z = small + bf_ref[...]
    logf = jnp.minimum(z, 0.0) - jnp.log1p(jnp.exp(-jnp.abs(z)))
    t_idx = lax.broadcasted_iota(jnp.int32, (tm, tm), 0)
    s_idx = lax.broadcasted_iota(jnp.int32, (tm, tm), 1)
    tri = jnp.where(s_idx <= t_idx, 1.0, 0.0).astype(bf16)
    hi, mid, lo = _split3(logf)
    local = _dot(tri, hi) + (_dot(tri, mid) + _dot(tri, lo))
    cin_ref[0, 0] = carry_ref[0:1, :]
    carry_ref[...] = carry_ref[...] + local[tm - 1:tm, :]
    b_hi, b_mid, b_lo = (v.astype(f32) for v in _split3(local * (-LOG2E)))
    ones3 = jnp.where(lane < 3, 1.0, 0.0).astype(bf16)

    for h in range(FOX_HEADS):
        q = _rms(head(COL_FQ, h), gq_ref[...] * (ATTN_SCALE * LOG2E))
        fq_ref[0, h] = jnp.concatenate([q.astype(bf16), ones3], axis=1)
    for h in range(FOX_HEADS):
        k = _rms(head(COL_FK, h), gk_ref[...])
        bias = jnp.where(lane == 0, b_hi[:, h:h + 1],
                         jnp.where(lane == 1, b_mid[:, h:h + 1],
                                   jnp.where(lane == 2, b_lo[:, h:h + 1], 0.0)))
        fk_ref[0, h] = jnp.concatenate([k.astype(bf16), bias.astype(bf16)], axis=1)

    gate_ref[0] = jax.nn.sigmoid(small)

    for h in range(NSA_HEADS):
        qn = _rms(head(COL_NQ, h), gn_ref[...] * (ATTN_SCALE * LOG2E))
        nqu_ref[0, h] = qn.astype(bf16)
        nqr_ref[0, h] = rope(qn).astype(bf16)
    onehot = jnp.where(lane == lax.shift_right_logical(pos, SEL_SHIFT), 1.0, 0.0).astype(bf16)
    for g in range(NSA_KV_HEADS):
        kc_ref[0, g] = head(COL_KC, g)
        vc_ref[0, g] = head(COL_VC, g)
        ks = rope(_rms(head(COL_KS, g), gs_ref[...])).astype(bf16)
        ks_ref[0, g] = jnp.concatenate([ks, onehot], axis=1)
        vst_ref[0, g, 0] = head(COL_VS, g).T.astype(bf16)
        kw_ref[0, g] = rope(_rms(head(COL_KW, g), gw_ref[...])).astype(bf16)
        vwt = head(COL_VW, g).T.astype(bf16)
        for c in range(tm // WIN_TILE):
            vw_ref[0, g, c] = vwt[:, c * WIN_TILE:(c + 1) * WIN_TILE]
    for h in range(FOX_HEADS):
        fvt_ref[0, h, 0] = head(COL_FV, h).T.astype(bf16)


def _prep(x2d, mod3, norm_g, wa, wb, ws, n_batch, seq_len, gq, gk, gn, gs, gw, b_forget_row, inv_freq):
    resident = lambda w: pl.BlockSpec(w.shape, lambda b, i: (0, 0), pipeline_mode=pl.Buffered(1))
    tm = KV_BLOCK
    per_b = seq_len // tm
    hshape = lambda n, w, dt: jax.ShapeDtypeStruct((n_batch, n, seq_len, w), dt)
    hspec = lambda n, w: pl.BlockSpec((1, n, tm, w), lambda b, i: (b, 0, i, 0))
    tshape = lambda n: jax.ShapeDtypeStruct((n_batch, n, per_b, HEAD_DIM, tm), bf16)
    tspec = lambda n: pl.BlockSpec((1, n, 1, HEAD_DIM, tm), lambda b, i: (b, 0, i, 0, 0))
    vec = pl.BlockSpec((1, LANES), lambda b, i: (0, 0))
    return pl.pallas_call(
        functools.partial(_prep_kernel, tm=tm),
        out_shape=[
            hshape(FOX_HEADS, 2 * HEAD_DIM, bf16), hshape(FOX_HEADS, 2 * HEAD_DIM, bf16), tshape(FOX_HEADS),
            jax.ShapeDtypeStruct((n_batch, per_b, 1, LANES), f32),
            hshape(NSA_HEADS, HEAD_DIM, bf16), hshape(NSA_HEADS, HEAD_DIM, bf16),
            hshape(NSA_KV_HEADS, HEAD_DIM, f32), hshape(NSA_KV_HEADS, HEAD_DIM, f32),
            hshape(NSA_KV_HEADS, HEAD_DIM + SEL_LANES, bf16), tshape(NSA_KV_HEADS),
            hshape(NSA_KV_HEADS, HEAD_DIM, bf16),
            jax.ShapeDtypeStruct((n_batch, NSA_KV_HEADS, seq_len // WIN_TILE, HEAD_DIM, WIN_TILE), bf16),
            jax.ShapeDtypeStruct((n_batch, seq_len, LANES), f32),
        ],
        grid=(n_batch, per_b),
        in_specs=[
            pl.BlockSpec((tm, D_MODEL), lambda b, i: (b * per_b + i, 0)),
            pl.BlockSpec((1, 6, D_MODEL), lambda b, i: (b, 0, 0)),
            pl.BlockSpec((1, D_MODEL), lambda b, i: (0, 0)),
            resident(wa), resident(wb), resident(ws),
            vec, vec, vec, vec, vec, vec, vec,
        ],
        out_specs=[
            hspec(FOX_HEADS, 2 * HEAD_DIM), hspec(FOX_HEADS, 2 * HEAD_DIM), tspec(FOX_HEADS),
            pl.BlockSpec((1, 1, 1, LANES), lambda b, i: (b, i, 0, 0)),
            hspec(NSA_HEADS, HEAD_DIM), hspec(NSA_HEADS, HEAD_DIM),
            hspec(NSA_KV_HEADS, HEAD_DIM), hspec(NSA_KV_HEADS, HEAD_DIM),
            hspec(NSA_KV_HEADS, HEAD_DIM + SEL_LANES), tspec(NSA_KV_HEADS),
            hspec(NSA_KV_HEADS, HEAD_DIM),
            pl.BlockSpec((1, NSA_KV_HEADS, tm // WIN_TILE, HEAD_DIM, WIN_TILE), lambda b, i: (b, 0, i, 0, 0)),
            pl.BlockSpec((1, tm, LANES), lambda b, i: (b, i, 0)),
        ],
        scratch_shapes=[pltpu.VMEM((8, LANES), f32), pltpu.VMEM((2, tm, LANES), f32)],
        compiler_params=pltpu.CompilerParams(dimension_semantics=("arbitrary", "arbitrary"),
                                             vmem_limit_bytes=VMEM_LIMIT_PREP),
        name="prep",
    )(x2d, mod3, norm_g, wa, wb, ws, gq, gk, gn, gs, gw, b_forget_row, inv_freq)


def _compress_kernel(x_ref, pe_ref, w1_ref, w2_ref, *rest, n_rows, do_norm):
    o_ref = rest[-1]
    a = jnp.zeros((n_rows, CMP_HIDDEN), f32)
    b = jnp.zeros((n_rows, CMP_HIDDEN), f32)
    for l in range(CMP_STRIDE):
        x = x_ref[0, 0, pl.ds(l, n_rows, stride=CMP_STRIDE), :]
        a = a + _dot((x + pe_ref[l:l + 1, :]).astype(bf16), w1_ref[l * HEAD_DIM:(l + 1) * HEAD_DIM, :])
        lb = CMP_STRIDE + l
        b = b + _dot((x + pe_ref[lb:lb + 1, :]).astype(bf16), w1_ref[lb * HEAD_DIM:(lb + 1) * HEAD_DIM, :])
    pre = a + pltpu.roll(b, n_rows - 1, 0)
    hid = pre * jax.nn.sigmoid(pre)
    out = _dot(hid.astype(bf16), w2_ref[...])
    if do_norm:
        out = _rms(out, rest[0][...])
    o_ref[0, 0] = out.astype(bf16)


def _compress(x4, pe, w1_bf16, w2_bf16, gain=None):
    n_batch, n_g, seq_len, _ = x4.shape
    n_rows = seq_len // CMP_STRIDE
    do_norm = gain is not None
    return pl.pallas_call(
        functools.partial(_compress_kernel, n_rows=n_rows, do_norm=do_norm),
        out_shape=jax.ShapeDtypeStruct((n_batch, n_g, n_rows, HEAD_DIM), bf16),
        grid=(n_batch, n_g),
        in_specs=[
            pl.BlockSpec((1, 1, seq_len, HEAD_DIM), lambda b, g: (b, g, 0, 0)),
            pl.BlockSpec((CMP_LEN, HEAD_DIM), lambda b, g: (0, 0)),
            pl.BlockSpec((CMP_LEN * HEAD_DIM, CMP_HIDDEN), lambda b, g: (0, 0)),
            pl.BlockSpec((CMP_HIDDEN, HEAD_DIM), lambda b, g: (0, 0)),
        ] + ([pl.BlockSpec((1, HEAD_DIM), lambda b, g: (0, 0))] if do_norm else []),
        out_specs=pl.BlockSpec((1, 1, n_rows, HEAD_DIM), lambda b, g: (b, g, 0, 0)),
        compiler_params=_params("arbitrary", "arbitrary"),
        name="compress",
    )(x4, pe, w1_bf16, w2_bf16, *([gain] if do_norm else []))


def _softmax_block(s, d, m_prev, l_prev):
    m_blk = jnp.max(s, axis=0, keepdims=True)
    m_new = jnp.maximum(m_prev, m_blk if d is None else m_blk + d)
    p = jnp.exp2(s - (m_new if d is None else m_new - d))
    alpha = jnp.exp2(m_prev - m_new)
    l_new = alpha * l_prev + jnp.sum(p, axis=0, keepdims=True)
    return m_new, l_new, alpha, p.astype(bf16)


class _Chain(NamedTuple):
    scores: Callable
    values: Callable
    offset: Callable
    diag_mask: Callable
    emit: Callable
    next_first: Callable
    s_buf: Any
    p_buf: Any


def _flash_sweep(n_full, chains, n_q, is_first):
    @pl.when(is_first)
    def _():
        for c in chains:
            c.s_buf[0] = c.scores(0)

    for c in chains:
        c.p_buf[1] = jnp.zeros(c.p_buf.shape[1:], c.p_buf.dtype)

    def accumulate(c, j, slot, alpha, acc):
        return alpha * acc + _dot(c.values(jnp.maximum(j, 0)), c.p_buf[slot])

    def step(j, cur, states):
        out = []
        for c, (m, l, acc, alpha_prev) in zip(chains, states):
            acc = accumulate(c, j - 1, 1 - cur, alpha_prev, acc)
            m, l, alpha, p = _softmax_block(c.s_buf[cur], c.offset(j), m, l)
            c.p_buf[cur] = p
            c.s_buf[1 - cur] = c.scores(j + 1)
            out.append((m, l, acc, alpha))
        return tuple(out)

    def finish(cur, states):
        nxt = [c.next_first() for c in chains]
        for c, s_next, (m, l, acc, alpha_prev) in zip(chains, nxt, states):
            if cur == 1:
                c.s_buf[0] = s_next
            acc = accumulate(c, n_full - 1, 1 - cur, alpha_prev, acc)
            m, l, alpha, p = _softmax_block(c.diag_mask(c.s_buf[cur]), None, m, l)
            if cur == 0:
                c.s_buf[0] = s_next
            acc = alpha * acc + _dot(c.values(n_full), p)
            c.emit(acc / l)

    init = tuple((jnp.full((1, n_q), -jnp.inf, f32), jnp.zeros((1, n_q), f32),
                  jnp.zeros((HEAD_DIM, n_q), f32), jnp.ones((1, n_q), f32)) for _ in chains)
    states = lax.fori_loop(0, n_full // 2, lambda jj, st: step(2 * jj + 1, 1, step(2 * jj, 0, st)), init)

    @pl.when(n_full % 2 == 0)
    def _():
        finish(0, states)

    @pl.when(n_full % 2 == 1)
    def _():
        finish(1, step(n_full - 1, 0, states))


def _fox_kernel(q_ref, qn_ref, k_ref, vt_ref, c_ref, o_ref, *bufs, tq, n_heads):
    hb = pl.program_id(1)
    i = pl.program_id(2)
    lane = lax.broadcasted_iota(jnp.int32, (1, LANES), 1)

    def diag_mask(s):
        kk = lax.broadcasted_iota(jnp.int32, (tq, tq), 0)
        qq = lax.broadcasted_iota(jnp.int32, (tq, tq), 1)
        return jnp.where(kk <= qq, s, -jnp.inf)

    def chain(u):
        qa = q_ref[0, u]

        def cin(j):
            return jnp.sum(jnp.where(lane == hb * n_heads + u, c_ref[0, j], 0.0), axis=-1, keepdims=True)

        ci = cin(i)

        def scores(j):
            return _dot_nt(k_ref[0, u, pl.ds(pl.multiple_of(j * tq, tq), tq), :], qa)

        def emit(out_t):
            o_ref[0, :, u * HEAD_DIM:(u + 1) * HEAD_DIM] = out_t.T.astype(bf16)

        def next_first():
            return _dot_nt(k_ref[0, u, 0:tq, :], qn_ref[0, u])

        return _Chain(scores, lambda j: vt_ref[0, u, j], lambda j: (ci - cin(j)) * LOG2E, diag_mask, emit,
                      next_first, bufs[2 * u], bufs[2 * u + 1])

    _flash_sweep(i, [chain(u) for u in range(n_heads)], tq, i == 0)


def _fox(fq, fk, fvt, cin):
    n_batch, n_h, seq_len, _ = fq.shape
    tq = KV_BLOCK
    nkb = seq_len // tq
    n_heads = 4
    return pl.pallas_call(
        functools.partial(_fox_kernel, tq=tq, n_heads=n_heads),
        out_shape=jax.ShapeDtypeStruct((n_batch, seq_len, n_h * HEAD_DIM), bf16),
        grid=(n_batch, n_h // n_heads, nkb),
        in_specs=[
            pl.BlockSpec((1, n_heads, tq, 2 * HEAD_DIM), lambda b, h, i: (b, h, i, 0)),
            pl.BlockSpec((1, n_heads, tq, 2 * HEAD_DIM), lambda b, h, i: (b, h, jnp.minimum(i + 1, nkb - 1), 0)),
            pl.BlockSpec((1, n_heads, seq_len, 2 * HEAD_DIM), lambda b, h, i: (b, h, 0, 0)),
            pl.BlockSpec((1, n_heads, nkb, HEAD_DIM, tq), lambda b, h, i: (b, h, 0, 0, 0)),
            pl.BlockSpec((1, nkb, 1, LANES), lambda b, h, i: (b, 0, 0, 0)),
        ],
        out_specs=pl.BlockSpec((1, tq, n_heads * HEAD_DIM), lambda b, h, i: (b, i, h)),
        scratch_shapes=[pltpu.VMEM((2, tq, tq), f32), pltpu.VMEM((2, tq, tq), bf16)] * n_heads,
        compiler_params=_params("arbitrary", "arbitrary", "arbitrary"),
        name="fox",
    )(fq, fq, fk, fvt, cin)


def _cmp_kernel(q_ref, k_ref, v_ref, gate_ref, ov_ref, o_ref, sel_ref, *, tq, n_rows, n_cmp, n_sel):
    i = pl.program_id(2)
    g = pl.program_id(1)
    hg = NSA_GROUP
    q = q_ref[0].reshape(hg * tq, HEAD_DIM)
    s = _dot_nt(q, k_ref[0, 0]).reshape(hg, tq, n_rows)
    t = i * tq + lax.broadcasted_iota(jnp.int32, (1, tq, 1), 1)
    n = lax.broadcasted_iota(jnp.int32, (1, 1, n_rows), 2)
    last = jnp.minimum(lax.shift_right_arithmetic(t - (CMP_LEN - 1), CMP_SHIFT), n_cmp - 1)
    s = jnp.where(n <= last, s, -jnp.inf)
    m = jnp.max(s, axis=-1, keepdims=True)
    m = jnp.where(m == -jnp.inf, 0.0, m)
    e = jnp.exp2(s - m)
    p = e * (1.0 / jnp.maximum(jnp.sum(e, axis=-1, keepdims=True), 1e-30))
    o = _dot(p.reshape(hg * tq, n_rows).astype(bf16), v_ref[0, 0])
    _write_gated(o_ref, gate_ref, o, g, 0, tq)

    lane = lax.broadcasted_iota(jnp.int32, (tq, LANES), 1)
    tq_pos = i * tq + lax.broadcasted_iota(jnp.int32, (tq, 1), 0)
    q_blk = lax.shift_right_logical(tq_pos, SEL_SHIFT)
    causal = lane <= q_blk
    n_top = min(SEL_TOPK, n_sel)

    @pl.when((i + 1) * tq <= n_top * SEL_BLOCK)
    def _():
        sel_ref[0, 0] = jnp.where(causal, 0.0, -MASK_BIG).astype(bf16)

    @pl.when((i + 1) * tq > n_top * SEL_BLOCK)
    def _():
        psum = p[0] + p[1] + p[2] + p[3]
        imp = _dot_f32_by_exact(psum, ov_ref[...])
        forced = (lane == 0) | (lane == q_blk) | (lane == q_blk - 1)
        key = jnp.where(forced, -2.0, jnp.where(causal, imp, -1.0))
        key = jnp.where(lane < n_sel, key, -3.0)
        lane_f = lane.astype(f32)
        sel = forced
        for _ in range(n_top - 3):
            mx = jnp.max(key, axis=-1, keepdims=True)
            first = jnp.min(jnp.where(key == mx, lane_f, float(LANES)), axis=-1, keepdims=True)
            pick = lane_f == first
            sel = sel | pick
            key = jnp.where(pick, -2.0, key)
        sel = ((q_blk < n_top) | sel) & causal
        sel_ref[0, 0] = jnp.where(sel, 0.0, -MASK_BIG).astype(bf16)


def _cmp(nqu, kcmp, vcmp, gates, n_cmp):
    n_batch, _, seq_len, _ = nqu.shape
    n_rows = kcmp.shape[2]
    n_sel = seq_len // SEL_BLOCK
    assert 3 <= n_sel <= SEL_LANES
    tq = 1024
    c0 = np.arange(n_rows) * CMP_STRIDE
    s0 = np.arange(SEL_LANES) * SEL_BLOCK
    overlap = np.clip(np.minimum(c0[:, None] + CMP_LEN, s0[None, :] + SEL_BLOCK)
                      - np.maximum(c0[:, None], s0[None, :]), 0, None).astype(np.float32) / CMP_LEN
    overlap[n_cmp:, :] = 0.0
    overlap[:, n_sel:] = 0.0
    return pl.pallas_call(
        functools.partial(_cmp_kernel, tq=tq, n_rows=n_rows, n_cmp=n_cmp, n_sel=n_sel),
        out_shape=[
            jax.ShapeDtypeStruct((n_batch, seq_len, NSA_W), bf16),
            jax.ShapeDtypeStruct((n_batch, NSA_KV_HEADS, seq_len, SEL_LANES), bf16),
        ],
        grid=(n_batch, NSA_KV_HEADS, seq_len // tq),
        in_specs=[
            pl.BlockSpec((1, NSA_GROUP, tq, HEAD_DIM), lambda b, g, i: (b, g, i, 0)),
            pl.BlockSpec((1, 1, n_rows, HEAD_DIM), lambda b, g, i: (b, g, 0, 0)),
            pl.BlockSpec((1, 1, n_rows, HEAD_DIM), lambda b, g, i: (b, g, 0, 0)),
            pl.BlockSpec((1, tq, LANES), lambda b, g, i: (b, i, 0)),
            pl.BlockSpec((n_rows, SEL_LANES), lambda b, g, i: (0, 0)),
        ],
        out_specs=[
            pl.BlockSpec((1, tq, NSA_GROUP * HEAD_DIM), lambda b, g, i: (b, i, g)),
            pl.BlockSpec((1, 1, tq, SEL_LANES), lambda b, g, i: (b, g, i, 0)),
        ],
        compiler_params=_params("arbitrary", "arbitrary", "arbitrary"),
        name="cmp",
    )(nqu, kcmp, vcmp, gates, jnp.asarray(overlap, dtype=bf16))


def _write_gated(o_ref, gate_ref, out, g, branch, tq):
    gates = gate_ref[0]
    lane = lax.broadcasted_iota(jnp.int32, (tq, LANES), 1)
    for h in range(NSA_GROUP):
        gcol = FOX_HEADS + (g * NSA_GROUP + h) * N_BRANCH + branch
        gh = jnp.sum(jnp.where(lane == gcol, gates, 0.0), axis=-1, keepdims=True)
        o_ref[0, :, h * HEAD_DIM:(h + 1) * HEAD_DIM] = (gh * out[h * tq:(h + 1) * tq]).astype(bf16)


def _slc_kernel(q_ref, sel_ref, qn_ref, seln_ref, k_ref, vt_ref, gate_ref, *rest, tq, tk, n_cast_w):
    o_ref = rest[n_cast_w]
    bufs = rest[2 * n_cast_w + 1:]
    for w32_ref, w16_ref in zip(rest[:n_cast_w], rest[n_cast_w + 1:2 * n_cast_w + 1]):
        w16_ref[...] = w32_ref[...].astype(bf16)
    i = pl.program_id(1)
    hg = NSA_GROUP
    n_q = hg * tq
    q0 = i * tq
    jd = q0 // tk

    def diag_mask(s):
        kp = jd * tk + lax.broadcasted_iota(jnp.int32, (tk, n_q), 0)
        t = q0 + (lax.broadcasted_iota(jnp.int32, (tk, n_q), 1) & (tq - 1))
        return jnp.where(kp <= t, s, -jnp.inf)

    def chain(g):
        def q_aug(qr, sr):
            q = qr[0, g * hg:(g + 1) * hg].reshape(n_q, HEAD_DIM)
            return jnp.concatenate([q, jnp.concatenate([sr[0, g]] * hg, axis=0)], axis=1)

        qa = q_aug(q_ref, sel_ref)

        def scores(j):
            return _dot_nt(k_ref[0, g, pl.ds(pl.multiple_of(j * tk, tk), tk), :], qa)

        def next_first():
            return _dot_nt(k_ref[0, g, 0:tk, :], q_aug(qn_ref, seln_ref))

        def emit(out):
            gates = gate_ref[0]
            lane = lax.broadcasted_iota(jnp.int32, (tq, LANES), 1)
            for h in range(hg):
                head = g * hg + h
                gh = jnp.sum(jnp.where(lane == FOX_HEADS + head * N_BRANCH + 1, gates, 0.0),
                             axis=-1, keepdims=True)
                o_ref[0, :, head * HEAD_DIM:(head + 1) * HEAD_DIM] = (
                    gh * out[:, h * tq:(h + 1) * tq].T).astype(bf16)

        return _Chain(scores, lambda j: vt_ref[0, g, j], lambda j: None, diag_mask, emit, next_first,
                      bufs[2 * g], bufs[2 * g + 1])

    _flash_sweep(jd, [chain(g) for g in range(NSA_KV_HEADS)], n_q, i == 0)


def _slc(nqr, selneg, kaug, vst, gates, cast_weights):
    n_batch, _, seq_len, _ = nqr.shape
    tq, tk = 256, KV_BLOCK
    nkb = seq_len // tk
    assert tq & (tq - 1) == 0
    n_q = NSA_GROUP * tq
    n_i = seq_len // tq
    cast_in, cast_out, cast_shapes = _cast_side_job(cast_weights, n_batch * n_i, lambda b, i: b * n_i + i)
    return pl.pallas_call(
        functools.partial(_slc_kernel, tq=tq, tk=tk, n_cast_w=len(cast_weights)),
        out_shape=[jax.ShapeDtypeStruct((n_batch, seq_len, NSA_W), bf16)] + cast_shapes,
        grid=(n_batch, n_i),
        in_specs=[
            pl.BlockSpec((1, NSA_HEADS, tq, HEAD_DIM), lambda b, i: (b, 0, i, 0)),
            pl.BlockSpec((1, NSA_KV_HEADS, tq, SEL_LANES), lambda b, i: (b, 0, i, 0)),
            pl.BlockSpec((1, NSA_HEADS, tq, HEAD_DIM), lambda b, i: (b, 0, jnp.minimum(i + 1, n_i - 1), 0)),
            pl.BlockSpec((1, NSA_KV_HEADS, tq, SEL_LANES), lambda b, i: (b, 0, jnp.minimum(i + 1, n_i - 1), 0)),
            pl.BlockSpec((1, NSA_KV_HEADS, seq_len, HEAD_DIM + SEL_LANES), lambda b, i: (b, 0, 0, 0)),
            pl.BlockSpec((1, NSA_KV_HEADS, nkb, HEAD_DIM, tk), lambda b, i: (b, 0, 0, 0, 0)),
            pl.BlockSpec((1, tq, LANES), lambda b, i: (b, i, 0)),
        ] + cast_in,
        out_specs=[pl.BlockSpec((1, tq, NSA_W), lambda b, i: (b, i, 0))] + cast_out,
        scratch_shapes=[pltpu.VMEM((2, tk, n_q), f32), pltpu.VMEM((2, tk, n_q), bf16)] * NSA_KV_HEADS,
        compiler_params=_params("arbitrary", "arbitrary"),
        name="slc",
    )(nqr, selneg, nqr, selneg, kaug, vst, gates, *cast_weights)


def _win_kernel(q_ref, k_ref, vt_ref, gate_ref, o_ref, *, tq, n_sub):
    i = pl.program_id(2)
    g = pl.program_id(1)
    hg = NSA_GROUP
    span = WINDOW + tq
    q0s = [(i * n_sub + u) * tq for u in range(n_sub)]
    k0s = [pl.multiple_of(jnp.maximum(q0 - WINDOW, 0), tq) for q0 in q0s]
    scores = []
    for u in range(n_sub):
        q = q_ref[0, :, u * tq:(u + 1) * tq, :].reshape(hg * tq, HEAD_DIM)
        scores.append(_dot_nt(k_ref[0, 0, pl.ds(k0s[u], span), :], q))
    probs, denoms = [], []
    for u in range(n_sub):
        kp = k0s[u] + lax.broadcasted_iota(jnp.int32, (span, tq), 0)
        t = q0s[u] + lax.broadcasted_iota(jnp.int32, (span, tq), 1)
        diff = t - kp
        bias = jnp.where((diff >= 0) & (diff < WINDOW), 0.0, -jnp.inf)
        s = jnp.concatenate([scores[u][:, h * tq:(h + 1) * tq] + bias for h in range(hg)], axis=1)
        e = jnp.exp2(s - jnp.max(s, axis=0, keepdims=True))
        denoms.append(jnp.sum(e, axis=0, keepdims=True))
        probs.append(e.astype(bf16))
    gates = gate_ref[0]
    lane = lax.broadcasted_iota(jnp.int32, (tq, LANES), 1)
    for u in range(n_sub):
        jb = k0s[u] // tq
        vt = jnp.concatenate([vt_ref[0, 0, jb + c] for c in range(span // tq)], axis=1)
        out = _dot(vt, probs[u]) / denoms[u]
        for h in range(hg):
            gcol = FOX_HEADS + (g * hg + h) * N_BRANCH + 2
            gh = jnp.sum(jnp.where(lane == gcol, gates[u * tq:(u + 1) * tq], 0.0), axis=-1, keepdims=True)
            o_ref[0, u * tq:(u + 1) * tq, h * HEAD_DIM:(h + 1) * HEAD_DIM] = (
                gh * out[:, h * tq:(h + 1) * tq].T).astype(bf16)


def _win(nqr, kw, vwt, gates):
    n_batch, _, seq_len, _ = nqr.shape
    tq, n_sub = WIN_TILE, 8
    assert seq_len >= WINDOW + tq and WINDOW % tq == 0
    return pl.pallas_call(
        functools.partial(_win_kernel, tq=tq, n_sub=n_sub),
        out_shape=jax.ShapeDtypeStruct((n_batch, seq_len, NSA_W), bf16),
        grid=(n_batch, NSA_KV_HEADS, seq_len // (tq * n_sub)),
        in_specs=[
            pl.BlockSpec((1, NSA_GROUP, tq * n_sub, HEAD_DIM), lambda b, g, i: (b, g, i, 0)),
            pl.BlockSpec((1, 1, seq_len, HEAD_DIM), lambda b, g, i: (b, g, 0, 0)),
            pl.BlockSpec((1, 1, seq_len // tq, HEAD_DIM, tq), lambda b, g, i: (b, g, 0, 0, 0)),
            pl.BlockSpec((1, tq * n_sub, LANES), lambda b, g, i: (b, i, 0)),
        ],
        out_specs=pl.BlockSpec((1, tq * n_sub, NSA_GROUP * HEAD_DIM), lambda b, g, i: (b, i, g)),
        compiler_params=_params("arbitrary", "arbitrary", "arbitrary"),
        name="win",
    )(nqr, kw, vwt, gates)


def _outproj_kernel(fox_ref, c_ref, s_ref, w_ref, x_ref, mod_ref, wo_ref, o_ref):
    mix = _dot(fox_ref[...], wo_ref[0:FOX_W, :])
    nsa = c_ref[...].astype(f32) + s_ref[...].astype(f32) + w_ref[...].astype(f32)
    mix = mix + _dot(nsa.astype(bf16), wo_ref[FOX_W:FOX_W + NSA_W, :])
    o_ref[...] = x_ref[...] + mod_ref[0][2:3] * mix


def _outproj(ofox, ocmp, oslc, owin, x2d, mod3, wo_bf16, seq_len):
    m_rows = x2d.shape[0]
    tm = 512
    per_b = seq_len // tm
    half = pl.BlockSpec((tm, FOX_W), lambda i: (i, 0))
    return pl.pallas_call(
        _outproj_kernel,
        out_shape=jax.ShapeDtypeStruct((m_rows, D_MODEL), f32),
        grid=(m_rows // tm,),
        in_specs=[
            half, half, half, half,
            pl.BlockSpec((tm, D_MODEL), lambda i: (i, 0)),
            pl.BlockSpec((1, 6, D_MODEL), lambda i: (i // per_b, 0, 0)),
            pl.BlockSpec((D_MODEL, D_MODEL), lambda i: (0, 0)),
        ],
        out_specs=pl.BlockSpec((tm, D_MODEL), lambda i: (i, 0)),
        compiler_params=_params("arbitrary"),
        name="outproj",
    )(ofox, ocmp, oslc, owin, x2d, mod3, wo_bf16)


def _mlp_kernel(x_ref, mod_ref, g_ref, wu_ref, wd_ref, o_ref, h_ref, acc_ref):
    f = pl.program_id(1)

    @pl.when(f == 0)
    def _():
        md = mod_ref[0]
        y = _rms(x_ref[...], g_ref[...])
        h_ref[...] = (y * (1.0 + md[4:5]) + md[3:4]).astype(bf16)
        acc_ref[...] = jnp.zeros_like(acc_ref)

    u = jnp.maximum(_dot(h_ref[...], wu_ref[...]), 0.0)
    acc_ref[...] += _dot((u * u).astype(bf16), wd_ref[...])

    @pl.when(f == pl.num_programs(1) - 1)
    def _():
        o_ref[...] = x_ref[...] + mod_ref[0][5:6] * acc_ref[...]


def _mlp(x2d, mod3, norm_g, wu_bf16, wd_bf16, seq_len):
    m_rows = x2d.shape[0]
    tm, tf = 512, 1024
    per_b = seq_len // tm
    return pl.pallas_call(
        _mlp_kernel,
        out_shape=jax.ShapeDtypeStruct((m_rows, D_MODEL), f32),
        grid=(m_rows // tm, D_FF // tf),
        in_specs=[
            pl.BlockSpec((tm, D_MODEL), lambda i, f: (i, 0)),
            pl.BlockSpec((1, 6, D_MODEL), lambda i, f: (i // per_b, 0, 0)),
            pl.BlockSpec((1, D_MODEL), lambda i, f: (0, 0)),
            pl.BlockSpec((D_MODEL, tf), lambda i, f: (0, f)),
            pl.BlockSpec((tf, D_MODEL), lambda i, f: (f, 0)),
        ],
        out_specs=pl.BlockSpec((tm, D_MODEL), lambda i, f: (i, 0)),
        scratch_shapes=[pltpu.VMEM((tm, D_MODEL), bf16), pltpu.VMEM((tm, D_MODEL), f32)],
        compiler_params=_params("arbitrary", "arbitrary"),
        name="mlp",
    )(x2d, mod3, norm_g, wu_bf16, wd_bf16)


def _layer(x, c, w_ada, b_ada, norm1_g, w_in, b_forget, fox_q_norm, fox_k_norm, nsa_q_norm,
           cmp_k_norm, slc_k_norm, win_k_norm, cmp_pe_k, cmp_w1_k, cmp_w2_k, cmp_pe_v, cmp_w1_v,
           cmp_w2_v, w_out, norm2_g, w_up, w_down):
    n_batch, seq_len, _ = x.shape
    n_cmp = (seq_len - CMP_LEN) // CMP_STRIDE + 1
    row = lambda v: v.reshape(1, -1)

    half = HEAD_DIM // 2
    inv_freq = ROPE_THETA ** (-jnp.arange(half, dtype=f32) / half)
    inv_freq = jnp.concatenate([inv_freq, inv_freq]).reshape(1, HEAD_DIM)

    w_a, w_b, w_s = _repack_w_in(w_in.T)

    mod3 = _ada(c, w_ada, b_ada).reshape(n_batch, 6, D_MODEL)
    x2d = x.reshape(n_batch * seq_len, D_MODEL)
    (fq, fk, fvt, cin, nqu, nqr, kc, vc, kaug, vst, kw, vw, gates) = _prep(
        x2d, mod3, row(norm1_g), w_a, w_b, w_s, n_batch, seq_len, row(fox_q_norm), row(fox_k_norm), row(nsa_q_norm), row(slc_k_norm),
        row(win_k_norm), jnp.pad(b_forget, (0, LANES - FOX_HEADS)).reshape(1, LANES), inv_freq)
    kcmp = _compress(kc, cmp_pe_k, cmp_w1_k.astype(bf16), cmp_w2_k.astype(bf16), row(cmp_k_norm))
    vcmp = _compress(vc, cmp_pe_v, cmp_w1_v.astype(bf16), cmp_w2_v.astype(bf16))
    ofox = _fox(fq, fk, fvt, cin)
    ocmp, selneg = _cmp(nqu, kcmp, vcmp, gates, n_cmp)
    oslc, w_up16, w_down16, w_out16 = _slc(nqr, selneg, kaug, vst, gates, [w_up, w_down, w_out])
    owin = _win(nqr, kw, vw, gates)
    x1 = _outproj(ofox.reshape(-1, FOX_W), ocmp.reshape(-1, NSA_W), oslc.reshape(-1, NSA_W),
                  owin.reshape(-1, NSA_W), x2d, mod3, w_out16, seq_len)
    x2 = _mlp(x1, mod3, row(norm2_g), w_up16, w_down16, seq_len)
    return x2.reshape(n_batch, seq_len, D_MODEL)


def kernel(x, c, w_ada, b_ada, norm1_g, w_in, b_forget, fox_q_norm, fox_k_norm, nsa_q_norm, cmp_k_norm,
           slc_k_norm, win_k_norm, cmp_pe_k, cmp_w1_k, cmp_w2_k, cmp_pe_v, cmp_w1_v, cmp_w2_v, w_out,
           norm2_g, w_up, w_down):
    depth = w_ada.shape[0]
    for l in range(depth):
        x = _layer(x, c, w_ada[l], b_ada[l], norm1_g[l], w_in[l], b_forget[l], fox_q_norm[l], fox_k_norm[l],
                   nsa_q_norm[l], cmp_k_norm[l], slc_k_norm[l], win_k_norm[l], cmp_pe_k[l], cmp_w1_k[l],
                   cmp_w2_k[l], cmp_pe_v[l], cmp_w1_v[l], cmp_w2_v[l], w_out[l], norm2_g[l], w_up[l],
                   w_down[l])
    return x
```

```python
import functools
import math
from typing import Any, Callable, NamedTuple

import numpy as np
import jax
import jax.numpy as jnp
from jax import lax
from jax.experimental import pallas as pl
from jax.experimental.pallas import tpu as pltpu

D_MODEL = 2048
HEAD_DIM = 128
FOX_HEADS = 8
NSA_HEADS = 8
NSA_KV_HEADS = 2
NSA_GROUP = NSA_HEADS // NSA_KV_HEADS
N_BRANCH = 3
D_FF = 4 * D_MODEL
ROPE_THETA = 10000.0
CMP_LEN = 32
CMP_STRIDE = 16
CMP_SHIFT = 4
CMP_HIDDEN = 2 * HEAD_DIM
SEL_BLOCK = 64
SEL_SHIFT = 6
SEL_TOPK = 16
WINDOW = 512
NORM_EPS = 1e-6
ATTN_SCALE = HEAD_DIM ** -0.5
FOX_W = FOX_HEADS * HEAD_DIM
NSA_W = NSA_HEADS * HEAD_DIM
KV_W = NSA_KV_HEADS * HEAD_DIM

LANES = 128
F32_SUBLANES = 8
BF16_SUBLANES = 16
SEL_LANES = LANES
MASK_BIG = 1e30
KV_BLOCK = 512
WIN_TILE = 128
LOG2E = math.log2(math.e)

COL_FQ = 0
COL_FK = COL_FQ + FOX_W
COL_FV = COL_FK + FOX_W
COL_NQ = COL_FV + FOX_W
COL_KC = COL_NQ + NSA_W
COL_VC = COL_KC + KV_W
COL_KS = COL_VC + KV_W
COL_VS = COL_KS + KV_W
COL_KW = COL_VS + KV_W
COL_VW = COL_KW + KV_W
COL_SMALL = COL_VW + KV_W
W_IN_Z0 = 3 * FOX_W
W_IN_NQ0 = W_IN_Z0 + FOX_HEADS
W_IN_GZ0 = W_IN_NQ0 + NSA_W + 6 * KV_W

VMEM_LIMIT = 56 * 1024 * 1024
VMEM_LIMIT_PREP = 60 * 1024 * 1024

f32 = jnp.float32
bf16 = jnp.bfloat16


def _params(*sem):
    return pltpu.CompilerParams(dimension_semantics=sem, vmem_limit_bytes=VMEM_LIMIT)


def _dot_nt(a, b):
    return lax.dot_general(a, b, (((1,), (1,)), ((), ())), preferred_element_type=f32)


def _dot(a, b):
    return jnp.dot(a, b, preferred_element_type=f32)


def _split3(x):
    hi = x.astype(bf16)
    r1 = x - hi.astype(f32)
    mid = r1.astype(bf16)
    lo = (r1 - mid.astype(f32)).astype(bf16)
    return hi, mid, lo


def _dot_f32_by_exact(x, w_bf16):
    hi, mid, lo = _split3(x)
    return _dot(hi, w_bf16) + (_dot(mid, w_bf16) + _dot(lo, w_bf16))


def _rms(x, gain):
    ms = jnp.mean(x * x, axis=-1, keepdims=True)
    return x * lax.rsqrt(ms + NORM_EPS) * gain


def _ada_kernel(ct_ref, w_ref, b_ref, o_ref, *, n_batch, k_chunk):
    ct = ct_ref[...]
    act = ct * jax.nn.sigmoid(ct)
    rows = []
    for b in range(n_batch):
        col = act[:, b:b + 1]
        acc = b_ref[...]
        for k0 in range(0, D_MODEL, k_chunk):
            acc = acc + jnp.sum(w_ref[k0:k0 + k_chunk, :] * col[k0:k0 + k_chunk], axis=0, keepdims=True)
        rows.append(acc)
    o_ref[...] = jnp.concatenate(rows, axis=0)


def _ada(c, w_ada, b_ada):
    n_batch = c.shape[0]
    n_out = w_ada.shape[1]
    tn = 2048
    return pl.pallas_call(
        functools.partial(_ada_kernel, n_batch=n_batch, k_chunk=256),
        out_shape=jax.ShapeDtypeStruct((n_batch, n_out), f32),
        grid=(n_out // tn,),
        in_specs=[
            pl.BlockSpec((D_MODEL, n_batch), lambda j: (0, 0)),
            pl.BlockSpec((D_MODEL, tn), lambda j: (0, j)),
            pl.BlockSpec((1, tn), lambda j: (0, j)),
        ],
        out_specs=pl.BlockSpec((n_batch, tn), lambda j: (0, j)),
        compiler_params=_params("arbitrary"),
        name="ada",
    )(c.T, w_ada, b_ada.reshape(1, n_out))


def _repack_kernel(a_ref, *rest, n_part):
    part_refs, (z_ref, gz_ref, wa_ref, wb_ref, ws_ref) = rest[:n_part], rest[n_part:]
    wa_ref[...] = a_ref[...].astype(bf16)
    wb_ref[...] = jnp.concatenate([r[...] for r in part_refs], axis=0).astype(bf16)
    pad = jnp.zeros((LANES - z_ref.shape[0] - gz_ref.shape[0], D_MODEL), f32)
    ws_ref[...] = jnp.concatenate([z_ref[...], gz_ref[...], pad], axis=0).astype(bf16)


def _repack_w_in(w_t):
    n_steps = 16
    ra, rb = W_IN_Z0 // n_steps, (W_IN_GZ0 - W_IN_NQ0) // n_steps
    hb = math.gcd(W_IN_NQ0, rb)
    n_part = rb // hb
    n_z, n_gz = W_IN_NQ0 - W_IN_Z0, w_t.shape[0] - W_IN_GZ0
    assert W_IN_Z0 % n_steps == 0 and (W_IN_GZ0 - W_IN_NQ0) % n_steps == 0 and ra % BF16_SUBLANES == 0 and rb % BF16_SUBLANES == 0
    assert hb % F32_SUBLANES == 0 and W_IN_Z0 % n_z == 0 and W_IN_GZ0 % n_gz == 0
    b_off = W_IN_NQ0 // hb
    part = lambda k: pl.BlockSpec((hb, D_MODEL), lambda j: (b_off + n_part * j + k, 0))
    return pl.pallas_call(
        functools.partial(_repack_kernel, n_part=n_part),
        out_shape=[jax.ShapeDtypeStruct((W_IN_Z0, D_MODEL), bf16),
                   jax.ShapeDtypeStruct((W_IN_GZ0 - W_IN_NQ0, D_MODEL), bf16),
                   jax.ShapeDtypeStruct((LANES, D_MODEL), bf16)],
        grid=(n_steps,),
        in_specs=[pl.BlockSpec((ra, D_MODEL), lambda j: (j, 0))] + [part(k) for k in range(n_part)] + [
            pl.BlockSpec((n_z, D_MODEL), lambda j: (W_IN_Z0 // n_z, 0)),
            pl.BlockSpec((n_gz, D_MODEL), lambda j: (W_IN_GZ0 // n_gz, 0)),
        ],
        out_specs=[pl.BlockSpec((ra, D_MODEL), lambda j: (j, 0)),
                   pl.BlockSpec((rb, D_MODEL), lambda j: (j, 0)),
                   pl.BlockSpec((LANES, D_MODEL), lambda j: (0, 0))],
        compiler_params=_params("arbitrary"),
        name="repack",
    )(*([w_t] * (n_part + 3)))


def _cast_side_job(weights, n_steps, step_of):
    n_cast = max(c for c in range(1, n_steps + 1) if all(w.shape[0] % (BF16_SUBLANES * c) == 0 for w in weights))
    idx = lambda *ids: (jnp.minimum(step_of(*ids), n_cast - 1), 0)
    specs = [pl.BlockSpec((w.shape[0] // n_cast, w.shape[1]), idx) for w in weights]
    return specs, specs, [jax.ShapeDtypeStruct(w.shape, bf16) for w in weights]


def _prep_kernel(x_ref, mod_ref, g1_ref, wa_ref, wb_ref, ws_ref,
                 gq_ref, gk_ref, gn_ref, gs_ref, gw_ref, bf_ref, inv_ref,
                 fq_ref, fk_ref, fvt_ref, cin_ref, nqu_ref, nqr_ref, kc_ref, vc_ref,
                 ks_ref, vst_ref, kw_ref, vw_ref, gate_ref, carry_ref, rot_ref, tri_ref, *, tm):
    i = pl.program_id(1)

    @pl.when(i == 0)
    def _():
        carry_ref[...] = jnp.zeros_like(carry_ref)

    md = mod_ref[0]
    hn = (_rms(x_ref[...], g1_ref[...]) * (1.0 + md[1:2]) + md[0:1]).astype(bf16)
    groups = {}

    def head(col, h):
        base, w_ref, r0 = next((b, w, r) for b, w, r in (
            (COL_KC, wb_ref, NSA_W), (COL_NQ, wb_ref, 0), (COL_FV, wa_ref, 2 * FOX_W),
            (COL_FK, wa_ref, FOX_W), (COL_FQ, wa_ref, 0)) if col >= b)
        if base not in groups:
            n = (COL_SMALL - COL_KC) if base == COL_KC else FOX_W
            groups[base] = _dot_nt(hn, w_ref[r0:r0 + n, :])
        c0 = col - base + h * HEAD_DIM
        return groups[base][:, c0:c0 + HEAD_DIM]

    row = lax.broadcasted_iota(jnp.int32, (tm, LANES), 0)
    lane = lax.broadcasted_iota(jnp.int32, (tm, LANES), 1)
    pos = i * tm + row

    @pl.when((pl.program_id(0) == 0) & (i == 0))
    def _():
        ang_row = row.astype(f32) * inv_ref[...]
        rot_ref[0] = jnp.cos(ang_row)
        rot_ref[1] = jnp.sin(ang_row)
        t_idx = lax.broadcasted_iota(jnp.int32, (tm, tm), 0)
        s_idx = lax.broadcasted_iota(jnp.int32, (tm, tm), 1)
        tri_ref[...] = jnp.where(s_idx <= t_idx, 1.0, 0.0).astype(bf16)

    ang0 = (i * tm).astype(f32) * inv_ref[...]
    cos0, sin0 = jnp.cos(ang0), jnp.sin(ang0)
    cos = cos0 * rot_ref[0] - sin0 * rot_ref[1]
    sin = sin0 * rot_ref[0] + cos0 * rot_ref[1]
    sin_signed = jnp.where(lane < HEAD_DIM // 2, -sin, sin)

    def rope(x):
        return x * cos + pltpu.roll(x, HEAD_DIM // 2, 1) * sin_signed

    small = _dot_nt(hn, ws_ref[...])
    z = small + bf_ref[...]
    logf = jnp.minimum(z, 0.0) - jnp.log1p(jnp.exp(-jnp.abs(z)))
    tri = tri_ref[...]
    hi, mid, lo = _split3(logf)
    local = _dot(tri, hi) + (_dot(tri, mid) + _dot(tri, lo))
    cin_ref[0, 0] = carry_ref[0:1, :]
    carry_ref[...] = carry_ref[...] + local[tm - 1:tm, :]
    b_hi, b_mid, b_lo = (v.astype(f32) for v in _split3(local * (-LOG2E)))
    ones3 = jnp.where(lane < 3, 1.0, 0.0).astype(bf16)

    for h in range(FOX_HEADS):
        q = _rms(head(COL_FQ, h), gq_ref[...] * (ATTN_SCALE * LOG2E))
        fq_ref[0, h] = jnp.concatenate([q.astype(bf16), ones3], axis=1)
    for h in range(FOX_HEADS):
        k = _rms(head(COL_FK, h), gk_ref[...])
        bias = jnp.where(lane == 0, b_hi[:, h:h + 1],
                         jnp.where(lane == 1, b_mid[:, h:h + 1],
                                   jnp.where(lane == 2, b_lo[:, h:h + 1], 0.0)))
        fk_ref[0, h] = jnp.concatenate([k.astype(bf16), bias.astype(bf16)], axis=1)

    gate_ref[0] = jax.nn.sigmoid(small)

    for h in range(NSA_HEADS):
        qn = _rms(head(COL_NQ, h), gn_ref[...] * (ATTN_SCALE * LOG2E))
        nqu_ref[0, h] = qn.astype(bf16)
        nqr_ref[0, h] = rope(qn).astype(bf16)
    onehot = jnp.where(lane == lax.shift_right_logical(pos, SEL_SHIFT), 1.0, 0.0).astype(bf16)
    for g in range(NSA_KV_HEADS):
        kc_ref[0, g] = head(COL_KC, g)
        vc_ref[0, g] = head(COL_VC, g)
        ks = rope(_rms(head(COL_KS, g), gs_ref[...])).astype(bf16)
        ks_ref[0, g] = jnp.concatenate([ks, onehot], axis=1)
        vst_ref[0, g, 0] = head(COL_VS, g).T.astype(bf16)
        kw_ref[0, g] = rope(_rms(head(COL_KW, g), gw_ref[...])).astype(bf16)
        vwt = head(COL_VW, g).T.astype(bf16)
        for c in range(tm // WIN_TILE):
            vw_ref[0, g, c] = vwt[:, c * WIN_TILE:(c + 1) * WIN_TILE]
    for h in range(FOX_HEADS):
        fvt_ref[0, h, 0] = head(COL_FV, h).T.astype(bf16)


def _prep(x2d, mod3, norm_g, wa, wb, ws, n_batch, seq_len, gq, gk, gn, gs, gw, b_forget_row, inv_freq):
    resident = lambda w: pl.BlockSpec(w.shape, lambda b, i: (0, 0), pipeline_mode=pl.Buffered(1))
    tm = KV_BLOCK
    per_b = seq_len // tm
    hshape = lambda n, w, dt: jax.ShapeDtypeStruct((n_batch, n, seq_len, w), dt)
    hspec = lambda n, w: pl.BlockSpec((1, n, tm, w), lambda b, i: (b, 0, i, 0))
    tshape = lambda n: jax.ShapeDtypeStruct((n_batch, n, per_b, HEAD_DIM, tm), bf16)
    tspec = lambda n: pl.BlockSpec((1, n, 1, HEAD_DIM, tm), lambda b, i: (b, 0, i, 0, 0))
    vec = pl.BlockSpec((1, LANES), lambda b, i: (0, 0))
    return pl.pallas_call(
        functools.partial(_prep_kernel, tm=tm),
        out_shape=[
            hshape(FOX_HEADS, 2 * HEAD_DIM, bf16), hshape(FOX_HEADS, 2 * HEAD_DIM, bf16), tshape(FOX_HEADS),
            jax.ShapeDtypeStruct((n_batch, per_b, 1, LANES), f32),
            hshape(NSA_HEADS, HEAD_DIM, bf16), hshape(NSA_HEADS, HEAD_DIM, bf16),
            hshape(NSA_KV_HEADS, HEAD_DIM, f32), hshape(NSA_KV_HEADS, HEAD_DIM, f32),
            hshape(NSA_KV_HEADS, HEAD_DIM + SEL_LANES, bf16), tshape(NSA_KV_HEADS),
            hshape(NSA_KV_HEADS, HEAD_DIM, bf16),
            jax.ShapeDtypeStruct((n_batch, NSA_KV_HEADS, seq_len // WIN_TILE, HEAD_DIM, WIN_TILE), bf16),
            jax.ShapeDtypeStruct((n_batch, seq_len, LANES), f32),
        ],
        grid=(n_batch, per_b),
        in_specs=[
            pl.BlockSpec((tm, D_MODEL), lambda b, i: (b * per_b + i, 0)),
            pl.BlockSpec((1, 6, D_MODEL), lambda b, i: (b, 0, 0)),
            pl.BlockSpec((1, D_MODEL), lambda b, i: (0, 0)),
            resident(wa), resident(wb), resident(ws),
            vec, vec, vec, vec, vec, vec, vec,
        ],
        out_specs=[
            hspec(FOX_HEADS, 2 * HEAD_DIM), hspec(FOX_HEADS, 2 * HEAD_DIM), tspec(FOX_HEADS),
            pl.BlockSpec((1, 1, 1, LANES), lambda b, i: (b, i, 0, 0)),
            hspec(NSA_HEADS, HEAD_DIM), hspec(NSA_HEADS, HEAD_DIM),
            hspec(NSA_KV_HEADS, HEAD_DIM), hspec(NSA_KV_HEADS, HEAD_DIM),
            hspec(NSA_KV_HEADS, HEAD_DIM + SEL_LANES), tspec(NSA_KV_HEADS),
            hspec(NSA_KV_HEADS, HEAD_DIM),
            pl.BlockSpec((1, NSA_KV_HEADS, tm // WIN_TILE, HEAD_DIM, WIN_TILE), lambda b, i: (b, 0, i, 0, 0)),
            pl.BlockSpec((1, tm, LANES), lambda b, i: (b, i, 0)),
        ],
        scratch_shapes=[pltpu.VMEM((8, LANES), f32), pltpu.VMEM((2, tm, LANES), f32), pltpu.VMEM((tm, tm), bf16)],
        compiler_params=pltpu.CompilerParams(dimension_semantics=("arbitrary", "arbitrary"),
                                             vmem_limit_bytes=VMEM_LIMIT_PREP),
        name="prep",
    )(x2d, mod3, norm_g, wa, wb, ws, gq, gk, gn, gs, gw, b_forget_row, inv_freq)


def _compress_kernel(x_ref, pe_ref, w1_ref, w2_ref, *rest, n_rows, do_norm):
    o_ref = rest[-1]
    a = jnp.zeros((n_rows, CMP_HIDDEN), f32)
    b = jnp.zeros((n_rows, CMP_HIDDEN), f32)
    for l in range(CMP_STRIDE):
        x = x_ref[0, 0, pl.ds(l, n_rows, stride=CMP_STRIDE), :]
        a = a + _dot((x + pe_ref[l:l + 1, :]).astype(bf16), w1_ref[l * HEAD_DIM:(l + 1) * HEAD_DIM, :])
        lb = CMP_STRIDE + l
        b = b + _dot((x + pe_ref[lb:lb + 1, :]).astype(bf16), w1_ref[lb * HEAD_DIM:(lb + 1) * HEAD_DIM, :])
    pre = a + pltpu.roll(b, n_rows - 1, 0)
    hid = pre * jax.nn.sigmoid(pre)
    out = _dot(hid.astype(bf16), w2_ref[...])
    if do_norm:
        out = _rms(out, rest[0][...])
    o_ref[0, 0] = out.astype(bf16)


def _compress(x4, pe, w1_bf16, w2_bf16, gain=None):
    n_batch, n_g, seq_len, _ = x4.shape
    n_rows = seq_len // CMP_STRIDE
    do_norm = gain is not None
    return pl.pallas_call(
        functools.partial(_compress_kernel, n_rows=n_rows, do_norm=do_norm),
        out_shape=jax.ShapeDtypeStruct((n_batch, n_g, n_rows, HEAD_DIM), bf16),
        grid=(n_batch, n_g),
        in_specs=[
            pl.BlockSpec((1, 1, seq_len, HEAD_DIM), lambda b, g: (b, g, 0, 0)),
            pl.BlockSpec((CMP_LEN, HEAD_DIM), lambda b, g: (0, 0)),
            pl.BlockSpec((CMP_LEN * HEAD_DIM, CMP_HIDDEN), lambda b, g: (0, 0)),
            pl.BlockSpec((CMP_HIDDEN, HEAD_DIM), lambda b, g: (0, 0)),
        ] + ([pl.BlockSpec((1, HEAD_DIM), lambda b, g: (0, 0))] if do_norm else []),
        out_specs=pl.BlockSpec((1, 1, n_rows, HEAD_DIM), lambda b, g: (b, g, 0, 0)),
        compiler_params=_params("arbitrary", "arbitrary"),
        name="compress",
    )(x4, pe, w1_bf16, w2_bf16, *([gain] if do_norm else []))


def _softmax_block(s, d, m_prev, l_prev):
    m_blk = jnp.max(s, axis=0, keepdims=True)
    m_new = jnp.maximum(m_prev, m_blk if d is None else m_blk + d)
    p = jnp.exp2(s - (m_new if d is None else m_new - d))
    alpha = jnp.exp2(m_prev - m_new)
    l_new = alpha * l_prev + jnp.sum(p, axis=0, keepdims=True)
    return m_new, l_new, alpha, p.astype(bf16)


class _Chain(NamedTuple):
    scores: Callable
    values: Callable
    offset: Callable
    diag_mask: Callable
    emit: Callable
    next_first: Callable
    s_buf: Any
    p_buf: Any


def _flash_sweep(n_full, chains, n_q, is_first):
    @pl.when(is_first)
    def _():
        for c in chains:
            c.s_buf[0] = c.scores(0)

    for c in chains:
        c.p_buf[1] = jnp.zeros(c.p_buf.shape[1:], c.p_buf.dtype)

    def accumulate(c, j, slot, alpha, acc):
        return alpha * acc + _dot(c.values(jnp.maximum(j, 0)), c.p_buf[slot])

    def step(j, cur, states):
        out = []
        for c, (m, l, acc, alpha_prev) in zip(chains, states):
            acc = accumulate(c, j - 1, 1 - cur, alpha_prev, acc)
            m, l, alpha, p = _softmax_block(c.s_buf[cur], c.offset(j), m, l)
            c.p_buf[cur] = p
            c.s_buf[1 - cur] = c.scores(j + 1)
            out.append((m, l, acc, alpha))
        return tuple(out)

    def finish(cur, states):
        nxt = [c.next_first() for c in chains]
        for c, s_next, (m, l, acc, alpha_prev) in zip(chains, nxt, states):
            if cur == 1:
                c.s_buf[0] = s_next
            acc = accumulate(c, n_full - 1, 1 - cur, alpha_prev, acc)
            m, l, alpha, p = _softmax_block(c.diag_mask(c.s_buf[cur]), None, m, l)
            if cur == 0:
                c.s_buf[0] = s_next
            acc = alpha * acc + _dot(c.values(n_full), p)
            c.emit(acc / l)

    init = tuple((jnp.full((1, n_q), -jnp.inf, f32), jnp.zeros((1, n_q), f32),
                  jnp.zeros((HEAD_DIM, n_q), f32), jnp.ones((1, n_q), f32)) for _ in chains)
    states = lax.fori_loop(0, n_full // 2, lambda jj, st: step(2 * jj + 1, 1, step(2 * jj, 0, st)), init)

    @pl.when(n_full % 2 == 0)
    def _():
        finish(0, states)

    @pl.when(n_full % 2 == 1)
    def _():
        finish(1, step(n_full - 1, 0, states))


def _fox_kernel(q_ref, qn_ref, k_ref, vt_ref, c_ref, o_ref, *bufs, tq, n_heads):
    hb = pl.program_id(1)
    i = pl.program_id(2)
    lane = lax.broadcasted_iota(jnp.int32, (1, LANES), 1)

    def diag_mask(s):
        kk = lax.broadcasted_iota(jnp.int32, (tq, tq), 0)
        qq = lax.broadcasted_iota(jnp.int32, (tq, tq), 1)
        return jnp.where(kk <= qq, s, -jnp.inf)

    def chain(u):
        qa = q_ref[0, u]

        def cin(j):
            return jnp.sum(jnp.where(lane == hb * n_heads + u, c_ref[0, j], 0.0), axis=-1, keepdims=True)

        ci = cin(i)

        def scores(j):
            return _dot_nt(k_ref[0, u, pl.ds(pl.multiple_of(j * tq, tq), tq), :], qa)

        def emit(out_t):
            o_ref[0, :, u * HEAD_DIM:(u + 1) * HEAD_DIM] = out_t.T.astype(bf16)

        def next_first():
            return _dot_nt(k_ref[0, u, 0:tq, :], qn_ref[0, u])

        return _Chain(scores, lambda j: vt_ref[0, u, j], lambda j: (ci - cin(j)) * LOG2E, diag_mask, emit,
                      next_first, bufs[2 * u], bufs[2 * u + 1])

    _flash_sweep(i, [chain(u) for u in range(n_heads)], tq, i == 0)


def _fox(fq, fk, fvt, cin):
    n_batch, n_h, seq_len, _ = fq.shape
    tq = KV_BLOCK
    nkb = seq_len // tq
    n_heads = 4
    return pl.pallas_call(
        functools.partial(_fox_kernel, tq=tq, n_heads=n_heads),
        out_shape=jax.ShapeDtypeStruct((n_batch, seq_len, n_h * HEAD_DIM), bf16),
        grid=(n_batch, n_h // n_heads, nkb),
        in_specs=[
            pl.BlockSpec((1, n_heads, tq, 2 * HEAD_DIM), lambda b, h, i: (b, h, i, 0)),
            pl.BlockSpec((1, n_heads, tq, 2 * HEAD_DIM), lambda b, h, i: (b, h, jnp.minimum(i + 1, nkb - 1), 0)),
            pl.BlockSpec((1, n_heads, seq_len, 2 * HEAD_DIM), lambda b, h, i: (b, h, 0, 0)),
            pl.BlockSpec((1, n_heads, nkb, HEAD_DIM, tq), lambda b, h, i: (b, h, 0, 0, 0)),
            pl.BlockSpec((1, nkb, 1, LANES), lambda b, h, i: (b, 0, 0, 0)),
        ],
        out_specs=pl.BlockSpec((1, tq, n_heads * HEAD_DIM), lambda b, h, i: (b, i, h)),
        scratch_shapes=[pltpu.VMEM((2, tq, tq), f32), pltpu.VMEM((2, tq, tq), bf16)] * n_heads,
        compiler_params=_params("arbitrary", "arbitrary", "arbitrary"),
        name="fox",
    )(fq, fq, fk, fvt, cin)


def _cmp_kernel(q_ref, k_ref, v_ref, gate_ref, ov_ref, o_ref, sel_ref, *, tq, n_rows, n_cmp, n_sel):
    i = pl.program_id(2)
    g = pl.program_id(1)
    hg = NSA_GROUP
    q = q_ref[0].reshape(hg * tq, HEAD_DIM)
    s = _dot_nt(q, k_ref[0, 0]).reshape(hg, tq, n_rows)
    t = i * tq + lax.broadcasted_iota(jnp.int32, (1, tq, 1), 1)
    n = lax.broadcasted_iota(jnp.int32, (1, 1, n_rows), 2)
    last = jnp.minimum(lax.shift_right_arithmetic(t - (CMP_LEN - 1), CMP_SHIFT), n_cmp - 1)
    s = jnp.where(n <= last, s, -jnp.inf)
    m = jnp.max(s, axis=-1, keepdims=True)
    m = jnp.where(m == -jnp.inf, 0.0, m)
    e = jnp.exp2(s - m)
    p = e * (1.0 / jnp.maximum(jnp.sum(e, axis=-1, keepdims=True), 1e-30))
    o = _dot(p.reshape(hg * tq, n_rows).astype(bf16), v_ref[0, 0])
    _write_gated(o_ref, gate_ref, o, g, 0, tq)

    lane = lax.broadcasted_iota(jnp.int32, (tq, LANES), 1)
    tq_pos = i * tq + lax.broadcasted_iota(jnp.int32, (tq, 1), 0)
    q_blk = lax.shift_right_logical(tq_pos, SEL_SHIFT)
    causal = lane <= q_blk
    n_top = min(SEL_TOPK, n_sel)

    @pl.when((i + 1) * tq <= n_top * SEL_BLOCK)
    def _():
        sel_ref[0, 0] = jnp.where(causal, 0.0, -MASK_BIG).astype(bf16)

    @pl.when((i + 1) * tq > n_top * SEL_BLOCK)
    def _():
        psum = p[0] + p[1] + p[2] + p[3]
        imp = _dot_f32_by_exact(psum, ov_ref[...])
        forced = (lane == 0) | (lane == q_blk) | (lane == q_blk - 1)
        key = jnp.where(forced, -2.0, jnp.where(causal, imp, -1.0))
        key = jnp.where(lane < n_sel, key, -3.0)
        lane_f = lane.astype(f32)
        sel = forced
        for _ in range(n_top - 3):
            mx = jnp.max(key, axis=-1, keepdims=True)
            first = jnp.min(jnp.where(key == mx, lane_f, float(LANES)), axis=-1, keepdims=True)
            pick = lane_f == first
            sel = sel | pick
            key = jnp.where(pick, -2.0, key)
        sel = ((q_blk < n_top) | sel) & causal
        sel_ref[0, 0] = jnp.where(sel, 0.0, -MASK_BIG).astype(bf16)


def _cmp(nqu, kcmp, vcmp, gates, n_cmp):
    n_batch, _, seq_len, _ = nqu.shape
    n_rows = kcmp.shape[2]
    n_sel = seq_len // SEL_BLOCK
    assert 3 <= n_sel <= SEL_LANES
    tq = 1024
    c0 = np.arange(n_rows) * CMP_STRIDE
    s0 = np.arange(SEL_LANES) * SEL_BLOCK
    overlap = np.clip(np.minimum(c0[:, None] + CMP_LEN, s0[None, :] + SEL_BLOCK)
                      - np.maximum(c0[:, None], s0[None, :]), 0, None).astype(np.float32) / CMP_LEN
    overlap[n_cmp:, :] = 0.0
    overlap[:, n_sel:] = 0.0
    return pl.pallas_call(
        functools.partial(_cmp_kernel, tq=tq, n_rows=n_rows, n_cmp=n_cmp, n_sel=n_sel),
        out_shape=[
            jax.ShapeDtypeStruct((n_batch, seq_len, NSA_W), bf16),
            jax.ShapeDtypeStruct((n_batch, NSA_KV_HEADS, seq_len, SEL_LANES), bf16),
        ],
        grid=(n_batch, NSA_KV_HEADS, seq_len // tq),
        in_specs=[
            pl.BlockSpec((1, NSA_GROUP, tq, HEAD_DIM), lambda b, g, i: (b, g, i, 0)),
            pl.BlockSpec((1, 1, n_rows, HEAD_DIM), lambda b, g, i: (b, g, 0, 0)),
            pl.BlockSpec((1, 1, n_rows, HEAD_DIM), lambda b, g, i: (b, g, 0, 0)),
            pl.BlockSpec((1, tq, LANES), lambda b, g, i: (b, i, 0)),
            pl.BlockSpec((n_rows, SEL_LANES), lambda b, g, i: (0, 0)),
        ],
        out_specs=[
            pl.BlockSpec((1, tq, NSA_GROUP * HEAD_DIM), lambda b, g, i: (b, i, g)),
            pl.BlockSpec((1, 1, tq, SEL_LANES), lambda b, g, i: (b, g, i, 0)),
        ],
        compiler_params=_params("arbitrary", "arbitrary", "arbitrary"),
        name="cmp",
    )(nqu, kcmp, vcmp, gates, jnp.asarray(overlap, dtype=bf16))


def _write_gated(o_ref, gate_ref, out, g, branch, tq):
    gates = gate_ref[0]
    lane = lax.broadcasted_iota(jnp.int32, (tq, LANES), 1)
    for h in range(NSA_GROUP):
        gcol = FOX_HEADS + (g * NSA_GROUP + h) * N_BRANCH + branch
        gh = jnp.sum(jnp.where(lane == gcol, gates, 0.0), axis=-1, keepdims=True)
        o_ref[0, :, h * HEAD_DIM:(h + 1) * HEAD_DIM] = (gh * out[h * tq:(h + 1) * tq]).astype(bf16)


def _slc_kernel(q_ref, sel_ref, qn_ref, seln_ref, k_ref, vt_ref, gate_ref, *rest, tq, tk, n_cast_w):
    o_ref = rest[n_cast_w]
    bufs = rest[2 * n_cast_w + 1:]
    for w32_ref, w16_ref in zip(rest[:n_cast_w], rest[n_cast_w + 1:2 * n_cast_w + 1]):
        w16_ref[...] = w32_ref[...].astype(bf16)
    i = pl.program_id(1)
    hg = NSA_GROUP
    n_q = hg * tq
    q0 = i * tq
    jd = q0 // tk

    def diag_mask(s):
        kp = jd * tk + lax.broadcasted_iota(jnp.int32, (tk, n_q), 0)
        t = q0 + (lax.broadcasted_iota(jnp.int32, (tk, n_q), 1) & (tq - 1))
        return jnp.where(kp <= t, s, -jnp.inf)

    def chain(g):
        def q_aug(qr, sr):
            q = qr[0, g * hg:(g + 1) * hg].reshape(n_q, HEAD_DIM)
            return jnp.concatenate([q, jnp.concatenate([sr[0, g]] * hg, axis=0)], axis=1)

        qa = q_aug(q_ref, sel_ref)

        def scores(j):
            return _dot_nt(k_ref[0, g, pl.ds(pl.multiple_of(j * tk, tk), tk), :], qa)

        def next_first():
            return _dot_nt(k_ref[0, g, 0:tk, :], q_aug(qn_ref, seln_ref))

        def emit(out):
            gates = gate_ref[0]
            lane = lax.broadcasted_iota(jnp.int32, (tq, LANES), 1)
            for h in range(hg):
                head = g * hg + h
                gh = jnp.sum(jnp.where(lane == FOX_HEADS + head * N_BRANCH + 1, gates, 0.0),
                             axis=-1, keepdims=True)
                o_ref[0, :, head * HEAD_DIM:(head + 1) * HEAD_DIM] = (
                    gh * out[:, h * tq:(h + 1) * tq].T).astype(bf16)

        return _Chain(scores, lambda j: vt_ref[0, g, j], lambda j: None, diag_mask, emit, next_first,
                      bufs[2 * g], bufs[2 * g + 1])

    _flash_sweep(jd, [chain(g) for g in range(NSA_KV_HEADS)], n_q, i == 0)


def _slc(nqr, selneg, kaug, vst, gates, cast_weights):
    n_batch, _, seq_len, _ = nqr.shape
    tq, tk = 256, KV_BLOCK
    nkb = seq_len // tk
    assert tq & (tq - 1) == 0
    n_q = NSA_GROUP * tq
    n_i = seq_len // tq
    cast_in, cast_out, cast_shapes = _cast_side_job(cast_weights, n_batch * n_i, lambda b, i: b * n_i + i)
    return pl.pallas_call(
        functools.partial(_slc_kernel, tq=tq, tk=tk, n_cast_w=len(cast_weights)),
        out_shape=[jax.ShapeDtypeStruct((n_batch, seq_len, NSA_W), bf16)] + cast_shapes,
        grid=(n_batch, n_i),
        in_specs=[
            pl.BlockSpec((1, NSA_HEADS, tq, HEAD_DIM), lambda b, i: (b, 0, i, 0)),
            pl.BlockSpec((1, NSA_KV_HEADS, tq, SEL_LANES), lambda b, i: (b, 0, i, 0)),
            pl.BlockSpec((1, NSA_HEADS, tq, HEAD_DIM), lambda b, i: (b, 0, jnp.minimum(i + 1, n_i - 1), 0)),
            pl.BlockSpec((1, NSA_KV_HEADS, tq, SEL_LANES), lambda b, i: (b, 0, jnp.minimum(i + 1, n_i - 1), 0)),
            pl.BlockSpec((1, NSA_KV_HEADS, seq_len, HEAD_DIM + SEL_LANES), lambda b, i: (b, 0, 0, 0)),
            pl.BlockSpec((1, NSA_KV_HEADS, nkb, HEAD_DIM, tk), lambda b, i: (b, 0, 0, 0, 0)),
            pl.BlockSpec((1, tq, LANES), lambda b, i: (b, i, 0)),
        ] + cast_in,
        out_specs=[pl.BlockSpec((1, tq, NSA_W), lambda b, i: (b, i, 0))] + cast_out,
        scratch_shapes=[pltpu.VMEM((2, tk, n_q), f32), pltpu.VMEM((2, tk, n_q), bf16)] * NSA_KV_HEADS,
        compiler_params=_params("arbitrary", "arbitrary"),
        name="slc",
    )(nqr, selneg, nqr, selneg, kaug, vst, gates, *cast_weights)


def _win_kernel(q_ref, k_ref, vt_ref, gate_ref, o_ref, *, tq, n_sub):
    i = pl.program_id(2)
    g = pl.program_id(1)
    hg = NSA_GROUP
    span = WINDOW + tq
    q0s = [(i * n_sub + u) * tq for u in range(n_sub)]
    k0s = [pl.multiple_of(jnp.maximum(q0 - WINDOW, 0), tq) for q0 in q0s]
    scores = []
    for u in range(n_sub):
        q = q_ref[0, :, u * tq:(u + 1) * tq, :].reshape(hg * tq, HEAD_DIM)
        scores.append(_dot_nt(k_ref[0, 0, pl.ds(k0s[u], span), :], q))
    probs, denoms = [], []
    for u in range(n_sub):
        kp = k0s[u] + lax.broadcasted_iota(jnp.int32, (span, tq), 0)
        t = q0s[u] + lax.broadcasted_iota(jnp.int32, (span, tq), 1)
        diff = t - kp
        bias = jnp.where((diff >= 0) & (diff < WINDOW), 0.0, -jnp.inf)
        s = jnp.concatenate([scores[u][:, h * tq:(h + 1) * tq] + bias for h in range(hg)], axis=1)
        e = jnp.exp2(s - jnp.max(s, axis=0, keepdims=True))
        denoms.append(jnp.sum(e, axis=0, keepdims=True))
        probs.append(e.astype(bf16))
    gates = gate_ref[0]
    lane = lax.broadcasted_iota(jnp.int32, (tq, LANES), 1)
    for u in range(n_sub):
        jb = k0s[u] // tq
        vt = jnp.concatenate([vt_ref[0, 0, jb + c] for c in range(span // tq)], axis=1)
        out = _dot(vt, probs[u]) / denoms[u]
        for h in range(hg):
            gcol = FOX_HEADS + (g * hg + h) * N_BRANCH + 2
            gh = jnp.sum(jnp.where(lane == gcol, gates[u * tq:(u + 1) * tq], 0.0), axis=-1, keepdims=True)
            o_ref[0, u * tq:(u + 1) * tq, h * HEAD_DIM:(h + 1) * HEAD_DIM] = (
                gh * out[:, h * tq:(h + 1) * tq].T).astype(bf16)


def _win(nqr, kw, vwt, gates):
    n_batch, _, seq_len, _ = nqr.shape
    tq, n_sub = WIN_TILE, 8
    assert seq_len >= WINDOW + tq and WINDOW % tq == 0
    return pl.pallas_call(
        functools.partial(_win_kernel, tq=tq, n_sub=n_sub),
        out_shape=jax.ShapeDtypeStruct((n_batch, seq_len, NSA_W), bf16),
        grid=(n_batch, NSA_KV_HEADS, seq_len // (tq * n_sub)),
        in_specs=[
            pl.BlockSpec((1, NSA_GROUP, tq * n_sub, HEAD_DIM), lambda b, g, i: (b, g, i, 0)),
            pl.BlockSpec((1, 1, seq_len, HEAD_DIM), lambda b, g, i: (b, g, 0, 0)),
            pl.BlockSpec((1, 1, seq_len // tq, HEAD_DIM, tq), lambda b, g, i: (b, g, 0, 0, 0)),
            pl.BlockSpec((1, tq * n_sub, LANES), lambda b, g, i: (b, i, 0)),
        ],
        out_specs=pl.BlockSpec((1, tq * n_sub, NSA_GROUP * HEAD_DIM), lambda b, g, i: (b, i, g)),
        compiler_params=_params("arbitrary", "arbitrary", "arbitrary"),
        name="win",
    )(nqr, kw, vwt, gates)


def _outproj_kernel(fox_ref, c_ref, s_ref, w_ref, x_ref, mod_ref, wo_ref, o_ref):
    mix = _dot(fox_ref[...], wo_ref[0:FOX_W, :])
    nsa = c_ref[...].astype(f32) + s_ref[...].astype(f32) + w_ref[...].astype(f32)
    mix = mix + _dot(nsa.astype(bf16), wo_ref[FOX_W:FOX_W + NSA_W, :])
    o_ref[...] = x_ref[...] + mod_ref[0][2:3] * mix


def _outproj(ofox, ocmp, oslc, owin, x2d, mod3, wo_bf16, seq_len):
    m_rows = x2d.shape[0]
    tm = 512
    per_b = seq_len // tm
    half = pl.BlockSpec((tm, FOX_W), lambda i: (i, 0))
    return pl.pallas_call(
        _outproj_kernel,
        out_shape=jax.ShapeDtypeStruct((m_rows, D_MODEL), f32),
        grid=(m_rows // tm,),
        in_specs=[
            half, half, half, half,
            pl.BlockSpec((tm, D_MODEL), lambda i: (i, 0)),
            pl.BlockSpec((1, 6, D_MODEL), lambda i: (i // per_b, 0, 0)),
            pl.BlockSpec((D_MODEL, D_MODEL), lambda i: (0, 0)),
        ],
        out_specs=pl.BlockSpec((tm, D_MODEL), lambda i: (i, 0)),
        compiler_params=_params("arbitrary"),
        name="outproj",
    )(ofox, ocmp, oslc, owin, x2d, mod3, wo_bf16)


def _mlp_kernel(x_ref, mod_ref, g_ref, wu_ref, wd_ref, o_ref, h_ref, acc_ref):
    f = pl.program_id(1)

    @pl.when(f == 0)
    def _():
        md = mod_ref[0]
        y = _rms(x_ref[...], g_ref[...])
        h_ref[...] = (y * (1.0 + md[4:5]) + md[3:4]).astype(bf16)
        acc_ref[...] = jnp.zeros_like(acc_ref)

    u = jnp.maximum(_dot(h_ref[...], wu_ref[...]), 0.0)
    acc_ref[...] += _dot((u * u).astype(bf16), wd_ref[...])

    @pl.when(f == pl.num_programs(1) - 1)
    def _():
        o_ref[...] = x_ref[...] + mod_ref[0][5:6] * acc_ref[...]


def _mlp(x2d, mod3, norm_g, wu_bf16, wd_bf16, seq_len):
    m_rows = x2d.shape[0]
    tm, tf = 512, 1024
    per_b = seq_len // tm
    return pl.pallas_call(
        _mlp_kernel,
        out_shape=jax.ShapeDtypeStruct((m_rows, D_MODEL), f32),
        grid=(m_rows // tm, D_FF // tf),
        in_specs=[
            pl.BlockSpec((tm, D_MODEL), lambda i, f: (i, 0)),
            pl.BlockSpec((1, 6, D_MODEL), lambda i, f: (i // per_b, 0, 0)),
            pl.BlockSpec((1, D_MODEL), lambda i, f: (0, 0)),
            pl.BlockSpec((D_MODEL, tf), lambda i, f: (0, f)),
            pl.BlockSpec((tf, D_MODEL), lambda i, f: (f, 0)),
        ],
        out_specs=pl.BlockSpec((tm, D_MODEL), lambda i, f: (i, 0)),
        scratch_shapes=[pltpu.VMEM((tm, D_MODEL), bf16), pltpu.VMEM((tm, D_MODEL), f32)],
        compiler_params=_params("arbitrary", "arbitrary"),
        name="mlp",
    )(x2d, mod3, norm_g, wu_bf16, wd_bf16)


def _layer(x, c, w_ada, b_ada, norm1_g, w_in, b_forget, fox_q_norm, fox_k_norm, nsa_q_norm,
           cmp_k_norm, slc_k_norm, win_k_norm, cmp_pe_k, cmp_w1_k, cmp_w2_k, cmp_pe_v, cmp_w1_v,
           cmp_w2_v, w_out, norm2_g, w_up, w_down):
    n_batch, seq_len, _ = x.shape
    n_cmp = (seq_len - CMP_LEN) // CMP_STRIDE + 1
    row = lambda v: v.reshape(1, -1)

    half = HEAD_DIM // 2
    inv_freq = ROPE_THETA ** (-jnp.arange(half, dtype=f32) / half)
    inv_freq = jnp.concatenate([inv_freq, inv_freq]).reshape(1, HEAD_DIM)

    w_a, w_b, w_s = _repack_w_in(w_in.T)

    mod3 = _ada(c, w_ada, b_ada).reshape(n_batch, 6, D_MODEL)
    x2d = x.reshape(n_batch * seq_len, D_MODEL)
    (fq, fk, fvt, cin, nqu, nqr, kc, vc, kaug, vst, kw, vw, gates) = _prep(
        x2d, mod3, row(norm1_g), w_a, w_b, w_s, n_batch, seq_len, row(fox_q_norm), row(fox_k_norm), row(nsa_q_norm), row(slc_k_norm),
        row(win_k_norm), jnp.pad(b_forget, (0, LANES - FOX_HEADS)).reshape(1, LANES), inv_freq)
    kcmp = _compress(kc, cmp_pe_k, cmp_w1_k.astype(bf16), cmp_w2_k.astype(bf16), row(cmp_k_norm))
    vcmp = _compress(vc, cmp_pe_v, cmp_w1_v.astype(bf16), cmp_w2_v.astype(bf16))
    ofox = _fox(fq, fk, fvt, cin)
    ocmp, selneg = _cmp(nqu, kcmp, vcmp, gates, n_cmp)
    oslc, w_up16, w_down16, w_out16 = _slc(nqr, selneg, kaug, vst, gates, [w_up, w_down, w_out])
    owin = _win(nqr, kw, vw, gates)
    x1 = _outproj(ofox.reshape(-1, FOX_W), ocmp.reshape(-1, NSA_W), oslc.reshape(-1, NSA_W),
                  owin.reshape(-1, NSA_W), x2d, mod3, w_out16, seq_len)
    x2 = _mlp(x1, mod3, row(norm2_g), w_up16, w_down16, seq_len)
    return x2.reshape(n_batch, seq_len, D_MODEL)


def kernel(x, c, w_ada, b_ada, norm1_g, w_in, b_forget, fox_q_norm, fox_k_norm, nsa_q_norm, cmp_k_norm,
           slc_k_norm, win_k_norm, cmp_pe_k, cmp_w1_k, cmp_w2_k, cmp_pe_v, cmp_w1_v, cmp_w2_v, w_out,
           norm2_g, w_up, w_down):
    depth = w_ada.shape[0]
    for l in range(depth):
        x = _layer(x, c, w_ada[l], b_ada[l], norm1_g[l], w_in[l], b_forget[l], fox_q_norm[l], fox_k_norm[l],
                   nsa_q_norm[l], cmp_k_norm[l], slc_k_norm[l], win_k_norm[l], cmp_pe_k[l], cmp_w1_k[l],
                   cmp_w2_k[l], cmp_pe_v[l], cmp_w1_v[l], cmp_w2_v[l], w_out[l], norm2_g[l], w_up[l],
                   w_down[l])
    return x
```

```python
import functools
import math
from typing import Any, Callable, NamedTuple

import numpy as np
import jax
import jax.numpy as jnp
from jax import lax
from jax.experimental import pallas as pl
from jax.experimental.pallas import tpu as pltpu

D_MODEL = 2048
HEAD_DIM = 128
FOX_HEADS = 8
NSA_HEADS = 8
NSA_KV_HEADS = 2
NSA_GROUP = NSA_HEADS // NSA_KV_HEADS
N_BRANCH = 3
D_FF = 4 * D_MODEL
ROPE_THETA = 10000.0
CMP_LEN = 32
CMP_STRIDE = 16
CMP_SHIFT = 4
CMP_HIDDEN = 2 * HEAD_DIM
SEL_BLOCK = 64
SEL_SHIFT = 6
SEL_TOPK = 16
WINDOW = 512
NORM_EPS = 1e-6
ATTN_SCALE = HEAD_DIM ** -0.5
FOX_W = FOX_HEADS * HEAD_DIM
NSA_W = NSA_HEADS * HEAD_DIM
KV_W = NSA_KV_HEADS * HEAD_DIM

LANES = 128
F32_SUBLANES = 8
BF16_SUBLANES = 16
SEL_LANES = LANES
MASK_BIG = 1e30
KV_BLOCK = 512
WIN_TILE = 128
LOG2E = math.log2(math.e)

COL_FQ = 0
COL_FK = COL_FQ + FOX_W
COL_FV = COL_FK + FOX_W
COL_NQ = COL_FV + FOX_W
COL_KC = COL_NQ + NSA_W
COL_VC = COL_KC + KV_W
COL_KS = COL_VC + KV_W
COL_VS = COL_KS + KV_W
COL_KW = COL_VS + KV_W
COL_VW = COL_KW + KV_W
COL_SMALL = COL_VW + KV_W
W_IN_Z0 = 3 * FOX_W
W_IN_NQ0 = W_IN_Z0 + FOX_HEADS
W_IN_GZ0 = W_IN_NQ0 + NSA_W + 6 * KV_W

VMEM_LIMIT = 56 * 1024 * 1024
VMEM_LIMIT_PREP = 60 * 1024 * 1024

f32 = jnp.float32
bf16 = jnp.bfloat16


def _params(*sem):
    return pltpu.CompilerParams(dimension_semantics=sem, vmem_limit_bytes=VMEM_LIMIT)


def _dot_nt(a, b):
    return lax.dot_general(a, b, (((1,), (1,)), ((), ())), preferred_element_type=f32)


def _dot(a, b):
    return jnp.dot(a, b, preferred_element_type=f32)


def _split3(x):
    hi = x.astype(bf16)
    r1 = x - hi.astype(f32)
    mid = r1.astype(bf16)
    lo = (r1 - mid.astype(f32)).astype(bf16)
    return hi, mid, lo


def _dot_f32_by_exact(x, w_bf16):
    hi, mid, lo = _split3(x)
    return _dot(hi, w_bf16) + (_dot(mid, w_bf16) + _dot(lo, w_bf16))


def _rms(x, gain):
    ms = jnp.mean(x * x, axis=-1, keepdims=True)
    return x * lax.rsqrt(ms + NORM_EPS) * gain


def _ada_kernel(ct_ref, w_ref, b_ref, o_ref, *, n_batch, k_chunk):
    ct = ct_ref[...]
    act = ct * jax.nn.sigmoid(ct)
    rows = []
    for b in range(n_batch):
        col = act[:, b:b + 1]
        acc = b_ref[...]
        for k0 in range(0, D_MODEL, k_chunk):
            acc = acc + jnp.sum(w_ref[k0:k0 + k_chunk, :] * col[k0:k0 + k_chunk], axis=0, keepdims=True)
        rows.append(acc)
    o_ref[...] = jnp.concatenate(rows, axis=0)


def _ada(c, w_ada, b_ada):
    n_batch = c.shape[0]
    n_out = w_ada.shape[1]
    tn = 1024
    return pl.pallas_call(
        functools.partial(_ada_kernel, n_batch=n_batch, k_chunk=256),
        out_shape=jax.ShapeDtypeStruct((n_batch, n_out), f32),
        grid=(n_out // tn,),
        in_specs=[
            pl.BlockSpec((D_MODEL, n_batch), lambda j: (0, 0)),
            pl.BlockSpec((D_MODEL, tn), lambda j: (0, j)),
            pl.BlockSpec((1, tn), lambda j: (0, j)),
        ],
        out_specs=pl.BlockSpec((n_batch, tn), lambda j: (0, j)),
        compiler_params=_params("arbitrary"),
        name="ada",
    )(c.T, w_ada, b_ada.reshape(1, n_out))


def _repack_kernel(a_ref, *rest, n_part):
    part_refs, (z_ref, gz_ref, wa_ref, wb_ref, ws_ref) = rest[:n_part], rest[n_part:]
    wa_ref[...] = a_ref[...].astype(bf16)
    wb_ref[...] = jnp.concatenate([r[...] for r in part_refs], axis=0).astype(bf16)
    pad = jnp.zeros((LANES - z_ref.shape[0] - gz_ref.shape[0], D_MODEL), f32)
    ws_ref[...] = jnp.concatenate([z_ref[...], gz_ref[...], pad], axis=0).astype(bf16)


def _repack_w_in(w_t):
    n_steps = 16
    ra, rb = W_IN_Z0 // n_steps, (W_IN_GZ0 - W_IN_NQ0) // n_steps
    hb = math.gcd(W_IN_NQ0, rb)
    n_part = rb // hb
    n_z, n_gz = W_IN_NQ0 - W_IN_Z0, w_t.shape[0] - W_IN_GZ0
    assert W_IN_Z0 % n_steps == 0 and (W_IN_GZ0 - W_IN_NQ0) % n_steps == 0 and ra % BF16_SUBLANES == 0 and rb % BF16_SUBLANES == 0
    assert hb % F32_SUBLANES == 0 and W_IN_Z0 % n_z == 0 and W_IN_GZ0 % n_gz == 0
    b_off = W_IN_NQ0 // hb
    part = lambda k: pl.BlockSpec((hb, D_MODEL), lambda j: (b_off + n_part * j + k, 0))
    return pl.pallas_call(
        functools.partial(_repack_kernel, n_part=n_part),
        out_shape=[jax.ShapeDtypeStruct((W_IN_Z0, D_MODEL), bf16),
                   jax.ShapeDtypeStruct((W_IN_GZ0 - W_IN_NQ0, D_MODEL), bf16),
                   jax.ShapeDtypeStruct((LANES, D_MODEL), bf16)],
        grid=(n_steps,),
        in_specs=[pl.BlockSpec((ra, D_MODEL), lambda j: (j, 0))] + [part(k) for k in range(n_part)] + [
            pl.BlockSpec((n_z, D_MODEL), lambda j: (W_IN_Z0 // n_z, 0)),
            pl.BlockSpec((n_gz, D_MODEL), lambda j: (W_IN_GZ0 // n_gz, 0)),
        ],
        out_specs=[pl.BlockSpec((ra, D_MODEL), lambda j: (j, 0)),
                   pl.BlockSpec((rb, D_MODEL), lambda j: (j, 0)),
                   pl.BlockSpec((LANES, D_MODEL), lambda j: (0, 0))],
        compiler_params=_params("arbitrary"),
        name="repack",
    )(*([w_t] * (n_part + 3)))


def _cast_side_job(weights, n_steps, step_of):
    n_cast = max(c for c in range(1, n_steps + 1) if all(w.shape[0] % (BF16_SUBLANES * c) == 0 for w in weights))
    idx = lambda *ids: (jnp.minimum(step_of(*ids), n_cast - 1), 0)
    specs = [pl.BlockSpec((w.shape[0] // n_cast, w.shape[1]), idx) for w in weights]
    return specs, specs, [jax.ShapeDtypeStruct(w.shape, bf16) for w in weights]


def _prep_kernel(x_ref, mod_ref, g1_ref, wa_ref, wb_ref, ws_ref,
                 gq_ref, gk_ref, gn_ref, gs_ref, gw_ref, bf_ref, inv_ref,
                 fq_ref, fk_ref, fvt_ref, cin_ref, nqu_ref, nqr_ref, kc_ref, vc_ref,
                 ks_ref, vst_ref, kw_ref, vw_ref, gate_ref, carry_ref, rot_ref, *, tm):
    i = pl.program_id(1)

    @pl.when(i == 0)
    def _():
        carry_ref[...] = jnp.zeros_like(carry_ref)

    md = mod_ref[0]
    hn = (_rms(x_ref[...], g1_ref[...]) * (1.0 + md[1:2]) + md[0:1]).astype(bf16)
    groups = {}

    def head(col, h):
        base, w_ref, r0 = next((b, w, r) for b, w, r in (
            (COL_KC, wb_ref, NSA_W), (COL_NQ, wb_ref, 0), (COL_FV, wa_ref, 2 * FOX_W),
            (COL_FK, wa_ref, FOX_W), (COL_FQ, wa_ref, 0)) if col >= b)
        if base not in groups:
            n = (COL_SMALL - COL_KC) if base == COL_KC else FOX_W
            groups[base] = _dot_nt(hn, w_ref[r0:r0 + n, :])
        c0 = col - base + h * HEAD_DIM
        return groups[base][:, c0:c0 + HEAD_DIM]

    row = lax.broadcasted_iota(jnp.int32, (tm, LANES), 0)
    lane = lax.broadcasted_iota(jnp.int32, (tm, LANES), 1)
    pos = i * tm + row

    @pl.when((pl.program_id(0) == 0) & (i == 0))
    def _():
        ang_row = row.astype(f32) * inv_ref[...]
        rot_ref[0] = jnp.cos(ang_row)
        rot_ref[1] = jnp.sin(ang_row)

    ang0 = (i * tm).astype(f32) * inv_ref[...]
    cos0, sin0 = jnp.cos(ang0), jnp.sin(ang0)
    cos = cos0 * rot_ref[0] - sin0 * rot_ref[1]
    sin = sin0 * rot_ref[0] + cos0 * rot_ref[1]
    sin_signed = jnp.where(lane < HEAD_DIM // 2, -sin, sin)

    def rope(x):
        return x * cos + pltpu.roll(x, HEAD_DIM // 2, 1) * sin_signed

    small = _dot_nt(hn, ws_ref[...])
    z = small + bf_ref[...]
    logf = jnp.minimum(z, 0.0) - jnp.log1p(jnp.exp(-jnp.abs(z)))
    t_idx = lax.broadcasted_iota(jnp.int32, (tm, tm), 0)
    s_idx = lax.broadcasted_iota(jnp.int32, (tm, tm), 1)
    tri = jnp.where(s_idx <= t_idx, 1.0, 0.0).astype(bf16)
    hi, mid, lo = _split3(logf)
    local = _dot(tri, hi) + (_dot(tri, mid) + _dot(tri, lo))
    cin_ref[0, 0] = carry_ref[0:1, :]
    carry_ref[...] = carry_ref[...] + local[tm - 1:tm, :]
    b_hi, b_mid, b_lo = (v.astype(f32) for v in _split3(local * (-LOG2E)))
    ones3 = jnp.where(lane < 3, 1.0, 0.0).astype(bf16)

    for h in range(FOX_HEADS):
        q = _rms(head(COL_FQ, h), gq_ref[...] * (ATTN_SCALE * LOG2E))
        fq_ref[0, h] = jnp.concatenate([q.astype(bf16), ones3], axis=1)
    for h in range(FOX_HEADS):
        k = _rms(head(COL_FK, h), gk_ref[...])
        bias = jnp.where(lane == 0, b_hi[:, h:h + 1],
                         jnp.where(lane == 1, b_mid[:, h:h + 1],
                                   jnp.where(lane == 2, b_lo[:, h:h + 1], 0.0)))
        fk_ref[0, h] = jnp.concatenate([k.astype(bf16), bias.astype(bf16)], axis=1)

    gate_ref[0] = jax.nn.sigmoid(small)

    for h in range(NSA_HEADS):
        qn = _rms(head(COL_NQ, h), gn_ref[...] * (ATTN_SCALE * LOG2E))
        nqu_ref[0, h] = qn.astype(bf16)
        nqr_ref[0, h] = rope(qn).astype(bf16)
    onehot = jnp.where(lane == lax.shift_right_logical(pos, SEL_SHIFT), 1.0, 0.0).astype(bf16)
    for g in range(NSA_KV_HEADS):
        kc_ref[0, g] = head(COL_KC, g)
        vc_ref[0, g] = head(COL_VC, g)
        ks = rope(_rms(head(COL_KS, g), gs_ref[...])).astype(bf16)
        ks_ref[0, g] = jnp.concatenate([ks, onehot], axis=1)
        vst_ref[0, g, 0] = head(COL_VS, g).T.astype(bf16)
        kw_ref[0, g] = rope(_rms(head(COL_KW, g), gw_ref[...])).astype(bf16)
        vwt = head(COL_VW, g).T.astype(bf16)
        for c in range(tm // WIN_TILE):
            vw_ref[0, g, c] = vwt[:, c * WIN_TILE:(c + 1) * WIN_TILE]
    for h in range(FOX_HEADS):
        fvt_ref[0, h, 0] = head(COL_FV, h).T.astype(bf16)


def _prep(x2d, mod3, norm_g, wa, wb, ws, n_batch, seq_len, gq, gk, gn, gs, gw, b_forget_row, inv_freq):
    resident = lambda w: pl.BlockSpec(w.shape, lambda b, i: (0, 0), pipeline_mode=pl.Buffered(1))
    tm = KV_BLOCK
    per_b = seq_len // tm
    hshape = lambda n, w, dt: jax.ShapeDtypeStruct((n_batch, n, seq_len, w), dt)
    hspec = lambda n, w: pl.BlockSpec((1, n, tm, w), lambda b, i: (b, 0, i, 0))
    tshape = lambda n: jax.ShapeDtypeStruct((n_batch, n, per_b, HEAD_DIM, tm), bf16)
    tspec = lambda n: pl.BlockSpec((1, n, 1, HEAD_DIM, tm), lambda b, i: (b, 0, i, 0, 0))
    vec = pl.BlockSpec((1, LANES), lambda b, i: (0, 0))
    return pl.pallas_call(
        functools.partial(_prep_kernel, tm=tm),
        out_shape=[
            hshape(FOX_HEADS, 2 * HEAD_DIM, bf16), hshape(FOX_HEADS, 2 * HEAD_DIM, bf16), tshape(FOX_HEADS),
            jax.ShapeDtypeStruct((n_batch, per_b, 1, LANES), f32),
            hshape(NSA_HEADS, HEAD_DIM, bf16), hshape(NSA_HEADS, HEAD_DIM, bf16),
            hshape(NSA_KV_HEADS, HEAD_DIM, f32), hshape(NSA_KV_HEADS, HEAD_DIM, f32),
            hshape(NSA_KV_HEADS, HEAD_DIM + SEL_LANES, bf16), tshape(NSA_KV_HEADS),
            hshape(NSA_KV_HEADS, HEAD_DIM, bf16),
            jax.ShapeDtypeStruct((n_batch, NSA_KV_HEADS, seq_len // WIN_TILE, HEAD_DIM, WIN_TILE), bf16),
            jax.ShapeDtypeStruct((n_batch, seq_len, LANES), f32),
        ],
        grid=(n_batch, per_b),
        in_specs=[
            pl.BlockSpec((tm, D_MODEL), lambda b, i: (b * per_b + i, 0)),
            pl.BlockSpec((1, 6, D_MODEL), lambda b, i: (b, 0, 0)),
            pl.BlockSpec((1, D_MODEL), lambda b, i: (0, 0)),
            resident(wa), resident(wb), resident(ws),
            vec, vec, vec, vec, vec, vec, vec,
        ],
        out_specs=[
            hspec(FOX_HEADS, 2 * HEAD_DIM), hspec(FOX_HEADS, 2 * HEAD_DIM), tspec(FOX_HEADS),
            pl.BlockSpec((1, 1, 1, LANES), lambda b, i: (b, i, 0, 0)),
            hspec(NSA_HEADS, HEAD_DIM), hspec(NSA_HEADS, HEAD_DIM),
            hspec(NSA_KV_HEADS, HEAD_DIM), hspec(NSA_KV_HEADS, HEAD_DIM),
            hspec(NSA_KV_HEADS, HEAD_DIM + SEL_LANES), tspec(NSA_KV_HEADS),
            hspec(NSA_KV_HEADS, HEAD_DIM),
            pl.BlockSpec((1, NSA_KV_HEADS, tm // WIN_TILE, HEAD_DIM, WIN_TILE), lambda b, i: (b, 0, i, 0, 0)),
            pl.BlockSpec((1, tm, LANES), lambda b, i: (b, i, 0)),
        ],
        scratch_shapes=[pltpu.VMEM((8, LANES), f32), pltpu.VMEM((2, tm, LANES), f32)],
        compiler_params=pltpu.CompilerParams(dimension_semantics=("arbitrary", "arbitrary"),
                                             vmem_limit_bytes=VMEM_LIMIT_PREP),
        name="prep",
    )(x2d, mod3, norm_g, wa, wb, ws, gq, gk, gn, gs, gw, b_forget_row, inv_freq)


def _compress_kernel(x_ref, pe_ref, w1_ref, w2_ref, *rest, n_rows, do_norm):
    o_ref = rest[-1]
    a = jnp.zeros((n_rows, CMP_HIDDEN), f32)
    b = jnp.zeros((n_rows, CMP_HIDDEN), f32)
    for l in range(CMP_STRIDE):
        x = x_ref[0, 0, pl.ds(l, n_rows, stride=CMP_STRIDE), :]
        a = a + _dot((x + pe_ref[l:l + 1, :]).astype(bf16), w1_ref[l * HEAD_DIM:(l + 1) * HEAD_DIM, :])
        lb = CMP_STRIDE + l
        b = b + _dot((x + pe_ref[lb:lb + 1, :]).astype(bf16), w1_ref[lb * HEAD_DIM:(lb + 1) * HEAD_DIM, :])
    pre = a + pltpu.roll(b, n_rows - 1, 0)
    hid = pre * jax.nn.sigmoid(pre)
    out = _dot(hid.astype(bf16), w2_ref[...])
    if do_norm:
        out = _rms(out, rest[0][...])
    o_ref[0, 0] = out.astype(bf16)


def _compress(x4, pe, w1_bf16, w2_bf16, gain=None):
    n_batch, n_g, seq_len, _ = x4.shape
    n_rows = seq_len // CMP_STRIDE
    do_norm = gain is not None
    return pl.pallas_call(
        functools.partial(_compress_kernel, n_rows=n_rows, do_norm=do_norm),
        out_shape=jax.ShapeDtypeStruct((n_batch, n_g, n_rows, HEAD_DIM), bf16),
        grid=(n_batch, n_g),
        in_specs=[
            pl.BlockSpec((1, 1, seq_len, HEAD_DIM), lambda b, g: (b, g, 0, 0)),
            pl.BlockSpec((CMP_LEN, HEAD_DIM), lambda b, g: (0, 0)),
            pl.BlockSpec((CMP_LEN * HEAD_DIM, CMP_HIDDEN), lambda b, g: (0, 0)),
            pl.BlockSpec((CMP_HIDDEN, HEAD_DIM), lambda b, g: (0, 0)),
        ] + ([pl.BlockSpec((1, HEAD_DIM), lambda b, g: (0, 0))] if do_norm else []),
        out_specs=pl.BlockSpec((1, 1, n_rows, HEAD_DIM), lambda b, g: (b, g, 0, 0)),
        compiler_params=_params("arbitrary", "arbitrary"),
        name="compress",
    )(x4, pe, w1_bf16, w2_bf16, *([gain] if do_norm else []))


def _softmax_block(s, d, m_prev, l_prev):
    m_blk = jnp.max(s, axis=0, keepdims=True)
    m_new = jnp.maximum(m_prev, m_blk if d is None else m_blk + d)
    p = jnp.exp2(s - (m_new if d is None else m_new - d))
    alpha = jnp.exp2(m_prev - m_new)
    l_new = alpha * l_prev + jnp.sum(p, axis=0, keepdims=True)
    return m_new, l_new, alpha, p.astype(bf16)


class _Chain(NamedTuple):
    scores: Callable
    values: Callable
    offset: Callable
    diag_mask: Callable
    emit: Callable
    next_first: Callable
    s_buf: Any
    p_buf: Any


def _flash_sweep(n_full, chains, n_q, is_first):
    @pl.when(is_first)
    def _():
        for c in chains:
            c.s_buf[0] = c.scores(0)

    for c in chains:
        c.p_buf[1] = jnp.zeros(c.p_buf.shape[1:], c.p_buf.dtype)

    def accumulate(c, j, slot, alpha, acc):
        return alpha * acc + _dot(c.values(jnp.maximum(j, 0)), c.p_buf[slot])

    def step(j, cur, states):
        out = []
        for c, (m, l, acc, alpha_prev) in zip(chains, states):
            acc = accumulate(c, j - 1, 1 - cur, alpha_prev, acc)
            m, l, alpha, p = _softmax_block(c.s_buf[cur], c.offset(j), m, l)
            c.p_buf[cur] = p
            c.s_buf[1 - cur] = c.scores(j + 1)
            out.append((m, l, acc, alpha))
        return tuple(out)

    def finish(cur, states):
        nxt = [c.next_first() for c in chains]
        for c, s_next, (m, l, acc, alpha_prev) in zip(chains, nxt, states):
            if cur == 1:
                c.s_buf[0] = s_next
            acc = accumulate(c, n_full - 1, 1 - cur, alpha_prev, acc)
            m, l, alpha, p = _softmax_block(c.diag_mask(c.s_buf[cur]), None, m, l)
            if cur == 0:
                c.s_buf[0] = s_next
            acc = alpha * acc + _dot(c.values(n_full), p)
            c.emit(acc / l)

    init = tuple((jnp.full((1, n_q), -jnp.inf, f32), jnp.zeros((1, n_q), f32),
                  jnp.zeros((HEAD_DIM, n_q), f32), jnp.ones((1, n_q), f32)) for _ in chains)
    states = lax.fori_loop(0, n_full // 2, lambda jj, st: step(2 * jj + 1, 1, step(2 * jj, 0, st)), init)

    @pl.when(n_full % 2 == 0)
    def _():
        finish(0, states)

    @pl.when(n_full % 2 == 1)
    def _():
        finish(1, step(n_full - 1, 0, states))


def _fox_kernel(q_ref, qn_ref, k_ref, vt_ref, c_ref, o_ref, *bufs, tq, n_heads):
    hb = pl.program_id(1)
    i = pl.program_id(2)
    lane = lax.broadcasted_iota(jnp.int32, (1, LANES), 1)

    def diag_mask(s):
        kk = lax.broadcasted_iota(jnp.int32, (tq, tq), 0)
        qq = lax.broadcasted_iota(jnp.int32, (tq, tq), 1)
        return jnp.where(kk <= qq, s, -jnp.inf)

    def chain(u):
        qa = q_ref[0, u]

        def cin(j):
            return jnp.sum(jnp.where(lane == hb * n_heads + u, c_ref[0, j], 0.0), axis=-1, keepdims=True)

        ci = cin(i)

        def scores(j):
            return _dot_nt(k_ref[0, u, pl.ds(pl.multiple_of(j * tq, tq), tq), :], qa)

        def emit(out_t):
            o_ref[0, :, u * HEAD_DIM:(u + 1) * HEAD_DIM] = out_t.T.astype(bf16)

        def next_first():
            return _dot_nt(k_ref[0, u, 0:tq, :], qn_ref[0, u])

        return _Chain(scores, lambda j: vt_ref[0, u, j], lambda j: (ci - cin(j)) * LOG2E, diag_mask, emit,
                      next_first, bufs[2 * u], bufs[2 * u + 1])

    _flash_sweep(i, [chain(u) for u in range(n_heads)], tq, i == 0)


def _fox(fq, fk, fvt, cin):
    n_batch, n_h, seq_len, _ = fq.shape
    tq = KV_BLOCK
    nkb = seq_len // tq
    n_heads = 4
    return pl.pallas_call(
        functools.partial(_fox_kernel, tq=tq, n_heads=n_heads),
        out_shape=jax.ShapeDtypeStruct((n_batch, seq_len, n_h * HEAD_DIM), bf16),
        grid=(n_batch, n_h // n_heads, nkb),
        in_specs=[
            pl.BlockSpec((1, n_heads, tq, 2 * HEAD_DIM), lambda b, h, i: (b, h, i, 0)),
            pl.BlockSpec((1, n_heads, tq, 2 * HEAD_DIM), lambda b, h, i: (b, h, jnp.minimum(i + 1, nkb - 1), 0)),
            pl.BlockSpec((1, n_heads, seq_len, 2 * HEAD_DIM), lambda b, h, i: (b, h, 0, 0)),
            pl.BlockSpec((1, n_heads, nkb, HEAD_DIM, tq), lambda b, h, i: (b, h, 0, 0, 0)),
            pl.BlockSpec((1, nkb, 1, LANES), lambda b, h, i: (b, 0, 0, 0)),
        ],
        out_specs=pl.BlockSpec((1, tq, n_heads * HEAD_DIM), lambda b, h, i: (b, i, h)),
        scratch_shapes=[pltpu.VMEM((2, tq, tq), f32), pltpu.VMEM((2, tq, tq), bf16)] * n_heads,
        compiler_params=_params("arbitrary", "arbitrary", "arbitrary"),
        name="fox",
    )(fq, fq, fk, fvt, cin)


def _cmp_kernel(q_ref, k_ref, v_ref, gate_ref, ov_ref, o_ref, sel_ref, *, tq, n_rows, n_cmp, n_sel):
    i = pl.program_id(2)
    g = pl.program_id(1)
    hg = NSA_GROUP
    q = q_ref[0].reshape(hg * tq, HEAD_DIM)
    s = _dot_nt(q, k_ref[0, 0]).reshape(hg, tq, n_rows)
    t = i * tq + lax.broadcasted_iota(jnp.int32, (1, tq, 1), 1)
    n = lax.broadcasted_iota(jnp.int32, (1, 1, n_rows), 2)
    last = jnp.minimum(lax.shift_right_arithmetic(t - (CMP_LEN - 1), CMP_SHIFT), n_cmp - 1)
    s = jnp.where(n <= last, s, -jnp.inf)
    m = jnp.max(s, axis=-1, keepdims=True)
    m = jnp.where(m == -jnp.inf, 0.0, m)
    e = jnp.exp2(s - m)
    p = e * (1.0 / jnp.maximum(jnp.sum(e, axis=-1, keepdims=True), 1e-30))
    o = _dot(p.reshape(hg * tq, n_rows).astype(bf16), v_ref[0, 0])
    _write_gated(o_ref, gate_ref, o, g, 0, tq)

    lane = lax.broadcasted_iota(jnp.int32, (tq, LANES), 1)
    tq_pos = i * tq + lax.broadcasted_iota(jnp.int32, (tq, 1), 0)
    q_blk = lax.shift_right_logical(tq_pos, SEL_SHIFT)
    causal = lane <= q_blk
    n_top = min(SEL_TOPK, n_sel)

    @pl.when((i + 1) * tq <= n_top * SEL_BLOCK)
    def _():
        sel_ref[0, 0] = jnp.where(causal, 0.0, -MASK_BIG).astype(bf16)

    @pl.when((i + 1) * tq > n_top * SEL_BLOCK)
    def _():
        psum = p[0] + p[1] + p[2] + p[3]
        imp = _dot_f32_by_exact(psum, ov_ref[...])
        forced = (lane == 0) | (lane == q_blk) | (lane == q_blk - 1)
        key = jnp.where(forced, -2.0, jnp.where(causal, imp, -1.0))
        key = jnp.where(lane < n_sel, key, -3.0)
        lane_f = lane.astype(f32)
        sel = forced
        for _ in range(n_top - 3):
            mx = jnp.max(key, axis=-1, keepdims=True)
            first = jnp.min(jnp.where(key == mx, lane_f, float(LANES)), axis=-1, keepdims=True)
            pick = lane_f == first
            sel = sel | pick
            key = jnp.where(pick, -2.0, key)
        sel = ((q_blk < n_top) | sel) & causal
        sel_ref[0, 0] = jnp.where(sel, 0.0, -MASK_BIG).astype(bf16)


def _cmp(nqu, kcmp, vcmp, gates, n_cmp):
    n_batch, _, seq_len, _ = nqu.shape
    n_rows = kcmp.shape[2]
    n_sel = seq_len // SEL_BLOCK
    assert 3 <= n_sel <= SEL_LANES
    tq = 1024
    c0 = np.arange(n_rows) * CMP_STRIDE
    s0 = np.arange(SEL_LANES) * SEL_BLOCK
    overlap = np.clip(np.minimum(c0[:, None] + CMP_LEN, s0[None, :] + SEL_BLOCK)
                      - np.maximum(c0[:, None], s0[None, :]), 0, None).astype(np.float32) / CMP_LEN
    overlap[n_cmp:, :] = 0.0
    overlap[:, n_sel:] = 0.0
    return pl.pallas_call(
        functools.partial(_cmp_kernel, tq=tq, n_rows=n_rows, n_cmp=n_cmp, n_sel=n_sel),
        out_shape=[
            jax.ShapeDtypeStruct((n_batch, seq_len, NSA_W), bf16),
            jax.ShapeDtypeStruct((n_batch, NSA_KV_HEADS, seq_len, SEL_LANES), bf16),
        ],
        grid=(n_batch, NSA_KV_HEADS, seq_len // tq),
        in_specs=[
            pl.BlockSpec((1, NSA_GROUP, tq, HEAD_DIM), lambda b, g, i: (b, g, i, 0)),
            pl.BlockSpec((1, 1, n_rows, HEAD_DIM), lambda b, g, i: (b, g, 0, 0)),
            pl.BlockSpec((1, 1, n_rows, HEAD_DIM), lambda b, g, i: (b, g, 0, 0)),
            pl.BlockSpec((1, tq, LANES), lambda b, g, i: (b, i, 0)),
            pl.BlockSpec((n_rows, SEL_LANES), lambda b, g, i: (0, 0)),
        ],
        out_specs=[
            pl.BlockSpec((1, tq, NSA_GROUP * HEAD_DIM), lambda b, g, i: (b, i, g)),
            pl.BlockSpec((1, 1, tq, SEL_LANES), lambda b, g, i: (b, g, i, 0)),
        ],
        compiler_params=_params("arbitrary", "arbitrary", "arbitrary"),
        name="cmp",
    )(nqu, kcmp, vcmp, gates, jnp.asarray(overlap, dtype=bf16))


def _write_gated(o_ref, gate_ref, out, g, branch, tq):
    gates = gate_ref[0]
    lane = lax.broadcasted_iota(jnp.int32, (tq, LANES), 1)
    for h in range(NSA_GROUP):
        gcol = FOX_HEADS + (g * NSA_GROUP + h) * N_BRANCH + branch
        gh = jnp.sum(jnp.where(lane == gcol, gates, 0.0), axis=-1, keepdims=True)
        o_ref[0, :, h * HEAD_DIM:(h + 1) * HEAD_DIM] = (gh * out[h * tq:(h + 1) * tq]).astype(bf16)


def _slc_kernel(q_ref, sel_ref, qn_ref, seln_ref, k_ref, vt_ref, gate_ref, *rest, tq, tk, n_cast_w):
    o_ref = rest[n_cast_w]
    bufs = rest[2 * n_cast_w + 1:]
    for w32_ref, w16_ref in zip(rest[:n_cast_w], rest[n_cast_w + 1:2 * n_cast_w + 1]):
        w16_ref[...] = w32_ref[...].astype(bf16)
    i = pl.program_id(1)
    hg = NSA_GROUP
    n_q = hg * tq
    q0 = i * tq
    jd = q0 // tk

    def diag_mask(s):
        kp = jd * tk + lax.broadcasted_iota(jnp.int32, (tk, n_q), 0)
        t = q0 + (lax.broadcasted_iota(jnp.int32, (tk, n_q), 1) & (tq - 1))
        return jnp.where(kp <= t, s, -jnp.inf)

    def chain(g):
        def q_aug(qr, sr):
            q = qr[0, g * hg:(g + 1) * hg].reshape(n_q, HEAD_DIM)
            return jnp.concatenate([q, jnp.concatenate([sr[0, g]] * hg, axis=0)], axis=1)

        qa = q_aug(q_ref, sel_ref)

        def scores(j):
            return _dot_nt(k_ref[0, g, pl.ds(pl.multiple_of(j * tk, tk), tk), :], qa)

        def next_first():
            return _dot_nt(k_ref[0, g, 0:tk, :], q_aug(qn_ref, seln_ref))

        def emit(out):
            gates = gate_ref[0]
            lane = lax.broadcasted_iota(jnp.int32, (tq, LANES), 1)
            for h in range(hg):
                head = g * hg + h
                gh = jnp.sum(jnp.where(lane == FOX_HEADS + head * N_BRANCH + 1, gates, 0.0),
                             axis=-1, keepdims=True)
                o_ref[0, :, head * HEAD_DIM:(head + 1) * HEAD_DIM] = (
                    gh * out[:, h * tq:(h + 1) * tq].T).astype(bf16)

        return _Chain(scores, lambda j: vt_ref[0, g, j], lambda j: None, diag_mask, emit, next_first,
                      bufs[2 * g], bufs[2 * g + 1])

    _flash_sweep(jd, [chain(g) for g in range(NSA_KV_HEADS)], n_q, i == 0)


def _slc(nqr, selneg, kaug, vst, gates, cast_weights):
    n_batch, _, seq_len, _ = nqr.shape
    tq, tk = 256, KV_BLOCK
    nkb = seq_len // tk
    assert tq & (tq - 1) == 0
    n_q = NSA_GROUP * tq
    n_i = seq_len // tq
    cast_in, cast_out, cast_shapes = _cast_side_job(cast_weights, n_batch * n_i, lambda b, i: b * n_i + i)
    return pl.pallas_call(
        functools.partial(_slc_kernel, tq=tq, tk=tk, n_cast_w=len(cast_weights)),
        out_shape=[jax.ShapeDtypeStruct((n_batch, seq_len, NSA_W), bf16)] + cast_shapes,
        grid=(n_batch, n_i),
        in_specs=[
            pl.BlockSpec((1, NSA_HEADS, tq, HEAD_DIM), lambda b, i: (b, 0, i, 0)),
            pl.BlockSpec((1, NSA_KV_HEADS, tq, SEL_LANES), lambda b, i: (b, 0, i, 0)),
            pl.BlockSpec((1, NSA_HEADS, tq, HEAD_DIM), lambda b, i: (b, 0, jnp.minimum(i + 1, n_i - 1), 0)),
            pl.BlockSpec((1, NSA_KV_HEADS, tq, SEL_LANES), lambda b, i: (b, 0, jnp.minimum(i + 1, n_i - 1), 0)),
            pl.BlockSpec((1, NSA_KV_HEADS, seq_len, HEAD_DIM + SEL_LANES), lambda b, i: (b, 0, 0, 0)),
            pl.BlockSpec((1, NSA_KV_HEADS, nkb, HEAD_DIM, tk), lambda b, i: (b, 0, 0, 0, 0)),
            pl.BlockSpec((1, tq, LANES), lambda b, i: (b, i, 0)),
        ] + cast_in,
        out_specs=[pl.BlockSpec((1, tq, NSA_W), lambda b, i: (b, i, 0))] + cast_out,
        scratch_shapes=[pltpu.VMEM((2, tk, n_q), f32), pltpu.VMEM((2, tk, n_q), bf16)] * NSA_KV_HEADS,
        compiler_params=_params("arbitrary", "arbitrary"),
        name="slc",
    )(nqr, selneg, nqr, selneg, kaug, vst, gates, *cast_weights)


def _win_kernel(q_ref, k_ref, vt_ref, gate_ref, o_ref, *, tq, n_sub):
    i = pl.program_id(2)
    g = pl.program_id(1)
    hg = NSA_GROUP
    span = WINDOW + tq
    q0s = [(i * n_sub + u) * tq for u in range(n_sub)]
    k0s = [pl.multiple_of(jnp.maximum(q0 - WINDOW, 0), tq) for q0 in q0s]
    scores = []
    for u in range(n_sub):
        q = q_ref[0, :, u * tq:(u + 1) * tq, :].reshape(hg * tq, HEAD_DIM)
        scores.append(_dot_nt(k_ref[0, 0, pl.ds(k0s[u], span), :], q))
    probs, denoms = [], []
    for u in range(n_sub):
        kp = k0s[u] + lax.broadcasted_iota(jnp.int32, (span, tq), 0)
        t = q0s[u] + lax.broadcasted_iota(jnp.int32, (span, tq), 1)
        diff = t - kp
        bias = jnp.where((diff >= 0) & (diff < WINDOW), 0.0, -jnp.inf)
        s = jnp.concatenate([scores[u][:, h * tq:(h + 1) * tq] + bias for h in range(hg)], axis=1)
        e = jnp.exp2(s - jnp.max(s, axis=0, keepdims=True))
        denoms.append(jnp.sum(e, axis=0, keepdims=True))
        probs.append(e.astype(bf16))
    gates = gate_ref[0]
    lane = lax.broadcasted_iota(jnp.int32, (tq, LANES), 1)
    for u in range(n_sub):
        jb = k0s[u] // tq
        vt = jnp.concatenate([vt_ref[0, 0, jb + c] for c in range(span // tq)], axis=1)
        out = _dot(vt, probs[u]) / denoms[u]
        for h in range(hg):
            gcol = FOX_HEADS + (g * hg + h) * N_BRANCH + 2
            gh = jnp.sum(jnp.where(lane == gcol, gates[u * tq:(u + 1) * tq], 0.0), axis=-1, keepdims=True)
            o_ref[0, u * tq:(u + 1) * tq, h * HEAD_DIM:(h + 1) * HEAD_DIM] = (
                gh * out[:, h * tq:(h + 1) * tq].T).astype(bf16)


def _win(nqr, kw, vwt, gates):
    n_batch, _, seq_len, _ = nqr.shape
    tq, n_sub = WIN_TILE, 8
    assert seq_len >= WINDOW + tq and WINDOW % tq == 0
    return pl.pallas_call(
        functools.partial(_win_kernel, tq=tq, n_sub=n_sub),
        out_shape=jax.ShapeDtypeStruct((n_batch, seq_len, NSA_W), bf16),
        grid=(n_batch, NSA_KV_HEADS, seq_len // (tq * n_sub)),
        in_specs=[
            pl.BlockSpec((1, NSA_GROUP, tq * n_sub, HEAD_DIM), lambda b, g, i: (b, g, i, 0)),
            pl.BlockSpec((1, 1, seq_len, HEAD_DIM), lambda b, g, i: (b, g, 0, 0)),
            pl.BlockSpec((1, 1, seq_len // tq, HEAD_DIM, tq), lambda b, g, i: (b, g, 0, 0, 0)),
            pl.BlockSpec((1, tq * n_sub, LANES), lambda b, g, i: (b, i, 0)),
        ],
        out_specs=pl.BlockSpec((1, tq * n_sub, NSA_GROUP * HEAD_DIM), lambda b, g, i: (b, i, g)),
        compiler_params=_params("arbitrary", "arbitrary", "arbitrary"),
        name="win",
    )(nqr, kw, vwt, gates)


def _outproj_kernel(fox_ref, c_ref, s_ref, w_ref, x_ref, mod_ref, wo_ref, o_ref):
    mix = _dot(fox_ref[...], wo_ref[0:FOX_W, :])
    nsa = c_ref[...].astype(f32) + s_ref[...].astype(f32) + w_ref[...].astype(f32)
    mix = mix + _dot(nsa.astype(bf16), wo_ref[FOX_W:FOX_W + NSA_W, :])
    o_ref[...] = x_ref[...] + mod_ref[0][2:3] * mix


def _outproj(ofox, ocmp, oslc, owin, x2d, mod3, wo_bf16, seq_len):
    m_rows = x2d.shape[0]
    tm = 512
    per_b = seq_len // tm
    half = pl.BlockSpec((tm, FOX_W), lambda i: (i, 0))
    return pl.pallas_call(
        _outproj_kernel,
        out_shape=jax.ShapeDtypeStruct((m_rows, D_MODEL), f32),
        grid=(m_rows // tm,),
        in_specs=[
            half, half, half, half,
            pl.BlockSpec((tm, D_MODEL), lambda i: (i, 0)),
            pl.BlockSpec((1, 6, D_MODEL), lambda i: (i // per_b, 0, 0)),
            pl.BlockSpec((D_MODEL, D_MODEL), lambda i: (0, 0)),
        ],
        out_specs=pl.BlockSpec((tm, D_MODEL), lambda i: (i, 0)),
        compiler_params=_params("arbitrary"),
        name="outproj",
    )(ofox, ocmp, oslc, owin, x2d, mod3, wo_bf16)


def _mlp_kernel(x_ref, mod_ref, g_ref, wu_ref, wd_ref, o_ref, h_ref, acc_ref):
    f = pl.program_id(1)

    @pl.when(f == 0)
    def _():
        md = mod_ref[0]
        y = _rms(x_ref[...], g_ref[...])
        h_ref[...] = (y * (1.0 + md[4:5]) + md[3:4]).astype(bf16)
        acc_ref[...] = jnp.zeros_like(acc_ref)

    u = jnp.maximum(_dot(h_ref[...], wu_ref[...]), 0.0)
    acc_ref[...] += _dot((u * u).astype(bf16), wd_ref[...])

    @pl.when(f == pl.num_programs(1) - 1)
    def _():
        o_ref[...] = x_ref[...] + mod_ref[0][5:6] * acc_ref[...]


def _mlp(x2d, mod3, norm_g, wu_bf16, wd_bf16, seq_len):
    m_rows = x2d.shape[0]
    tm, tf = 512, 1024
    per_b = seq_len // tm
    return pl.pallas_call(
        _mlp_kernel,
        out_shape=jax.ShapeDtypeStruct((m_rows, D_MODEL), f32),
        grid=(m_rows // tm, D_FF // tf),
        in_specs=[
            pl.BlockSpec((tm, D_MODEL), lambda i, f: (i, 0)),
            pl.BlockSpec((1, 6, D_MODEL), lambda i, f: (i // per_b, 0, 0)),
            pl.BlockSpec((1, D_MODEL), lambda i, f: (0, 0)),
            pl.BlockSpec((D_MODEL, tf), lambda i, f: (0, f)),
            pl.BlockSpec((tf, D_MODEL), lambda i, f: (f, 0)),
        ],
        out_specs=pl.BlockSpec((tm, D_MODEL), lambda i, f: (i, 0)),
        scratch_shapes=[pltpu.VMEM((tm, D_MODEL), bf16), pltpu.VMEM((tm, D_MODEL), f32)],
        compiler_params=_params("arbitrary", "arbitrary"),
        name="mlp",
    )(x2d, mod3, norm_g, wu_bf16, wd_bf16)


def _layer(x, c, w_ada, b_ada, norm1_g, w_in, b_forget, fox_q_norm, fox_k_norm, nsa_q_norm,
           cmp_k_norm, slc_k_norm, win_k_norm, cmp_pe_k, cmp_w1_k, cmp_w2_k, cmp_pe_v, cmp_w1_v,
           cmp_w2_v, w_out, norm2_g, w_up, w_down):
    n_batch, seq_len, _ = x.shape
    n_cmp = (seq_len - CMP_LEN) // CMP_STRIDE + 1
    row = lambda v: v.reshape(1, -1)

    half = HEAD_DIM // 2
    inv_freq = ROPE_THETA ** (-jnp.arange(half, dtype=f32) / half)
    inv_freq = jnp.concatenate([inv_freq, inv_freq]).reshape(1, HEAD_DIM)

    w_a, w_b, w_s = _repack_w_in(w_in.T)

    mod3 = _ada(c, w_ada, b_ada).reshape(n_batch, 6, D_MODEL)
    x2d = x.reshape(n_batch * seq_len, D_MODEL)
    (fq, fk, fvt, cin, nqu, nqr, kc, vc, kaug, vst, kw, vw, gates) = _prep(
        x2d, mod3, row(norm1_g), w_a, w_b, w_s, n_batch, seq_len, row(fox_q_norm), row(fox_k_norm), row(nsa_q_norm), row(slc_k_norm),
        row(win_k_norm), jnp.pad(b_forget, (0, LANES - FOX_HEADS)).reshape(1, LANES), inv_freq)
    kcmp = _compress(kc, cmp_pe_k, cmp_w1_k.astype(bf16), cmp_w2_k.astype(bf16), row(cmp_k_norm))
    vcmp = _compress(vc, cmp_pe_v, cmp_w1_v.astype(bf16), cmp_w2_v.astype(bf16))
    ofox = _fox(fq, fk, fvt, cin)
    ocmp, selneg = _cmp(nqu, kcmp, vcmp, gates, n_cmp)
    oslc, w_up16, w_down16, w_out16 = _slc(nqr, selneg, kaug, vst, gates, [w_up, w_down, w_out])
    owin = _win(nqr, kw, vw, gates)
    x1 = _outproj(ofox.reshape(-1, FOX_W), ocmp.reshape(-1, NSA_W), oslc.reshape(-1, NSA_W),
                  owin.reshape(-1, NSA_W), x2d, mod3, w_out16, seq_len)
    x2 = _mlp(x1, mod3, row(norm2_g), w_up16, w_down16, seq_len)
    return x2.reshape(n_batch, seq_len, D_MODEL)


def kernel(x, c, w_ada, b_ada, norm1_g, w_in, b_forget, fox_q_norm, fox_k_norm, nsa_q_norm, cmp_k_norm,
           slc_k_norm, win_k_norm, cmp_pe_k, cmp_w1_k, cmp_w2_k, cmp_pe_v, cmp_w1_v, cmp_w2_v, w_out,
           norm2_g, w_up, w_down):
    depth = w_ada.shape[0]
    for l in range(depth):
        x = _layer(x, c, w_ada[l], b_ada[l], norm1_g[l], w_in[l], b_forget[l], fox_q_norm[l], fox_k_norm[l],
                   nsa_q_norm[l], cmp_k_norm[l], slc_k_norm[l], win_k_norm[l], cmp_pe_k[l], cmp_w1_k[l],
                   cmp_w2_k[l], cmp_pe_v[l], cmp_w1_v[l], cmp_w2_v[l], w_out[l], norm2_g[l], w_up[l],
                   w_down[l])
    return x
```

```python
import functools
import math
from typing import Any, Callable, NamedTuple

import numpy as np
import jax
import jax.numpy as jnp
from jax import lax
from jax.experimental import pallas as pl
from jax.experimental.pallas import tpu as pltpu

D_MODEL = 2048
HEAD_DIM = 128
FOX_HEADS = 8
NSA_HEADS = 8
NSA_KV_HEADS = 2
NSA_GROUP = NSA_HEADS // NSA_KV_HEADS
N_BRANCH = 3
D_FF = 4 * D_MODEL
ROPE_THETA = 10000.0
CMP_LEN = 32
CMP_STRIDE = 16
CMP_SHIFT = 4
CMP_HIDDEN = 2 * HEAD_DIM
SEL_BLOCK = 64
SEL_SHIFT = 6
SEL_TOPK = 16
WINDOW = 512
NORM_EPS = 1e-6
ATTN_SCALE = HEAD_DIM ** -0.5
FOX_W = FOX_HEADS * HEAD_DIM
NSA_W = NSA_HEADS * HEAD_DIM
KV_W = NSA_KV_HEADS * HEAD_DIM

LANES = 128
F32_SUBLANES = 8
BF16_SUBLANES = 16
SEL_LANES = LANES
MASK_BIG = 1e30
KV_BLOCK = 512
WIN_TILE = 128
LOG2E = math.log2(math.e)

COL_FQ = 0
COL_FK = COL_FQ + FOX_W
COL_FV = COL_FK + FOX_W
COL_NQ = COL_FV + FOX_W
COL_KC = COL_NQ + NSA_W
COL_VC = COL_KC + KV_W
COL_KS = COL_VC + KV_W
COL_VS = COL_KS + KV_W
COL_KW = COL_VS + KV_W
COL_VW = COL_KW + KV_W
COL_SMALL = COL_VW + KV_W
W_IN_Z0 = 3 * FOX_W
W_IN_NQ0 = W_IN_Z0 + FOX_HEADS
W_IN_GZ0 = W_IN_NQ0 + NSA_W + 6 * KV_W

VMEM_LIMIT = 56 * 1024 * 1024
VMEM_LIMIT_PREP = 60 * 1024 * 1024

f32 = jnp.float32
bf16 = jnp.bfloat16


def _params(*sem):
    return pltpu.CompilerParams(dimension_semantics=sem, vmem_limit_bytes=VMEM_LIMIT)


def _dot_nt(a, b):
    return lax.dot_general(a, b, (((1,), (1,)), ((), ())), preferred_element_type=f32)


def _dot(a, b):
    return jnp.dot(a, b, preferred_element_type=f32)


def _split3(x):
    hi = x.astype(bf16)
    r1 = x - hi.astype(f32)
    mid = r1.astype(bf16)
    lo = (r1 - mid.astype(f32)).astype(bf16)
    return hi, mid, lo


def _dot_f32_by_exact(x, w_bf16):
    hi, mid, lo = _split3(x)
    return _dot(hi, w_bf16) + (_dot(mid, w_bf16) + _dot(lo, w_bf16))


def _rms(x, gain):
    ms = jnp.mean(x * x, axis=-1, keepdims=True)
    return x * lax.rsqrt(ms + NORM_EPS) * gain


def _ada_kernel(ct_ref, w_ref, b_ref, o_ref, *, n_batch, k_chunk):
    ct = ct_ref[...]
    act = ct * jax.nn.sigmoid(ct)
    rows = []
    for b in range(n_batch):
        col = act[:, b:b + 1]
        acc = b_ref[...]
        for k0 in range(0, D_MODEL, k_chunk):
            acc = acc + jnp.sum(w_ref[k0:k0 + k_chunk, :] * col[k0:k0 + k_chunk], axis=0, keepdims=True)
        rows.append(acc)
    o_ref[...] = jnp.concatenate(rows, axis=0)


def _ada(c, w_ada, b_ada):
    n_batch = c.shape[0]
    n_out = w_ada.shape[1]
    tn = 1024
    return pl.pallas_call(
        functools.partial(_ada_kernel, n_batch=n_batch, k_chunk=256),
        out_shape=jax.ShapeDtypeStruct((n_batch, n_out), f32),
        grid=(n_out // tn,),
        in_specs=[
            pl.BlockSpec((D_MODEL, n_batch), lambda j: (0, 0)),
            pl.BlockSpec((D_MODEL, tn), lambda j: (0, j)),
            pl.BlockSpec((1, tn), lambda j: (0, j)),
        ],
        out_specs=pl.BlockSpec((n_batch, tn), lambda j: (0, j)),
        compiler_params=_params("arbitrary"),
        name="ada",
    )(c.T, w_ada, b_ada.reshape(1, n_out))


def _repack_kernel(a_ref, *rest, n_part):
    part_refs, (z_ref, gz_ref, wa_ref, wb_ref, ws_ref) = rest[:n_part], rest[n_part:]
    wa_ref[...] = a_ref[...].astype(bf16)
    wb_ref[...] = jnp.concatenate([r[...] for r in part_refs], axis=0).astype(bf16)
    pad = jnp.zeros((LANES - z_ref.shape[0] - gz_ref.shape[0], D_MODEL), f32)
    ws_ref[...] = jnp.concatenate([z_ref[...], gz_ref[...], pad], axis=0).astype(bf16)


def _repack_w_in(w_t):
    n_steps = 16
    ra, rb = W_IN_Z0 // n_steps, (W_IN_GZ0 - W_IN_NQ0) // n_steps
    hb = math.gcd(W_IN_NQ0, rb)
    n_part = rb // hb
    n_z, n_gz = W_IN_NQ0 - W_IN_Z0, w_t.shape[0] - W_IN_GZ0
    assert W_IN_Z0 % n_steps == 0 and (W_IN_GZ0 - W_IN_NQ0) % n_steps == 0 and ra % BF16_SUBLANES == 0 and rb % BF16_SUBLANES == 0
    assert hb % F32_SUBLANES == 0 and W_IN_Z0 % n_z == 0 and W_IN_GZ0 % n_gz == 0
    b_off = W_IN_NQ0 // hb
    part = lambda k: pl.BlockSpec((hb, D_MODEL), lambda j: (b_off + n_part * j + k, 0))
    return pl.pallas_call(
        functools.partial(_repack_kernel, n_part=n_part),
        out_shape=[jax.ShapeDtypeStruct((W_IN_Z0, D_MODEL), bf16),
                   jax.ShapeDtypeStruct((W_IN_GZ0 - W_IN_NQ0, D_MODEL), bf16),
                   jax.ShapeDtypeStruct((LANES, D_MODEL), bf16)],
        grid=(n_steps,),
        in_specs=[pl.BlockSpec((ra, D_MODEL), lambda j: (j, 0))] + [part(k) for k in range(n_part)] + [
            pl.BlockSpec((n_z, D_MODEL), lambda j: (W_IN_Z0 // n_z, 0)),
            pl.BlockSpec((n_gz, D_MODEL), lambda j: (W_IN_GZ0 // n_gz, 0)),
        ],
        out_specs=[pl.BlockSpec((ra, D_MODEL), lambda j: (j, 0)),
                   pl.BlockSpec((rb, D_MODEL), lambda j: (j, 0)),
                   pl.BlockSpec((LANES, D_MODEL), lambda j: (0, 0))],
        compiler_params=_params("arbitrary"),
        name="repack",
    )(*([w_t] * (n_part + 3)))


def _cast_side_job(weights, n_steps, step_of):
    n_cast = max(c for c in range(1, n_steps + 1) if all(w.shape[0] % (BF16_SUBLANES * c) == 0 for w in weights))
    idx = lambda *ids: (jnp.minimum(step_of(*ids), n_cast - 1), 0)
    specs = [pl.BlockSpec((w.shape[0] // n_cast, w.shape[1]), idx) for w in weights]
    return specs, specs, [jax.ShapeDtypeStruct(w.shape, bf16) for w in weights]


def _prep_kernel(x_ref, mod_ref, g1_ref, wa_ref, wb_ref, ws_ref,
                 gq_ref, gk_ref, gn_ref, gs_ref, gw_ref, bf_ref, inv_ref,
                 fq_ref, fk_ref, fvt_ref, cin_ref, nqu_ref, nqr_ref, kc_ref, vc_ref,
                 ks_ref, vst_ref, kw_ref, vw_ref, gate_ref, carry_ref, rot_ref, *, tm):
    i = pl.program_id(1)

    @pl.when(i == 0)
    def _():
        carry_ref[...] = jnp.zeros_like(carry_ref)

    md = mod_ref[0]
    hn = (_rms(x_ref[...], g1_ref[...]) * (1.0 + md[1:2]) + md[0:1]).astype(bf16)
    groups = {}

    def head(col, h):
        base, w_ref, r0 = next((b, w, r) for b, w, r in (
            (COL_KC, wb_ref, NSA_W), (COL_NQ, wb_ref, 0), (COL_FV, wa_ref, 2 * FOX_W),
            (COL_FK, wa_ref, FOX_W), (COL_FQ, wa_ref, 0)) if col >= b)
        if base not in groups:
            n = (COL_SMALL - COL_KC) if base == COL_KC else FOX_W
            groups[base] = _dot_nt(hn, w_ref[r0:r0 + n, :])
        c0 = col - base + h * HEAD_DIM
        return groups[base][:, c0:c0 + HEAD_DIM]

    row = lax.broadcasted_iota(jnp.int32, (tm, LANES), 0)
    lane = lax.broadcasted_iota(jnp.int32, (tm, LANES), 1)
    pos = i * tm + row

    @pl.when((pl.program_id(0) == 0) & (i == 0))
    def _():
        ang_row = row.astype(f32) * inv_ref[...]
        rot_ref[0] = jnp.cos(ang_row)
        rot_ref[1] = jnp.sin(ang_row)

    ang0 = (i * tm).astype(f32) * inv_ref[...]
    cos0, sin0 = jnp.cos(ang0), jnp.sin(ang0)
    cos = cos0 * rot_ref[0] - sin0 * rot_ref[1]
    sin = sin0 * rot_ref[0] + cos0 * rot_ref[1]
    sin_signed = jnp.where(lane < HEAD_DIM // 2, -sin, sin)

    def rope(x):
        return x * cos + pltpu.roll(x, HEAD_DIM // 2, 1) * sin_signed

    small = _dot_nt(hn, ws_ref[...])
    z = small + bf_ref[...]
    logf = jnp.minimum(z, 0.0) - jnp.log1p(jnp.exp(-jnp.abs(z)))
    t_idx = lax.broadcasted_iota(jnp.int32, (tm, tm), 0)
    s_idx = lax.broadcasted_iota(jnp.int32, (tm, tm), 1)
    tri = jnp.where(s_idx <= t_idx, 1.0, 0.0).astype(bf16)
    hi, mid, lo = _split3(logf)
    local = _dot(tri, hi) + (_dot(tri, mid) + _dot(tri, lo))
    cin_ref[0, 0] = carry_ref[0:1, :]
    carry_ref[...] = carry_ref[...] + local[tm - 1:tm, :]
    b_hi, b_mid, b_lo = (v.astype(f32) for v in _split3(local * (-LOG2E)))
    ones3 = jnp.where(lane < 3, 1.0, 0.0).astype(bf16)

    for h in range(FOX_HEADS):
        q = _rms(head(COL_FQ, h), gq_ref[...] * (ATTN_SCALE * LOG2E))
        fq_ref[0, h] = jnp.concatenate([q.astype(bf16), ones3], axis=1)
    for h in range(FOX_HEADS):
        k = _rms(head(COL_FK, h), gk_ref[...])
        bias = jnp.where(lane == 0, b_hi[:, h:h + 1],
                         jnp.where(lane == 1, b_mid[:, h:h + 1],
                                   jnp.where(lane == 2, b_lo[:, h:h + 1], 0.0)))
        fk_ref[0, h] = jnp.concatenate([k.astype(bf16), bias.astype(bf16)], axis=1)

    gate_ref[0] = jax.nn.sigmoid(small)

    for h in range(NSA_HEADS):
        qn = _rms(head(COL_NQ, h), gn_ref[...] * (ATTN_SCALE * LOG2E))
        nqu_ref[0, h] = qn.astype(bf16)
        nqr_ref[0, h] = rope(qn).astype(bf16)
    onehot = jnp.where(lane == lax.shift_right_logical(pos, SEL_SHIFT), 1.0, 0.0).astype(bf16)
    for g in range(NSA_KV_HEADS):
        kc_ref[0, g] = head(COL_KC, g)
        vc_ref[0, g] = head(COL_VC, g)
        ks = rope(_rms(head(COL_KS, g), gs_ref[...])).astype(bf16)
        ks_ref[0, g] = jnp.concatenate([ks, onehot], axis=1)
        vst_ref[0, g, 0] = head(COL_VS, g).T.astype(bf16)
        kw_ref[0, g] = rope(_rms(head(COL_KW, g), gw_ref[...])).astype(bf16)
        vwt = head(COL_VW, g).T.astype(bf16)
        for c in range(tm // WIN_TILE):
            vw_ref[0, g, c] = vwt[:, c * WIN_TILE:(c + 1) * WIN_TILE]
    for h in range(FOX_HEADS):
        fvt_ref[0, h, 0] = head(COL_FV, h).T.astype(bf16)


def _prep(x2d, mod3, norm_g, wa, wb, ws, n_batch, seq_len, gq, gk, gn, gs, gw, b_forget_row, inv_freq):
    resident = lambda w: pl.BlockSpec(w.shape, lambda b, i: (0, 0), pipeline_mode=pl.Buffered(1))
    tm = KV_BLOCK
    per_b = seq_len // tm
    hshape = lambda n, w, dt: jax.ShapeDtypeStruct((n_batch, n, seq_len, w), dt)
    hspec = lambda n, w: pl.BlockSpec((1, n, tm, w), lambda b, i: (b, 0, i, 0))
    tshape = lambda n: jax.ShapeDtypeStruct((n_batch, n, per_b, HEAD_DIM, tm), bf16)
    tspec = lambda n: pl.BlockSpec((1, n, 1, HEAD_DIM, tm), lambda b, i: (b, 0, i, 0, 0))
    vec = pl.BlockSpec((1, LANES), lambda b, i: (0, 0))
    return pl.pallas_call(
        functools.partial(_prep_kernel, tm=tm),
        out_shape=[
            hshape(FOX_HEADS, 2 * HEAD_DIM, bf16), hshape(FOX_HEADS, 2 * HEAD_DIM, bf16), tshape(FOX_HEADS),
            jax.ShapeDtypeStruct((n_batch, per_b, 1, LANES), f32),
            hshape(NSA_HEADS, HEAD_DIM, bf16), hshape(NSA_HEADS, HEAD_DIM, bf16),
            hshape(NSA_KV_HEADS, HEAD_DIM, f32), hshape(NSA_KV_HEADS, HEAD_DIM, f32),
            hshape(NSA_KV_HEADS, HEAD_DIM + SEL_LANES, bf16), tshape(NSA_KV_HEADS),
            hshape(NSA_KV_HEADS, HEAD_DIM, bf16),
            jax.ShapeDtypeStruct((n_batch, NSA_KV_HEADS, seq_len // WIN_TILE, HEAD_DIM, WIN_TILE), bf16),
            jax.ShapeDtypeStruct((n_batch, seq_len, LANES), f32),
        ],
        grid=(n_batch, per_b),
        in_specs=[
            pl.BlockSpec((tm, D_MODEL), lambda b, i: (b * per_b + i, 0)),
            pl.BlockSpec((1, 6, D_MODEL), lambda b, i: (b, 0, 0)),
            pl.BlockSpec((1, D_MODEL), lambda b, i: (0, 0)),
            resident(wa), resident(wb), resident(ws),
            vec, vec, vec, vec, vec, vec, vec,
        ],
        out_specs=[
            hspec(FOX_HEADS, 2 * HEAD_DIM), hspec(FOX_HEADS, 2 * HEAD_DIM), tspec(FOX_HEADS),
            pl.BlockSpec((1, 1, 1, LANES), lambda b, i: (b, i, 0, 0)),
            hspec(NSA_HEADS, HEAD_DIM), hspec(NSA_HEADS, HEAD_DIM),
            hspec(NSA_KV_HEADS, HEAD_DIM), hspec(NSA_KV_HEADS, HEAD_DIM),
            hspec(NSA_KV_HEADS, HEAD_DIM + SEL_LANES), tspec(NSA_KV_HEADS),
            hspec(NSA_KV_HEADS, HEAD_DIM),
            pl.BlockSpec((1, NSA_KV_HEADS, tm // WIN_TILE, HEAD_DIM, WIN_TILE), lambda b, i: (b, 0, i, 0, 0)),
            pl.BlockSpec((1, tm, LANES), lambda b, i: (b, i, 0)),
        ],
        scratch_shapes=[pltpu.VMEM((8, LANES), f32), pltpu.VMEM((2, tm, LANES), f32)],
        compiler_params=pltpu.CompilerParams(dimension_semantics=("arbitrary", "arbitrary"),
                                             vmem_limit_bytes=VMEM_LIMIT_PREP),
        name="prep",
    )(x2d, mod3, norm_g, wa, wb, ws, gq, gk, gn, gs, gw, b_forget_row, inv_freq)


def _compress_kernel(x_ref, pe_ref, w1_ref, w2_ref, *rest, n_rows, do_norm):
    o_ref = rest[-1]
    a = jnp.zeros((n_rows, CMP_HIDDEN), f32)
    b = jnp.zeros((n_rows, CMP_HIDDEN), f32)
    for l in range(CMP_STRIDE):
        x = x_ref[0, 0, pl.ds(l, n_rows, stride=CMP_STRIDE), :]
        a = a + _dot((x + pe_ref[l:l + 1, :]).astype(bf16), w1_ref[l * HEAD_DIM:(l + 1) * HEAD_DIM, :])
        lb = CMP_STRIDE + l
        b = b + _dot((x + pe_ref[lb:lb + 1, :]).astype(bf16), w1_ref[lb * HEAD_DIM:(lb + 1) * HEAD_DIM, :])
    pre = a + pltpu.roll(b, n_rows - 1, 0)
    hid = pre * jax.nn.sigmoid(pre)
    out = _dot(hid.astype(bf16), w2_ref[...])
    if do_norm:
        out = _rms(out, rest[0][...])
    o_ref[0, 0] = out.astype(bf16)


def _compress(x4, pe, w1_bf16, w2_bf16, gain=None):
    n_batch, n_g, seq_len, _ = x4.shape
    n_rows = seq_len // CMP_STRIDE
    do_norm = gain is not None
    return pl.pallas_call(
        functools.partial(_compress_kernel, n_rows=n_rows, do_norm=do_norm),
        out_shape=jax.ShapeDtypeStruct((n_batch, n_g, n_rows, HEAD_DIM), bf16),
        grid=(n_batch, n_g),
        in_specs=[
            pl.BlockSpec((1, 1, seq_len, HEAD_DIM), lambda b, g: (b, g, 0, 0)),
            pl.BlockSpec((CMP_LEN, HEAD_DIM), lambda b, g: (0, 0)),
            pl.BlockSpec((CMP_LEN * HEAD_DIM, CMP_HIDDEN), lambda b, g: (0, 0)),
            pl.BlockSpec((CMP_HIDDEN, HEAD_DIM), lambda b, g: (0, 0)),
        ] + ([pl.BlockSpec((1, HEAD_DIM), lambda b, g: (0, 0))] if do_norm else []),
        out_specs=pl.BlockSpec((1, 1, n_rows, HEAD_DIM), lambda b, g: (b, g, 0, 0)),
        compiler_params=_params("arbitrary", "arbitrary"),
        name="compress",
    )(x4, pe, w1_bf16, w2_bf16, *([gain] if do_norm else []))


def _softmax_block(s, d, m_prev, l_prev):
    m_blk = jnp.max(s, axis=0, keepdims=True)
    m_new = jnp.maximum(m_prev, m_blk if d is None else m_blk + d)
    p = jnp.exp2(s - (m_new if d is None else m_new - d))
    alpha = jnp.exp2(m_prev - m_new)
    l_new = alpha * l_prev + jnp.sum(p, axis=0, keepdims=True)
    return m_new, l_new, alpha, p.astype(bf16)


class _Chain(NamedTuple):
    scores: Callable
    values: Callable
    offset: Callable
    diag_mask: Callable
    emit: Callable
    next_first: Callable
    s_buf: Any
    p_buf: Any


def _flash_sweep(n_full, chains, n_q, is_first):
    @pl.when(is_first)
    def _():
        for c in chains:
            c.s_buf[0] = c.scores(0)

    for c in chains:
        c.p_buf[1] = jnp.zeros(c.p_buf.shape[1:], c.p_buf.dtype)

    def accumulate(c, j, slot, alpha, acc):
        return alpha * acc + _dot(c.values(jnp.maximum(j, 0)), c.p_buf[slot])

    def step(j, cur, states):
        out = []
        for c, (m, l, acc, alpha_prev) in zip(chains, states):
            acc = accumulate(c, j - 1, 1 - cur, alpha_prev, acc)
            m, l, alpha, p = _softmax_block(c.s_buf[cur], c.offset(j), m, l)
            c.p_buf[cur] = p
            c.s_buf[1 - cur] = c.scores(j + 1)
            out.append((m, l, acc, alpha))
        return tuple(out)

    def finish(cur, states):
        nxt = [c.next_first() for c in chains]
        for c, s_next, (m, l, acc, alpha_prev) in zip(chains, nxt, states):
            if cur == 1:
                c.s_buf[0] = s_next
            acc = accumulate(c, n_full - 1, 1 - cur, alpha_prev, acc)
            m, l, alpha, p = _softmax_block(c.diag_mask(c.s_buf[cur]), None, m, l)
            if cur == 0:
                c.s_buf[0] = s_next
            acc = alpha * acc + _dot(c.values(n_full), p)
            c.emit(acc / l)

    init = tuple((jnp.full((1, n_q), -jnp.inf, f32), jnp.zeros((1, n_q), f32),
                  jnp.zeros((HEAD_DIM, n_q), f32), jnp.ones((1, n_q), f32)) for _ in chains)
    states = lax.fori_loop(0, n_full // 2, lambda jj, st: step(2 * jj + 1, 1, step(2 * jj, 0, st)), init)

    @pl.when(n_full % 2 == 0)
    def _():
        finish(0, states)

    @pl.when(n_full % 2 == 1)
    def _():
        finish(1, step(n_full - 1, 0, states))


def _fox_kernel(q_ref, qn_ref, k_ref, vt_ref, c_ref, o_ref, *bufs, tq, n_heads):
    hb = pl.program_id(1)
    i = pl.program_id(2)
    lane = lax.broadcasted_iota(jnp.int32, (1, LANES), 1)

    def diag_mask(s):
        kk = lax.broadcasted_iota(jnp.int32, (tq, tq), 0)
        qq = lax.broadcasted_iota(jnp.int32, (tq, tq), 1)
        return jnp.where(kk <= qq, s, -jnp.inf)

    def chain(u):
        qa = q_ref[0, u]

        def cin(j):
            return jnp.sum(jnp.where(lane == hb * n_heads + u, c_ref[0, j], 0.0), axis=-1, keepdims=True)

        ci = cin(i)

        def scores(j):
            return _dot_nt(k_ref[0, u, pl.ds(pl.multiple_of(j * tq, tq), tq), :], qa)

        def emit(out_t):
            o_ref[0, :, u * HEAD_DIM:(u + 1) * HEAD_DIM] = out_t.T.astype(bf16)

        def next_first():
            return _dot_nt(k_ref[0, u, 0:tq, :], qn_ref[0, u])

        return _Chain(scores, lambda j: vt_ref[0, u, j], lambda j: (ci - cin(j)) * LOG2E, diag_mask, emit,
                      next_first, bufs[2 * u], bufs[2 * u + 1])

    _flash_sweep(i, [chain(u) for u in range(n_heads)], tq, i == 0)


def _fox(fq, fk, fvt, cin):
    n_batch, n_h, seq_len, _ = fq.shape
    tq = KV_BLOCK
    nkb = seq_len // tq
    n_heads = 4
    return pl.pallas_call(
        functools.partial(_fox_kernel, tq=tq, n_heads=n_heads),
        out_shape=jax.ShapeDtypeStruct((n_batch, seq_len, n_h * HEAD_DIM), bf16),
        grid=(n_batch, n_h // n_heads, nkb),
        in_specs=[
            pl.BlockSpec((1, n_heads, tq, 2 * HEAD_DIM), lambda b, h, i: (b, h, i, 0)),
            pl.BlockSpec((1, n_heads, tq, 2 * HEAD_DIM), lambda b, h, i: (b, h, jnp.minimum(i + 1, nkb - 1), 0)),
            pl.BlockSpec((1, n_heads, seq_len, 2 * HEAD_DIM), lambda b, h, i: (b, h, 0, 0)),
            pl.BlockSpec((1, n_heads, nkb, HEAD_DIM, tq), lambda b, h, i: (b, h, 0, 0, 0)),
            pl.BlockSpec((1, nkb, 1, LANES), lambda b, h, i: (b, 0, 0, 0)),
        ],
        out_specs=pl.BlockSpec((1, tq, n_heads * HEAD_DIM), lambda b, h, i: (b, i, h)),
        scratch_shapes=[pltpu.VMEM((2, tq, tq), f32), pltpu.VMEM((2, tq, tq), bf16)] * n_heads,
        compiler_params=_params("arbitrary", "arbitrary", "arbitrary"),
        name="fox",
    )(fq, fq, fk, fvt, cin)


def _cmp_kernel(q_ref, k_ref, v_ref, gate_ref, ov_ref, o_ref, sel_ref, *, tq, n_rows, n_cmp, n_sel):
    i = pl.program_id(2)
    g = pl.program_id(1)
    hg = NSA_GROUP
    q = q_ref[0].reshape(hg * tq, HEAD_DIM)
    s = _dot_nt(q, k_ref[0, 0]).reshape(hg, tq, n_rows)
    t = i * tq + lax.broadcasted_iota(jnp.int32, (1, tq, 1), 1)
    n = lax.broadcasted_iota(jnp.int32, (1, 1, n_rows), 2)
    last = jnp.minimum(lax.shift_right_arithmetic(t - (CMP_LEN - 1), CMP_SHIFT), n_cmp - 1)
    s = jnp.where(n <= last, s, -jnp.inf)
    m = jnp.max(s, axis=-1, keepdims=True)
    m = jnp.where(m == -jnp.inf, 0.0, m)
    e = jnp.exp2(s - m)
    p = e * (1.0 / jnp.maximum(jnp.sum(e, axis=-1, keepdims=True), 1e-30))
    o = _dot(p.reshape(hg * tq, n_rows).astype(bf16), v_ref[0, 0])
    _write_gated(o_ref, gate_ref, o, g, 0, tq)

    lane = lax.broadcasted_iota(jnp.int32, (tq, LANES), 1)
    tq_pos = i * tq + lax.broadcasted_iota(jnp.int32, (tq, 1), 0)
    q_blk = lax.shift_right_logical(tq_pos, SEL_SHIFT)
    causal = lane <= q_blk
    n_top = min(SEL_TOPK, n_sel)

    @pl.when((i + 1) * tq <= n_top * SEL_BLOCK)
    def _():
        sel_ref[0, 0] = jnp.where(causal, 0.0, -MASK_BIG).astype(bf16)

    @pl.when((i + 1) * tq > n_top * SEL_BLOCK)
    def _():
        psum = p[0] + p[1] + p[2] + p[3]
        imp = _dot_f32_by_exact(psum, ov_ref[...])
        forced = (lane == 0) | (lane == q_blk) | (lane == q_blk - 1)
        key = jnp.where(forced, -2.0, jnp.where(causal, imp, -1.0))
        key = jnp.where(lane < n_sel, key, -3.0)
        lane_f = lane.astype(f32)
        sel = forced
        for _ in range(n_top - 3):
            mx = jnp.max(key, axis=-1, keepdims=True)
            first = jnp.min(jnp.where(key == mx, lane_f, float(LANES)), axis=-1, keepdims=True)
            pick = lane_f == first
            sel = sel | pick
            key = jnp.where(pick, -2.0, key)
        sel = ((q_blk < n_top) | sel) & causal
        sel_ref[0, 0] = jnp.where(sel, 0.0, -MASK_BIG).astype(bf16)


def _cmp(nqu, kcmp, vcmp, gates, n_cmp):
    n_batch, _, seq_len, _ = nqu.shape
    n_rows = kcmp.shape[2]
    n_sel = seq_len // SEL_BLOCK
    assert 3 <= n_sel <= SEL_LANES
    tq = 1024
    c0 = np.arange(n_rows) * CMP_STRIDE
    s0 = np.arange(SEL_LANES) * SEL_BLOCK
    overlap = np.clip(np.minimum(c0[:, None] + CMP_LEN, s0[None, :] + SEL_BLOCK)
                      - np.maximum(c0[:, None], s0[None, :]), 0, None).astype(np.float32) / CMP_LEN
    overlap[n_cmp:, :] = 0.0
    overlap[:, n_sel:] = 0.0
    return pl.pallas_call(
        functools.partial(_cmp_kernel, tq=tq, n_rows=n_rows, n_cmp=n_cmp, n_sel=n_sel),
        out_shape=[
            jax.ShapeDtypeStruct((n_batch, seq_len, NSA_W), bf16),
            jax.ShapeDtypeStruct((n_batch, NSA_KV_HEADS, seq_len, SEL_LANES), bf16),
        ],
        grid=(n_batch, NSA_KV_HEADS, seq_len // tq),
        in_specs=[
            pl.BlockSpec((1, NSA_GROUP, tq, HEAD_DIM), lambda b, g, i: (b, g, i, 0)),
            pl.BlockSpec((1, 1, n_rows, HEAD_DIM), lambda b, g, i: (b, g, 0, 0)),
            pl.BlockSpec((1, 1, n_rows, HEAD_DIM), lambda b, g, i: (b, g, 0, 0)),
            pl.BlockSpec((1, tq, LANES), lambda b, g, i: (b, i, 0)),
            pl.BlockSpec((n_rows, SEL_LANES), lambda b, g, i: (0, 0)),
        ],
        out_specs=[
            pl.BlockSpec((1, tq, NSA_GROUP * HEAD_DIM), lambda b, g, i: (b, i, g)),
            pl.BlockSpec((1, 1, tq, SEL_LANES), lambda b, g, i: (b, g, i, 0)),
        ],
        compiler_params=_params("arbitrary", "arbitrary", "arbitrary"),
        name="cmp",
    )(nqu, kcmp, vcmp, gates, jnp.asarray(overlap, dtype=bf16))


def _write_gated(o_ref, gate_ref, out, g, branch, tq):
    gates = gate_ref[0]
    lane = lax.broadcasted_iota(jnp.int32, (tq, LANES), 1)
    for h in range(NSA_GROUP):
        gcol = FOX_HEADS + (g * NSA_GROUP + h) * N_BRANCH + branch
        gh = jnp.sum(jnp.where(lane == gcol, gates, 0.0), axis=-1, keepdims=True)
        o_ref[0, :, h * HEAD_DIM:(h + 1) * HEAD_DIM] = (gh * out[h * tq:(h + 1) * tq]).astype(bf16)


def _slc_kernel(q_ref, sel_ref, qn_ref, seln_ref, k_ref, vt_ref, gate_ref, *rest, tq, tk, n_cast_w):
    o_ref = rest[n_cast_w]
    bufs = rest[2 * n_cast_w + 1:]
    for w32_ref, w16_ref in zip(rest[:n_cast_w], rest[n_cast_w + 1:2 * n_cast_w + 1]):
        w16_ref[...] = w32_ref[...].astype(bf16)
    i = pl.program_id(1)
    hg = NSA_GROUP
    n_q = hg * tq
    q0 = i * tq
    jd = q0 // tk

    def diag_mask(s):
        kp = jd * tk + lax.broadcasted_iota(jnp.int32, (tk, n_q), 0)
        t = q0 + (lax.broadcasted_iota(jnp.int32, (tk, n_q), 1) & (tq - 1))
        return jnp.where(kp <= t, s, -jnp.inf)

    def chain(g):
        def q_aug(qr, sr):
            q = qr[0, g * hg:(g + 1) * hg].reshape(n_q, HEAD_DIM)
            return jnp.concatenate([q, jnp.concatenate([sr[0, g]] * hg, axis=0)], axis=1)

        qa = q_aug(q_ref, sel_ref)

        def scores(j):
            return _dot_nt(k_ref[0, g, pl.ds(pl.multiple_of(j * tk, tk), tk), :], qa)

        def next_first():
            return _dot_nt(k_ref[0, g, 0:tk, :], q_aug(qn_ref, seln_ref))

        def emit(out):
            gates = gate_ref[0]
            lane = lax.broadcasted_iota(jnp.int32, (tq, LANES), 1)
            for h in range(hg):
                head = g * hg + h
                gh = jnp.sum(jnp.where(lane == FOX_HEADS + head * N_BRANCH + 1, gates, 0.0),
                             axis=-1, keepdims=True)
                o_ref[0, :, head * HEAD_DIM:(head + 1) * HEAD_DIM] = (
                    gh * out[:, h * tq:(h + 1) * tq].T).astype(bf16)

        return _Chain(scores, lambda j: vt_ref[0, g, j], lambda j: None, diag_mask, emit, next_first,
                      bufs[2 * g], bufs[2 * g + 1])

    _flash_sweep(jd, [chain(g) for g in range(NSA_KV_HEADS)], n_q, i == 0)


def _slc(nqr, selneg, kaug, vst, gates, cast_weights):
    n_batch, _, seq_len, _ = nqr.shape
    tq, tk = 256, KV_BLOCK
    nkb = seq_len // tk
    assert tq & (tq - 1) == 0
    n_q = NSA_GROUP * tq
    n_i = seq_len // tq
    cast_in, cast_out, cast_shapes = _cast_side_job(cast_weights, n_batch * n_i, lambda b, i: b * n_i + i)
    return pl.pallas_call(
        functools.partial(_slc_kernel, tq=tq, tk=tk, n_cast_w=len(cast_weights)),
        out_shape=[jax.ShapeDtypeStruct((n_batch, seq_len, NSA_W), bf16)] + cast_shapes,
        grid=(n_batch, n_i),
        in_specs=[
            pl.BlockSpec((1, NSA_HEADS, tq, HEAD_DIM), lambda b, i: (b, 0, i, 0)),
            pl.BlockSpec((1, NSA_KV_HEADS, tq, SEL_LANES), lambda b, i: (b, 0, i, 0)),
            pl.BlockSpec((1, NSA_HEADS, tq, HEAD_DIM), lambda b, i: (b, 0, jnp.minimum(i + 1, n_i - 1), 0)),
            pl.BlockSpec((1, NSA_KV_HEADS, tq, SEL_LANES), lambda b, i: (b, 0, jnp.minimum(i + 1, n_i - 1), 0)),
            pl.BlockSpec((1, NSA_KV_HEADS, seq_len, HEAD_DIM + SEL_LANES), lambda b, i: (b, 0, 0, 0)),
            pl.BlockSpec((1, NSA_KV_HEADS, nkb, HEAD_DIM, tk), lambda b, i: (b, 0, 0, 0, 0)),
            pl.BlockSpec((1, tq, LANES), lambda b, i: (b, i, 0)),
        ] + cast_in,
        out_specs=[pl.BlockSpec((1, tq, NSA_W), lambda b, i: (b, i, 0))] + cast_out,
        scratch_shapes=[pltpu.VMEM((2, tk, n_q), f32), pltpu.VMEM((2, tk, n_q), bf16)] * NSA_KV_HEADS,
        compiler_params=_params("arbitrary", "arbitrary"),
        name="slc",
    )(nqr, selneg, nqr, selneg, kaug, vst, gates, *cast_weights)


def _win_kernel(q_ref, k_ref, vt_ref, gate_ref, o_ref, *, tq, n_sub):
    i = pl.program_id(2)
    g = pl.program_id(1)
    hg = NSA_GROUP
    span = WINDOW + tq
    q0s = [(i * n_sub + u) * tq for u in range(n_sub)]
    k0s = [pl.multiple_of(jnp.maximum(q0 - WINDOW, 0), tq) for q0 in q0s]
    scores = []
    for u in range(n_sub):
        q = q_ref[0, :, u * tq:(u + 1) * tq, :].reshape(hg * tq, HEAD_DIM)
        scores.append(_dot_nt(k_ref[0, 0, pl.ds(k0s[u], span), :], q))
    probs, denoms = [], []
    for u in range(n_sub):
        kp = k0s[u] + lax.broadcasted_iota(jnp.int32, (span, tq), 0)
        t = q0s[u] + lax.broadcasted_iota(jnp.int32, (span, tq), 1)
        diff = t - kp
        bias = jnp.where((diff >= 0) & (diff < WINDOW), 0.0, -jnp.inf)
        s = jnp.concatenate([scores[u][:, h * tq:(h + 1) * tq] + bias for h in range(hg)], axis=1)
        e = jnp.exp2(s - jnp.max(s, axis=0, keepdims=True))
        denoms.append(jnp.sum(e, axis=0, keepdims=True))
        probs.append(e.astype(bf16))
    gates = gate_ref[0]
    lane = lax.broadcasted_iota(jnp.int32, (tq, LANES), 1)
    for u in range(n_sub):
        jb = k0s[u] // tq
        vt = jnp.concatenate([vt_ref[0, 0, jb + c] for c in range(span // tq)], axis=1)
        out = _dot(vt, probs[u]) / denoms[u]
        for h in range(hg):
            gcol = FOX_HEADS + (g * hg + h) * N_BRANCH + 2
            gh = jnp.sum(jnp.where(lane == gcol, gates[u * tq:(u + 1) * tq], 0.0), axis=-1, keepdims=True)
            o_ref[0, u * tq:(u + 1) * tq, h * HEAD_DIM:(h + 1) * HEAD_DIM] = (
                gh * out[:, h * tq:(h + 1) * tq].T).astype(bf16)


def _win(nqr, kw, vwt, gates):
    n_batch, _, seq_len, _ = nqr.shape
    tq, n_sub = WIN_TILE, 8
    assert seq_len >= WINDOW + tq and WINDOW % tq == 0
    return pl.pallas_call(
        functools.partial(_win_kernel, tq=tq, n_sub=n_sub),
        out_shape=jax.ShapeDtypeStruct((n_batch, seq_len, NSA_W), bf16),
        grid=(n_batch, NSA_KV_HEADS, seq_len // (tq * n_sub)),
        in_specs=[
            pl.BlockSpec((1, NSA_GROUP, tq * n_sub, HEAD_DIM), lambda b, g, i: (b, g, i, 0)),
            pl.BlockSpec((1, 1, seq_len, HEAD_DIM), lambda b, g, i: (b, g, 0, 0)),
            pl.BlockSpec((1, 1, seq_len // tq, HEAD_DIM, tq), lambda b, g, i: (b, g, 0, 0, 0)),
            pl.BlockSpec((1, tq * n_sub, LANES), lambda b, g, i: (b, i, 0)),
        ],
        out_specs=pl.BlockSpec((1, tq * n_sub, NSA_GROUP * HEAD_DIM), lambda b, g, i: (b, i, g)),
        compiler_params=_params("arbitrary", "arbitrary", "arbitrary"),
        name="win",
    )(nqr, kw, vwt, gates)


def _outproj_kernel(fox_ref, c_ref, s_ref, w_ref, x_ref, mod_ref, wo_ref, o_ref):
    mix = _dot(fox_ref[...], wo_ref[0:FOX_W, :])
    nsa = c_ref[...].astype(f32) + s_ref[...].astype(f32) + w_ref[...].astype(f32)
    mix = mix + _dot(nsa.astype(bf16), wo_ref[FOX_W:FOX_W + NSA_W, :])
    o_ref[...] = x_ref[...] + mod_ref[0][2:3] * mix


def _outproj(ofox, ocmp, oslc, owin, x2d, mod3, wo_bf16, seq_len):
    m_rows = x2d.shape[0]
    tm = 512
    per_b = seq_len // tm
    half = pl.BlockSpec((tm, FOX_W), lambda i: (i, 0))
    return pl.pallas_call(
        _outproj_kernel,
        out_shape=jax.ShapeDtypeStruct((m_rows, D_MODEL), f32),
        grid=(m_rows // tm,),
        in_specs=[
            half, half, half, half,
            pl.BlockSpec((tm, D_MODEL), lambda i: (i, 0)),
            pl.BlockSpec((1, 6, D_MODEL), lambda i: (i // per_b, 0, 0)),
            pl.BlockSpec((D_MODEL, D_MODEL), lambda i: (0, 0)),
        ],
        out_specs=pl.BlockSpec((tm, D_MODEL), lambda i: (i, 0)),
        compiler_params=_params("arbitrary"),
        name="outproj",
    )(ofox, ocmp, oslc, owin, x2d, mod3, wo_bf16)


def _mlp_kernel(x_ref, mod_ref, g_ref, wu_ref, wdp_ref, wdc_ref, o_ref, h_ref, aa_ref, ab_ref, acc_ref, *, th):
    s = pl.program_id(1)
    n_s = pl.num_programs(1) - 1

    def up(half, a_ref):
        u = jnp.maximum(_dot(h_ref[...], wu_ref[:, half * th:(half + 1) * th]), 0.0)
        a_ref[...] = (u * u).astype(bf16)

    @pl.when(s == 0)
    def _():
        md = mod_ref[0]
        y = _rms(x_ref[...], g_ref[...])
        h_ref[...] = (y * (1.0 + md[4:5]) + md[3:4]).astype(bf16)
        up(0, aa_ref)
        up(1, ab_ref)
        acc_ref[...] = _dot(aa_ref[...], wdc_ref[...])

    @pl.when((s > 0) & (s < n_s))
    def _():
        up(0, aa_ref)
        acc_ref[...] += _dot(ab_ref[...], wdp_ref[...])
        up(1, ab_ref)
        acc_ref[...] += _dot(aa_ref[...], wdc_ref[...])

    @pl.when(s == n_s)
    def _():
        acc = acc_ref[...] + _dot(ab_ref[...], wdp_ref[...])
        o_ref[...] = x_ref[...] + mod_ref[0][5:6] * acc


def _mlp(x2d, mod3, norm_g, wu_bf16, wd_bf16, seq_len):
    m_rows = x2d.shape[0]
    tm, th = 512, 512
    per_b = seq_len // tm
    n_s = D_FF // (2 * th)
    n_half = 2 * n_s
    return pl.pallas_call(
        functools.partial(_mlp_kernel, th=th),
        out_shape=jax.ShapeDtypeStruct((m_rows, D_MODEL), f32),
        grid=(m_rows // tm, n_s + 1),
        in_specs=[
            pl.BlockSpec((tm, D_MODEL), lambda i, s: (i, 0)),
            pl.BlockSpec((1, 6, D_MODEL), lambda i, s: (i // per_b, 0, 0)),
            pl.BlockSpec((1, D_MODEL), lambda i, s: (0, 0)),
            pl.BlockSpec((D_MODEL, 2 * th), lambda i, s: (0, jnp.minimum(s, n_s - 1))),
            pl.BlockSpec((th, D_MODEL), lambda i, s: (jnp.maximum(2 * s - 1, 0), 0)),
            pl.BlockSpec((th, D_MODEL), lambda i, s: (jnp.minimum(2 * s, n_half - 1), 0)),
        ],
        out_specs=pl.BlockSpec((tm, D_MODEL), lambda i, s: (i, 0)),
        scratch_shapes=[pltpu.VMEM((tm, D_MODEL), bf16), pltpu.VMEM((tm, th), bf16), pltpu.VMEM((tm, th), bf16),
                        pltpu.VMEM((tm, D_MODEL), f32)],
        compiler_params=_params("arbitrary", "arbitrary"),
        name="mlp",
    )(x2d, mod3, norm_g, wu_bf16, wd_bf16, wd_bf16)


def _layer(x, c, w_ada, b_ada, norm1_g, w_in, b_forget, fox_q_norm, fox_k_norm, nsa_q_norm,
           cmp_k_norm, slc_k_norm, win_k_norm, cmp_pe_k, cmp_w1_k, cmp_w2_k, cmp_pe_v, cmp_w1_v,
           cmp_w2_v, w_out, norm2_g, w_up, w_down):
    n_batch, seq_len, _ = x.shape
    n_cmp = (seq_len - CMP_LEN) // CMP_STRIDE + 1
    row = lambda v: v.reshape(1, -1)

    half = HEAD_DIM // 2
    inv_freq = ROPE_THETA ** (-jnp.arange(half, dtype=f32) / half)
    inv_freq = jnp.concatenate([inv_freq, inv_freq]).reshape(1, HEAD_DIM)

    w_a, w_b, w_s = _repack_w_in(w_in.T)

    mod3 = _ada(c, w_ada, b_ada).reshape(n_batch, 6, D_MODEL)
    x2d = x.reshape(n_batch * seq_len, D_MODEL)
    (fq, fk, fvt, cin, nqu, nqr, kc, vc, kaug, vst, kw, vw, gates) = _prep(
        x2d, mod3, row(norm1_g), w_a, w_b, w_s, n_batch, seq_len, row(fox_q_norm), row(fox_k_norm), row(nsa_q_norm), row(slc_k_norm),
        row(win_k_norm), jnp.pad(b_forget, (0, LANES - FOX_HEADS)).reshape(1, LANES), inv_freq)
    kcmp = _compress(kc, cmp_pe_k, cmp_w1_k.astype(bf16), cmp_w2_k.astype(bf16), row(cmp_k_norm))
    vcmp = _compress(vc, cmp_pe_v, cmp_w1_v.astype(bf16), cmp_w2_v.astype(bf16))
    ofox = _fox(fq, fk, fvt, cin)
    ocmp, selneg = _cmp(nqu, kcmp, vcmp, gates, n_cmp)
    oslc, w_up16, w_down16, w_out16 = _slc(nqr, selneg, kaug, vst, gates, [w_up, w_down, w_out])
    owin = _win(nqr, kw, vw, gates)
    x1 = _outproj(ofox.reshape(-1, FOX_W), ocmp.reshape(-1, NSA_W), oslc.reshape(-1, NSA_W),
                  owin.reshape(-1, NSA_W), x2d, mod3, w_out16, seq_len)
    x2 = _mlp(x1, mod3, row(norm2_g), w_up16, w_down16, seq_len)
    return x2.reshape(n_batch, seq_len, D_MODEL)


def kernel(x, c, w_ada, b_ada, norm1_g, w_in, b_forget, fox_q_norm, fox_k_norm, nsa_q_norm, cmp_k_norm,
           slc_k_norm, win_k_norm, cmp_pe_k, cmp_w1_k, cmp_w2_k, cmp_pe_v, cmp_w1_v, cmp_w2_v, w_out,
           norm2_g, w_up, w_down):
    depth = w_ada.shape[0]
    for l in range(depth):
        x = _layer(x, c, w_ada[l], b_ada[l], norm1_g[l], w_in[l], b_forget[l], fox_q_norm[l], fox_k_norm[l],
                   nsa_q_norm[l], cmp_k_norm[l], slc_k_norm[l], win_k_norm[l], cmp_pe_k[l], cmp_w1_k[l],
                   cmp_w2_k[l], cmp_pe_v[l], cmp_w1_v[l], cmp_w2_v[l], w_out[l], norm2_g[l], w_up[l],
                   w_down[l])
    return x
```

```python
import functools
import math
from typing import Any, Callable, NamedTuple

import numpy as np
import jax
import jax.numpy as jnp
from jax import lax
from jax.experimental import pallas as pl
from jax.experimental.pallas import tpu as pltpu

D_MODEL = 2048
HEAD_DIM = 128
FOX_HEADS = 8
NSA_HEADS = 8
NSA_KV_HEADS = 2
NSA_GROUP = NSA_HEADS // NSA_KV_HEADS
N_BRANCH = 3
D_FF = 4 * D_MODEL
ROPE_THETA = 10000.0
CMP_LEN = 32
CMP_STRIDE = 16
CMP_SHIFT = 4
CMP_HIDDEN = 2 * HEAD_DIM
SEL_BLOCK = 64
SEL_SHIFT = 6
SEL_TOPK = 16
WINDOW = 512
NORM_EPS = 1e-6
ATTN_SCALE = HEAD_DIM ** -0.5
FOX_W = FOX_HEADS * HEAD_DIM
NSA_W = NSA_HEADS * HEAD_DIM
KV_W = NSA_KV_HEADS * HEAD_DIM

LANES = 128
F32_SUBLANES = 8
BF16_SUBLANES = 16
SEL_LANES = LANES
MASK_BIG = 1e30
KV_BLOCK = 512
WIN_TILE = 128
LOG2E = math.log2(math.e)

COL_FQ = 0
COL_FK = COL_FQ + FOX_W
COL_FV = COL_FK + FOX_W
COL_NQ = COL_FV + FOX_W
COL_KC = COL_NQ + NSA_W
COL_VC = COL_KC + KV_W
COL_KS = COL_VC + KV_W
COL_VS = COL_KS + KV_W
COL_KW = COL_VS + KV_W
COL_VW = COL_KW + KV_W
COL_SMALL = COL_VW + KV_W
W_IN_Z0 = 3 * FOX_W
W_IN_NQ0 = W_IN_Z0 + FOX_HEADS
W_IN_GZ0 = W_IN_NQ0 + NSA_W + 6 * KV_W

VMEM_LIMIT = 56 * 1024 * 1024
VMEM_LIMIT_PREP = 60 * 1024 * 1024

f32 = jnp.float32
bf16 = jnp.bfloat16


def _params(*sem):
    return pltpu.CompilerParams(dimension_semantics=sem, vmem_limit_bytes=VMEM_LIMIT)


def _dot_nt(a, b):
    return lax.dot_general(a, b, (((1,), (1,)), ((), ())), preferred_element_type=f32)


def _dot(a, b):
    return jnp.dot(a, b, preferred_element_type=f32)


def _split3(x):
    hi = x.astype(bf16)
    r1 = x - hi.astype(f32)
    mid = r1.astype(bf16)
    lo = (r1 - mid.astype(f32)).astype(bf16)
    return hi, mid, lo


def _dot_f32_by_exact(x, w_bf16):
    hi, mid, lo = _split3(x)
    return _dot(hi, w_bf16) + (_dot(mid, w_bf16) + _dot(lo, w_bf16))


def _rms(x, gain):
    ms = jnp.mean(x * x, axis=-1, keepdims=True)
    return x * lax.rsqrt(ms + NORM_EPS) * gain


def _ada_kernel(ct_ref, w_ref, b_ref, o_ref, *, n_batch, k_chunk):
    ct = ct_ref[...]
    act = ct * jax.nn.sigmoid(ct)
    rows = []
    for b in range(n_batch):
        col = act[:, b:b + 1]
        acc = b_ref[...]
        for k0 in range(0, D_MODEL, k_chunk):
            acc = acc + jnp.sum(w_ref[k0:k0 + k_chunk, :] * col[k0:k0 + k_chunk], axis=0, keepdims=True)
        rows.append(acc)
    o_ref[...] = jnp.concatenate(rows, axis=0)


def _ada(c, w_ada, b_ada):
    n_batch = c.shape[0]
    n_out = w_ada.shape[1]
    tn = 1024
    return pl.pallas_call(
        functools.partial(_ada_kernel, n_batch=n_batch, k_chunk=256),
        out_shape=jax.ShapeDtypeStruct((n_batch, n_out), f32),
        grid=(n_out // tn,),
        in_specs=[
            pl.BlockSpec((D_MODEL, n_batch), lambda j: (0, 0)),
            pl.BlockSpec((D_MODEL, tn), lambda j: (0, j)),
            pl.BlockSpec((1, tn), lambda j: (0, j)),
        ],
        out_specs=pl.BlockSpec((n_batch, tn), lambda j: (0, j)),
        compiler_params=_params("arbitrary"),
        name="ada",
    )(c.T, w_ada, b_ada.reshape(1, n_out))


def _repack_kernel(a_ref, *rest, n_part):
    part_refs, (z_ref, gz_ref, wa_ref, wb_ref, ws_ref) = rest[:n_part], rest[n_part:]
    wa_ref[...] = a_ref[...].astype(bf16)
    wb_ref[...] = jnp.concatenate([r[...] for r in part_refs], axis=0).astype(bf16)
    pad = jnp.zeros((LANES - z_ref.shape[0] - gz_ref.shape[0], D_MODEL), f32)
    ws_ref[...] = jnp.concatenate([z_ref[...], gz_ref[...], pad], axis=0).astype(bf16)


def _repack_w_in(w_t):
    n_steps = 16
    ra, rb = W_IN_Z0 // n_steps, (W_IN_GZ0 - W_IN_NQ0) // n_steps
    hb = math.gcd(W_IN_NQ0, rb)
    n_part = rb // hb
    n_z, n_gz = W_IN_NQ0 - W_IN_Z0, w_t.shape[0] - W_IN_GZ0
    assert W_IN_Z0 % n_steps == 0 and (W_IN_GZ0 - W_IN_NQ0) % n_steps == 0 and ra % BF16_SUBLANES == 0 and rb % BF16_SUBLANES == 0
    assert hb % F32_SUBLANES == 0 and W_IN_Z0 % n_z == 0 and W_IN_GZ0 % n_gz == 0
    b_off = W_IN_NQ0 // hb
    part = lambda k: pl.BlockSpec((hb, D_MODEL), lambda j: (b_off + n_part * j + k, 0))
    return pl.pallas_call(
        functools.partial(_repack_kernel, n_part=n_part),
        out_shape=[jax.ShapeDtypeStruct((W_IN_Z0, D_MODEL), bf16),
                   jax.ShapeDtypeStruct((W_IN_GZ0 - W_IN_NQ0, D_MODEL), bf16),
                   jax.ShapeDtypeStruct((LANES, D_MODEL), bf16)],
        grid=(n_steps,),
        in_specs=[pl.BlockSpec((ra, D_MODEL), lambda j: (j, 0))] + [part(k) for k in range(n_part)] + [
            pl.BlockSpec((n_z, D_MODEL), lambda j: (W_IN_Z0 // n_z, 0)),
            pl.BlockSpec((n_gz, D_MODEL), lambda j: (W_IN_GZ0 // n_gz, 0)),
        ],
        out_specs=[pl.BlockSpec((ra, D_MODEL), lambda j: (j, 0)),
                   pl.BlockSpec((rb, D_MODEL), lambda j: (j, 0)),
                   pl.BlockSpec((LANES, D_MODEL), lambda j: (0, 0))],
        compiler_params=_params("arbitrary"),
        name="repack",
    )(*([w_t] * (n_part + 3)))


def _cast_side_job(weights, n_steps, step_of):
    n_cast = max(c for c in range(1, n_steps + 1) if all(w.shape[0] % (BF16_SUBLANES * c) == 0 for w in weights))
    idx = lambda *ids: (jnp.minimum(step_of(*ids), n_cast - 1), 0)
    specs = [pl.BlockSpec((w.shape[0] // n_cast, w.shape[1]), idx) for w in weights]
    return specs, specs, [jax.ShapeDtypeStruct(w.shape, bf16) for w in weights]


def _prep_kernel(x_ref, mod_ref, g1_ref, wa_ref, wb_ref, ws_ref,
                 gq_ref, gk_ref, gn_ref, gs_ref, gw_ref, bf_ref, inv_ref,
                 fq_ref, fk_ref, fvt_ref, cin_ref, nqu_ref, nqr_ref, kc_ref, vc_ref,
                 ks_ref, vst_ref, kw_ref, vw_ref, gate_ref, carry_ref, rot_ref, *, tm):
    i = pl.program_id(1)

    @pl.when(i == 0)
    def _():
        carry_ref[...] = jnp.zeros_like(carry_ref)

    md = mod_ref[0]
    hn = (_rms(x_ref[...], g1_ref[...]) * (1.0 + md[1:2]) + md[0:1]).astype(bf16)
    groups = {}

    def head(col, h):
        base, w_ref, r0 = next((b, w, r) for b, w, r in (
            (COL_KC, wb_ref, NSA_W), (COL_NQ, wb_ref, 0), (COL_FV, wa_ref, 2 * FOX_W),
            (COL_FK, wa_ref, FOX_W), (COL_FQ, wa_ref, 0)) if col >= b)
        if base not in groups:
            n = (COL_SMALL - COL_KC) if base == COL_KC else FOX_W
            groups[base] = _dot_nt(hn, w_ref[r0:r0 + n, :])
        c0 = col - base + h * HEAD_DIM
        return groups[base][:, c0:c0 + HEAD_DIM]

    row = lax.broadcasted_iota(jnp.int32, (tm, LANES), 0)
    lane = lax.broadcasted_iota(jnp.int32, (tm, LANES), 1)
    pos = i * tm + row

    @pl.when((pl.program_id(0) == 0) & (i == 0))
    def _():
        ang_row = row.astype(f32) * inv_ref[...]
        rot_ref[0] = jnp.cos(ang_row)
        rot_ref[1] = jnp.sin(ang_row)

    ang0 = (i * tm).astype(f32) * inv_ref[...]
    cos0, sin0 = jnp.cos(ang0), jnp.sin(ang0)
    cos = cos0 * rot_ref[0] - sin0 * rot_ref[1]
    sin = sin0 * rot_ref[0] + cos0 * rot_ref[1]
    sin_signed = jnp.where(lane < HEAD_DIM // 2, -sin, sin)

    def rope(x):
        return x * cos + pltpu.roll(x, HEAD_DIM // 2, 1) * sin_signed

    small = _dot_nt(hn, ws_ref[...])
    z = small + bf_ref[...]
    logf = jnp.minimum(z, 0.0) - jnp.log1p(jnp.exp(-jnp.abs(z)))
    t_idx = lax.broadcasted_iota(jnp.int32, (tm, tm), 0)
    s_idx = lax.broadcasted_iota(jnp.int32, (tm, tm), 1)
    tri = jnp.where(s_idx <= t_idx, 1.0, 0.0).astype(bf16)
    hi, mid, lo = _split3(logf)
    local = _dot(tri, hi) + (_dot(tri, mid) + _dot(tri, lo))
    cin_ref[0, 0] = carry_ref[0:1, :]
    carry_ref[...] = carry_ref[...] + local[tm - 1:tm, :]
    b_hi, b_mid, b_lo = (v.astype(f32) for v in _split3(local * (-LOG2E)))
    ones3 = jnp.where(lane < 3, 1.0, 0.0).astype(bf16)

    for h in range(FOX_HEADS):
        q = _rms(head(COL_FQ, h), gq_ref[...] * (ATTN_SCALE * LOG2E))
        fq_ref[0, h] = jnp.concatenate([q.astype(bf16), ones3], axis=1)
    for h in range(FOX_HEADS):
        k = _rms(head(COL_FK, h), gk_ref[...])
        bias = jnp.where(lane == 0, b_hi[:, h:h + 1],
                         jnp.where(lane == 1, b_mid[:, h:h + 1],
                                   jnp.where(lane == 2, b_lo[:, h:h + 1], 0.0)))
        fk_ref[0, h] = jnp.concatenate([k.astype(bf16), bias.astype(bf16)], axis=1)

    gate_ref[0] = jax.nn.sigmoid(small)

    for h in range(NSA_HEADS):
        qn = _rms(head(COL_NQ, h), gn_ref[...] * (ATTN_SCALE * LOG2E))
        nqu_ref[0, h] = qn.astype(bf16)
        nqr_ref[0, h] = rope(qn).astype(bf16)
    onehot = jnp.where(lane == lax.shift_right_logical(pos, SEL_SHIFT), 1.0, 0.0).astype(bf16)
    for g in range(NSA_KV_HEADS):
        kc_ref[0, g] = head(COL_KC, g)
        vc_ref[0, g] = head(COL_VC, g)
        ks = rope(_rms(head(COL_KS, g), gs_ref[...])).astype(bf16)
        ks_ref[0, g] = jnp.concatenate([ks, onehot], axis=1)
        vst_ref[0, g, 0] = head(COL_VS, g).T.astype(bf16)
        kw_ref[0, g] = rope(_rms(head(COL_KW, g), gw_ref[...])).astype(bf16)
        vwt = head(COL_VW, g).T.astype(bf16)
        for c in range(tm // WIN_TILE):
            vw_ref[0, g, c] = vwt[:, c * WIN_TILE:(c + 1) * WIN_TILE]
    for h in range(FOX_HEADS):
        fvt_ref[0, h, 0] = head(COL_FV, h).T.astype(bf16)


def _prep(x2d, mod3, norm_g, wa, wb, ws, n_batch, seq_len, gq, gk, gn, gs, gw, b_forget_row, inv_freq):
    resident = lambda w: pl.BlockSpec(w.shape, lambda b, i: (0, 0), pipeline_mode=pl.Buffered(1))
    tm = KV_BLOCK
    per_b = seq_len // tm
    hshape = lambda n, w, dt: jax.ShapeDtypeStruct((n_batch, n, seq_len, w), dt)
    hspec = lambda n, w: pl.BlockSpec((1, n, tm, w), lambda b, i: (b, 0, i, 0))
    tshape = lambda n: jax.ShapeDtypeStruct((n_batch, n, per_b, HEAD_DIM, tm), bf16)
    tspec = lambda n: pl.BlockSpec((1, n, 1, HEAD_DIM, tm), lambda b, i: (b, 0, i, 0, 0))
    vec = pl.BlockSpec((1, LANES), lambda b, i: (0, 0))
    return pl.pallas_call(
        functools.partial(_prep_kernel, tm=tm),
        out_shape=[
            hshape(FOX_HEADS, 2 * HEAD_DIM, bf16), hshape(FOX_HEADS, 2 * HEAD_DIM, bf16), tshape(FOX_HEADS),
            jax.ShapeDtypeStruct((n_batch, per_b, 1, LANES), f32),
            hshape(NSA_HEADS, HEAD_DIM, bf16), hshape(NSA_HEADS, HEAD_DIM, bf16),
            hshape(NSA_KV_HEADS, HEAD_DIM, f32), hshape(NSA_KV_HEADS, HEAD_DIM, f32),
            hshape(NSA_KV_HEADS, HEAD_DIM + SEL_LANES, bf16), tshape(NSA_KV_HEADS),
            hshape(NSA_KV_HEADS, HEAD_DIM, bf16),
            jax.ShapeDtypeStruct((n_batch, NSA_KV_HEADS, seq_len // WIN_TILE, HEAD_DIM, WIN_TILE), bf16),
            jax.ShapeDtypeStruct((n_batch, seq_len, LANES), f32),
        ],
        grid=(n_batch, per_b),
        in_specs=[
            pl.BlockSpec((tm, D_MODEL), lambda b, i: (b * per_b + i, 0)),
            pl.BlockSpec((1, 6, D_MODEL), lambda b, i: (b, 0, 0)),
            pl.BlockSpec((1, D_MODEL), lambda b, i: (0, 0)),
            resident(wa), resident(wb), resident(ws),
            vec, vec, vec, vec, vec, vec, vec,
        ],
        out_specs=[
            hspec(FOX_HEADS, 2 * HEAD_DIM), hspec(FOX_HEADS, 2 * HEAD_DIM), tspec(FOX_HEADS),
            pl.BlockSpec((1, 1, 1, LANES), lambda b, i: (b, i, 0, 0)),
            hspec(NSA_HEADS, HEAD_DIM), hspec(NSA_HEADS, HEAD_DIM),
            hspec(NSA_KV_HEADS, HEAD_DIM), hspec(NSA_KV_HEADS, HEAD_DIM),
            hspec(NSA_KV_HEADS, HEAD_DIM + SEL_LANES), tspec(NSA_KV_HEADS),
            hspec(NSA_KV_HEADS, HEAD_DIM),
            pl.BlockSpec((1, NSA_KV_HEADS, tm // WIN_TILE, HEAD_DIM, WIN_TILE), lambda b, i: (b, 0, i, 0, 0)),
            pl.BlockSpec((1, tm, LANES), lambda b, i: (b, i, 0)),
        ],
        scratch_shapes=[pltpu.VMEM((8, LANES), f32), pltpu.VMEM((2, tm, LANES), f32)],
        compiler_params=pltpu.CompilerParams(dimension_semantics=("arbitrary", "arbitrary"),
                                             vmem_limit_bytes=VMEM_LIMIT_PREP),
        name="prep",
    )(x2d, mod3, norm_g, wa, wb, ws, gq, gk, gn, gs, gw, b_forget_row, inv_freq)


def _compress_kernel(x_ref, pe_ref, w1_ref, w2_ref, *rest, n_rows, do_norm):
    o_ref = rest[-1]
    a = jnp.zeros((n_rows, CMP_HIDDEN), f32)
    b = jnp.zeros((n_rows, CMP_HIDDEN), f32)
    for l in range(CMP_STRIDE):
        x = x_ref[0, 0, pl.ds(l, n_rows, stride=CMP_STRIDE), :]
        a = a + _dot((x + pe_ref[l:l + 1, :]).astype(bf16), w1_ref[l * HEAD_DIM:(l + 1) * HEAD_DIM, :])
        lb = CMP_STRIDE + l
        b = b + _dot((x + pe_ref[lb:lb + 1, :]).astype(bf16), w1_ref[lb * HEAD_DIM:(lb + 1) * HEAD_DIM, :])
    pre = a + pltpu.roll(b, n_rows - 1, 0)
    hid = pre * jax.nn.sigmoid(pre)
    out = _dot(hid.astype(bf16), w2_ref[...])
    if do_norm:
        out = _rms(out, rest[0][...])
    o_ref[0, 0] = out.astype(bf16)


def _compress(x4, pe, w1_bf16, w2_bf16, gain=None):
    n_batch, n_g, seq_len, _ = x4.shape
    n_rows = seq_len // CMP_STRIDE
    do_norm = gain is not None
    return pl.pallas_call(
        functools.partial(_compress_kernel, n_rows=n_rows, do_norm=do_norm),
        out_shape=jax.ShapeDtypeStruct((n_batch, n_g, n_rows, HEAD_DIM), bf16),
        grid=(n_batch, n_g),
        in_specs=[
            pl.BlockSpec((1, 1, seq_len, HEAD_DIM), lambda b, g: (b, g, 0, 0)),
            pl.BlockSpec((CMP_LEN, HEAD_DIM), lambda b, g: (0, 0)),
            pl.BlockSpec((CMP_LEN * HEAD_DIM, CMP_HIDDEN), lambda b, g: (0, 0)),
            pl.BlockSpec((CMP_HIDDEN, HEAD_DIM), lambda b, g: (0, 0)),
        ] + ([pl.BlockSpec((1, HEAD_DIM), lambda b, g: (0, 0))] if do_norm else []),
        out_specs=pl.BlockSpec((1, 1, n_rows, HEAD_DIM), lambda b, g: (b, g, 0, 0)),
        compiler_params=_params("arbitrary", "arbitrary"),
        name="compress",
    )(x4, pe, w1_bf16, w2_bf16, *([gain] if do_norm else []))


def _softmax_block(s, d, m_prev, l_prev):
    m_blk = jnp.max(s, axis=0, keepdims=True)
    m_new = jnp.maximum(m_prev, m_blk if d is None else m_blk + d)
    p = jnp.exp2(s - (m_new if d is None else m_new - d))
    alpha = jnp.exp2(m_prev - m_new)
    l_new = alpha * l_prev + jnp.sum(p, axis=0, keepdims=True)
    return m_new, l_new, alpha, p.astype(bf16)


class _Chain(NamedTuple):
    scores: Callable
    values: Callable
    offset: Callable
    diag_mask: Callable
    emit: Callable
    next_first: Callable
    s_buf: Any
    p_buf: Any


def _flash_sweep(n_full, chains, n_q, is_first):
    @pl.when(is_first)
    def _():
        for c in chains:
            c.s_buf[0] = c.scores(0)

    for c in chains:
        c.p_buf[1] = jnp.zeros(c.p_buf.shape[1:], c.p_buf.dtype)

    def accumulate(c, j, slot, alpha, acc):
        return alpha * acc + _dot(c.values(jnp.maximum(j, 0)), c.p_buf[slot])

    def step(j, cur, states):
        out = []
        for c, (m, l, acc, alpha_prev) in zip(chains, states):
            acc = accumulate(c, j - 1, 1 - cur, alpha_prev, acc)
            m, l, alpha, p = _softmax_block(c.s_buf[cur], c.offset(j), m, l)
            c.p_buf[cur] = p
            c.s_buf[1 - cur] = c.scores(j + 1)
            out.append((m, l, acc, alpha))
        return tuple(out)

    def finish(cur, states):
        nxt = [c.next_first() for c in chains]
        for c, s_next, (m, l, acc, alpha_prev) in zip(chains, nxt, states):
            if cur == 1:
                c.s_buf[0] = s_next
            acc = accumulate(c, n_full - 1, 1 - cur, alpha_prev, acc)
            m, l, alpha, p = _softmax_block(c.diag_mask(c.s_buf[cur]), None, m, l)
            if cur == 0:
                c.s_buf[0] = s_next
            acc = alpha * acc + _dot(c.values(n_full), p)
            c.emit(acc / l)

    init = tuple((jnp.full((1, n_q), -jnp.inf, f32), jnp.zeros((1, n_q), f32),
                  jnp.zeros((HEAD_DIM, n_q), f32), jnp.ones((1, n_q), f32)) for _ in chains)
    states = lax.fori_loop(0, n_full // 2, lambda jj, st: step(2 * jj + 1, 1, step(2 * jj, 0, st)), init)

    @pl.when(n_full % 2 == 0)
    def _():
        finish(0, states)

    @pl.when(n_full % 2 == 1)
    def _():
        finish(1, step(n_full - 1, 0, states))


def _fox_kernel(q_ref, qn_ref, k_ref, vt_ref, c_ref, o_ref, *bufs, tq, n_heads):
    hb = pl.program_id(1)
    i = pl.program_id(2)
    lane = lax.broadcasted_iota(jnp.int32, (1, LANES), 1)

    def diag_mask(s):
        kk = lax.broadcasted_iota(jnp.int32, (tq, tq), 0)
        qq = lax.broadcasted_iota(jnp.int32, (tq, tq), 1)
        return jnp.where(kk <= qq, s, -jnp.inf)

    def chain(u):
        qa = q_ref[0, u]

        def cin(j):
            return jnp.sum(jnp.where(lane == hb * n_heads + u, c_ref[0, j], 0.0), axis=-1, keepdims=True)

        ci = cin(i)

        def scores(j):
            return _dot_nt(k_ref[0, u, pl.ds(pl.multiple_of(j * tq, tq), tq), :], qa)

        def emit(out_t):
            o_ref[0, :, u * HEAD_DIM:(u + 1) * HEAD_DIM] = out_t.T.astype(bf16)

        def next_first():
            return _dot_nt(k_ref[0, u, 0:tq, :], qn_ref[0, u])

        return _Chain(scores, lambda j: vt_ref[0, u, j], lambda j: (ci - cin(j)) * LOG2E, diag_mask, emit,
                      next_first, bufs[2 * u], bufs[2 * u + 1])

    _flash_sweep(i, [chain(u) for u in range(n_heads)], tq, i == 0)


def _fox(fq, fk, fvt, cin):
    n_batch, n_h, seq_len, _ = fq.shape
    tq = KV_BLOCK
    nkb = seq_len // tq
    n_heads = 4
    return pl.pallas_call(
        functools.partial(_fox_kernel, tq=tq, n_heads=n_heads),
        out_shape=jax.ShapeDtypeStruct((n_batch, seq_len, n_h * HEAD_DIM), bf16),
        grid=(n_batch, n_h // n_heads, nkb),
        in_specs=[
            pl.BlockSpec((1, n_heads, tq, 2 * HEAD_DIM), lambda b, h, i: (b, h, i, 0)),
            pl.BlockSpec((1, n_heads, tq, 2 * HEAD_DIM), lambda b, h, i: (b, h, jnp.minimum(i + 1, nkb - 1), 0)),
            pl.BlockSpec((1, n_heads, seq_len, 2 * HEAD_DIM), lambda b, h, i: (b, h, 0, 0)),
            pl.BlockSpec((1, n_heads, nkb, HEAD_DIM, tq), lambda b, h, i: (b, h, 0, 0, 0)),
            pl.BlockSpec((1, nkb, 1, LANES), lambda b, h, i: (b, 0, 0, 0)),
        ],
        out_specs=pl.BlockSpec((1, tq, n_heads * HEAD_DIM), lambda b, h, i: (b, i, h)),
        scratch_shapes=[pltpu.VMEM((2, tq, tq), f32), pltpu.VMEM((2, tq, tq), bf16)] * n_heads,
        compiler_params=_params("arbitrary", "arbitrary", "arbitrary"),
        name="fox",
    )(fq, fq, fk, fvt, cin)


def _cmp_kernel(q_ref, k_ref, v_ref, gate_ref, ov_ref, o_ref, sel_ref, *, tq, n_rows, n_cmp, n_sel):
    i = pl.program_id(2)
    g = pl.program_id(1)
    hg = NSA_GROUP
    q = q_ref[0].reshape(hg * tq, HEAD_DIM)
    s = _dot_nt(q, k_ref[0, 0]).reshape(hg, tq, n_rows)
    t = i * tq + lax.broadcasted_iota(jnp.int32, (1, tq, 1), 1)
    n = lax.broadcasted_iota(jnp.int32, (1, 1, n_rows), 2)
    last = jnp.minimum(lax.shift_right_arithmetic(t - (CMP_LEN - 1), CMP_SHIFT), n_cmp - 1)
    s = jnp.where(n <= last, s, -jnp.inf)
    m = jnp.max(s, axis=-1, keepdims=True)
    m = jnp.where(m == -jnp.inf, 0.0, m)
    e = jnp.exp2(s - m)
    p = e * (1.0 / jnp.maximum(jnp.sum(e, axis=-1, keepdims=True), 1e-30))
    o = _dot(p.reshape(hg * tq, n_rows).astype(bf16), v_ref[0, 0])
    _write_gated(o_ref, gate_ref, o, g, 0, tq)

    lane = lax.broadcasted_iota(jnp.int32, (tq, LANES), 1)
    tq_pos = i * tq + lax.broadcasted_iota(jnp.int32, (tq, 1), 0)
    q_blk = lax.shift_right_logical(tq_pos, SEL_SHIFT)
    causal = lane <= q_blk
    n_top = min(SEL_TOPK, n_sel)

    @pl.when((i + 1) * tq <= n_top * SEL_BLOCK)
    def _():
        sel_ref[0, 0] = jnp.where(causal, 0.0, -MASK_BIG).astype(bf16)

    @pl.when((i + 1) * tq > n_top * SEL_BLOCK)
    def _():
        psum = p[0] + p[1] + p[2] + p[3]
        imp = _dot_f32_by_exact(psum, ov_ref[...])
        forced = (lane == 0) | (lane == q_blk) | (lane == q_blk - 1)
        key = jnp.where(forced, -2.0, jnp.where(causal, imp, -1.0))
        key = jnp.where(lane < n_sel, key, -3.0)
        lane_f = lane.astype(f32)
        sel = forced
        for _ in range(n_top - 3):
            mx = jnp.max(key, axis=-1, keepdims=True)
            first = jnp.min(jnp.where(key == mx, lane_f, float(LANES)), axis=-1, keepdims=True)
            pick = lane_f == first
            sel = sel | pick
            key = jnp.where(pick, -2.0, key)
        sel = ((q_blk < n_top) | sel) & causal
        sel_ref[0, 0] = jnp.where(sel, 0.0, -MASK_BIG).astype(bf16)


def _cmp(nqu, kcmp, vcmp, gates, n_cmp):
    n_batch, _, seq_len, _ = nqu.shape
    n_rows = kcmp.shape[2]
    n_sel = seq_len // SEL_BLOCK
    assert 3 <= n_sel <= SEL_LANES
    tq = 1024
    c0 = np.arange(n_rows) * CMP_STRIDE
    s0 = np.arange(SEL_LANES) * SEL_BLOCK
    overlap = np.clip(np.minimum(c0[:, None] + CMP_LEN, s0[None, :] + SEL_BLOCK)
                      - np.maximum(c0[:, None], s0[None, :]), 0, None).astype(np.float32) / CMP_LEN
    overlap[n_cmp:, :] = 0.0
    overlap[:, n_sel:] = 0.0
    return pl.pallas_call(
        functools.partial(_cmp_kernel, tq=tq, n_rows=n_rows, n_cmp=n_cmp, n_sel=n_sel),
        out_shape=[
            jax.ShapeDtypeStruct((n_batch, seq_len, NSA_W), bf16),
            jax.ShapeDtypeStruct((n_batch, NSA_KV_HEADS, seq_len, SEL_LANES), bf16),
        ],
        grid=(n_batch, NSA_KV_HEADS, seq_len // tq),
        in_specs=[
            pl.BlockSpec((1, NSA_GROUP, tq, HEAD_DIM), lambda b, g, i: (b, g, i, 0)),
            pl.BlockSpec((1, 1, n_rows, HEAD_DIM), lambda b, g, i: (b, g, 0, 0)),
            pl.BlockSpec((1, 1, n_rows, HEAD_DIM), lambda b, g, i: (b, g, 0, 0)),
            pl.BlockSpec((1, tq, LANES), lambda b, g, i: (b, i, 0)),
            pl.BlockSpec((n_rows, SEL_LANES), lambda b, g, i: (0, 0)),
        ],
        out_specs=[
            pl.BlockSpec((1, tq, NSA_GROUP * HEAD_DIM), lambda b, g, i: (b, i, g)),
            pl.BlockSpec((1, 1, tq, SEL_LANES), lambda b, g, i: (b, g, i, 0)),
        ],
        compiler_params=_params("arbitrary", "arbitrary", "arbitrary"),
        name="cmp",
    )(nqu, kcmp, vcmp, gates, jnp.asarray(overlap, dtype=bf16))


def _write_gated(o_ref, gate_ref, out, g, branch, tq):
    gates = gate_ref[0]
    lane = lax.broadcasted_iota(jnp.int32, (tq, LANES), 1)
    for h in range(NSA_GROUP):
        gcol = FOX_HEADS + (g * NSA_GROUP + h) * N_BRANCH + branch
        gh = jnp.sum(jnp.where(lane == gcol, gates, 0.0), axis=-1, keepdims=True)
        o_ref[0, :, h * HEAD_DIM:(h + 1) * HEAD_DIM] = (gh * out[h * tq:(h + 1) * tq]).astype(bf16)


def _slc_kernel(q_ref, sel_ref, qn_ref, seln_ref, k_ref, vt_ref, gate_ref, *rest, tq, tk, n_cast_w):
    o_ref = rest[n_cast_w]
    bufs = rest[2 * n_cast_w + 1:]
    for w32_ref, w16_ref in zip(rest[:n_cast_w], rest[n_cast_w + 1:2 * n_cast_w + 1]):
        w16_ref[...] = w32_ref[...].astype(bf16)
    i = pl.program_id(1)
    hg = NSA_GROUP
    n_q = hg * tq
    q0 = i * tq
    jd = q0 // tk

    def diag_mask(s):
        kp = jd * tk + lax.broadcasted_iota(jnp.int32, (tk, n_q), 0)
        t = q0 + (lax.broadcasted_iota(jnp.int32, (tk, n_q), 1) & (tq - 1))
        return jnp.where(kp <= t, s, -jnp.inf)

    def chain(g):
        def q_aug(qr, sr):
            q = qr[0, g * hg:(g + 1) * hg].reshape(n_q, HEAD_DIM)
            return jnp.concatenate([q, jnp.concatenate([sr[0, g]] * hg, axis=0)], axis=1)

        qa = q_aug(q_ref, sel_ref)

        def scores(j):
            return _dot_nt(k_ref[0, g, pl.ds(pl.multiple_of(j * tk, tk), tk), :], qa)

        def next_first():
            return _dot_nt(k_ref[0, g, 0:tk, :], q_aug(qn_ref, seln_ref))

        def emit(out):
            gates = gate_ref[0]
            lane = lax.broadcasted_iota(jnp.int32, (tq, LANES), 1)
            for h in range(hg):
                head = g * hg + h
                gh = jnp.sum(jnp.where(lane == FOX_HEADS + head * N_BRANCH + 1, gates, 0.0),
                             axis=-1, keepdims=True)
                o_ref[0, :, head * HEAD_DIM:(head + 1) * HEAD_DIM] = (
                    gh * out[:, h * tq:(h + 1) * tq].T).astype(bf16)

        return _Chain(scores, lambda j: vt_ref[0, g, j], lambda j: None, diag_mask, emit, next_first,
                      bufs[2 * g], bufs[2 * g + 1])

    _flash_sweep(jd, [chain(g) for g in range(NSA_KV_HEADS)], n_q, i == 0)


def _slc(nqr, selneg, kaug, vst, gates, cast_weights):
    n_batch, _, seq_len, _ = nqr.shape
    tq, tk = 256, KV_BLOCK
    nkb = seq_len // tk
    assert tq & (tq - 1) == 0
    n_q = NSA_GROUP * tq
    n_i = seq_len // tq
    cast_in, cast_out, cast_shapes = _cast_side_job(cast_weights, n_batch * n_i, lambda b, i: b * n_i + i)
    return pl.pallas_call(
        functools.partial(_slc_kernel, tq=tq, tk=tk, n_cast_w=len(cast_weights)),
        out_shape=[jax.ShapeDtypeStruct((n_batch, seq_len, NSA_W), bf16)] + cast_shapes,
        grid=(n_batch, n_i),
        in_specs=[
            pl.BlockSpec((1, NSA_HEADS, tq, HEAD_DIM), lambda b, i: (b, 0, i, 0)),
            pl.BlockSpec((1, NSA_KV_HEADS, tq, SEL_LANES), lambda b, i: (b, 0, i, 0)),
            pl.BlockSpec((1, NSA_HEADS, tq, HEAD_DIM), lambda b, i: (b, 0, jnp.minimum(i + 1, n_i - 1), 0)),
            pl.BlockSpec((1, NSA_KV_HEADS, tq, SEL_LANES), lambda b, i: (b, 0, jnp.minimum(i + 1, n_i - 1), 0)),
            pl.BlockSpec((1, NSA_KV_HEADS, seq_len, HEAD_DIM + SEL_LANES), lambda b, i: (b, 0, 0, 0)),
            pl.BlockSpec((1, NSA_KV_HEADS, nkb, HEAD_DIM, tk), lambda b, i: (b, 0, 0, 0, 0)),
            pl.BlockSpec((1, tq, LANES), lambda b, i: (b, i, 0)),
        ] + cast_in,
        out_specs=[pl.BlockSpec((1, tq, NSA_W), lambda b, i: (b, i, 0))] + cast_out,
        scratch_shapes=[pltpu.VMEM((2, tk, n_q), f32), pltpu.VMEM((2, tk, n_q), bf16)] * NSA_KV_HEADS,
        compiler_params=_params("arbitrary", "arbitrary"),
        name="slc",
    )(nqr, selneg, nqr, selneg, kaug, vst, gates, *cast_weights)


def _win_kernel(q_ref, k_ref, vt_ref, gate_ref, o_ref, *, tq, n_sub):
    i = pl.program_id(2)
    g = pl.program_id(1)
    hg = NSA_GROUP
    span = WINDOW + tq
    q0s = [(i * n_sub + u) * tq for u in range(n_sub)]
    k0s = [pl.multiple_of(jnp.maximum(q0 - WINDOW, 0), tq) for q0 in q0s]
    scores = []
    for u in range(n_sub):
        q = q_ref[0, :, u * tq:(u + 1) * tq, :].reshape(hg * tq, HEAD_DIM)
        scores.append(_dot_nt(k_ref[0, 0, pl.ds(k0s[u], span), :], q))
    probs, denoms = [], []
    for u in range(n_sub):
        kp = k0s[u] + lax.broadcasted_iota(jnp.int32, (span, tq), 0)
        t = q0s[u] + lax.broadcasted_iota(jnp.int32, (span, tq), 1)
        diff = t - kp
        bias = jnp.where((diff >= 0) & (diff < WINDOW), 0.0, -jnp.inf)
        s = jnp.concatenate([scores[u][:, h * tq:(h + 1) * tq] + bias for h in range(hg)], axis=1)
        e = jnp.exp2(s - jnp.max(s, axis=0, keepdims=True))
        denoms.append(jnp.sum(e, axis=0, keepdims=True))
        probs.append(e.astype(bf16))
    gates = gate_ref[0]
    lane = lax.broadcasted_iota(jnp.int32, (tq, LANES), 1)
    for u in range(n_sub):
        jb = k0s[u] // tq
        vt = jnp.concatenate([vt_ref[0, 0, jb + c] for c in range(span // tq)], axis=1)
        out = _dot(vt, probs[u]) / denoms[u]
        for h in range(hg):
            gcol = FOX_HEADS + (g * hg + h) * N_BRANCH + 2
            gh = jnp.sum(jnp.where(lane == gcol, gates[u * tq:(u + 1) * tq], 0.0), axis=-1, keepdims=True)
            o_ref[0, u * tq:(u + 1) * tq, h * HEAD_DIM:(h + 1) * HEAD_DIM] = (
                gh * out[:, h * tq:(h + 1) * tq].T).astype(bf16)


def _win(nqr, kw, vwt, gates):
    n_batch, _, seq_len, _ = nqr.shape
    tq, n_sub = WIN_TILE, 8
    assert seq_len >= WINDOW + tq and WINDOW % tq == 0
    return pl.pallas_call(
        functools.partial(_win_kernel, tq=tq, n_sub=n_sub),
        out_shape=jax.ShapeDtypeStruct((n_batch, seq_len, NSA_W), bf16),
        grid=(n_batch, NSA_KV_HEADS, seq_len // (tq * n_sub)),
        in_specs=[
            pl.BlockSpec((1, NSA_GROUP, tq * n_sub, HEAD_DIM), lambda b, g, i: (b, g, i, 0)),
            pl.BlockSpec((1, 1, seq_len, HEAD_DIM), lambda b, g, i: (b, g, 0, 0)),
            pl.BlockSpec((1, 1, seq_len // tq, HEAD_DIM, tq), lambda b, g, i: (b, g, 0, 0, 0)),
            pl.BlockSpec((1, tq * n_sub, LANES), lambda b, g, i: (b, i, 0)),
        ],
        out_specs=pl.BlockSpec((1, tq * n_sub, NSA_GROUP * HEAD_DIM), lambda b, g, i: (b, i, g)),
        compiler_params=_params("arbitrary", "arbitrary", "arbitrary"),
        name="win",
    )(nqr, kw, vwt, gates)


def _outproj_kernel(fox_ref, c_ref, s_ref, w_ref, x_ref, mod_ref, wo_ref, o_ref):
    mix = _dot(fox_ref[...], wo_ref[0:FOX_W, :])
    nsa = c_ref[...].astype(f32) + s_ref[...].astype(f32) + w_ref[...].astype(f32)
    mix = mix + _dot(nsa.astype(bf16), wo_ref[FOX_W:FOX_W + NSA_W, :])
    o_ref[...] = x_ref[...] + mod_ref[0][2:3] * mix


def _outproj(ofox, ocmp, oslc, owin, x2d, mod3, wo_bf16, seq_len):
    m_rows = x2d.shape[0]
    tm = 512
    per_b = seq_len // tm
    half = pl.BlockSpec((tm, FOX_W), lambda i: (i, 0))
    return pl.pallas_call(
        _outproj_kernel,
        out_shape=jax.ShapeDtypeStruct((m_rows, D_MODEL), f32),
        grid=(m_rows // tm,),
        in_specs=[
            half, half, half, half,
            pl.BlockSpec((tm, D_MODEL), lambda i: (i, 0)),
            pl.BlockSpec((1, 6, D_MODEL), lambda i: (i // per_b, 0, 0)),
            pl.BlockSpec((D_MODEL, D_MODEL), lambda i: (0, 0)),
        ],
        out_specs=pl.BlockSpec((tm, D_MODEL), lambda i: (i, 0)),
        compiler_params=_params("arbitrary"),
        name="outproj",
    )(ofox, ocmp, oslc, owin, x2d, mod3, wo_bf16)


def _mlp_kernel(x_ref, mod_ref, g_ref, wu_ref, wd_ref, o_ref, h_ref, acc_ref):
    f = pl.program_id(1)

    @pl.when(f == 0)
    def _():
        md = mod_ref[0]
        y = _rms(x_ref[...], g_ref[...])
        h_ref[...] = (y * (1.0 + md[4:5]) + md[3:4]).astype(bf16)
        acc_ref[...] = jnp.zeros_like(acc_ref)

    u = jnp.maximum(_dot(h_ref[...], wu_ref[...]), 0.0)
    acc_ref[...] += _dot((u * u).astype(bf16), wd_ref[...])

    @pl.when(f == pl.num_programs(1) - 1)
    def _():
        o_ref[...] = x_ref[...] + mod_ref[0][5:6] * acc_ref[...]


def _mlp(x2d, mod3, norm_g, wu_bf16, wd_bf16, seq_len):
    m_rows = x2d.shape[0]
    tm, tf = 1024, 512
    per_b = seq_len // tm
    return pl.pallas_call(
        _mlp_kernel,
        out_shape=jax.ShapeDtypeStruct((m_rows, D_MODEL), f32),
        grid=(m_rows // tm, D_FF // tf),
        in_specs=[
            pl.BlockSpec((tm, D_MODEL), lambda i, f: (i, 0)),
            pl.BlockSpec((1, 6, D_MODEL), lambda i, f: (i // per_b, 0, 0)),
            pl.BlockSpec((1, D_MODEL), lambda i, f: (0, 0)),
            pl.BlockSpec((D_MODEL, tf), lambda i, f: (0, f)),
            pl.BlockSpec((tf, D_MODEL), lambda i, f: (f, 0)),
        ],
        out_specs=pl.BlockSpec((tm, D_MODEL), lambda i, f: (i, 0), pipeline_mode=pl.Buffered(1)),
        scratch_shapes=[pltpu.VMEM((tm, D_MODEL), bf16), pltpu.VMEM((tm, D_MODEL), f32)],
        compiler_params=pltpu.CompilerParams(dimension_semantics=("arbitrary", "arbitrary"),
                                             vmem_limit_bytes=VMEM_LIMIT_PREP),
        name="mlp",
    )(x2d, mod3, norm_g, wu_bf16, wd_bf16)


def _layer(x, c, w_ada, b_ada, norm1_g, w_in, b_forget, fox_q_norm, fox_k_norm, nsa_q_norm,
           cmp_k_norm, slc_k_norm, win_k_norm, cmp_pe_k, cmp_w1_k, cmp_w2_k, cmp_pe_v, cmp_w1_v,
           cmp_w2_v, w_out, norm2_g, w_up, w_down):
    n_batch, seq_len, _ = x.shape
    n_cmp = (seq_len - CMP_LEN) // CMP_STRIDE + 1
    row = lambda v: v.reshape(1, -1)

    half = HEAD_DIM // 2
    inv_freq = ROPE_THETA ** (-jnp.arange(half, dtype=f32) / half)
    inv_freq = jnp.concatenate([inv_freq, inv_freq]).reshape(1, HEAD_DIM)

    w_a, w_b, w_s = _repack_w_in(w_in.T)

    mod3 = _ada(c, w_ada, b_ada).reshape(n_batch, 6, D_MODEL)
    x2d = x.reshape(n_batch * seq_len, D_MODEL)
    (fq, fk, fvt, cin, nqu, nqr, kc, vc, kaug, vst, kw, vw, gates) = _prep(
        x2d, mod3, row(norm1_g), w_a, w_b, w_s, n_batch, seq_len, row(fox_q_norm), row(fox_k_norm), row(nsa_q_norm), row(slc_k_norm),
        row(win_k_norm), jnp.pad(b_forget, (0, LANES - FOX_HEADS)).reshape(1, LANES), inv_freq)
    kcmp = _compress(kc, cmp_pe_k, cmp_w1_k.astype(bf16), cmp_w2_k.astype(bf16), row(cmp_k_norm))
    vcmp = _compress(vc, cmp_pe_v, cmp_w1_v.astype(bf16), cmp_w2_v.astype(bf16))
    ofox = _fox(fq, fk, fvt, cin)
    ocmp, selneg = _cmp(nqu, kcmp, vcmp, gates, n_cmp)
    oslc, w_up16, w_down16, w_out16 = _slc(nqr, selneg, kaug, vst, gates, [w_up, w_down, w_out])
    owin = _win(nqr, kw, vw, gates)
    x1 = _outproj(ofox.reshape(-1, FOX_W), ocmp.reshape(-1, NSA_W), oslc.reshape(-1, NSA_W),
                  owin.reshape(-1, NSA_W), x2d, mod3, w_out16, seq_len)
    x2 = _mlp(x1, mod3, row(norm2_g), w_up16, w_down16, seq_len)
    return x2.reshape(n_batch, seq_len, D_MODEL)


def kernel(x, c, w_ada, b_ada, norm1_g, w_in, b_forget, fox_q_norm, fox_k_norm, nsa_q_norm, cmp_k_norm,
           slc_k_norm, win_k_norm, cmp_pe_k, cmp_w1_k, cmp_w2_k, cmp_pe_v, cmp_w1_v, cmp_w2_v, w_out,
           norm2_g, w_up, w_down):
    depth = w_ada.shape[0]
    for l in range(depth):
        x = _layer(x, c, w_ada[l], b_ada[l], norm1_g[l], w_in[l], b_forget[l], fox_q_norm[l], fox_k_norm[l],
                   nsa_q_norm[l], cmp_k_norm[l], slc_k_norm[l], win_k_norm[l], cmp_pe_k[l], cmp_w1_k[l],
                   cmp_w2_k[l], cmp_pe_v[l], cmp_w1_v[l], cmp_w2_v[l], w_out[l], norm2_g[l], w_up[l],
                   w_down[l])
    return x
```

```python
import functools
import math
from typing import Any, Callable, NamedTuple

import numpy as np
import jax
import jax.numpy as jnp
from jax import lax
from jax.experimental import pallas as pl
from jax.experimental.pallas import tpu as pltpu

D_MODEL = 2048
HEAD_DIM = 128
FOX_HEADS = 8
NSA_HEADS = 8
NSA_KV_HEADS = 2
NSA_GROUP = NSA_HEADS // NSA_KV_HEADS
N_BRANCH = 3
D_FF = 4 * D_MODEL
ROPE_THETA = 10000.0
CMP_LEN = 32
CMP_STRIDE = 16
CMP_SHIFT = 4
CMP_HIDDEN = 2 * HEAD_DIM
SEL_BLOCK = 64
SEL_SHIFT = 6
SEL_TOPK = 16
WINDOW = 512
NORM_EPS = 1e-6
ATTN_SCALE = HEAD_DIM ** -0.5
FOX_W = FOX_HEADS * HEAD_DIM
NSA_W = NSA_HEADS * HEAD_DIM
KV_W = NSA_KV_HEADS * HEAD_DIM

LANES = 128
F32_SUBLANES = 8
BF16_SUBLANES = 16
SEL_LANES = LANES
MASK_BIG = 1e30
KV_BLOCK = 512
WIN_TILE = 128
LOG2E = math.log2(math.e)

COL_FQ = 0
COL_FK = COL_FQ + FOX_W
COL_FV = COL_FK + FOX_W
COL_NQ = COL_FV + FOX_W
COL_KC = COL_NQ + NSA_W
COL_VC = COL_KC + KV_W
COL_KS = COL_VC + KV_W
COL_VS = COL_KS + KV_W
COL_KW = COL_VS + KV_W
COL_VW = COL_KW + KV_W
COL_SMALL = COL_VW + KV_W
W_IN_Z0 = 3 * FOX_W
W_IN_NQ0 = W_IN_Z0 + FOX_HEADS
W_IN_GZ0 = W_IN_NQ0 + NSA_W + 6 * KV_W

VMEM_LIMIT = 56 * 1024 * 1024
VMEM_LIMIT_PREP = 60 * 1024 * 1024

f32 = jnp.float32
bf16 = jnp.bfloat16


def _params(*sem):
    return pltpu.CompilerParams(dimension_semantics=sem, vmem_limit_bytes=VMEM_LIMIT)


def _dot_nt(a, b):
    return lax.dot_general(a, b, (((1,), (1,)), ((), ())), preferred_element_type=f32)


def _dot(a, b):
    return jnp.dot(a, b, preferred_element_type=f32)


def _split3(x):
    hi = x.astype(bf16)
    r1 = x - hi.astype(f32)
    mid = r1.astype(bf16)
    lo = (r1 - mid.astype(f32)).astype(bf16)
    return hi, mid, lo


def _dot_f32_by_exact(x, w_bf16):
    hi, mid, lo = _split3(x)
    return _dot(hi, w_bf16) + (_dot(mid, w_bf16) + _dot(lo, w_bf16))


def _rms(x, gain):
    ms = jnp.mean(x * x, axis=-1, keepdims=True)
    return x * lax.rsqrt(ms + NORM_EPS) * gain


def _ada_kernel(ct_ref, w_ref, b_ref, o_ref, *, n_batch, k_chunk):
    ct = ct_ref[...]
    act = ct * jax.nn.sigmoid(ct)
    rows = []
    for b in range(n_batch):
        col = act[:, b:b + 1]
        acc = b_ref[...]
        for k0 in range(0, D_MODEL, k_chunk):
            acc = acc + jnp.sum(w_ref[k0:k0 + k_chunk, :] * col[k0:k0 + k_chunk], axis=0, keepdims=True)
        rows.append(acc)
    o_ref[...] = jnp.concatenate(rows, axis=0)


def _ada(c, w_ada, b_ada):
    n_batch = c.shape[0]
    n_out = w_ada.shape[1]
    tn = 1024
    return pl.pallas_call(
        functools.partial(_ada_kernel, n_batch=n_batch, k_chunk=256),
        out_shape=jax.ShapeDtypeStruct((n_batch, n_out), f32),
        grid=(n_out // tn,),
        in_specs=[
            pl.BlockSpec((D_MODEL, n_batch), lambda j: (0, 0)),
            pl.BlockSpec((D_MODEL, tn), lambda j: (0, j)),
            pl.BlockSpec((1, tn), lambda j: (0, j)),
        ],
        out_specs=pl.BlockSpec((n_batch, tn), lambda j: (0, j)),
        compiler_params=_params("arbitrary"),
        name="ada",
    )(c.T, w_ada, b_ada.reshape(1, n_out))


def _repack_kernel(a_ref, *rest, n_part):
    part_refs, (z_ref, gz_ref, wa_ref, wb_ref, ws_ref) = rest[:n_part], rest[n_part:]
    wa_ref[...] = a_ref[...].astype(bf16)
    wb_ref[...] = jnp.concatenate([r[...] for r in part_refs], axis=0).astype(bf16)
    pad = jnp.zeros((LANES - z_ref.shape[0] - gz_ref.shape[0], D_MODEL), f32)
    ws_ref[...] = jnp.concatenate([z_ref[...], gz_ref[...], pad], axis=0).astype(bf16)


def _repack_w_in(w_t):
    n_steps = 16
    ra, rb = W_IN_Z0 // n_steps, (W_IN_GZ0 - W_IN_NQ0) // n_steps
    hb = math.gcd(W_IN_NQ0, rb)
    n_part = rb // hb
    n_z, n_gz = W_IN_NQ0 - W_IN_Z0, w_t.shape[0] - W_IN_GZ0
    assert W_IN_Z0 % n_steps == 0 and (W_IN_GZ0 - W_IN_NQ0) % n_steps == 0 and ra % BF16_SUBLANES == 0 and rb % BF16_SUBLANES == 0
    assert hb % F32_SUBLANES == 0 and W_IN_Z0 % n_z == 0 and W_IN_GZ0 % n_gz == 0
    b_off = W_IN_NQ0 // hb
    part = lambda k: pl.BlockSpec((hb, D_MODEL), lambda j: (b_off + n_part * j + k, 0))
    return pl.pallas_call(
        functools.partial(_repack_kernel, n_part=n_part),
        out_shape=[jax.ShapeDtypeStruct((W_IN_Z0, D_MODEL), bf16),
                   jax.ShapeDtypeStruct((W_IN_GZ0 - W_IN_NQ0, D_MODEL), bf16),
                   jax.ShapeDtypeStruct((LANES, D_MODEL), bf16)],
        grid=(n_steps,),
        in_specs=[pl.BlockSpec((ra, D_MODEL), lambda j: (j, 0))] + [part(k) for k in range(n_part)] + [
            pl.BlockSpec((n_z, D_MODEL), lambda j: (W_IN_Z0 // n_z, 0)),
            pl.BlockSpec((n_gz, D_MODEL), lambda j: (W_IN_GZ0 // n_gz, 0)),
        ],
        out_specs=[pl.BlockSpec((ra, D_MODEL), lambda j: (j, 0)),
                   pl.BlockSpec((rb, D_MODEL), lambda j: (j, 0)),
                   pl.BlockSpec((LANES, D_MODEL), lambda j: (0, 0))],
        compiler_params=_params("arbitrary"),
        name="repack",
    )(*([w_t] * (n_part + 3)))


def _cast_side_job(weights, n_steps, step_of):
    n_cast = max(c for c in range(1, n_steps + 1) if all(w.shape[0] % (BF16_SUBLANES * c) == 0 for w in weights))
    idx = lambda *ids: (jnp.minimum(step_of(*ids), n_cast - 1), 0)
    specs = [pl.BlockSpec((w.shape[0] // n_cast, w.shape[1]), idx) for w in weights]
    return specs, specs, [jax.ShapeDtypeStruct(w.shape, bf16) for w in weights]


def _prep_kernel(x_ref, mod_ref, g1_ref, wa_ref, wb_ref, ws_ref,
                 gq_ref, gk_ref, gn_ref, gs_ref, gw_ref, bf_ref, inv_ref,
                 fq_ref, fk_ref, fvt_ref, cin_ref, nqu_ref, nqr_ref, kc_ref, vc_ref,
                 ks_ref, vst_ref, kw_ref, vw_ref, gate_ref, carry_ref, rot_ref, *, tm):
    i = pl.program_id(1)

    @pl.when(i == 0)
    def _():
        carry_ref[...] = jnp.zeros_like(carry_ref)

    md = mod_ref[0]
    hn = (_rms(x_ref[...], g1_ref[...]) * (1.0 + md[1:2]) + md[0:1]).astype(bf16)
    groups = {}

    def head(col, h):
        base, w_ref, r0 = next((b, w, r) for b, w, r in (
            (COL_KC, wb_ref, NSA_W), (COL_NQ, wb_ref, 0), (COL_FV, wa_ref, 2 * FOX_W),
            (COL_FK, wa_ref, FOX_W), (COL_FQ, wa_ref, 0)) if col >= b)
        if base not in groups:
            n = (COL_SMALL - COL_KC) if base == COL_KC else FOX_W
            groups[base] = _dot_nt(hn, w_ref[r0:r0 + n, :])
        c0 = col - base + h * HEAD_DIM
        return groups[base][:, c0:c0 + HEAD_DIM]

    row = lax.broadcasted_iota(jnp.int32, (tm, LANES), 0)
    lane = lax.broadcasted_iota(jnp.int32, (tm, LANES), 1)
    pos = i * tm + row

    @pl.when((pl.program_id(0) == 0) & (i == 0))
    def _():
        ang_row = row.astype(f32) * inv_ref[...]
        rot_ref[0] = jnp.cos(ang_row)
        rot_ref[1] = jnp.sin(ang_row)

    ang0 = (i * tm).astype(f32) * inv_ref[...]
    cos0, sin0 = jnp.cos(ang0), jnp.sin(ang0)
    cos = cos0 * rot_ref[0] - sin0 * rot_ref[1]
    sin = sin0 * rot_ref[0] + cos0 * rot_ref[1]
    sin_signed = jnp.where(lane < HEAD_DIM // 2, -sin, sin)

    def rope(x):
        return x * cos + pltpu.roll(x, HEAD_DIM // 2, 1) * sin_signed

    small = _dot_nt(hn, ws_ref[...])
    z = small + bf_ref[...]
    logf = jnp.minimum(z, 0.0) - jnp.log1p(jnp.exp(-jnp.abs(z)))
    t_idx = lax.broadcasted_iota(jnp.int32, (tm, tm), 0)
    s_idx = lax.broadcasted_iota(jnp.int32, (tm, tm), 1)
    tri = jnp.where(s_idx <= t_idx, 1.0, 0.0).astype(bf16)
    hi, mid, lo = _split3(logf)
    local = _dot(tri, hi) + (_dot(tri, mid) + _dot(tri, lo))
    cin_ref[0, 0] = carry_ref[0:1, :]
    carry_ref[...] = carry_ref[...] + local[tm - 1:tm, :]
    b_hi, b_mid, b_lo = (v.astype(f32) for v in _split3(local * (-LOG2E)))
    ones3 = jnp.where(lane < 3, 1.0, 0.0).astype(bf16)

    for h in range(FOX_HEADS):
        q = _rms(head(COL_FQ, h), gq_ref[...] * (ATTN_SCALE * LOG2E))
        fq_ref[0, h] = jnp.concatenate([q.astype(bf16), ones3], axis=1)
    for h in range(FOX_HEADS):
        k = _rms(head(COL_FK, h), gk_ref[...])
        bias = jnp.where(lane == 0, b_hi[:, h:h + 1],
                         jnp.where(lane == 1, b_mid[:, h:h + 1],
                                   jnp.where(lane == 2, b_lo[:, h:h + 1], 0.0)))
        fk_ref[0, h] = jnp.concatenate([k.astype(bf16), bias.astype(bf16)], axis=1)

    gate_ref[0] = jax.nn.sigmoid(small)

    for h in range(NSA_HEADS):
        qn = _rms(head(COL_NQ, h), gn_ref[...] * (ATTN_SCALE * LOG2E))
        nqu_ref[0, h] = qn.astype(bf16)
        nqr_ref[0, h] = rope(qn).astype(bf16)
    onehot = jnp.where(lane == lax.shift_right_logical(pos, SEL_SHIFT), 1.0, 0.0).astype(bf16)
    for g in range(NSA_KV_HEADS):
        kc_ref[0, g] = head(COL_KC, g)
        vc_ref[0, g] = head(COL_VC, g)
        ks = rope(_rms(head(COL_KS, g), gs_ref[...])).astype(bf16)
        ks_ref[0, g] = jnp.concatenate([ks, onehot], axis=1)
        vst_ref[0, g, 0] = head(COL_VS, g).T.astype(bf16)
        kw_ref[0, g] = rope(_rms(head(COL_KW, g), gw_ref[...])).astype(bf16)
        vwt = head(COL_VW, g).T.astype(bf16)
        for c in range(tm // WIN_TILE):
            vw_ref[0, g, c] = vwt[:, c * WIN_TILE:(c + 1) * WIN_TILE]
    for h in range(FOX_HEADS):
        fvt_ref[0, h, 0] = head(COL_FV, h).T.astype(bf16)


def _prep(x2d, mod3, norm_g, wa, wb, ws, n_batch, seq_len, gq, gk, gn, gs, gw, b_forget_row, inv_freq):
    resident = lambda w: pl.BlockSpec(w.shape, lambda b, i: (0, 0), pipeline_mode=pl.Buffered(1))
    tm = KV_BLOCK
    per_b = seq_len // tm
    hshape = lambda n, w, dt: jax.ShapeDtypeStruct((n_batch, n, seq_len, w), dt)
    hspec = lambda n, w: pl.BlockSpec((1, n, tm, w), lambda b, i: (b, 0, i, 0))
    tshape = lambda n: jax.ShapeDtypeStruct((n_batch, n, per_b, HEAD_DIM, tm), bf16)
    tspec = lambda n: pl.BlockSpec((1, n, 1, HEAD_DIM, tm), lambda b, i: (b, 0, i, 0, 0))
    vec = pl.BlockSpec((1, LANES), lambda b, i: (0, 0))
    return pl.pallas_call(
        functools.partial(_prep_kernel, tm=tm),
        out_shape=[
            hshape(FOX_HEADS, 2 * HEAD_DIM, bf16), hshape(FOX_HEADS, 2 * HEAD_DIM, bf16), tshape(FOX_HEADS),
            jax.ShapeDtypeStruct((n_batch, per_b, 1, LANES), f32),
            hshape(NSA_HEADS, HEAD_DIM, bf16), hshape(NSA_HEADS, HEAD_DIM, bf16),
            hshape(NSA_KV_HEADS, HEAD_DIM, f32), hshape(NSA_KV_HEADS, HEAD_DIM, f32),
            hshape(NSA_KV_HEADS, HEAD_DIM + SEL_LANES, bf16), tshape(NSA_KV_HEADS),
            hshape(NSA_KV_HEADS, HEAD_DIM, bf16),
            jax.ShapeDtypeStruct((n_batch, NSA_KV_HEADS, seq_len // WIN_TILE, HEAD_DIM, WIN_TILE), bf16),
            jax.ShapeDtypeStruct((n_batch, seq_len, LANES), f32),
        ],
        grid=(n_batch, per_b),
        in_specs=[
            pl.BlockSpec((tm, D_MODEL), lambda b, i: (b * per_b + i, 0)),
            pl.BlockSpec((1, 6, D_MODEL), lambda b, i: (b, 0, 0)),
            pl.BlockSpec((1, D_MODEL), lambda b, i: (0, 0)),
            resident(wa), resident(wb), resident(ws),
            vec, vec, vec, vec, vec, vec, vec,
        ],
        out_specs=[
            hspec(FOX_HEADS, 2 * HEAD_DIM), hspec(FOX_HEADS, 2 * HEAD_DIM), tspec(FOX_HEADS),
            pl.BlockSpec((1, 1, 1, LANES), lambda b, i: (b, i, 0, 0)),
            hspec(NSA_HEADS, HEAD_DIM), hspec(NSA_HEADS, HEAD_DIM),
            hspec(NSA_KV_HEADS, HEAD_DIM), hspec(NSA_KV_HEADS, HEAD_DIM),
            hspec(NSA_KV_HEADS, HEAD_DIM + SEL_LANES), tspec(NSA_KV_HEADS),
            hspec(NSA_KV_HEADS, HEAD_DIM),
            pl.BlockSpec((1, NSA_KV_HEADS, tm // WIN_TILE, HEAD_DIM, WIN_TILE), lambda b, i: (b, 0, i, 0, 0)),
            pl.BlockSpec((1, tm, LANES), lambda b, i: (b, i, 0)),
        ],
        scratch_shapes=[pltpu.VMEM((8, LANES), f32), pltpu.VMEM((2, tm, LANES), f32)],
        compiler_params=pltpu.CompilerParams(dimension_semantics=("arbitrary", "arbitrary"),
                                             vmem_limit_bytes=VMEM_LIMIT_PREP),
        name="prep",
    )(x2d, mod3, norm_g, wa, wb, ws, gq, gk, gn, gs, gw, b_forget_row, inv_freq)


def _compress_kernel(x_ref, pe_ref, w1_ref, w2_ref, *rest, n_rows, do_norm):
    o_ref = rest[-1]
    a = jnp.zeros((n_rows, CMP_HIDDEN), f32)
    b = jnp.zeros((n_rows, CMP_HIDDEN), f32)
    for l in range(CMP_STRIDE):
        x = x_ref[0, 0, pl.ds(l, n_rows, stride=CMP_STRIDE), :]
        a = a + _dot((x + pe_ref[l:l + 1, :]).astype(bf16), w1_ref[l * HEAD_DIM:(l + 1) * HEAD_DIM, :])
        lb = CMP_STRIDE + l
        b = b + _dot((x + pe_ref[lb:lb + 1, :]).astype(bf16), w1_ref[lb * HEAD_DIM:(lb + 1) * HEAD_DIM, :])
    pre = a + pltpu.roll(b, n_rows - 1, 0)
    hid = pre * jax.nn.sigmoid(pre)
    out = _dot(hid.astype(bf16), w2_ref[...])
    if do_norm:
        out = _rms(out, rest[0][...])
    o_ref[0, 0] = out.astype(bf16)


def _compress(x4, pe, w1_bf16, w2_bf16, gain=None):
    n_batch, n_g, seq_len, _ = x4.shape
    n_rows = seq_len // CMP_STRIDE
    do_norm = gain is not None
    return pl.pallas_call(
        functools.partial(_compress_kernel, n_rows=n_rows, do_norm=do_norm),
        out_shape=jax.ShapeDtypeStruct((n_batch, n_g, n_rows, HEAD_DIM), bf16),
        grid=(n_batch, n_g),
        in_specs=[
            pl.BlockSpec((1, 1, seq_len, HEAD_DIM), lambda b, g: (b, g, 0, 0)),
            pl.BlockSpec((CMP_LEN, HEAD_DIM), lambda b, g: (0, 0)),
            pl.BlockSpec((CMP_LEN * HEAD_DIM, CMP_HIDDEN), lambda b, g: (0, 0)),
            pl.BlockSpec((CMP_HIDDEN, HEAD_DIM), lambda b, g: (0, 0)),
        ] + ([pl.BlockSpec((1, HEAD_DIM), lambda b, g: (0, 0))] if do_norm else []),
        out_specs=pl.BlockSpec((1, 1, n_rows, HEAD_DIM), lambda b, g: (b, g, 0, 0)),
        compiler_params=_params("arbitrary", "arbitrary"),
        name="compress",
    )(x4, pe, w1_bf16, w2_bf16, *([gain] if do_norm else []))


def _softmax_block(s, d, m_prev, l_prev):
    m_blk = jnp.max(s, axis=0, keepdims=True)
    m_new = jnp.maximum(m_prev, m_blk if d is None else m_blk + d)
    p = jnp.exp2(s - (m_new if d is None else m_new - d))
    alpha = jnp.exp2(m_prev - m_new)
    l_new = alpha * l_prev + jnp.sum(p, axis=0, keepdims=True)
    return m_new, l_new, alpha, p.astype(bf16)


class _Chain(NamedTuple):
    scores: Callable
    values: Callable
    offset: Callable
    diag_mask: Callable
    emit: Callable
    next_first: Callable
    s_buf: Any
    p_buf: Any


def _flash_sweep(n_full, chains, n_q, is_first):
    @pl.when(is_first)
    def _():
        for c in chains:
            c.s_buf[0] = c.scores(0)

    for c in chains:
        c.p_buf[1] = jnp.zeros(c.p_buf.shape[1:], c.p_buf.dtype)

    def accumulate(c, j, slot, alpha, acc):
        return alpha * acc + _dot(c.values(jnp.maximum(j, 0)), c.p_buf[slot])

    def step(j, cur, states):
        out = []
        for c, (m, l, acc, alpha_prev) in zip(chains, states):
            acc = accumulate(c, j - 1, 1 - cur, alpha_prev, acc)
            m, l, alpha, p = _softmax_block(c.s_buf[cur], c.offset(j), m, l)
            c.p_buf[cur] = p
            c.s_buf[1 - cur] = c.scores(j + 1)
            out.append((m, l, acc, alpha))
        return tuple(out)

    def finish(cur, states):
        if cur == 1:
            for c in chains:
                c.s_buf[0] = c.next_first()
        for c, (m, l, acc, alpha_prev) in zip(chains, states):
            acc = accumulate(c, n_full - 1, 1 - cur, alpha_prev, acc)
            m, l, alpha, p = _softmax_block(c.diag_mask(c.s_buf[cur]), None, m, l)
            if cur == 0:
                c.s_buf[0] = c.next_first()
            acc = alpha * acc + _dot(c.values(n_full), p)
            c.emit(acc / l)

    init = tuple((jnp.full((1, n_q), -jnp.inf, f32), jnp.zeros((1, n_q), f32),
                  jnp.zeros((HEAD_DIM, n_q), f32), jnp.ones((1, n_q), f32)) for _ in chains)
    states = lax.fori_loop(0, n_full // 2, lambda jj, st: step(2 * jj + 1, 1, step(2 * jj, 0, st)), init)

    @pl.when(n_full % 2 == 0)
    def _():
        finish(0, states)

    @pl.when(n_full % 2 == 1)
    def _():
        finish(1, step(n_full - 1, 0, states))


def _fox_kernel(q_ref, qn_ref, k_ref, vt_ref, c_ref, o_ref, *bufs, tq, n_heads):
    hb = pl.program_id(1)
    i = pl.program_id(2)
    lane = lax.broadcasted_iota(jnp.int32, (1, LANES), 1)

    def diag_mask(s):
        kk = lax.broadcasted_iota(jnp.int32, (tq, tq), 0)
        qq = lax.broadcasted_iota(jnp.int32, (tq, tq), 1)
        return jnp.where(kk <= qq, s, -jnp.inf)

    def chain(u):
        qa = q_ref[0, u]

        def cin(j):
            return jnp.sum(jnp.where(lane == hb * n_heads + u, c_ref[0, j], 0.0), axis=-1, keepdims=True)

        ci = cin(i)

        def scores(j):
            return _dot_nt(k_ref[0, u, pl.ds(pl.multiple_of(j * tq, tq), tq), :], qa)

        def emit(out_t):
            o_ref[0, :, u * HEAD_DIM:(u + 1) * HEAD_DIM] = out_t.T.astype(bf16)

        def next_first():
            return _dot_nt(k_ref[0, u, 0:tq, :], qn_ref[0, u])

        return _Chain(scores, lambda j: vt_ref[0, u, j], lambda j: (ci - cin(j)) * LOG2E, diag_mask, emit,
                      next_first, bufs[2 * u], bufs[2 * u + 1])

    _flash_sweep(i, [chain(u) for u in range(n_heads)], tq, i == 0)


def _fox(fq, fk, fvt, cin):
    n_batch, n_h, seq_len, _ = fq.shape
    tq = KV_BLOCK
    nkb = seq_len // tq
    n_heads = 4
    return pl.pallas_call(
        functools.partial(_fox_kernel, tq=tq, n_heads=n_heads),
        out_shape=jax.ShapeDtypeStruct((n_batch, seq_len, n_h * HEAD_DIM), bf16),
        grid=(n_batch, n_h // n_heads, nkb),
        in_specs=[
            pl.BlockSpec((1, n_heads, tq, 2 * HEAD_DIM), lambda b, h, i: (b, h, i, 0)),
            pl.BlockSpec((1, n_heads, tq, 2 * HEAD_DIM), lambda b, h, i: (b, h, jnp.minimum(i + 1, nkb - 1), 0)),
            pl.BlockSpec((1, n_heads, seq_len, 2 * HEAD_DIM), lambda b, h, i: (b, h, 0, 0)),
            pl.BlockSpec((1, n_heads, nkb, HEAD_DIM, tq), lambda b, h, i: (b, h, 0, 0, 0)),
            pl.BlockSpec((1, nkb, 1, LANES), lambda b, h, i: (b, 0, 0, 0)),
        ],
        out_specs=pl.BlockSpec((1, tq, n_heads * HEAD_DIM), lambda b, h, i: (b, i, h)),
        scratch_shapes=[pltpu.VMEM((2, tq, tq), f32), pltpu.VMEM((2, tq, tq), bf16)] * n_heads,
        compiler_params=_params("arbitrary", "arbitrary", "arbitrary"),
        name="fox",
    )(fq, fq, fk, fvt, cin)


def _cmp_kernel(q_ref, k_ref, v_ref, gate_ref, ov_ref, o_ref, sel_ref, *, tq, n_rows, n_cmp, n_sel):
    i = pl.program_id(2)
    g = pl.program_id(1)
    hg = NSA_GROUP
    q = q_ref[0].reshape(hg * tq, HEAD_DIM)
    s = _dot_nt(q, k_ref[0, 0]).reshape(hg, tq, n_rows)
    t = i * tq + lax.broadcasted_iota(jnp.int32, (1, tq, 1), 1)
    n = lax.broadcasted_iota(jnp.int32, (1, 1, n_rows), 2)
    last = jnp.minimum(lax.shift_right_arithmetic(t - (CMP_LEN - 1), CMP_SHIFT), n_cmp - 1)
    s = jnp.where(n <= last, s, -jnp.inf)
    m = jnp.max(s, axis=-1, keepdims=True)
    m = jnp.where(m == -jnp.inf, 0.0, m)
    e = jnp.exp2(s - m)
    p = e * (1.0 / jnp.maximum(jnp.sum(e, axis=-1, keepdims=True), 1e-30))
    o = _dot(p.reshape(hg * tq, n_rows).astype(bf16), v_ref[0, 0])
    _write_gated(o_ref, gate_ref, o, g, 0, tq)

    lane = lax.broadcasted_iota(jnp.int32, (tq, LANES), 1)
    tq_pos = i * tq + lax.broadcasted_iota(jnp.int32, (tq, 1), 0)
    q_blk = lax.shift_right_logical(tq_pos, SEL_SHIFT)
    causal = lane <= q_blk
    n_top = min(SEL_TOPK, n_sel)

    @pl.when((i + 1) * tq <= n_top * SEL_BLOCK)
    def _():
        sel_ref[0, 0] = jnp.where(causal, 0.0, -MASK_BIG).astype(bf16)

    @pl.when((i + 1) * tq > n_top * SEL_BLOCK)
    def _():
        psum = p[0] + p[1] + p[2] + p[3]
        imp = _dot_f32_by_exact(psum, ov_ref[...])
        forced = (lane == 0) | (lane == q_blk) | (lane == q_blk - 1)
        key = jnp.where(forced, -2.0, jnp.where(causal, imp, -1.0))
        key = jnp.where(lane < n_sel, key, -3.0)
        lane_f = lane.astype(f32)
        sel = forced
        for _ in range(n_top - 3):
            mx = jnp.max(key, axis=-1, keepdims=True)
            first = jnp.min(jnp.where(key == mx, lane_f, float(LANES)), axis=-1, keepdims=True)
            pick = lane_f == first
            sel = sel | pick
            key = jnp.where(pick, -2.0, key)
        sel = ((q_blk < n_top) | sel) & causal
        sel_ref[0, 0] = jnp.where(sel, 0.0, -MASK_BIG).astype(bf16)


def _cmp(nqu, kcmp, vcmp, gates, n_cmp):
    n_batch, _, seq_len, _ = nqu.shape
    n_rows = kcmp.shape[2]
    n_sel = seq_len // SEL_BLOCK
    assert 3 <= n_sel <= SEL_LANES
    tq = 1024
    c0 = np.arange(n_rows) * CMP_STRIDE
    s0 = np.arange(SEL_LANES) * SEL_BLOCK
    overlap = np.clip(np.minimum(c0[:, None] + CMP_LEN, s0[None, :] + SEL_BLOCK)
                      - np.maximum(c0[:, None], s0[None, :]), 0, None).astype(np.float32) / CMP_LEN
    overlap[n_cmp:, :] = 0.0
    overlap[:, n_sel:] = 0.0
    return pl.pallas_call(
        functools.partial(_cmp_kernel, tq=tq, n_rows=n_rows, n_cmp=n_cmp, n_sel=n_sel),
        out_shape=[
            jax.ShapeDtypeStruct((n_batch, seq_len, NSA_W), bf16),
            jax.ShapeDtypeStruct((n_batch, NSA_KV_HEADS, seq_len, SEL_LANES), bf16),
        ],
        grid=(n_batch, NSA_KV_HEADS, seq_len // tq),
        in_specs=[
            pl.BlockSpec((1, NSA_GROUP, tq, HEAD_DIM), lambda b, g, i: (b, g, i, 0)),
            pl.BlockSpec((1, 1, n_rows, HEAD_DIM), lambda b, g, i: (b, g, 0, 0)),
            pl.BlockSpec((1, 1, n_rows, HEAD_DIM), lambda b, g, i: (b, g, 0, 0)),
            pl.BlockSpec((1, tq, LANES), lambda b, g, i: (b, i, 0)),
            pl.BlockSpec((n_rows, SEL_LANES), lambda b, g, i: (0, 0)),
        ],
        out_specs=[
            pl.BlockSpec((1, tq, NSA_GROUP * HEAD_DIM), lambda b, g, i: (b, i, g)),
            pl.BlockSpec((1, 1, tq, SEL_LANES), lambda b, g, i: (b, g, i, 0)),
        ],
        compiler_params=_params("arbitrary", "arbitrary", "arbitrary"),
        name="cmp",
    )(nqu, kcmp, vcmp, gates, jnp.asarray(overlap, dtype=bf16))


def _write_gated(o_ref, gate_ref, out, g, branch, tq):
    gates = gate_ref[0]
    lane = lax.broadcasted_iota(jnp.int32, (tq, LANES), 1)
    for h in range(NSA_GROUP):
        gcol = FOX_HEADS + (g * NSA_GROUP + h) * N_BRANCH + branch
        gh = jnp.sum(jnp.where(lane == gcol, gates, 0.0), axis=-1, keepdims=True)
        o_ref[0, :, h * HEAD_DIM:(h + 1) * HEAD_DIM] = (gh * out[h * tq:(h + 1) * tq]).astype(bf16)


def _slc_kernel(q_ref, sel_ref, qn_ref, seln_ref, k_ref, vt_ref, gate_ref, *rest, tq, tk, n_cast_w):
    o_ref = rest[n_cast_w]
    bufs = rest[2 * n_cast_w + 1:]
    for w32_ref, w16_ref in zip(rest[:n_cast_w], rest[n_cast_w + 1:2 * n_cast_w + 1]):
        w16_ref[...] = w32_ref[...].astype(bf16)
    i = pl.program_id(1)
    hg = NSA_GROUP
    n_q = hg * tq
    q0 = i * tq
    jd = q0 // tk

    def diag_mask(s):
        kp = jd * tk + lax.broadcasted_iota(jnp.int32, (tk, n_q), 0)
        t = q0 + (lax.broadcasted_iota(jnp.int32, (tk, n_q), 1) & (tq - 1))
        return jnp.where(kp <= t, s, -jnp.inf)

    def chain(g):
        def q_aug(qr, sr):
            q = qr[0, g * hg:(g + 1) * hg].reshape(n_q, HEAD_DIM)
            return jnp.concatenate([q, jnp.concatenate([sr[0, g]] * hg, axis=0)], axis=1)

        qa = q_aug(q_ref, sel_ref)

        def scores(j):
            return _dot_nt(k_ref[0, g, pl.ds(pl.multiple_of(j * tk, tk), tk), :], qa)

        def next_first():
            return _dot_nt(k_ref[0, g, 0:tk, :], q_aug(qn_ref, seln_ref))

        def emit(out):
            gates = gate_ref[0]
            lane = lax.broadcasted_iota(jnp.int32, (tq, LANES), 1)
            for h in range(hg):
                head = g * hg + h
                gh = jnp.sum(jnp.where(lane == FOX_HEADS + head * N_BRANCH + 1, gates, 0.0),
                             axis=-1, keepdims=True)
                o_ref[0, :, head * HEAD_DIM:(head + 1) * HEAD_DIM] = (
                    gh * out[:, h * tq:(h + 1) * tq].T).astype(bf16)

        return _Chain(scores, lambda j: vt_ref[0, g, j], lambda j: None, diag_mask, emit, next_first,
                      bufs[2 * g], bufs[2 * g + 1])

    _flash_sweep(jd, [chain(g) for g in range(NSA_KV_HEADS)], n_q, i == 0)


def _slc(nqr, selneg, kaug, vst, gates, cast_weights):
    n_batch, _, seq_len, _ = nqr.shape
    tq, tk = 256, KV_BLOCK
    nkb = seq_len // tk
    assert tq & (tq - 1) == 0
    n_q = NSA_GROUP * tq
    n_i = seq_len // tq
    cast_in, cast_out, cast_shapes = _cast_side_job(cast_weights, n_batch * n_i, lambda b, i: b * n_i + i)
    return pl.pallas_call(
        functools.partial(_slc_kernel, tq=tq, tk=tk, n_cast_w=len(cast_weights)),
        out_shape=[jax.ShapeDtypeStruct((n_batch, seq_len, NSA_W), bf16)] + cast_shapes,
        grid=(n_batch, n_i),
        in_specs=[
            pl.BlockSpec((1, NSA_HEADS, tq, HEAD_DIM), lambda b, i: (b, 0, i, 0)),
            pl.BlockSpec((1, NSA_KV_HEADS, tq, SEL_LANES), lambda b, i: (b, 0, i, 0)),
            pl.BlockSpec((1, NSA_HEADS, tq, HEAD_DIM), lambda b, i: (b, 0, jnp.minimum(i + 1, n_i - 1), 0)),
            pl.BlockSpec((1, NSA_KV_HEADS, tq, SEL_LANES), lambda b, i: (b, 0, jnp.minimum(i + 1, n_i - 1), 0)),
            pl.BlockSpec((1, NSA_KV_HEADS, seq_len, HEAD_DIM + SEL_LANES), lambda b, i: (b, 0, 0, 0)),
            pl.BlockSpec((1, NSA_KV_HEADS, nkb, HEAD_DIM, tk), lambda b, i: (b, 0, 0, 0, 0)),
            pl.BlockSpec((1, tq, LANES), lambda b, i: (b, i, 0)),
        ] + cast_in,
        out_specs=[pl.BlockSpec((1, tq, NSA_W), lambda b, i: (b, i, 0))] + cast_out,
        scratch_shapes=[pltpu.VMEM((2, tk, n_q), f32), pltpu.VMEM((2, tk, n_q), bf16)] * NSA_KV_HEADS,
        compiler_params=_params("arbitrary", "arbitrary"),
        name="slc",
    )(nqr, selneg, nqr, selneg, kaug, vst, gates, *cast_weights)


def _win_kernel(q_ref, k_ref, vt_ref, gate_ref, o_ref, *, tq, n_sub):
    i = pl.program_id(2)
    g = pl.program_id(1)
    hg = NSA_GROUP
    span = WINDOW + tq
    q0s = [(i * n_sub + u) * tq for u in range(n_sub)]
    k0s = [pl.multiple_of(jnp.maximum(q0 - WINDOW, 0), tq) for q0 in q0s]
    scores = []
    for u in range(n_sub):
        q = q_ref[0, :, u * tq:(u + 1) * tq, :].reshape(hg * tq, HEAD_DIM)
        scores.append(_dot_nt(k_ref[0, 0, pl.ds(k0s[u], span), :], q))
    probs, denoms = [], []
    for u in range(n_sub):
        kp = k0s[u] + lax.broadcasted_iota(jnp.int32, (span, tq), 0)
        t = q0s[u] + lax.broadcasted_iota(jnp.int32, (span, tq), 1)
        diff = t - kp
        bias = jnp.where((diff >= 0) & (diff < WINDOW), 0.0, -jnp.inf)
        s = jnp.concatenate([scores[u][:, h * tq:(h + 1) * tq] + bias for h in range(hg)], axis=1)
        e = jnp.exp2(s - jnp.max(s, axis=0, keepdims=True))
        denoms.append(jnp.sum(e, axis=0, keepdims=True))
        probs.append(e.astype(bf16))
    gates = gate_ref[0]
    lane = lax.broadcasted_iota(jnp.int32, (tq, LANES), 1)
    for u in range(n_sub):
        jb = k0s[u] // tq
        vt = jnp.concatenate([vt_ref[0, 0, jb + c] for c in range(span // tq)], axis=1)
        out = _dot(vt, probs[u]) / denoms[u]
        for h in range(hg):
            gcol = FOX_HEADS + (g * hg + h) * N_BRANCH + 2
            gh = jnp.sum(jnp.where(lane == gcol, gates[u * tq:(u + 1) * tq], 0.0), axis=-1, keepdims=True)
            o_ref[0, u * tq:(u + 1) * tq, h * HEAD_DIM:(h + 1) * HEAD_DIM] = (
                gh * out[:, h * tq:(h + 1) * tq].T).astype(bf16)


def _win(nqr, kw, vwt, gates):
    n_batch, _, seq_len, _ = nqr.shape
    tq, n_sub = WIN_TILE, 8
    assert seq_len >= WINDOW + tq and WINDOW % tq == 0
    return pl.pallas_call(
        functools.partial(_win_kernel, tq=tq, n_sub=n_sub),
        out_shape=jax.ShapeDtypeStruct((n_batch, seq_len, NSA_W), bf16),
        grid=(n_batch, NSA_KV_HEADS, seq_len // (tq * n_sub)),
        in_specs=[
            pl.BlockSpec((1, NSA_GROUP, tq * n_sub, HEAD_DIM), lambda b, g, i: (b, g, i, 0)),
            pl.BlockSpec((1, 1, seq_len, HEAD_DIM), lambda b, g, i: (b, g, 0, 0)),
            pl.BlockSpec((1, 1, seq_len // tq, HEAD_DIM, tq), lambda b, g, i: (b, g, 0, 0, 0)),
            pl.BlockSpec((1, tq * n_sub, LANES), lambda b, g, i: (b, i, 0)),
        ],
        out_specs=pl.BlockSpec((1, tq * n_sub, NSA_GROUP * HEAD_DIM), lambda b, g, i: (b, i, g)),
        compiler_params=_params("arbitrary", "arbitrary", "arbitrary"),
        name="win",
    )(nqr, kw, vwt, gates)


def _outproj_kernel(fox_ref, c_ref, s_ref, w_ref, x_ref, mod_ref, wo_ref, o_ref):
    mix = _dot(fox_ref[...], wo_ref[0:FOX_W, :])
    nsa = c_ref[...].astype(f32) + s_ref[...].astype(f32) + w_ref[...].astype(f32)
    mix = mix + _dot(nsa.astype(bf16), wo_ref[FOX_W:FOX_W + NSA_W, :])
    o_ref[...] = x_ref[...] + mod_ref[0][2:3] * mix


def _outproj(ofox, ocmp, oslc, owin, x2d, mod3, wo_bf16, seq_len):
    m_rows = x2d.shape[0]
    tm = 512
    per_b = seq_len // tm
    half = pl.BlockSpec((tm, FOX_W), lambda i: (i, 0))
    return pl.pallas_call(
        _outproj_kernel,
        out_shape=jax.ShapeDtypeStruct((m_rows, D_MODEL), f32),
        grid=(m_rows // tm,),
        in_specs=[
            half, half, half, half,
            pl.BlockSpec((tm, D_MODEL), lambda i: (i, 0)),
            pl.BlockSpec((1, 6, D_MODEL), lambda i: (i // per_b, 0, 0)),
            pl.BlockSpec((D_MODEL, D_MODEL), lambda i: (0, 0)),
        ],
        out_specs=pl.BlockSpec((tm, D_MODEL), lambda i: (i, 0)),
        compiler_params=_params("arbitrary"),
        name="outproj",
    )(ofox, ocmp, oslc, owin, x2d, mod3, wo_bf16)


def _mlp_kernel(x_ref, mod_ref, g_ref, wu_ref, wd_ref, o_ref, h_ref, acc_ref):
    f = pl.program_id(1)

    @pl.when(f == 0)
    def _():
        md = mod_ref[0]
        y = _rms(x_ref[...], g_ref[...])
        h_ref[...] = (y * (1.0 + md[4:5]) + md[3:4]).astype(bf16)
        acc_ref[...] = jnp.zeros_like(acc_ref)

    u = jnp.maximum(_dot(h_ref[...], wu_ref[...]), 0.0)
    acc_ref[...] += _dot((u * u).astype(bf16), wd_ref[...])

    @pl.when(f == pl.num_programs(1) - 1)
    def _():
        o_ref[...] = x_ref[...] + mod_ref[0][5:6] * acc_ref[...]


def _mlp(x2d, mod3, norm_g, wu_bf16, wd_bf16, seq_len):
    m_rows = x2d.shape[0]
    tm, tf = 512, 1024
    per_b = seq_len // tm
    return pl.pallas_call(
        _mlp_kernel,
        out_shape=jax.ShapeDtypeStruct((m_rows, D_MODEL), f32),
        grid=(m_rows // tm, D_FF // tf),
        in_specs=[
            pl.BlockSpec((tm, D_MODEL), lambda i, f: (i, 0)),
            pl.BlockSpec((1, 6, D_MODEL), lambda i, f: (i // per_b, 0, 0)),
            pl.BlockSpec((1, D_MODEL), lambda i, f: (0, 0)),
            pl.BlockSpec((D_MODEL, tf), lambda i, f: (0, f)),
            pl.BlockSpec((tf, D_MODEL), lambda i, f: (f, 0)),
        ],
        out_specs=pl.BlockSpec((tm, D_MODEL), lambda i, f: (i, 0)),
        scratch_shapes=[pltpu.VMEM((tm, D_MODEL), bf16), pltpu.VMEM((tm, D_MODEL), f32)],
        compiler_params=_params("arbitrary", "arbitrary"),
        name="mlp",
    )(x2d, mod3, norm_g, wu_bf16, wd_bf16)


def _layer(x, c, w_ada, b_ada, norm1_g, w_in, b_forget, fox_q_norm, fox_k_norm, nsa_q_norm,
           cmp_k_norm, slc_k_norm, win_k_norm, cmp_pe_k, cmp_w1_k, cmp_w2_k, cmp_pe_v, cmp_w1_v,
           cmp_w2_v, w_out, norm2_g, w_up, w_down):
    n_batch, seq_len, _ = x.shape
    n_cmp = (seq_len - CMP_LEN) // CMP_STRIDE + 1
    row = lambda v: v.reshape(1, -1)

    half = HEAD_DIM // 2
    inv_freq = ROPE_THETA ** (-jnp.arange(half, dtype=f32) / half)
    inv_freq = jnp.concatenate([inv_freq, inv_freq]).reshape(1, HEAD_DIM)

    w_a, w_b, w_s = _repack_w_in(w_in.T)

    mod3 = _ada(c, w_ada, b_ada).reshape(n_batch, 6, D_MODEL)
    x2d = x.reshape(n_batch * seq_len, D_MODEL)
    (fq, fk, fvt, cin, nqu, nqr, kc, vc, kaug, vst, kw, vw, gates) = _prep(
        x2d, mod3, row(norm1_g), w_a, w_b, w_s, n_batch, seq_len, row(fox_q_norm), row(fox_k_norm), row(nsa_q_norm), row(slc_k_norm),
        row(win_k_norm), jnp.pad(b_forget, (0, LANES - FOX_HEADS)).reshape(1, LANES), inv_freq)
    kcmp = _compress(kc, cmp_pe_k, cmp_w1_k.astype(bf16), cmp_w2_k.astype(bf16), row(cmp_k_norm))
    vcmp = _compress(vc, cmp_pe_v, cmp_w1_v.astype(bf16), cmp_w2_v.astype(bf16))
    ofox = _fox(fq, fk, fvt, cin)
    ocmp, selneg = _cmp(nqu, kcmp, vcmp, gates, n_cmp)
    oslc, w_up16, w_down16, w_out16 = _slc(nqr, selneg, kaug, vst, gates, [w_up, w_down, w_out])
    owin = _win(nqr, kw, vw, gates)
    x1 = _outproj(ofox.reshape(-1, FOX_W), ocmp.reshape(-1, NSA_W), oslc.reshape(-1, NSA_W),
                  owin.reshape(-1, NSA_W), x2d, mod3, w_out16, seq_len)
    x2 = _mlp(x1, mod3, row(norm2_g), w_up16, w_down16, seq_len)
    return x2.reshape(n_batch, seq_len, D_MODEL)


def kernel(x, c, w_ada, b_ada, norm1_g, w_in, b_forget, fox_q_norm, fox_k_norm, nsa_q_norm, cmp_k_norm,
           slc_k_norm, win_k_norm, cmp_pe_k, cmp_w1_k, cmp_w2_k, cmp_pe_v, cmp_w1_v, cmp_w2_v, w_out,
           norm2_g, w_up, w_down):
    depth = w_ada.shape[0]
    for l in range(depth):
        x = _layer(x, c, w_ada[l], b_ada[l], norm1_g[l], w_in[l], b_forget[l], fox_q_norm[l], fox_k_norm[l],
                   nsa_q_norm[l], cmp_k_norm[l], slc_k_norm[l], win_k_norm[l], cmp_pe_k[l], cmp_w1_k[l],
                   cmp_w2_k[l], cmp_pe_v[l], cmp_w1_v[l], cmp_w2_v[l], w_out[l], norm2_g[l], w_up[l],
                   w_down[l])
    return x
```

```python
import functools
import math
from typing import Any, Callable, NamedTuple

import numpy as np
import jax
import jax.numpy as jnp
from jax import lax
from jax.experimental import pallas as pl
from jax.experimental.pallas import tpu as pltpu

D_MODEL = 2048
HEAD_DIM = 128
FOX_HEADS = 8
NSA_HEADS = 8
NSA_KV_HEADS = 2
NSA_GROUP = NSA_HEADS // NSA_KV_HEADS
N_BRANCH = 3
D_FF = 4 * D_MODEL
ROPE_THETA = 10000.0
CMP_LEN = 32
CMP_STRIDE = 16
CMP_SHIFT = 4
CMP_HIDDEN = 2 * HEAD_DIM
SEL_BLOCK = 64
SEL_SHIFT = 6
SEL_TOPK = 16
WINDOW = 512
NORM_EPS = 1e-6
ATTN_SCALE = HEAD_DIM ** -0.5
FOX_W = FOX_HEADS * HEAD_DIM
NSA_W = NSA_HEADS * HEAD_DIM
KV_W = NSA_KV_HEADS * HEAD_DIM

LANES = 128
F32_SUBLANES = 8
BF16_SUBLANES = 16
SEL_LANES = LANES
MASK_BIG = 1e30
KV_BLOCK = 512
WIN_TILE = 128
LOG2E = math.log2(math.e)

COL_FQ = 0
COL_FK = COL_FQ + FOX_W
COL_FV = COL_FK + FOX_W
COL_NQ = COL_FV + FOX_W
COL_KC = COL_NQ + NSA_W
COL_VC = COL_KC + KV_W
COL_KS = COL_VC + KV_W
COL_VS = COL_KS + KV_W
COL_KW = COL_VS + KV_W
COL_VW = COL_KW + KV_W
COL_SMALL = COL_VW + KV_W
W_IN_Z0 = 3 * FOX_W
W_IN_NQ0 = W_IN_Z0 + FOX_HEADS
W_IN_GZ0 = W_IN_NQ0 + NSA_W + 6 * KV_W

VMEM_LIMIT = 56 * 1024 * 1024
VMEM_LIMIT_PREP = 60 * 1024 * 1024

f32 = jnp.float32
bf16 = jnp.bfloat16


def _params(*sem):
    return pltpu.CompilerParams(dimension_semantics=sem, vmem_limit_bytes=VMEM_LIMIT)


def _dot_nt(a, b):
    return lax.dot_general(a, b, (((1,), (1,)), ((), ())), preferred_element_type=f32)


def _dot(a, b):
    return jnp.dot(a, b, preferred_element_type=f32)


def _split3(x):
    hi = x.astype(bf16)
    r1 = x - hi.astype(f32)
    mid = r1.astype(bf16)
    lo = (r1 - mid.astype(f32)).astype(bf16)
    return hi, mid, lo


def _dot_f32_by_exact(x, w_bf16):
    hi, mid, lo = _split3(x)
    return _dot(hi, w_bf16) + (_dot(mid, w_bf16) + _dot(lo, w_bf16))


def _rms(x, gain):
    ms = jnp.mean(x * x, axis=-1, keepdims=True)
    return x * lax.rsqrt(ms + NORM_EPS) * gain


def _ada_kernel(ct_ref, w_ref, b_ref, o_ref, *, n_batch, k_chunk):
    ct = ct_ref[...]
    act = ct * jax.nn.sigmoid(ct)
    rows = []
    for b in range(n_batch):
        col = act[:, b:b + 1]
        acc = b_ref[...]
        for k0 in range(0, D_MODEL, k_chunk):
            acc = acc + jnp.sum(w_ref[k0:k0 + k_chunk, :] * col[k0:k0 + k_chunk], axis=0, keepdims=True)
        rows.append(acc)
    o_ref[...] = jnp.concatenate(rows, axis=0)


def _ada(c, w_ada, b_ada):
    n_batch = c.shape[0]
    n_out = w_ada.shape[1]
    tn = 1024
    return pl.pallas_call(
        functools.partial(_ada_kernel, n_batch=n_batch, k_chunk=256),
        out_shape=jax.ShapeDtypeStruct((n_batch, n_out), f32),
        grid=(n_out // tn,),
        in_specs=[
            pl.BlockSpec((D_MODEL, n_batch), lambda j: (0, 0)),
            pl.BlockSpec((D_MODEL, tn), lambda j: (0, j)),
            pl.BlockSpec((1, tn), lambda j: (0, j)),
        ],
        out_specs=pl.BlockSpec((n_batch, tn), lambda j: (0, j)),
        compiler_params=_params("arbitrary"),
        name="ada",
    )(c.T, w_ada, b_ada.reshape(1, n_out))


def _repack_kernel(a_ref, *rest, n_part):
    part_refs, (z_ref, gz_ref, wa_ref, wb_ref, ws_ref) = rest[:n_part], rest[n_part:]
    wa_ref[...] = a_ref[...].astype(bf16)
    wb_ref[...] = jnp.concatenate([r[...] for r in part_refs], axis=0).astype(bf16)
    pad = jnp.zeros((LANES - z_ref.shape[0] - gz_ref.shape[0], D_MODEL), f32)
    ws_ref[...] = jnp.concatenate([z_ref[...], gz_ref[...], pad], axis=0).astype(bf16)


def _repack_w_in(w_t):
    n_steps = 16
    ra, rb = W_IN_Z0 // n_steps, (W_IN_GZ0 - W_IN_NQ0) // n_steps
    hb = math.gcd(W_IN_NQ0, rb)
    n_part = rb // hb
    n_z, n_gz = W_IN_NQ0 - W_IN_Z0, w_t.shape[0] - W_IN_GZ0
    assert W_IN_Z0 % n_steps == 0 and (W_IN_GZ0 - W_IN_NQ0) % n_steps == 0 and ra % BF16_SUBLANES == 0 and rb % BF16_SUBLANES == 0
    assert hb % F32_SUBLANES == 0 and W_IN_Z0 % n_z == 0 and W_IN_GZ0 % n_gz == 0
    b_off = W_IN_NQ0 // hb
    part = lambda k: pl.BlockSpec((hb, D_MODEL), lambda j: (b_off + n_part * j + k, 0))
    return pl.pallas_call(
        functools.partial(_repack_kernel, n_part=n_part),
        out_shape=[jax.ShapeDtypeStruct((W_IN_Z0, D_MODEL), bf16),
                   jax.ShapeDtypeStruct((W_IN_GZ0 - W_IN_NQ0, D_MODEL), bf16),
                   jax.ShapeDtypeStruct((LANES, D_MODEL), bf16)],
        grid=(n_steps,),
        in_specs=[pl.BlockSpec((ra, D_MODEL), lambda j: (j, 0))] + [part(k) for k in range(n_part)] + [
            pl.BlockSpec((n_z, D_MODEL), lambda j: (W_IN_Z0 // n_z, 0)),
            pl.BlockSpec((n_gz, D_MODEL), lambda j: (W_IN_GZ0 // n_gz, 0)),
        ],
        out_specs=[pl.BlockSpec((ra, D_MODEL), lambda j: (j, 0)),
                   pl.BlockSpec((rb, D_MODEL), lambda j: (j, 0)),
                   pl.BlockSpec((LANES, D_MODEL), lambda j: (0, 0))],
        compiler_params=_params("arbitrary"),
        name="repack",
    )(*([w_t] * (n_part + 3)))


def _cast_side_job(weights, n_steps, step_of):
    n_cast = max(c for c in range(1, n_steps + 1) if all(w.shape[0] % (BF16_SUBLANES * c) == 0 for w in weights))
    idx = lambda *ids: (jnp.minimum(step_of(*ids), n_cast - 1), 0)
    specs = [pl.BlockSpec((w.shape[0] // n_cast, w.shape[1]), idx) for w in weights]
    return specs, specs, [jax.ShapeDtypeStruct(w.shape, bf16) for w in weights]


def _prep_kernel(x_ref, mod_ref, g1_ref, wa_ref, wb_ref, ws_ref,
                 gq_ref, gk_ref, gn_ref, gs_ref, gw_ref, bf_ref, inv_ref,
                 fq_ref, fk_ref, fvt_ref, cin_ref, nqu_ref, nqr_ref, kc_ref, vc_ref,
                 ks_ref, vst_ref, kw_ref, vw_ref, gate_ref, carry_ref, rot_ref, *, tm):
    i = pl.program_id(1)

    @pl.when(i == 0)
    def _():
        carry_ref[...] = jnp.zeros_like(carry_ref)

    md = mod_ref[0]
    hn = (_rms(x_ref[...], g1_ref[...]) * (1.0 + md[1:2]) + md[0:1]).astype(bf16)
    groups = {}

    def head(col, h):
        base, w_ref, r0 = next((b, w, r) for b, w, r in (
            (COL_KC, wb_ref, NSA_W), (COL_NQ, wb_ref, 0), (COL_FV, wa_ref, 2 * FOX_W),
            (COL_FK, wa_ref, FOX_W), (COL_FQ, wa_ref, 0)) if col >= b)
        if base not in groups:
            n = (COL_SMALL - COL_KC) if base == COL_KC else FOX_W
            groups[base] = _dot_nt(hn, w_ref[r0:r0 + n, :])
        c0 = col - base + h * HEAD_DIM
        return groups[base][:, c0:c0 + HEAD_DIM]

    row = lax.broadcasted_iota(jnp.int32, (tm, LANES), 0)
    lane = lax.broadcasted_iota(jnp.int32, (tm, LANES), 1)
    pos = i * tm + row

    @pl.when((pl.program_id(0) == 0) & (i == 0))
    def _():
        ang_row = row.astype(f32) * inv_ref[...]
        rot_ref[0] = jnp.cos(ang_row)
        rot_ref[1] = jnp.sin(ang_row)

    ang0 = (i * tm).astype(f32) * inv_ref[...]
    cos0, sin0 = jnp.cos(ang0), jnp.sin(ang0)
    cos = cos0 * rot_ref[0] - sin0 * rot_ref[1]
    sin = sin0 * rot_ref[0] + cos0 * rot_ref[1]
    sin_signed = jnp.where(lane < HEAD_DIM // 2, -sin, sin)

    def rope(x):
        return x * cos + pltpu.roll(x, HEAD_DIM // 2, 1) * sin_signed

    small = _dot_nt(hn, ws_ref[...])
    z = small + bf_ref[...]
    logf = jnp.minimum(z, 0.0) - jnp.log1p(jnp.exp(-jnp.abs(z)))
    t_idx = lax.broadcasted_iota(jnp.int32, (tm, tm), 0)
    s_idx = lax.broadcasted_iota(jnp.int32, (tm, tm), 1)
    tri = jnp.where(s_idx <= t_idx, 1.0, 0.0).astype(bf16)
    hi, mid, lo = _split3(logf)
    local = _dot(tri, hi) + (_dot(tri, mid) + _dot(tri, lo))
    cin_ref[0, 0] = carry_ref[0:1, :]
    carry_ref[...] = carry_ref[...] + local[tm - 1:tm, :]
    b_hi, b_mid, b_lo = (v.astype(f32) for v in _split3(local * (-LOG2E)))
    ones3 = jnp.where(lane < 3, 1.0, 0.0).astype(bf16)

    for h in range(FOX_HEADS):
        q = _rms(head(COL_FQ, h), gq_ref[...] * (ATTN_SCALE * LOG2E))
        fq_ref[0, h] = jnp.concatenate([q.astype(bf16), ones3], axis=1)
    for h in range(FOX_HEADS):
        k = _rms(head(COL_FK, h), gk_ref[...])
        bias = jnp.where(lane == 0, b_hi[:, h:h + 1],
                         jnp.where(lane == 1, b_mid[:, h:h + 1],
                                   jnp.where(lane == 2, b_lo[:, h:h + 1], 0.0)))
        fk_ref[0, h] = jnp.concatenate([k.astype(bf16), bias.astype(bf16)], axis=1)

    gate_ref[0] = jax.nn.sigmoid(small)

    for h in range(NSA_HEADS):
        qn = _rms(head(COL_NQ, h), gn_ref[...] * (ATTN_SCALE * LOG2E))
        nqu_ref[0, h] = qn.astype(bf16)
        nqr_ref[0, h] = rope(qn).astype(bf16)
    onehot = jnp.where(lane == lax.shift_right_logical(pos, SEL_SHIFT), 1.0, 0.0).astype(bf16)
    for g in range(NSA_KV_HEADS):
        kc_ref[0, g] = head(COL_KC, g)
        vc_ref[0, g] = head(COL_VC, g)
        ks = rope(_rms(head(COL_KS, g), gs_ref[...])).astype(bf16)
        ks_ref[0, g] = jnp.concatenate([ks, onehot], axis=1)
        vst_ref[0, g, 0] = head(COL_VS, g).T.astype(bf16)
        kw_ref[0, g] = rope(_rms(head(COL_KW, g), gw_ref[...])).astype(bf16)
        vwt = head(COL_VW, g).T.astype(bf16)
        for c in range(tm // WIN_TILE):
            vw_ref[0, g, c] = vwt[:, c * WIN_TILE:(c + 1) * WIN_TILE]
    for h in range(FOX_HEADS):
        fvt_ref[0, h, 0] = head(COL_FV, h).T.astype(bf16)


def _prep(x2d, mod3, norm_g, wa, wb, ws, n_batch, seq_len, gq, gk, gn, gs, gw, b_forget_row, inv_freq):
    resident = lambda w: pl.BlockSpec(w.shape, lambda b, i: (0, 0), pipeline_mode=pl.Buffered(1))
    tm = KV_BLOCK
    per_b = seq_len // tm
    hshape = lambda n, w, dt: jax.ShapeDtypeStruct((n_batch, n, seq_len, w), dt)
    hspec = lambda n, w: pl.BlockSpec((1, n, tm, w), lambda b, i: (b, 0, i, 0))
    tshape = lambda n: jax.ShapeDtypeStruct((n_batch, n, per_b, HEAD_DIM, tm), bf16)
    tspec = lambda n: pl.BlockSpec((1, n, 1, HEAD_DIM, tm), lambda b, i: (b, 0, i, 0, 0))
    vec = pl.BlockSpec((1, LANES), lambda b, i: (0, 0))
    return pl.pallas_call(
        functools.partial(_prep_kernel, tm=tm),
        out_shape=[
            hshape(FOX_HEADS, 2 * HEAD_DIM, bf16), hshape(FOX_HEADS, 2 * HEAD_DIM, bf16), tshape(FOX_HEADS),
            jax.ShapeDtypeStruct((n_batch, per_b, 1, LANES), f32),
            hshape(NSA_HEADS, HEAD_DIM, bf16), hshape(NSA_HEADS, HEAD_DIM, bf16),
            hshape(NSA_KV_HEADS, HEAD_DIM, f32), hshape(NSA_KV_HEADS, HEAD_DIM, f32),
            hshape(NSA_KV_HEADS, HEAD_DIM + SEL_LANES, bf16), tshape(NSA_KV_HEADS),
            hshape(NSA_KV_HEADS, HEAD_DIM, bf16),
            jax.ShapeDtypeStruct((n_batch, NSA_KV_HEADS, seq_len // WIN_TILE, HEAD_DIM, WIN_TILE), bf16),
            jax.ShapeDtypeStruct((n_batch, seq_len, LANES), f32),
        ],
        grid=(n_batch, per_b),
        in_specs=[
            pl.BlockSpec((tm, D_MODEL), lambda b, i: (b * per_b + i, 0)),
            pl.BlockSpec((1, 6, D_MODEL), lambda b, i: (b, 0, 0)),
            pl.BlockSpec((1, D_MODEL), lambda b, i: (0, 0)),
            resident(wa), resident(wb), resident(ws),
            vec, vec, vec, vec, vec, vec, vec,
        ],
        out_specs=[
            hspec(FOX_HEADS, 2 * HEAD_DIM), hspec(FOX_HEADS, 2 * HEAD_DIM), tspec(FOX_HEADS),
            pl.BlockSpec((1, 1, 1, LANES), lambda b, i: (b, i, 0, 0)),
            hspec(NSA_HEADS, HEAD_DIM), hspec(NSA_HEADS, HEAD_DIM),
            hspec(NSA_KV_HEADS, HEAD_DIM), hspec(NSA_KV_HEADS, HEAD_DIM),
            hspec(NSA_KV_HEADS, HEAD_DIM + SEL_LANES), tspec(NSA_KV_HEADS),
            hspec(NSA_KV_HEADS, HEAD_DIM),
            pl.BlockSpec((1, NSA_KV_HEADS, tm // WIN_TILE, HEAD_DIM, WIN_TILE), lambda b, i: (b, 0, i, 0, 0)),
            pl.BlockSpec((1, tm, LANES), lambda b, i: (b, i, 0)),
        ],
        scratch_shapes=[pltpu.VMEM((8, LANES), f32), pltpu.VMEM((2, tm, LANES), f32)],
        compiler_params=pltpu.CompilerParams(dimension_semantics=("arbitrary", "arbitrary"),
                                             vmem_limit_bytes=VMEM_LIMIT_PREP),
        name="prep",
    )(x2d, mod3, norm_g, wa, wb, ws, gq, gk, gn, gs, gw, b_forget_row, inv_freq)


def _compress_kernel(xk_ref, xv_ref, pek_ref, pev_ref, w1k_ref, w1v_ref, w2k_ref, w2v_ref, g_ref,
                     ok_ref, ov_ref, *, n_rows):
    streams = ((xk_ref, pek_ref, w1k_ref), (xv_ref, pev_ref, w1v_ref))
    acc = [[jnp.zeros((n_rows, CMP_HIDDEN), f32) for _ in range(2)] for _ in streams]
    for l in range(CMP_STRIDE):
        for c, (x_ref, pe_ref, w1_ref) in enumerate(streams):
            x = x_ref[0, 0, pl.ds(l, n_rows, stride=CMP_STRIDE), :]
            for half in range(2):
                pos = half * CMP_STRIDE + l
                acc[c][half] = acc[c][half] + _dot((x + pe_ref[pos:pos + 1, :]).astype(bf16),
                                                   w1_ref[pos * HEAD_DIM:(pos + 1) * HEAD_DIM, :])
    outs = []
    for (a, b), w2_ref in zip(acc, (w2k_ref, w2v_ref)):
        pre = a + pltpu.roll(b, n_rows - 1, 0)
        hid = pre * jax.nn.sigmoid(pre)
        outs.append(_dot(hid.astype(bf16), w2_ref[...]))
    ok_ref[0, 0] = _rms(outs[0], g_ref[...]).astype(bf16)
    ov_ref[0, 0] = outs[1].astype(bf16)


def _compress(xk, xv, pe_k, pe_v, w1k, w1v, w2k, w2v, gain_k):
    n_batch, n_g, seq_len, _ = xk.shape
    n_rows = seq_len // CMP_STRIDE
    tok = pl.BlockSpec((1, 1, seq_len, HEAD_DIM), lambda b, g: (b, g, 0, 0))
    full = lambda a: pl.BlockSpec(a.shape, lambda b, g: (0, 0))
    out = pl.BlockSpec((1, 1, n_rows, HEAD_DIM), lambda b, g: (b, g, 0, 0))
    shape = jax.ShapeDtypeStruct((n_batch, n_g, n_rows, HEAD_DIM), bf16)
    return pl.pallas_call(
        functools.partial(_compress_kernel, n_rows=n_rows),
        out_shape=[shape, shape],
        grid=(n_batch, n_g),
        in_specs=[tok, tok, full(pe_k), full(pe_v), full(w1k), full(w1v), full(w2k), full(w2v), full(gain_k)],
        out_specs=[out, out],
        compiler_params=_params("arbitrary", "arbitrary"),
        name="compress",
    )(xk, xv, pe_k, pe_v, w1k, w1v, w2k, w2v, gain_k)


def _softmax_block(s, d, m_prev, l_prev):
    m_blk = jnp.max(s, axis=0, keepdims=True)
    m_new = jnp.maximum(m_prev, m_blk if d is None else m_blk + d)
    p = jnp.exp2(s - (m_new if d is None else m_new - d))
    alpha = jnp.exp2(m_prev - m_new)
    l_new = alpha * l_prev + jnp.sum(p, axis=0, keepdims=True)
    return m_new, l_new, alpha, p.astype(bf16)


class _Chain(NamedTuple):
    scores: Callable
    values: Callable
    offset: Callable
    diag_mask: Callable
    emit: Callable
    next_first: Callable
    s_buf: Any
    p_buf: Any


def _flash_sweep(n_full, chains, n_q, is_first):
    @pl.when(is_first)
    def _():
        for c in chains:
            c.s_buf[0] = c.scores(0)

    for c in chains:
        c.p_buf[1] = jnp.zeros(c.p_buf.shape[1:], c.p_buf.dtype)

    def accumulate(c, j, slot, alpha, acc):
        return alpha * acc + _dot(c.values(jnp.maximum(j, 0)), c.p_buf[slot])

    def step(j, cur, states):
        out = []
        for c, (m, l, acc, alpha_prev) in zip(chains, states):
            acc = accumulate(c, j - 1, 1 - cur, alpha_prev, acc)
            m, l, alpha, p = _softmax_block(c.s_buf[cur], c.offset(j), m, l)
            c.p_buf[cur] = p
            c.s_buf[1 - cur] = c.scores(j + 1)
            out.append((m, l, acc, alpha))
        return tuple(out)

    def finish(cur, states):
        nxt = [c.next_first() for c in chains]
        for c, s_next, (m, l, acc, alpha_prev) in zip(chains, nxt, states):
            if cur == 1:
                c.s_buf[0] = s_next
            acc = accumulate(c, n_full - 1, 1 - cur, alpha_prev, acc)
            m, l, alpha, p = _softmax_block(c.diag_mask(c.s_buf[cur]), None, m, l)
            if cur == 0:
                c.s_buf[0] = s_next
            acc = alpha * acc + _dot(c.values(n_full), p)
            c.emit(acc / l)

    init = tuple((jnp.full((1, n_q), -jnp.inf, f32), jnp.zeros((1, n_q), f32),
                  jnp.zeros((HEAD_DIM, n_q), f32), jnp.ones((1, n_q), f32)) for _ in chains)
    states = lax.fori_loop(0, n_full // 2, lambda jj, st: step(2 * jj + 1, 1, step(2 * jj, 0, st)), init)

    @pl.when(n_full % 2 == 0)
    def _():
        finish(0, states)

    @pl.when(n_full % 2 == 1)
    def _():
        finish(1, step(n_full - 1, 0, states))


def _fox_kernel(q_ref, qn_ref, k_ref, vt_ref, c_ref, o_ref, *bufs, tq, n_heads):
    hb = pl.program_id(1)
    i = pl.program_id(2)
    lane = lax.broadcasted_iota(jnp.int32, (1, LANES), 1)

    def diag_mask(s):
        kk = lax.broadcasted_iota(jnp.int32, (tq, tq), 0)
        qq = lax.broadcasted_iota(jnp.int32, (tq, tq), 1)
        return jnp.where(kk <= qq, s, -jnp.inf)

    def chain(u):
        qa = q_ref[0, u]

        def cin(j):
            return jnp.sum(jnp.where(lane == hb * n_heads + u, c_ref[0, j], 0.0), axis=-1, keepdims=True)

        ci = cin(i)

        def scores(j):
            return _dot_nt(k_ref[0, u, pl.ds(pl.multiple_of(j * tq, tq), tq), :], qa)

        def emit(out_t):
            o_ref[0, :, u * HEAD_DIM:(u + 1) * HEAD_DIM] = out_t.T.astype(bf16)

        def next_first():
            return _dot_nt(k_ref[0, u, 0:tq, :], qn_ref[0, u])

        return _Chain(scores, lambda j: vt_ref[0, u, j], lambda j: (ci - cin(j)) * LOG2E, diag_mask, emit,
                      next_first, bufs[2 * u], bufs[2 * u + 1])

    _flash_sweep(i, [chain(u) for u in range(n_heads)], tq, i == 0)


def _fox(fq, fk, fvt, cin):
    n_batch, n_h, seq_len, _ = fq.shape
    tq = KV_BLOCK
    nkb = seq_len // tq
    n_heads = 4
    return pl.pallas_call(
        functools.partial(_fox_kernel, tq=tq, n_heads=n_heads),
        out_shape=jax.ShapeDtypeStruct((n_batch, seq_len, n_h * HEAD_DIM), bf16),
        grid=(n_batch, n_h // n_heads, nkb),
        in_specs=[
            pl.BlockSpec((1, n_heads, tq, 2 * HEAD_DIM), lambda b, h, i: (b, h, i, 0)),
            pl.BlockSpec((1, n_heads, tq, 2 * HEAD_DIM), lambda b, h, i: (b, h, jnp.minimum(i + 1, nkb - 1), 0)),
            pl.BlockSpec((1, n_heads, seq_len, 2 * HEAD_DIM), lambda b, h, i: (b, h, 0, 0)),
            pl.BlockSpec((1, n_heads, nkb, HEAD_DIM, tq), lambda b, h, i: (b, h, 0, 0, 0)),
            pl.BlockSpec((1, nkb, 1, LANES), lambda b, h, i: (b, 0, 0, 0)),
        ],
        out_specs=pl.BlockSpec((1, tq, n_heads * HEAD_DIM), lambda b, h, i: (b, i, h)),
        scratch_shapes=[pltpu.VMEM((2, tq, tq), f32), pltpu.VMEM((2, tq, tq), bf16)] * n_heads,
        compiler_params=_params("arbitrary", "arbitrary", "arbitrary"),
        name="fox",
    )(fq, fq, fk, fvt, cin)


def _cmp_kernel(q_ref, k_ref, v_ref, gate_ref, ov_ref, o_ref, sel_ref, *, tq, n_rows, n_cmp, n_sel):
    i = pl.program_id(2)
    g = pl.program_id(1)
    hg = NSA_GROUP
    q = q_ref[0].reshape(hg * tq, HEAD_DIM)
    s = _dot_nt(q, k_ref[0, 0]).reshape(hg, tq, n_rows)
    t = i * tq + lax.broadcasted_iota(jnp.int32, (1, tq, 1), 1)
    n = lax.broadcasted_iota(jnp.int32, (1, 1, n_rows), 2)
    last = jnp.minimum(lax.shift_right_arithmetic(t - (CMP_LEN - 1), CMP_SHIFT), n_cmp - 1)
    s = jnp.where(n <= last, s, -jnp.inf)
    m = jnp.max(s, axis=-1, keepdims=True)
    m = jnp.where(m == -jnp.inf, 0.0, m)
    e = jnp.exp2(s - m)
    p = e * (1.0 / jnp.maximum(jnp.sum(e, axis=-1, keepdims=True), 1e-30))
    o = _dot(p.reshape(hg * tq, n_rows).astype(bf16), v_ref[0, 0])
    _write_gated(o_ref, gate_ref, o, g, 0, tq)

    lane = lax.broadcasted_iota(jnp.int32, (tq, LANES), 1)
    tq_pos = i * tq + lax.broadcasted_iota(jnp.int32, (tq, 1), 0)
    q_blk = lax.shift_right_logical(tq_pos, SEL_SHIFT)
    causal = lane <= q_blk
    n_top = min(SEL_TOPK, n_sel)

    @pl.when((i + 1) * tq <= n_top * SEL_BLOCK)
    def _():
        sel_ref[0, 0] = jnp.where(causal, 0.0, -MASK_BIG).astype(bf16)

    @pl.when((i + 1) * tq > n_top * SEL_BLOCK)
    def _():
        psum = p[0] + p[1] + p[2] + p[3]
        imp = _dot_f32_by_exact(psum, ov_ref[...])
        forced = (lane == 0) | (lane == q_blk) | (lane == q_blk - 1)
        key = jnp.where(forced, -2.0, jnp.where(causal, imp, -1.0))
        key = jnp.where(lane < n_sel, key, -3.0)
        lane_f = lane.astype(f32)
        sel = forced
        for _ in range(n_top - 3):
            mx = jnp.max(key, axis=-1, keepdims=True)
            first = jnp.min(jnp.where(key == mx, lane_f, float(LANES)), axis=-1, keepdims=True)
            pick = lane_f == first
            sel = sel | pick
            key = jnp.where(pick, -2.0, key)
        sel = ((q_blk < n_top) | sel) & causal
        sel_ref[0, 0] = jnp.where(sel, 0.0, -MASK_BIG).astype(bf16)


def _cmp(nqu, kcmp, vcmp, gates, n_cmp):
    n_batch, _, seq_len, _ = nqu.shape
    n_rows = kcmp.shape[2]
    n_sel = seq_len // SEL_BLOCK
    assert 3 <= n_sel <= SEL_LANES
    tq = 1024
    c0 = np.arange(n_rows) * CMP_STRIDE
    s0 = np.arange(SEL_LANES) * SEL_BLOCK
    overlap = np.clip(np.minimum(c0[:, None] + CMP_LEN, s0[None, :] + SEL_BLOCK)
                      - np.maximum(c0[:, None], s0[None, :]), 0, None).astype(np.float32) / CMP_LEN
    overlap[n_cmp:, :] = 0.0
    overlap[:, n_sel:] = 0.0
    return pl.pallas_call(
        functools.partial(_cmp_kernel, tq=tq, n_rows=n_rows, n_cmp=n_cmp, n_sel=n_sel),
        out_shape=[
            jax.ShapeDtypeStruct((n_batch, seq_len, NSA_W), bf16),
            jax.ShapeDtypeStruct((n_batch, NSA_KV_HEADS, seq_len, SEL_LANES), bf16),
        ],
        grid=(n_batch, NSA_KV_HEADS, seq_len // tq),
        in_specs=[
            pl.BlockSpec((1, NSA_GROUP, tq, HEAD_DIM), lambda b, g, i: (b, g, i, 0)),
            pl.BlockSpec((1, 1, n_rows, HEAD_DIM), lambda b, g, i: (b, g, 0, 0)),
            pl.BlockSpec((1, 1, n_rows, HEAD_DIM), lambda b, g, i: (b, g, 0, 0)),
            pl.BlockSpec((1, tq, LANES), lambda b, g, i: (b, i, 0)),
            pl.BlockSpec((n_rows, SEL_LANES), lambda b, g, i: (0, 0)),
        ],
        out_specs=[
            pl.BlockSpec((1, tq, NSA_GROUP * HEAD_DIM), lambda b, g, i: (b, i, g)),
            pl.BlockSpec((1, 1, tq, SEL_LANES), lambda b, g, i: (b, g, i, 0)),
        ],
        compiler_params=_params("arbitrary", "arbitrary", "arbitrary"),
        name="cmp",
    )(nqu, kcmp, vcmp, gates, jnp.asarray(overlap, dtype=bf16))


def _write_gated(o_ref, gate_ref, out, g, branch, tq):
    gates = gate_ref[0]
    lane = lax.broadcasted_iota(jnp.int32, (tq, LANES), 1)
    for h in range(NSA_GROUP):
        gcol = FOX_HEADS + (g * NSA_GROUP + h) * N_BRANCH + branch
        gh = jnp.sum(jnp.where(lane == gcol, gates, 0.0), axis=-1, keepdims=True)
        o_ref[0, :, h * HEAD_DIM:(h + 1) * HEAD_DIM] = (gh * out[h * tq:(h + 1) * tq]).astype(bf16)


def _slc_kernel(q_ref, sel_ref, qn_ref, seln_ref, k_ref, vt_ref, gate_ref, *rest, tq, tk, n_cast_w):
    o_ref = rest[n_cast_w]
    bufs = rest[2 * n_cast_w + 1:]
    for w32_ref, w16_ref in zip(rest[:n_cast_w], rest[n_cast_w + 1:2 * n_cast_w + 1]):
        w16_ref[...] = w32_ref[...].astype(bf16)
    i = pl.program_id(1)
    hg = NSA_GROUP
    n_q = hg * tq
    q0 = i * tq
    jd = q0 // tk

    def diag_mask(s):
        kp = jd * tk + lax.broadcasted_iota(jnp.int32, (tk, n_q), 0)
        t = q0 + (lax.broadcasted_iota(jnp.int32, (tk, n_q), 1) & (tq - 1))
        return jnp.where(kp <= t, s, -jnp.inf)

    def chain(g):
        def q_aug(qr, sr):
            q = qr[0, g * hg:(g + 1) * hg].reshape(n_q, HEAD_DIM)
            return jnp.concatenate([q, jnp.concatenate([sr[0, g]] * hg, axis=0)], axis=1)

        qa = q_aug(q_ref, sel_ref)

        def scores(j):
            return _dot_nt(k_ref[0, g, pl.ds(pl.multiple_of(j * tk, tk), tk), :], qa)

        def next_first():
            return _dot_nt(k_ref[0, g, 0:tk, :], q_aug(qn_ref, seln_ref))

        def emit(out):
            gates = gate_ref[0]
            lane = lax.broadcasted_iota(jnp.int32, (tq, LANES), 1)
            for h in range(hg):
                head = g * hg + h
                gh = jnp.sum(jnp.where(lane == FOX_HEADS + head * N_BRANCH + 1, gates, 0.0),
                             axis=-1, keepdims=True)
                o_ref[0, :, head * HEAD_DIM:(head + 1) * HEAD_DIM] = (
                    gh * out[:, h * tq:(h + 1) * tq].T).astype(bf16)

        return _Chain(scores, lambda j: vt_ref[0, g, j], lambda j: None, diag_mask, emit, next_first,
                      bufs[2 * g], bufs[2 * g + 1])

    _flash_sweep(jd, [chain(g) for g in range(NSA_KV_HEADS)], n_q, i == 0)


def _slc(nqr, selneg, kaug, vst, gates, cast_weights):
    n_batch, _, seq_len, _ = nqr.shape
    tq, tk = 256, KV_BLOCK
    nkb = seq_len // tk
    assert tq & (tq - 1) == 0
    n_q = NSA_GROUP * tq
    n_i = seq_len // tq
    cast_in, cast_out, cast_shapes = _cast_side_job(cast_weights, n_batch * n_i, lambda b, i: b * n_i + i)
    return pl.pallas_call(
        functools.partial(_slc_kernel, tq=tq, tk=tk, n_cast_w=len(cast_weights)),
        out_shape=[jax.ShapeDtypeStruct((n_batch, seq_len, NSA_W), bf16)] + cast_shapes,
        grid=(n_batch, n_i),
        in_specs=[
            pl.BlockSpec((1, NSA_HEADS, tq, HEAD_DIM), lambda b, i: (b, 0, i, 0)),
            pl.BlockSpec((1, NSA_KV_HEADS, tq, SEL_LANES), lambda b, i: (b, 0, i, 0)),
            pl.BlockSpec((1, NSA_HEADS, tq, HEAD_DIM), lambda b, i: (b, 0, jnp.minimum(i + 1, n_i - 1), 0)),
            pl.BlockSpec((1, NSA_KV_HEADS, tq, SEL_LANES), lambda b, i: (b, 0, jnp.minimum(i + 1, n_i - 1), 0)),
            pl.BlockSpec((1, NSA_KV_HEADS, seq_len, HEAD_DIM + SEL_LANES), lambda b, i: (b, 0, 0, 0)),
            pl.BlockSpec((1, NSA_KV_HEADS, nkb, HEAD_DIM, tk), lambda b, i: (b, 0, 0, 0, 0)),
            pl.BlockSpec((1, tq, LANES), lambda b, i: (b, i, 0)),
        ] + cast_in,
        out_specs=[pl.BlockSpec((1, tq, NSA_W), lambda b, i: (b, i, 0))] + cast_out,
        scratch_shapes=[pltpu.VMEM((2, tk, n_q), f32), pltpu.VMEM((2, tk, n_q), bf16)] * NSA_KV_HEADS,
        compiler_params=_params("arbitrary", "arbitrary"),
        name="slc",
    )(nqr, selneg, nqr, selneg, kaug, vst, gates, *cast_weights)


def _win_kernel(q_ref, k_ref, vt_ref, gate_ref, o_ref, *, tq, n_sub):
    i = pl.program_id(2)
    g = pl.program_id(1)
    hg = NSA_GROUP
    span = WINDOW + tq
    q0s = [(i * n_sub + u) * tq for u in range(n_sub)]
    k0s = [pl.multiple_of(jnp.maximum(q0 - WINDOW, 0), tq) for q0 in q0s]
    scores = []
    for u in range(n_sub):
        q = q_ref[0, :, u * tq:(u + 1) * tq, :].reshape(hg * tq, HEAD_DIM)
        scores.append(_dot_nt(k_ref[0, 0, pl.ds(k0s[u], span), :], q))
    probs, denoms = [], []
    for u in range(n_sub):
        kp = k0s[u] + lax.broadcasted_iota(jnp.int32, (span, tq), 0)
        t = q0s[u] + lax.broadcasted_iota(jnp.int32, (span, tq), 1)
        diff = t - kp
        bias = jnp.where((diff >= 0) & (diff < WINDOW), 0.0, -jnp.inf)
        s = jnp.concatenate([scores[u][:, h * tq:(h + 1) * tq] + bias for h in range(hg)], axis=1)
        e = jnp.exp2(s - jnp.max(s, axis=0, keepdims=True))
        denoms.append(jnp.sum(e, axis=0, keepdims=True))
        probs.append(e.astype(bf16))
    gates = gate_ref[0]
    lane = lax.broadcasted_iota(jnp.int32, (tq, LANES), 1)
    for u in range(n_sub):
        jb = k0s[u] // tq
        vt = jnp.concatenate([vt_ref[0, 0, jb + c] for c in range(span // tq)], axis=1)
        out = _dot(vt, probs[u]) / denoms[u]
        for h in range(hg):
            gcol = FOX_HEADS + (g * hg + h) * N_BRANCH + 2
            gh = jnp.sum(jnp.where(lane == gcol, gates[u * tq:(u + 1) * tq], 0.0), axis=-1, keepdims=True)
            o_ref[0, u * tq:(u + 1) * tq, h * HEAD_DIM:(h + 1) * HEAD_DIM] = (
                gh * out[:, h * tq:(h + 1) * tq].T).astype(bf16)


def _win(nqr, kw, vwt, gates):
    n_batch, _, seq_len, _ = nqr.shape
    tq, n_sub = WIN_TILE, 8
    assert seq_len >= WINDOW + tq and WINDOW % tq == 0
    return pl.pallas_call(
        functools.partial(_win_kernel, tq=tq, n_sub=n_sub),
        out_shape=jax.ShapeDtypeStruct((n_batch, seq_len, NSA_W), bf16),
        grid=(n_batch, NSA_KV_HEADS, seq_len // (tq * n_sub)),
        in_specs=[
            pl.BlockSpec((1, NSA_GROUP, tq * n_sub, HEAD_DIM), lambda b, g, i: (b, g, i, 0)),
            pl.BlockSpec((1, 1, seq_len, HEAD_DIM), lambda b, g, i: (b, g, 0, 0)),
            pl.BlockSpec((1, 1, seq_len // tq, HEAD_DIM, tq), lambda b, g, i: (b, g, 0, 0, 0)),
            pl.BlockSpec((1, tq * n_sub, LANES), lambda b, g, i: (b, i, 0)),
        ],
        out_specs=pl.BlockSpec((1, tq * n_sub, NSA_GROUP * HEAD_DIM), lambda b, g, i: (b, i, g)),
        compiler_params=_params("arbitrary", "arbitrary", "arbitrary"),
        name="win",
    )(nqr, kw, vwt, gates)


def _outproj_kernel(fox_ref, c_ref, s_ref, w_ref, x_ref, mod_ref, wo_ref, o_ref):
    mix = _dot(fox_ref[...], wo_ref[0:FOX_W, :])
    nsa = c_ref[...].astype(f32) + s_ref[...].astype(f32) + w_ref[...].astype(f32)
    mix = mix + _dot(nsa.astype(bf16), wo_ref[FOX_W:FOX_W + NSA_W, :])
    o_ref[...] = x_ref[...] + mod_ref[0][2:3] * mix


def _outproj(ofox, ocmp, oslc, owin, x2d, mod3, wo_bf16, seq_len):
    m_rows = x2d.shape[0]
    tm = 512
    per_b = seq_len // tm
    half = pl.BlockSpec((tm, FOX_W), lambda i: (i, 0))
    return pl.pallas_call(
        _outproj_kernel,
        out_shape=jax.ShapeDtypeStruct((m_rows, D_MODEL), f32),
        grid=(m_rows // tm,),
        in_specs=[
            half, half, half, half,
            pl.BlockSpec((tm, D_MODEL), lambda i: (i, 0)),
            pl.BlockSpec((1, 6, D_MODEL), lambda i: (i // per_b, 0, 0)),
            pl.BlockSpec((D_MODEL, D_MODEL), lambda i: (0, 0)),
        ],
        out_specs=pl.BlockSpec((tm, D_MODEL), lambda i: (i, 0)),
        compiler_params=_params("arbitrary"),
        name="outproj",
    )(ofox, ocmp, oslc, owin, x2d, mod3, wo_bf16)


def _mlp_kernel(x_ref, mod_ref, g_ref, wu_ref, wd_ref, o_ref, h_ref, acc_ref):
    f = pl.program_id(1)

    @pl.when(f == 0)
    def _():
        md = mod_ref[0]
        y = _rms(x_ref[...], g_ref[...])
        h_ref[...] = (y * (1.0 + md[4:5]) + md[3:4]).astype(bf16)
        acc_ref[...] = jnp.zeros_like(acc_ref)

    u = jnp.maximum(_dot(h_ref[...], wu_ref[...]), 0.0)
    acc_ref[...] += _dot((u * u).astype(bf16), wd_ref[...])

    @pl.when(f == pl.num_programs(1) - 1)
    def _():
        o_ref[...] = x_ref[...] + mod_ref[0][5:6] * acc_ref[...]


def _mlp(x2d, mod3, norm_g, wu_bf16, wd_bf16, seq_len):
    m_rows = x2d.shape[0]
    tm, tf = 512, 1024
    per_b = seq_len // tm
    return pl.pallas_call(
        _mlp_kernel,
        out_shape=jax.ShapeDtypeStruct((m_rows, D_MODEL), f32),
        grid=(m_rows // tm, D_FF // tf),
        in_specs=[
            pl.BlockSpec((tm, D_MODEL), lambda i, f: (i, 0)),
            pl.BlockSpec((1, 6, D_MODEL), lambda i, f: (i // per_b, 0, 0)),
            pl.BlockSpec((1, D_MODEL), lambda i, f: (0, 0)),
            pl.BlockSpec((D_MODEL, tf), lambda i, f: (0, f)),
            pl.BlockSpec((tf, D_MODEL), lambda i, f: (f, 0)),
        ],
        out_specs=pl.BlockSpec((tm, D_MODEL), lambda i, f: (i, 0)),
        scratch_shapes=[pltpu.VMEM((tm, D_MODEL), bf16), pltpu.VMEM((tm, D_MODEL), f32)],
        compiler_params=_params("arbitrary", "arbitrary"),
        name="mlp",
    )(x2d, mod3, norm_g, wu_bf16, wd_bf16)


def _layer(x, c, w_ada, b_ada, norm1_g, w_in, b_forget, fox_q_norm, fox_k_norm, nsa_q_norm,
           cmp_k_norm, slc_k_norm, win_k_norm, cmp_pe_k, cmp_w1_k, cmp_w2_k, cmp_pe_v, cmp_w1_v,
           cmp_w2_v, w_out, norm2_g, w_up, w_down):
    n_batch, seq_len, _ = x.shape
    n_cmp = (seq_len - CMP_LEN) // CMP_STRIDE + 1
    row = lambda v: v.reshape(1, -1)

    half = HEAD_DIM // 2
    inv_freq = ROPE_THETA ** (-jnp.arange(half, dtype=f32) / half)
    inv_freq = jnp.concatenate([inv_freq, inv_freq]).reshape(1, HEAD_DIM)

    w_a, w_b, w_s = _repack_w_in(w_in.T)

    mod3 = _ada(c, w_ada, b_ada).reshape(n_batch, 6, D_MODEL)
    x2d = x.reshape(n_batch * seq_len, D_MODEL)
    (fq, fk, fvt, cin, nqu, nqr, kc, vc, kaug, vst, kw, vw, gates) = _prep(
        x2d, mod3, row(norm1_g), w_a, w_b, w_s, n_batch, seq_len, row(fox_q_norm), row(fox_k_norm), row(nsa_q_norm), row(slc_k_norm),
        row(win_k_norm), jnp.pad(b_forget, (0, LANES - FOX_HEADS)).reshape(1, LANES), inv_freq)
    kcmp, vcmp = _compress(kc, vc, cmp_pe_k, cmp_pe_v, cmp_w1_k.astype(bf16), cmp_w1_v.astype(bf16),
                           cmp_w2_k.astype(bf16), cmp_w2_v.astype(bf16), row(cmp_k_norm))
    ofox = _fox(fq, fk, fvt, cin)
    ocmp, selneg = _cmp(nqu, kcmp, vcmp, gates, n_cmp)
    oslc, w_up16, w_down16, w_out16 = _slc(nqr, selneg, kaug, vst, gates, [w_up, w_down, w_out])
    owin = _win(nqr, kw, vw, gates)
    x1 = _outproj(ofox.reshape(-1, FOX_W), ocmp.reshape(-1, NSA_W), oslc.reshape(-1, NSA_W),
                  owin.reshape(-1, NSA_W), x2d, mod3, w_out16, seq_len)
    x2 = _mlp(x1, mod3, row(norm2_g), w_up16, w_down16, seq_len)
    return x2.reshape(n_batch, seq_len, D_MODEL)


def kernel(x, c, w_ada, b_ada, norm1_g, w_in, b_forget, fox_q_norm, fox_k_norm, nsa_q_norm, cmp_k_norm,
           slc_k_norm, win_k_norm, cmp_pe_k, cmp_w1_k, cmp_w2_k, cmp_pe_v, cmp_w1_v, cmp_w2_v, w_out,
           norm2_g, w_up, w_down):
    depth = w_ada.shape[0]
    for l in range(depth):
        x = _layer(x, c, w_ada[l], b_ada[l], norm1_g[l], w_in[l], b_forget[l], fox_q_norm[l], fox_k_norm[l],
                   nsa_q_norm[l], cmp_k_norm[l], slc_k_norm[l], win_k_norm[l], cmp_pe_k[l], cmp_w1_k[l],
                   cmp_w2_k[l], cmp_pe_v[l], cmp_w1_v[l], cmp_w2_v[l], w_out[l], norm2_g[l], w_up[l],
                   w_down[l])
    return x
```
